```python
import math
import jax
import jax.numpy as jnp
from jax import lax
import numpy as np

D_MODEL = 1024
BATCH = 2
SEQ = 8192
DEPTH = 2

GRID_W = 64
CTX_LEN = 256
EPS = 1e-6
HG_HEADS = 4
HG_DK = 128
HG_DV = 128
HG_W = HG_HEADS * HG_DK
HG_VW = HG_HEADS * HG_DV
HG_CHUNK = 64
NA_HEADS = 8
NA_HD = 64
NA_W = NA_HEADS * NA_HD
NA_WIN_R = 8
NA_WIN_C = 16
ROPE_THETA = 10000.0
HY_W = 512
HY_BANDS = 16
HY_PE_DIM = 1 + 2 * HY_BANDS
HY_FILT_HID = 64
HY_FAST_DECAY = 0.3
HY_SLOW_DECAY = 1.5
HY_TARGET = 1e-2
OFF_FF = 0
OFF_FB = OFF_FF + HG_W
OFF_I = OFF_FB + HG_W
OFF_NK = OFF_I + HG_VW
OFF_NV = OFF_NK + NA_W
OFF_HQ = OFF_NV + NA_W
OFF_NQ = OFF_HQ + HG_W
OFF_HG = OFF_NQ + NA_W
OFF_HY = OFF_HG + HG_VW
OFF_GATE = OFF_HY + 3 * HY_W
IN_COLS = OFF_GATE + 3 * D_MODEL
CTX_STATE_COLS = OFF_HQ
N_EXPERTS = 16
EC_CAP_FACTOR = 2
D_FF_EXPERT = 2816

kernel_name = 'hybrid_hgrn2_natten_hyena_ec_block'

f32 = jnp.float32


def _rms(x):
    xf = x.astype(f32)
    return (xf * lax.rsqrt(jnp.mean(xf * xf, axis=-1, keepdims=True) + EPS)).astype(x.dtype)


def rms_norm(x, g):
    return _rms(x) * g


def modulate(x, g, shift, scale):
    return rms_norm(x, g) * (1 + scale) + shift


def heads(t, h):
    return t.reshape(t.shape[0], t.shape[1], h, -1)


def rev(t):
    return jnp.flip(t, axis=1)


def log_forget(z, lower):
    return jnp.logaddexp(jnp.log(lower), jnp.log1p(-lower) + jax.nn.log_sigmoid(z.astype(f32)))


def gla_chunk_scan(logf, v, s0, q=None):
    B, N, H, _ = logf.shape
    n = N // HG_CHUNK

    def blocks(t):
        return t.reshape(B, n, HG_CHUNK, H, t.shape[-1]).transpose(1, 0, 3, 2, 4)

    incl = jnp.tril(jnp.ones((HG_CHUNK, HG_CHUNK), bool))[:, :, None]

    def step(S, blk):
        lf, vc = blk[0], blk[1]
        b = jnp.cumsum(lf, axis=2)
        kc = -jnp.expm1(lf)
        b_end = b[:, :, -1:, :]
        S_new = (jnp.exp(b_end[:, :, 0])[..., None] * S
                 + jnp.einsum('bhsk,bhsv->bhkv', kc * jnp.exp(b_end - b), vc))
        if q is None:
            return S_new, None
        qc = blk[2]
        o = jnp.einsum('bhtk,bhkv->bhtv', qc * jnp.exp(b), S)
        diff = jnp.where(incl, b[:, :, :, None, :] - b[:, :, None, :, :], -jnp.inf)
        att = jnp.einsum('bhtk,bhsk,bhtsk->bhts', qc, kc, jnp.exp(diff))
        o = o + jnp.einsum('bhts,bhsv->bhtv', att, vc)
        return S_new, o

    xs = (blocks(logf), blocks(v)) if q is None else (blocks(logf), blocks(v), blocks(q))
    S, o = lax.scan(step, s0, xs)
    if q is None:
        return None, S
    return o.transpose(1, 0, 3, 2, 4).reshape(B, N, H, v.shape[-1]), S


def hgrn2_readout(o, g, w):
    B, N = o.shape[:2]
    o = _rms(o).reshape(B, N, -1).astype(g.dtype)
    return (o * jax.nn.silu(g)) @ w


def _rope_1d(x, pos):
    n = x.shape[-1] // 2
    inv = ROPE_THETA ** (-jnp.arange(n, dtype=f32) / n)
    ang = pos.astype(f32)[:, None] * inv
    cos = jnp.cos(ang)[None, :, None, :].astype(x.dtype)
    sin = jnp.sin(ang)[None, :, None, :].astype(x.dtype)
    a, b = x[..., :n], x[..., n:]
    return jnp.concatenate([a * cos - b * sin, a * sin + b * cos], axis=-1)


def axial_rope(x, row_pos, col_pos):
    half = x.shape[-1] // 2
    return jnp.concatenate([_rope_1d(x[..., :half], row_pos), _rope_1d(x[..., half:], col_pos)], axis=-1)


def neighbourhood_attention(q, q_rot, k_rot, v, k_ctx, v_ctx, rpb):
    B, L, H, d = q.shape
    rows = L // GRID_W
    wr = min(NA_WIN_R, rows)
    scale = d ** -0.5

    def grid(t):
        return t.reshape(B, rows, GRID_W, H, d)

    qg, qrg, krg, vg = grid(q), grid(q_rot), grid(k_rot), grid(v)
    col = jnp.arange(GRID_W)
    col_idx = jnp.clip(col - NA_WIN_C // 2, 0, GRID_W - NA_WIN_C)[:, None] + jnp.arange(NA_WIN_C)
    dc = col_idx - col[:, None] + (NA_WIN_C - 1)

    def row_block(r):
        rs = jnp.clip(r - wr // 2, 0, rows - wr)
        q_r = lax.dynamic_index_in_dim(qrg, r, axis=1, keepdims=False)
        q_p = lax.dynamic_index_in_dim(qg, r, axis=1, keepdims=False)
        k_win = lax.dynamic_slice_in_dim(krg, rs, wr, axis=1)[:, :, col_idx]
        v_win = lax.dynamic_slice_in_dim(vg, rs, wr, axis=1)[:, :, col_idx]
        dr = rs + jnp.arange(wr) - r + (NA_WIN_R - 1)
        bias = rpb[:, dr[:, None, None], dc[None]].transpose(0, 2, 1, 3)
        s_win = jnp.einsum('bqhd,brqjhd->bhqrj', q_r, k_win).astype(f32) * scale + bias
        s_ctx = jnp.einsum('bqhd,bnhd->bhqn', q_p, k_ctx).astype(f32) * scale
        s = jnp.concatenate([s_win.reshape(B, H, GRID_W, wr * NA_WIN_C), s_ctx], axis=-1)
        p = jax.nn.softmax(s, axis=-1).astype(v.dtype)
        p_win = p[..., :wr * NA_WIN_C].reshape(B, H, GRID_W, wr, NA_WIN_C)
        return (jnp.einsum('bhqrj,brqjhd->bqhd', p_win, v_win)
                + jnp.einsum('bhqn,bnhd->bqhd', p[..., wr * NA_WIN_C:], v_ctx))

    o = lax.map(row_block, jnp.arange(rows))
    return o.transpose(1, 0, 2, 3, 4).reshape(B, L, H * d)


def context_attention(q, k, v):
    B, N, H, d = q.shape
    s = jnp.einsum('bqhd,bkhd->bhqk', q, k).astype(f32) * (d ** -0.5)
    p = jax.nn.softmax(s, axis=-1).astype(v.dtype)
    return jnp.einsum('bhqk,bkhd->bqhd', p, v).reshape(B, N, H * d)


def short_conv3(u, w, b):
    up = jnp.pad(u, ((0, 0), (1, 1), (0, 0)))
    return up[:, :-2] * w[0] + up[:, 1:-1] * w[1] + up[:, 2:] * w[2] + b


def implicit_filter(n, w1, b1, fr1, w2, b2, fr2, w3):
    t = jnp.linspace(0.0, 1.0, n, dtype=f32)[:, None]
    w = 2 * math.pi * jnp.arange(n, dtype=f32)[:, None] / n
    f = jnp.linspace(1e-4, HY_BANDS - 1, HY_BANDS, dtype=f32)[None]
    z = jnp.concatenate([t, jnp.cos(f * w), -jnp.sin(f * w)], axis=-1)
    h = jnp.sin(fr1.astype(f32) * (z @ w1.astype(f32) + b1.astype(f32)))
    h = jnp.sin(fr2.astype(f32) * (h @ w2.astype(f32) + b2.astype(f32)))
    h = h @ w3.astype(f32)
    deltas = jnp.abs(jnp.linspace(math.log(HY_TARGET) / HY_SLOW_DECAY, math.log(HY_TARGET) / HY_FAST_DECAY,
                                  2 * HY_W, dtype=f32))
    return h * jnp.exp(-t * deltas)


def bidir_fft_conv(u, h, skip):
    n, C = u.shape[1], u.shape[-1]
    hf, hb = h[:, :C], h[:, C:]
    k = jnp.concatenate([hf, jnp.zeros((1, C), f32), hb[:0:-1]], axis=0)
    k = k / jnp.sum(jnp.abs(k), axis=0, keepdims=True)
    uf = u.astype(f32)
    y = jnp.fft.irfft(jnp.fft.rfft(uf, n=2 * n, axis=1) * jnp.fft.rfft(k, axis=0)[None], n=2 * n, axis=1)[:, :n]
    return (y + uf * skip.astype(f32)).astype(u.dtype)


def hyena_operator(p, conv_w, conv_b, filt, skip):
    u = short_conv3(p, conv_w, conv_b)
    x0, x1, v = jnp.split(u, 3, axis=-1)
    h = implicit_filter(p.shape[1], *filt)
    return x0 * bidir_fft_conv(x1 * v, h, skip)


def gated_merge(graw, ya, yb, yc, w_out):
    ga, gb, gc = jnp.split(jax.nn.sigmoid(graw), 3, axis=-1)
    return (ga * ya + gb * yb + gc * yc) @ w_out


def expert_choice_ffn(h, w_router, w_gate, w_up, w_down):
    B, N, D = h.shape
    cap = EC_CAP_FACTOR * N // N_EXPERTS
    aff = jax.nn.softmax((h @ w_router).astype(f32), axis=-1)
    g, idx = lax.top_k(jnp.swapaxes(aff, 1, 2), cap)

    def per_set(hs, idx_s, g_s):
        xg = hs[idx_s]
        a = jnp.einsum('ecd,edf->ecf', xg, w_gate)
        u = jnp.einsum('ecd,edf->ecf', xg, w_up)
        y = jnp.einsum('ecf,efd->ecd', jax.nn.silu(a) * u, w_down) * g_s[..., None].astype(hs.dtype)
        return jnp.zeros_like(hs).at[idx_s.reshape(-1)].add(y.reshape(-1, D))

    return jax.vmap(per_set)(h, idx, g)


def mixing_sublayer(hx, hc, need_ctx, lb, w_in, q_gain, k_gain, rpb, conv_w, conv_b, filt, skip,
                    w_a, w_b, w_c, w_out):
    L = hx.shape[1]
    px = hx @ w_in
    pc = hc @ (w_in if need_ctx else w_in[:, :CTX_STATE_COLS])
    s0 = jnp.zeros((hc.shape[0], HG_HEADS, HG_DK, HG_DV), f32)

    lff_c = heads(log_forget(pc[..., OFF_FF:OFF_FB], lb[0]), HG_HEADS)
    lfb_c = heads(log_forget(pc[..., OFF_FB:OFF_I], lb[1]), HG_HEADS)
    i_c = heads(pc[..., OFF_I:OFF_NK].astype(f32), HG_HEADS)
    q_c = heads(jax.nn.silu(pc[..., OFF_HQ:OFF_NQ]).astype(f32), HG_HEADS) if need_ctx else None
    oc_f, s_f = gla_chunk_scan(lff_c, i_c, s0, q_c)
    oc_b, s_b = gla_chunk_scan(rev(lfb_c), rev(i_c), s0, None if q_c is None else rev(q_c))
    lff_x = heads(log_forget(px[..., OFF_FF:OFF_FB], lb[0]), HG_HEADS)
    lfb_x = heads(log_forget(px[..., OFF_FB:OFF_I], lb[1]), HG_HEADS)
    i_x = heads(px[..., OFF_I:OFF_NK].astype(f32), HG_HEADS)
    q_x = heads(jax.nn.silu(px[..., OFF_HQ:OFF_NQ]).astype(f32), HG_HEADS)
    ox_f, _ = gla_chunk_scan(lff_x, i_x, s_f, q_x)
    ox_b, _ = gla_chunk_scan(rev(lfb_x), rev(i_x), s_b, rev(q_x))
    ya_x = hgrn2_readout(ox_f + rev(ox_b), px[..., OFF_HG:OFF_HY], w_a)

    kc = rms_norm(heads(pc[..., OFF_NK:OFF_NV], NA_HEADS), k_gain)
    vc = heads(pc[..., OFF_NV:OFF_HQ], NA_HEADS)
    qn = rms_norm(heads(px[..., OFF_NQ:OFF_HG], NA_HEADS), q_gain)
    kn = rms_norm(heads(px[..., OFF_NK:OFF_NV], NA_HEADS), k_gain)
    vn = heads(px[..., OFF_NV:OFF_HQ], NA_HEADS)
    pos = jnp.arange(L)
    prow, pcol = pos // GRID_W, pos % GRID_W
    yb_x = neighbourhood_attention(qn, axial_rope(qn, prow, pcol), axial_rope(kn, prow, pcol), vn, kc, vc, rpb) @ w_b

    yc_x = hyena_operator(px[..., OFF_HY:OFF_GATE], conv_w, conv_b, filt, skip) @ w_c
    out_x = gated_merge(px[..., OFF_GATE:], ya_x, yb_x, yc_x, w_out)
    if not need_ctx:
        return out_x, None

    ya_c = hgrn2_readout(oc_f + rev(oc_b), pc[..., OFF_HG:OFF_HY], w_a)
    qnc = rms_norm(heads(pc[..., OFF_NQ:OFF_HG], NA_HEADS), q_gain)
    yb_c = context_attention(qnc, kc, vc) @ w_b
    yc_c = hyena_operator(pc[..., OFF_HY:OFF_GATE], conv_w, conv_b, filt, skip) @ w_c
    out_c = gated_merge(pc[..., OFF_GATE:], ya_c, yb_c, yc_c, w_out)
    return out_x, out_c


def setup_inputs(seed: int = 0) -> dict:
    key = jax.random.key(seed)
    ks = iter(jax.random.split(key, 40))

    def nrm(shape, scale):
        return jax.random.normal(next(ks), shape, f32) * scale

    D = D_MODEL
    return {
        'x': nrm((BATCH, SEQ, D), 1.0),
        'c': nrm((BATCH, D), 1.0),
        'ctx': nrm((BATCH, CTX_LEN, D), 1.0),
        'c_ctx': nrm((D,), 1.0),
        'w_mod': nrm((DEPTH, D, 6 * D), 0.5 * D ** -0.5),
        'b_mod': nrm((DEPTH, 6 * D), 0.01),
        'norm_mix': 1.0 + nrm((DEPTH, D), 0.02),
        'norm_ffn': 1.0 + nrm((DEPTH, D), 0.02),
        'w_in': nrm((DEPTH, D, IN_COLS), D ** -0.5),
        'hg_lb': nrm((DEPTH, 2, HG_W), 1.0),
        'na_q_gain': 1.0 + nrm((DEPTH, NA_HD), 0.02),
        'na_k_gain': 1.0 + nrm((DEPTH, NA_HD), 0.02),
        'na_rpb': nrm((DEPTH, NA_HEADS, 2 * NA_WIN_R - 1, 2 * NA_WIN_C - 1), 0.02),
        'hy_conv_w': nrm((DEPTH, 3, 3 * HY_W), 3 ** -0.5),
        'hy_conv_b': nrm((DEPTH, 3 * HY_W), 0.01),
        'hy_pe_w1': nrm((DEPTH, HY_PE_DIM, HY_FILT_HID), HY_PE_DIM ** -0.5),
        'hy_pe_b1': nrm((DEPTH, HY_FILT_HID), 0.01),
        'hy_pe_freq1': 1.0 + nrm((DEPTH, HY_FILT_HID), 0.01),
        'hy_pe_w2': nrm((DEPTH, HY_FILT_HID, HY_FILT_HID), HY_FILT_HID ** -0.5),
        'hy_pe_b2': nrm((DEPTH, HY_FILT_HID), 0.01),
        'hy_pe_freq2': 1.0 + nrm((DEPTH, HY_FILT_HID), 0.01),
        'hy_pe_w3': nrm((DEPTH, HY_FILT_HID, 2 * HY_W), HY_FILT_HID ** -0.5),
        'hy_skip': nrm((DEPTH, HY_W), 0.5),
        'w_branch_a': nrm((DEPTH, HG_VW, D), HG_VW ** -0.5),
        'w_branch_b': nrm((DEPTH, NA_W, D), NA_W ** -0.5),
        'w_branch_c': nrm((DEPTH, HY_W, D), HY_W ** -0.5),
        'w_out': nrm((DEPTH, D, D), D ** -0.5),
        'w_router': nrm((DEPTH, D, N_EXPERTS), D ** -0.5),
        'w_e_gate': nrm((DEPTH, N_EXPERTS, D, D_FF_EXPERT), D ** -0.5),
        'w_e_up': nrm((DEPTH, N_EXPERTS, D, D_FF_EXPERT), D ** -0.5),
        'w_e_down': nrm((DEPTH, N_EXPERTS, D_FF_EXPERT, D), D_FF_EXPERT ** -0.5),
    }


def reference(x, c, ctx, c_ctx, w_mod, b_mod, norm_mix, norm_ffn, w_in, hg_lb, na_q_gain, na_k_gain, na_rpb,
              hy_conv_w, hy_conv_b, hy_pe_w1, hy_pe_b1, hy_pe_freq1, hy_pe_w2, hy_pe_b2, hy_pe_freq2, hy_pe_w3,
              hy_skip, w_branch_a, w_branch_b, w_branch_c, w_out, w_router, w_e_gate, w_e_up, w_e_down):
    lb_all = jnp.cumsum(jax.nn.softmax(hg_lb.astype(f32), axis=0), axis=0)
    lb_all = lb_all - lb_all[:1]
    sc = jax.nn.silu(c)
    scc = jax.nn.silu(c_ctx)
    for l in range(DEPTH):
        need_ctx = l < DEPTH - 1
        mx = jnp.split((sc @ w_mod[l] + b_mod[l])[:, None, :], 6, axis=-1)
        mc = jnp.split(scc @ w_mod[l] + b_mod[l], 6, axis=-1)
        hx = modulate(x, norm_mix[l], mx[0], mx[1])
        hc = modulate(ctx, norm_mix[l], mc[0], mc[1])
        filt = (hy_pe_w1[l], hy_pe_b1[l], hy_pe_freq1[l], hy_pe_w2[l], hy_pe_b2[l], hy_pe_freq2[l], hy_pe_w3[l])
        yx, yc = mixing_sublayer(hx, hc, need_ctx, lb_all[l], w_in[l], na_q_gain[l], na_k_gain[l], na_rpb[l],
                                 hy_conv_w[l], hy_conv_b[l], filt, hy_skip[l],
                                 w_branch_a[l], w_branch_b[l], w_branch_c[l], w_out[l])
        x = x + mx[2] * yx
        hx = modulate(x, norm_ffn[l], mx[3], mx[4])
        x = x + mx[5] * expert_choice_ffn(hx, w_router[l], w_e_gate[l], w_e_up[l], w_e_down[l])
        if need_ctx:
            ctx = ctx + mc[2] * yc
            hc = modulate(ctx, norm_ffn[l], mc[3], mc[4])
            ctx = ctx + mc[5] * expert_choice_ffn(hc, w_router[l], w_e_gate[l], w_e_up[l], w_e_down[l])
    return x
```

```python
import functools
import math

import numpy as np
import jax
import jax.numpy as jnp
from jax import lax
from jax.experimental import pallas as pl
from jax.experimental.pallas import tpu as pltpu

f32 = jnp.float32
bf16 = jnp.bfloat16
i32 = jnp.int32
HI = lax.Precision.HIGHEST

D_MODEL = 1024
DEPTH = 2
GRID_W = 64
EPS = 1e-6
HG_HEADS = 4
HG_W = 512
HG_CHUNK = 64
NA_HEADS = 8
NA_HD = 64
NA_W = 512
NA_WIN_R = 8
NA_WIN_C = 16
ROPE_THETA = 10000.0
HY_W = 512
HY_BANDS = 16
HY_PE_DIM = 1 + 2 * HY_BANDS
HY_FILT_HID = 64
HY_FAST_DECAY = 0.3
HY_SLOW_DECAY = 1.5
HY_TARGET = 1e-2
OFF_FF = 0
OFF_FB = 512
OFF_I = 1024
OFF_NK = 1536
OFF_NV = 2048
OFF_HQ = 2560
OFF_NQ = 3072
OFF_HG = 3584
OFF_HY = 4096
OFF_GATE = 5632
IN_COLS = 8704
N_EXPERTS = 16
EC_CAP_FACTOR = 2
D_FF_EXPERT = 2816

LANES = 128
NEG_BIG = -1e30
VMEM_LIMIT = 56 * 1024 * 1024


def _cparams(sem, vmem=VMEM_LIMIT):
    return pltpu.CompilerParams(dimension_semantics=sem, vmem_limit_bytes=vmem)


def _nt(a, b, precision=None):
    return lax.dot_general(a, b, (((1,), (1,)), ((), ())), precision=precision, preferred_element_type=f32)


def _tn(a, b, precision=None):
    return lax.dot_general(a, b, (((0,), (0,)), ((), ())), precision=precision, preferred_element_type=f32)


def _silu(x):
    return x * jax.nn.sigmoid(x)


def _modvec_kernel(s_ref, w_ref, b_ref, o_ref):
    s = _silu(s_ref[...])
    o_ref[...] = jnp.dot(s, w_ref[...], precision=HI, preferred_element_type=f32) + b_ref[...]


def modvec(s8, w, b):
    n = w.shape[1]
    tn = 1024
    return pl.pallas_call(
        _modvec_kernel,
        grid=(n // tn,),
        in_specs=[pl.BlockSpec((8, D_MODEL), lambda j: (0, 0)),
                  pl.BlockSpec((D_MODEL, tn), lambda j: (0, j)),
                  pl.BlockSpec((1, tn), lambda j: (0, j))],
        out_specs=pl.BlockSpec((8, tn), lambda j: (0, j)),
        out_shape=jax.ShapeDtypeStruct((8, n), f32),
        compiler_params=_cparams(("parallel",)),
        name="modvec",
    )(s8, w, b.reshape(1, n))


def _modulate_kernel(x_ref, g_ref, sh_ref, sc_ref, o_ref):
    x = x_ref[...]
    ms = jnp.mean(x * x, axis=-1, keepdims=True)
    y = x * lax.rsqrt(ms + EPS)
    o_ref[...] = (y * g_ref[...] * (1.0 + sc_ref[0]) + sh_ref[0]).astype(o_ref.dtype)


def modulate(x2d, g, shift, scale, rows_per_group, out_dtype):
    R = x2d.shape[0]
    tm = 512
    tpg = rows_per_group // tm
    G = shift.shape[0]
    return pl.pallas_call(
        _modulate_kernel,
        grid=(R // tm,),
        in_specs=[pl.BlockSpec((tm, D_MODEL), lambda i: (i, 0)),
                  pl.BlockSpec((1, D_MODEL), lambda i: (0, 0)),
                  pl.BlockSpec((1, 1, D_MODEL), lambda i: (i // tpg, 0, 0)),
                  pl.BlockSpec((1, 1, D_MODEL), lambda i: (i // tpg, 0, 0))],
        out_specs=pl.BlockSpec((tm, D_MODEL), lambda i: (i, 0)),
        out_shape=jax.ShapeDtypeStruct((R, D_MODEL), out_dtype),
        compiler_params=_cparams(("parallel",)),
        name="modulate",
    )(x2d, g.reshape(1, D_MODEL), shift.reshape(G, 1, D_MODEL), scale.reshape(G, 1, D_MODEL))


def _log_sigmoid(z):
    return jnp.minimum(z, 0.0) - jnp.log1p(jnp.exp(-jnp.abs(z)))


def _epi_raw(acc, o_ref):
    o_ref[...] = acc.astype(o_ref.dtype)


def _epi_silu(acc, o_ref):
    o_ref[...] = _silu(acc).astype(o_ref.dtype)


def _epi_sigmoid(acc, o_ref):
    o_ref[...] = jax.nn.sigmoid(acc).astype(o_ref.dtype)


def _epi_logforget(acc, la_ref, lc_ref, o_ref):
    la = la_ref[...]
    c = lc_ref[...] + _log_sigmoid(acc)
    o_ref[...] = jnp.maximum(la, c) + jnp.log1p(jnp.exp(-jnp.abs(la - c)))


def _head_rms(acc, gain_ref, bd_ref):
    sq = acc * acc
    hi = sq.astype(bf16)
    lo = (sq - hi.astype(f32)).astype(bf16)
    ms = jnp.dot(hi, bd_ref[...], preferred_element_type=f32) + jnp.dot(lo, bd_ref[...], preferred_element_type=f32)
    return acc * lax.rsqrt(ms + EPS) * gain_ref[...]


def _rope(y, c_ref, s1_ref, s2_ref):
    reps = y.shape[1] // LANES
    c = jnp.concatenate([c_ref[...]] * reps, axis=1)
    s1 = jnp.concatenate([s1_ref[...]] * reps, axis=1)
    s2 = jnp.concatenate([s2_ref[...]] * reps, axis=1)
    w = y.shape[1]
    return y * c + pltpu.roll(y, w - 16, axis=1) * s1 + pltpu.roll(y, 16, axis=1) * s2


def _epi_norm(acc, gain_ref, bd_ref, o_ref):
    o_ref[...] = _head_rms(acc, gain_ref, bd_ref).astype(o_ref.dtype)


def _epi_norm_rope(acc, gain_ref, bd_ref, c_ref, s1_ref, s2_ref, o_ref):
    y = _head_rms(acc, gain_ref, bd_ref)
    o_ref[...] = _rope(y, c_ref, s1_ref, s2_ref).astype(o_ref.dtype)


def _epi_norm_both(acc, gain_ref, bd_ref, c_ref, s1_ref, s2_ref, on_ref, or_ref):
    y = _head_rms(acc, gain_ref, bd_ref)
    on_ref[...] = y.astype(on_ref.dtype)
    or_ref[...] = _rope(y, c_ref, s1_ref, s2_ref).astype(or_ref.dtype)


def _proj_kernel(h_ref, w_ref, *rest, epi):
    acc = jnp.dot(h_ref[...], w_ref[...], preferred_element_type=f32)
    epi(acc, *rest)


def project(h, w, c0, width, epi, aux=(), out_dtypes=(bf16,), tm=1024, rows_per_seq=None):
    R = h.shape[0]
    tn = 512
    nj = width // tn
    cb = c0 // tn
    in_specs = [pl.BlockSpec((tm, D_MODEL), lambda i, j: (i, 0)),
                pl.BlockSpec((D_MODEL, tn), lambda i, j: (0, cb + j))]
    args = [h, w]
    for kind, arr in aux:
        if kind == "col":
            in_specs.append(pl.BlockSpec((1, tn), lambda i, j: (0, j)))
        elif kind == "const":
            in_specs.append(pl.BlockSpec(arr.shape, lambda i, j: (0, 0)))
        else:
            tps = rows_per_seq // tm
            in_specs.append(pl.BlockSpec((tm, LANES), lambda i, j: (i % tps, 0)))
        args.append(arr)
    out_specs = [pl.BlockSpec((tm, tn), lambda i, j: (i, j)) for _ in out_dtypes]
    out_shape = [jax.ShapeDtypeStruct((R, width), dt) for dt in out_dtypes]
    res = pl.pallas_call(
        functools.partial(_proj_kernel, epi=epi),
        grid=(R // tm, nj),
        in_specs=in_specs,
        out_specs=out_specs,
        out_shape=out_shape,
        compiler_params=_cparams(("parallel", "parallel")),
        name="proj_" + epi.__name__[5:],
    )(*args)
    return res[0] if len(res) == 1 else res


def _hgrn_tmatrix(C, reverse):
    L = int(round(math.log2(C)))
    t = np.arange(C)
    tau = (C - 1 - t) if reverse else t
    tt, uu = tau[:, None], tau[None, :]
    T = np.zeros((2 + L, C, C), np.float32)
    T[0] = uu <= tt
    T[1] = uu > tt
    for l in range(L):
        same = (tt >> (l + 1)) == (uu >> (l + 1))
        tr = ((tt >> l) & 1) == 1
        ur = ((uu >> l) & 1) == 1
        T[2 + l] = same & ((tr & ur & (uu <= tt)) | (~tr & ~ur & (uu > tt)))
    return T.reshape((2 + L) * C, C)


def _hgrn_kernel(*refs, C, reverse, fuse):
    if fuse:
        lf_ref, v_ref, q_ref, s0_ref, t_ref, of_ref, g_ref, o_ref, sfin_ref, s_scr = refs
    else:
        lf_ref, v_ref, q_ref, s0_ref, t_ref, o_ref, sfin_ref, s_scr = refs
    L = int(round(math.log2(C)))
    c = pl.program_id(1)

    @pl.when(c == 0)
    def _():
        s_scr[...] = s0_ref[0]

    lf = lf_ref[0]
    hi = lf.astype(bf16)
    lo = (lf - hi.astype(f32)).astype(bf16)
    tm = t_ref[...]
    E = jnp.dot(tm, hi, preferred_element_type=f32) + jnp.dot(tm, lo, preferred_element_type=f32)
    kc_all = 1.0 - jnp.exp(lf)
    q_all = q_ref[0].astype(f32)
    v_all = v_ref[0]

    row = lax.broadcasted_iota(i32, (C, LANES), 0)
    tau = (C - 1 - row) if reverse else row
    ti = lax.broadcasted_iota(i32, (C, C), 0)
    si = lax.broadcasted_iota(i32, (C, C), 1)
    if reverse:
        ti, si = C - 1 - ti, C - 1 - si
    end_row = 0 if reverse else C - 1

    outs = []
    for h in range(HG_HEADS):
        sl = slice(LANES * h, LANES * (h + 1))
        qh, kch, vh = q_all[:, sl], kc_all[:, sl], v_all[:, sl]
        b = E[0:C, sl]
        suf = E[C:2 * C, sl]
        st = s_scr[h]
        qb = (qh * jnp.exp(b)).astype(bf16)
        o = _nt(qb, st.astype(bf16))
        att = jnp.where(ti == si, _nt(qh.astype(bf16), kch.astype(bf16)), 0.0)
        for l in range(L):
            x = jnp.exp(E[(2 + l) * C:(3 + l) * C, sl])
            later = ((tau >> l) & 1) == 1
            ql = jnp.where(later, qh * x, 0.0).astype(bf16)
            kl = jnp.where(later, 0.0, kch * x).astype(bf16)
            att = att + jnp.where((ti >> (l + 1)) == (si >> (l + 1)), _nt(ql, kl), 0.0)
        o = o + jnp.dot(att.astype(bf16), vh, preferred_element_type=f32)
        kd = (kch * jnp.exp(suf)).astype(bf16)
        bend = b[end_row:end_row + 1, :]
        s_scr[h] = st * jnp.exp(bend) + _tn(vh, kd)
        outs.append(o)

    if fuse:
        of = of_ref[0]
        g = g_ref[0].astype(f32)
        res = []
        for h in range(HG_HEADS):
            sl = slice(LANES * h, LANES * (h + 1))
            tot = of[:, sl] + outs[h]
            ms = jnp.mean(tot * tot, axis=-1, keepdims=True)
            res.append(tot * lax.rsqrt(ms + EPS) * g[:, sl])
        o_ref[0] = jnp.concatenate(res, axis=1).astype(o_ref.dtype)
    else:
        o_ref[0] = jnp.concatenate(outs, axis=1).astype(o_ref.dtype)

    @pl.when(c == pl.num_programs(1) - 1)
    def _():
        sfin_ref[0] = s_scr[...]


def hgrn_scan(lf, v, q, s0, reverse, o_fwd=None, gate=None):
    B, N, W = lf.shape
    C = HG_CHUNK
    nch = N // C
    fuse = o_fwd is not None
    tmat = jnp.asarray(_hgrn_tmatrix(C, reverse), bf16)
    cmap = (lambda b, c: (b, nch - 1 - c, 0)) if reverse else (lambda b, c: (b, c, 0))
    seq = pl.BlockSpec((1, C, W), cmap)
    in_specs = [seq, seq, seq,
                pl.BlockSpec((1, HG_HEADS, LANES, LANES), lambda b, c: (b, 0, 0, 0)),
                pl.BlockSpec(tmat.shape, lambda b, c: (0, 0))]
    args = [lf, v, q, s0, tmat]
    if fuse:
        in_specs += [seq, seq]
        args += [o_fwd, gate]
    o, sfin = pl.pallas_call(
        functools.partial(_hgrn_kernel, C=C, reverse=reverse, fuse=fuse),
        grid=(B, nch),
        in_specs=in_specs,
        out_specs=[seq, pl.BlockSpec((1, HG_HEADS, LANES, LANES), lambda b, c: (b, 0, 0, 0))],
        out_shape=[jax.ShapeDtypeStruct((B, N, W), bf16 if fuse else f32),
                   jax.ShapeDtypeStruct((B, HG_HEADS, LANES, LANES), f32)],
        scratch_shapes=[pltpu.VMEM((HG_HEADS, LANES, LANES), f32)],
        compiler_params=_cparams(("parallel", "arbitrary")),
        name="hgrn_bwd" if reverse else "hgrn_fwd",
    )(*args)
    return o, sfin


def _na_kernel(qr_ref, qn_ref, k_ref, v_ref, kc_ref, vc_ref, tab_ref, o_ref, *, rows_per_step, n_rows):
    g = pl.program_id(2)
    scale = NA_HD ** -0.5
    lane = lax.broadcasted_iota(i32, (GRID_W, LANES), 1)
    kcx = kc_ref[0]
    vcx = vc_ref[0]
    win = NA_WIN_R * GRID_W
    for i in range(rows_per_step):
        r = g * rows_per_step + i
        rs = jnp.clip(r - NA_WIN_R // 2, 0, n_rows - NA_WIN_R)
        off = rs - r + (NA_WIN_R - 1)
        start = pl.multiple_of(rs * GRID_W, GRID_W)
        kw = k_ref[0, pl.ds(start, win), :]
        vw = v_ref[0, pl.ds(start, win), :]
        qr = qr_ref[0, i * GRID_W:(i + 1) * GRID_W, :]
        qn = qn_ref[0, i * GRID_W:(i + 1) * GRID_W, :]
        res = []
        for hh in range(2):
            m = (lane >= NA_HD * hh) & (lane < NA_HD * (hh + 1))
            qrm = jnp.where(m, qr, jnp.zeros_like(qr))
            qnm = jnp.where(m, qn, jnp.zeros_like(qn))
            s = _nt(qrm, kw) * scale + tab_ref[hh, off]
            sc = _nt(qnm, kcx) * scale
            mx = jnp.maximum(jnp.max(s, axis=-1, keepdims=True), jnp.max(sc, axis=-1, keepdims=True))
            p = jnp.exp(s - mx)
            pc = jnp.exp(sc - mx)
            inv = 1.0 / (jnp.sum(p, axis=-1, keepdims=True) + jnp.sum(pc, axis=-1, keepdims=True))
            res.append(jnp.dot((p * inv).astype(bf16), vw, preferred_element_type=f32)
                       + jnp.dot((pc * inv).astype(bf16), vcx, preferred_element_type=f32))
        o_ref[0, i * GRID_W:(i + 1) * GRID_W, :] = jnp.where(lane < NA_HD, res[0], res[1]).astype(o_ref.dtype)


def _na_bias_table(rpb):
    col = jnp.arange(GRID_W)
    cs = jnp.clip(col - NA_WIN_C // 2, 0, GRID_W - NA_WIN_C)
    kc = jnp.arange(GRID_W)
    valid = (kc[None, :] >= cs[:, None]) & (kc[None, :] < cs[:, None] + NA_WIN_C)
    dc = jnp.clip(kc[None, :] - col[:, None] + (NA_WIN_C - 1), 0, 2 * NA_WIN_C - 2)
    bc = jnp.where(valid[None, None], rpb[:, :, dc], NEG_BIG)
    t2 = jnp.stack([bc[:, o:o + NA_WIN_R] for o in range(NA_WIN_R)], axis=1)
    t2 = t2.transpose(0, 1, 3, 2, 4)
    return t2.reshape(NA_HEADS, NA_WIN_R, GRID_W, NA_WIN_R * GRID_W).astype(f32)


def neighbourhood_attention(q_rot, qn, k_rot, v, kc, vc, table):
    B, N, W = q_rot.shape
    n_rows = N // GRID_W
    rps = 8
    ctx_len = kc.shape[1]
    pairs = W // LANES
    return pl.pallas_call(
        functools.partial(_na_kernel, rows_per_step=rps, n_rows=n_rows),
        grid=(B, pairs, n_rows // rps),
        in_specs=[pl.BlockSpec((1, rps * GRID_W, LANES), lambda b, p, g: (b, g, p)),
                  pl.BlockSpec((1, rps * GRID_W, LANES), lambda b, p, g: (b, g, p)),
                  pl.BlockSpec((1, N, LANES), lambda b, p, g: (b, 0, p)),
                  pl.BlockSpec((1, N, LANES), lambda b, p, g: (b, 0, p)),
                  pl.BlockSpec((1, ctx_len, LANES), lambda b, p, g: (b, 0, p)),
                  pl.BlockSpec((1, ctx_len, LANES), lambda b, p, g: (b, 0, p)),
                  pl.BlockSpec((2, NA_WIN_R, GRID_W, NA_WIN_R * GRID_W), lambda b, p, g: (p, 0, 0, 0))],
        out_specs=pl.BlockSpec((1, rps * GRID_W, LANES), lambda b, p, g: (b, g, p)),
        out_shape=jax.ShapeDtypeStruct((B, N, W), bf16),
        compiler_params=_cparams(("parallel", "parallel", "arbitrary")),
        name="natten",
    )(q_rot, qn, k_rot, v, kc, vc, table)


def _ctx_attn_kernel(q_ref, k_ref, v_ref, o_ref):
    scale = NA_HD ** -0.5
    q = q_ref[0]
    k = k_ref[0]
    v = v_ref[0]
    lane = lax.broadcasted_iota(i32, q.shape, 1)
    res = []
    for hh in range(2):
        m = (lane >= NA_HD * hh) & (lane < NA_HD * (hh + 1))
        s = _nt(jnp.where(m, q, jnp.zeros_like(q)), k) * scale
        p = jnp.exp(s - jnp.max(s, axis=-1, keepdims=True))
        p = p / jnp.sum(p, axis=-1, keepdims=True)
        res.append(jnp.dot(p.astype(bf16), v, preferred_element_type=f32))
    o_ref[0] = jnp.where(lane < NA_HD, res[0], res[1]).astype(o_ref.dtype)


def context_attention(q, k, v):
    B, N, W = q.shape
    spec = pl.BlockSpec((1, N, LANES), lambda b, p: (b, 0, p))
    return pl.pallas_call(
        _ctx_attn_kernel,
        grid=(B, W // LANES),
        in_specs=[spec, spec, spec],
        out_specs=spec,
        out_shape=jax.ShapeDtypeStruct((B, N, W), bf16),
        compiler_params=_cparams(("parallel", "parallel")),
        name="ctx_attn",
    )(q, k, v)


def _filter_kernel(z_ref, w1_ref, b1_ref, f1_ref, w2_ref, b2_ref, f2_ref, w3_ref, dl_ref, h_ref, nrm_ref, *, tm):
    i = pl.program_id(0)
    z = z_ref[...]
    a = jnp.sin(f1_ref[...] * (jnp.dot(z, w1_ref[...], precision=HI, preferred_element_type=f32) + b1_ref[...]))
    a = jnp.sin(f2_ref[...] * (jnp.dot(a, w2_ref[...], precision=HI, preferred_element_type=f32) + b2_ref[...]))
    h = jnp.dot(a, w3_ref[...], precision=HI, preferred_element_type=f32)
    h = h * jnp.exp(-z[:, 0:1] * dl_ref[...])
    row = lax.broadcasted_iota(i32, h.shape, 0) + i * tm
    col = lax.broadcasted_iota(i32, h.shape, 1)
    h = jnp.where((row == 0) & (col >= HY_W), 0.0, h)
    h_ref[...] = h
    s = jnp.sum(jnp.abs(h), axis=0, keepdims=True)
    part = s[:, :HY_W] + s[:, HY_W:]

    @pl.when(i == 0)
    def _():
        nrm_ref[...] = part

    @pl.when(i > 0)
    def _():
        nrm_ref[...] = nrm_ref[...] + part


def hyena_filter(n, w1, b1, fr1, w2, b2, fr2, w3):
    t = jnp.linspace(0.0, 1.0, n, dtype=f32)[:, None]
    w = 2 * math.pi * jnp.arange(n, dtype=f32)[:, None] / n
    fb = jnp.linspace(1e-4, HY_BANDS - 1, HY_BANDS, dtype=f32)[None]
    z = jnp.concatenate([t, jnp.cos(fb * w), -jnp.sin(fb * w)], axis=-1)
    z = jnp.pad(z, ((0, 0), (0, LANES - HY_PE_DIM)))
    w1p = jnp.pad(w1.astype(f32), ((0, LANES - HY_PE_DIM), (0, 0)))
    deltas = jnp.abs(jnp.linspace(math.log(HY_TARGET) / HY_SLOW_DECAY, math.log(HY_TARGET) / HY_FAST_DECAY,
                                  2 * HY_W, dtype=f32))[None]
    tm = min(n, 512)
    hid = HY_FILT_HID
    full = lambda shape: pl.BlockSpec(shape, lambda i: (0, 0))
    return pl.pallas_call(
        functools.partial(_filter_kernel, tm=tm),
        grid=(n // tm,),
        in_specs=[pl.BlockSpec((tm, LANES), lambda i: (i, 0)),
                  full((LANES, hid)), full((1, hid)), full((1, hid)),
                  full((hid, hid)), full((1, hid)), full((1, hid)),
                  full((hid, 2 * HY_W)), full((1, 2 * HY_W))],
        out_specs=[pl.BlockSpec((tm, 2 * HY_W), lambda i: (i, 0)), pl.BlockSpec((1, HY_W), lambda i: (0, 0))],
        out_shape=[jax.ShapeDtypeStruct((n, 2 * HY_W), f32), jax.ShapeDtypeStruct((1, HY_W), f32)],
        compiler_params=_cparams(("arbitrary",)),
        name="hyena_filter",
    )(z, w1p, b1.reshape(1, hid), fr1.reshape(1, hid), w2.astype(f32), b2.reshape(1, hid), fr2.reshape(1, hid),
      w3.astype(f32), deltas)


def _conv3(u, w_ref, b_ref):
    n = u.shape[0]
    row = lax.broadcasted_iota(i32, u.shape, 0)
    prev = jnp.where(row == 0, 0.0, pltpu.roll(u, 1, axis=0))
    nxt = jnp.where(row == n - 1, 0.0, pltpu.roll(u, n - 1, axis=0))
    return prev * w_ref[0:1, :] + u * w_ref[1:2, :] + nxt * w_ref[2:3, :] + b_ref[...]


def _hyena_pre_kernel(p0_ref, p1_ref, p2_ref, w0_ref, w1_ref, w2_ref, b0_ref, b1_ref, b2_ref, z_ref, x0_ref):
    x0_ref[0] = _conv3(p0_ref[0], w0_ref, b0_ref).astype(x0_ref.dtype)
    z_ref[0] = _conv3(p1_ref[0], w1_ref, b1_ref) * _conv3(p2_ref[0], w2_ref, b2_ref)


def hyena_pre(p, conv_w, conv_b):
    B, N, _ = p.shape
    nb = HY_W // LANES
    conv_b = conv_b.reshape(1, 3 * HY_W)
    pspec = lambda g: pl.BlockSpec((1, N, LANES), lambda b, c: (b, 0, g * nb + c))
    wspec = lambda g: pl.BlockSpec((3, LANES), lambda b, c: (0, g * nb + c))
    bspec = lambda g: pl.BlockSpec((1, LANES), lambda b, c: (0, g * nb + c))
    ospec = pl.BlockSpec((1, N, LANES), lambda b, c: (b, 0, c))
    return pl.pallas_call(
        _hyena_pre_kernel,
        grid=(B, nb),
        in_specs=[pspec(0), pspec(1), pspec(2), wspec(0), wspec(1), wspec(2), bspec(0), bspec(1), bspec(2)],
        out_specs=[ospec, ospec],
        out_shape=[jax.ShapeDtypeStruct((B, N, HY_W), f32), jax.ShapeDtypeStruct((B, N, HY_W), bf16)],
        compiler_params=_cparams(("parallel", "parallel")),
        name="hyena_pre",
    )(p, p, p, conv_w, conv_w, conv_w, conv_b, conv_b, conv_b)


def _dft_consts(n):
    N = 2 * n
    na = N // LANES
    t1n = na // 2
    k1n = na // 2 + 1
    k1p = -(-k1n // 8) * 8
    k1 = np.arange(k1n)
    t1 = np.arange(t1n)
    th = 2 * np.pi * ((t1[None, :] * k1[:, None]) % na) / na
    f1c = np.zeros((2 * k1p, t1n))
    f1c[:k1n] = np.cos(th)
    f1c[k1p:k1p + k1n] = -np.sin(th)
    k2 = np.arange(LANES)
    t2 = np.arange(LANES)
    m = (t2[None, None, :] * (k1[:, None, None] + na * k2[None, :, None])) % N
    ph = 2 * np.pi * m / N
    g = np.concatenate([np.cos(ph), -np.sin(ph)], axis=1)
    pht = ph.transpose(0, 2, 1)
    gi = np.concatenate([np.cos(pht), np.sin(pht)], axis=1)
    wk = np.where((k1 == 0) | (k1 == na // 2), 1.0, 2.0) / N
    f1i = np.zeros((t1n, 2 * k1p))
    f1i[:, :k1n] = np.cos(th.T) * wk[None, :]
    f1i[:, k1p:k1p + k1n] = -np.sin(th.T) * wk[None, :]
    as32 = lambda a: jnp.asarray(a.astype(np.float32))
    return dict(na=na, t1n=t1n, k1n=k1n, k1p=k1p, f1c=as32(f1c), g=as32(g), gi=as32(gi), f1i=as32(f1i))


def _dft_stage1(src_ref, f1c_ref, are_ref, aim_ref, t1n, k1p):
    f1c = f1c_ref[...]

    def body(t2, carry):
        zs = src_ref[pl.ds(t2, t1n, stride=LANES), :]
        r = jnp.dot(f1c, zs, precision=HI, preferred_element_type=f32)
        are_ref[pl.ds(t2, k1p, stride=LANES), :] = r[:k1p]
        aim_ref[pl.ds(t2, k1p, stride=LANES), :] = r[k1p:]
        return carry

    lax.fori_loop(0, LANES, body, 0)


def _cplx_left(gc, xre, xim):
    p = jnp.dot(gc, xre, precision=HI, preferred_element_type=f32)
    q = jnp.dot(gc, xim, precision=HI, preferred_element_type=f32)
    return p[:LANES] - q[LANES:], p[LANES:] + q[:LANES]


def _spectrum_kernel(h_ref, f1c_ref, g_ref, xre_ref, xim_ref, are, aim, *, t1n, k1p):
    k1 = pl.program_id(1)

    @pl.when(k1 == 0)
    def _():
        _dft_stage1(h_ref, f1c_ref, are, aim, t1n, k1p)

    r0 = pl.multiple_of(k1 * LANES, LANES)
    xre, xim = _cplx_left(g_ref[0], are[pl.ds(r0, LANES), :], aim[pl.ds(r0, LANES), :])
    xre_ref[...] = xre
    xim_ref[...] = xim


def hyena_spectrum(h, dc):
    n, C = h.shape
    k1n, k1p, t1n = dc["k1n"], dc["k1p"], dc["t1n"]
    out = jax.ShapeDtypeStruct((k1n * LANES, C), f32)
    ospec = pl.BlockSpec((LANES, LANES), lambda c, k: (k, c))
    return pl.pallas_call(
        functools.partial(_spectrum_kernel, t1n=t1n, k1p=k1p),
        grid=(C // LANES, k1n),
        in_specs=[pl.BlockSpec((n, LANES), lambda c, k: (0, c)),
                  pl.BlockSpec(dc["f1c"].shape, lambda c, k: (0, 0)),
                  pl.BlockSpec((1, 2 * LANES, LANES), lambda c, k: (k, 0, 0))],
        out_specs=[ospec, ospec],
        out_shape=[out, out],
        scratch_shapes=[pltpu.VMEM((k1p * LANES, LANES), f32), pltpu.VMEM((k1p * LANES, LANES), f32)],
        compiler_params=_cparams(("parallel", "arbitrary")),
        name="hyena_spectrum",
    )(h, dc["f1c"], dc["g"])


def _hyena_conv_kernel(z_ref, x0_ref, f1c_ref, g_ref, gi_ref, f1i_ref, fre_ref, fim_ref, gre_ref, gim_ref,
                       inv_ref, skip_ref, o_ref, are, aim, y_scr, *, t1n, k1p):
    k1 = pl.program_id(2)

    @pl.when(k1 == 0)
    def _():
        _dft_stage1(z_ref.at[0], f1c_ref, are, aim, t1n, k1p)

    r0 = pl.multiple_of(k1 * LANES, LANES)
    xre, xim = _cplx_left(g_ref[0], are[pl.ds(r0, LANES), :], aim[pl.ds(r0, LANES), :])
    inv = inv_ref[...]
    kre = (fre_ref[...] + gre_ref[...]) * inv
    kim = (fim_ref[...] - gim_ref[...]) * inv
    yre = xre * kre - xim * kim
    yim = xre * kim + xim * kre
    bre, bim = _cplx_left(gi_ref[0], yre, yim)
    are[pl.ds(r0, LANES), :] = bre
    aim[pl.ds(r0, LANES), :] = bim

    @pl.when(k1 == pl.num_programs(2) - 1)
    def _():
        f1i = f1i_ref[...]

        def body(t2, carry):
            bb = jnp.concatenate([are[pl.ds(t2, k1p, stride=LANES), :], aim[pl.ds(t2, k1p, stride=LANES), :]], axis=0)
            y_scr[pl.ds(t2, t1n, stride=LANES), :] = jnp.dot(f1i, bb, precision=HI, preferred_element_type=f32)
            return carry

        lax.fori_loop(0, LANES, body, 0)
        z = z_ref[0]
        o_ref[0] = (x0_ref[0].astype(f32) * (y_scr[...] + z * skip_ref[...])).astype(o_ref.dtype)


def hyena_conv(z, x0, spec_re, spec_im, inv_norm, skip, dc):
    B, n, W = z.shape
    nb = W // LANES
    k1n, k1p, t1n = dc["k1n"], dc["k1p"], dc["t1n"]
    seq = pl.BlockSpec((1, n, LANES), lambda b, c, k: (b, 0, c))
    fspec = pl.BlockSpec((LANES, LANES), lambda b, c, k: (k, c))
    gspec = pl.BlockSpec((LANES, LANES), lambda b, c, k: (k, nb + c))
    cspec = pl.BlockSpec((1, 2 * LANES, LANES), lambda b, c, k: (k, 0, 0))
    vspec = pl.BlockSpec((1, LANES), lambda b, c, k: (0, c))
    return pl.pallas_call(
        functools.partial(_hyena_conv_kernel, t1n=t1n, k1p=k1p),
        grid=(B, nb, k1n),
        in_specs=[seq, seq,
                  pl.BlockSpec(dc["f1c"].shape, lambda b, c, k: (0, 0)), cspec, cspec,
                  pl.BlockSpec(dc["f1i"].shape, lambda b, c, k: (0, 0)),
                  fspec, fspec, gspec, gspec, vspec, vspec],
        out_specs=seq,
        out_shape=jax.ShapeDtypeStruct((B, n, W), bf16),
        scratch_shapes=[pltpu.VMEM((k1p * LANES, LANES), f32), pltpu.VMEM((k1p * LANES, LANES), f32),
                        pltpu.VMEM((n, LANES), f32)],
        compiler_params=_cparams(("parallel", "parallel", "arbitrary")),
        name="hyena_conv",
    )(z, x0, dc["f1c"], dc["g"], dc["gi"], dc["f1i"], spec_re, spec_im, spec_re, spec_im, inv_norm, skip)


def _hyena_ctx_kernel(p0_ref, p1_ref, p2_ref, w0_ref, w1_ref, w2_ref, b0_ref, b1_ref, b2_ref,
                      hf_ref, hb_ref, inv_ref, skip_ref, fd_ref, fi_ref, o_ref, *, n):
    x0 = _conv3(p0_ref[0], w0_ref, b0_ref)
    z = _conv3(p1_ref[0], w1_ref, b1_ref) * _conv3(p2_ref[0], w2_ref, b2_ref)
    fd = fd_ref[...]
    N = 2 * n
    dft = lambda a: jnp.dot(fd, a, precision=HI, preferred_element_type=f32)
    zf, hf, hb = dft(z), dft(hf_ref[...]), dft(hb_ref[...])
    inv = inv_ref[...]
    kre = (hf[:N] + hb[:N]) * inv
    kim = (hf[N:] - hb[N:]) * inv
    yre = zf[:N] * kre - zf[N:] * kim
    yim = zf[:N] * kim + zf[N:] * kre
    y = jnp.dot(fi_ref[...], jnp.concatenate([yre, yim], axis=0), precision=HI, preferred_element_type=f32)
    o_ref[0] = (x0 * (y + z * skip_ref[...])).astype(o_ref.dtype)


def hyena_ctx(p, conv_w, conv_b, h, inv_norm, skip):
    B, n, _ = p.shape
    N = 2 * n
    nb = HY_W // LANES
    k = np.arange(N)
    t = np.arange(n)
    ph = 2 * np.pi * ((k[:, None] * t[None, :]) % N) / N
    fd = jnp.asarray(np.concatenate([np.cos(ph), -np.sin(ph)], axis=0).astype(np.float32))
    fi = jnp.asarray((np.concatenate([np.cos(ph.T), -np.sin(ph.T)], axis=1) / N).astype(np.float32))
    conv_b = conv_b.reshape(1, 3 * HY_W)
    pspec = lambda g: pl.BlockSpec((1, n, LANES), lambda b, c: (b, 0, g * nb + c))
    wspec = lambda g: pl.BlockSpec((3, LANES), lambda b, c: (0, g * nb + c))
    bspec = lambda g: pl.BlockSpec((1, LANES), lambda b, c: (0, g * nb + c))
    vspec = pl.BlockSpec((1, LANES), lambda b, c: (0, c))
    return pl.pallas_call(
        functools.partial(_hyena_ctx_kernel, n=n),
        grid=(B, nb),
        in_specs=[pspec(0), pspec(1), pspec(2), wspec(0), wspec(1), wspec(2), bspec(0), bspec(1), bspec(2),
                  pl.BlockSpec((n, LANES), lambda b, c: (0, c)), pl.BlockSpec((n, LANES), lambda b, c: (0, nb + c)),
                  vspec, vspec,
                  pl.BlockSpec(fd.shape, lambda b, c: (0, 0)), pl.BlockSpec(fi.shape, lambda b, c: (0, 0))],
        out_specs=pl.BlockSpec((1, n, LANES), lambda b, c: (b, 0, c)),
        out_shape=jax.ShapeDtypeStruct((B, n, HY_W), bf16),
        compiler_params=_cparams(("parallel", "parallel")),
        name="hyena_ctx",
    )(p, p, p, conv_w, conv_w, conv_w, conv_b, conv_b, conv_b, h, h, inv_norm, skip, fd, fi)


def _merge_kernel(ra_ref, nb_ref, hc_ref, g_ref, wa_ref, wb_ref, wc_ref, wo_ref, x_ref, m_ref, o_ref):
    d = D_MODEL
    ya = jnp.dot(ra_ref[...], wa_ref[...], preferred_element_type=f32)
    yb = jnp.dot(nb_ref[...], wb_ref[...], preferred_element_type=f32)
    yc = jnp.dot(hc_ref[...], wc_ref[...], preferred_element_type=f32)
    g = g_ref[...].astype(f32)
    mix = g[:, :d] * ya + g[:, d:2 * d] * yb + g[:, 2 * d:] * yc
    y = jnp.dot(mix.astype(bf16), wo_ref[...], preferred_element_type=f32)
    o_ref[...] = x_ref[...] + m_ref[0] * y


def merge(ra, nb, hc, gates, wa, wb, wc, wo, x2d, m, rows_per_group):
    R = x2d.shape[0]
    tm = 512
    tpg = rows_per_group // tm
    G = m.shape[0]
    row = lambda w: pl.BlockSpec((tm, w), lambda i: (i, 0))
    full = lambda a: pl.BlockSpec(a.shape, lambda i: (0, 0))
    return pl.pallas_call(
        _merge_kernel,
        grid=(R // tm,),
        in_specs=[row(HG_W), row(NA_W), row(HY_W), row(3 * D_MODEL), full(wa), full(wb), full(wc), full(wo),
                  row(D_MODEL), pl.BlockSpec((1, 1, D_MODEL), lambda i: (i // tpg, 0, 0))],
        out_specs=row(D_MODEL),
        out_shape=jax.ShapeDtypeStruct((R, D_MODEL), f32),
        compiler_params=_cparams(("parallel",)),
        name="merge",
    )(ra, nb, hc, gates, wa, wb, wc, wo, x2d, m.reshape(G, 1, D_MODEL))


def _router_kernel(x_ref, g_ref, sh_ref, sc_ref, wr_ref, h_ref, a_ref):
    x = x_ref[...]
    ms = jnp.mean(x * x, axis=-1, keepdims=True)
    h = x * lax.rsqrt(ms + EPS) * g_ref[...] * (1.0 + sc_ref[0]) + sh_ref[0]
    h_ref[...] = h
    lt = _nt(wr_ref[...], h, precision=HI)
    e = jnp.exp(lt - jnp.max(lt, axis=0, keepdims=True))
    a_ref[0] = e / jnp.sum(e, axis=0, keepdims=True)


def router(x2d, g, shift, scale, w_router, n_per_set):
    R = x2d.shape[0]
    tm = min(512, n_per_set)
    tps = n_per_set // tm
    S = R // n_per_set
    G = shift.shape[0]
    gmap = (lambda i: (i // tps, 0, 0)) if G > 1 else (lambda i: (0, 0, 0))
    return pl.pallas_call(
        _router_kernel,
        grid=(R // tm,),
        in_specs=[pl.BlockSpec((tm, D_MODEL), lambda i: (i, 0)),
                  pl.BlockSpec((1, D_MODEL), lambda i: (0, 0)),
                  pl.BlockSpec((1, 1, D_MODEL), gmap),
                  pl.BlockSpec((1, 1, D_MODEL), gmap),
                  pl.BlockSpec((N_EXPERTS, D_MODEL), lambda i: (0, 0))],
        out_specs=[pl.BlockSpec((tm, D_MODEL), lambda i: (i, 0)),
                   pl.BlockSpec((1, N_EXPERTS, tm), lambda i: (i // tps, 0, i % tps))],
        out_shape=[jax.ShapeDtypeStruct((R, D_MODEL), f32), jax.ShapeDtypeStruct((S, N_EXPERTS, n_per_set), f32)],
        compiler_params=_cparams(("parallel",)),
        name="router",
    )(x2d, g.reshape(1, D_MODEL), shift.reshape(G, 1, D_MODEL), scale.reshape(G, 1, D_MODEL), w_router.T)


SEL_BLK = 256


def _prefix_incl(mask_f, tri, T):
    outs = []
    off = jnp.zeros((mask_f.shape[0], 1), f32)
    for b in range(T // SEL_BLK):
        blk = mask_f[:, b * SEL_BLK:(b + 1) * SEL_BLK].astype(bf16)
        pre = jnp.dot(blk, tri, preferred_element_type=f32) + off
        outs.append(pre)
        off = pre[:, SEL_BLK - 1:SEL_BLK]
    return jnp.concatenate(outs, axis=1)


def _select_kernel(a_ref, tri_ref, idx_ref, g_ref, posm, affs, *, T, cap, rb):
    e = pl.program_id(1)

    @pl.when(e == 0)
    def _():
        aff = a_ref[0]
        bits = pltpu.bitcast(aff, i32)
        tri = tri_ref[...]

        def bit_step(i, thr):
            cand = thr | (1 << (30 - i))
            cnt = jnp.sum((bits >= cand).astype(f32), axis=1, keepdims=True)
            return jnp.where(cnt >= cap, cand, thr)

        thr = lax.fori_loop(0, 31, bit_step, jnp.zeros((N_EXPERTS, 1), i32))
        gt = bits > thr
        eq = bits == thr
        need = cap - jnp.sum(gt.astype(f32), axis=1, keepdims=True)
        eqf = eq.astype(f32)
        rank_eq = _prefix_incl(eqf, tri, T) - eqf
        sel = gt | (eq & (rank_eq < need))
        self_ = sel.astype(f32)
        pos = _prefix_incl(self_, tri, T) - self_
        pm = jnp.where(sel, pos, -1.0)
        for k in range(N_EXPERTS):
            posm[k] = pm[k:k + 1, :]
            affs[k] = aff[k:k + 1, :]

    nq = min(4, T // LANES)
    tb = nq * LANES
    for r0 in range(0, cap, rb):
        rid = (lax.broadcasted_iota(i32, (rb, LANES), 0) + r0).astype(f32)
        lane = lax.broadcasted_iota(i32, (rb, LANES), 1)

        def t_step(j, carry):
            acc_i, acc_g = carry
            for q in range(nq):
                t0 = pl.multiple_of(j * tb, tb) + q * LANES
                pr = posm[e, :, pl.ds(t0, LANES)]
                ar = affs[e, :, pl.ds(t0, LANES)]
                hit = pr == rid
                acc_i = acc_i + jnp.where(hit, (lane + t0).astype(f32), 0.0)
                acc_g = acc_g + jnp.where(hit, ar, 0.0)
            return acc_i, acc_g

        z = jnp.zeros((rb, LANES), f32)
        acc_i, acc_g = lax.fori_loop(0, T // tb, t_step, (z, z))
        idx_ref[0, 0, r0:r0 + rb, :] = jnp.sum(acc_i, axis=1, keepdims=True).astype(i32)
        g_ref[0, 0, r0:r0 + rb, :] = jnp.sum(acc_g, axis=1, keepdims=True)


def select_topk(aff, cap):
    S, E, T = aff.shape
    rb = min(cap, LANES)
    tri = jnp.asarray(np.triu(np.ones((SEL_BLK, SEL_BLK), np.float32)), bf16)
    idx, g = pl.pallas_call(
        functools.partial(_select_kernel, T=T, cap=cap, rb=rb),
        grid=(S, E),
        in_specs=[pl.BlockSpec((1, E, T), lambda s, e: (s, 0, 0)),
                  pl.BlockSpec((SEL_BLK, SEL_BLK), lambda s, e: (0, 0))],
        out_specs=[pl.BlockSpec((1, 1, cap, 1), lambda s, e: (s, e, 0, 0)),
                   pl.BlockSpec((1, 1, cap, 1), lambda s, e: (s, e, 0, 0))],
        out_shape=[jax.ShapeDtypeStruct((S, E, cap, 1), i32), jax.ShapeDtypeStruct((S, E, cap, 1), f32)],
        scratch_shapes=[pltpu.VMEM((E, 1, T), f32), pltpu.VMEM((E, 1, T), f32)],
        compiler_params=_cparams(("parallel", "arbitrary")),
        name="select_topk",
    )(aff, tri)
    return idx[..., 0], g[..., 0]


EXPERT_TF = 256


def _expert_kernel(*refs, segs):
    nsrc = max(s[2] for s in segs) + 1
    idx_ref = refs[0]
    g_ref, m_ref, wg_ref, wu_ref, wd_ref = refs[1:6]
    src = refs[6:6 + nsrc]
    outs = refs[6 + 2 * nsrc:6 + 3 * nsrc]
    xg32, xg16, acc, sem = refs[6 + 3 * nsrc:]
    j = pl.program_id(1)

    def row_copies(hbm_of_seg, to_vmem):
        def run(wait):
            for (r0, nr, si, _) in segs:
                hbm = hbm_of_seg(si)

                def body(r, carry):
                    a = hbm.at[pl.ds(idx_ref[0, 0, r], 1), :]
                    b = xg32.at[pl.ds(r, 1), :]
                    cp = pltpu.make_async_copy(a, b, sem) if to_vmem else pltpu.make_async_copy(b, a, sem)
                    cp.wait() if wait else cp.start()
                    return carry

                lax.fori_loop(r0, r0 + nr, body, 0)
        run(False)
        run(True)

    @pl.when(j == 0)
    def _():
        row_copies(lambda si: src[si], True)
        xg16[...] = xg32[...].astype(bf16)
        acc[...] = jnp.zeros_like(acc)

    x = xg16[...]
    a = jnp.dot(x, wg_ref[0].astype(bf16), preferred_element_type=f32)
    u = jnp.dot(x, wu_ref[0].astype(bf16), preferred_element_type=f32)
    hmid = (_silu(a) * u).astype(bf16)
    acc[...] += jnp.dot(hmid, wd_ref[0].astype(bf16), preferred_element_type=f32)

    @pl.when(j == pl.num_programs(1) - 1)
    def _():
        row_copies(lambda si: outs[si], True)
        for (r0, nr, _, mi) in segs:
            sl = slice(r0, r0 + nr)
            xg32[sl, :] = xg32[sl, :] + m_ref[mi:mi + 1, :] * (acc[sl, :] * g_ref[0, sl, :])
        row_copies(lambda si: outs[si], False)


def expert_ffn(idx, gates, mvec, w_gate, w_up, w_down, srcs, resids, segs):
    E, _, R = idx.shape
    nsrc = len(srcs)
    nf = D_FF_EXPERT // EXPERT_TF
    anyspec = pl.BlockSpec(memory_space=pl.ANY)
    in_specs = [pl.BlockSpec((1, 1, R), lambda e, j: (e, 0, 0), memory_space=pltpu.SMEM),
                pl.BlockSpec((1, R, 1), lambda e, j: (e, 0, 0)),
                pl.BlockSpec((8, D_MODEL), lambda e, j: (0, 0)),
                pl.BlockSpec((1, D_MODEL, EXPERT_TF), lambda e, j: (e, 0, j)),
                pl.BlockSpec((1, D_MODEL, EXPERT_TF), lambda e, j: (e, 0, j)),
                pl.BlockSpec((1, EXPERT_TF, D_MODEL), lambda e, j: (e, j, 0))]
    in_specs += [anyspec] * (2 * nsrc)
    res = pl.pallas_call(
        functools.partial(_expert_kernel, segs=segs),
        grid=(E, nf),
        in_specs=in_specs,
        out_specs=[anyspec] * nsrc,
        out_shape=[jax.ShapeDtypeStruct(r.shape, r.dtype) for r in resids],
        input_output_aliases={6 + nsrc + k: k for k in range(nsrc)},
        scratch_shapes=[pltpu.VMEM((R, D_MODEL), f32), pltpu.VMEM((R, D_MODEL), bf16),
                        pltpu.VMEM((R, D_MODEL), f32), pltpu.SemaphoreType.DMA(())],
        compiler_params=_cparams(("arbitrary", "arbitrary")),
        name="expert_ffn",
    )(idx, gates, mvec, w_gate, w_up, w_down, *srcs, *resids)
    return res


def _rope_tables(n):
    half = NA_HD // 2
    q = half // 2
    inv = ROPE_THETA ** (-jnp.arange(q, dtype=f32) / q)
    pos = jnp.arange(n)
    ang_r = (pos // GRID_W).astype(f32)[:, None] * inv
    ang_c = (pos % GRID_W).astype(f32)[:, None] * inv
    zero = jnp.zeros_like(ang_r)
    c = jnp.concatenate([jnp.cos(ang_r)] * 2 + [jnp.cos(ang_c)] * 2, axis=1)
    s1 = jnp.concatenate([-jnp.sin(ang_r), zero, -jnp.sin(ang_c), zero], axis=1)
    s2 = jnp.concatenate([zero, jnp.sin(ang_r), zero, jnp.sin(ang_c)], axis=1)
    two = lambda a: jnp.concatenate([a, a], axis=1)
    return two(c), two(s1), two(s2)


def _mixing(hx, hc, need_ctx, B, N, NC, la, lc, w_in, q_gain, k_gain, table, rope, bd, conv_w, conv_b,
            spec, inv_norm, skip, filt_c, wa, wb, wc, wo, x2d, c2d, mx2, mc2):
    tile8 = lambda v: jnp.tile(v.reshape(1, NA_HD), (1, NA_HEADS))
    qg, kg = tile8(q_gain), tile8(k_gain)
    norm_aux = [("col", kg), ("const", bd)]
    rope_aux = [("row", rope[0]), ("row", rope[1]), ("row", rope[2])]
    lf_aux = lambda d: [("col", la[d:d + 1]), ("col", lc[d:d + 1])]
    tc = hc.shape[0]

    lff_c = project(hc, w_in, OFF_FF, 512, _epi_logforget, lf_aux(0), (f32,), tm=tc).reshape(B, NC, 512)
    lfb_c = project(hc, w_in, OFF_FB, 512, _epi_logforget, lf_aux(1), (f32,), tm=tc).reshape(B, NC, 512)
    i_c = project(hc, w_in, OFF_I, 512, _epi_raw, tm=tc).reshape(B, NC, 512)
    k_c = project(hc, w_in, OFF_NK, 512, _epi_norm, norm_aux, tm=tc).reshape(B, NC, 512)
    v_c = project(hc, w_in, OFF_NV, 512, _epi_raw, tm=tc).reshape(B, NC, 512)
    if need_ctx:
        q_c = project(hc, w_in, OFF_HQ, 512, _epi_silu, tm=tc).reshape(B, NC, 512)
    else:
        q_c = jnp.zeros((B, NC, 512), bf16)
    s0 = jnp.zeros((B, HG_HEADS, LANES, LANES), f32)
    oc_f, s_f = hgrn_scan(lff_c, i_c, q_c, s0, False)

    lff_x = project(hx, w_in, OFF_FF, 512, _epi_logforget, lf_aux(0), (f32,)).reshape(B, N, 512)
    lfb_x = project(hx, w_in, OFF_FB, 512, _epi_logforget, lf_aux(1), (f32,)).reshape(B, N, 512)
    i_x = project(hx, w_in, OFF_I, 512, _epi_raw).reshape(B, N, 512)
    q_x = project(hx, w_in, OFF_HQ, 512, _epi_silu).reshape(B, N, 512)
    g_x = project(hx, w_in, OFF_HG, 512, _epi_silu).reshape(B, N, 512)
    k_x = project(hx, w_in, OFF_NK, 512, _epi_norm_rope, norm_aux + rope_aux, rows_per_seq=N).reshape(B, N, 512)
    v_x = project(hx, w_in, OFF_NV, 512, _epi_raw).reshape(B, N, 512)
    qn_x, qr_x = project(hx, w_in, OFF_NQ, 512, _epi_norm_both, [("col", qg), ("const", bd)] + rope_aux,
                         (bf16, bf16), rows_per_seq=N)
    p_x = project(hx, w_in, OFF_HY, 3 * HY_W, _epi_raw, out_dtypes=(f32,)).reshape(B, N, 3 * HY_W)
    gates_x = project(hx, w_in, OFF_GATE, 3 * D_MODEL, _epi_sigmoid)

    ox_f, _ = hgrn_scan(lff_x, i_x, q_x, s_f, False)
    if need_ctx:
        g_c = project(hc, w_in, OFF_HG, 512, _epi_silu, tm=tc).reshape(B, NC, 512)
        ra_c, s_b = hgrn_scan(lfb_c, i_c, q_c, s0, True, oc_f, g_c)
    else:
        _, s_b = hgrn_scan(lfb_c, i_c, q_c, s0, True)
    ra_x, _ = hgrn_scan(lfb_x, i_x, q_x, s_b, True, ox_f, g_x)

    nb_x = neighbourhood_attention(qr_x.reshape(B, N, 512), qn_x.reshape(B, N, 512), k_x, v_x, k_c, v_c, table)

    z_x, x0_x = hyena_pre(p_x, conv_w, conv_b)
    hy_x = hyena_conv(z_x, x0_x, spec[0], spec[1], inv_norm, skip, spec[2])

    x_new = merge(ra_x.reshape(B * N, 512), nb_x.reshape(B * N, 512), hy_x.reshape(B * N, 512), gates_x,
                  wa, wb, wc, wo, x2d, mx2, N)
    if not need_ctx:
        return x_new, None

    qn_c = project(hc, w_in, OFF_NQ, 512, _epi_norm, [("col", qg), ("const", bd)], tm=tc).reshape(B, NC, 512)
    nb_c = context_attention(qn_c, k_c, v_c)
    p_c = project(hc, w_in, OFF_HY, 3 * HY_W, _epi_raw, out_dtypes=(f32,), tm=tc).reshape(B, NC, 3 * HY_W)
    hy_c = hyena_ctx(p_c, conv_w, conv_b, filt_c[0], filt_c[1], skip)
    gates_c = project(hc, w_in, OFF_GATE, 3 * D_MODEL, _epi_sigmoid, tm=tc)
    c_new = merge(ra_c.reshape(B * NC, 512), nb_c.reshape(B * NC, 512), hy_c.reshape(B * NC, 512), gates_c,
                  wa, wb, wc, wo, c2d, mc2, B * NC)
    return x_new, c_new


def kernel(x, c, ctx, c_ctx, w_mod, b_mod, norm_mix, norm_ffn, w_in, hg_lb, na_q_gain, na_k_gain, na_rpb,
           hy_conv_w, hy_conv_b, hy_pe_w1, hy_pe_b1, hy_pe_freq1, hy_pe_w2, hy_pe_b2, hy_pe_freq2, hy_pe_w3,
           hy_skip, w_branch_a, w_branch_b, w_branch_c, w_out, w_router, w_e_gate, w_e_up, w_e_down):
    B, N, D = x.shape
    NC = ctx.shape[1]
    E = N_EXPERTS
    cap_x = EC_CAP_FACTOR * N // E
    cap_c = EC_CAP_FACTOR * NC // E

    lb = jnp.cumsum(jax.nn.softmax(hg_lb.astype(f32), axis=0), axis=0)
    lb = lb - lb[:1]
    la_all, lc_all = jnp.log(lb), jnp.log1p(-lb)

    s8 = jnp.zeros((8, D), f32).at[:B].set(c).at[B].set(c_ctx)
    rope = _rope_tables(N)
    bd = jnp.asarray(np.kron(np.eye(NA_HEADS), np.full((NA_HD, NA_HD), 1.0 / NA_HD)).astype(np.float32), bf16)
    dcx = _dft_consts(N)

    x2d = x.reshape(B * N, D)
    c2d = ctx.reshape(B * NC, D)
    for l in range(DEPTH):
        need_ctx = l < DEPTH - 1
        mv = modvec(s8, w_mod[l], b_mod[l])
        mx = [mv[:B, k * D:(k + 1) * D] for k in range(6)]
        mc = [mv[B:B + 1, k * D:(k + 1) * D] for k in range(6)]
        w_in_l = w_in[l].astype(bf16)
        hx = modulate(x2d, norm_mix[l], mx[0], mx[1], N, bf16)
        hc = modulate(c2d, norm_mix[l], mc[0], mc[1], B * NC, bf16)

        filt = (hy_pe_w1[l], hy_pe_b1[l], hy_pe_freq1[l], hy_pe_w2[l], hy_pe_b2[l], hy_pe_freq2[l], hy_pe_w3[l])
        h_x, nrm_x = hyena_filter(N, *filt)
        sre, sim = hyena_spectrum(h_x, dcx)
        skip = hy_skip[l].reshape(1, HY_W)
        filt_c = None
        if need_ctx:
            h_c, nrm_c = hyena_filter(NC, *filt)
            filt_c = (h_c, 1.0 / nrm_c)

        x2d, c_new = _mixing(
            hx, hc, need_ctx, B, N, NC, la_all[l], lc_all[l], w_in_l, na_q_gain[l], na_k_gain[l],
            _na_bias_table(na_rpb[l]), rope, bd, hy_conv_w[l], hy_conv_b[l], (sre, sim, dcx), 1.0 / nrm_x, skip,
            filt_c, w_branch_a[l].astype(bf16), w_branch_b[l].astype(bf16), w_branch_c[l].astype(bf16),
            w_out[l].astype(bf16), x2d, c2d, mx[2], mc[2])

        h2, aff = router(x2d, norm_ffn[l], mx[3], mx[4], w_router[l], N)
        idx, g = select_topk(aff, cap_x)
        idx = idx + (jnp.arange(B, dtype=i32) * N)[:, None, None]
        idx_e = idx.transpose(1, 0, 2).reshape(E, B * cap_x)
        g_e = g.transpose(1, 0, 2).reshape(E, B * cap_x)
        segs = [(b * cap_x, cap_x, 0, b) for b in range(B)]
        srcs, resids = [h2], [x2d]
        if need_ctx:
            c2d = c_new
            hc2, aff_c = router(c2d, norm_ffn[l], mc[3], mc[4], w_router[l], NC)
            idx_c, g_c = select_topk(aff_c, cap_c)
            idx_c = idx_c + (jnp.arange(B, dtype=i32) * NC)[:, None, None]
            idx_e = jnp.concatenate([idx_e, idx_c.transpose(1, 0, 2).reshape(E, B * cap_c)], axis=1)
            g_e = jnp.concatenate([g_e, g_c.transpose(1, 0, 2).reshape(E, B * cap_c)], axis=1)
            segs.append((B * cap_x, B * cap_c, 1, B))
            srcs.append(hc2)
            resids.append(c2d)
        mvec = jnp.zeros((8, D), f32).at[:B].set(mx[5]).at[B].set(mc[5][0])
        res = expert_ffn(idx_e[:, None, :], g_e[:, :, None], mvec, w_e_gate[l], w_e_up[l], w_e_down[l],
                         srcs, resids, segs)
        x2d = res[0]
        if need_ctx:
            c2d = res[1]
    return x2d.reshape(B, N, D)
```

```python
import functools
import math

import numpy as np
import jax
import jax.numpy as jnp
from jax import lax
from jax.experimental import pallas as pl
from jax.experimental.pallas import tpu as pltpu

f32 = jnp.float32
bf16 = jnp.bfloat16
i32 = jnp.int32
HI = lax.Precision.HIGHEST

D_MODEL = 1024
DEPTH = 2
GRID_W = 64
EPS = 1e-6
HG_HEADS = 4
HG_W = 512
HG_CHUNK = 64
NA_HEADS = 8
NA_HD = 64
NA_W = 512
NA_WIN_R = 8
NA_WIN_C = 16
ROPE_THETA = 10000.0
HY_W = 512
HY_BANDS = 16
HY_PE_DIM = 1 + 2 * HY_BANDS
HY_FILT_HID = 64
HY_FAST_DECAY = 0.3
HY_SLOW_DECAY = 1.5
HY_TARGET = 1e-2
OFF_FF = 0
OFF_FB = 512
OFF_I = 1024
OFF_NK = 1536
OFF_NV = 2048
OFF_HQ = 2560
OFF_NQ = 3072
OFF_HG = 3584
OFF_HY = 4096
OFF_GATE = 5632
IN_COLS = 8704
N_EXPERTS = 16
EC_CAP_FACTOR = 2
D_FF_EXPERT = 2816

LANES = 128
NEG_BIG = -1e30
VMEM_LIMIT = 56 * 1024 * 1024


def _cparams(sem, vmem=VMEM_LIMIT):
    return pltpu.CompilerParams(dimension_semantics=sem, vmem_limit_bytes=vmem)


def _nt(a, b, precision=None):
    return lax.dot_general(a, b, (((1,), (1,)), ((), ())), precision=precision, preferred_element_type=f32)


def _tn(a, b, precision=None):
    return lax.dot_general(a, b, (((0,), (0,)), ((), ())), precision=precision, preferred_element_type=f32)


def _silu(x):
    return x * jax.nn.sigmoid(x)


def _split(x):
    hi = x.astype(bf16)
    return hi, (x - hi.astype(f32)).astype(bf16)


def _modvec_kernel(s_ref, w_ref, b_ref, o_ref):
    s = _silu(s_ref[...])
    o_ref[...] = jnp.dot(s, w_ref[...], precision=HI, preferred_element_type=f32) + b_ref[...]


def modvec(s8, w, b):
    n = w.shape[1]
    tn = 1024
    return pl.pallas_call(
        _modvec_kernel,
        grid=(n // tn,),
        in_specs=[pl.BlockSpec((8, D_MODEL), lambda j: (0, 0)),
                  pl.BlockSpec((D_MODEL, tn), lambda j: (0, j)),
                  pl.BlockSpec((1, tn), lambda j: (0, j))],
        out_specs=pl.BlockSpec((8, tn), lambda j: (0, j)),
        out_shape=jax.ShapeDtypeStruct((8, n), f32),
        compiler_params=_cparams(("parallel",)),
        name="modvec",
    )(s8, w, b.reshape(1, n))


def _modulate_kernel(x_ref, g_ref, sh_ref, sc_ref, o_ref):
    x = x_ref[...]
    ms = jnp.mean(x * x, axis=-1, keepdims=True)
    y = x * lax.rsqrt(ms + EPS)
    o_ref[...] = (y * g_ref[...] * (1.0 + sc_ref[0]) + sh_ref[0]).astype(o_ref.dtype)


def modulate(x2d, g, shift, scale, rows_per_group, out_dtype):
    R = x2d.shape[0]
    tm = 512
    tpg = rows_per_group // tm
    G = shift.shape[0]
    return pl.pallas_call(
        _modulate_kernel,
        grid=(R // tm,),
        in_specs=[pl.BlockSpec((tm, D_MODEL), lambda i: (i, 0)),
                  pl.BlockSpec((1, D_MODEL), lambda i: (0, 0)),
                  pl.BlockSpec((1, 1, D_MODEL), lambda i: (i // tpg, 0, 0)),
                  pl.BlockSpec((1, 1, D_MODEL), lambda i: (i // tpg, 0, 0))],
        out_specs=pl.BlockSpec((tm, D_MODEL), lambda i: (i, 0)),
        out_shape=jax.ShapeDtypeStruct((R, D_MODEL), out_dtype),
        compiler_params=_cparams(("parallel",)),
        name="modulate",
    )(x2d, g.reshape(1, D_MODEL), shift.reshape(G, 1, D_MODEL), scale.reshape(G, 1, D_MODEL))


def _log_sigmoid(z):
    return jnp.minimum(z, 0.0) - jnp.log1p(jnp.exp(-jnp.abs(z)))


def _epi_raw(acc, o_ref):
    o_ref[...] = acc.astype(o_ref.dtype)


def _epi_silu(acc, o_ref):
    o_ref[...] = _silu(acc).astype(o_ref.dtype)


def _epi_sigmoid(acc, o_ref):
    o_ref[...] = jax.nn.sigmoid(acc).astype(o_ref.dtype)


def _epi_logforget(acc, la_ref, lc_ref, o_ref):
    la = la_ref[...]
    c = lc_ref[...] + _log_sigmoid(acc)
    o_ref[...] = jnp.maximum(la, c) + jnp.log1p(jnp.exp(-jnp.abs(la - c)))


def _head_rms(acc, gain_ref, bd_ref):
    hi, lo = _split(acc * acc)
    ms = jnp.dot(hi, bd_ref[...], preferred_element_type=f32) + jnp.dot(lo, bd_ref[...], preferred_element_type=f32)
    return acc * lax.rsqrt(ms + EPS) * gain_ref[...]


def _rope(y, c_ref, s1_ref, s2_ref):
    reps = y.shape[1] // LANES
    c = jnp.concatenate([c_ref[...]] * reps, axis=1)
    s1 = jnp.concatenate([s1_ref[...]] * reps, axis=1)
    s2 = jnp.concatenate([s2_ref[...]] * reps, axis=1)
    w = y.shape[1]
    return y * c + pltpu.roll(y, w - 16, axis=1) * s1 + pltpu.roll(y, 16, axis=1) * s2


def _epi_norm(acc, gain_ref, bd_ref, o_ref):
    o_ref[...] = _head_rms(acc, gain_ref, bd_ref).astype(o_ref.dtype)


def _epi_norm_rope(acc, gain_ref, bd_ref, c_ref, s1_ref, s2_ref, o_ref):
    y = _head_rms(acc, gain_ref, bd_ref)
    o_ref[...] = _rope(y, c_ref, s1_ref, s2_ref).astype(o_ref.dtype)


def _epi_norm_both(acc, gain_ref, bd_ref, c_ref, s1_ref, s2_ref, on_ref, or_ref):
    y = _head_rms(acc, gain_ref, bd_ref)
    on_ref[...] = y.astype(on_ref.dtype)
    or_ref[...] = _rope(y, c_ref, s1_ref, s2_ref).astype(or_ref.dtype)


def _proj_kernel(h_ref, w_ref, *rest, epi):
    acc = jnp.dot(h_ref[...], w_ref[...], preferred_element_type=f32)
    epi(acc, *rest)


def project(h, w, c0, width, epi, aux=(), out_dtypes=(bf16,), tm=1024, rows_per_seq=None):
    R = h.shape[0]
    tn = 512
    nj = width // tn
    cb = c0 // tn
    in_specs = [pl.BlockSpec((tm, D_MODEL), lambda i, j: (i, 0)),
                pl.BlockSpec((D_MODEL, tn), lambda i, j: (0, cb + j))]
    args = [h, w]
    for kind, arr in aux:
        if kind == "col":
            in_specs.append(pl.BlockSpec((1, tn), lambda i, j: (0, j)))
        elif kind == "const":
            in_specs.append(pl.BlockSpec(arr.shape, lambda i, j: (0, 0)))
        else:
            tps = rows_per_seq // tm
            in_specs.append(pl.BlockSpec((tm, LANES), lambda i, j: (i % tps, 0)))
        args.append(arr)
    out_specs = [pl.BlockSpec((tm, tn), lambda i, j: (i, j)) for _ in out_dtypes]
    out_shape = [jax.ShapeDtypeStruct((R, width), dt) for dt in out_dtypes]
    res = pl.pallas_call(
        functools.partial(_proj_kernel, epi=epi),
        grid=(R // tm, nj),
        in_specs=in_specs,
        out_specs=out_specs,
        out_shape=out_shape,
        compiler_params=_cparams(("parallel", "parallel")),
        name="proj_" + epi.__name__[5:],
    )(*args)
    return res[0] if len(res) == 1 else res


def _hgrn_tmatrix(C, reverse):
    L = int(round(math.log2(C)))
    t = np.arange(C)
    tau = (C - 1 - t) if reverse else t
    tt, uu = tau[:, None], tau[None, :]
    T = np.zeros((2 + L, C, C), np.float32)
    T[0] = uu <= tt
    T[1] = uu > tt
    for l in range(L):
        same = (tt >> (l + 1)) == (uu >> (l + 1))
        tr = ((tt >> l) & 1) == 1
        ur = ((uu >> l) & 1) == 1
        T[2 + l] = same & ((tr & ur & (uu <= tt)) | (~tr & ~ur & (uu > tt)))
    return T.reshape((2 + L) * C, C)


def _hgrn_kernel(*refs, C, reverse, fuse):
    if fuse:
        lf_ref, v_ref, q_ref, s0_ref, t_ref, of_ref, g_ref, o_ref, sfin_ref, s_scr = refs
    else:
        lf_ref, v_ref, q_ref, s0_ref, t_ref, o_ref, sfin_ref, s_scr = refs
    L = int(round(math.log2(C)))
    c = pl.program_id(1)

    @pl.when(c == 0)
    def _():
        s_scr[...] = s0_ref[0]

    lf = lf_ref[0]
    hi, lo = _split(lf)
    tm = t_ref[...]
    E = jnp.dot(tm, hi, preferred_element_type=f32) + jnp.dot(tm, lo, preferred_element_type=f32)
    kc_all = 1.0 - jnp.exp(lf)
    q_all = q_ref[0].astype(f32)
    v_all = v_ref[0]

    row = lax.broadcasted_iota(i32, (C, LANES), 0)
    tau = (C - 1 - row) if reverse else row
    ti = lax.broadcasted_iota(i32, (C, C), 0)
    si = lax.broadcasted_iota(i32, (C, C), 1)
    if reverse:
        ti, si = C - 1 - ti, C - 1 - si
    end_row = 0 if reverse else C - 1

    outs = []
    for h in range(HG_HEADS):
        sl = slice(LANES * h, LANES * (h + 1))
        qh, kch, vh = q_all[:, sl], kc_all[:, sl], v_all[:, sl]
        b = E[0:C, sl]
        suf = E[C:2 * C, sl]
        st = s_scr[h]
        qb = (qh * jnp.exp(b)).astype(bf16)
        o = _nt(qb, st.astype(bf16))
        att = jnp.where(ti == si, _nt(qh.astype(bf16), kch.astype(bf16)), 0.0)
        for l in range(L):
            x = jnp.exp(E[(2 + l) * C:(3 + l) * C, sl])
            later = ((tau >> l) & 1) == 1
            ql = jnp.where(later, qh * x, 0.0).astype(bf16)
            kl = jnp.where(later, 0.0, kch * x).astype(bf16)
            att = att + jnp.where((ti >> (l + 1)) == (si >> (l + 1)), _nt(ql, kl), 0.0)
        o = o + jnp.dot(att.astype(bf16), vh, preferred_element_type=f32)
        kd = (kch * jnp.exp(suf)).astype(bf16)
        bend = b[end_row:end_row + 1, :]
        s_scr[h] = st * jnp.exp(bend) + _tn(vh, kd)
        outs.append(o)

    if fuse:
        of = of_ref[0]
        g = g_ref[0].astype(f32)
        res = []
        for h in range(HG_HEADS):
            sl = slice(LANES * h, LANES * (h + 1))
            tot = of[:, sl] + outs[h]
            ms = jnp.mean(tot * tot, axis=-1, keepdims=True)
            res.append(tot * lax.rsqrt(ms + EPS) * g[:, sl])
        o_ref[0] = jnp.concatenate(res, axis=1).astype(o_ref.dtype)
    else:
        o_ref[0] = jnp.concatenate(outs, axis=1).astype(o_ref.dtype)

    @pl.when(c == pl.num_programs(1) - 1)
    def _():
        sfin_ref[0] = s_scr[...]


def hgrn_scan(lf, v, q, s0, reverse, o_fwd=None, gate=None):
    B, N, W = lf.shape
    C = HG_CHUNK
    nch = N // C
    fuse = o_fwd is not None
    tmat = jnp.asarray(_hgrn_tmatrix(C, reverse), bf16)
    cmap = (lambda b, c: (b, nch - 1 - c, 0)) if reverse else (lambda b, c: (b, c, 0))
    seq = pl.BlockSpec((1, C, W), cmap)
    in_specs = [seq, seq, seq,
                pl.BlockSpec((1, HG_HEADS, LANES, LANES), lambda b, c: (b, 0, 0, 0)),
                pl.BlockSpec(tmat.shape, lambda b, c: (0, 0))]
    args = [lf, v, q, s0, tmat]
    if fuse:
        in_specs += [seq, seq]
        args += [o_fwd, gate]
    o, sfin = pl.pallas_call(
        functools.partial(_hgrn_kernel, C=C, reverse=reverse, fuse=fuse),
        grid=(B, nch),
        in_specs=in_specs,
        out_specs=[seq, pl.BlockSpec((1, HG_HEADS, LANES, LANES), lambda b, c: (b, 0, 0, 0))],
        out_shape=[jax.ShapeDtypeStruct((B, N, W), bf16 if fuse else f32),
                   jax.ShapeDtypeStruct((B, HG_HEADS, LANES, LANES), f32)],
        scratch_shapes=[pltpu.VMEM((HG_HEADS, LANES, LANES), f32)],
        compiler_params=_cparams(("parallel", "arbitrary")),
        name="hgrn_bwd" if reverse else "hgrn_fwd",
    )(*args)
    return o, sfin


def _na_kernel(qr_ref, qn_ref, k_ref, v_ref, kc_ref, vc_ref, tab_ref, o_ref, s_scr, p_scr, *, rows_per_step, n_rows):
    g = pl.program_id(2)
    scale = NA_HD ** -0.5
    lane = lax.broadcasted_iota(i32, (GRID_W, LANES), 1)
    kcx = kc_ref[0]
    vcx = vc_ref[0]
    win = NA_WIN_R * GRID_W
    ctx_len = kcx.shape[0]

    starts = []
    for i in range(rows_per_step):
        r = g * rows_per_step + i
        rs = jnp.clip(r - NA_WIN_R // 2, 0, n_rows - NA_WIN_R)
        off = rs - r + (NA_WIN_R - 1)
        start = pl.multiple_of(rs * GRID_W, GRID_W)
        starts.append(start)
        kw = k_ref[0, pl.ds(start, win), :]
        qr = qr_ref[0, i * GRID_W:(i + 1) * GRID_W, :]
        qn = qn_ref[0, i * GRID_W:(i + 1) * GRID_W, :]
        for hh in range(2):
            m = (lane >= NA_HD * hh) & (lane < NA_HD * (hh + 1))
            qrm = jnp.where(m, qr, jnp.zeros_like(qr))
            qnm = jnp.where(m, qn, jnp.zeros_like(qn))
            row0 = (2 * i + hh) * GRID_W
            s_scr[row0:row0 + GRID_W, 0:win] = _nt(qrm, kw) * scale + tab_ref[hh, off]
            s_scr[row0:row0 + GRID_W, win:win + ctx_len] = _nt(qnm, kcx) * scale

    def softmax_rows(c, carry):
        r0 = pl.multiple_of(c * LANES, LANES)
        s = s_scr[pl.ds(r0, LANES), :]
        p = jnp.exp(s - jnp.max(s, axis=-1, keepdims=True))
        inv = 1.0 / jnp.sum(p, axis=-1, keepdims=True)
        p_scr[pl.ds(r0, LANES), :] = (p * inv).astype(bf16)
        return carry

    lax.fori_loop(0, 2 * rows_per_step * GRID_W // LANES, softmax_rows, 0, unroll=2)

    for i in range(rows_per_step):
        vw = v_ref[0, pl.ds(starts[i], win), :]
        res = []
        for hh in range(2):
            row0 = (2 * i + hh) * GRID_W
            p = p_scr[row0:row0 + GRID_W, :]
            res.append(jnp.dot(p[:, :win], vw, preferred_element_type=f32)
                       + jnp.dot(p[:, win:], vcx, preferred_element_type=f32))
        o_ref[0, i * GRID_W:(i + 1) * GRID_W, :] = jnp.where(lane < NA_HD, res[0], res[1]).astype(o_ref.dtype)


def _na_bias_table(rpb):
    col = jnp.arange(GRID_W)
    cs = jnp.clip(col - NA_WIN_C // 2, 0, GRID_W - NA_WIN_C)
    kc = jnp.arange(GRID_W)
    valid = (kc[None, :] >= cs[:, None]) & (kc[None, :] < cs[:, None] + NA_WIN_C)
    dc = jnp.clip(kc[None, :] - col[:, None] + (NA_WIN_C - 1), 0, 2 * NA_WIN_C - 2)
    bc = jnp.where(valid[None, None], rpb[:, :, dc], NEG_BIG)
    t2 = jnp.stack([bc[:, o:o + NA_WIN_R] for o in range(NA_WIN_R)], axis=1)
    t2 = t2.transpose(0, 1, 3, 2, 4)
    return t2.reshape(NA_HEADS, NA_WIN_R, GRID_W, NA_WIN_R * GRID_W).astype(f32)


def neighbourhood_attention(q_rot, qn, k_rot, v, kc, vc, table):
    B, N, W = q_rot.shape
    n_rows = N // GRID_W
    rps = 8
    ctx_len = kc.shape[1]
    pairs = W // LANES
    keys = NA_WIN_R * GRID_W + ctx_len
    return pl.pallas_call(
        functools.partial(_na_kernel, rows_per_step=rps, n_rows=n_rows),
        grid=(B, pairs, n_rows // rps),
        in_specs=[pl.BlockSpec((1, rps * GRID_W, LANES), lambda b, p, g: (b, g, p)),
                  pl.BlockSpec((1, rps * GRID_W, LANES), lambda b, p, g: (b, g, p)),
                  pl.BlockSpec((1, N, LANES), lambda b, p, g: (b, 0, p)),
                  pl.BlockSpec((1, N, LANES), lambda b, p, g: (b, 0, p)),
                  pl.BlockSpec((1, ctx_len, LANES), lambda b, p, g: (b, 0, p)),
                  pl.BlockSpec((1, ctx_len, LANES), lambda b, p, g: (b, 0, p)),
                  pl.BlockSpec((2, NA_WIN_R, GRID_W, NA_WIN_R * GRID_W), lambda b, p, g: (p, 0, 0, 0))],
        out_specs=pl.BlockSpec((1, rps * GRID_W, LANES), lambda b, p, g: (b, g, p)),
        out_shape=jax.ShapeDtypeStruct((B, N, W), bf16),
        scratch_shapes=[pltpu.VMEM((2 * rps * GRID_W, keys), f32), pltpu.VMEM((2 * rps * GRID_W, keys), bf16)],
        compiler_params=_cparams(("parallel", "parallel", "arbitrary")),
        name="natten",
    )(q_rot, qn, k_rot, v, kc, vc, table)


def _ctx_attn_kernel(q_ref, k_ref, v_ref, o_ref):
    scale = NA_HD ** -0.5
    q = q_ref[0]
    k = k_ref[0]
    v = v_ref[0]
    lane = lax.broadcasted_iota(i32, q.shape, 1)
    res = []
    for hh in range(2):
        m = (lane >= NA_HD * hh) & (lane < NA_HD * (hh + 1))
        s = _nt(jnp.where(m, q, jnp.zeros_like(q)), k) * scale
        p = jnp.exp(s - jnp.max(s, axis=-1, keepdims=True))
        p = p / jnp.sum(p, axis=-1, keepdims=True)
        res.append(jnp.dot(p.astype(bf16), v, preferred_element_type=f32))
    o_ref[0] = jnp.where(lane < NA_HD, res[0], res[1]).astype(o_ref.dtype)


def context_attention(q, k, v):
    B, N, W = q.shape
    spec = pl.BlockSpec((1, N, LANES), lambda b, p: (b, 0, p))
    return pl.pallas_call(
        _ctx_attn_kernel,
        grid=(B, W // LANES),
        in_specs=[spec, spec, spec],
        out_specs=spec,
        out_shape=jax.ShapeDtypeStruct((B, N, W), bf16),
        compiler_params=_cparams(("parallel", "parallel")),
        name="ctx_attn",
    )(q, k, v)


def _filter_kernel(z_ref, w1_ref, b1_ref, f1_ref, w2_ref, b2_ref, f2_ref, w3_ref, dl_ref, h_ref, nrm_ref, *, tm):
    i = pl.program_id(0)
    z = z_ref[...]
    a = jnp.sin(f1_ref[...] * (jnp.dot(z, w1_ref[...], precision=HI, preferred_element_type=f32) + b1_ref[...]))
    a = jnp.sin(f2_ref[...] * (jnp.dot(a, w2_ref[...], precision=HI, preferred_element_type=f32) + b2_ref[...]))
    h = jnp.dot(a, w3_ref[...], precision=HI, preferred_element_type=f32)
    h = h * jnp.exp(-z[:, 0:1] * dl_ref[...])
    row = lax.broadcasted_iota(i32, h.shape, 0) + i * tm
    col = lax.broadcasted_iota(i32, h.shape, 1)
    h = jnp.where((row == 0) & (col >= HY_W), 0.0, h)
    h_ref[...] = h
    s = jnp.sum(jnp.abs(h), axis=0, keepdims=True)
    part = s[:, :HY_W] + s[:, HY_W:]

    @pl.when(i == 0)
    def _():
        nrm_ref[...] = part

    @pl.when(i > 0)
    def _():
        nrm_ref[...] = nrm_ref[...] + part


def hyena_filter(n, w1, b1, fr1, w2, b2, fr2, w3):
    t = jnp.linspace(0.0, 1.0, n, dtype=f32)[:, None]
    w = 2 * math.pi * jnp.arange(n, dtype=f32)[:, None] / n
    fb = jnp.linspace(1e-4, HY_BANDS - 1, HY_BANDS, dtype=f32)[None]
    z = jnp.concatenate([t, jnp.cos(fb * w), -jnp.sin(fb * w)], axis=-1)
    z = jnp.pad(z, ((0, 0), (0, LANES - HY_PE_DIM)))
    w1p = jnp.pad(w1.astype(f32), ((0, LANES - HY_PE_DIM), (0, 0)))
    deltas = jnp.abs(jnp.linspace(math.log(HY_TARGET) / HY_SLOW_DECAY, math.log(HY_TARGET) / HY_FAST_DECAY,
                                  2 * HY_W, dtype=f32))[None]
    tm = min(n, 512)
    hid = HY_FILT_HID
    full = lambda shape: pl.BlockSpec(shape, lambda i: (0, 0))
    return pl.pallas_call(
        functools.partial(_filter_kernel, tm=tm),
        grid=(n // tm,),
        in_specs=[pl.BlockSpec((tm, LANES), lambda i: (i, 0)),
                  full((LANES, hid)), full((1, hid)), full((1, hid)),
                  full((hid, hid)), full((1, hid)), full((1, hid)),
                  full((hid, 2 * HY_W)), full((1, 2 * HY_W))],
        out_specs=[pl.BlockSpec((tm, 2 * HY_W), lambda i: (i, 0)), pl.BlockSpec((1, HY_W), lambda i: (0, 0))],
        out_shape=[jax.ShapeDtypeStruct((n, 2 * HY_W), f32), jax.ShapeDtypeStruct((1, HY_W), f32)],
        compiler_params=_cparams(("arbitrary",)),
        name="hyena_filter",
    )(z, w1p, b1.reshape(1, hid), fr1.reshape(1, hid), w2.astype(f32), b2.reshape(1, hid), fr2.reshape(1, hid),
      w3.astype(f32), deltas)


def _conv3(u, w_ref, b_ref):
    n = u.shape[0]
    row = lax.broadcasted_iota(i32, u.shape, 0)
    prev = jnp.where(row == 0, 0.0, pltpu.roll(u, 1, axis=0))
    nxt = jnp.where(row == n - 1, 0.0, pltpu.roll(u, n - 1, axis=0))
    return prev * w_ref[0:1, :] + u * w_ref[1:2, :] + nxt * w_ref[2:3, :] + b_ref[...]


def _hyena_pre_kernel(p0_ref, p1_ref, p2_ref, w0_ref, w1_ref, w2_ref, b0_ref, b1_ref, b2_ref, z_ref, x0_ref):
    x0_ref[0] = _conv3(p0_ref[0], w0_ref, b0_ref).astype(x0_ref.dtype)
    z_ref[0] = _conv3(p1_ref[0], w1_ref, b1_ref) * _conv3(p2_ref[0], w2_ref, b2_ref)


def hyena_pre(p, conv_w, conv_b):
    B, N, _ = p.shape
    nb = HY_W // LANES
    conv_b = conv_b.reshape(1, 3 * HY_W)
    pspec = lambda g: pl.BlockSpec((1, N, LANES), lambda b, c: (b, 0, g * nb + c))
    wspec = lambda g: pl.BlockSpec((3, LANES), lambda b, c: (0, g * nb + c))
    bspec = lambda g: pl.BlockSpec((1, LANES), lambda b, c: (0, g * nb + c))
    ospec = pl.BlockSpec((1, N, LANES), lambda b, c: (b, 0, c))
    return pl.pallas_call(
        _hyena_pre_kernel,
        grid=(B, nb),
        in_specs=[pspec(0), pspec(1), pspec(2), wspec(0), wspec(1), wspec(2), bspec(0), bspec(1), bspec(2)],
        out_specs=[ospec, ospec],
        out_shape=[jax.ShapeDtypeStruct((B, N, HY_W), f32), jax.ShapeDtypeStruct((B, N, HY_W), bf16)],
        compiler_params=_cparams(("parallel", "parallel")),
        name="hyena_pre",
    )(p, p, p, conv_w, conv_w, conv_w, conv_b, conv_b, conv_b)


def _hl(a):
    a32 = jnp.asarray(a.astype(np.float32))
    hi = a32.astype(bf16)
    lo = (a32 - hi.astype(f32)).astype(bf16)
    return jnp.concatenate([hi, lo], axis=-2)


def _dot3(a_hl, m, x):
    xh, xl = _split(x)
    r = jnp.dot(a_hl, xh, preferred_element_type=f32)
    return r[:m] + r[m:] + jnp.dot(a_hl[:m], xl, preferred_element_type=f32)


def _dft_consts(n):
    N = 2 * n
    na = N // LANES
    t1n = na // 2
    k1n = na // 2 + 1
    k1p = -(-k1n // 8) * 8
    k1 = np.arange(k1n)
    t1 = np.arange(t1n)
    th = 2 * np.pi * ((t1[None, :] * k1[:, None]) % na) / na
    f1c = np.zeros((2 * k1p, t1n))
    f1c[:k1n] = np.cos(th)
    f1c[k1p:k1p + k1n] = -np.sin(th)
    k2 = np.arange(LANES)
    t2 = np.arange(LANES)
    m = (t2[None, None, :] * (k1[:, None, None] + na * k2[None, :, None])) % N
    ph = 2 * np.pi * m / N
    g = np.concatenate([np.cos(ph), -np.sin(ph)], axis=1)
    pht = ph.transpose(0, 2, 1)
    gi = np.concatenate([np.cos(pht), np.sin(pht)], axis=1)
    wk = np.where((k1 == 0) | (k1 == na // 2), 1.0, 2.0) / N
    f1i = np.zeros((t1n, 2 * k1p))
    f1i[:, :k1n] = np.cos(th.T) * wk[None, :]
    f1i[:, k1p:k1p + k1n] = -np.sin(th.T) * wk[None, :]
    return dict(na=na, t1n=t1n, k1n=k1n, k1p=k1p, f1c=_hl(f1c), g=_hl(g), gi=_hl(gi), f1i=_hl(f1i))


def _dft_stage1(src_ref, f1c_ref, are_ref, aim_ref, t1n, k1p):
    f1c = f1c_ref[...]

    def body(t2, carry):
        zs = src_ref[pl.ds(t2, t1n, stride=LANES), :]
        r = _dot3(f1c, 2 * k1p, zs)
        are_ref[pl.ds(t2, k1p, stride=LANES), :] = r[:k1p]
        aim_ref[pl.ds(t2, k1p, stride=LANES), :] = r[k1p:]
        return carry

    lax.fori_loop(0, LANES, body, 0, unroll=4)


def _cplx_left(gc_hl, xre, xim):
    p = _dot3(gc_hl, 2 * LANES, xre)
    q = _dot3(gc_hl, 2 * LANES, xim)
    return p[:LANES] - q[LANES:], p[LANES:] + q[:LANES]


def _spectrum_kernel(h_ref, f1c_ref, g_ref, xre_ref, xim_ref, are, aim, *, t1n, k1p):
    k1 = pl.program_id(1)

    @pl.when(k1 == 0)
    def _():
        _dft_stage1(h_ref, f1c_ref, are, aim, t1n, k1p)

    r0 = pl.multiple_of(k1 * LANES, LANES)
    xre, xim = _cplx_left(g_ref[0], are[pl.ds(r0, LANES), :], aim[pl.ds(r0, LANES), :])
    xre_ref[...] = xre
    xim_ref[...] = xim


def hyena_spectrum(h, dc):
    n, C = h.shape
    k1n, k1p, t1n = dc["k1n"], dc["k1p"], dc["t1n"]
    cw = LANES
    out = jax.ShapeDtypeStruct((k1n * LANES, C), f32)
    ospec = pl.BlockSpec((LANES, cw), lambda c, k: (k, c))
    return pl.pallas_call(
        functools.partial(_spectrum_kernel, t1n=t1n, k1p=k1p),
        grid=(C // cw, k1n),
        in_specs=[pl.BlockSpec((n, cw), lambda c, k: (0, c)),
                  pl.BlockSpec(dc["f1c"].shape, lambda c, k: (0, 0)),
                  pl.BlockSpec((1, 4 * LANES, LANES), lambda c, k: (k, 0, 0))],
        out_specs=[ospec, ospec],
        out_shape=[out, out],
        scratch_shapes=[pltpu.VMEM((k1p * LANES, cw), f32), pltpu.VMEM((k1p * LANES, cw), f32)],
        compiler_params=_cparams(("parallel", "arbitrary")),
        name="hyena_spectrum",
    )(h, dc["f1c"], dc["g"])


def _hyena_conv_kernel(z_ref, x0_ref, f1c_ref, g_ref, gi_ref, f1i_ref, fre_ref, fim_ref, gre_ref, gim_ref,
                       inv_ref, skip_ref, o_ref, are, aim, y_scr, *, t1n, k1p):
    k1 = pl.program_id(2)

    @pl.when(k1 == 0)
    def _():
        _dft_stage1(z_ref.at[0], f1c_ref, are, aim, t1n, k1p)

    r0 = pl.multiple_of(k1 * LANES, LANES)
    xre, xim = _cplx_left(g_ref[0], are[pl.ds(r0, LANES), :], aim[pl.ds(r0, LANES), :])
    inv = inv_ref[...]
    kre = (fre_ref[...] + gre_ref[...]) * inv
    kim = (fim_ref[...] - gim_ref[...]) * inv
    yre = xre * kre - xim * kim
    yim = xre * kim + xim * kre
    bre, bim = _cplx_left(gi_ref[0], yre, yim)
    are[pl.ds(r0, LANES), :] = bre
    aim[pl.ds(r0, LANES), :] = bim

    @pl.when(k1 == pl.num_programs(2) - 1)
    def _():
        f1i = f1i_ref[...]

        def body(t2, carry):
            bb = jnp.concatenate([are[pl.ds(t2, k1p, stride=LANES), :], aim[pl.ds(t2, k1p, stride=LANES), :]], axis=0)
            y_scr[pl.ds(t2, t1n, stride=LANES), :] = _dot3(f1i, t1n, bb)
            return carry

        lax.fori_loop(0, LANES, body, 0, unroll=4)
        z = z_ref[0]
        o_ref[0] = (x0_ref[0].astype(f32) * (y_scr[...] + z * skip_ref[...])).astype(o_ref.dtype)


def hyena_conv(z, x0, spec_re, spec_im, inv_norm, skip, dc):
    B, n, W = z.shape
    nb = W // LANES
    k1n, k1p, t1n = dc["k1n"], dc["k1p"], dc["t1n"]
    seq = pl.BlockSpec((1, n, LANES), lambda b, c, k: (b, 0, c))
    fspec = pl.BlockSpec((LANES, LANES), lambda b, c, k: (k, c))
    gspec = pl.BlockSpec((LANES, LANES), lambda b, c, k: (k, nb + c))
    cspec = pl.BlockSpec((1, 4 * LANES, LANES), lambda b, c, k: (k, 0, 0))
    vspec = pl.BlockSpec((1, LANES), lambda b, c, k: (0, c))
    return pl.pallas_call(
        functools.partial(_hyena_conv_kernel, t1n=t1n, k1p=k1p),
        grid=(B, nb, k1n),
        in_specs=[seq, seq,
                  pl.BlockSpec(dc["f1c"].shape, lambda b, c, k: (0, 0)), cspec, cspec,
                  pl.BlockSpec(dc["f1i"].shape, lambda b, c, k: (0, 0)),
                  fspec, fspec, gspec, gspec, vspec, vspec],
        out_specs=seq,
        out_shape=jax.ShapeDtypeStruct((B, n, W), bf16),
        scratch_shapes=[pltpu.VMEM((k1p * LANES, LANES), f32), pltpu.VMEM((k1p * LANES, LANES), f32),
                        pltpu.VMEM((n, LANES), f32)],
        compiler_params=_cparams(("parallel", "parallel", "arbitrary")),
        name="hyena_conv",
    )(z, x0, dc["f1c"], dc["g"], dc["gi"], dc["f1i"], spec_re, spec_im, spec_re, spec_im, inv_norm, skip)


def _hyena_ctx_kernel(p0_ref, p1_ref, p2_ref, w0_ref, w1_ref, w2_ref, b0_ref, b1_ref, b2_ref,
                      hf_ref, hb_ref, inv_ref, skip_ref, fd_ref, fi_ref, o_ref, *, n):
    x0 = _conv3(p0_ref[0], w0_ref, b0_ref)
    z = _conv3(p1_ref[0], w1_ref, b1_ref) * _conv3(p2_ref[0], w2_ref, b2_ref)
    fd = fd_ref[...]
    N = 2 * n
    dft = lambda a: jnp.dot(fd, a, precision=HI, preferred_element_type=f32)
    zf, hf, hb = dft(z), dft(hf_ref[...]), dft(hb_ref[...])
    inv = inv_ref[...]
    kre = (hf[:N] + hb[:N]) * inv
    kim = (hf[N:] - hb[N:]) * inv
    yre = zf[:N] * kre - zf[N:] * kim
    yim = zf[:N] * kim + zf[N:] * kre
    y = jnp.dot(fi_ref[...], jnp.concatenate([yre, yim], axis=0), precision=HI, preferred_element_type=f32)
    o_ref[0] = (x0 * (y + z * skip_ref[...])).astype(o_ref.dtype)


def hyena_ctx(p, conv_w, conv_b, h, inv_norm, skip):
    B, n, _ = p.shape
    N = 2 * n
    nb = HY_W // LANES
    k = np.arange(N)
    t = np.arange(n)
    ph = 2 * np.pi * ((k[:, None] * t[None, :]) % N) / N
    fd = jnp.asarray(np.concatenate([np.cos(ph), -np.sin(ph)], axis=0).astype(np.float32))
    fi = jnp.asarray((np.concatenate([np.cos(ph.T), -np.sin(ph.T)], axis=1) / N).astype(np.float32))
    conv_b = conv_b.reshape(1, 3 * HY_W)
    pspec = lambda g: pl.BlockSpec((1, n, LANES), lambda b, c: (b, 0, g * nb + c))
    wspec = lambda g: pl.BlockSpec((3, LANES), lambda b, c: (0, g * nb + c))
    bspec = lambda g: pl.BlockSpec((1, LANES), lambda b, c: (0, g * nb + c))
    vspec = pl.BlockSpec((1, LANES), lambda b, c: (0, c))
    return pl.pallas_call(
        functools.partial(_hyena_ctx_kernel, n=n),
        grid=(B, nb),
        in_specs=[pspec(0), pspec(1), pspec(2), wspec(0), wspec(1), wspec(2), bspec(0), bspec(1), bspec(2),
                  pl.BlockSpec((n, LANES), lambda b, c: (0, c)), pl.BlockSpec((n, LANES), lambda b, c: (0, nb + c)),
                  vspec, vspec,
                  pl.BlockSpec(fd.shape, lambda b, c: (0, 0)), pl.BlockSpec(fi.shape, lambda b, c: (0, 0))],
        out_specs=pl.BlockSpec((1, n, LANES), lambda b, c: (b, 0, c)),
        out_shape=jax.ShapeDtypeStruct((B, n, HY_W), bf16),
        compiler_params=_cparams(("parallel", "parallel")),
        name="hyena_ctx",
    )(p, p, p, conv_w, conv_w, conv_w, conv_b, conv_b, conv_b, h, h, inv_norm, skip, fd, fi)


def _merge_kernel(ra_ref, nb_ref, hc_ref, g_ref, wa_ref, wb_ref, wc_ref, wo_ref, x_ref, m_ref, o_ref):
    d = D_MODEL
    ya = jnp.dot(ra_ref[...], wa_ref[...], preferred_element_type=f32)
    yb = jnp.dot(nb_ref[...], wb_ref[...], preferred_element_type=f32)
    yc = jnp.dot(hc_ref[...], wc_ref[...], preferred_element_type=f32)
    g = g_ref[...].astype(f32)
    mix = g[:, :d] * ya + g[:, d:2 * d] * yb + g[:, 2 * d:] * yc
    y = jnp.dot(mix.astype(bf16), wo_ref[...], preferred_element_type=f32)
    o_ref[...] = x_ref[...] + m_ref[0] * y


def merge(ra, nb, hc, gates, wa, wb, wc, wo, x2d, m, rows_per_group):
    R = x2d.shape[0]
    tm = 512
    tpg = rows_per_group // tm
    G = m.shape[0]
    row = lambda w: pl.BlockSpec((tm, w), lambda i: (i, 0))
    full = lambda a: pl.BlockSpec(a.shape, lambda i: (0, 0))
    return pl.pallas_call(
        _merge_kernel,
        grid=(R // tm,),
        in_specs=[row(HG_W), row(NA_W), row(HY_W), row(3 * D_MODEL), full(wa), full(wb), full(wc), full(wo),
                  row(D_MODEL), pl.BlockSpec((1, 1, D_MODEL), lambda i: (i // tpg, 0, 0))],
        out_specs=row(D_MODEL),
        out_shape=jax.ShapeDtypeStruct((R, D_MODEL), f32),
        compiler_params=_cparams(("parallel",)),
        name="merge",
    )(ra, nb, hc, gates, wa, wb, wc, wo, x2d, m.reshape(G, 1, D_MODEL))


def _router_kernel(x_ref, g_ref, sh_ref, sc_ref, wrt_ref, wr_ref, h_ref, at_ref, am_ref):
    x = x_ref[...]
    ms = jnp.mean(x * x, axis=-1, keepdims=True)
    h = x * lax.rsqrt(ms + EPS) * g_ref[...] * (1.0 + sc_ref[0]) + sh_ref[0]
    h_ref[...] = h.astype(h_ref.dtype)
    lt = _nt(wrt_ref[...], h, precision=HI)
    et = jnp.exp(lt - jnp.max(lt, axis=0, keepdims=True))
    at_ref[0] = et / jnp.sum(et, axis=0, keepdims=True)
    lm = jnp.dot(h, wr_ref[...], precision=HI, preferred_element_type=f32)
    em = jnp.exp(lm - jnp.max(lm, axis=1, keepdims=True))
    am_ref[...] = em / jnp.sum(em, axis=1, keepdims=True)


def router(x2d, g, shift, scale, w_router, n_per_set):
    R = x2d.shape[0]
    tm = min(512, n_per_set)
    tps = n_per_set // tm
    S = R // n_per_set
    G = shift.shape[0]
    gmap = (lambda i: (i // tps, 0, 0)) if G > 1 else (lambda i: (0, 0, 0))
    wr = w_router.astype(f32)
    return pl.pallas_call(
        _router_kernel,
        grid=(R // tm,),
        in_specs=[pl.BlockSpec((tm, D_MODEL), lambda i: (i, 0)),
                  pl.BlockSpec((1, D_MODEL), lambda i: (0, 0)),
                  pl.BlockSpec((1, 1, D_MODEL), gmap),
                  pl.BlockSpec((1, 1, D_MODEL), gmap),
                  pl.BlockSpec((N_EXPERTS, D_MODEL), lambda i: (0, 0)),
                  pl.BlockSpec((D_MODEL, N_EXPERTS), lambda i: (0, 0))],
        out_specs=[pl.BlockSpec((tm, D_MODEL), lambda i: (i, 0)),
                   pl.BlockSpec((1, N_EXPERTS, tm), lambda i: (i // tps, 0, i % tps)),
                   pl.BlockSpec((tm, N_EXPERTS), lambda i: (i, 0))],
        out_shape=[jax.ShapeDtypeStruct((R, D_MODEL), bf16),
                   jax.ShapeDtypeStruct((S, N_EXPERTS, n_per_set), f32),
                   jax.ShapeDtypeStruct((R, N_EXPERTS), f32)],
        compiler_params=_cparams(("parallel",)),
        name="router",
    )(x2d, g.reshape(1, D_MODEL), shift.reshape(G, 1, D_MODEL), scale.reshape(G, 1, D_MODEL), wr.T, wr)


SEL_BLK = 256
SUB = LANES
SUBW = SUB + 8
UNSEL = -float(2 ** 30)


def _prefix_incl(mask_f, tri, T):
    outs = []
    off = jnp.zeros((mask_f.shape[0], 1), f32)
    for b in range(T // SEL_BLK):
        blk = mask_f[:, b * SEL_BLK:(b + 1) * SEL_BLK].astype(bf16)
        pre = jnp.dot(blk, tri, preferred_element_type=f32) + off
        outs.append(pre)
        off = pre[:, SEL_BLK - 1:SEL_BLK]
    return jnp.concatenate(outs, axis=1)


def _select_kernel(a_ref, tri_ref, cm_ref, posm_ref, cnt_ref, *, T, cap):
    aff = a_ref[0]
    bits = pltpu.bitcast(aff, i32)
    tri = tri_ref[...]

    def bit_step(i, thr):
        cand = thr | (1 << (30 - i))
        cnt = jnp.sum((bits >= cand).astype(f32), axis=1, keepdims=True)
        return jnp.where(cnt >= cap, cand, thr)

    thr = lax.fori_loop(0, 31, bit_step, jnp.zeros((N_EXPERTS, 1), i32))
    gt = bits > thr
    eq = bits == thr
    need = cap - jnp.sum(gt.astype(f32), axis=1, keepdims=True)
    eqf = eq.astype(f32)
    rank_eq = _prefix_incl(eqf, tri, T) - eqf
    sel = gt | (eq & (rank_eq < need))
    self_ = sel.astype(f32)
    pos = _prefix_incl(self_, tri, T) - self_
    posm_ref[0] = jnp.where(sel, pos, UNSEL)
    cnt_ref[0] = jnp.dot(self_.astype(bf16), cm_ref[...], preferred_element_type=f32).astype(i32)


def select_topk(aff, cap):
    S, E, T = aff.shape
    tri = jnp.asarray(np.triu(np.ones((SEL_BLK, SEL_BLK), np.float32)), bf16)
    cm = (jnp.arange(T)[:, None] < jnp.arange(LANES)[None, :] * SUB).astype(bf16)
    return pl.pallas_call(
        functools.partial(_select_kernel, T=T, cap=cap),
        grid=(S,),
        in_specs=[pl.BlockSpec((1, E, T), lambda s: (s, 0, 0)),
                  pl.BlockSpec((SEL_BLK, SEL_BLK), lambda s: (0, 0)),
                  pl.BlockSpec((T, LANES), lambda s: (0, 0))],
        out_specs=[pl.BlockSpec((1, E, T), lambda s: (s, 0, 0)),
                   pl.BlockSpec((1, E, LANES), lambda s: (s, 0, 0))],
        out_shape=[jax.ShapeDtypeStruct((S, E, T), f32), jax.ShapeDtypeStruct((S, E, LANES), i32)],
        compiler_params=_cparams(("parallel",)),
        name="select_topk",
    )(aff, tri, cm)


def _align8(v):
    return lax.shift_left(lax.shift_right_logical(v, 3), 3)


def _gather_kernel(cnt_ref, h_ref, pos_ref, o_ref, acc, *, TT, cap, tps, R):
    e = pl.program_id(0)
    tl = pl.program_id(1)

    @pl.when(tl == 0)
    def _():
        acc[...] = jnp.zeros_like(acc)

    st = tl // tps
    nsub = TT // SUB
    cbase = (st * N_EXPERTS + e) * LANES + (tl % tps) * nsub
    rid = lax.broadcasted_iota(i32, (SUBW, SUB), 0).astype(f32)
    for s in range(nsub):
        off8 = _align8(cnt_ref[cbase + s])
        pos = pos_ref[0, 0, :, s * SUB:(s + 1) * SUB]
        onehot = jnp.where(pos == rid + off8.astype(f32), 1.0, 0.0).astype(bf16)
        rows = jnp.dot(onehot, h_ref[s * SUB:(s + 1) * SUB, :], preferred_element_type=f32)
        r0 = pl.multiple_of(st * cap + off8, 8)
        acc[pl.ds(r0, SUBW), :] += rows

    @pl.when(tl == pl.num_programs(1) - 1)
    def _():
        o_ref[0] = acc[0:R, :].astype(o_ref.dtype)


def gather_rows(cnt, h, posm, cap, TT):
    S, E, T = posm.shape
    tps = T // TT
    R = S * cap
    gs = pltpu.PrefetchScalarGridSpec(
        num_scalar_prefetch=1,
        grid=(E, S * tps),
        in_specs=[pl.BlockSpec((TT, D_MODEL), lambda e, t, c: (t, 0)),
                  pl.BlockSpec((1, 1, 1, TT), lambda e, t, c: (t // tps, e, 0, t % tps))],
        out_specs=pl.BlockSpec((1, R, D_MODEL), lambda e, t, c: (e, 0, 0)),
        scratch_shapes=[pltpu.VMEM((R + SUBW, D_MODEL), f32)])
    return pl.pallas_call(
        functools.partial(_gather_kernel, TT=TT, cap=cap, tps=tps, R=R),
        grid_spec=gs,
        out_shape=jax.ShapeDtypeStruct((E, R, D_MODEL), bf16),
        compiler_params=_cparams(("parallel", "arbitrary")),
        name="moe_gather",
    )(cnt.reshape(-1), h, posm.reshape(S, E, 1, T))


EXPERT_TF = 256


def _ffn_kernel(*refs, n):
    xs = refs[:n]
    wg_ref, wu_ref, wd_ref = refs[n:n + 3]
    ys = refs[n + 3:]
    j = pl.program_id(1)
    wg = wg_ref[0, 0].astype(bf16)
    wu = wu_ref[0, 0].astype(bf16)
    wd = wd_ref[0, 0].astype(bf16)
    for x_ref, y_ref in zip(xs, ys):
        x = x_ref[0]
        a = jnp.dot(x, wg, preferred_element_type=f32)
        u = jnp.dot(x, wu, preferred_element_type=f32)
        d = jnp.dot((_silu(a) * u).astype(bf16), wd, preferred_element_type=f32)

        @pl.when(j == 0)
        def _(y_ref=y_ref, d=d):
            y_ref[0] = d

        @pl.when(j > 0)
        def _(y_ref=y_ref, d=d):
            y_ref[0] += d


def expert_ffn(xgs, layer, w_gate, w_up, w_down):
    E = xgs[0].shape[0]
    nf = D_FF_EXPERT // EXPERT_TF
    n = len(xgs)
    rowspec = lambda a: pl.BlockSpec((1, a.shape[1], D_MODEL), lambda e, j: (e, 0, 0))
    return pl.pallas_call(
        functools.partial(_ffn_kernel, n=n),
        grid=(E, nf),
        in_specs=[rowspec(a) for a in xgs] + [
            pl.BlockSpec((1, 1, D_MODEL, EXPERT_TF), lambda e, j: (layer, e, 0, j)),
            pl.BlockSpec((1, 1, D_MODEL, EXPERT_TF), lambda e, j: (layer, e, 0, j)),
            pl.BlockSpec((1, 1, EXPERT_TF, D_MODEL), lambda e, j: (layer, e, j, 0))],
        out_specs=[rowspec(a) for a in xgs],
        out_shape=[jax.ShapeDtypeStruct(a.shape, f32) for a in xgs],
        compiler_params=_cparams(("parallel", "arbitrary")),
        name="expert_ffn",
    )(*xgs, w_gate, w_up, w_down)


def _combine_kernel(cnt_ref, x_ref, pos_ref, am_ref, m_ref, y_ref, o_ref, *, TT, cap, R, W, ytot):
    st = pl.program_id(0)
    tl = pl.program_id(1)
    e = pl.program_id(2)

    @pl.when(e == 0)
    def _():
        o_ref[...] = x_ref[...]

    nsub = TT // SUB
    cbase = (st * N_EXPERTS + e) * LANES + tl * nsub
    rowbase = e * R + st * cap
    ws = jnp.minimum(rowbase + _align8(cnt_ref[cbase]), ytot - W)
    lane = lax.broadcasted_iota(i32, (TT, N_EXPERTS), 1)
    gcol = jnp.sum(jnp.where(lane == e, am_ref[...], 0.0), axis=1, keepdims=True)
    m5 = m_ref[0]
    rid = lax.broadcasted_iota(i32, (SUBW, SUB), 0).astype(f32)
    for s in range(nsub):
        rel = jnp.minimum(rowbase + _align8(cnt_ref[cbase + s]) - ws, W - SUBW)
        rel = pl.multiple_of(rel, 8)
        yw = y_ref[pl.ds(rel, SUBW), :]
        first = (ws + rel - rowbase).astype(f32)
        pos = pos_ref[0, 0, :, s * SUB:(s + 1) * SUB]
        onehot = jnp.where(pos == rid + first, 1.0, 0.0).astype(bf16)
        hi = yw.astype(bf16)
        r1 = yw - hi.astype(f32)
        mid = r1.astype(bf16)
        lo = (r1 - mid.astype(f32)).astype(bf16)
        picked = _tn(onehot, hi) + _tn(onehot, mid) + _tn(onehot, lo)
        sl = slice(s * SUB, (s + 1) * SUB)
        o_ref[sl, :] += m5 * (gcol[sl] * picked)


def combine(cnt, x2d, posm, aff_tm, mvec, y, cap, TT):
    S, E, T = posm.shape
    tps = T // TT
    R = S * cap
    W = TT + 16
    ytot = E * R
    nsub = TT // SUB
    G = mvec.shape[0]

    def ymap(st, tl, e, c):
        off8 = _align8(c[(st * E + e) * LANES + tl * nsub])
        return (pl.multiple_of(jnp.minimum(e * R + st * cap + off8, ytot - W), 8), 0)

    tok = lambda w: pl.BlockSpec((TT, w), lambda st, tl, e, c: (st * tps + tl, 0))
    mmap = (lambda st, tl, e, c: (st, 0, 0)) if G > 1 else (lambda st, tl, e, c: (0, 0, 0))
    gs = pltpu.PrefetchScalarGridSpec(
        num_scalar_prefetch=1,
        grid=(S, tps, E),
        in_specs=[tok(D_MODEL),
                  pl.BlockSpec((1, 1, 1, TT), lambda st, tl, e, c: (st, e, 0, tl)),
                  tok(N_EXPERTS),
                  pl.BlockSpec((1, 1, D_MODEL), mmap),
                  pl.BlockSpec((pl.Element(W), pl.Element(D_MODEL)), ymap)],
        out_specs=tok(D_MODEL))
    return pl.pallas_call(
        functools.partial(_combine_kernel, TT=TT, cap=cap, R=R, W=W, ytot=ytot),
        grid_spec=gs,
        out_shape=jax.ShapeDtypeStruct(x2d.shape, f32),
        compiler_params=_cparams(("parallel", "parallel", "arbitrary")),
        name="moe_combine",
    )(cnt.reshape(-1), x2d, posm.reshape(S, E, 1, T), aff_tm, mvec.reshape(G, 1, D_MODEL), y.reshape(ytot, D_MODEL))


def _rope_tables(n):
    half = NA_HD // 2
    q = half // 2
    inv = ROPE_THETA ** (-jnp.arange(q, dtype=f32) / q)
    pos = jnp.arange(n)
    ang_r = (pos // GRID_W).astype(f32)[:, None] * inv
    ang_c = (pos % GRID_W).astype(f32)[:, None] * inv
    zero = jnp.zeros_like(ang_r)
    c = jnp.concatenate([jnp.cos(ang_r)] * 2 + [jnp.cos(ang_c)] * 2, axis=1)
    s1 = jnp.concatenate([-jnp.sin(ang_r), zero, -jnp.sin(ang_c), zero], axis=1)
    s2 = jnp.concatenate([zero, jnp.sin(ang_r), zero, jnp.sin(ang_c)], axis=1)
    two = lambda a: jnp.concatenate([a, a], axis=1)
    return two(c), two(s1), two(s2)


def _mixing(hx, hc, need_ctx, B, N, NC, la, lc, w_in, q_gain, k_gain, table, rope, bd, conv_w, conv_b,
            spec, inv_norm, skip, filt_c, wa, wb, wc, wo, x2d, c2d, mx2, mc2):
    tile8 = lambda v: jnp.tile(v.reshape(1, NA_HD), (1, NA_HEADS))
    qg, kg = tile8(q_gain), tile8(k_gain)
    norm_aux = [("col", kg), ("const", bd)]
    rope_aux = [("row", rope[0]), ("row", rope[1]), ("row", rope[2])]
    lf_aux = lambda d: [("col", la[d:d + 1]), ("col", lc[d:d + 1])]
    tc = hc.shape[0]

    lff_c = project(hc, w_in, OFF_FF, 512, _epi_logforget, lf_aux(0), (f32,), tm=tc).reshape(B, NC, 512)
    lfb_c = project(hc, w_in, OFF_FB, 512, _epi_logforget, lf_aux(1), (f32,), tm=tc).reshape(B, NC, 512)
    i_c = project(hc, w_in, OFF_I, 512, _epi_raw, tm=tc).reshape(B, NC, 512)
    k_c = project(hc, w_in, OFF_NK, 512, _epi_norm, norm_aux, tm=tc).reshape(B, NC, 512)
    v_c = project(hc, w_in, OFF_NV, 512, _epi_raw, tm=tc).reshape(B, NC, 512)
    if need_ctx:
        q_c = project(hc, w_in, OFF_HQ, 512, _epi_silu, tm=tc).reshape(B, NC, 512)
    else:
        q_c = jnp.zeros((B, NC, 512), bf16)
    s0 = jnp.zeros((B, HG_HEADS, LANES, LANES), f32)
    oc_f, s_f = hgrn_scan(lff_c, i_c, q_c, s0, False)

    lff_x = project(hx, w_in, OFF_FF, 512, _epi_logforget, lf_aux(0), (f32,)).reshape(B, N, 512)
    lfb_x = project(hx, w_in, OFF_FB, 512, _epi_logforget, lf_aux(1), (f32,)).reshape(B, N, 512)
    i_x = project(hx, w_in, OFF_I, 512, _epi_raw).reshape(B, N, 512)
    q_x = project(hx, w_in, OFF_HQ, 512, _epi_silu).reshape(B, N, 512)
    g_x = project(hx, w_in, OFF_HG, 512, _epi_silu).reshape(B, N, 512)
    k_x = project(hx, w_in, OFF_NK, 512, _epi_norm_rope, norm_aux + rope_aux, rows_per_seq=N).reshape(B, N, 512)
    v_x = project(hx, w_in, OFF_NV, 512, _epi_raw).reshape(B, N, 512)
    qn_x, qr_x = project(hx, w_in, OFF_NQ, 512, _epi_norm_both, [("col", qg), ("const", bd)] + rope_aux,
                         (bf16, bf16), rows_per_seq=N)
    p_x = project(hx, w_in, OFF_HY, 3 * HY_W, _epi_raw, out_dtypes=(f32,)).reshape(B, N, 3 * HY_W)
    gates_x = project(hx, w_in, OFF_GATE, 3 * D_MODEL, _epi_sigmoid)

    ox_f, _ = hgrn_scan(lff_x, i_x, q_x, s_f, False)
    if need_ctx:
        g_c = project(hc, w_in, OFF_HG, 512, _epi_silu, tm=tc).reshape(B, NC, 512)
        ra_c, s_b = hgrn_scan(lfb_c, i_c, q_c, s0, True, oc_f, g_c)
    else:
        _, s_b = hgrn_scan(lfb_c, i_c, q_c, s0, True)
    ra_x, _ = hgrn_scan(lfb_x, i_x, q_x, s_b, True, ox_f, g_x)

    nb_x = neighbourhood_attention(qr_x.reshape(B, N, 512), qn_x.reshape(B, N, 512), k_x, v_x, k_c, v_c, table)

    z_x, x0_x = hyena_pre(p_x, conv_w, conv_b)
    hy_x = hyena_conv(z_x, x0_x, spec[0], spec[1], inv_norm, skip, spec[2])

    x_new = merge(ra_x.reshape(B * N, 512), nb_x.reshape(B * N, 512), hy_x.reshape(B * N, 512), gates_x,
                  wa, wb, wc, wo, x2d, mx2, N)
    if not need_ctx:
        return x_new, None

    qn_c = project(hc, w_in, OFF_NQ, 512, _epi_norm, [("col", qg), ("const", bd)], tm=tc).reshape(B, NC, 512)
    nb_c = context_attention(qn_c, k_c, v_c)
    p_c = project(hc, w_in, OFF_HY, 3 * HY_W, _epi_raw, out_dtypes=(f32,), tm=tc).reshape(B, NC, 3 * HY_W)
    hy_c = hyena_ctx(p_c, conv_w, conv_b, filt_c[0], filt_c[1], skip)
    gates_c = project(hc, w_in, OFF_GATE, 3 * D_MODEL, _epi_sigmoid, tm=tc)
    c_new = merge(ra_c.reshape(B * NC, 512), nb_c.reshape(B * NC, 512), hy_c.reshape(B * NC, 512), gates_c,
                  wa, wb, wc, wo, c2d, mc2, B * NC)
    return x_new, c_new


def kernel(x, c, ctx, c_ctx, w_mod, b_mod, norm_mix, norm_ffn, w_in, hg_lb, na_q_gain, na_k_gain, na_rpb,
           hy_conv_w, hy_conv_b, hy_pe_w1, hy_pe_b1, hy_pe_freq1, hy_pe_w2, hy_pe_b2, hy_pe_freq2, hy_pe_w3,
           hy_skip, w_branch_a, w_branch_b, w_branch_c, w_out, w_router, w_e_gate, w_e_up, w_e_down):
    B, N, D = x.shape
    NC = ctx.shape[1]
    E = N_EXPERTS
    cap_x = EC_CAP_FACTOR * N // E
    cap_c = EC_CAP_FACTOR * NC // E

    lb = jnp.cumsum(jax.nn.softmax(hg_lb.astype(f32), axis=0), axis=0)
    lb = lb - lb[:1]
    la_all, lc_all = jnp.log(lb), jnp.log1p(-lb)

    s8 = jnp.zeros((8, D), f32).at[:B].set(c).at[B].set(c_ctx)
    rope = _rope_tables(N)
    bd = jnp.asarray(np.kron(np.eye(NA_HEADS), np.full((NA_HD, NA_HD), 1.0 / NA_HD)).astype(np.float32), bf16)
    dcx = _dft_consts(N)

    x2d = x.reshape(B * N, D)
    c2d = ctx.reshape(B * NC, D)
    for l in range(DEPTH):
        need_ctx = l < DEPTH - 1
        mv = modvec(s8, w_mod[l], b_mod[l])
        mx = [mv[:B, k * D:(k + 1) * D] for k in range(6)]
        mc = [mv[B:B + 1, k * D:(k + 1) * D] for k in range(6)]
        w_in_l = w_in[l].astype(bf16)
        hx = modulate(x2d, norm_mix[l], mx[0], mx[1], N, bf16)
        hc = modulate(c2d, norm_mix[l], mc[0], mc[1], B * NC, bf16)

        filt = (hy_pe_w1[l], hy_pe_b1[l], hy_pe_freq1[l], hy_pe_w2[l], hy_pe_b2[l], hy_pe_freq2[l], hy_pe_w3[l])
        h_x, nrm_x = hyena_filter(N, *filt)
        sre, sim = hyena_spectrum(h_x, dcx)
        skip = hy_skip[l].reshape(1, HY_W)
        filt_c = None
        if need_ctx:
            h_c, nrm_c = hyena_filter(NC, *filt)
            filt_c = (h_c, 1.0 / nrm_c)

        x2d, c_new = _mixing(
            hx, hc, need_ctx, B, N, NC, la_all[l], lc_all[l], w_in_l, na_q_gain[l], na_k_gain[l],
            _na_bias_table(na_rpb[l]), rope, bd, hy_conv_w[l], hy_conv_b[l], (sre, sim, dcx), 1.0 / nrm_x, skip,
            filt_c, w_branch_a[l].astype(bf16), w_branch_b[l].astype(bf16), w_branch_c[l].astype(bf16),
            w_out[l].astype(bf16), x2d, c2d, mx[2], mc[2])

        h2, aff_t, aff_m = router(x2d, norm_ffn[l], mx[3], mx[4], w_router[l], N)
        posm, cnt = select_topk(aff_t, cap_x)
        xgs = [gather_rows(cnt, h2, posm, cap_x, 1024)]
        if need_ctx:
            c2d = c_new
            hc2, aff_tc, aff_mc = router(c2d, norm_ffn[l], mc[3], mc[4], w_router[l], NC)
            posm_c, cnt_c = select_topk(aff_tc, cap_c)
            xgs.append(gather_rows(cnt_c, hc2, posm_c, cap_c, NC))
        ys = expert_ffn(xgs, l, w_e_gate, w_e_up, w_e_down)
        x2d = combine(cnt, x2d, posm, aff_m, mx[5], ys[0], cap_x, 512)
        if need_ctx:
            c2d = combine(cnt_c, c2d, posm_c, aff_mc, mc[5], ys[1], cap_c, NC)
    return x2d.reshape(B, N, D)
```

```python
import functools
import math

import numpy as np
import jax
import jax.numpy as jnp
from jax import lax
from jax.experimental import pallas as pl
from jax.experimental.pallas import tpu as pltpu

f32 = jnp.float32
bf16 = jnp.bfloat16
i32 = jnp.int32
HI = lax.Precision.HIGHEST

D_MODEL = 1024
DEPTH = 2
GRID_W = 64
EPS = 1e-6
HG_HEADS = 4
HG_W = 512
HG_CHUNK = 64
NA_HEADS = 8
NA_HD = 64
NA_W = 512
NA_WIN_R = 8
NA_WIN_C = 16
ROPE_THETA = 10000.0
HY_W = 512
HY_BANDS = 16
HY_PE_DIM = 1 + 2 * HY_BANDS
HY_FILT_HID = 64
HY_FAST_DECAY = 0.3
HY_SLOW_DECAY = 1.5
HY_TARGET = 1e-2
OFF_FF = 0
OFF_FB = 512
OFF_I = 1024
OFF_NK = 1536
OFF_NV = 2048
OFF_HQ = 2560
OFF_NQ = 3072
OFF_HG = 3584
OFF_HY = 4096
OFF_GATE = 5632
IN_COLS = 8704
N_EXPERTS = 16
EC_CAP_FACTOR = 2
D_FF_EXPERT = 2816

LANES = 128
NEG_BIG = -1e30
VMEM_LIMIT = 56 * 1024 * 1024


def _cparams(sem, vmem=VMEM_LIMIT):
    return pltpu.CompilerParams(dimension_semantics=sem, vmem_limit_bytes=vmem)


def _nt(a, b, precision=None):
    return lax.dot_general(a, b, (((1,), (1,)), ((), ())), precision=precision, preferred_element_type=f32)


def _tn(a, b, precision=None):
    return lax.dot_general(a, b, (((0,), (0,)), ((), ())), precision=precision, preferred_element_type=f32)


def _silu(x):
    return x * jax.nn.sigmoid(x)


def _split(x):
    hi = x.astype(bf16)
    return hi, (x - hi.astype(f32)).astype(bf16)


def _modvec_kernel(s_ref, w_ref, b_ref, o_ref):
    s = _silu(s_ref[...])
    o_ref[...] = jnp.dot(s, w_ref[...], precision=HI, preferred_element_type=f32) + b_ref[...]


def modvec(s8, w, b):
    n = w.shape[1]
    tn = 1024
    return pl.pallas_call(
        _modvec_kernel,
        grid=(n // tn,),
        in_specs=[pl.BlockSpec((8, D_MODEL), lambda j: (0, 0)),
                  pl.BlockSpec((D_MODEL, tn), lambda j: (0, j)),
                  pl.BlockSpec((1, tn), lambda j: (0, j))],
        out_specs=pl.BlockSpec((8, tn), lambda j: (0, j)),
        out_shape=jax.ShapeDtypeStruct((8, n), f32),
        compiler_params=_cparams(("parallel",)),
        name="modvec",
    )(s8, w, b.reshape(1, n))


def _modulate_kernel(x_ref, g_ref, sh_ref, sc_ref, o_ref):
    x = x_ref[...]
    ms = jnp.mean(x * x, axis=-1, keepdims=True)
    y = x * lax.rsqrt(ms + EPS)
    o_ref[...] = (y * g_ref[...] * (1.0 + sc_ref[0]) + sh_ref[0]).astype(o_ref.dtype)


def modulate(x2d, g, shift, scale, rows_per_group, out_dtype):
    R = x2d.shape[0]
    tm = 512
    tpg = rows_per_group // tm
    G = shift.shape[0]
    return pl.pallas_call(
        _modulate_kernel,
        grid=(R // tm,),
        in_specs=[pl.BlockSpec((tm, D_MODEL), lambda i: (i, 0)),
                  pl.BlockSpec((1, D_MODEL), lambda i: (0, 0)),
                  pl.BlockSpec((1, 1, D_MODEL), lambda i: (i // tpg, 0, 0)),
                  pl.BlockSpec((1, 1, D_MODEL), lambda i: (i // tpg, 0, 0))],
        out_specs=pl.BlockSpec((tm, D_MODEL), lambda i: (i, 0)),
        out_shape=jax.ShapeDtypeStruct((R, D_MODEL), out_dtype),
        compiler_params=_cparams(("parallel",)),
        name="modulate",
    )(x2d, g.reshape(1, D_MODEL), shift.reshape(G, 1, D_MODEL), scale.reshape(G, 1, D_MODEL))


def _log_sigmoid(z):
    return jnp.minimum(z, 0.0) - jnp.log1p(jnp.exp(-jnp.abs(z)))


def _epi_raw(acc, o_ref):
    o_ref[...] = acc.astype(o_ref.dtype)


def _epi_silu(acc, o_ref):
    o_ref[...] = _silu(acc).astype(o_ref.dtype)


def _epi_sigmoid(acc, o_ref):
    o_ref[...] = jax.nn.sigmoid(acc).astype(o_ref.dtype)


def _epi_logforget(acc, la_ref, lc_ref, o_ref):
    la = la_ref[...]
    c = lc_ref[...] + _log_sigmoid(acc)
    o_ref[...] = jnp.maximum(la, c) + jnp.log1p(jnp.exp(-jnp.abs(la - c)))


def _head_rms(acc, gain_ref, bd_ref):
    hi, lo = _split(acc * acc)
    ms = jnp.dot(hi, bd_ref[...], preferred_element_type=f32) + jnp.dot(lo, bd_ref[...], preferred_element_type=f32)
    return acc * lax.rsqrt(ms + EPS) * gain_ref[...]


def _rope(y, c_ref, s1_ref, s2_ref):
    reps = y.shape[1] // LANES
    c = jnp.concatenate([c_ref[...]] * reps, axis=1)
    s1 = jnp.concatenate([s1_ref[...]] * reps, axis=1)
    s2 = jnp.concatenate([s2_ref[...]] * reps, axis=1)
    w = y.shape[1]
    return y * c + pltpu.roll(y, w - 16, axis=1) * s1 + pltpu.roll(y, 16, axis=1) * s2


def _epi_norm(acc, gain_ref, bd_ref, o_ref):
    o_ref[...] = _head_rms(acc, gain_ref, bd_ref).astype(o_ref.dtype)


def _epi_norm_rope(acc, gain_ref, bd_ref, c_ref, s1_ref, s2_ref, o_ref):
    y = _head_rms(acc, gain_ref, bd_ref)
    o_ref[...] = _rope(y, c_ref, s1_ref, s2_ref).astype(o_ref.dtype)


def _epi_norm_both(acc, gain_ref, bd_ref, c_ref, s1_ref, s2_ref, on_ref, or_ref):
    y = _head_rms(acc, gain_ref, bd_ref)
    on_ref[...] = y.astype(on_ref.dtype)
    or_ref[...] = _rope(y, c_ref, s1_ref, s2_ref).astype(or_ref.dtype)


def _proj_kernel(h_ref, w_ref, *rest, epi):
    acc = jnp.dot(h_ref[...], w_ref[...], preferred_element_type=f32)
    epi(acc, *rest)


def project(h, w, c0, width, epi, aux=(), out_dtypes=(bf16,), tm=1024, rows_per_seq=None):
    R = h.shape[0]
    tn = 512
    nj = width // tn
    cb = c0 // tn
    in_specs = [pl.BlockSpec((tm, D_MODEL), lambda i, j: (i, 0)),
                pl.BlockSpec((D_MODEL, tn), lambda i, j: (0, cb + j))]
    args = [h, w]
    for kind, arr in aux:
        if kind == "col":
            in_specs.append(pl.BlockSpec((1, tn), lambda i, j: (0, j)))
        elif kind == "const":
            in_specs.append(pl.BlockSpec(arr.shape, lambda i, j: (0, 0)))
        else:
            tps = rows_per_seq // tm
            in_specs.append(pl.BlockSpec((tm, LANES), lambda i, j: (i % tps, 0)))
        args.append(arr)
    out_specs = [pl.BlockSpec((tm, tn), lambda i, j: (i, j)) for _ in out_dtypes]
    out_shape = [jax.ShapeDtypeStruct((R, width), dt) for dt in out_dtypes]
    res = pl.pallas_call(
        functools.partial(_proj_kernel, epi=epi),
        grid=(R // tm, nj),
        in_specs=in_specs,
        out_specs=out_specs,
        out_shape=out_shape,
        compiler_params=_cparams(("parallel", "parallel")),
        name="proj_" + epi.__name__[5:],
    )(*args)
    return res[0] if len(res) == 1 else res


def _hgrn_tmatrix(C, reverse):
    L = int(round(math.log2(C)))
    t = np.arange(C)
    tau = (C - 1 - t) if reverse else t
    tt, uu = tau[:, None], tau[None, :]
    T = np.zeros((2 + L, C, C), np.float32)
    T[0] = uu <= tt
    T[1] = uu > tt
    for l in range(L):
        same = (tt >> (l + 1)) == (uu >> (l + 1))
        tr = ((tt >> l) & 1) == 1
        ur = ((uu >> l) & 1) == 1
        T[2 + l] = same & ((tr & ur & (uu <= tt)) | (~tr & ~ur & (uu > tt)))
    return T.reshape((2 + L) * C, C)


def _hgrn_chunk(lf, v_all, q_all, tm, s_scr, C, reverse):
    L = int(round(math.log2(C)))
    W = lf.shape[1]
    hi, lo = _split(lf)
    r = jnp.dot(tm, jnp.concatenate([hi, lo], axis=1), preferred_element_type=f32)
    E = r[:, :W] + r[:, W:]
    kc_all = 1.0 - jnp.exp(lf)

    row = lax.broadcasted_iota(i32, (C, LANES), 0)
    tau = (C - 1 - row) if reverse else row
    ti = lax.broadcasted_iota(i32, (C, C), 0)
    si = lax.broadcasted_iota(i32, (C, C), 1)
    if reverse:
        ti, si = C - 1 - ti, C - 1 - si
    end_row = 0 if reverse else C - 1

    outs = []
    for h in range(HG_HEADS):
        sl = slice(LANES * h, LANES * (h + 1))
        qh, kch, vh = q_all[:, sl], kc_all[:, sl], v_all[:, sl]
        b = E[0:C, sl]
        suf = E[C:2 * C, sl]
        st = s_scr[h]
        qb = (qh * jnp.exp(b)).astype(bf16)
        o = _nt(qb, st.astype(bf16))
        att = jnp.where(ti == si, _nt(qh.astype(bf16), kch.astype(bf16)), 0.0)
        for l in range(L):
            x = jnp.exp(E[(2 + l) * C:(3 + l) * C, sl])
            later = ((tau >> l) & 1) == 1
            ql = jnp.where(later, qh * x, 0.0).astype(bf16)
            kl = jnp.where(later, 0.0, kch * x).astype(bf16)
            att = att + jnp.where((ti >> (l + 1)) == (si >> (l + 1)), _nt(ql, kl), 0.0)
        o = o + jnp.dot(att.astype(bf16), vh, preferred_element_type=f32)
        kd = (kch * jnp.exp(suf)).astype(bf16)
        bend = b[end_row:end_row + 1, :]
        s_scr[h] = st * jnp.exp(bend) + _tn(vh, kd)
        outs.append(o)
    return jnp.concatenate(outs, axis=1)


def _hgrn_kernel(lff_ref, lfb_ref, vf_ref, vb_ref, qf_ref, qb_ref, s0f_ref, s0b_ref, tf_ref, tb_ref,
                 of_ref, ob_ref, sff_ref, sfb_ref, s_scr, *, C, B):
    c = pl.program_id(0)

    @pl.when(c == 0)
    def _():
        s_scr[0] = s0f_ref[...]
        s_scr[1] = s0b_ref[...]

    for b in range(B):
        of_ref[b] = _hgrn_chunk(lff_ref[b], vf_ref[b], qf_ref[b].astype(f32), tf_ref[...], s_scr.at[0, b], C, False)
        ob_ref[b] = _hgrn_chunk(lfb_ref[b], vb_ref[b], qb_ref[b].astype(f32), tb_ref[...], s_scr.at[1, b], C, True)

    @pl.when(c == pl.num_programs(0) - 1)
    def _():
        sff_ref[...] = s_scr[0]
        sfb_ref[...] = s_scr[1]


def hgrn_bidir(lf_f, lf_b, v, q, s0_f, s0_b):
    B, N, W = lf_f.shape
    C = HG_CHUNK
    nch = N // C
    tf = jnp.asarray(_hgrn_tmatrix(C, False), bf16)
    tb = jnp.asarray(_hgrn_tmatrix(C, True), bf16)
    fw = pl.BlockSpec((B, C, W), lambda c: (0, c, 0))
    bw = pl.BlockSpec((B, C, W), lambda c: (0, nch - 1 - c, 0))
    st = pl.BlockSpec((B, HG_HEADS, LANES, LANES), lambda c: (0, 0, 0, 0))
    tsp = pl.BlockSpec(tf.shape, lambda c: (0, 0))
    seq = jax.ShapeDtypeStruct((B, N, W), f32)
    sts = jax.ShapeDtypeStruct((B, HG_HEADS, LANES, LANES), f32)
    return pl.pallas_call(
        functools.partial(_hgrn_kernel, C=C, B=B),
        grid=(nch,),
        in_specs=[fw, bw, fw, bw, fw, bw, st, st, tsp, tsp],
        out_specs=[fw, bw, st, st],
        out_shape=[seq, seq, sts, sts],
        scratch_shapes=[pltpu.VMEM((2, B, HG_HEADS, LANES, LANES), f32)],
        compiler_params=_cparams(("arbitrary",)),
        name="hgrn",
    )(lf_f, lf_b, v, v, q, q, s0_f, s0_b, tf, tb)


def _na_kernel(qr_ref, qn_ref, k_ref, v_ref, kc_ref, vc_ref, tab_ref, o_ref, s_scr, p_scr, *, rows_per_step, n_rows):
    g = pl.program_id(2)
    scale = NA_HD ** -0.5
    lane = lax.broadcasted_iota(i32, (GRID_W, LANES), 1)
    kcx = kc_ref[0]
    vcx = vc_ref[0]
    win = NA_WIN_R * GRID_W
    ctx_len = kcx.shape[0]

    starts = []
    for i in range(rows_per_step):
        r = g * rows_per_step + i
        rs = jnp.clip(r - NA_WIN_R // 2, 0, n_rows - NA_WIN_R)
        off = rs - r + (NA_WIN_R - 1)
        start = pl.multiple_of(rs * GRID_W, GRID_W)
        starts.append(start)
        kw = k_ref[0, pl.ds(start, win), :]
        qr = qr_ref[0, i * GRID_W:(i + 1) * GRID_W, :]
        qn = qn_ref[0, i * GRID_W:(i + 1) * GRID_W, :]
        for hh in range(2):
            m = (lane >= NA_HD * hh) & (lane < NA_HD * (hh + 1))
            qrm = jnp.where(m, qr, jnp.zeros_like(qr))
            qnm = jnp.where(m, qn, jnp.zeros_like(qn))
            row0 = (2 * i + hh) * GRID_W
            s_scr[row0:row0 + GRID_W, 0:win] = _nt(qrm, kw) * scale + tab_ref[hh, off]
            s_scr[row0:row0 + GRID_W, win:win + ctx_len] = _nt(qnm, kcx) * scale

    def softmax_rows(c, carry):
        r0 = pl.multiple_of(c * LANES, LANES)
        s = s_scr[pl.ds(r0, LANES), :]
        p = jnp.exp(s - jnp.max(s, axis=-1, keepdims=True))
        inv = 1.0 / jnp.sum(p, axis=-1, keepdims=True)
        p_scr[pl.ds(r0, LANES), :] = (p * inv).astype(bf16)
        return carry

    lax.fori_loop(0, 2 * rows_per_step * GRID_W // LANES, softmax_rows, 0, unroll=2)

    for i in range(rows_per_step):
        vw = v_ref[0, pl.ds(starts[i], win), :]
        res = []
        for hh in range(2):
            row0 = (2 * i + hh) * GRID_W
            p = p_scr[row0:row0 + GRID_W, :]
            res.append(jnp.dot(p[:, :win], vw, preferred_element_type=f32)
                       + jnp.dot(p[:, win:], vcx, preferred_element_type=f32))
        o_ref[0, i * GRID_W:(i + 1) * GRID_W, :] = jnp.where(lane < NA_HD, res[0], res[1]).astype(o_ref.dtype)


def _na_bias_table(rpb):
    col = jnp.arange(GRID_W)
    cs = jnp.clip(col - NA_WIN_C // 2, 0, GRID_W - NA_WIN_C)
    kc = jnp.arange(GRID_W)
    valid = (kc[None, :] >= cs[:, None]) & (kc[None, :] < cs[:, None] + NA_WIN_C)
    dc = jnp.clip(kc[None, :] - col[:, None] + (NA_WIN_C - 1), 0, 2 * NA_WIN_C - 2)
    bc = jnp.where(valid[None, None], rpb[:, :, dc], NEG_BIG)
    t2 = jnp.stack([bc[:, o:o + NA_WIN_R] for o in range(NA_WIN_R)], axis=1)
    t2 = t2.transpose(0, 1, 3, 2, 4)
    return t2.reshape(NA_HEADS, NA_WIN_R, GRID_W, NA_WIN_R * GRID_W).astype(f32)


def neighbourhood_attention(q_rot, qn, k_rot, v, kc, vc, table):
    B, N, W = q_rot.shape
    n_rows = N // GRID_W
    rps = 8
    ctx_len = kc.shape[1]
    pairs = W // LANES
    keys = NA_WIN_R * GRID_W + ctx_len
    return pl.pallas_call(
        functools.partial(_na_kernel, rows_per_step=rps, n_rows=n_rows),
        grid=(B, pairs, n_rows // rps),
        in_specs=[pl.BlockSpec((1, rps * GRID_W, LANES), lambda b, p, g: (b, g, p)),
                  pl.BlockSpec((1, rps * GRID_W, LANES), lambda b, p, g: (b, g, p)),
                  pl.BlockSpec((1, N, LANES), lambda b, p, g: (b, 0, p)),
                  pl.BlockSpec((1, N, LANES), lambda b, p, g: (b, 0, p)),
                  pl.BlockSpec((1, ctx_len, LANES), lambda b, p, g: (b, 0, p)),
                  pl.BlockSpec((1, ctx_len, LANES), lambda b, p, g: (b, 0, p)),
                  pl.BlockSpec((2, NA_WIN_R, GRID_W, NA_WIN_R * GRID_W), lambda b, p, g: (p, 0, 0, 0))],
        out_specs=pl.BlockSpec((1, rps * GRID_W, LANES), lambda b, p, g: (b, g, p)),
        out_shape=jax.ShapeDtypeStruct((B, N, W), bf16),
        scratch_shapes=[pltpu.VMEM((2 * rps * GRID_W, keys), f32), pltpu.VMEM((2 * rps * GRID_W, keys), bf16)],
        compiler_params=_cparams(("parallel", "parallel", "arbitrary")),
        name="natten",
    )(q_rot, qn, k_rot, v, kc, vc, table)


def _ctx_attn_kernel(q_ref, k_ref, v_ref, o_ref):
    scale = NA_HD ** -0.5
    q = q_ref[0]
    k = k_ref[0]
    v = v_ref[0]
    lane = lax.broadcasted_iota(i32, q.shape, 1)
    res = []
    for hh in range(2):
        m = (lane >= NA_HD * hh) & (lane < NA_HD * (hh + 1))
        s = _nt(jnp.where(m, q, jnp.zeros_like(q)), k) * scale
        p = jnp.exp(s - jnp.max(s, axis=-1, keepdims=True))
        p = p / jnp.sum(p, axis=-1, keepdims=True)
        res.append(jnp.dot(p.astype(bf16), v, preferred_element_type=f32))
    o_ref[0] = jnp.where(lane < NA_HD, res[0], res[1]).astype(o_ref.dtype)


def context_attention(q, k, v):
    B, N, W = q.shape
    spec = pl.BlockSpec((1, N, LANES), lambda b, p: (b, 0, p))
    return pl.pallas_call(
        _ctx_attn_kernel,
        grid=(B, W // LANES),
        in_specs=[spec, spec, spec],
        out_specs=spec,
        out_shape=jax.ShapeDtypeStruct((B, N, W), bf16),
        compiler_params=_cparams(("parallel", "parallel")),
        name="ctx_attn",
    )(q, k, v)


def _filter_kernel(z_ref, w1_ref, b1_ref, f1_ref, w2_ref, b2_ref, f2_ref, w3_ref, dl_ref, k_ref, nrm_ref, *, tm, n):
    i = pl.program_id(0)
    z = z_ref[...]
    a = jnp.sin(f1_ref[...] * (jnp.dot(z, w1_ref[...], precision=HI, preferred_element_type=f32) + b1_ref[...]))
    a = jnp.sin(f2_ref[...] * (jnp.dot(a, w2_ref[...], precision=HI, preferred_element_type=f32) + b2_ref[...]))
    h = jnp.dot(a, w3_ref[...], precision=HI, preferred_element_type=f32)
    h = h * jnp.exp(-z[:, 0:1] * dl_ref[...])
    row = lax.broadcasted_iota(i32, (tm, HY_W), 0) + i * tm
    k = jnp.where(row < n, h[:, :HY_W], jnp.where(row == n, 0.0, h[:, HY_W:]))
    k_ref[...] = k
    part = jnp.sum(jnp.abs(k), axis=0, keepdims=True)

    @pl.when(i == 0)
    def _():
        nrm_ref[...] = part

    @pl.when(i > 0)
    def _():
        nrm_ref[...] = nrm_ref[...] + part


def hyena_filter(n, w1, b1, fr1, w2, b2, fr2, w3):
    t = jnp.linspace(0.0, 1.0, n, dtype=f32)[:, None]
    w = 2 * math.pi * jnp.arange(n, dtype=f32)[:, None] / n
    fb = jnp.linspace(1e-4, HY_BANDS - 1, HY_BANDS, dtype=f32)[None]
    z = jnp.concatenate([t, jnp.cos(fb * w), -jnp.sin(fb * w)], axis=-1)
    z = jnp.concatenate([z, jnp.zeros((1, HY_PE_DIM), f32), z[:0:-1]], axis=0)
    z = jnp.pad(z, ((0, 0), (0, LANES - HY_PE_DIM)))
    w1p = jnp.pad(w1.astype(f32), ((0, LANES - HY_PE_DIM), (0, 0)))
    deltas = jnp.abs(jnp.linspace(math.log(HY_TARGET) / HY_SLOW_DECAY, math.log(HY_TARGET) / HY_FAST_DECAY,
                                  2 * HY_W, dtype=f32))[None]
    tm = min(2 * n, 512)
    hid = HY_FILT_HID
    full = lambda shape: pl.BlockSpec(shape, lambda i: (0, 0))
    return pl.pallas_call(
        functools.partial(_filter_kernel, tm=tm, n=n),
        grid=(2 * n // tm,),
        in_specs=[pl.BlockSpec((tm, LANES), lambda i: (i, 0)),
                  full((LANES, hid)), full((1, hid)), full((1, hid)),
                  full((hid, hid)), full((1, hid)), full((1, hid)),
                  full((hid, 2 * HY_W)), full((1, 2 * HY_W))],
        out_specs=[pl.BlockSpec((tm, HY_W), lambda i: (i, 0)), pl.BlockSpec((1, HY_W), lambda i: (0, 0))],
        out_shape=[jax.ShapeDtypeStruct((2 * n, HY_W), f32), jax.ShapeDtypeStruct((1, HY_W), f32)],
        compiler_params=_cparams(("arbitrary",)),
        name="hyena_filter",
    )(z, w1p, b1.reshape(1, hid), fr1.reshape(1, hid), w2.astype(f32), b2.reshape(1, hid), fr2.reshape(1, hid),
      w3.astype(f32), deltas)


def _conv3(u, w_ref, b_ref):
    n = u.shape[0]
    row = lax.broadcasted_iota(i32, u.shape, 0)
    prev = jnp.where(row == 0, 0.0, pltpu.roll(u, 1, axis=0))
    nxt = jnp.where(row == n - 1, 0.0, pltpu.roll(u, n - 1, axis=0))
    return prev * w_ref[0:1, :] + u * w_ref[1:2, :] + nxt * w_ref[2:3, :] + b_ref[...]


def _hyena_pre_kernel(p0_ref, p1_ref, p2_ref, w0_ref, w1_ref, w2_ref, b0_ref, b1_ref, b2_ref, z_ref, x0_ref):
    x0_ref[0] = _conv3(p0_ref[0], w0_ref, b0_ref).astype(x0_ref.dtype)
    z_ref[0] = _conv3(p1_ref[0], w1_ref, b1_ref) * _conv3(p2_ref[0], w2_ref, b2_ref)


def hyena_pre(p, conv_w, conv_b):
    B, N, _ = p.shape
    nb = HY_W // LANES
    conv_b = conv_b.reshape(1, 3 * HY_W)
    pspec = lambda g: pl.BlockSpec((1, N, LANES), lambda b, c: (b, 0, g * nb + c))
    wspec = lambda g: pl.BlockSpec((3, LANES), lambda b, c: (0, g * nb + c))
    bspec = lambda g: pl.BlockSpec((1, LANES), lambda b, c: (0, g * nb + c))
    ospec = pl.BlockSpec((1, N, LANES), lambda b, c: (b, 0, c))
    return pl.pallas_call(
        _hyena_pre_kernel,
        grid=(B, nb),
        in_specs=[pspec(0), pspec(1), pspec(2), wspec(0), wspec(1), wspec(2), bspec(0), bspec(1), bspec(2)],
        out_specs=[ospec, ospec],
        out_shape=[jax.ShapeDtypeStruct((B, N, HY_W), f32), jax.ShapeDtypeStruct((B, N, HY_W), bf16)],
        compiler_params=_cparams(("parallel", "parallel")),
        name="hyena_pre",
    )(p, p, p, conv_w, conv_w, conv_w, conv_b, conv_b, conv_b)


def _hl(a):
    a32 = jnp.asarray(a.astype(np.float32))
    hi = a32.astype(bf16)
    lo = (a32 - hi.astype(f32)).astype(bf16)
    return jnp.concatenate([hi, lo], axis=-2)


def _dot3(a_hl, m, x):
    xh, xl = _split(x)
    r = jnp.dot(a_hl, xh, preferred_element_type=f32)
    return r[:m] + r[m:] + jnp.dot(a_hl[:m], xl, preferred_element_type=f32)


def _dft_consts(n):
    N = 2 * n
    na = N // LANES
    t1n = na // 2
    k1n = na // 2 + 1
    k1p = -(-k1n // 8) * 8
    k1 = np.arange(k1n)
    t1 = np.arange(t1n)
    th = 2 * np.pi * ((t1[None, :] * k1[:, None]) % na) / na
    f1c = np.zeros((2 * k1p, t1n))
    f1c[:k1n] = np.cos(th)
    f1c[k1p:k1p + k1n] = -np.sin(th)
    thf = 2 * np.pi * ((np.arange(na)[None, :] * k1[:, None]) % na) / na
    f1f = np.zeros((2 * k1p, na))
    f1f[:k1n] = np.cos(thf)
    f1f[k1p:k1p + k1n] = -np.sin(thf)
    k2 = np.arange(LANES)
    t2 = np.arange(LANES)
    m = (t2[None, None, :] * (k1[:, None, None] + na * k2[None, :, None])) % N
    ph = 2 * np.pi * m / N
    g = np.concatenate([np.cos(ph), -np.sin(ph)], axis=1)
    pht = ph.transpose(0, 2, 1)
    gi = np.concatenate([np.cos(pht), np.sin(pht)], axis=1)
    wk = np.where((k1 == 0) | (k1 == na // 2), 1.0, 2.0) / N
    f1i = np.zeros((t1n, 2 * k1p))
    f1i[:, :k1n] = np.cos(th.T) * wk[None, :]
    f1i[:, k1p:k1p + k1n] = -np.sin(th.T) * wk[None, :]
    k1e = k1n + k1n % 2
    g = np.concatenate([g, np.zeros((k1e - k1n,) + g.shape[1:])], axis=0)
    gi = np.concatenate([gi, np.zeros((k1e - k1n,) + gi.shape[1:])], axis=0)
    return dict(na=na, t1n=t1n, k1n=k1n, k1e=k1e, k1p=k1p, f1c=_hl(f1c), f1f=_hl(f1f), g=_hl(g), gi=_hl(gi),
                f1i=_hl(f1i))


def _dft_stage1(src_ref, f1c_ref, are_ref, aim_ref, t1n, k1p):
    f1c = f1c_ref[...]

    def body(t2, carry):
        zs = src_ref[pl.ds(t2, t1n, stride=LANES), :]
        r = _dot3(f1c, 2 * k1p, zs)
        are_ref[pl.ds(t2, k1p, stride=LANES), :] = r[:k1p]
        aim_ref[pl.ds(t2, k1p, stride=LANES), :] = r[k1p:]
        return carry

    lax.fori_loop(0, LANES, body, 0, unroll=4)


def _cplx_left(gc_hl, xre, xim):
    cw = xre.shape[1]
    r = _dot3(gc_hl, 2 * LANES, jnp.concatenate([xre, xim], axis=1))
    p, q = r[:, :cw], r[:, cw:]
    return p[:LANES] - q[LANES:], p[LANES:] + q[:LANES]


def _spectrum_kernel(k_ref, f1f_ref, g_ref, inv_ref, xre_ref, xim_ref, are, aim, *, na, k1p):
    j = pl.program_id(1)

    @pl.when(j == 0)
    def _():
        _dft_stage1(k_ref, f1f_ref, are, aim, na, k1p)

    for half in range(2):
        r0 = pl.multiple_of((2 * j + half) * LANES, LANES)
        xre, xim = _cplx_left(g_ref[half], are[pl.ds(r0, LANES), :], aim[pl.ds(r0, LANES), :])
        rows = slice(half * LANES, (half + 1) * LANES)
        xre_ref[rows, :] = xre * inv_ref[...]
        xim_ref[rows, :] = xim * inv_ref[...]


def hyena_spectrum(k, inv_norm, dc):
    n2, C = k.shape
    k1e, k1p, na = dc["k1e"], dc["k1p"], dc["na"]
    cw = LANES
    out = jax.ShapeDtypeStruct((k1e * LANES, C), f32)
    ospec = pl.BlockSpec((2 * LANES, cw), lambda c, k: (k, c))
    return pl.pallas_call(
        functools.partial(_spectrum_kernel, na=na, k1p=k1p),
        grid=(C // cw, k1e // 2),
        in_specs=[pl.BlockSpec((n2, cw), lambda c, k: (0, c)),
                  pl.BlockSpec(dc["f1f"].shape, lambda c, k: (0, 0)),
                  pl.BlockSpec((2, 4 * LANES, LANES), lambda c, k: (k, 0, 0)),
                  pl.BlockSpec((1, cw), lambda c, k: (0, c))],
        out_specs=[ospec, ospec],
        out_shape=[out, out],
        scratch_shapes=[pltpu.VMEM((k1p * LANES, cw), f32), pltpu.VMEM((k1p * LANES, cw), f32)],
        compiler_params=_cparams(("parallel", "arbitrary")),
        name="hyena_spectrum",
    )(k, dc["f1f"], dc["g"], inv_norm)


def _hyena_conv_kernel(z_ref, x0_ref, f1c_ref, g_ref, gi_ref, f1i_ref, kre_ref, kim_ref,
                       skip_ref, o_ref, are, aim, y_scr, *, t1n, k1p):
    j = pl.program_id(2)

    @pl.when(j == 0)
    def _():
        _dft_stage1(z_ref.at[0], f1c_ref, are, aim, t1n, k1p)

    for half in range(2):
        r0 = pl.multiple_of((2 * j + half) * LANES, LANES)
        xre, xim = _cplx_left(g_ref[half], are[pl.ds(r0, LANES), :], aim[pl.ds(r0, LANES), :])
        rows = slice(half * LANES, (half + 1) * LANES)
        kre = kre_ref[rows, :]
        kim = kim_ref[rows, :]
        yre = xre * kre - xim * kim
        yim = xre * kim + xim * kre
        bre, bim = _cplx_left(gi_ref[half], yre, yim)
        are[pl.ds(r0, LANES), :] = bre
        aim[pl.ds(r0, LANES), :] = bim

    @pl.when(j == pl.num_programs(2) - 1)
    def _():
        f1i = f1i_ref[...]

        def body(t2, carry):
            bb = jnp.concatenate([are[pl.ds(t2, k1p, stride=LANES), :], aim[pl.ds(t2, k1p, stride=LANES), :]], axis=0)
            y_scr[pl.ds(t2, t1n, stride=LANES), :] = _dot3(f1i, t1n, bb)
            return carry

        lax.fori_loop(0, LANES, body, 0, unroll=4)
        z = z_ref[0]
        o_ref[0] = (x0_ref[0].astype(f32) * (y_scr[...] + z * skip_ref[...])).astype(o_ref.dtype)


def hyena_conv(z, x0, spec_re, spec_im, skip, dc):
    B, n, W = z.shape
    nb = W // LANES
    k1e, k1p, t1n = dc["k1e"], dc["k1p"], dc["t1n"]
    seq = pl.BlockSpec((1, n, LANES), lambda b, c, k: (b, 0, c))
    fspec = pl.BlockSpec((2 * LANES, LANES), lambda b, c, k: (k, c))
    cspec = pl.BlockSpec((2, 4 * LANES, LANES), lambda b, c, k: (k, 0, 0))
    vspec = pl.BlockSpec((1, LANES), lambda b, c, k: (0, c))
    return pl.pallas_call(
        functools.partial(_hyena_conv_kernel, t1n=t1n, k1p=k1p),
        grid=(B, nb, k1e // 2),
        in_specs=[seq, seq,
                  pl.BlockSpec(dc["f1c"].shape, lambda b, c, k: (0, 0)), cspec, cspec,
                  pl.BlockSpec(dc["f1i"].shape, lambda b, c, k: (0, 0)),
                  fspec, fspec, vspec],
        out_specs=seq,
        out_shape=jax.ShapeDtypeStruct((B, n, W), bf16),
        scratch_shapes=[pltpu.VMEM((k1p * LANES, LANES), f32), pltpu.VMEM((k1p * LANES, LANES), f32),
                        pltpu.VMEM((n, LANES), f32)],
        compiler_params=_cparams(("parallel", "parallel", "arbitrary")),
        name="hyena_conv",
    )(z, x0, dc["f1c"], dc["g"], dc["gi"], dc["f1i"], spec_re, spec_im, skip)


def _hyena_ctx_kernel(p0_ref, p1_ref, p2_ref, w0_ref, w1_ref, w2_ref, b0_ref, b1_ref, b2_ref,
                      k_ref, inv_ref, skip_ref, fd_ref, fi_ref, o_ref, *, n):
    x0 = _conv3(p0_ref[0], w0_ref, b0_ref)
    z = _conv3(p1_ref[0], w1_ref, b1_ref) * _conv3(p2_ref[0], w2_ref, b2_ref)
    fd = fd_ref[...]
    N = 2 * n
    zf = jnp.dot(fd[:, :n], z, precision=HI, preferred_element_type=f32)
    kf = jnp.dot(fd, k_ref[...], precision=HI, preferred_element_type=f32) * inv_ref[...]
    yre = zf[:N] * kf[:N] - zf[N:] * kf[N:]
    yim = zf[:N] * kf[N:] + zf[N:] * kf[:N]
    y = jnp.dot(fi_ref[...], jnp.concatenate([yre, yim], axis=0), precision=HI, preferred_element_type=f32)
    o_ref[0] = (x0 * (y + z * skip_ref[...])).astype(o_ref.dtype)


def hyena_ctx(p, conv_w, conv_b, k, inv_norm, skip):
    B, n, _ = p.shape
    N = 2 * n
    nb = HY_W // LANES
    kk = np.arange(N)
    ph = 2 * np.pi * ((kk[:, None] * kk[None, :]) % N) / N
    fd = jnp.asarray(np.concatenate([np.cos(ph), -np.sin(ph)], axis=0).astype(np.float32))
    fi = jnp.asarray((np.concatenate([np.cos(ph[:n]), -np.sin(ph[:n])], axis=1) / N).astype(np.float32))
    conv_b = conv_b.reshape(1, 3 * HY_W)
    pspec = lambda g: pl.BlockSpec((1, n, LANES), lambda b, c: (b, 0, g * nb + c))
    wspec = lambda g: pl.BlockSpec((3, LANES), lambda b, c: (0, g * nb + c))
    bspec = lambda g: pl.BlockSpec((1, LANES), lambda b, c: (0, g * nb + c))
    vspec = pl.BlockSpec((1, LANES), lambda b, c: (0, c))
    return pl.pallas_call(
        functools.partial(_hyena_ctx_kernel, n=n),
        grid=(B, nb),
        in_specs=[pspec(0), pspec(1), pspec(2), wspec(0), wspec(1), wspec(2), bspec(0), bspec(1), bspec(2),
                  pl.BlockSpec((N, LANES), lambda b, c: (0, c)),
                  vspec, vspec,
                  pl.BlockSpec(fd.shape, lambda b, c: (0, 0)), pl.BlockSpec(fi.shape, lambda b, c: (0, 0))],
        out_specs=pl.BlockSpec((1, n, LANES), lambda b, c: (b, 0, c)),
        out_shape=jax.ShapeDtypeStruct((B, n, HY_W), bf16),
        compiler_params=_cparams(("parallel", "parallel")),
        name="hyena_ctx",
    )(p, p, p, conv_w, conv_w, conv_w, conv_b, conv_b, conv_b, k, inv_norm, skip, fd, fi)


def _merge_kernel(of_ref, ob_ref, gs_ref, nb_ref, hc_ref, g_ref, wa_ref, wb_ref, wc_ref, wo_ref, x_ref, m_ref, o_ref):
    d = D_MODEL
    tot = of_ref[...] + ob_ref[...]
    gs = gs_ref[...].astype(f32)
    ra = []
    for h in range(HG_HEADS):
        sl = slice(LANES * h, LANES * (h + 1))
        th = tot[:, sl]
        ms = jnp.mean(th * th, axis=-1, keepdims=True)
        ra.append(th * lax.rsqrt(ms + EPS) * gs[:, sl])
    ya = jnp.dot(jnp.concatenate(ra, axis=1).astype(bf16), wa_ref[...], preferred_element_type=f32)
    yb = jnp.dot(nb_ref[...], wb_ref[...], preferred_element_type=f32)
    yc = jnp.dot(hc_ref[...], wc_ref[...], preferred_element_type=f32)
    g = g_ref[...].astype(f32)
    mix = g[:, :d] * ya + g[:, d:2 * d] * yb + g[:, 2 * d:] * yc
    y = jnp.dot(mix.astype(bf16), wo_ref[...], preferred_element_type=f32)
    o_ref[...] = x_ref[...] + m_ref[0] * y


def merge(o_f, o_b, gs, nb, hc, gates, wa, wb, wc, wo, x2d, m, rows_per_group):
    R = x2d.shape[0]
    tm = 512
    tpg = rows_per_group // tm
    G = m.shape[0]
    row = lambda w: pl.BlockSpec((tm, w), lambda i: (i, 0))
    full = lambda a: pl.BlockSpec(a.shape, lambda i: (0, 0))
    return pl.pallas_call(
        _merge_kernel,
        grid=(R // tm,),
        in_specs=[row(HG_W), row(HG_W), row(HG_W), row(NA_W), row(HY_W), row(3 * D_MODEL),
                  full(wa), full(wb), full(wc), full(wo),
                  row(D_MODEL), pl.BlockSpec((1, 1, D_MODEL), lambda i: (i // tpg, 0, 0))],
        out_specs=row(D_MODEL),
        out_shape=jax.ShapeDtypeStruct((R, D_MODEL), f32),
        compiler_params=_cparams(("parallel",)),
        name="merge",
    )(o_f, o_b, gs, nb, hc, gates, wa, wb, wc, wo, x2d, m.reshape(G, 1, D_MODEL))


def _router_kernel(x_ref, g_ref, sh_ref, sc_ref, wrt_ref, wr_ref, h_ref, at_ref, am_ref):
    x = x_ref[...]
    ms = jnp.mean(x * x, axis=-1, keepdims=True)
    h = x * lax.rsqrt(ms + EPS) * g_ref[...] * (1.0 + sc_ref[0]) + sh_ref[0]
    h_ref[...] = h.astype(h_ref.dtype)
    lt = _nt(wrt_ref[...], h, precision=HI)
    et = jnp.exp(lt - jnp.max(lt, axis=0, keepdims=True))
    at_ref[0] = et / jnp.sum(et, axis=0, keepdims=True)
    lm = jnp.dot(h, wr_ref[...], precision=HI, preferred_element_type=f32)
    em = jnp.exp(lm - jnp.max(lm, axis=1, keepdims=True))
    am_ref[...] = em / jnp.sum(em, axis=1, keepdims=True)


def router(x2d, g, shift, scale, w_router, n_per_set):
    R = x2d.shape[0]
    tm = min(512, n_per_set)
    tps = n_per_set // tm
    S = R // n_per_set
    G = shift.shape[0]
    gmap = (lambda i: (i // tps, 0, 0)) if G > 1 else (lambda i: (0, 0, 0))
    wr = w_router.astype(f32)
    return pl.pallas_call(
        _router_kernel,
        grid=(R // tm,),
        in_specs=[pl.BlockSpec((tm, D_MODEL), lambda i: (i, 0)),
                  pl.BlockSpec((1, D_MODEL), lambda i: (0, 0)),
                  pl.BlockSpec((1, 1, D_MODEL), gmap),
                  pl.BlockSpec((1, 1, D_MODEL), gmap),
                  pl.BlockSpec((N_EXPERTS, D_MODEL), lambda i: (0, 0)),
                  pl.BlockSpec((D_MODEL, N_EXPERTS), lambda i: (0, 0))],
        out_specs=[pl.BlockSpec((tm, D_MODEL), lambda i: (i, 0)),
                   pl.BlockSpec((1, N_EXPERTS, tm), lambda i: (i // tps, 0, i % tps)),
                   pl.BlockSpec((tm, N_EXPERTS), lambda i: (i, 0))],
        out_shape=[jax.ShapeDtypeStruct((R, D_MODEL), bf16),
                   jax.ShapeDtypeStruct((S, N_EXPERTS, n_per_set), f32),
                   jax.ShapeDtypeStruct((R, N_EXPERTS), f32)],
        compiler_params=_cparams(("parallel",)),
        name="router",
    )(x2d, g.reshape(1, D_MODEL), shift.reshape(G, 1, D_MODEL), scale.reshape(G, 1, D_MODEL), wr.T, wr)


SEL_BLK = 256
SUB = LANES
SUBW = SUB + 8
UNSEL = -float(2 ** 30)


def _prefix_incl(mask_f, tri, T):
    outs = []
    off = jnp.zeros((mask_f.shape[0], 1), f32)
    for b in range(T // SEL_BLK):
        blk = mask_f[:, b * SEL_BLK:(b + 1) * SEL_BLK].astype(bf16)
        pre = jnp.dot(blk, tri, preferred_element_type=f32) + off
        outs.append(pre)
        off = pre[:, SEL_BLK - 1:SEL_BLK]
    return jnp.concatenate(outs, axis=1)


def _select_kernel(a_ref, tri_ref, cm_ref, posm_ref, cnt_ref, *, T, cap):
    aff = a_ref[0]
    bits = pltpu.bitcast(aff, i32)
    tri = tri_ref[...]

    def bit_step(i, thr):
        cand = thr | (1 << (30 - i))
        cnt = jnp.sum((bits >= cand).astype(f32), axis=1, keepdims=True)
        return jnp.where(cnt >= cap, cand, thr)

    thr = lax.fori_loop(0, 31, bit_step, jnp.zeros((N_EXPERTS, 1), i32))
    gt = bits > thr
    eq = bits == thr
    need = cap - jnp.sum(gt.astype(f32), axis=1, keepdims=True)
    eqf = eq.astype(f32)
    rank_eq = _prefix_incl(eqf, tri, T) - eqf
    sel = gt | (eq & (rank_eq < need))
    self_ = sel.astype(f32)
    pos = _prefix_incl(self_, tri, T) - self_
    posm_ref[0] = jnp.where(sel, pos, UNSEL)
    cnt_ref[0] = jnp.dot(self_.astype(bf16), cm_ref[...], preferred_element_type=f32).astype(i32)


def select_topk(aff, cap):
    S, E, T = aff.shape
    tri = jnp.asarray(np.triu(np.ones((SEL_BLK, SEL_BLK), np.float32)), bf16)
    cm = (jnp.arange(T)[:, None] < jnp.arange(LANES)[None, :] * SUB).astype(bf16)
    return pl.pallas_call(
        functools.partial(_select_kernel, T=T, cap=cap),
        grid=(S,),
        in_specs=[pl.BlockSpec((1, E, T), lambda s: (s, 0, 0)),
                  pl.BlockSpec((SEL_BLK, SEL_BLK), lambda s: (0, 0)),
                  pl.BlockSpec((T, LANES), lambda s: (0, 0))],
        out_specs=[pl.BlockSpec((1, E, T), lambda s: (s, 0, 0)),
                   pl.BlockSpec((1, E, LANES), lambda s: (s, 0, 0))],
        out_shape=[jax.ShapeDtypeStruct((S, E, T), f32), jax.ShapeDtypeStruct((S, E, LANES), i32)],
        compiler_params=_cparams(("parallel",)),
        name="select_topk",
    )(aff, tri, cm)


def _align8(v):
    return lax.shift_left(lax.shift_right_logical(v, 3), 3)


def _align16(v):
    return lax.shift_left(lax.shift_right_logical(v, 4), 4)


CMB_ROWS = SUB + 16


def _gather_kernel(cnt_ref, h_ref, pos_ref, o_ref, acc, *, TT, cap, tps, R):
    e = pl.program_id(0)
    tl = pl.program_id(1)

    @pl.when(tl == 0)
    def _():
        acc[...] = jnp.zeros_like(acc)

    st = tl // tps
    nsub = TT // SUB
    cbase = (st * N_EXPERTS + e) * LANES + (tl % tps) * nsub
    rid = lax.broadcasted_iota(i32, (SUBW, SUB), 0).astype(f32)
    for s in range(nsub):
        off8 = _align8(cnt_ref[cbase + s])
        pos = pos_ref[0, 0, :, s * SUB:(s + 1) * SUB]
        onehot = jnp.where(pos == rid + off8.astype(f32), 1.0, 0.0).astype(bf16)
        rows = jnp.dot(onehot, h_ref[s * SUB:(s + 1) * SUB, :], preferred_element_type=f32)
        r0 = pl.multiple_of(st * cap + off8, 8)
        acc[pl.ds(r0, SUBW), :] += rows

    @pl.when(tl == pl.num_programs(1) - 1)
    def _():
        o_ref[0] = acc[0:R, :].astype(o_ref.dtype)


def gather_rows(cnt, h, posm, cap, TT):
    S, E, T = posm.shape
    tps = T // TT
    R = S * cap
    gs = pltpu.PrefetchScalarGridSpec(
        num_scalar_prefetch=1,
        grid=(E, S * tps),
        in_specs=[pl.BlockSpec((TT, D_MODEL), lambda e, t, c: (t, 0)),
                  pl.BlockSpec((1, 1, 1, TT), lambda e, t, c: (t // tps, e, 0, t % tps))],
        out_specs=pl.BlockSpec((1, R, D_MODEL), lambda e, t, c: (e, 0, 0)),
        scratch_shapes=[pltpu.VMEM((R + SUBW, D_MODEL), f32)])
    return pl.pallas_call(
        functools.partial(_gather_kernel, TT=TT, cap=cap, tps=tps, R=R),
        grid_spec=gs,
        out_shape=jax.ShapeDtypeStruct((E, R, D_MODEL), bf16),
        compiler_params=_cparams(("parallel", "arbitrary")),
        name="moe_gather",
    )(cnt.reshape(-1), h, posm.reshape(S, E, 1, T))


EXPERT_TF = 256


def _ffn_kernel(*refs, n):
    xs = refs[:n]
    wg_ref, wu_ref, wd_ref = refs[n:n + 3]
    his = refs[n + 3:2 * n + 3]
    los = refs[2 * n + 3:3 * n + 3]
    accs = refs[3 * n + 3:]
    j = pl.program_id(1)
    wg = wg_ref[0, 0].astype(bf16)
    wu = wu_ref[0, 0].astype(bf16)
    wd = wd_ref[0, 0].astype(bf16)
    for x_ref, hi_ref, lo_ref, acc in zip(xs, his, los, accs):
        x = x_ref[0]
        a = jnp.dot(x, wg, preferred_element_type=f32)
        u = jnp.dot(x, wu, preferred_element_type=f32)
        d = jnp.dot((_silu(a) * u).astype(bf16), wd, preferred_element_type=f32)

        @pl.when(j == 0)
        def _(acc=acc, d=d):
            acc[...] = d

        @pl.when(j > 0)
        def _(acc=acc, d=d):
            acc[...] += d

        @pl.when(j == pl.num_programs(1) - 1)
        def _(acc=acc, hi_ref=hi_ref, lo_ref=lo_ref):
            hi, lo = _split(acc[...])
            hi_ref[0] = hi
            lo_ref[0] = lo


def expert_ffn(xgs, layer, w_gate, w_up, w_down):
    E = xgs[0].shape[0]
    nf = D_FF_EXPERT // EXPERT_TF
    n = len(xgs)
    rowspec = lambda a: pl.BlockSpec((1, a.shape[1], D_MODEL), lambda e, j: (e, 0, 0))
    res = pl.pallas_call(
        functools.partial(_ffn_kernel, n=n),
        grid=(E, nf),
        in_specs=[rowspec(a) for a in xgs] + [
            pl.BlockSpec((1, 1, D_MODEL, EXPERT_TF), lambda e, j: (layer, e, 0, j)),
            pl.BlockSpec((1, 1, D_MODEL, EXPERT_TF), lambda e, j: (layer, e, 0, j)),
            pl.BlockSpec((1, 1, EXPERT_TF, D_MODEL), lambda e, j: (layer, e, j, 0))],
        out_specs=[rowspec(a) for a in xgs] * 2,
        out_shape=[jax.ShapeDtypeStruct(a.shape, bf16) for a in xgs] * 2,
        scratch_shapes=[pltpu.VMEM(a.shape[1:], f32) for a in xgs],
        compiler_params=_cparams(("parallel", "arbitrary")),
        name="expert_ffn",
    )(*xgs, w_gate, w_up, w_down)
    return [(res[i], res[n + i]) for i in range(n)]


def _combine_kernel(cnt_ref, x_ref, pos_ref, am_ref, m_ref, yh_ref, yl_ref, o_ref, *, TT, cap, R, W, ytot):
    st = pl.program_id(0)
    tl = pl.program_id(1)
    e = pl.program_id(2)

    @pl.when(e == 0)
    def _():
        o_ref[...] = x_ref[...]

    nsub = TT // SUB
    cbase = (st * N_EXPERTS + e) * LANES + tl * nsub
    rowbase = e * R + st * cap
    ws = jnp.minimum(rowbase + _align16(cnt_ref[cbase]), ytot - W)
    lane = lax.broadcasted_iota(i32, (TT, N_EXPERTS), 1)
    gcol = jnp.sum(jnp.where(lane == e, am_ref[...], 0.0), axis=1, keepdims=True)
    m5 = m_ref[0]
    rid = lax.broadcasted_iota(i32, (CMB_ROWS, SUB), 0).astype(f32)
    for s in range(nsub):
        rel = jnp.minimum(rowbase + _align16(cnt_ref[cbase + s]) - ws, W - CMB_ROWS)
        rel = pl.multiple_of(rel, 16)
        first = (ws + rel - rowbase).astype(f32)
        pos = pos_ref[0, 0, :, s * SUB:(s + 1) * SUB]
        onehot = jnp.where(pos == rid + first, 1.0, 0.0).astype(bf16)
        picked = (_tn(onehot, yh_ref[pl.ds(rel, CMB_ROWS), :])
                  + _tn(onehot, yl_ref[pl.ds(rel, CMB_ROWS), :]))
        sl = slice(s * SUB, (s + 1) * SUB)
        o_ref[sl, :] += m5 * (gcol[sl] * picked)


def combine(cnt, x2d, posm, aff_tm, mvec, y_hl, cap, TT):
    S, E, T = posm.shape
    tps = T // TT
    R = S * cap
    W = TT + 32
    ytot = E * R
    nsub = TT // SUB
    G = mvec.shape[0]

    def ymap(st, tl, e, c):
        off = _align16(c[(st * E + e) * LANES + tl * nsub])
        return (pl.multiple_of(jnp.minimum(e * R + st * cap + off, ytot - W), 16), 0)

    tok = lambda w: pl.BlockSpec((TT, w), lambda st, tl, e, c: (st * tps + tl, 0))
    mmap = (lambda st, tl, e, c: (st, 0, 0)) if G > 1 else (lambda st, tl, e, c: (0, 0, 0))
    yspec = pl.BlockSpec((pl.Element(W), pl.Element(D_MODEL)), ymap)
    gs = pltpu.PrefetchScalarGridSpec(
        num_scalar_prefetch=1,
        grid=(S, tps, E),
        in_specs=[tok(D_MODEL),
                  pl.BlockSpec((1, 1, 1, TT), lambda st, tl, e, c: (st, e, 0, tl)),
                  tok(N_EXPERTS),
                  pl.BlockSpec((1, 1, D_MODEL), mmap),
                  yspec, yspec],
        out_specs=tok(D_MODEL))
    return pl.pallas_call(
        functools.partial(_combine_kernel, TT=TT, cap=cap, R=R, W=W, ytot=ytot),
        grid_spec=gs,
        out_shape=jax.ShapeDtypeStruct(x2d.shape, f32),
        compiler_params=_cparams(("parallel", "parallel", "arbitrary")),
        name="moe_combine",
    )(cnt.reshape(-1), x2d, posm.reshape(S, E, 1, T), aff_tm, mvec.reshape(G, 1, D_MODEL),
      y_hl[0].reshape(ytot, D_MODEL), y_hl[1].reshape(ytot, D_MODEL))


def _rope_tables(n):
    half = NA_HD // 2
    q = half // 2
    inv = ROPE_THETA ** (-jnp.arange(q, dtype=f32) / q)
    pos = jnp.arange(n)
    ang_r = (pos // GRID_W).astype(f32)[:, None] * inv
    ang_c = (pos % GRID_W).astype(f32)[:, None] * inv
    zero = jnp.zeros_like(ang_r)
    c = jnp.concatenate([jnp.cos(ang_r)] * 2 + [jnp.cos(ang_c)] * 2, axis=1)
    s1 = jnp.concatenate([-jnp.sin(ang_r), zero, -jnp.sin(ang_c), zero], axis=1)
    s2 = jnp.concatenate([zero, jnp.sin(ang_r), zero, jnp.sin(ang_c)], axis=1)
    two = lambda a: jnp.concatenate([a, a], axis=1)
    return two(c), two(s1), two(s2)


def _mixing(hx, hc, need_ctx, B, N, NC, la, lc, w_in, q_gain, k_gain, table, rope, bd, conv_w, conv_b,
            spec, skip, filt_c, wa, wb, wc, wo, x2d, c2d, mx2, mc2):
    tile8 = lambda v: jnp.tile(v.reshape(1, NA_HD), (1, NA_HEADS))
    qg, kg = tile8(q_gain), tile8(k_gain)
    norm_aux = [("col", kg), ("const", bd)]
    rope_aux = [("row", rope[0]), ("row", rope[1]), ("row", rope[2])]
    lf_aux = lambda d: [("col", la[d:d + 1]), ("col", lc[d:d + 1])]
    tc = hc.shape[0]

    lff_c = project(hc, w_in, OFF_FF, 512, _epi_logforget, lf_aux(0), (f32,), tm=tc).reshape(B, NC, 512)
    lfb_c = project(hc, w_in, OFF_FB, 512, _epi_logforget, lf_aux(1), (f32,), tm=tc).reshape(B, NC, 512)
    i_c = project(hc, w_in, OFF_I, 512, _epi_raw, tm=tc).reshape(B, NC, 512)
    k_c = project(hc, w_in, OFF_NK, 512, _epi_norm, norm_aux, tm=tc).reshape(B, NC, 512)
    v_c = project(hc, w_in, OFF_NV, 512, _epi_raw, tm=tc).reshape(B, NC, 512)
    if need_ctx:
        q_c = project(hc, w_in, OFF_HQ, 512, _epi_silu, tm=tc).reshape(B, NC, 512)
    else:
        q_c = jnp.zeros((B, NC, 512), bf16)
    s0 = jnp.zeros((B, HG_HEADS, LANES, LANES), f32)
    oc_f, oc_b, s_f, s_b = hgrn_bidir(lff_c, lfb_c, i_c, q_c, s0, s0)

    lff_x = project(hx, w_in, OFF_FF, 512, _epi_logforget, lf_aux(0), (f32,)).reshape(B, N, 512)
    lfb_x = project(hx, w_in, OFF_FB, 512, _epi_logforget, lf_aux(1), (f32,)).reshape(B, N, 512)
    i_x = project(hx, w_in, OFF_I, 512, _epi_raw).reshape(B, N, 512)
    q_x = project(hx, w_in, OFF_HQ, 512, _epi_silu).reshape(B, N, 512)
    g_x = project(hx, w_in, OFF_HG, 512, _epi_silu).reshape(B, N, 512)
    k_x = project(hx, w_in, OFF_NK, 512, _epi_norm_rope, norm_aux + rope_aux, rows_per_seq=N).reshape(B, N, 512)
    v_x = project(hx, w_in, OFF_NV, 512, _epi_raw).reshape(B, N, 512)
    qn_x, qr_x = project(hx, w_in, OFF_NQ, 512, _epi_norm_both, [("col", qg), ("const", bd)] + rope_aux,
                         (bf16, bf16), rows_per_seq=N)
    p_x = project(hx, w_in, OFF_HY, 3 * HY_W, _epi_raw, out_dtypes=(f32,)).reshape(B, N, 3 * HY_W)
    gates_x = project(hx, w_in, OFF_GATE, 3 * D_MODEL, _epi_sigmoid)

    ox_f, ox_b, _, _ = hgrn_bidir(lff_x, lfb_x, i_x, q_x, s_f, s_b)

    nb_x = neighbourhood_attention(qr_x.reshape(B, N, 512), qn_x.reshape(B, N, 512), k_x, v_x, k_c, v_c, table)

    z_x, x0_x = hyena_pre(p_x, conv_w, conv_b)
    hy_x = hyena_conv(z_x, x0_x, spec[0], spec[1], skip, spec[2])

    flat = lambda a: a.reshape(-1, a.shape[-1])
    x_new = merge(flat(ox_f), flat(ox_b), flat(g_x), flat(nb_x), flat(hy_x), gates_x, wa, wb, wc, wo, x2d, mx2, N)
    if not need_ctx:
        return x_new, None

    qn_c = project(hc, w_in, OFF_NQ, 512, _epi_norm, [("col", qg), ("const", bd)], tm=tc).reshape(B, NC, 512)
    nb_c = context_attention(qn_c, k_c, v_c)
    p_c = project(hc, w_in, OFF_HY, 3 * HY_W, _epi_raw, out_dtypes=(f32,), tm=tc).reshape(B, NC, 3 * HY_W)
    hy_c = hyena_ctx(p_c, conv_w, conv_b, filt_c[0], filt_c[1], skip)
    gates_c = project(hc, w_in, OFF_GATE, 3 * D_MODEL, _epi_sigmoid, tm=tc)
    g_c = project(hc, w_in, OFF_HG, 512, _epi_silu, tm=tc)
    c_new = merge(flat(oc_f), flat(oc_b), g_c, flat(nb_c), flat(hy_c), gates_c, wa, wb, wc, wo, c2d, mc2, B * NC)
    return x_new, c_new


def kernel(x, c, ctx, c_ctx, w_mod, b_mod, norm_mix, norm_ffn, w_in, hg_lb, na_q_gain, na_k_gain, na_rpb,
           hy_conv_w, hy_conv_b, hy_pe_w1, hy_pe_b1, hy_pe_freq1, hy_pe_w2, hy_pe_b2, hy_pe_freq2, hy_pe_w3,
           hy_skip, w_branch_a, w_branch_b, w_branch_c, w_out, w_router, w_e_gate, w_e_up, w_e_down):
    B, N, D = x.shape
    NC = ctx.shape[1]
    E = N_EXPERTS
    cap_x = EC_CAP_FACTOR * N // E
    cap_c = EC_CAP_FACTOR * NC // E

    lb = jnp.cumsum(jax.nn.softmax(hg_lb.astype(f32), axis=0), axis=0)
    lb = lb - lb[:1]
    la_all, lc_all = jnp.log(lb), jnp.log1p(-lb)

    s8 = jnp.zeros((8, D), f32).at[:B].set(c).at[B].set(c_ctx)
    rope = _rope_tables(N)
    bd = jnp.asarray(np.kron(np.eye(NA_HEADS), np.full((NA_HD, NA_HD), 1.0 / NA_HD)).astype(np.float32), bf16)
    dcx = _dft_consts(N)

    x2d = x.reshape(B * N, D)
    c2d = ctx.reshape(B * NC, D)
    for l in range(DEPTH):
        need_ctx = l < DEPTH - 1
        mv = modvec(s8, w_mod[l], b_mod[l])
        mx = [mv[:B, k * D:(k + 1) * D] for k in range(6)]
        mc = [mv[B:B + 1, k * D:(k + 1) * D] for k in range(6)]
        w_in_l = w_in[l].astype(bf16)
        hx = modulate(x2d, norm_mix[l], mx[0], mx[1], N, bf16)
        hc = modulate(c2d, norm_mix[l], mc[0], mc[1], B * NC, bf16)

        filt = (hy_pe_w1[l], hy_pe_b1[l], hy_pe_freq1[l], hy_pe_w2[l], hy_pe_b2[l], hy_pe_freq2[l], hy_pe_w3[l])
        k_x, nrm_x = hyena_filter(N, *filt)
        sre, sim = hyena_spectrum(k_x, 1.0 / nrm_x, dcx)
        skip = hy_skip[l].reshape(1, HY_W)
        filt_c = None
        if need_ctx:
            h_c, nrm_c = hyena_filter(NC, *filt)
            filt_c = (h_c, 1.0 / nrm_c)

        x2d, c_new = _mixing(
            hx, hc, need_ctx, B, N, NC, la_all[l], lc_all[l], w_in_l, na_q_gain[l], na_k_gain[l],
            _na_bias_table(na_rpb[l]), rope, bd, hy_conv_w[l], hy_conv_b[l], (sre, sim, dcx), skip,
            filt_c, w_branch_a[l].astype(bf16), w_branch_b[l].astype(bf16), w_branch_c[l].astype(bf16),
            w_out[l].astype(bf16), x2d, c2d, mx[2], mc[2])

        h2, aff_t, aff_m = router(x2d, norm_ffn[l], mx[3], mx[4], w_router[l], N)
        posm, cnt = select_topk(aff_t, cap_x)
        xgs = [gather_rows(cnt, h2, posm, cap_x, 1024)]
        if need_ctx:
            c2d = c_new
            hc2, aff_tc, aff_mc = router(c2d, norm_ffn[l], mc[3], mc[4], w_router[l], NC)
            posm_c, cnt_c = select_topk(aff_tc, cap_c)
            xgs.append(gather_rows(cnt_c, hc2, posm_c, cap_c, NC))
        ys = expert_ffn(xgs, l, w_e_gate, w_e_up, w_e_down)
        x2d = combine(cnt, x2d, posm, aff_m, mx[5], ys[0], cap_x, 512)
        if need_ctx:
            c2d = combine(cnt_c, c2d, posm_c, aff_mc, mc[5], ys[1], cap_c, NC)
    return x2d.reshape(B, N, D)
```

```python
import functools
import math

import numpy as np
import jax
import jax.numpy as jnp
from jax import lax
from jax.experimental import pallas as pl
from jax.experimental.pallas import tpu as pltpu

f32 = jnp.float32
bf16 = jnp.bfloat16
i32 = jnp.int32
HI = lax.Precision.HIGHEST

D_MODEL = 1024
DEPTH = 2
GRID_W = 64
EPS = 1e-6
HG_HEADS = 4
HG_W = 512
HG_CHUNK = 64
NA_HEADS = 8
NA_HD = 64
NA_W = 512
NA_WIN_R = 8
NA_WIN_C = 16
ROPE_THETA = 10000.0
HY_W = 512
HY_BANDS = 16
HY_PE_DIM = 1 + 2 * HY_BANDS
HY_FILT_HID = 64
HY_FAST_DECAY = 0.3
HY_SLOW_DECAY = 1.5
HY_TARGET = 1e-2
OFF_FF = 0
OFF_FB = 512
OFF_I = 1024
OFF_NK = 1536
OFF_NV = 2048
OFF_HQ = 2560
OFF_NQ = 3072
OFF_HG = 3584
OFF_HY = 4096
OFF_GATE = 5632
IN_COLS = 8704
N_EXPERTS = 16
EC_CAP_FACTOR = 2
D_FF_EXPERT = 2816

LANES = 128
NEG_BIG = -1e30
VMEM_LIMIT = 56 * 1024 * 1024


def _cparams(sem, vmem=VMEM_LIMIT):
    return pltpu.CompilerParams(dimension_semantics=sem, vmem_limit_bytes=vmem)


def _nt(a, b, precision=None):
    return lax.dot_general(a, b, (((1,), (1,)), ((), ())), precision=precision, preferred_element_type=f32)


def _tn(a, b, precision=None):
    return lax.dot_general(a, b, (((0,), (0,)), ((), ())), precision=precision, preferred_element_type=f32)


def _silu(x):
    return x * jax.nn.sigmoid(x)


def _split(x):
    hi = x.astype(bf16)
    return hi, (x - hi.astype(f32)).astype(bf16)


def _modvec_kernel(s_ref, w_ref, b_ref, o_ref):
    s = _silu(s_ref[...])
    o_ref[...] = jnp.dot(s, w_ref[...], precision=HI, preferred_element_type=f32) + b_ref[...]


def modvec(s8, w, b):
    n = w.shape[1]
    tn = 1024
    return pl.pallas_call(
        _modvec_kernel,
        grid=(n // tn,),
        in_specs=[pl.BlockSpec((8, D_MODEL), lambda j: (0, 0)),
                  pl.BlockSpec((D_MODEL, tn), lambda j: (0, j)),
                  pl.BlockSpec((1, tn), lambda j: (0, j))],
        out_specs=pl.BlockSpec((8, tn), lambda j: (0, j)),
        out_shape=jax.ShapeDtypeStruct((8, n), f32),
        compiler_params=_cparams(("parallel",)),
        name="modvec",
    )(s8, w, b.reshape(1, n))


def _modulate_kernel(x_ref, g_ref, sh_ref, sc_ref, o_ref):
    x = x_ref[...]
    ms = jnp.mean(x * x, axis=-1, keepdims=True)
    y = x * lax.rsqrt(ms + EPS)
    o_ref[...] = (y * g_ref[...] * (1.0 + sc_ref[0]) + sh_ref[0]).astype(o_ref.dtype)


def modulate(x2d, g, shift, scale, rows_per_group, out_dtype):
    R = x2d.shape[0]
    tm = 512
    tpg = rows_per_group // tm
    G = shift.shape[0]
    return pl.pallas_call(
        _modulate_kernel,
        grid=(R // tm,),
        in_specs=[pl.BlockSpec((tm, D_MODEL), lambda i: (i, 0)),
                  pl.BlockSpec((1, D_MODEL), lambda i: (0, 0)),
                  pl.BlockSpec((1, 1, D_MODEL), lambda i: (i // tpg, 0, 0)),
                  pl.BlockSpec((1, 1, D_MODEL), lambda i: (i // tpg, 0, 0))],
        out_specs=pl.BlockSpec((tm, D_MODEL), lambda i: (i, 0)),
        out_shape=jax.ShapeDtypeStruct((R, D_MODEL), out_dtype),
        compiler_params=_cparams(("parallel",)),
        name="modulate",
    )(x2d, g.reshape(1, D_MODEL), shift.reshape(G, 1, D_MODEL), scale.reshape(G, 1, D_MODEL))


def _log_sigmoid(z):
    return jnp.minimum(z, 0.0) - jnp.log1p(jnp.exp(-jnp.abs(z)))


def _epi_raw(acc, o_ref):
    o_ref[...] = acc.astype(o_ref.dtype)


def _epi_silu(acc, o_ref):
    o_ref[...] = _silu(acc).astype(o_ref.dtype)


def _epi_sigmoid(acc, o_ref):
    o_ref[...] = jax.nn.sigmoid(acc).astype(o_ref.dtype)


def _epi_logforget(acc, la_ref, lc_ref, o_ref):
    la = la_ref[...]
    c = lc_ref[...] + _log_sigmoid(acc)
    o_ref[...] = jnp.maximum(la, c) + jnp.log1p(jnp.exp(-jnp.abs(la - c)))


def _head_rms(acc, gain_ref, bd_ref):
    hi, lo = _split(acc * acc)
    ms = jnp.dot(hi, bd_ref[...], preferred_element_type=f32) + jnp.dot(lo, bd_ref[...], preferred_element_type=f32)
    return acc * lax.rsqrt(ms + EPS) * gain_ref[...]


def _rope(y, c_ref, s1_ref, s2_ref):
    reps = y.shape[1] // LANES
    c = jnp.concatenate([c_ref[...]] * reps, axis=1)
    s1 = jnp.concatenate([s1_ref[...]] * reps, axis=1)
    s2 = jnp.concatenate([s2_ref[...]] * reps, axis=1)
    w = y.shape[1]
    return y * c + pltpu.roll(y, w - 16, axis=1) * s1 + pltpu.roll(y, 16, axis=1) * s2


def _epi_norm(acc, gain_ref, bd_ref, o_ref):
    o_ref[...] = _head_rms(acc, gain_ref, bd_ref).astype(o_ref.dtype)


def _epi_norm_rope(acc, gain_ref, bd_ref, c_ref, s1_ref, s2_ref, o_ref):
    y = _head_rms(acc, gain_ref, bd_ref)
    o_ref[...] = _rope(y, c_ref, s1_ref, s2_ref).astype(o_ref.dtype)


def _epi_norm_both(acc, gain_ref, bd_ref, c_ref, s1_ref, s2_ref, on_ref, or_ref):
    y = _head_rms(acc, gain_ref, bd_ref)
    on_ref[...] = y.astype(on_ref.dtype)
    or_ref[...] = _rope(y, c_ref, s1_ref, s2_ref).astype(or_ref.dtype)


def _proj_kernel(h_ref, w_ref, *rest, epi):
    acc = jnp.dot(h_ref[...], w_ref[...], preferred_element_type=f32)
    epi(acc, *rest)


def project(h, w, c0, width, epi, aux=(), out_dtypes=(bf16,), tm=1024, rows_per_seq=None):
    R = h.shape[0]
    tn = 512
    nj = width // tn
    cb = c0 // tn
    in_specs = [pl.BlockSpec((tm, D_MODEL), lambda i, j: (i, 0)),
                pl.BlockSpec((D_MODEL, tn), lambda i, j: (0, cb + j))]
    args = [h, w]
    for kind, arr in aux:
        if kind == "col":
            in_specs.append(pl.BlockSpec((1, tn), lambda i, j: (0, j)))
        elif kind == "const":
            in_specs.append(pl.BlockSpec(arr.shape, lambda i, j: (0, 0)))
        else:
            tps = rows_per_seq // tm
            in_specs.append(pl.BlockSpec((tm, LANES), lambda i, j: (i % tps, 0)))
        args.append(arr)
    out_specs = [pl.BlockSpec((tm, tn), lambda i, j: (i, j)) for _ in out_dtypes]
    out_shape = [jax.ShapeDtypeStruct((R, width), dt) for dt in out_dtypes]
    res = pl.pallas_call(
        functools.partial(_proj_kernel, epi=epi),
        grid=(R // tm, nj),
        in_specs=in_specs,
        out_specs=out_specs,
        out_shape=out_shape,
        compiler_params=_cparams(("parallel", "parallel")),
        name="proj_" + epi.__name__[5:],
    )(*args)
    return res[0] if len(res) == 1 else res


def _hgrn_tmatrix(C, reverse):
    L = int(round(math.log2(C)))
    t = np.arange(C)
    tau = (C - 1 - t) if reverse else t
    tt, uu = tau[:, None], tau[None, :]
    T = np.zeros((2 + L, C, C), np.float32)
    T[0] = uu <= tt
    T[1] = uu > tt
    for l in range(L):
        same = (tt >> (l + 1)) == (uu >> (l + 1))
        tr = ((tt >> l) & 1) == 1
        ur = ((uu >> l) & 1) == 1
        T[2 + l] = same & ((tr & ur & (uu <= tt)) | (~tr & ~ur & (uu > tt)))
    return T.reshape((2 + L) * C, C)


def _hgrn_chunk(lf, v_all, q_all, tm, s_scr, C, reverse):
    L = int(round(math.log2(C)))
    W = lf.shape[1]
    hi, lo = _split(lf)
    r = jnp.dot(tm, jnp.concatenate([hi, lo], axis=1), preferred_element_type=f32)
    E = r[:, :W] + r[:, W:]
    kc_all = 1.0 - jnp.exp(lf)

    row = lax.broadcasted_iota(i32, (C, LANES), 0)
    tau = (C - 1 - row) if reverse else row
    ti = lax.broadcasted_iota(i32, (C, C), 0)
    si = lax.broadcasted_iota(i32, (C, C), 1)
    if reverse:
        ti, si = C - 1 - ti, C - 1 - si
    end_row = 0 if reverse else C - 1

    outs = []
    for h in range(HG_HEADS):
        sl = slice(LANES * h, LANES * (h + 1))
        qh, kch, vh = q_all[:, sl], kc_all[:, sl], v_all[:, sl]
        b = E[0:C, sl]
        suf = E[C:2 * C, sl]
        st = s_scr[h]
        qb = (qh * jnp.exp(b)).astype(bf16)
        o = _nt(qb, st.astype(bf16))
        att = jnp.where(ti == si, _nt(qh.astype(bf16), kch.astype(bf16)), 0.0)
        for l in range(L):
            x = jnp.exp(E[(2 + l) * C:(3 + l) * C, sl])
            later = ((tau >> l) & 1) == 1
            ql = jnp.where(later, qh * x, 0.0).astype(bf16)
            kl = jnp.where(later, 0.0, kch * x).astype(bf16)
            att = att + jnp.where((ti >> (l + 1)) == (si >> (l + 1)), _nt(ql, kl), 0.0)
        o = o + jnp.dot(att.astype(bf16), vh, preferred_element_type=f32)
        kd = (kch * jnp.exp(suf)).astype(bf16)
        bend = b[end_row:end_row + 1, :]
        s_scr[h] = st * jnp.exp(bend) + _tn(vh, kd)
        outs.append(o)
    return jnp.concatenate(outs, axis=1)


def _hgrn_kernel(lff_ref, lfb_ref, vf_ref, vb_ref, qf_ref, qb_ref, s0f_ref, s0b_ref, tf_ref, tb_ref,
                 of_ref, ob_ref, sff_ref, sfb_ref, s_scr, *, C, B):
    c = pl.program_id(0)

    @pl.when(c == 0)
    def _():
        s_scr[0] = s0f_ref[...]
        s_scr[1] = s0b_ref[...]

    for b in range(B):
        of_ref[b] = _hgrn_chunk(lff_ref[b], vf_ref[b], qf_ref[b].astype(f32), tf_ref[...], s_scr.at[0, b], C, False)
        ob_ref[b] = _hgrn_chunk(lfb_ref[b], vb_ref[b], qb_ref[b].astype(f32), tb_ref[...], s_scr.at[1, b], C, True)

    @pl.when(c == pl.num_programs(0) - 1)
    def _():
        sff_ref[...] = s_scr[0]
        sfb_ref[...] = s_scr[1]


def hgrn_bidir(lf_f, lf_b, v, q, s0_f, s0_b):
    B, N, W = lf_f.shape
    C = HG_CHUNK
    nch = N // C
    tf = jnp.asarray(_hgrn_tmatrix(C, False), bf16)
    tb = jnp.asarray(_hgrn_tmatrix(C, True), bf16)
    fw = pl.BlockSpec((B, C, W), lambda c: (0, c, 0))
    bw = pl.BlockSpec((B, C, W), lambda c: (0, nch - 1 - c, 0))
    st = pl.BlockSpec((B, HG_HEADS, LANES, LANES), lambda c: (0, 0, 0, 0))
    tsp = pl.BlockSpec(tf.shape, lambda c: (0, 0))
    seq = jax.ShapeDtypeStruct((B, N, W), f32)
    sts = jax.ShapeDtypeStruct((B, HG_HEADS, LANES, LANES), f32)
    return pl.pallas_call(
        functools.partial(_hgrn_kernel, C=C, B=B),
        grid=(nch,),
        in_specs=[fw, bw, fw, bw, fw, bw, st, st, tsp, tsp],
        out_specs=[fw, bw, st, st],
        out_shape=[seq, seq, sts, sts],
        scratch_shapes=[pltpu.VMEM((2, B, HG_HEADS, LANES, LANES), f32)],
        compiler_params=_cparams(("arbitrary",)),
        name="hgrn",
    )(lf_f, lf_b, v, v, q, q, s0_f, s0_b, tf, tb)


def _na_kernel(qr_ref, qn_ref, k_ref, v_ref, kc_ref, vc_ref, tab_ref, o_ref, s_scr, p_scr, *, rows_per_step, n_rows):
    g = pl.program_id(2)
    scale = NA_HD ** -0.5
    lane = lax.broadcasted_iota(i32, (GRID_W, LANES), 1)
    kcx = kc_ref[0]
    vcx = vc_ref[0]
    win = NA_WIN_R * GRID_W
    ctx_len = kcx.shape[0]

    starts = []
    for i in range(rows_per_step):
        r = g * rows_per_step + i
        rs = jnp.clip(r - NA_WIN_R // 2, 0, n_rows - NA_WIN_R)
        off = rs - r + (NA_WIN_R - 1)
        start = pl.multiple_of(rs * GRID_W, GRID_W)
        starts.append(start)
        kw = k_ref[0, pl.ds(start, win), :]
        qr = qr_ref[0, i * GRID_W:(i + 1) * GRID_W, :]
        qn = qn_ref[0, i * GRID_W:(i + 1) * GRID_W, :]
        for hh in range(2):
            m = (lane >= NA_HD * hh) & (lane < NA_HD * (hh + 1))
            qrm = jnp.where(m, qr, jnp.zeros_like(qr))
            qnm = jnp.where(m, qn, jnp.zeros_like(qn))
            row0 = (2 * i + hh) * GRID_W
            s_scr[row0:row0 + GRID_W, 0:win] = _nt(qrm, kw) * scale + tab_ref[hh, off]
            s_scr[row0:row0 + GRID_W, win:win + ctx_len] = _nt(qnm, kcx) * scale

    def softmax_rows(c, carry):
        r0 = pl.multiple_of(c * LANES, LANES)
        s = s_scr[pl.ds(r0, LANES), :]
        p = jnp.exp(s - jnp.max(s, axis=-1, keepdims=True))
        inv = 1.0 / jnp.sum(p, axis=-1, keepdims=True)
        p_scr[pl.ds(r0, LANES), :] = (p * inv).astype(bf16)
        return carry

    lax.fori_loop(0, 2 * rows_per_step * GRID_W // LANES, softmax_rows, 0, unroll=2)

    for i in range(rows_per_step):
        vw = v_ref[0, pl.ds(starts[i], win), :]
        res = []
        for hh in range(2):
            row0 = (2 * i + hh) * GRID_W
            p = p_scr[row0:row0 + GRID_W, :]
            res.append(jnp.dot(p[:, :win], vw, preferred_element_type=f32)
                       + jnp.dot(p[:, win:], vcx, preferred_element_type=f32))
        o_ref[0, i * GRID_W:(i + 1) * GRID_W, :] = jnp.where(lane < NA_HD, res[0], res[1]).astype(o_ref.dtype)


def _na_bias_table(rpb):
    col = jnp.arange(GRID_W)
    cs = jnp.clip(col - NA_WIN_C // 2, 0, GRID_W - NA_WIN_C)
    kc = jnp.arange(GRID_W)
    valid = (kc[None, :] >= cs[:, None]) & (kc[None, :] < cs[:, None] + NA_WIN_C)
    dc = jnp.clip(kc[None, :] - col[:, None] + (NA_WIN_C - 1), 0, 2 * NA_WIN_C - 2)
    bc = jnp.where(valid[None, None], rpb[:, :, dc], NEG_BIG)
    t2 = jnp.stack([bc[:, o:o + NA_WIN_R] for o in range(NA_WIN_R)], axis=1)
    t2 = t2.transpose(0, 1, 3, 2, 4)
    return t2.reshape(NA_HEADS, NA_WIN_R, GRID_W, NA_WIN_R * GRID_W).astype(f32)


def neighbourhood_attention(q_rot, qn, k_rot, v, kc, vc, table):
    B, N, W = q_rot.shape
    n_rows = N // GRID_W
    rps = 8
    ctx_len = kc.shape[1]
    pairs = W // LANES
    keys = NA_WIN_R * GRID_W + ctx_len
    return pl.pallas_call(
        functools.partial(_na_kernel, rows_per_step=rps, n_rows=n_rows),
        grid=(B, pairs, n_rows // rps),
        in_specs=[pl.BlockSpec((1, rps * GRID_W, LANES), lambda b, p, g: (b, g, p)),
                  pl.BlockSpec((1, rps * GRID_W, LANES), lambda b, p, g: (b, g, p)),
                  pl.BlockSpec((1, N, LANES), lambda b, p, g: (b, 0, p)),
                  pl.BlockSpec((1, N, LANES), lambda b, p, g: (b, 0, p)),
                  pl.BlockSpec((1, ctx_len, LANES), lambda b, p, g: (b, 0, p)),
                  pl.BlockSpec((1, ctx_len, LANES), lambda b, p, g: (b, 0, p)),
                  pl.BlockSpec((2, NA_WIN_R, GRID_W, NA_WIN_R * GRID_W), lambda b, p, g: (p, 0, 0, 0))],
        out_specs=pl.BlockSpec((1, rps * GRID_W, LANES), lambda b, p, g: (b, g, p)),
        out_shape=jax.ShapeDtypeStruct((B, N, W), bf16),
        scratch_shapes=[pltpu.VMEM((2 * rps * GRID_W, keys), f32), pltpu.VMEM((2 * rps * GRID_W, keys), bf16)],
        compiler_params=_cparams(("parallel", "parallel", "arbitrary")),
        name="natten",
    )(q_rot, qn, k_rot, v, kc, vc, table)


def _ctx_attn_kernel(q_ref, k_ref, v_ref, o_ref):
    scale = NA_HD ** -0.5
    q = q_ref[0]
    k = k_ref[0]
    v = v_ref[0]
    lane = lax.broadcasted_iota(i32, q.shape, 1)
    res = []
    for hh in range(2):
        m = (lane >= NA_HD * hh) & (lane < NA_HD * (hh + 1))
        s = _nt(jnp.where(m, q, jnp.zeros_like(q)), k) * scale
        p = jnp.exp(s - jnp.max(s, axis=-1, keepdims=True))
        p = p / jnp.sum(p, axis=-1, keepdims=True)
        res.append(jnp.dot(p.astype(bf16), v, preferred_element_type=f32))
    o_ref[0] = jnp.where(lane < NA_HD, res[0], res[1]).astype(o_ref.dtype)


def context_attention(q, k, v):
    B, N, W = q.shape
    spec = pl.BlockSpec((1, N, LANES), lambda b, p: (b, 0, p))
    return pl.pallas_call(
        _ctx_attn_kernel,
        grid=(B, W // LANES),
        in_specs=[spec, spec, spec],
        out_specs=spec,
        out_shape=jax.ShapeDtypeStruct((B, N, W), bf16),
        compiler_params=_cparams(("parallel", "parallel")),
        name="ctx_attn",
    )(q, k, v)


def _filter_kernel(z_ref, w1_ref, b1_ref, f1_ref, w2_ref, b2_ref, f2_ref, w3_ref, dl_ref, k_ref, nrm_ref, *, tm, n):
    i = pl.program_id(0)
    z = z_ref[...]
    a = jnp.sin(f1_ref[...] * (jnp.dot(z, w1_ref[...], precision=HI, preferred_element_type=f32) + b1_ref[...]))
    a = jnp.sin(f2_ref[...] * (jnp.dot(a, w2_ref[...], precision=HI, preferred_element_type=f32) + b2_ref[...]))
    h = jnp.dot(a, w3_ref[...], precision=HI, preferred_element_type=f32)
    h = h * jnp.exp(-z[:, 0:1] * dl_ref[...])
    row = lax.broadcasted_iota(i32, (tm, HY_W), 0) + i * tm
    k = jnp.where(row == n, 0.0, h)
    k_ref[...] = k
    part = jnp.sum(jnp.abs(k), axis=0, keepdims=True)

    @pl.when(i == 0)
    def _():
        nrm_ref[...] = part

    @pl.when(i > 0)
    def _():
        nrm_ref[...] = nrm_ref[...] + part


def hyena_filter(n, w1, b1, fr1, w2, b2, fr2, w3):
    t = jnp.linspace(0.0, 1.0, n, dtype=f32)[:, None]
    w = 2 * math.pi * jnp.arange(n, dtype=f32)[:, None] / n
    fb = jnp.linspace(1e-4, HY_BANDS - 1, HY_BANDS, dtype=f32)[None]
    z = jnp.concatenate([t, jnp.cos(fb * w), -jnp.sin(fb * w)], axis=-1)
    z = jnp.concatenate([z, jnp.zeros((1, HY_PE_DIM), f32), z[:0:-1]], axis=0)
    z = jnp.pad(z, ((0, 0), (0, LANES - HY_PE_DIM)))
    w1p = jnp.pad(w1.astype(f32), ((0, LANES - HY_PE_DIM), (0, 0)))
    deltas = jnp.abs(jnp.linspace(math.log(HY_TARGET) / HY_SLOW_DECAY, math.log(HY_TARGET) / HY_FAST_DECAY,
                                  2 * HY_W, dtype=f32))[None]
    tm = min(n, 512)
    hid = HY_FILT_HID
    full = lambda shape: pl.BlockSpec(shape, lambda i: (0, 0))
    tph = n // tm
    return pl.pallas_call(
        functools.partial(_filter_kernel, tm=tm, n=n),
        grid=(2 * n // tm,),
        in_specs=[pl.BlockSpec((tm, LANES), lambda i: (i, 0)),
                  full((LANES, hid)), full((1, hid)), full((1, hid)),
                  full((hid, hid)), full((1, hid)), full((1, hid)),
                  pl.BlockSpec((hid, HY_W), lambda i: (0, i // tph)),
                  pl.BlockSpec((1, HY_W), lambda i: (0, i // tph))],
        out_specs=[pl.BlockSpec((tm, HY_W), lambda i: (i, 0)), pl.BlockSpec((1, HY_W), lambda i: (0, 0))],
        out_shape=[jax.ShapeDtypeStruct((2 * n, HY_W), f32), jax.ShapeDtypeStruct((1, HY_W), f32)],
        compiler_params=_cparams(("arbitrary",)),
        name="hyena_filter",
    )(z, w1p, b1.reshape(1, hid), fr1.reshape(1, hid), w2.astype(f32), b2.reshape(1, hid), fr2.reshape(1, hid),
      w3.astype(f32), deltas)


def _conv3(u, w_ref, b_ref):
    n = u.shape[0]
    row = lax.broadcasted_iota(i32, u.shape, 0)
    prev = jnp.where(row == 0, 0.0, pltpu.roll(u, 1, axis=0))
    nxt = jnp.where(row == n - 1, 0.0, pltpu.roll(u, n - 1, axis=0))
    return prev * w_ref[0:1, :] + u * w_ref[1:2, :] + nxt * w_ref[2:3, :] + b_ref[...]


def _hyena_pre_kernel(p0_ref, p1_ref, p2_ref, w0_ref, w1_ref, w2_ref, b0_ref, b1_ref, b2_ref, z_ref, x0_ref):
    x0_ref[0] = _conv3(p0_ref[0], w0_ref, b0_ref).astype(x0_ref.dtype)
    z_ref[0] = _conv3(p1_ref[0], w1_ref, b1_ref) * _conv3(p2_ref[0], w2_ref, b2_ref)


def hyena_pre(p, conv_w, conv_b):
    B, N, _ = p.shape
    nb = HY_W // LANES
    conv_b = conv_b.reshape(1, 3 * HY_W)
    pspec = lambda g: pl.BlockSpec((1, N, LANES), lambda b, c: (b, 0, g * nb + c))
    wspec = lambda g: pl.BlockSpec((3, LANES), lambda b, c: (0, g * nb + c))
    bspec = lambda g: pl.BlockSpec((1, LANES), lambda b, c: (0, g * nb + c))
    ospec = pl.BlockSpec((1, N, LANES), lambda b, c: (b, 0, c))
    return pl.pallas_call(
        _hyena_pre_kernel,
        grid=(B, nb),
        in_specs=[pspec(0), pspec(1), pspec(2), wspec(0), wspec(1), wspec(2), bspec(0), bspec(1), bspec(2)],
        out_specs=[ospec, ospec],
        out_shape=[jax.ShapeDtypeStruct((B, N, HY_W), f32), jax.ShapeDtypeStruct((B, N, HY_W), bf16)],
        compiler_params=_cparams(("parallel", "parallel")),
        name="hyena_pre",
    )(p, p, p, conv_w, conv_w, conv_w, conv_b, conv_b, conv_b)


def _hl(a):
    a32 = jnp.asarray(a.astype(np.float32))
    hi = a32.astype(bf16)
    lo = (a32 - hi.astype(f32)).astype(bf16)
    return jnp.concatenate([hi, lo], axis=-2)


def _dot3(a_hl, m, x):
    xh, xl = _split(x)
    r = jnp.dot(a_hl, xh, preferred_element_type=f32)
    return r[:m] + r[m:] + jnp.dot(a_hl[:m], xl, preferred_element_type=f32)


def _dft_consts(n):
    N = 2 * n
    na = N // LANES
    t1n = na // 2
    k1n = na // 2 + 1
    k1p = -(-k1n // 8) * 8
    k1 = np.arange(k1n)
    t1 = np.arange(t1n)
    th = 2 * np.pi * ((t1[None, :] * k1[:, None]) % na) / na
    f1c = np.zeros((2 * k1p, t1n))
    f1c[:k1n] = np.cos(th)
    f1c[k1p:k1p + k1n] = -np.sin(th)
    thf = 2 * np.pi * ((np.arange(na)[None, :] * k1[:, None]) % na) / na
    f1f = np.zeros((2 * k1p, na))
    f1f[:k1n] = np.cos(thf)
    f1f[k1p:k1p + k1n] = -np.sin(thf)
    k2 = np.arange(LANES)
    t2 = np.arange(LANES)
    m = (t2[None, None, :] * (k1[:, None, None] + na * k2[None, :, None])) % N
    ph = 2 * np.pi * m / N
    g = np.concatenate([np.cos(ph), -np.sin(ph)], axis=1)
    pht = ph.transpose(0, 2, 1)
    gi = np.concatenate([np.cos(pht), np.sin(pht)], axis=1)
    wk = np.where((k1 == 0) | (k1 == na // 2), 1.0, 2.0) / N
    f1i = np.zeros((t1n, 2 * k1p))
    f1i[:, :k1n] = np.cos(th.T) * wk[None, :]
    f1i[:, k1p:k1p + k1n] = -np.sin(th.T) * wk[None, :]
    k1e = k1n + k1n % 2
    g = np.concatenate([g, np.zeros((k1e - k1n,) + g.shape[1:])], axis=0)
    gi = np.concatenate([gi, np.zeros((k1e - k1n,) + gi.shape[1:])], axis=0)
    return dict(na=na, t1n=t1n, k1n=k1n, k1e=k1e, k1p=k1p, f1c=_hl(f1c), f1f=_hl(f1f), g=_hl(g), gi=_hl(gi),
                f1i=_hl(f1i))


def _dft_stage1(src_ref, f1c_ref, are_ref, aim_ref, t1n, k1p):
    f1c = f1c_ref[...]

    def body(t2, carry):
        zs = src_ref[pl.ds(t2, t1n, stride=LANES), :]
        r = _dot3(f1c, 2 * k1p, zs)
        are_ref[pl.ds(t2, k1p, stride=LANES), :] = r[:k1p]
        aim_ref[pl.ds(t2, k1p, stride=LANES), :] = r[k1p:]
        return carry

    lax.fori_loop(0, LANES, body, 0, unroll=4)


def _cplx_left(gc_hl, xre, xim):
    cw = xre.shape[1]
    r = _dot3(gc_hl, 2 * LANES, jnp.concatenate([xre, xim], axis=1))
    p, q = r[:, :cw], r[:, cw:]
    return p[:LANES] - q[LANES:], p[LANES:] + q[:LANES]


def _spectrum_kernel(k_ref, f1f_ref, g_ref, inv_ref, xre_ref, xim_ref, are, aim, *, na, k1p):
    j = pl.program_id(1)

    @pl.when(j == 0)
    def _():
        _dft_stage1(k_ref, f1f_ref, are, aim, na, k1p)

    for half in range(2):
        r0 = pl.multiple_of((2 * j + half) * LANES, LANES)
        xre, xim = _cplx_left(g_ref[half], are[pl.ds(r0, LANES), :], aim[pl.ds(r0, LANES), :])
        rows = slice(half * LANES, (half + 1) * LANES)
        xre_ref[rows, :] = xre * inv_ref[...]
        xim_ref[rows, :] = xim * inv_ref[...]


def hyena_spectrum(k, inv_norm, dc):
    n2, C = k.shape
    k1e, k1p, na = dc["k1e"], dc["k1p"], dc["na"]
    cw = LANES
    out = jax.ShapeDtypeStruct((k1e * LANES, C), f32)
    ospec = pl.BlockSpec((2 * LANES, cw), lambda c, k: (k, c))
    return pl.pallas_call(
        functools.partial(_spectrum_kernel, na=na, k1p=k1p),
        grid=(C // cw, k1e // 2),
        in_specs=[pl.BlockSpec((n2, cw), lambda c, k: (0, c)),
                  pl.BlockSpec(dc["f1f"].shape, lambda c, k: (0, 0)),
                  pl.BlockSpec((2, 4 * LANES, LANES), lambda c, k: (k, 0, 0)),
                  pl.BlockSpec((1, cw), lambda c, k: (0, c))],
        out_specs=[ospec, ospec],
        out_shape=[out, out],
        scratch_shapes=[pltpu.VMEM((k1p * LANES, cw), f32), pltpu.VMEM((k1p * LANES, cw), f32)],
        compiler_params=_cparams(("parallel", "arbitrary")),
        name="hyena_spectrum",
    )(k, dc["f1f"], dc["g"], inv_norm)


def _hyena_conv_kernel(z_ref, x0_ref, f1c_ref, g_ref, gi_ref, f1i_ref, kre_ref, kim_ref,
                       skip_ref, o_ref, are, aim, y_scr, *, t1n, k1p):
    j = pl.program_id(2)

    @pl.when(j == 0)
    def _():
        _dft_stage1(z_ref.at[0], f1c_ref, are, aim, t1n, k1p)

    for half in range(2):
        r0 = pl.multiple_of((2 * j + half) * LANES, LANES)
        xre, xim = _cplx_left(g_ref[half], are[pl.ds(r0, LANES), :], aim[pl.ds(r0, LANES), :])
        rows = slice(half * LANES, (half + 1) * LANES)
        kre = kre_ref[rows, :]
        kim = kim_ref[rows, :]
        yre = xre * kre - xim * kim
        yim = xre * kim + xim * kre
        bre, bim = _cplx_left(gi_ref[half], yre, yim)
        are[pl.ds(r0, LANES), :] = bre
        aim[pl.ds(r0, LANES), :] = bim

    @pl.when(j == pl.num_programs(2) - 1)
    def _():
        f1i = f1i_ref[...]

        def body(t2, carry):
            bb = jnp.concatenate([are[pl.ds(t2, k1p, stride=LANES), :], aim[pl.ds(t2, k1p, stride=LANES), :]], axis=0)
            y_scr[pl.ds(t2, t1n, stride=LANES), :] = _dot3(f1i, t1n, bb)
            return carry

        lax.fori_loop(0, LANES, body, 0, unroll=4)
        z = z_ref[0]
        o_ref[0] = (x0_ref[0].astype(f32) * (y_scr[...] + z * skip_ref[...])).astype(o_ref.dtype)


def hyena_conv(z, x0, spec_re, spec_im, skip, dc):
    B, n, W = z.shape
    nb = W // LANES
    k1e, k1p, t1n = dc["k1e"], dc["k1p"], dc["t1n"]
    seq = pl.BlockSpec((1, n, LANES), lambda b, c, k: (b, 0, c))
    fspec = pl.BlockSpec((2 * LANES, LANES), lambda b, c, k: (k, c))
    cspec = pl.BlockSpec((2, 4 * LANES, LANES), lambda b, c, k: (k, 0, 0))
    vspec = pl.BlockSpec((1, LANES), lambda b, c, k: (0, c))
    return pl.pallas_call(
        functools.partial(_hyena_conv_kernel, t1n=t1n, k1p=k1p),
        grid=(B, nb, k1e // 2),
        in_specs=[seq, seq,
                  pl.BlockSpec(dc["f1c"].shape, lambda b, c, k: (0, 0)), cspec, cspec,
                  pl.BlockSpec(dc["f1i"].shape, lambda b, c, k: (0, 0)),
                  fspec, fspec, vspec],
        out_specs=seq,
        out_shape=jax.ShapeDtypeStruct((B, n, W), bf16),
        scratch_shapes=[pltpu.VMEM((k1p * LANES, LANES), f32), pltpu.VMEM((k1p * LANES, LANES), f32),
                        pltpu.VMEM((n, LANES), f32)],
        compiler_params=_cparams(("parallel", "parallel", "arbitrary")),
        name="hyena_conv",
    )(z, x0, dc["f1c"], dc["g"], dc["gi"], dc["f1i"], spec_re, spec_im, skip)


def _hyena_ctx_kernel(p0_ref, p1_ref, p2_ref, w0_ref, w1_ref, w2_ref, b0_ref, b1_ref, b2_ref,
                      k_ref, inv_ref, skip_ref, fd_ref, fi_ref, o_ref, *, n):
    x0 = _conv3(p0_ref[0], w0_ref, b0_ref)
    z = _conv3(p1_ref[0], w1_ref, b1_ref) * _conv3(p2_ref[0], w2_ref, b2_ref)
    fd = fd_ref[...]
    N = 2 * n
    zf = jnp.dot(fd[:, :n], z, precision=HI, preferred_element_type=f32)
    kf = jnp.dot(fd, k_ref[...], precision=HI, preferred_element_type=f32) * inv_ref[...]
    yre = zf[:N] * kf[:N] - zf[N:] * kf[N:]
    yim = zf[:N] * kf[N:] + zf[N:] * kf[:N]
    y = jnp.dot(fi_ref[...], jnp.concatenate([yre, yim], axis=0), precision=HI, preferred_element_type=f32)
    o_ref[0] = (x0 * (y + z * skip_ref[...])).astype(o_ref.dtype)


def hyena_ctx(p, conv_w, conv_b, k, inv_norm, skip):
    B, n, _ = p.shape
    N = 2 * n
    nb = HY_W // LANES
    kk = np.arange(N)
    ph = 2 * np.pi * ((kk[:, None] * kk[None, :]) % N) / N
    fd = jnp.asarray(np.concatenate([np.cos(ph), -np.sin(ph)], axis=0).astype(np.float32))
    fi = jnp.asarray((np.concatenate([np.cos(ph[:n]), -np.sin(ph[:n])], axis=1) / N).astype(np.float32))
    conv_b = conv_b.reshape(1, 3 * HY_W)
    pspec = lambda g: pl.BlockSpec((1, n, LANES), lambda b, c: (b, 0, g * nb + c))
    wspec = lambda g: pl.BlockSpec((3, LANES), lambda b, c: (0, g * nb + c))
    bspec = lambda g: pl.BlockSpec((1, LANES), lambda b, c: (0, g * nb + c))
    vspec = pl.BlockSpec((1, LANES), lambda b, c: (0, c))
    return pl.pallas_call(
        functools.partial(_hyena_ctx_kernel, n=n),
        grid=(B, nb),
        in_specs=[pspec(0), pspec(1), pspec(2), wspec(0), wspec(1), wspec(2), bspec(0), bspec(1), bspec(2),
                  pl.BlockSpec((N, LANES), lambda b, c: (0, c)),
                  vspec, vspec,
                  pl.BlockSpec(fd.shape, lambda b, c: (0, 0)), pl.BlockSpec(fi.shape, lambda b, c: (0, 0))],
        out_specs=pl.BlockSpec((1, n, LANES), lambda b, c: (b, 0, c)),
        out_shape=jax.ShapeDtypeStruct((B, n, HY_W), bf16),
        compiler_params=_cparams(("parallel", "parallel")),
        name="hyena_ctx",
    )(p, p, p, conv_w, conv_w, conv_w, conv_b, conv_b, conv_b, k, inv_norm, skip, fd, fi)


def _merge_kernel(of_ref, ob_ref, gs_ref, nb_ref, hc_ref, g_ref, wa_ref, wb_ref, wc_ref, wo_ref, x_ref, m_ref, o_ref):
    d = D_MODEL
    tot = of_ref[...] + ob_ref[...]
    gs = gs_ref[...].astype(f32)
    ra = []
    for h in range(HG_HEADS):
        sl = slice(LANES * h, LANES * (h + 1))
        th = tot[:, sl]
        ms = jnp.mean(th * th, axis=-1, keepdims=True)
        ra.append(th * lax.rsqrt(ms + EPS) * gs[:, sl])
    ya = jnp.dot(jnp.concatenate(ra, axis=1).astype(bf16), wa_ref[...], preferred_element_type=f32)
    yb = jnp.dot(nb_ref[...], wb_ref[...], preferred_element_type=f32)
    yc = jnp.dot(hc_ref[...], wc_ref[...], preferred_element_type=f32)
    g = g_ref[...].astype(f32)
    mix = g[:, :d] * ya + g[:, d:2 * d] * yb + g[:, 2 * d:] * yc
    y = jnp.dot(mix.astype(bf16), wo_ref[...], preferred_element_type=f32)
    o_ref[...] = x_ref[...] + m_ref[0] * y


def merge(o_f, o_b, gs, nb, hc, gates, wa, wb, wc, wo, x2d, m, rows_per_group):
    R = x2d.shape[0]
    tm = 512
    tpg = rows_per_group // tm
    G = m.shape[0]
    row = lambda w: pl.BlockSpec((tm, w), lambda i: (i, 0))
    full = lambda a: pl.BlockSpec(a.shape, lambda i: (0, 0))
    return pl.pallas_call(
        _merge_kernel,
        grid=(R // tm,),
        in_specs=[row(HG_W), row(HG_W), row(HG_W), row(NA_W), row(HY_W), row(3 * D_MODEL),
                  full(wa), full(wb), full(wc), full(wo),
                  row(D_MODEL), pl.BlockSpec((1, 1, D_MODEL), lambda i: (i // tpg, 0, 0))],
        out_specs=row(D_MODEL),
        out_shape=jax.ShapeDtypeStruct((R, D_MODEL), f32),
        compiler_params=_cparams(("parallel",)),
        name="merge",
    )(o_f, o_b, gs, nb, hc, gates, wa, wb, wc, wo, x2d, m.reshape(G, 1, D_MODEL))


def _router_kernel(x_ref, g_ref, sh_ref, sc_ref, wrt_ref, wr_ref, h_ref, at_ref, am_ref):
    x = x_ref[...]
    ms = jnp.mean(x * x, axis=-1, keepdims=True)
    h = x * lax.rsqrt(ms + EPS) * g_ref[...] * (1.0 + sc_ref[0]) + sh_ref[0]
    h_ref[...] = h.astype(h_ref.dtype)
    lt = _nt(wrt_ref[...], h, precision=HI)
    et = jnp.exp(lt - jnp.max(lt, axis=0, keepdims=True))
    at_ref[0] = et / jnp.sum(et, axis=0, keepdims=True)
    lm = jnp.dot(h, wr_ref[...], precision=HI, preferred_element_type=f32)
    em = jnp.exp(lm - jnp.max(lm, axis=1, keepdims=True))
    am_ref[...] = em / jnp.sum(em, axis=1, keepdims=True)


def router(x2d, g, shift, scale, w_router, n_per_set):
    R = x2d.shape[0]
    tm = min(512, n_per_set)
    tps = n_per_set // tm
    S = R // n_per_set
    G = shift.shape[0]
    gmap = (lambda i: (i // tps, 0, 0)) if G > 1 else (lambda i: (0, 0, 0))
    wr = w_router.astype(f32)
    return pl.pallas_call(
        _router_kernel,
        grid=(R // tm,),
        in_specs=[pl.BlockSpec((tm, D_MODEL), lambda i: (i, 0)),
                  pl.BlockSpec((1, D_MODEL), lambda i: (0, 0)),
                  pl.BlockSpec((1, 1, D_MODEL), gmap),
                  pl.BlockSpec((1, 1, D_MODEL), gmap),
                  pl.BlockSpec((N_EXPERTS, D_MODEL), lambda i: (0, 0)),
                  pl.BlockSpec((D_MODEL, N_EXPERTS), lambda i: (0, 0))],
        out_specs=[pl.BlockSpec((tm, D_MODEL), lambda i: (i, 0)),
                   pl.BlockSpec((1, N_EXPERTS, tm), lambda i: (i // tps, 0, i % tps)),
                   pl.BlockSpec((tm, N_EXPERTS), lambda i: (i, 0))],
        out_shape=[jax.ShapeDtypeStruct((R, D_MODEL), bf16),
                   jax.ShapeDtypeStruct((S, N_EXPERTS, n_per_set), f32),
                   jax.ShapeDtypeStruct((R, N_EXPERTS), f32)],
        compiler_params=_cparams(("parallel",)),
        name="router",
    )(x2d, g.reshape(1, D_MODEL), shift.reshape(G, 1, D_MODEL), scale.reshape(G, 1, D_MODEL), wr.T, wr)


SEL_BLK = 256
SUB = LANES
SUBW = SUB + 8
UNSEL = -float(2 ** 30)


def _prefix_incl(mask_f, tri, T):
    outs = []
    off = jnp.zeros((mask_f.shape[0], 1), f32)
    for b in range(T // SEL_BLK):
        blk = mask_f[:, b * SEL_BLK:(b + 1) * SEL_BLK].astype(bf16)
        pre = jnp.dot(blk, tri, preferred_element_type=f32) + off
        outs.append(pre)
        off = pre[:, SEL_BLK - 1:SEL_BLK]
    return jnp.concatenate(outs, axis=1)


def _select_kernel(a_ref, tri_ref, cm_ref, posm_ref, cnt_ref, *, T, cap):
    aff = a_ref[0]
    bits = pltpu.bitcast(aff, i32)
    tri = tri_ref[...]

    def bit_step(i, thr):
        cand = thr | (1 << (30 - i))
        cnt = jnp.sum((bits >= cand).astype(f32), axis=1, keepdims=True)
        return jnp.where(cnt >= cap, cand, thr)

    thr = lax.fori_loop(0, 31, bit_step, jnp.zeros((N_EXPERTS, 1), i32))
    gt = bits > thr
    eq = bits == thr
    need = cap - jnp.sum(gt.astype(f32), axis=1, keepdims=True)
    eqf = eq.astype(f32)
    rank_eq = _prefix_incl(eqf, tri, T) - eqf
    sel = gt | (eq & (rank_eq < need))
    self_ = sel.astype(f32)
    pos = _prefix_incl(self_, tri, T) - self_
    posm_ref[0] = jnp.where(sel, pos, UNSEL)
    cnt_ref[0] = jnp.dot(self_.astype(bf16), cm_ref[...], preferred_element_type=f32).astype(i32)


def select_topk(aff, cap):
    S, E, T = aff.shape
    tri = jnp.asarray(np.triu(np.ones((SEL_BLK, SEL_BLK), np.float32)), bf16)
    cm = (jnp.arange(T)[:, None] < jnp.arange(LANES)[None, :] * SUB).astype(bf16)
    return pl.pallas_call(
        functools.partial(_select_kernel, T=T, cap=cap),
        grid=(S,),
        in_specs=[pl.BlockSpec((1, E, T), lambda s: (s, 0, 0)),
                  pl.BlockSpec((SEL_BLK, SEL_BLK), lambda s: (0, 0)),
                  pl.BlockSpec((T, LANES), lambda s: (0, 0))],
        out_specs=[pl.BlockSpec((1, E, T), lambda s: (s, 0, 0)),
                   pl.BlockSpec((1, E, LANES), lambda s: (s, 0, 0))],
        out_shape=[jax.ShapeDtypeStruct((S, E, T), f32), jax.ShapeDtypeStruct((S, E, LANES), i32)],
        compiler_params=_cparams(("parallel",)),
        name="select_topk",
    )(aff, tri, cm)


def _align8(v):
    return lax.shift_left(lax.shift_right_logical(v, 3), 3)


def _align16(v):
    return lax.shift_left(lax.shift_right_logical(v, 4), 4)


CMB_ROWS = SUB + 16


def _gather_kernel(cnt_ref, h_ref, pos_ref, o_ref, acc, *, TT, cap, tps, R, srows, EP):
    tl = pl.program_id(1)

    @pl.when(tl == 0)
    def _():
        acc[...] = jnp.zeros_like(acc)

    st = tl // tps
    nsub = TT // SUB
    rid = lax.broadcasted_iota(i32, (srows, SUB), 0).astype(f32)
    for ep in range(EP):
        e = pl.program_id(0) * EP + ep
        cbase = (st * N_EXPERTS + e) * LANES + (tl % tps) * nsub
        for s in range(nsub):
            off8 = _align8(cnt_ref[cbase + s])
            pos = pos_ref[0, ep, :, s * SUB:(s + 1) * SUB]
            onehot = jnp.where(pos == rid + off8.astype(f32), 1.0, 0.0).astype(bf16)
            rows = jnp.dot(onehot, h_ref[s * SUB:(s + 1) * SUB, :], preferred_element_type=f32)
            r0 = pl.multiple_of(st * cap + off8, 8)
            acc[ep, pl.ds(r0, srows), :] += rows

    @pl.when(tl == pl.num_programs(1) - 1)
    def _():
        o_ref[...] = acc[:, 0:R, :].astype(o_ref.dtype)


def _gather_call(cnt, h, posm, cap, TT, srows, EP):
    S, E, T = posm.shape
    tps = T // TT
    R = S * cap
    gs = pltpu.PrefetchScalarGridSpec(
        num_scalar_prefetch=1,
        grid=(E // EP, S * tps),
        in_specs=[pl.BlockSpec((TT, D_MODEL), lambda e, t, c: (t, 0)),
                  pl.BlockSpec((1, EP, 1, TT), lambda e, t, c: (t // tps, e, 0, t % tps))],
        out_specs=pl.BlockSpec((EP, R, D_MODEL), lambda e, t, c: (e, 0, 0)),
        scratch_shapes=[pltpu.VMEM((EP, R + srows, D_MODEL), f32)])
    return pl.pallas_call(
        functools.partial(_gather_kernel, TT=TT, cap=cap, tps=tps, R=R, srows=srows, EP=EP),
        grid_spec=gs,
        out_shape=jax.ShapeDtypeStruct((E, R, D_MODEL), bf16),
        compiler_params=_cparams(("parallel", "arbitrary")),
        name="moe_gather",
    )(cnt.reshape(-1), h, posm.reshape(S, E, 1, T))


FAST_SUB_MAX = 48
FAST_TILE_MAX = 112
FAST_ROWS = 64
CMB_TILE = 512


def _fits_fast(cnt, T):
    nsub = T // SUB
    per_sub = cnt[..., 1:nsub + 1] - cnt[..., :nsub]
    k = CMB_TILE // SUB
    per_tile = cnt[..., k:nsub + 1:k] - cnt[..., 0:nsub:k]
    return (jnp.max(per_sub) <= FAST_SUB_MAX) & (jnp.max(per_tile) <= FAST_TILE_MAX)


def gather_rows(cnt, h, posm, cap, TT, fast_ok=None):
    safe = lambda: _gather_call(cnt, h, posm, cap, TT, SUBW, 1)
    if fast_ok is None:
        return safe()
    return lax.cond(fast_ok, lambda: _gather_call(cnt, h, posm, cap, TT, FAST_ROWS, 2), safe)


EXPERT_TF = 256


def _ffn_kernel(*refs, n):
    xs = refs[:n]
    wg_ref, wu_ref, wd_ref = refs[n:n + 3]
    his = refs[n + 3:2 * n + 3]
    los = refs[2 * n + 3:3 * n + 3]
    accs = refs[3 * n + 3:]
    j = pl.program_id(1)
    wg = wg_ref[0, 0].astype(bf16)
    wu = wu_ref[0, 0].astype(bf16)
    wd = wd_ref[0, 0].astype(bf16)
    for x_ref, hi_ref, lo_ref, acc in zip(xs, his, los, accs):
        x = x_ref[0]
        a = jnp.dot(x, wg, preferred_element_type=f32)
        u = jnp.dot(x, wu, preferred_element_type=f32)
        d = jnp.dot((_silu(a) * u).astype(bf16), wd, preferred_element_type=f32)

        @pl.when(j == 0)
        def _(acc=acc, d=d):
            acc[...] = d

        @pl.when(j > 0)
        def _(acc=acc, d=d):
            acc[...] += d

        @pl.when(j == pl.num_programs(1) - 1)
        def _(acc=acc, hi_ref=hi_ref, lo_ref=lo_ref):
            hi, lo = _split(acc[...])
            hi_ref[0] = hi
            lo_ref[0] = lo


def expert_ffn(xgs, layer, w_gate, w_up, w_down):
    E = xgs[0].shape[0]
    nf = D_FF_EXPERT // EXPERT_TF
    n = len(xgs)
    rowspec = lambda a: pl.BlockSpec((1, a.shape[1], D_MODEL), lambda e, j: (e, 0, 0))
    res = pl.pallas_call(
        functools.partial(_ffn_kernel, n=n),
        grid=(E, nf),
        in_specs=[rowspec(a) for a in xgs] + [
            pl.BlockSpec((1, 1, D_MODEL, EXPERT_TF), lambda e, j: (layer, e, 0, j)),
            pl.BlockSpec((1, 1, D_MODEL, EXPERT_TF), lambda e, j: (layer, e, 0, j)),
            pl.BlockSpec((1, 1, EXPERT_TF, D_MODEL), lambda e, j: (layer, e, j, 0))],
        out_specs=[rowspec(a) for a in xgs] * 2,
        out_shape=[jax.ShapeDtypeStruct(a.shape, bf16) for a in xgs] * 2,
        scratch_shapes=[pltpu.VMEM(a.shape[1:], f32) for a in xgs],
        compiler_params=_cparams(("parallel", "arbitrary")),
        name="expert_ffn",
    )(*xgs, w_gate, w_up, w_down)
    return [(res[i], res[n + i]) for i in range(n)]


def _combine_kernel(cnt_ref, x_ref, pos_ref, am_ref, m_ref, *rest, TT, cap, R, W, ytot, crows, EP):
    y_refs, o_ref = rest[:2 * EP], rest[2 * EP]
    st = pl.program_id(0)
    tl = pl.program_id(1)
    eg = pl.program_id(2)

    @pl.when(eg == 0)
    def _():
        o_ref[...] = x_ref[...]

    nsub = TT // SUB
    lane = lax.broadcasted_iota(i32, (TT, N_EXPERTS), 1)
    m5 = m_ref[0]
    rid = lax.broadcasted_iota(i32, (crows, SUB), 0).astype(f32)
    for ep in range(EP):
        e = eg * EP + ep
        yh_ref, yl_ref = y_refs[2 * ep], y_refs[2 * ep + 1]
        cbase = (st * N_EXPERTS + e) * LANES + tl * nsub
        rowbase = e * R + st * cap
        ws = jnp.minimum(rowbase + _align16(cnt_ref[cbase]), ytot - W)
        gcol = jnp.sum(jnp.where(lane == e, am_ref[...], 0.0), axis=1, keepdims=True)
        for s in range(nsub):
            rel = jnp.minimum(rowbase + _align16(cnt_ref[cbase + s]) - ws, W - crows)
            rel = pl.multiple_of(rel, 16)
            first = (ws + rel - rowbase).astype(f32)
            pos = pos_ref[0, ep, :, s * SUB:(s + 1) * SUB]
            onehot = jnp.where(pos == rid + first, 1.0, 0.0).astype(bf16)
            picked = (_tn(onehot, yh_ref[pl.ds(rel, crows), :])
                      + _tn(onehot, yl_ref[pl.ds(rel, crows), :]))
            sl = slice(s * SUB, (s + 1) * SUB)
            o_ref[sl, :] += m5 * (gcol[sl] * picked)


def _combine_call(cnt, x2d, posm, aff_tm, mvec, y_hl, cap, TT, crows, W, EP):
    S, E, T = posm.shape
    tps = T // TT
    R = S * cap
    ytot = E * R
    nsub = TT // SUB
    G = mvec.shape[0]

    def yspec(ep):
        def ymap(st, tl, eg, c):
            e = eg * EP + ep
            off = _align16(c[(st * E + e) * LANES + tl * nsub])
            return (pl.multiple_of(jnp.minimum(e * R + st * cap + off, ytot - W), 16), 0)
        return pl.BlockSpec((pl.Element(W), pl.Element(D_MODEL)), ymap)

    tok = lambda w: pl.BlockSpec((TT, w), lambda st, tl, e, c: (st * tps + tl, 0))
    mmap = (lambda st, tl, e, c: (st, 0, 0)) if G > 1 else (lambda st, tl, e, c: (0, 0, 0))
    gs = pltpu.PrefetchScalarGridSpec(
        num_scalar_prefetch=1,
        grid=(S, tps, E // EP),
        in_specs=[tok(D_MODEL),
                  pl.BlockSpec((1, EP, 1, TT), lambda st, tl, e, c: (st, e, 0, tl)),
                  tok(N_EXPERTS),
                  pl.BlockSpec((1, 1, D_MODEL), mmap)] + [yspec(ep) for ep in range(EP) for _ in range(2)],
        out_specs=tok(D_MODEL))
    yh, yl = y_hl[0].reshape(ytot, D_MODEL), y_hl[1].reshape(ytot, D_MODEL)
    return pl.pallas_call(
        functools.partial(_combine_kernel, TT=TT, cap=cap, R=R, W=W, ytot=ytot, crows=crows, EP=EP),
        grid_spec=gs,
        out_shape=jax.ShapeDtypeStruct(x2d.shape, f32),
        compiler_params=_cparams(("parallel", "parallel", "arbitrary")),
        name="moe_combine",
    )(cnt.reshape(-1), x2d, posm.reshape(S, E, 1, T), aff_tm, mvec.reshape(G, 1, D_MODEL), *([yh, yl] * EP))


def combine(cnt, x2d, posm, aff_tm, mvec, y_hl, cap, TT, fast_ok=None):
    safe = lambda: _combine_call(cnt, x2d, posm, aff_tm, mvec, y_hl, cap, TT, CMB_ROWS, TT + 32, 1)
    if fast_ok is None:
        return safe()
    wfast = FAST_TILE_MAX + 16 + FAST_ROWS
    return lax.cond(fast_ok, lambda: _combine_call(cnt, x2d, posm, aff_tm, mvec, y_hl, cap, TT, FAST_ROWS, wfast, 2),
                    safe)


def _rope_tables(n):
    half = NA_HD // 2
    q = half // 2
    inv = ROPE_THETA ** (-jnp.arange(q, dtype=f32) / q)
    pos = jnp.arange(n)
    ang_r = (pos // GRID_W).astype(f32)[:, None] * inv
    ang_c = (pos % GRID_W).astype(f32)[:, None] * inv
    zero = jnp.zeros_like(ang_r)
    c = jnp.concatenate([jnp.cos(ang_r)] * 2 + [jnp.cos(ang_c)] * 2, axis=1)
    s1 = jnp.concatenate([-jnp.sin(ang_r), zero, -jnp.sin(ang_c), zero], axis=1)
    s2 = jnp.concatenate([zero, jnp.sin(ang_r), zero, jnp.sin(ang_c)], axis=1)
    two = lambda a: jnp.concatenate([a, a], axis=1)
    return two(c), two(s1), two(s2)


def _mixing(hx, hc, need_ctx, B, N, NC, la, lc, w_in, q_gain, k_gain, table, rope, bd, conv_w, conv_b,
            spec, skip, filt_c, wa, wb, wc, wo, x2d, c2d, mx2, mc2):
    tile8 = lambda v: jnp.tile(v.reshape(1, NA_HD), (1, NA_HEADS))
    qg, kg = tile8(q_gain), tile8(k_gain)
    norm_aux = [("col", kg), ("const", bd)]
    rope_aux = [("row", rope[0]), ("row", rope[1]), ("row", rope[2])]
    lf_aux = lambda d: [("col", la[d:d + 1]), ("col", lc[d:d + 1])]
    tc = hc.shape[0]

    lff_c = project(hc, w_in, OFF_FF, 512, _epi_logforget, lf_aux(0), (f32,), tm=tc).reshape(B, NC, 512)
    lfb_c = project(hc, w_in, OFF_FB, 512, _epi_logforget, lf_aux(1), (f32,), tm=tc).reshape(B, NC, 512)
    i_c = project(hc, w_in, OFF_I, 512, _epi_raw, tm=tc).reshape(B, NC, 512)
    k_c = project(hc, w_in, OFF_NK, 512, _epi_norm, norm_aux, tm=tc).reshape(B, NC, 512)
    v_c = project(hc, w_in, OFF_NV, 512, _epi_raw, tm=tc).reshape(B, NC, 512)
    if need_ctx:
        q_c = project(hc, w_in, OFF_HQ, 512, _epi_silu, tm=tc).reshape(B, NC, 512)
    else:
        q_c = jnp.zeros((B, NC, 512), bf16)
    s0 = jnp.zeros((B, HG_HEADS, LANES, LANES), f32)
    oc_f, oc_b, s_f, s_b = hgrn_bidir(lff_c, lfb_c, i_c, q_c, s0, s0)

    lff_x = project(hx, w_in, OFF_FF, 512, _epi_logforget, lf_aux(0), (f32,)).reshape(B, N, 512)
    lfb_x = project(hx, w_in, OFF_FB, 512, _epi_logforget, lf_aux(1), (f32,)).reshape(B, N, 512)
    i_x = project(hx, w_in, OFF_I, 512, _epi_raw).reshape(B, N, 512)
    q_x = project(hx, w_in, OFF_HQ, 512, _epi_silu).reshape(B, N, 512)
    g_x = project(hx, w_in, OFF_HG, 512, _epi_silu).reshape(B, N, 512)
    k_x = project(hx, w_in, OFF_NK, 512, _epi_norm_rope, norm_aux + rope_aux, rows_per_seq=N).reshape(B, N, 512)
    v_x = project(hx, w_in, OFF_NV, 512, _epi_raw).reshape(B, N, 512)
    qn_x, qr_x = project(hx, w_in, OFF_NQ, 512, _epi_norm_both, [("col", qg), ("const", bd)] + rope_aux,
                         (bf16, bf16), rows_per_seq=N)
    p_x = project(hx, w_in, OFF_HY, 3 * HY_W, _epi_raw, out_dtypes=(f32,)).reshape(B, N, 3 * HY_W)
    gates_x = project(hx, w_in, OFF_GATE, 3 * D_MODEL, _epi_sigmoid)

    ox_f, ox_b, _, _ = hgrn_bidir(lff_x, lfb_x, i_x, q_x, s_f, s_b)

    nb_x = neighbourhood_attention(qr_x.reshape(B, N, 512), qn_x.reshape(B, N, 512), k_x, v_x, k_c, v_c, table)

    z_x, x0_x = hyena_pre(p_x, conv_w, conv_b)
    hy_x = hyena_conv(z_x, x0_x, spec[0], spec[1], skip, spec[2])

    flat = lambda a: a.reshape(-1, a.shape[-1])
    x_new = merge(flat(ox_f), flat(ox_b), flat(g_x), flat(nb_x), flat(hy_x), gates_x, wa, wb, wc, wo, x2d, mx2, N)
    if not need_ctx:
        return x_new, None

    qn_c = project(hc, w_in, OFF_NQ, 512, _epi_norm, [("col", qg), ("const", bd)], tm=tc).reshape(B, NC, 512)
    nb_c = context_attention(qn_c, k_c, v_c)
    p_c = project(hc, w_in, OFF_HY, 3 * HY_W, _epi_raw, out_dtypes=(f32,), tm=tc).reshape(B, NC, 3 * HY_W)
    hy_c = hyena_ctx(p_c, conv_w, conv_b, filt_c[0], filt_c[1], skip)
    gates_c = project(hc, w_in, OFF_GATE, 3 * D_MODEL, _epi_sigmoid, tm=tc)
    g_c = project(hc, w_in, OFF_HG, 512, _epi_silu, tm=tc)
    c_new = merge(flat(oc_f), flat(oc_b), g_c, flat(nb_c), flat(hy_c), gates_c, wa, wb, wc, wo, c2d, mc2, B * NC)
    return x_new, c_new


def kernel(x, c, ctx, c_ctx, w_mod, b_mod, norm_mix, norm_ffn, w_in, hg_lb, na_q_gain, na_k_gain, na_rpb,
           hy_conv_w, hy_conv_b, hy_pe_w1, hy_pe_b1, hy_pe_freq1, hy_pe_w2, hy_pe_b2, hy_pe_freq2, hy_pe_w3,
           hy_skip, w_branch_a, w_branch_b, w_branch_c, w_out, w_router, w_e_gate, w_e_up, w_e_down):
    B, N, D = x.shape
    NC = ctx.shape[1]
    E = N_EXPERTS
    cap_x = EC_CAP_FACTOR * N // E
    cap_c = EC_CAP_FACTOR * NC // E

    lb = jnp.cumsum(jax.nn.softmax(hg_lb.astype(f32), axis=0), axis=0)
    lb = lb - lb[:1]
    la_all, lc_all = jnp.log(lb), jnp.log1p(-lb)

    s8 = jnp.zeros((8, D), f32).at[:B].set(c).at[B].set(c_ctx)
    rope = _rope_tables(N)
    bd = jnp.asarray(np.kron(np.eye(NA_HEADS), np.full((NA_HD, NA_HD), 1.0 / NA_HD)).astype(np.float32), bf16)
    dcx = _dft_consts(N)

    x2d = x.reshape(B * N, D)
    c2d = ctx.reshape(B * NC, D)
    for l in range(DEPTH):
        need_ctx = l < DEPTH - 1
        mv = modvec(s8, w_mod[l], b_mod[l])
        mx = [mv[:B, k * D:(k + 1) * D] for k in range(6)]
        mc = [mv[B:B + 1, k * D:(k + 1) * D] for k in range(6)]
        w_in_l = w_in[l].astype(bf16)
        hx = modulate(x2d, norm_mix[l], mx[0], mx[1], N, bf16)
        hc = modulate(c2d, norm_mix[l], mc[0], mc[1], B * NC, bf16)

        filt = (hy_pe_w1[l], hy_pe_b1[l], hy_pe_freq1[l], hy_pe_w2[l], hy_pe_b2[l], hy_pe_freq2[l], hy_pe_w3[l])
        k_x, nrm_x = hyena_filter(N, *filt)
        sre, sim = hyena_spectrum(k_x, 1.0 / nrm_x, dcx)
        skip = hy_skip[l].reshape(1, HY_W)
        filt_c = None
        if need_ctx:
            h_c, nrm_c = hyena_filter(NC, *filt)
            filt_c = (h_c, 1.0 / nrm_c)

        x2d, c_new = _mixing(
            hx, hc, need_ctx, B, N, NC, la_all[l], lc_all[l], w_in_l, na_q_gain[l], na_k_gain[l],
            _na_bias_table(na_rpb[l]), rope, bd, hy_conv_w[l], hy_conv_b[l], (sre, sim, dcx), skip,
            filt_c, w_branch_a[l].astype(bf16), w_branch_b[l].astype(bf16), w_branch_c[l].astype(bf16),
            w_out[l].astype(bf16), x2d, c2d, mx[2], mc[2])

        h2, aff_t, aff_m = router(x2d, norm_ffn[l], mx[3], mx[4], w_router[l], N)
        posm, cnt = select_topk(aff_t, cap_x)
        fast_ok = _fits_fast(cnt, N)
        xgs = [gather_rows(cnt, h2, posm, cap_x, 1024, fast_ok)]
        if need_ctx:
            c2d = c_new
            hc2, aff_tc, aff_mc = router(c2d, norm_ffn[l], mc[3], mc[4], w_router[l], NC)
            posm_c, cnt_c = select_topk(aff_tc, cap_c)
            xgs.append(gather_rows(cnt_c, hc2, posm_c, cap_c, NC))
        ys = expert_ffn(xgs, l, w_e_gate, w_e_up, w_e_down)
        x2d = combine(cnt, x2d, posm, aff_m, mx[5], ys[0], cap_x, CMB_TILE, fast_ok)
        if need_ctx:
            c2d = combine(cnt_c, c2d, posm_c, aff_mc, mc[5], ys[1], cap_c, NC)
    return x2d.reshape(B, N, D)
```

```python
import functools
import math

import numpy as np
import jax
import jax.numpy as jnp
from jax import lax
from jax.experimental import pallas as pl
from jax.experimental.pallas import tpu as pltpu

f32 = jnp.float32
bf16 = jnp.bfloat16
i32 = jnp.int32
HI = lax.Precision.HIGHEST

D_MODEL = 1024
DEPTH = 2
GRID_W = 64
EPS = 1e-6
HG_HEADS = 4
HG_W = 512
HG_CHUNK = 64
NA_HEADS = 8
NA_HD = 64
NA_W = 512
NA_WIN_R = 8
NA_WIN_C = 16
ROPE_THETA = 10000.0
HY_W = 512
HY_BANDS = 16
HY_PE_DIM = 1 + 2 * HY_BANDS
HY_FILT_HID = 64
HY_FAST_DECAY = 0.3
HY_SLOW_DECAY = 1.5
HY_TARGET = 1e-2
OFF_FF = 0
OFF_FB = 512
OFF_I = 1024
OFF_NK = 1536
OFF_NV = 2048
OFF_HQ = 2560
OFF_NQ = 3072
OFF_HG = 3584
OFF_HY = 4096
OFF_GATE = 5632
IN_COLS = 8704
N_EXPERTS = 16
EC_CAP_FACTOR = 2
D_FF_EXPERT = 2816

LANES = 128
NEG_BIG = -1e30
VMEM_LIMIT = 56 * 1024 * 1024


def _cparams(sem, vmem=VMEM_LIMIT):
    return pltpu.CompilerParams(dimension_semantics=sem, vmem_limit_bytes=vmem)


def _nt(a, b, precision=None):
    return lax.dot_general(a, b, (((1,), (1,)), ((), ())), precision=precision, preferred_element_type=f32)


def _tn(a, b, precision=None):
    return lax.dot_general(a, b, (((0,), (0,)), ((), ())), precision=precision, preferred_element_type=f32)


def _silu(x):
    return x * jax.nn.sigmoid(x)


def _split(x):
    hi = x.astype(bf16)
    return hi, (x - hi.astype(f32)).astype(bf16)


def _modvec_kernel(s_ref, w_ref, b_ref, o_ref):
    s = _silu(s_ref[...])
    o_ref[...] = jnp.dot(s, w_ref[...], precision=HI, preferred_element_type=f32) + b_ref[...]


def modvec(s8, w, b):
    n = w.shape[1]
    tn = 1024
    return pl.pallas_call(
        _modvec_kernel,
        grid=(n // tn,),
        in_specs=[pl.BlockSpec((8, D_MODEL), lambda j: (0, 0)),
                  pl.BlockSpec((D_MODEL, tn), lambda j: (0, j)),
                  pl.BlockSpec((1, tn), lambda j: (0, j))],
        out_specs=pl.BlockSpec((8, tn), lambda j: (0, j)),
        out_shape=jax.ShapeDtypeStruct((8, n), f32),
        compiler_params=_cparams(("parallel",)),
        name="modvec",
    )(s8, w, b.reshape(1, n))


def _modulate_kernel(x_ref, g_ref, sh_ref, sc_ref, o_ref):
    x = x_ref[...]
    ms = jnp.mean(x * x, axis=-1, keepdims=True)
    y = x * lax.rsqrt(ms + EPS)
    o_ref[...] = (y * g_ref[...] * (1.0 + sc_ref[0]) + sh_ref[0]).astype(o_ref.dtype)


def modulate(x2d, g, shift, scale, rows_per_group, out_dtype):
    R = x2d.shape[0]
    tm = 512
    tpg = rows_per_group // tm
    G = shift.shape[0]
    return pl.pallas_call(
        _modulate_kernel,
        grid=(R // tm,),
        in_specs=[pl.BlockSpec((tm, D_MODEL), lambda i: (i, 0)),
                  pl.BlockSpec((1, D_MODEL), lambda i: (0, 0)),
                  pl.BlockSpec((1, 1, D_MODEL), lambda i: (i // tpg, 0, 0)),
                  pl.BlockSpec((1, 1, D_MODEL), lambda i: (i // tpg, 0, 0))],
        out_specs=pl.BlockSpec((tm, D_MODEL), lambda i: (i, 0)),
        out_shape=jax.ShapeDtypeStruct((R, D_MODEL), out_dtype),
        compiler_params=_cparams(("parallel",)),
        name="modulate",
    )(x2d, g.reshape(1, D_MODEL), shift.reshape(G, 1, D_MODEL), scale.reshape(G, 1, D_MODEL))


def _log_sigmoid(z):
    return jnp.minimum(z, 0.0) - jnp.log1p(jnp.exp(-jnp.abs(z)))


def _epi_raw(acc, o_ref):
    o_ref[...] = acc.astype(o_ref.dtype)


def _epi_silu(acc, o_ref):
    o_ref[...] = _silu(acc).astype(o_ref.dtype)


def _epi_sigmoid(acc, o_ref):
    o_ref[...] = jax.nn.sigmoid(acc).astype(o_ref.dtype)


def _epi_logforget(acc, la_ref, lc_ref, o_ref):
    la = la_ref[...]
    c = lc_ref[...] + _log_sigmoid(acc)
    o_ref[...] = jnp.maximum(la, c) + jnp.log1p(jnp.exp(-jnp.abs(la - c)))


def _head_rms(acc, gain_ref, bd_ref):
    hi, lo = _split(acc * acc)
    ms = jnp.dot(hi, bd_ref[...], preferred_element_type=f32) + jnp.dot(lo, bd_ref[...], preferred_element_type=f32)
    return acc * lax.rsqrt(ms + EPS) * gain_ref[...]


def _rope(y, c_ref, s1_ref, s2_ref):
    reps = y.shape[1] // LANES
    c = jnp.concatenate([c_ref[...]] * reps, axis=1)
    s1 = jnp.concatenate([s1_ref[...]] * reps, axis=1)
    s2 = jnp.concatenate([s2_ref[...]] * reps, axis=1)
    w = y.shape[1]
    return y * c + pltpu.roll(y, w - 16, axis=1) * s1 + pltpu.roll(y, 16, axis=1) * s2


def _epi_norm(acc, gain_ref, bd_ref, o_ref):
    o_ref[...] = _head_rms(acc, gain_ref, bd_ref).astype(o_ref.dtype)


def _epi_norm_rope(acc, gain_ref, bd_ref, c_ref, s1_ref, s2_ref, o_ref):
    y = _head_rms(acc, gain_ref, bd_ref)
    o_ref[...] = _rope(y, c_ref, s1_ref, s2_ref).astype(o_ref.dtype)


def _epi_norm_both(acc, gain_ref, bd_ref, c_ref, s1_ref, s2_ref, on_ref, or_ref):
    y = _head_rms(acc, gain_ref, bd_ref)
    on_ref[...] = y.astype(on_ref.dtype)
    or_ref[...] = _rope(y, c_ref, s1_ref, s2_ref).astype(or_ref.dtype)


def _proj_kernel(h_ref, w_ref, *rest, epi):
    acc = jnp.dot(h_ref[...], w_ref[...], preferred_element_type=f32)
    epi(acc, *rest)


def project(h, w, c0, width, epi, aux=(), out_dtypes=(bf16,), tm=1024, rows_per_seq=None):
    R = h.shape[0]
    tn = 512
    nj = width // tn
    cb = c0 // tn
    in_specs = [pl.BlockSpec((tm, D_MODEL), lambda i, j: (i, 0)),
                pl.BlockSpec((D_MODEL, tn), lambda i, j: (0, cb + j))]
    args = [h, w]
    for kind, arr in aux:
        if kind == "col":
            in_specs.append(pl.BlockSpec((1, tn), lambda i, j: (0, j)))
        elif kind == "const":
            in_specs.append(pl.BlockSpec(arr.shape, lambda i, j: (0, 0)))
        else:
            tps = rows_per_seq // tm
            in_specs.append(pl.BlockSpec((tm, LANES), lambda i, j: (i % tps, 0)))
        args.append(arr)
    out_specs = [pl.BlockSpec((tm, tn), lambda i, j: (i, j)) for _ in out_dtypes]
    out_shape = [jax.ShapeDtypeStruct((R, width), dt) for dt in out_dtypes]
    res = pl.pallas_call(
        functools.partial(_proj_kernel, epi=epi),
        grid=(R // tm, nj),
        in_specs=in_specs,
        out_specs=out_specs,
        out_shape=out_shape,
        compiler_params=_cparams(("parallel", "parallel")),
        name="proj_" + epi.__name__[5:],
    )(*args)
    return res[0] if len(res) == 1 else res


def _hgrn_tmatrix(C, reverse):
    L = int(round(math.log2(C)))
    t = np.arange(C)
    tau = (C - 1 - t) if reverse else t
    tt, uu = tau[:, None], tau[None, :]
    T = np.zeros((2 + L, C, C), np.float32)
    T[0] = uu <= tt
    T[1] = uu > tt
    for l in range(L):
        same = (tt >> (l + 1)) == (uu >> (l + 1))
        tr = ((tt >> l) & 1) == 1
        ur = ((uu >> l) & 1) == 1
        T[2 + l] = same & ((tr & ur & (uu <= tt)) | (~tr & ~ur & (uu > tt)))
    return T.reshape((2 + L) * C, C)


def _hgrn_step(chains, C):
    L = int(round(math.log2(C)))
    row = lax.broadcasted_iota(i32, (C, LANES), 0)
    ti0 = lax.broadcasted_iota(i32, (C, C), 0)
    si0 = lax.broadcasted_iota(i32, (C, C), 1)

    es, kcs = [], []
    for lf, _, _, tm, _, _ in chains:
        W = lf.shape[1]
        hi, lo = _split(lf)
        r = jnp.dot(tm, jnp.concatenate([hi, lo], axis=1), preferred_element_type=f32)
        es.append(r[:, :W] + r[:, W:])
        kcs.append(1.0 - jnp.exp(lf))

    items = []
    for (lf, v_all, q_all, _, s_view, reverse), E, kc_all in zip(chains, es, kcs):
        tau = (C - 1 - row) if reverse else row
        ti, si = (C - 1 - ti0, C - 1 - si0) if reverse else (ti0, si0)
        end_row = 0 if reverse else C - 1
        for h in range(HG_HEADS):
            sl = slice(LANES * h, LANES * (h + 1))
            qh, kch, vh = q_all[:, sl], kc_all[:, sl], v_all[:, sl]
            b = E[0:C, sl]
            st = s_view[h]
            pairs = [(qh.astype(bf16), kch.astype(bf16), ti == si)]
            for l in range(L):
                x = jnp.exp(E[(2 + l) * C:(3 + l) * C, sl])
                later = ((tau >> l) & 1) == 1
                pairs.append((jnp.where(later, qh * x, 0.0).astype(bf16), jnp.where(later, 0.0, kch * x).astype(bf16),
                              (ti >> (l + 1)) == (si >> (l + 1))))
            items.append(dict(qb=(qh * jnp.exp(b)).astype(bf16), st=st, pairs=pairs, vh=vh,
                              kd=(kch * jnp.exp(E[C:2 * C, sl])).astype(bf16),
                              decay=jnp.exp(b[end_row:end_row + 1, :]), view=s_view, h=h))

    for it in items:
        it["o"] = _nt(it["qb"], it["st"].astype(bf16))
        it["att"] = [(_nt(ql, kl), m) for ql, kl, m in it["pairs"]]

    outs = []
    for it in items:
        att = None
        for a, m in it["att"]:
            t = jnp.where(m, a, 0.0)
            att = t if att is None else att + t
        it["o"] = it["o"] + jnp.dot(att.astype(bf16), it["vh"], preferred_element_type=f32)
        it["view"][it["h"]] = it["st"] * it["decay"] + _tn(it["vh"], it["kd"])
    nh = HG_HEADS
    for c in range(len(chains)):
        outs.append(jnp.concatenate([items[c * nh + h]["o"] for h in range(nh)], axis=1))
    return outs


def _hgrn_kernel(lff_ref, lfb_ref, vf_ref, vb_ref, qf_ref, qb_ref, s0f_ref, s0b_ref, tf_ref, tb_ref,
                 of_ref, ob_ref, sff_ref, sfb_ref, s_scr, *, C, B):
    c = pl.program_id(0)

    @pl.when(c == 0)
    def _():
        s_scr[0] = s0f_ref[...]
        s_scr[1] = s0b_ref[...]

    chains = []
    for b in range(B):
        chains.append((lff_ref[b], vf_ref[b], qf_ref[b].astype(f32), tf_ref[...], s_scr.at[0, b], False))
        chains.append((lfb_ref[b], vb_ref[b], qb_ref[b].astype(f32), tb_ref[...], s_scr.at[1, b], True))
    outs = _hgrn_step(chains, C)
    for b in range(B):
        of_ref[b] = outs[2 * b]
        ob_ref[b] = outs[2 * b + 1]

    @pl.when(c == pl.num_programs(0) - 1)
    def _():
        sff_ref[...] = s_scr[0]
        sfb_ref[...] = s_scr[1]


def hgrn_bidir(lf_f, lf_b, v, q, s0_f, s0_b):
    B, N, W = lf_f.shape
    C = HG_CHUNK
    nch = N // C
    tf = jnp.asarray(_hgrn_tmatrix(C, False), bf16)
    tb = jnp.asarray(_hgrn_tmatrix(C, True), bf16)
    fw = pl.BlockSpec((B, C, W), lambda c: (0, c, 0))
    bw = pl.BlockSpec((B, C, W), lambda c: (0, nch - 1 - c, 0))
    st = pl.BlockSpec((B, HG_HEADS, LANES, LANES), lambda c: (0, 0, 0, 0))
    tsp = pl.BlockSpec(tf.shape, lambda c: (0, 0))
    seq = jax.ShapeDtypeStruct((B, N, W), f32)
    sts = jax.ShapeDtypeStruct((B, HG_HEADS, LANES, LANES), f32)
    return pl.pallas_call(
        functools.partial(_hgrn_kernel, C=C, B=B),
        grid=(nch,),
        in_specs=[fw, bw, fw, bw, fw, bw, st, st, tsp, tsp],
        out_specs=[fw, bw, st, st],
        out_shape=[seq, seq, sts, sts],
        scratch_shapes=[pltpu.VMEM((2, B, HG_HEADS, LANES, LANES), f32)],
        compiler_params=_cparams(("arbitrary",)),
        name="hgrn",
    )(lf_f, lf_b, v, v, q, q, s0_f, s0_b, tf, tb)


def _na_kernel(qr_ref, qn_ref, k_ref, v_ref, kc_ref, vc_ref, tab_ref, o_ref, s_scr, p_scr, *, rows_per_step, n_rows):
    g = pl.program_id(2)
    scale = NA_HD ** -0.5
    lane = lax.broadcasted_iota(i32, (GRID_W, LANES), 1)
    kcx = kc_ref[0]
    vcx = vc_ref[0]
    win = NA_WIN_R * GRID_W
    ctx_len = kcx.shape[0]

    starts = []
    for i in range(rows_per_step):
        r = g * rows_per_step + i
        rs = jnp.clip(r - NA_WIN_R // 2, 0, n_rows - NA_WIN_R)
        off = rs - r + (NA_WIN_R - 1)
        start = pl.multiple_of(rs * GRID_W, GRID_W)
        starts.append(start)
        kw = k_ref[0, pl.ds(start, win), :]
        qr = qr_ref[0, i * GRID_W:(i + 1) * GRID_W, :]
        qn = qn_ref[0, i * GRID_W:(i + 1) * GRID_W, :]
        for hh in range(2):
            m = (lane >= NA_HD * hh) & (lane < NA_HD * (hh + 1))
            qrm = jnp.where(m, qr, jnp.zeros_like(qr))
            qnm = jnp.where(m, qn, jnp.zeros_like(qn))
            row0 = (2 * i + hh) * GRID_W
            s_scr[row0:row0 + GRID_W, 0:win] = _nt(qrm, kw) * scale + tab_ref[hh, off]
            s_scr[row0:row0 + GRID_W, win:win + ctx_len] = _nt(qnm, kcx) * scale

    def softmax_rows(c, carry):
        r0 = pl.multiple_of(c * LANES, LANES)
        s = s_scr[pl.ds(r0, LANES), :]
        p = jnp.exp(s - jnp.max(s, axis=-1, keepdims=True))
        inv = 1.0 / jnp.sum(p, axis=-1, keepdims=True)
        p_scr[pl.ds(r0, LANES), :] = (p * inv).astype(bf16)
        return carry

    lax.fori_loop(0, 2 * rows_per_step * GRID_W // LANES, softmax_rows, 0, unroll=2)

    for i in range(rows_per_step):
        vw = v_ref[0, pl.ds(starts[i], win), :]
        res = []
        for hh in range(2):
            row0 = (2 * i + hh) * GRID_W
            p = p_scr[row0:row0 + GRID_W, :]
            res.append(jnp.dot(p[:, :win], vw, preferred_element_type=f32)
                       + jnp.dot(p[:, win:], vcx, preferred_element_type=f32))
        o_ref[0, i * GRID_W:(i + 1) * GRID_W, :] = jnp.where(lane < NA_HD, res[0], res[1]).astype(o_ref.dtype)


def _na_bias_table(rpb):
    col = jnp.arange(GRID_W)
    cs = jnp.clip(col - NA_WIN_C // 2, 0, GRID_W - NA_WIN_C)
    kc = jnp.arange(GRID_W)
    valid = (kc[None, :] >= cs[:, None]) & (kc[None, :] < cs[:, None] + NA_WIN_C)
    dc = jnp.clip(kc[None, :] - col[:, None] + (NA_WIN_C - 1), 0, 2 * NA_WIN_C - 2)
    bc = jnp.where(valid[None, None], rpb[:, :, dc], NEG_BIG)
    t2 = jnp.stack([bc[:, o:o + NA_WIN_R] for o in range(NA_WIN_R)], axis=1)
    t2 = t2.transpose(0, 1, 3, 2, 4)
    return t2.reshape(NA_HEADS, NA_WIN_R, GRID_W, NA_WIN_R * GRID_W).astype(f32)


def neighbourhood_attention(q_rot, qn, k_rot, v, kc, vc, table):
    B, N, W = q_rot.shape
    n_rows = N // GRID_W
    rps = 8
    ctx_len = kc.shape[1]
    pairs = W // LANES
    keys = NA_WIN_R * GRID_W + ctx_len
    return pl.pallas_call(
        functools.partial(_na_kernel, rows_per_step=rps, n_rows=n_rows),
        grid=(B, pairs, n_rows // rps),
        in_specs=[pl.BlockSpec((1, rps * GRID_W, LANES), lambda b, p, g: (b, g, p)),
                  pl.BlockSpec((1, rps * GRID_W, LANES), lambda b, p, g: (b, g, p)),
                  pl.BlockSpec((1, N, LANES), lambda b, p, g: (b, 0, p)),
                  pl.BlockSpec((1, N, LANES), lambda b, p, g: (b, 0, p)),
                  pl.BlockSpec((1, ctx_len, LANES), lambda b, p, g: (b, 0, p)),
                  pl.BlockSpec((1, ctx_len, LANES), lambda b, p, g: (b, 0, p)),
                  pl.BlockSpec((2, NA_WIN_R, GRID_W, NA_WIN_R * GRID_W), lambda b, p, g: (p, 0, 0, 0))],
        out_specs=pl.BlockSpec((1, rps * GRID_W, LANES), lambda b, p, g: (b, g, p)),
        out_shape=jax.ShapeDtypeStruct((B, N, W), bf16),
        scratch_shapes=[pltpu.VMEM((2 * rps * GRID_W, keys), f32), pltpu.VMEM((2 * rps * GRID_W, keys), bf16)],
        compiler_params=_cparams(("parallel", "parallel", "arbitrary")),
        name="natten",
    )(q_rot, qn, k_rot, v, kc, vc, table)


def _ctx_attn_kernel(q_ref, k_ref, v_ref, o_ref):
    scale = NA_HD ** -0.5
    q = q_ref[0]
    k = k_ref[0]
    v = v_ref[0]
    lane = lax.broadcasted_iota(i32, q.shape, 1)
    res = []
    for hh in range(2):
        m = (lane >= NA_HD * hh) & (lane < NA_HD * (hh + 1))
        s = _nt(jnp.where(m, q, jnp.zeros_like(q)), k) * scale
        p = jnp.exp(s - jnp.max(s, axis=-1, keepdims=True))
        p = p / jnp.sum(p, axis=-1, keepdims=True)
        res.append(jnp.dot(p.astype(bf16), v, preferred_element_type=f32))
    o_ref[0] = jnp.where(lane < NA_HD, res[0], res[1]).astype(o_ref.dtype)


def context_attention(q, k, v):
    B, N, W = q.shape
    spec = pl.BlockSpec((1, N, LANES), lambda b, p: (b, 0, p))
    return pl.pallas_call(
        _ctx_attn_kernel,
        grid=(B, W // LANES),
        in_specs=[spec, spec, spec],
        out_specs=spec,
        out_shape=jax.ShapeDtypeStruct((B, N, W), bf16),
        compiler_params=_cparams(("parallel", "parallel")),
        name="ctx_attn",
    )(q, k, v)


def _filter_kernel(z_ref, w1_ref, b1_ref, f1_ref, w2_ref, b2_ref, f2_ref, w3_ref, dl_ref, k_ref, nrm_ref, *, tm, n):
    i = pl.program_id(0)
    z = z_ref[...]
    a = jnp.sin(f1_ref[...] * (jnp.dot(z, w1_ref[...], precision=HI, preferred_element_type=f32) + b1_ref[...]))
    a = jnp.sin(f2_ref[...] * (jnp.dot(a, w2_ref[...], precision=HI, preferred_element_type=f32) + b2_ref[...]))
    h = jnp.dot(a, w3_ref[...], precision=HI, preferred_element_type=f32)
    h = h * jnp.exp(-z[:, 0:1] * dl_ref[...])
    row = lax.broadcasted_iota(i32, (tm, HY_W), 0) + i * tm
    k = jnp.where(row == n, 0.0, h)
    k_ref[...] = k
    part = jnp.sum(jnp.abs(k), axis=0, keepdims=True)

    @pl.when(i == 0)
    def _():
        nrm_ref[...] = part

    @pl.when(i > 0)
    def _():
        nrm_ref[...] = nrm_ref[...] + part


def hyena_filter(n, w1, b1, fr1, w2, b2, fr2, w3):
    t = jnp.linspace(0.0, 1.0, n, dtype=f32)[:, None]
    w = 2 * math.pi * jnp.arange(n, dtype=f32)[:, None] / n
    fb = jnp.linspace(1e-4, HY_BANDS - 1, HY_BANDS, dtype=f32)[None]
    z = jnp.concatenate([t, jnp.cos(fb * w), -jnp.sin(fb * w)], axis=-1)
    z = jnp.concatenate([z, jnp.zeros((1, HY_PE_DIM), f32), z[:0:-1]], axis=0)
    z = jnp.pad(z, ((0, 0), (0, LANES - HY_PE_DIM)))
    w1p = jnp.pad(w1.astype(f32), ((0, LANES - HY_PE_DIM), (0, 0)))
    deltas = jnp.abs(jnp.linspace(math.log(HY_TARGET) / HY_SLOW_DECAY, math.log(HY_TARGET) / HY_FAST_DECAY,
                                  2 * HY_W, dtype=f32))[None]
    tm = min(n, 512)
    hid = HY_FILT_HID
    full = lambda shape: pl.BlockSpec(shape, lambda i: (0, 0))
    tph = n // tm
    return pl.pallas_call(
        functools.partial(_filter_kernel, tm=tm, n=n),
        grid=(2 * n // tm,),
        in_specs=[pl.BlockSpec((tm, LANES), lambda i: (i, 0)),
                  full((LANES, hid)), full((1, hid)), full((1, hid)),
                  full((hid, hid)), full((1, hid)), full((1, hid)),
                  pl.BlockSpec((hid, HY_W), lambda i: (0, i // tph)),
                  pl.BlockSpec((1, HY_W), lambda i: (0, i // tph))],
        out_specs=[pl.BlockSpec((tm, HY_W), lambda i: (i, 0)), pl.BlockSpec((1, HY_W), lambda i: (0, 0))],
        out_shape=[jax.ShapeDtypeStruct((2 * n, HY_W), f32), jax.ShapeDtypeStruct((1, HY_W), f32)],
        compiler_params=_cparams(("arbitrary",)),
        name="hyena_filter",
    )(z, w1p, b1.reshape(1, hid), fr1.reshape(1, hid), w2.astype(f32), b2.reshape(1, hid), fr2.reshape(1, hid),
      w3.astype(f32), deltas)


def _conv3(u, w_ref, b_ref):
    n = u.shape[0]
    row = lax.broadcasted_iota(i32, u.shape, 0)
    prev = jnp.where(row == 0, 0.0, pltpu.roll(u, 1, axis=0))
    nxt = jnp.where(row == n - 1, 0.0, pltpu.roll(u, n - 1, axis=0))
    return prev * w_ref[0:1, :] + u * w_ref[1:2, :] + nxt * w_ref[2:3, :] + b_ref[...]


def _hyena_pre_kernel(p0_ref, p1_ref, p2_ref, w0_ref, w1_ref, w2_ref, b0_ref, b1_ref, b2_ref, z_ref, x0_ref):
    x0_ref[0] = _conv3(p0_ref[0], w0_ref, b0_ref).astype(x0_ref.dtype)
    z_ref[0] = _conv3(p1_ref[0], w1_ref, b1_ref) * _conv3(p2_ref[0], w2_ref, b2_ref)


def hyena_pre(p, conv_w, conv_b):
    B, N, _ = p.shape
    nb = HY_W // LANES
    conv_b = conv_b.reshape(1, 3 * HY_W)
    pspec = lambda g: pl.BlockSpec((1, N, LANES), lambda b, c: (b, 0, g * nb + c))
    wspec = lambda g: pl.BlockSpec((3, LANES), lambda b, c: (0, g * nb + c))
    bspec = lambda g: pl.BlockSpec((1, LANES), lambda b, c: (0, g * nb + c))
    ospec = pl.BlockSpec((1, N, LANES), lambda b, c: (b, 0, c))
    return pl.pallas_call(
        _hyena_pre_kernel,
        grid=(B, nb),
        in_specs=[pspec(0), pspec(1), pspec(2), wspec(0), wspec(1), wspec(2), bspec(0), bspec(1), bspec(2)],
        out_specs=[ospec, ospec],
        out_shape=[jax.ShapeDtypeStruct((B, N, HY_W), f32), jax.ShapeDtypeStruct((B, N, HY_W), bf16)],
        compiler_params=_cparams(("parallel", "parallel")),
        name="hyena_pre",
    )(p, p, p, conv_w, conv_w, conv_w, conv_b, conv_b, conv_b)


def _hl(a):
    a32 = jnp.asarray(a.astype(np.float32))
    hi = a32.astype(bf16)
    lo = (a32 - hi.astype(f32)).astype(bf16)
    return jnp.concatenate([hi, lo], axis=-2)


def _dot3(a_hl, m, x):
    xh, xl = _split(x)
    r = jnp.dot(a_hl, xh, preferred_element_type=f32)
    return r[:m] + r[m:] + jnp.dot(a_hl[:m], xl, preferred_element_type=f32)


def _dft_consts(n):
    N = 2 * n
    na = N // LANES
    t1n = na // 2
    k1n = na // 2 + 1
    k1p = -(-k1n // 8) * 8
    k1 = np.arange(k1n)
    t1 = np.arange(t1n)
    th = 2 * np.pi * ((t1[None, :] * k1[:, None]) % na) / na
    f1c = np.zeros((2 * k1p, t1n))
    f1c[:k1n] = np.cos(th)
    f1c[k1p:k1p + k1n] = -np.sin(th)
    thf = 2 * np.pi * ((np.arange(na)[None, :] * k1[:, None]) % na) / na
    f1f = np.zeros((2 * k1p, na))
    f1f[:k1n] = np.cos(thf)
    f1f[k1p:k1p + k1n] = -np.sin(thf)
    k2 = np.arange(LANES)
    t2 = np.arange(LANES)
    m = (t2[None, None, :] * (k1[:, None, None] + na * k2[None, :, None])) % N
    ph = 2 * np.pi * m / N
    g = np.concatenate([np.cos(ph), -np.sin(ph)], axis=1)
    pht = ph.transpose(0, 2, 1)
    gi = np.concatenate([np.cos(pht), np.sin(pht)], axis=1)
    wk = np.where((k1 == 0) | (k1 == na // 2), 1.0, 2.0) / N
    f1i = np.zeros((t1n, 2 * k1p))
    f1i[:, :k1n] = np.cos(th.T) * wk[None, :]
    f1i[:, k1p:k1p + k1n] = -np.sin(th.T) * wk[None, :]
    k1e = k1n + k1n % 2
    g = np.concatenate([g, np.zeros((k1e - k1n,) + g.shape[1:])], axis=0)
    gi = np.concatenate([gi, np.zeros((k1e - k1n,) + gi.shape[1:])], axis=0)
    return dict(na=na, t1n=t1n, k1n=k1n, k1e=k1e, k1p=k1p, f1c=_hl(f1c), f1f=_hl(f1f), g=_hl(g), gi=_hl(gi),
                f1i=_hl(f1i))


def _dft_stage1(src_ref, f1c_ref, are_ref, aim_ref, t1n, k1p):
    f1c = f1c_ref[...]

    def body(t2, carry):
        zs = src_ref[pl.ds(t2, t1n, stride=LANES), :]
        r = _dot3(f1c, 2 * k1p, zs)
        are_ref[pl.ds(t2, k1p, stride=LANES), :] = r[:k1p]
        aim_ref[pl.ds(t2, k1p, stride=LANES), :] = r[k1p:]
        return carry

    lax.fori_loop(0, LANES, body, 0, unroll=4)


def _cplx_left(gc_hl, xre, xim):
    cw = xre.shape[1]
    r = _dot3(gc_hl, 2 * LANES, jnp.concatenate([xre, xim], axis=1))
    p, q = r[:, :cw], r[:, cw:]
    return p[:LANES] - q[LANES:], p[LANES:] + q[:LANES]


def _spectrum_kernel(k_ref, f1f_ref, g_ref, inv_ref, xre_ref, xim_ref, are, aim, *, na, k1p):
    j = pl.program_id(1)

    @pl.when(j == 0)
    def _():
        _dft_stage1(k_ref, f1f_ref, are, aim, na, k1p)

    for half in range(2):
        r0 = pl.multiple_of((2 * j + half) * LANES, LANES)
        xre, xim = _cplx_left(g_ref[half], are[pl.ds(r0, LANES), :], aim[pl.ds(r0, LANES), :])
        rows = slice(half * LANES, (half + 1) * LANES)
        xre_ref[rows, :] = xre * inv_ref[...]
        xim_ref[rows, :] = xim * inv_ref[...]


def hyena_spectrum(k, inv_norm, dc):
    n2, C = k.shape
    k1e, k1p, na = dc["k1e"], dc["k1p"], dc["na"]
    cw = LANES
    out = jax.ShapeDtypeStruct((k1e * LANES, C), f32)
    ospec = pl.BlockSpec((2 * LANES, cw), lambda c, k: (k, c))
    return pl.pallas_call(
        functools.partial(_spectrum_kernel, na=na, k1p=k1p),
        grid=(C // cw, k1e // 2),
        in_specs=[pl.BlockSpec((n2, cw), lambda c, k: (0, c)),
                  pl.BlockSpec(dc["f1f"].shape, lambda c, k: (0, 0)),
                  pl.BlockSpec((2, 4 * LANES, LANES), lambda c, k: (k, 0, 0)),
                  pl.BlockSpec((1, cw), lambda c, k: (0, c))],
        out_specs=[ospec, ospec],
        out_shape=[out, out],
        scratch_shapes=[pltpu.VMEM((k1p * LANES, cw), f32), pltpu.VMEM((k1p * LANES, cw), f32)],
        compiler_params=_cparams(("parallel", "arbitrary")),
        name="hyena_spectrum",
    )(k, dc["f1f"], dc["g"], inv_norm)


def _hyena_conv_kernel(z_ref, x0_ref, f1c_ref, g_ref, gi_ref, f1i_ref, kre_ref, kim_ref,
                       skip_ref, o_ref, are, aim, y_scr, *, t1n, k1p):
    j = pl.program_id(2)

    @pl.when(j == 0)
    def _():
        _dft_stage1(z_ref.at[0], f1c_ref, are, aim, t1n, k1p)

    for half in range(2):
        r0 = pl.multiple_of((2 * j + half) * LANES, LANES)
        xre, xim = _cplx_left(g_ref[half], are[pl.ds(r0, LANES), :], aim[pl.ds(r0, LANES), :])
        rows = slice(half * LANES, (half + 1) * LANES)
        kre = kre_ref[rows, :]
        kim = kim_ref[rows, :]
        yre = xre * kre - xim * kim
        yim = xre * kim + xim * kre
        bre, bim = _cplx_left(gi_ref[half], yre, yim)
        are[pl.ds(r0, LANES), :] = bre
        aim[pl.ds(r0, LANES), :] = bim

    @pl.when(j == pl.num_programs(2) - 1)
    def _():
        f1i = f1i_ref[...]

        def body(t2, carry):
            bb = jnp.concatenate([are[pl.ds(t2, k1p, stride=LANES), :], aim[pl.ds(t2, k1p, stride=LANES), :]], axis=0)
            y_scr[pl.ds(t2, t1n, stride=LANES), :] = _dot3(f1i, t1n, bb)
            return carry

        lax.fori_loop(0, LANES, body, 0, unroll=4)
        z = z_ref[0]
        o_ref[0] = (x0_ref[0].astype(f32) * (y_scr[...] + z * skip_ref[...])).astype(o_ref.dtype)


def hyena_conv(z, x0, spec_re, spec_im, skip, dc):
    B, n, W = z.shape
    nb = W // LANES
    k1e, k1p, t1n = dc["k1e"], dc["k1p"], dc["t1n"]
    seq = pl.BlockSpec((1, n, LANES), lambda b, c, k: (b, 0, c))
    fspec = pl.BlockSpec((2 * LANES, LANES), lambda b, c, k: (k, c))
    cspec = pl.BlockSpec((2, 4 * LANES, LANES), lambda b, c, k: (k, 0, 0))
    vspec = pl.BlockSpec((1, LANES), lambda b, c, k: (0, c))
    return pl.pallas_call(
        functools.partial(_hyena_conv_kernel, t1n=t1n, k1p=k1p),
        grid=(B, nb, k1e // 2),
        in_specs=[seq, seq,
                  pl.BlockSpec(dc["f1c"].shape, lambda b, c, k: (0, 0)), cspec, cspec,
                  pl.BlockSpec(dc["f1i"].shape, lambda b, c, k: (0, 0)),
                  fspec, fspec, vspec],
        out_specs=seq,
        out_shape=jax.ShapeDtypeStruct((B, n, W), bf16),
        scratch_shapes=[pltpu.VMEM((k1p * LANES, LANES), f32), pltpu.VMEM((k1p * LANES, LANES), f32),
                        pltpu.VMEM((n, LANES), f32)],
        compiler_params=_cparams(("parallel", "parallel", "arbitrary")),
        name="hyena_conv",
    )(z, x0, dc["f1c"], dc["g"], dc["gi"], dc["f1i"], spec_re, spec_im, skip)


def _hyena_ctx_kernel(p0_ref, p1_ref, p2_ref, w0_ref, w1_ref, w2_ref, b0_ref, b1_ref, b2_ref,
                      k_ref, inv_ref, skip_ref, fd_ref, fi_ref, o_ref, *, n):
    x0 = _conv3(p0_ref[0], w0_ref, b0_ref)
    z = _conv3(p1_ref[0], w1_ref, b1_ref) * _conv3(p2_ref[0], w2_ref, b2_ref)
    fd = fd_ref[...]
    N = 2 * n
    zf = jnp.dot(fd[:, :n], z, precision=HI, preferred_element_type=f32)
    kf = jnp.dot(fd, k_ref[...], precision=HI, preferred_element_type=f32) * inv_ref[...]
    yre = zf[:N] * kf[:N] - zf[N:] * kf[N:]
    yim = zf[:N] * kf[N:] + zf[N:] * kf[:N]
    y = jnp.dot(fi_ref[...], jnp.concatenate([yre, yim], axis=0), precision=HI, preferred_element_type=f32)
    o_ref[0] = (x0 * (y + z * skip_ref[...])).astype(o_ref.dtype)


def hyena_ctx(p, conv_w, conv_b, k, inv_norm, skip):
    B, n, _ = p.shape
    N = 2 * n
    nb = HY_W // LANES
    kk = np.arange(N)
    ph = 2 * np.pi * ((kk[:, None] * kk[None, :]) % N) / N
    fd = jnp.asarray(np.concatenate([np.cos(ph), -np.sin(ph)], axis=0).astype(np.float32))
    fi = jnp.asarray((np.concatenate([np.cos(ph[:n]), -np.sin(ph[:n])], axis=1) / N).astype(np.float32))
    conv_b = conv_b.reshape(1, 3 * HY_W)
    pspec = lambda g: pl.BlockSpec((1, n, LANES), lambda b, c: (b, 0, g * nb + c))
    wspec = lambda g: pl.BlockSpec((3, LANES), lambda b, c: (0, g * nb + c))
    bspec = lambda g: pl.BlockSpec((1, LANES), lambda b, c: (0, g * nb + c))
    vspec = pl.BlockSpec((1, LANES), lambda b, c: (0, c))
    return pl.pallas_call(
        functools.partial(_hyena_ctx_kernel, n=n),
        grid=(B, nb),
        in_specs=[pspec(0), pspec(1), pspec(2), wspec(0), wspec(1), wspec(2), bspec(0), bspec(1), bspec(2),
                  pl.BlockSpec((N, LANES), lambda b, c: (0, c)),
                  vspec, vspec,
                  pl.BlockSpec(fd.shape, lambda b, c: (0, 0)), pl.BlockSpec(fi.shape, lambda b, c: (0, 0))],
        out_specs=pl.BlockSpec((1, n, LANES), lambda b, c: (b, 0, c)),
        out_shape=jax.ShapeDtypeStruct((B, n, HY_W), bf16),
        compiler_params=_cparams(("parallel", "parallel")),
        name="hyena_ctx",
    )(p, p, p, conv_w, conv_w, conv_w, conv_b, conv_b, conv_b, k, inv_norm, skip, fd, fi)


def _merge_kernel(of_ref, ob_ref, gs_ref, nb_ref, hc_ref, g_ref, wa_ref, wb_ref, wc_ref, wo_ref, x_ref, m_ref, o_ref):
    d = D_MODEL
    tot = of_ref[...] + ob_ref[...]
    gs = gs_ref[...].astype(f32)
    ra = []
    for h in range(HG_HEADS):
        sl = slice(LANES * h, LANES * (h + 1))
        th = tot[:, sl]
        ms = jnp.mean(th * th, axis=-1, keepdims=True)
        ra.append(th * lax.rsqrt(ms + EPS) * gs[:, sl])
    ya = jnp.dot(jnp.concatenate(ra, axis=1).astype(bf16), wa_ref[...], preferred_element_type=f32)
    yb = jnp.dot(nb_ref[...], wb_ref[...], preferred_element_type=f32)
    yc = jnp.dot(hc_ref[...], wc_ref[...], preferred_element_type=f32)
    g = g_ref[...].astype(f32)
    mix = g[:, :d] * ya + g[:, d:2 * d] * yb + g[:, 2 * d:] * yc
    y = jnp.dot(mix.astype(bf16), wo_ref[...], preferred_element_type=f32)
    o_ref[...] = x_ref[...] + m_ref[0] * y


def merge(o_f, o_b, gs, nb, hc, gates, wa, wb, wc, wo, x2d, m, rows_per_group):
    R = x2d.shape[0]
    tm = 512
    tpg = rows_per_group // tm
    G = m.shape[0]
    row = lambda w: pl.BlockSpec((tm, w), lambda i: (i, 0))
    full = lambda a: pl.BlockSpec(a.shape, lambda i: (0, 0))
    return pl.pallas_call(
        _merge_kernel,
        grid=(R // tm,),
        in_specs=[row(HG_W), row(HG_W), row(HG_W), row(NA_W), row(HY_W), row(3 * D_MODEL),
                  full(wa), full(wb), full(wc), full(wo),
                  row(D_MODEL), pl.BlockSpec((1, 1, D_MODEL), lambda i: (i // tpg, 0, 0))],
        out_specs=row(D_MODEL),
        out_shape=jax.ShapeDtypeStruct((R, D_MODEL), f32),
        compiler_params=_cparams(("parallel",)),
        name="merge",
    )(o_f, o_b, gs, nb, hc, gates, wa, wb, wc, wo, x2d, m.reshape(G, 1, D_MODEL))


def _router_kernel(x_ref, g_ref, sh_ref, sc_ref, wrt_ref, wr_ref, h_ref, at_ref, am_ref):
    x = x_ref[...]
    ms = jnp.mean(x * x, axis=-1, keepdims=True)
    h = x * lax.rsqrt(ms + EPS) * g_ref[...] * (1.0 + sc_ref[0]) + sh_ref[0]
    h_ref[...] = h.astype(h_ref.dtype)
    lt = _nt(wrt_ref[...], h, precision=HI)
    et = jnp.exp(lt - jnp.max(lt, axis=0, keepdims=True))
    at_ref[0] = et / jnp.sum(et, axis=0, keepdims=True)
    lm = jnp.dot(h, wr_ref[...], precision=HI, preferred_element_type=f32)
    em = jnp.exp(lm - jnp.max(lm, axis=1, keepdims=True))
    am_ref[...] = em / jnp.sum(em, axis=1, keepdims=True)


def router(x2d, g, shift, scale, w_router, n_per_set):
    R = x2d.shape[0]
    tm = min(512, n_per_set)
    tps = n_per_set // tm
    S = R // n_per_set
    G = shift.shape[0]
    gmap = (lambda i: (i // tps, 0, 0)) if G > 1 else (lambda i: (0, 0, 0))
    wr = w_router.astype(f32)
    return pl.pallas_call(
        _router_kernel,
        grid=(R // tm,),
        in_specs=[pl.BlockSpec((tm, D_MODEL), lambda i: (i, 0)),
                  pl.BlockSpec((1, D_MODEL), lambda i: (0, 0)),
                  pl.BlockSpec((1, 1, D_MODEL), gmap),
                  pl.BlockSpec((1, 1, D_MODEL), gmap),
                  pl.BlockSpec((N_EXPERTS, D_MODEL), lambda i: (0, 0)),
                  pl.BlockSpec((D_MODEL, N_EXPERTS), lambda i: (0, 0))],
        out_specs=[pl.BlockSpec((tm, D_MODEL), lambda i: (i, 0)),
                   pl.BlockSpec((1, N_EXPERTS, tm), lambda i: (i // tps, 0, i % tps)),
                   pl.BlockSpec((tm, N_EXPERTS), lambda i: (i, 0))],
        out_shape=[jax.ShapeDtypeStruct((R, D_MODEL), bf16),
                   jax.ShapeDtypeStruct((S, N_EXPERTS, n_per_set), f32),
                   jax.ShapeDtypeStruct((R, N_EXPERTS), f32)],
        compiler_params=_cparams(("parallel",)),
        name="router",
    )(x2d, g.reshape(1, D_MODEL), shift.reshape(G, 1, D_MODEL), scale.reshape(G, 1, D_MODEL), wr.T, wr)


SEL_BLK = 256
SUB = LANES
SUBW = SUB + 8
UNSEL = -float(2 ** 30)


def _prefix_incl(mask_f, tri, T):
    outs = []
    off = jnp.zeros((mask_f.shape[0], 1), f32)
    for b in range(T // SEL_BLK):
        blk = mask_f[:, b * SEL_BLK:(b + 1) * SEL_BLK].astype(bf16)
        pre = jnp.dot(blk, tri, preferred_element_type=f32) + off
        outs.append(pre)
        off = pre[:, SEL_BLK - 1:SEL_BLK]
    return jnp.concatenate(outs, axis=1)


def _select_kernel(a_ref, tri_ref, cm_ref, posm_ref, cnt_ref, *, T, cap):
    aff = a_ref[0]
    bits = pltpu.bitcast(aff, i32)
    tri = tri_ref[...]

    def bit_step(i, thr):
        cand = thr | (1 << (30 - i))
        cnt = jnp.sum((bits >= cand).astype(f32), axis=1, keepdims=True)
        return jnp.where(cnt >= cap, cand, thr)

    thr = lax.fori_loop(0, 31, bit_step, jnp.zeros((N_EXPERTS, 1), i32))
    gt = bits > thr
    eq = bits == thr
    need = cap - jnp.sum(gt.astype(f32), axis=1, keepdims=True)
    eqf = eq.astype(f32)
    rank_eq = _prefix_incl(eqf, tri, T) - eqf
    sel = gt | (eq & (rank_eq < need))
    self_ = sel.astype(f32)
    pos = _prefix_incl(self_, tri, T) - self_
    posm_ref[0] = jnp.where(sel, pos, UNSEL)
    cnt_ref[0] = jnp.dot(self_.astype(bf16), cm_ref[...], preferred_element_type=f32).astype(i32)


def select_topk(aff, cap):
    S, E, T = aff.shape
    tri = jnp.asarray(np.triu(np.ones((SEL_BLK, SEL_BLK), np.float32)), bf16)
    cm = (jnp.arange(T)[:, None] < jnp.arange(LANES)[None, :] * SUB).astype(bf16)
    return pl.pallas_call(
        functools.partial(_select_kernel, T=T, cap=cap),
        grid=(S,),
        in_specs=[pl.BlockSpec((1, E, T), lambda s: (s, 0, 0)),
                  pl.BlockSpec((SEL_BLK, SEL_BLK), lambda s: (0, 0)),
                  pl.BlockSpec((T, LANES), lambda s: (0, 0))],
        out_specs=[pl.BlockSpec((1, E, T), lambda s: (s, 0, 0)),
                   pl.BlockSpec((1, E, LANES), lambda s: (s, 0, 0))],
        out_shape=[jax.ShapeDtypeStruct((S, E, T), f32), jax.ShapeDtypeStruct((S, E, LANES), i32)],
        compiler_params=_cparams(("parallel",)),
        name="select_topk",
    )(aff, tri, cm)


def _align8(v):
    return lax.shift_left(lax.shift_right_logical(v, 3), 3)


def _align16(v):
    return lax.shift_left(lax.shift_right_logical(v, 4), 4)


CMB_ROWS = SUB + 16


def _gather_kernel(cnt_ref, h_ref, pos_ref, o_ref, acc, *, TT, cap, tps, R, srows, EP):
    tl = pl.program_id(1)

    @pl.when(tl == 0)
    def _():
        acc[...] = jnp.zeros_like(acc)

    st = tl // tps
    nsub = TT // SUB
    rid = lax.broadcasted_iota(i32, (srows, SUB), 0).astype(f32)
    for ep in range(EP):
        e = pl.program_id(0) * EP + ep
        cbase = (st * N_EXPERTS + e) * LANES + (tl % tps) * nsub
        for s in range(nsub):
            off8 = _align8(cnt_ref[cbase + s])
            pos = pos_ref[0, ep, :, s * SUB:(s + 1) * SUB]
            onehot = jnp.where(pos == rid + off8.astype(f32), 1.0, 0.0).astype(bf16)
            rows = jnp.dot(onehot, h_ref[s * SUB:(s + 1) * SUB, :], preferred_element_type=f32)
            r0 = pl.multiple_of(st * cap + off8, 8)
            acc[ep, pl.ds(r0, srows), :] += rows

    @pl.when(tl == pl.num_programs(1) - 1)
    def _():
        o_ref[...] = acc[:, 0:R, :].astype(o_ref.dtype)


def _gather_call(cnt, h, posm, cap, TT, srows, EP):
    S, E, T = posm.shape
    tps = T // TT
    R = S * cap
    gs = pltpu.PrefetchScalarGridSpec(
        num_scalar_prefetch=1,
        grid=(E // EP, S * tps),
        in_specs=[pl.BlockSpec((TT, D_MODEL), lambda e, t, c: (t, 0)),
                  pl.BlockSpec((1, EP, 1, TT), lambda e, t, c: (t // tps, e, 0, t % tps))],
        out_specs=pl.BlockSpec((EP, R, D_MODEL), lambda e, t, c: (e, 0, 0)),
        scratch_shapes=[pltpu.VMEM((EP, R + srows, D_MODEL), f32)])
    return pl.pallas_call(
        functools.partial(_gather_kernel, TT=TT, cap=cap, tps=tps, R=R, srows=srows, EP=EP),
        grid_spec=gs,
        out_shape=jax.ShapeDtypeStruct((E, R, D_MODEL), bf16),
        compiler_params=_cparams(("parallel", "arbitrary")),
        name="moe_gather",
    )(cnt.reshape(-1), h, posm.reshape(S, E, 1, T))


FAST_SUB_MAX = 48
FAST_TILE_MAX = 112
FAST_ROWS = 64
CMB_TILE = 512


def _fits_fast(cnt, T):
    nsub = T // SUB
    per_sub = cnt[..., 1:nsub + 1] - cnt[..., :nsub]
    k = CMB_TILE // SUB
    per_tile = cnt[..., k:nsub + 1:k] - cnt[..., 0:nsub:k]
    return (jnp.max(per_sub) <= FAST_SUB_MAX) & (jnp.max(per_tile) <= FAST_TILE_MAX)


def gather_rows(cnt, h, posm, cap, TT, fast_ok=None):
    safe = lambda: _gather_call(cnt, h, posm, cap, TT, SUBW, 1)
    if fast_ok is None:
        return safe()
    return lax.cond(fast_ok, lambda: _gather_call(cnt, h, posm, cap, TT, FAST_ROWS, 2), safe)


EXPERT_TF = 256


def _ffn_kernel(*refs, n):
    xs = refs[:n]
    wg_ref, wu_ref, wd_ref = refs[n:n + 3]
    his = refs[n + 3:2 * n + 3]
    los = refs[2 * n + 3:3 * n + 3]
    accs = refs[3 * n + 3:]
    j = pl.program_id(1)
    wg = wg_ref[0, 0].astype(bf16)
    wu = wu_ref[0, 0].astype(bf16)
    wd = wd_ref[0, 0].astype(bf16)
    @pl.when(j == 0)
    def _():
        for acc in accs:
            acc[...] = jnp.zeros_like(acc)

    for x_ref, hi_ref, lo_ref, acc in zip(xs, his, los, accs):
        x = x_ref[0]
        a = jnp.dot(x, wg, preferred_element_type=f32)
        u = jnp.dot(x, wu, preferred_element_type=f32)
        acc[...] += jnp.dot((_silu(a) * u).astype(bf16), wd, preferred_element_type=f32)

        @pl.when(j == pl.num_programs(1) - 1)
        def _(acc=acc, hi_ref=hi_ref, lo_ref=lo_ref):
            hi, lo = _split(acc[...])
            hi_ref[0] = hi
            lo_ref[0] = lo


def expert_ffn(xgs, layer, w_gate, w_up, w_down):
    E = xgs[0].shape[0]
    nf = D_FF_EXPERT // EXPERT_TF
    n = len(xgs)
    rowspec = lambda a: pl.BlockSpec((1, a.shape[1], D_MODEL), lambda e, j: (e, 0, 0))
    res = pl.pallas_call(
        functools.partial(_ffn_kernel, n=n),
        grid=(E, nf),
        in_specs=[rowspec(a) for a in xgs] + [
            pl.BlockSpec((1, 1, D_MODEL, EXPERT_TF), lambda e, j: (layer, e, 0, j)),
            pl.BlockSpec((1, 1, D_MODEL, EXPERT_TF), lambda e, j: (layer, e, 0, j)),
            pl.BlockSpec((1, 1, EXPERT_TF, D_MODEL), lambda e, j: (layer, e, j, 0))],
        out_specs=[rowspec(a) for a in xgs] * 2,
        out_shape=[jax.ShapeDtypeStruct(a.shape, bf16) for a in xgs] * 2,
        scratch_shapes=[pltpu.VMEM(a.shape[1:], f32) for a in xgs],
        compiler_params=_cparams(("parallel", "arbitrary")),
        name="expert_ffn",
    )(*xgs, w_gate, w_up, w_down)
    return [(res[i], res[n + i]) for i in range(n)]


def _combine_kernel(cnt_ref, x_ref, pos_ref, am_ref, m_ref, *rest, TT, cap, R, W, ytot, crows, EP):
    y_refs, o_ref = rest[:2 * EP], rest[2 * EP]
    st = pl.program_id(0)
    tl = pl.program_id(1)
    eg = pl.program_id(2)

    @pl.when(eg == 0)
    def _():
        o_ref[...] = x_ref[...]

    nsub = TT // SUB
    lane = lax.broadcasted_iota(i32, (TT, N_EXPERTS), 1)
    m5 = m_ref[0]
    rid = lax.broadcasted_iota(i32, (crows, SUB), 0).astype(f32)
    am = am_ref[...]
    gcols, wss = [], []
    for ep in range(EP):
        e = eg * EP + ep
        cbase = (st * N_EXPERTS + e) * LANES + tl * nsub
        wss.append(jnp.minimum(e * R + st * cap + _align16(cnt_ref[cbase]), ytot - W))
        gcols.append(jnp.sum(jnp.where(lane == e, am, 0.0), axis=1, keepdims=True))
    for s in range(nsub):
        sl = slice(s * SUB, (s + 1) * SUB)
        tot = None
        for ep in range(EP):
            e = eg * EP + ep
            rowbase = e * R + st * cap
            off = _align16(cnt_ref[(st * N_EXPERTS + e) * LANES + tl * nsub + s])
            rel = pl.multiple_of(jnp.minimum(rowbase + off - wss[ep], W - crows), 16)
            first = (wss[ep] + rel - rowbase).astype(f32)
            pos = pos_ref[0, ep, :, sl]
            onehot = jnp.where(pos == rid + first, 1.0, 0.0).astype(bf16)
            ywin = jnp.concatenate([y_refs[2 * ep][pl.ds(rel, crows), :], y_refs[2 * ep + 1][pl.ds(rel, crows), :]],
                                   axis=0)
            picked = _tn(jnp.concatenate([onehot, onehot], axis=0), ywin)
            term = gcols[ep][sl] * picked
            tot = term if tot is None else tot + term
        o_ref[sl, :] += m5 * tot


def _combine_call(cnt, x2d, posm, aff_tm, mvec, y_hl, cap, TT, crows, W, EP):
    S, E, T = posm.shape
    tps = T // TT
    R = S * cap
    ytot = E * R
    nsub = TT // SUB
    G = mvec.shape[0]

    def yspec(ep):
        def ymap(st, tl, eg, c):
            e = eg * EP + ep
            off = _align16(c[(st * E + e) * LANES + tl * nsub])
            return (pl.multiple_of(jnp.minimum(e * R + st * cap + off, ytot - W), 16), 0)
        return pl.BlockSpec((pl.Element(W), pl.Element(D_MODEL)), ymap)

    tok = lambda w: pl.BlockSpec((TT, w), lambda st, tl, e, c: (st * tps + tl, 0))
    mmap = (lambda st, tl, e, c: (st, 0, 0)) if G > 1 else (lambda st, tl, e, c: (0, 0, 0))
    gs = pltpu.PrefetchScalarGridSpec(
        num_scalar_prefetch=1,
        grid=(S, tps, E // EP),
        in_specs=[tok(D_MODEL),
                  pl.BlockSpec((1, EP, 1, TT), lambda st, tl, e, c: (st, e, 0, tl)),
                  tok(N_EXPERTS),
                  pl.BlockSpec((1, 1, D_MODEL), mmap)] + [yspec(ep) for ep in range(EP) for _ in range(2)],
        out_specs=tok(D_MODEL))
    yh, yl = y_hl[0].reshape(ytot, D_MODEL), y_hl[1].reshape(ytot, D_MODEL)
    return pl.pallas_call(
        functools.partial(_combine_kernel, TT=TT, cap=cap, R=R, W=W, ytot=ytot, crows=crows, EP=EP),
        grid_spec=gs,
        out_shape=jax.ShapeDtypeStruct(x2d.shape, f32),
        compiler_params=_cparams(("parallel", "parallel", "arbitrary")),
        name="moe_combine",
    )(cnt.reshape(-1), x2d, posm.reshape(S, E, 1, T), aff_tm, mvec.reshape(G, 1, D_MODEL), *([yh, yl] * EP))


def combine(cnt, x2d, posm, aff_tm, mvec, y_hl, cap, TT, fast_ok=None):
    safe = lambda: _combine_call(cnt, x2d, posm, aff_tm, mvec, y_hl, cap, TT, CMB_ROWS, TT + 32, 1)
    if fast_ok is None:
        return safe()
    wfast = FAST_TILE_MAX + 16 + FAST_ROWS
    return lax.cond(fast_ok, lambda: _combine_call(cnt, x2d, posm, aff_tm, mvec, y_hl, cap, TT, FAST_ROWS, wfast, 4),
                    safe)


def _rope_tables(n):
    half = NA_HD // 2
    q = half // 2
    inv = ROPE_THETA ** (-jnp.arange(q, dtype=f32) / q)
    pos = jnp.arange(n)
    ang_r = (pos // GRID_W).astype(f32)[:, None] * inv
    ang_c = (pos % GRID_W).astype(f32)[:, None] * inv
    zero = jnp.zeros_like(ang_r)
    c = jnp.concatenate([jnp.cos(ang_r)] * 2 + [jnp.cos(ang_c)] * 2, axis=1)
    s1 = jnp.concatenate([-jnp.sin(ang_r), zero, -jnp.sin(ang_c), zero], axis=1)
    s2 = jnp.concatenate([zero, jnp.sin(ang_r), zero, jnp.sin(ang_c)], axis=1)
    two = lambda a: jnp.concatenate([a, a], axis=1)
    return two(c), two(s1), two(s2)


def _mixing(hx, hc, need_ctx, B, N, NC, la, lc, w_in, q_gain, k_gain, table, rope, bd, conv_w, conv_b,
            spec, skip, filt_c, wa, wb, wc, wo, x2d, c2d, mx2, mc2):
    tile8 = lambda v: jnp.tile(v.reshape(1, NA_HD), (1, NA_HEADS))
    qg, kg = tile8(q_gain), tile8(k_gain)
    norm_aux = [("col", kg), ("const", bd)]
    rope_aux = [("row", rope[0]), ("row", rope[1]), ("row", rope[2])]
    lf_aux = lambda d: [("col", la[d:d + 1]), ("col", lc[d:d + 1])]
    tc = hc.shape[0]

    lff_c = project(hc, w_in, OFF_FF, 512, _epi_logforget, lf_aux(0), (f32,), tm=tc).reshape(B, NC, 512)
    lfb_c = project(hc, w_in, OFF_FB, 512, _epi_logforget, lf_aux(1), (f32,), tm=tc).reshape(B, NC, 512)
    i_c = project(hc, w_in, OFF_I, 512, _epi_raw, tm=tc).reshape(B, NC, 512)
    k_c = project(hc, w_in, OFF_NK, 512, _epi_norm, norm_aux, tm=tc).reshape(B, NC, 512)
    v_c = project(hc, w_in, OFF_NV, 512, _epi_raw, tm=tc).reshape(B, NC, 512)
    if need_ctx:
        q_c = project(hc, w_in, OFF_HQ, 512, _epi_silu, tm=tc).reshape(B, NC, 512)
    else:
        q_c = jnp.zeros((B, NC, 512), bf16)
    s0 = jnp.zeros((B, HG_HEADS, LANES, LANES), f32)
    oc_f, oc_b, s_f, s_b = hgrn_bidir(lff_c, lfb_c, i_c, q_c, s0, s0)

    lff_x = project(hx, w_in, OFF_FF, 512, _epi_logforget, lf_aux(0), (f32,)).reshape(B, N, 512)
    lfb_x = project(hx, w_in, OFF_FB, 512, _epi_logforget, lf_aux(1), (f32,)).reshape(B, N, 512)
    i_x = project(hx, w_in, OFF_I, 512, _epi_raw).reshape(B, N, 512)
    q_x = project(hx, w_in, OFF_HQ, 512, _epi_silu).reshape(B, N, 512)
    g_x = project(hx, w_in, OFF_HG, 512, _epi_silu).reshape(B, N, 512)
    k_x = project(hx, w_in, OFF_NK, 512, _epi_norm_rope, norm_aux + rope_aux, rows_per_seq=N).reshape(B, N, 512)
    v_x = project(hx, w_in, OFF_NV, 512, _epi_raw).reshape(B, N, 512)
    qn_x, qr_x = project(hx, w_in, OFF_NQ, 512, _epi_norm_both, [("col", qg), ("const", bd)] + rope_aux,
                         (bf16, bf16), rows_per_seq=N)
    p_x = project(hx, w_in, OFF_HY, 3 * HY_W, _epi_raw, out_dtypes=(f32,)).reshape(B, N, 3 * HY_W)
    gates_x = project(hx, w_in, OFF_GATE, 3 * D_MODEL, _epi_sigmoid)

    ox_f, ox_b, _, _ = hgrn_bidir(lff_x, lfb_x, i_x, q_x, s_f, s_b)

    nb_x = neighbourhood_attention(qr_x.reshape(B, N, 512), qn_x.reshape(B, N, 512), k_x, v_x, k_c, v_c, table)

    z_x, x0_x = hyena_pre(p_x, conv_w, conv_b)
    hy_x = hyena_conv(z_x, x0_x, spec[0], spec[1], skip, spec[2])

    flat = lambda a: a.reshape(-1, a.shape[-1])
    x_new = merge(flat(ox_f), flat(ox_b), flat(g_x), flat(nb_x), flat(hy_x), gates_x, wa, wb, wc, wo, x2d, mx2, N)
    if not need_ctx:
        return x_new, None

    qn_c = project(hc, w_in, OFF_NQ, 512, _epi_norm, [("col", qg), ("const", bd)], tm=tc).reshape(B, NC, 512)
    nb_c = context_attention(qn_c, k_c, v_c)
    p_c = project(hc, w_in, OFF_HY, 3 * HY_W, _epi_raw, out_dtypes=(f32,), tm=tc).reshape(B, NC, 3 * HY_W)
    hy_c = hyena_ctx(p_c, conv_w, conv_b, filt_c[0], filt_c[1], skip)
    gates_c = project(hc, w_in, OFF_GATE, 3 * D_MODEL, _epi_sigmoid, tm=tc)
    g_c = project(hc, w_in, OFF_HG, 512, _epi_silu, tm=tc)
    c_new = merge(flat(oc_f), flat(oc_b), g_c, flat(nb_c), flat(hy_c), gates_c, wa, wb, wc, wo, c2d, mc2, B * NC)
    return x_new, c_new


def kernel(x, c, ctx, c_ctx, w_mod, b_mod, norm_mix, norm_ffn, w_in, hg_lb, na_q_gain, na_k_gain, na_rpb,
           hy_conv_w, hy_conv_b, hy_pe_w1, hy_pe_b1, hy_pe_freq1, hy_pe_w2, hy_pe_b2, hy_pe_freq2, hy_pe_w3,
           hy_skip, w_branch_a, w_branch_b, w_branch_c, w_out, w_router, w_e_gate, w_e_up, w_e_down):
    B, N, D = x.shape
    NC = ctx.shape[1]
    E = N_EXPERTS
    cap_x = EC_CAP_FACTOR * N // E
    cap_c = EC_CAP_FACTOR * NC // E

    lb = jnp.cumsum(jax.nn.softmax(hg_lb.astype(f32), axis=0), axis=0)
    lb = lb - lb[:1]
    la_all, lc_all = jnp.log(lb), jnp.log1p(-lb)

    s8 = jnp.zeros((8, D), f32).at[:B].set(c).at[B].set(c_ctx)
    rope = _rope_tables(N)
    bd = jnp.asarray(np.kron(np.eye(NA_HEADS), np.full((NA_HD, NA_HD), 1.0 / NA_HD)).astype(np.float32), bf16)
    dcx = _dft_consts(N)

    x2d = x.reshape(B * N, D)
    c2d = ctx.reshape(B * NC, D)
    for l in range(DEPTH):
        need_ctx = l < DEPTH - 1
        mv = modvec(s8, w_mod[l], b_mod[l])
        mx = [mv[:B, k * D:(k + 1) * D] for k in range(6)]
        mc = [mv[B:B + 1, k * D:(k + 1) * D] for k in range(6)]
        w_in_l = w_in[l].astype(bf16)
        hx = modulate(x2d, norm_mix[l], mx[0], mx[1], N, bf16)
        hc = modulate(c2d, norm_mix[l], mc[0], mc[1], B * NC, bf16)

        filt = (hy_pe_w1[l], hy_pe_b1[l], hy_pe_freq1[l], hy_pe_w2[l], hy_pe_b2[l], hy_pe_freq2[l], hy_pe_w3[l])
        k_x, nrm_x = hyena_filter(N, *filt)
        sre, sim = hyena_spectrum(k_x, 1.0 / nrm_x, dcx)
        skip = hy_skip[l].reshape(1, HY_W)
        filt_c = None
        if need_ctx:
            h_c, nrm_c = hyena_filter(NC, *filt)
            filt_c = (h_c, 1.0 / nrm_c)

        x2d, c_new = _mixing(
            hx, hc, need_ctx, B, N, NC, la_all[l], lc_all[l], w_in_l, na_q_gain[l], na_k_gain[l],
            _na_bias_table(na_rpb[l]), rope, bd, hy_conv_w[l], hy_conv_b[l], (sre, sim, dcx), skip,
            filt_c, w_branch_a[l].astype(bf16), w_branch_b[l].astype(bf16), w_branch_c[l].astype(bf16),
            w_out[l].astype(bf16), x2d, c2d, mx[2], mc[2])

        h2, aff_t, aff_m = router(x2d, norm_ffn[l], mx[3], mx[4], w_router[l], N)
        posm, cnt = select_topk(aff_t, cap_x)
        fast_ok = _fits_fast(cnt, N)
        xgs = [gather_rows(cnt, h2, posm, cap_x, 1024, fast_ok)]
        if need_ctx:
            c2d = c_new
            hc2, aff_tc, aff_mc = router(c2d, norm_ffn[l], mc[3], mc[4], w_router[l], NC)
            posm_c, cnt_c = select_topk(aff_tc, cap_c)
            xgs.append(gather_rows(cnt_c, hc2, posm_c, cap_c, NC))
        ys = expert_ffn(xgs, l, w_e_gate, w_e_up, w_e_down)
        x2d = combine(cnt, x2d, posm, aff_m, mx[5], ys[0], cap_x, CMB_TILE, fast_ok)
        if need_ctx:
            c2d = combine(cnt_c, c2d, posm_c, aff_mc, mc[5], ys[1], cap_c, NC)
    return x2d.reshape(B, N, D)
```

```python
import functools
import math

import numpy as np
import jax
import jax.numpy as jnp
from jax import lax
from jax.experimental import pallas as pl
from jax.experimental.pallas import tpu as pltpu

f32 = jnp.float32
bf16 = jnp.bfloat16
i32 = jnp.int32
HI = lax.Precision.HIGHEST

D_MODEL = 1024
DEPTH = 2
GRID_W = 64
EPS = 1e-6
HG_HEADS = 4
HG_W = 512
HG_CHUNK = 64
NA_HEADS = 8
NA_HD = 64
NA_W = 512
NA_WIN_R = 8
NA_WIN_C = 16
ROPE_THETA = 10000.0
HY_W = 512
HY_BANDS = 16
HY_PE_DIM = 1 + 2 * HY_BANDS
HY_FILT_HID = 64
HY_FAST_DECAY = 0.3
HY_SLOW_DECAY = 1.5
HY_TARGET = 1e-2
OFF_FF = 0
OFF_FB = 512
OFF_I = 1024
OFF_NK = 1536
OFF_NV = 2048
OFF_HQ = 2560
OFF_NQ = 3072
OFF_HG = 3584
OFF_HY = 4096
OFF_GATE = 5632
IN_COLS = 8704
N_EXPERTS = 16
EC_CAP_FACTOR = 2
D_FF_EXPERT = 2816

LANES = 128
NEG_BIG = -1e30
VMEM_LIMIT = 56 * 1024 * 1024


def _cparams(sem, vmem=VMEM_LIMIT):
    return pltpu.CompilerParams(dimension_semantics=sem, vmem_limit_bytes=vmem)


def _nt(a, b, precision=None):
    return lax.dot_general(a, b, (((1,), (1,)), ((), ())), precision=precision, preferred_element_type=f32)


def _tn(a, b, precision=None):
    return lax.dot_general(a, b, (((0,), (0,)), ((), ())), precision=precision, preferred_element_type=f32)


def _silu(x):
    return x * jax.nn.sigmoid(x)


def _split(x):
    hi = x.astype(bf16)
    return hi, (x - hi.astype(f32)).astype(bf16)


def _modvec_kernel(s_ref, w_ref, b_ref, o_ref):
    s = _silu(s_ref[...])
    o_ref[...] = jnp.dot(s, w_ref[...], precision=HI, preferred_element_type=f32) + b_ref[...]


def modvec(s8, w, b):
    n = w.shape[1]
    tn = 1024
    return pl.pallas_call(
        _modvec_kernel,
        grid=(n // tn,),
        in_specs=[pl.BlockSpec((8, D_MODEL), lambda j: (0, 0)),
                  pl.BlockSpec((D_MODEL, tn), lambda j: (0, j)),
                  pl.BlockSpec((1, tn), lambda j: (0, j))],
        out_specs=pl.BlockSpec((8, tn), lambda j: (0, j)),
        out_shape=jax.ShapeDtypeStruct((8, n), f32),
        compiler_params=_cparams(("parallel",)),
        name="modvec",
    )(s8, w, b.reshape(1, n))


def _modulate_kernel(x_ref, g_ref, sh_ref, sc_ref, o_ref):
    x = x_ref[...]
    ms = jnp.mean(x * x, axis=-1, keepdims=True)
    y = x * lax.rsqrt(ms + EPS)
    o_ref[...] = (y * g_ref[...] * (1.0 + sc_ref[0]) + sh_ref[0]).astype(o_ref.dtype)


def modulate(x2d, g, shift, scale, rows_per_group, out_dtype):
    R = x2d.shape[0]
    tm = 512
    tpg = rows_per_group // tm
    G = shift.shape[0]
    return pl.pallas_call(
        _modulate_kernel,
        grid=(R // tm,),
        in_specs=[pl.BlockSpec((tm, D_MODEL), lambda i: (i, 0)),
                  pl.BlockSpec((1, D_MODEL), lambda i: (0, 0)),
                  pl.BlockSpec((1, 1, D_MODEL), lambda i: (i // tpg, 0, 0)),
                  pl.BlockSpec((1, 1, D_MODEL), lambda i: (i // tpg, 0, 0))],
        out_specs=pl.BlockSpec((tm, D_MODEL), lambda i: (i, 0)),
        out_shape=jax.ShapeDtypeStruct((R, D_MODEL), out_dtype),
        compiler_params=_cparams(("parallel",)),
        name="modulate",
    )(x2d, g.reshape(1, D_MODEL), shift.reshape(G, 1, D_MODEL), scale.reshape(G, 1, D_MODEL))


def _log_sigmoid(z):
    return jnp.minimum(z, 0.0) - jnp.log1p(jnp.exp(-jnp.abs(z)))


def _epi_raw(acc, o_ref):
    o_ref[...] = acc.astype(o_ref.dtype)


def _epi_silu(acc, o_ref):
    o_ref[...] = _silu(acc).astype(o_ref.dtype)


def _epi_sigmoid(acc, o_ref):
    o_ref[...] = jax.nn.sigmoid(acc).astype(o_ref.dtype)


def _epi_logforget(acc, la_ref, lc_ref, o_ref):
    la = la_ref[...]
    c = lc_ref[...] + _log_sigmoid(acc)
    o_ref[...] = jnp.maximum(la, c) + jnp.log1p(jnp.exp(-jnp.abs(la - c)))


def _head_rms(acc, gain_ref, bd_ref):
    hi, lo = _split(acc * acc)
    ms = jnp.dot(hi, bd_ref[...], preferred_element_type=f32) + jnp.dot(lo, bd_ref[...], preferred_element_type=f32)
    return acc * lax.rsqrt(ms + EPS) * gain_ref[...]


def _rope(y, c_ref, s1_ref, s2_ref):
    reps = y.shape[1] // LANES
    c = jnp.concatenate([c_ref[...]] * reps, axis=1)
    s1 = jnp.concatenate([s1_ref[...]] * reps, axis=1)
    s2 = jnp.concatenate([s2_ref[...]] * reps, axis=1)
    w = y.shape[1]
    return y * c + pltpu.roll(y, w - 16, axis=1) * s1 + pltpu.roll(y, 16, axis=1) * s2


def _epi_norm(acc, gain_ref, bd_ref, o_ref):
    o_ref[...] = _head_rms(acc, gain_ref, bd_ref).astype(o_ref.dtype)


def _epi_norm_rope(acc, gain_ref, bd_ref, c_ref, s1_ref, s2_ref, o_ref):
    y = _head_rms(acc, gain_ref, bd_ref)
    o_ref[...] = _rope(y, c_ref, s1_ref, s2_ref).astype(o_ref.dtype)


def _epi_norm_both(acc, gain_ref, bd_ref, c_ref, s1_ref, s2_ref, on_ref, or_ref):
    y = _head_rms(acc, gain_ref, bd_ref)
    on_ref[...] = y.astype(on_ref.dtype)
    or_ref[...] = _rope(y, c_ref, s1_ref, s2_ref).astype(or_ref.dtype)


def _proj_kernel(h_ref, w_ref, *rest, epi):
    acc = jnp.dot(h_ref[...], w_ref[...], preferred_element_type=f32)
    epi(acc, *rest)


def project(h, w, c0, width, epi, aux=(), out_dtypes=(bf16,), tm=1024, rows_per_seq=None):
    R = h.shape[0]
    tn = 512
    nj = width // tn
    cb = c0 // tn
    in_specs = [pl.BlockSpec((tm, D_MODEL), lambda i, j: (i, 0)),
                pl.BlockSpec((D_MODEL, tn), lambda i, j: (0, cb + j))]
    args = [h, w]
    for kind, arr in aux:
        if kind == "col":
            in_specs.append(pl.BlockSpec((1, tn), lambda i, j: (0, j)))
        elif kind == "const":
            in_specs.append(pl.BlockSpec(arr.shape, lambda i, j: (0, 0)))
        else:
            tps = rows_per_seq // tm
            in_specs.append(pl.BlockSpec((tm, LANES), lambda i, j: (i % tps, 0)))
        args.append(arr)
    out_specs = [pl.BlockSpec((tm, tn), lambda i, j: (i, j)) for _ in out_dtypes]
    out_shape = [jax.ShapeDtypeStruct((R, width), dt) for dt in out_dtypes]
    res = pl.pallas_call(
        functools.partial(_proj_kernel, epi=epi),
        grid=(R // tm, nj),
        in_specs=in_specs,
        out_specs=out_specs,
        out_shape=out_shape,
        compiler_params=_cparams(("parallel", "parallel")),
        name="proj_" + epi.__name__[5:],
    )(*args)
    return res[0] if len(res) == 1 else res


def _hgrn_tmatrix(C, reverse):
    L = int(round(math.log2(C)))
    t = np.arange(C)
    tau = (C - 1 - t) if reverse else t
    tt, uu = tau[:, None], tau[None, :]
    T = np.zeros((2 + L, C, C), np.float32)
    T[0] = uu <= tt
    T[1] = uu > tt
    for l in range(L):
        same = (tt >> (l + 1)) == (uu >> (l + 1))
        tr = ((tt >> l) & 1) == 1
        ur = ((uu >> l) & 1) == 1
        T[2 + l] = same & ((tr & ur & (uu <= tt)) | (~tr & ~ur & (uu > tt)))
    return T.reshape((2 + L) * C, C)


def _hgrn_step(chains, C):
    L = int(round(math.log2(C)))
    row = lax.broadcasted_iota(i32, (C, LANES), 0)
    ti0 = lax.broadcasted_iota(i32, (C, C), 0)
    si0 = lax.broadcasted_iota(i32, (C, C), 1)

    es, kcs = [], []
    for lf, _, _, tm, _, _ in chains:
        W = lf.shape[1]
        hi, lo = _split(lf)
        r = jnp.dot(tm, jnp.concatenate([hi, lo], axis=1), preferred_element_type=f32)
        es.append(r[:, :W] + r[:, W:])
        kcs.append(1.0 - jnp.exp(lf))

    items = []
    for (lf, v_all, q_all, _, s_view, reverse), E, kc_all in zip(chains, es, kcs):
        tau = (C - 1 - row) if reverse else row
        ti, si = (C - 1 - ti0, C - 1 - si0) if reverse else (ti0, si0)
        end_row = 0 if reverse else C - 1
        for h in range(HG_HEADS):
            sl = slice(LANES * h, LANES * (h + 1))
            qh, kch, vh = q_all[:, sl], kc_all[:, sl], v_all[:, sl]
            b = E[0:C, sl]
            st = s_view[h]
            pairs = [(qh.astype(bf16), kch.astype(bf16), ti == si)]
            for l in range(L):
                x = jnp.exp(E[(2 + l) * C:(3 + l) * C, sl])
                later = ((tau >> l) & 1) == 1
                pairs.append((jnp.where(later, qh * x, 0.0).astype(bf16), jnp.where(later, 0.0, kch * x).astype(bf16),
                              (ti >> (l + 1)) == (si >> (l + 1))))
            items.append(dict(qb=(qh * jnp.exp(b)).astype(bf16), st=st, pairs=pairs, vh=vh,
                              kd=(kch * jnp.exp(E[C:2 * C, sl])).astype(bf16),
                              decay=jnp.exp(b[end_row:end_row + 1, :]), view=s_view, h=h))

    for it in items:
        it["o"] = _nt(it["qb"], it["st"].astype(bf16))
        it["att"] = [(_nt(ql, kl), m) for ql, kl, m in it["pairs"]]

    outs = []
    for it in items:
        att = None
        for a, m in it["att"]:
            t = jnp.where(m, a, 0.0)
            att = t if att is None else att + t
        it["o"] = it["o"] + jnp.dot(att.astype(bf16), it["vh"], preferred_element_type=f32)
        it["view"][it["h"]] = it["st"] * it["decay"] + _tn(it["vh"], it["kd"])
    nh = HG_HEADS
    for c in range(len(chains)):
        outs.append(jnp.concatenate([items[c * nh + h]["o"] for h in range(nh)], axis=1))
    return outs


def _hgrn_kernel(lff_ref, lfb_ref, vf_ref, vb_ref, qf_ref, qb_ref, s0f_ref, s0b_ref, tf_ref, tb_ref,
                 of_ref, ob_ref, sff_ref, sfb_ref, s_scr, *, C, B):
    c = pl.program_id(0)

    @pl.when(c == 0)
    def _():
        s_scr[0] = s0f_ref[...]
        s_scr[1] = s0b_ref[...]

    chains = []
    for b in range(B):
        chains.append((lff_ref[b], vf_ref[b], qf_ref[b].astype(f32), tf_ref[...], s_scr.at[0, b], False))
        chains.append((lfb_ref[b], vb_ref[b], qb_ref[b].astype(f32), tb_ref[...], s_scr.at[1, b], True))
    outs = _hgrn_step(chains, C)
    for b in range(B):
        of_ref[b] = outs[2 * b]
        ob_ref[b] = outs[2 * b + 1]

    @pl.when(c == pl.num_programs(0) - 1)
    def _():
        sff_ref[...] = s_scr[0]
        sfb_ref[...] = s_scr[1]


def hgrn_bidir(lf_f, lf_b, v, q, s0_f, s0_b):
    B, N, W = lf_f.shape
    C = HG_CHUNK
    nch = N // C
    tf = jnp.asarray(_hgrn_tmatrix(C, False), bf16)
    tb = jnp.asarray(_hgrn_tmatrix(C, True), bf16)
    fw = pl.BlockSpec((B, C, W), lambda c: (0, c, 0))
    bw = pl.BlockSpec((B, C, W), lambda c: (0, nch - 1 - c, 0))
    st = pl.BlockSpec((B, HG_HEADS, LANES, LANES), lambda c: (0, 0, 0, 0))
    tsp = pl.BlockSpec(tf.shape, lambda c: (0, 0))
    seq = jax.ShapeDtypeStruct((B, N, W), f32)
    sts = jax.ShapeDtypeStruct((B, HG_HEADS, LANES, LANES), f32)
    return pl.pallas_call(
        functools.partial(_hgrn_kernel, C=C, B=B),
        grid=(nch,),
        in_specs=[fw, bw, fw, bw, fw, bw, st, st, tsp, tsp],
        out_specs=[fw, bw, st, st],
        out_shape=[seq, seq, sts, sts],
        scratch_shapes=[pltpu.VMEM((2, B, HG_HEADS, LANES, LANES), f32)],
        compiler_params=_cparams(("arbitrary",)),
        name="hgrn",
    )(lf_f, lf_b, v, v, q, q, s0_f, s0_b, tf, tb)


def _na_kernel(qr_ref, qn_ref, k_ref, v_ref, kc_ref, vc_ref, tab_ref, o_ref, s_scr, p_scr, *, rows_per_step, n_rows):
    g = pl.program_id(2)
    scale = NA_HD ** -0.5
    lane = lax.broadcasted_iota(i32, (GRID_W, LANES), 1)
    kcx = kc_ref[0]
    vcx = vc_ref[0]
    win = NA_WIN_R * GRID_W
    ctx_len = kcx.shape[0]

    starts = []
    for i in range(rows_per_step):
        r = g * rows_per_step + i
        rs = jnp.clip(r - NA_WIN_R // 2, 0, n_rows - NA_WIN_R)
        off = rs - r + (NA_WIN_R - 1)
        start = pl.multiple_of(rs * GRID_W, GRID_W)
        starts.append(start)
        kw = k_ref[0, pl.ds(start, win), :]
        qr = qr_ref[0, i * GRID_W:(i + 1) * GRID_W, :]
        qn = qn_ref[0, i * GRID_W:(i + 1) * GRID_W, :]
        zq = jnp.zeros_like(qr)
        qrs = jnp.concatenate([jnp.where(lane < NA_HD, qr, zq), jnp.where(lane < NA_HD, zq, qr)], axis=0)
        qns = jnp.concatenate([jnp.where(lane < NA_HD, qn, zq), jnp.where(lane < NA_HD, zq, qn)], axis=0)
        bias = jnp.concatenate([tab_ref[0, off], tab_ref[1, off]], axis=0)
        row0 = 2 * i * GRID_W
        s_scr[row0:row0 + 2 * GRID_W, 0:win] = _nt(qrs, kw) * scale + bias
        s_scr[row0:row0 + 2 * GRID_W, win:win + ctx_len] = _nt(qns, kcx) * scale

    def softmax_rows(c, carry):
        r0 = pl.multiple_of(c * LANES, LANES)
        s = s_scr[pl.ds(r0, LANES), :]
        p = jnp.exp(s - jnp.max(s, axis=-1, keepdims=True))
        inv = 1.0 / jnp.sum(p, axis=-1, keepdims=True)
        p_scr[pl.ds(r0, LANES), :] = (p * inv).astype(bf16)
        return carry

    lax.fori_loop(0, 2 * rows_per_step * GRID_W // LANES, softmax_rows, 0, unroll=2)

    for i in range(rows_per_step):
        vw = v_ref[0, pl.ds(starts[i], win), :]
        row0 = 2 * i * GRID_W
        p = p_scr[row0:row0 + 2 * GRID_W, :]
        res = (jnp.dot(p[:, :win], vw, preferred_element_type=f32)
               + jnp.dot(p[:, win:], vcx, preferred_element_type=f32))
        o_ref[0, i * GRID_W:(i + 1) * GRID_W, :] = jnp.where(lane < NA_HD, res[:GRID_W], res[GRID_W:]).astype(o_ref.dtype)


def _na_bias_table(rpb):
    col = jnp.arange(GRID_W)
    cs = jnp.clip(col - NA_WIN_C // 2, 0, GRID_W - NA_WIN_C)
    kc = jnp.arange(GRID_W)
    valid = (kc[None, :] >= cs[:, None]) & (kc[None, :] < cs[:, None] + NA_WIN_C)
    dc = jnp.clip(kc[None, :] - col[:, None] + (NA_WIN_C - 1), 0, 2 * NA_WIN_C - 2)
    bc = jnp.where(valid[None, None], rpb[:, :, dc], NEG_BIG)
    t2 = jnp.stack([bc[:, o:o + NA_WIN_R] for o in range(NA_WIN_R)], axis=1)
    t2 = t2.transpose(0, 1, 3, 2, 4)
    return t2.reshape(NA_HEADS, NA_WIN_R, GRID_W, NA_WIN_R * GRID_W).astype(f32)


def neighbourhood_attention(q_rot, qn, k_rot, v, kc, vc, table):
    B, N, W = q_rot.shape
    n_rows = N // GRID_W
    rps = 8
    ctx_len = kc.shape[1]
    pairs = W // LANES
    keys = NA_WIN_R * GRID_W + ctx_len
    return pl.pallas_call(
        functools.partial(_na_kernel, rows_per_step=rps, n_rows=n_rows),
        grid=(B, pairs, n_rows // rps),
        in_specs=[pl.BlockSpec((1, rps * GRID_W, LANES), lambda b, p, g: (b, g, p)),
                  pl.BlockSpec((1, rps * GRID_W, LANES), lambda b, p, g: (b, g, p)),
                  pl.BlockSpec((1, N, LANES), lambda b, p, g: (b, 0, p)),
                  pl.BlockSpec((1, N, LANES), lambda b, p, g: (b, 0, p)),
                  pl.BlockSpec((1, ctx_len, LANES), lambda b, p, g: (b, 0, p)),
                  pl.BlockSpec((1, ctx_len, LANES), lambda b, p, g: (b, 0, p)),
                  pl.BlockSpec((2, NA_WIN_R, GRID_W, NA_WIN_R * GRID_W), lambda b, p, g: (p, 0, 0, 0))],
        out_specs=pl.BlockSpec((1, rps * GRID_W, LANES), lambda b, p, g: (b, g, p)),
        out_shape=jax.ShapeDtypeStruct((B, N, W), bf16),
        scratch_shapes=[pltpu.VMEM((2 * rps * GRID_W, keys), f32), pltpu.VMEM((2 * rps * GRID_W, keys), bf16)],
        compiler_params=_cparams(("parallel", "parallel", "arbitrary")),
        name="natten",
    )(q_rot, qn, k_rot, v, kc, vc, table)


def _ctx_attn_kernel(q_ref, k_ref, v_ref, o_ref):
    scale = NA_HD ** -0.5
    q = q_ref[0]
    k = k_ref[0]
    v = v_ref[0]
    lane = lax.broadcasted_iota(i32, q.shape, 1)
    res = []
    for hh in range(2):
        m = (lane >= NA_HD * hh) & (lane < NA_HD * (hh + 1))
        s = _nt(jnp.where(m, q, jnp.zeros_like(q)), k) * scale
        p = jnp.exp(s - jnp.max(s, axis=-1, keepdims=True))
        p = p / jnp.sum(p, axis=-1, keepdims=True)
        res.append(jnp.dot(p.astype(bf16), v, preferred_element_type=f32))
    o_ref[0] = jnp.where(lane < NA_HD, res[0], res[1]).astype(o_ref.dtype)


def context_attention(q, k, v):
    B, N, W = q.shape
    spec = pl.BlockSpec((1, N, LANES), lambda b, p: (b, 0, p))
    return pl.pallas_call(
        _ctx_attn_kernel,
        grid=(B, W // LANES),
        in_specs=[spec, spec, spec],
        out_specs=spec,
        out_shape=jax.ShapeDtypeStruct((B, N, W), bf16),
        compiler_params=_cparams(("parallel", "parallel")),
        name="ctx_attn",
    )(q, k, v)


def _filter_kernel(z_ref, w1_ref, b1_ref, f1_ref, w2_ref, b2_ref, f2_ref, w3_ref, dl_ref, k_ref, nrm_ref, *, tm, n):
    i = pl.program_id(0)
    z = z_ref[...]
    a = jnp.sin(f1_ref[...] * (jnp.dot(z, w1_ref[...], precision=HI, preferred_element_type=f32) + b1_ref[...]))
    a = jnp.sin(f2_ref[...] * (jnp.dot(a, w2_ref[...], precision=HI, preferred_element_type=f32) + b2_ref[...]))
    h = jnp.dot(a, w3_ref[...], precision=HI, preferred_element_type=f32)
    h = h * jnp.exp(-z[:, 0:1] * dl_ref[...])
    row = lax.broadcasted_iota(i32, (tm, HY_W), 0) + i * tm
    k = jnp.where(row == n, 0.0, h)
    k_ref[...] = k
    part = jnp.sum(jnp.abs(k), axis=0, keepdims=True)

    @pl.when(i == 0)
    def _():
        nrm_ref[...] = part

    @pl.when(i > 0)
    def _():
        nrm_ref[...] = nrm_ref[...] + part


def hyena_filter(n, w1, b1, fr1, w2, b2, fr2, w3):
    t = np.linspace(0.0, 1.0, n)[:, None]
    w = 2 * math.pi * np.arange(n)[:, None] / n
    fb = np.linspace(1e-4, HY_BANDS - 1, HY_BANDS)[None]
    z = np.concatenate([t, np.cos(fb * w), -np.sin(fb * w)], axis=-1)
    z = np.concatenate([z, np.zeros((1, HY_PE_DIM)), z[:0:-1]], axis=0)
    z = jnp.asarray(np.pad(z, ((0, 0), (0, LANES - HY_PE_DIM))).astype(np.float32))
    w1p = jnp.pad(w1.astype(f32), ((0, LANES - HY_PE_DIM), (0, 0)))
    deltas = jnp.asarray(np.abs(np.linspace(math.log(HY_TARGET) / HY_SLOW_DECAY, math.log(HY_TARGET) / HY_FAST_DECAY,
                                            2 * HY_W))[None].astype(np.float32))
    tm = min(n, 512)
    hid = HY_FILT_HID
    full = lambda shape: pl.BlockSpec(shape, lambda i: (0, 0))
    tph = n // tm
    return pl.pallas_call(
        functools.partial(_filter_kernel, tm=tm, n=n),
        grid=(2 * n // tm,),
        in_specs=[pl.BlockSpec((tm, LANES), lambda i: (i, 0)),
                  full((LANES, hid)), full((1, hid)), full((1, hid)),
                  full((hid, hid)), full((1, hid)), full((1, hid)),
                  pl.BlockSpec((hid, HY_W), lambda i: (0, i // tph)),
                  pl.BlockSpec((1, HY_W), lambda i: (0, i // tph))],
        out_specs=[pl.BlockSpec((tm, HY_W), lambda i: (i, 0)), pl.BlockSpec((1, HY_W), lambda i: (0, 0))],
        out_shape=[jax.ShapeDtypeStruct((2 * n, HY_W), f32), jax.ShapeDtypeStruct((1, HY_W), f32)],
        compiler_params=_cparams(("arbitrary",)),
        name="hyena_filter",
    )(z, w1p, b1.reshape(1, hid), fr1.reshape(1, hid), w2.astype(f32), b2.reshape(1, hid), fr2.reshape(1, hid),
      w3.astype(f32), deltas)


def _conv3(u, w_ref, b_ref):
    n = u.shape[0]
    row = lax.broadcasted_iota(i32, u.shape, 0)
    prev = jnp.where(row == 0, 0.0, pltpu.roll(u, 1, axis=0))
    nxt = jnp.where(row == n - 1, 0.0, pltpu.roll(u, n - 1, axis=0))
    return prev * w_ref[0:1, :] + u * w_ref[1:2, :] + nxt * w_ref[2:3, :] + b_ref[...]


def _hyena_pre_kernel(p0_ref, p1_ref, p2_ref, w0_ref, w1_ref, w2_ref, b0_ref, b1_ref, b2_ref, z_ref, x0_ref):
    x0_ref[0] = _conv3(p0_ref[0], w0_ref, b0_ref).astype(x0_ref.dtype)
    z_ref[0] = _conv3(p1_ref[0], w1_ref, b1_ref) * _conv3(p2_ref[0], w2_ref, b2_ref)


def hyena_pre(p, conv_w, conv_b):
    B, N, _ = p.shape
    nb = HY_W // LANES
    conv_b = conv_b.reshape(1, 3 * HY_W)
    pspec = lambda g: pl.BlockSpec((1, N, LANES), lambda b, c: (b, 0, g * nb + c))
    wspec = lambda g: pl.BlockSpec((3, LANES), lambda b, c: (0, g * nb + c))
    bspec = lambda g: pl.BlockSpec((1, LANES), lambda b, c: (0, g * nb + c))
    ospec = pl.BlockSpec((1, N, LANES), lambda b, c: (b, 0, c))
    return pl.pallas_call(
        _hyena_pre_kernel,
        grid=(B, nb),
        in_specs=[pspec(0), pspec(1), pspec(2), wspec(0), wspec(1), wspec(2), bspec(0), bspec(1), bspec(2)],
        out_specs=[ospec, ospec],
        out_shape=[jax.ShapeDtypeStruct((B, N, HY_W), f32), jax.ShapeDtypeStruct((B, N, HY_W), bf16)],
        compiler_params=_cparams(("parallel", "parallel")),
        name="hyena_pre",
    )(p, p, p, conv_w, conv_w, conv_w, conv_b, conv_b, conv_b)


def _hl(a):
    a32 = jnp.asarray(a.astype(np.float32))
    hi = a32.astype(bf16)
    lo = (a32 - hi.astype(f32)).astype(bf16)
    return jnp.concatenate([hi, lo], axis=-2)


def _dot3(a_hl, m, x):
    xh, xl = _split(x)
    r = jnp.dot(a_hl, xh, preferred_element_type=f32)
    return r[:m] + r[m:] + jnp.dot(a_hl[:m], xl, preferred_element_type=f32)


def _dft_consts(n):
    N = 2 * n
    na = N // LANES
    t1n = na // 2
    k1n = na // 2 + 1
    k1p = -(-k1n // 8) * 8
    k1 = np.arange(k1n)
    t1 = np.arange(t1n)
    th = 2 * np.pi * ((t1[None, :] * k1[:, None]) % na) / na
    f1c = np.zeros((2 * k1p, t1n))
    f1c[:k1n] = np.cos(th)
    f1c[k1p:k1p + k1n] = -np.sin(th)
    thf = 2 * np.pi * ((np.arange(na)[None, :] * k1[:, None]) % na) / na
    f1f = np.zeros((2 * k1p, na))
    f1f[:k1n] = np.cos(thf)
    f1f[k1p:k1p + k1n] = -np.sin(thf)
    k2 = np.arange(LANES)
    t2 = np.arange(LANES)
    m = (t2[None, None, :] * (k1[:, None, None] + na * k2[None, :, None])) % N
    ph = 2 * np.pi * m / N
    g = np.concatenate([np.cos(ph), -np.sin(ph)], axis=1)
    pht = ph.transpose(0, 2, 1)
    gi = np.concatenate([np.cos(pht), np.sin(pht)], axis=1)
    wk = np.where((k1 == 0) | (k1 == na // 2), 1.0, 2.0) / N
    f1i = np.zeros((t1n, 2 * k1p))
    f1i[:, :k1n] = np.cos(th.T) * wk[None, :]
    f1i[:, k1p:k1p + k1n] = -np.sin(th.T) * wk[None, :]
    k1e = k1n + k1n % 2
    g = np.concatenate([g, np.zeros((k1e - k1n,) + g.shape[1:])], axis=0)
    gi = np.concatenate([gi, np.zeros((k1e - k1n,) + gi.shape[1:])], axis=0)
    return dict(na=na, t1n=t1n, k1n=k1n, k1e=k1e, k1p=k1p, f1c=_hl(f1c), f1f=_hl(f1f), g=_hl(g), gi=_hl(gi),
                f1i=_hl(f1i))


def _dft_stage1(src_ref, f1c_ref, are_ref, aim_ref, t1n, k1p):
    f1c = f1c_ref[...]

    def body(t2, carry):
        zs = src_ref[pl.ds(t2, t1n, stride=LANES), :]
        r = _dot3(f1c, 2 * k1p, zs)
        are_ref[pl.ds(t2, k1p, stride=LANES), :] = r[:k1p]
        aim_ref[pl.ds(t2, k1p, stride=LANES), :] = r[k1p:]
        return carry

    lax.fori_loop(0, LANES, body, 0, unroll=4)


def _cplx_left(gc_hl, xre, xim):
    cw = xre.shape[1]
    r = _dot3(gc_hl, 2 * LANES, jnp.concatenate([xre, xim], axis=1))
    p, q = r[:, :cw], r[:, cw:]
    return p[:LANES] - q[LANES:], p[LANES:] + q[:LANES]


def _spectrum_kernel(k_ref, f1f_ref, g_ref, inv_ref, xre_ref, xim_ref, are, aim, *, na, k1p):
    j = pl.program_id(1)

    @pl.when(j == 0)
    def _():
        _dft_stage1(k_ref, f1f_ref, are, aim, na, k1p)

    for half in range(2):
        r0 = pl.multiple_of((2 * j + half) * LANES, LANES)
        xre, xim = _cplx_left(g_ref[half], are[pl.ds(r0, LANES), :], aim[pl.ds(r0, LANES), :])
        rows = slice(half * LANES, (half + 1) * LANES)
        xre_ref[rows, :] = xre * inv_ref[...]
        xim_ref[rows, :] = xim * inv_ref[...]


def hyena_spectrum(k, inv_norm, dc):
    n2, C = k.shape
    k1e, k1p, na = dc["k1e"], dc["k1p"], dc["na"]
    cw = LANES
    out = jax.ShapeDtypeStruct((k1e * LANES, C), f32)
    ospec = pl.BlockSpec((2 * LANES, cw), lambda c, k: (k, c))
    return pl.pallas_call(
        functools.partial(_spectrum_kernel, na=na, k1p=k1p),
        grid=(C // cw, k1e // 2),
        in_specs=[pl.BlockSpec((n2, cw), lambda c, k: (0, c)),
                  pl.BlockSpec(dc["f1f"].shape, lambda c, k: (0, 0)),
                  pl.BlockSpec((2, 4 * LANES, LANES), lambda c, k: (k, 0, 0)),
                  pl.BlockSpec((1, cw), lambda c, k: (0, c))],
        out_specs=[ospec, ospec],
        out_shape=[out, out],
        scratch_shapes=[pltpu.VMEM((k1p * LANES, cw), f32), pltpu.VMEM((k1p * LANES, cw), f32)],
        compiler_params=_cparams(("parallel", "arbitrary")),
        name="hyena_spectrum",
    )(k, dc["f1f"], dc["g"], inv_norm)


def _hyena_conv_kernel(z_ref, x0_ref, f1c_ref, g_ref, gi_ref, f1i_ref, kre_ref, kim_ref,
                       skip_ref, o_ref, are, aim, y_scr, *, t1n, k1p):
    j = pl.program_id(2)

    @pl.when(j == 0)
    def _():
        _dft_stage1(z_ref.at[0], f1c_ref, are, aim, t1n, k1p)

    for half in range(2):
        r0 = pl.multiple_of((2 * j + half) * LANES, LANES)
        xre, xim = _cplx_left(g_ref[half], are[pl.ds(r0, LANES), :], aim[pl.ds(r0, LANES), :])
        rows = slice(half * LANES, (half + 1) * LANES)
        kre = kre_ref[rows, :]
        kim = kim_ref[rows, :]
        yre = xre * kre - xim * kim
        yim = xre * kim + xim * kre
        bre, bim = _cplx_left(gi_ref[half], yre, yim)
        are[pl.ds(r0, LANES), :] = bre
        aim[pl.ds(r0, LANES), :] = bim

    @pl.when(j == pl.num_programs(2) - 1)
    def _():
        f1i = f1i_ref[...]

        def body(t2, carry):
            bb = jnp.concatenate([are[pl.ds(t2, k1p, stride=LANES), :], aim[pl.ds(t2, k1p, stride=LANES), :]], axis=0)
            y_scr[pl.ds(t2, t1n, stride=LANES), :] = _dot3(f1i, t1n, bb)
            return carry

        lax.fori_loop(0, LANES, body, 0, unroll=4)
        z = z_ref[0]
        o_ref[0] = (x0_ref[0].astype(f32) * (y_scr[...] + z * skip_ref[...])).astype(o_ref.dtype)


def hyena_conv(z, x0, spec_re, spec_im, skip, dc):
    B, n, W = z.shape
    nb = W // LANES
    k1e, k1p, t1n = dc["k1e"], dc["k1p"], dc["t1n"]
    seq = pl.BlockSpec((1, n, LANES), lambda b, c, k: (b, 0, c))
    fspec = pl.BlockSpec((2 * LANES, LANES), lambda b, c, k: (k, c))
    cspec = pl.BlockSpec((2, 4 * LANES, LANES), lambda b, c, k: (k, 0, 0))
    vspec = pl.BlockSpec((1, LANES), lambda b, c, k: (0, c))
    return pl.pallas_call(
        functools.partial(_hyena_conv_kernel, t1n=t1n, k1p=k1p),
        grid=(B, nb, k1e // 2),
        in_specs=[seq, seq,
                  pl.BlockSpec(dc["f1c"].shape, lambda b, c, k: (0, 0)), cspec, cspec,
                  pl.BlockSpec(dc["f1i"].shape, lambda b, c, k: (0, 0)),
                  fspec, fspec, vspec],
        out_specs=seq,
        out_shape=jax.ShapeDtypeStruct((B, n, W), bf16),
        scratch_shapes=[pltpu.VMEM((k1p * LANES, LANES), f32), pltpu.VMEM((k1p * LANES, LANES), f32),
                        pltpu.VMEM((n, LANES), f32)],
        compiler_params=_cparams(("parallel", "parallel", "arbitrary")),
        name="hyena_conv",
    )(z, x0, dc["f1c"], dc["g"], dc["gi"], dc["f1i"], spec_re, spec_im, skip)


def _hyena_ctx_kernel(p0_ref, p1_ref, p2_ref, w0_ref, w1_ref, w2_ref, b0_ref, b1_ref, b2_ref,
                      k_ref, inv_ref, skip_ref, fd_ref, fi_ref, o_ref, *, n):
    x0 = _conv3(p0_ref[0], w0_ref, b0_ref)
    z = _conv3(p1_ref[0], w1_ref, b1_ref) * _conv3(p2_ref[0], w2_ref, b2_ref)
    fd = fd_ref[...]
    N = 2 * n
    zf = jnp.dot(fd[:, :n], z, precision=HI, preferred_element_type=f32)
    kf = jnp.dot(fd, k_ref[...], precision=HI, preferred_element_type=f32) * inv_ref[...]
    yre = zf[:N] * kf[:N] - zf[N:] * kf[N:]
    yim = zf[:N] * kf[N:] + zf[N:] * kf[:N]
    y = jnp.dot(fi_ref[...], jnp.concatenate([yre, yim], axis=0), precision=HI, preferred_element_type=f32)
    o_ref[0] = (x0 * (y + z * skip_ref[...])).astype(o_ref.dtype)


def hyena_ctx(p, conv_w, conv_b, k, inv_norm, skip):
    B, n, _ = p.shape
    N = 2 * n
    nb = HY_W // LANES
    kk = np.arange(N)
    ph = 2 * np.pi * ((kk[:, None] * kk[None, :]) % N) / N
    fd = jnp.asarray(np.concatenate([np.cos(ph), -np.sin(ph)], axis=0).astype(np.float32))
    fi = jnp.asarray((np.concatenate([np.cos(ph[:n]), -np.sin(ph[:n])], axis=1) / N).astype(np.float32))
    conv_b = conv_b.reshape(1, 3 * HY_W)
    pspec = lambda g: pl.BlockSpec((1, n, LANES), lambda b, c: (b, 0, g * nb + c))
    wspec = lambda g: pl.BlockSpec((3, LANES), lambda b, c: (0, g * nb + c))
    bspec = lambda g: pl.BlockSpec((1, LANES), lambda b, c: (0, g * nb + c))
    vspec = pl.BlockSpec((1, LANES), lambda b, c: (0, c))
    return pl.pallas_call(
        functools.partial(_hyena_ctx_kernel, n=n),
        grid=(B, nb),
        in_specs=[pspec(0), pspec(1), pspec(2), wspec(0), wspec(1), wspec(2), bspec(0), bspec(1), bspec(2),
                  pl.BlockSpec((N, LANES), lambda b, c: (0, c)),
                  vspec, vspec,
                  pl.BlockSpec(fd.shape, lambda b, c: (0, 0)), pl.BlockSpec(fi.shape, lambda b, c: (0, 0))],
        out_specs=pl.BlockSpec((1, n, LANES), lambda b, c: (b, 0, c)),
        out_shape=jax.ShapeDtypeStruct((B, n, HY_W), bf16),
        compiler_params=_cparams(("parallel", "parallel")),
        name="hyena_ctx",
    )(p, p, p, conv_w, conv_w, conv_w, conv_b, conv_b, conv_b, k, inv_norm, skip, fd, fi)


def _merge_kernel(of_ref, ob_ref, gs_ref, nb_ref, hc_ref, g_ref, wa_ref, wb_ref, wc_ref, wo_ref, x_ref, m_ref, o_ref):
    d = D_MODEL
    tot = of_ref[...] + ob_ref[...]
    gs = gs_ref[...].astype(f32)
    ra = []
    for h in range(HG_HEADS):
        sl = slice(LANES * h, LANES * (h + 1))
        th = tot[:, sl]
        ms = jnp.mean(th * th, axis=-1, keepdims=True)
        ra.append(th * lax.rsqrt(ms + EPS) * gs[:, sl])
    ya = jnp.dot(jnp.concatenate(ra, axis=1).astype(bf16), wa_ref[...], preferred_element_type=f32)
    yb = jnp.dot(nb_ref[...], wb_ref[...], preferred_element_type=f32)
    yc = jnp.dot(hc_ref[...], wc_ref[...], preferred_element_type=f32)
    g = g_ref[...].astype(f32)
    mix = g[:, :d] * ya + g[:, d:2 * d] * yb + g[:, 2 * d:] * yc
    y = jnp.dot(mix.astype(bf16), wo_ref[...], preferred_element_type=f32)
    o_ref[...] = x_ref[...] + m_ref[0] * y


def merge(o_f, o_b, gs, nb, hc, gates, wa, wb, wc, wo, x2d, m, rows_per_group):
    R = x2d.shape[0]
    tm = 512
    tpg = rows_per_group // tm
    G = m.shape[0]
    row = lambda w: pl.BlockSpec((tm, w), lambda i: (i, 0))
    full = lambda a: pl.BlockSpec(a.shape, lambda i: (0, 0))
    return pl.pallas_call(
        _merge_kernel,
        grid=(R // tm,),
        in_specs=[row(HG_W), row(HG_W), row(HG_W), row(NA_W), row(HY_W), row(3 * D_MODEL),
                  full(wa), full(wb), full(wc), full(wo),
                  row(D_MODEL), pl.BlockSpec((1, 1, D_MODEL), lambda i: (i // tpg, 0, 0))],
        out_specs=row(D_MODEL),
        out_shape=jax.ShapeDtypeStruct((R, D_MODEL), f32),
        compiler_params=_cparams(("parallel",)),
        name="merge",
    )(o_f, o_b, gs, nb, hc, gates, wa, wb, wc, wo, x2d, m.reshape(G, 1, D_MODEL))


def _router_kernel(x_ref, g_ref, sh_ref, sc_ref, wrt_ref, wr_ref, h_ref, at_ref, am_ref):
    x = x_ref[...]
    ms = jnp.mean(x * x, axis=-1, keepdims=True)
    h = x * lax.rsqrt(ms + EPS) * g_ref[...] * (1.0 + sc_ref[0]) + sh_ref[0]
    h_ref[...] = h.astype(h_ref.dtype)
    hh, hl = _split(h)
    ne = N_EXPERTS
    wt = wrt_ref[...]
    rt = _nt(wt, hh)
    lt = rt[:ne] + rt[ne:] + _nt(wt[:ne], hl)
    et = jnp.exp(lt - jnp.max(lt, axis=0, keepdims=True))
    at_ref[0] = et / jnp.sum(et, axis=0, keepdims=True)
    wm = wr_ref[...]
    rm = jnp.dot(hh, wm, preferred_element_type=f32)
    lm = rm[:, :ne] + rm[:, ne:] + jnp.dot(hl, wm[:, :ne], preferred_element_type=f32)
    em = jnp.exp(lm - jnp.max(lm, axis=1, keepdims=True))
    am_ref[...] = em / jnp.sum(em, axis=1, keepdims=True)


def router(x2d, g, shift, scale, w_router, n_per_set):
    R = x2d.shape[0]
    tm = min(512, n_per_set)
    tps = n_per_set // tm
    S = R // n_per_set
    G = shift.shape[0]
    gmap = (lambda i: (i // tps, 0, 0)) if G > 1 else (lambda i: (0, 0, 0))
    whi, wlo = _split(w_router.astype(f32))
    wr = jnp.concatenate([whi, wlo], axis=1)
    return pl.pallas_call(
        _router_kernel,
        grid=(R // tm,),
        in_specs=[pl.BlockSpec((tm, D_MODEL), lambda i: (i, 0)),
                  pl.BlockSpec((1, D_MODEL), lambda i: (0, 0)),
                  pl.BlockSpec((1, 1, D_MODEL), gmap),
                  pl.BlockSpec((1, 1, D_MODEL), gmap),
                  pl.BlockSpec((2 * N_EXPERTS, D_MODEL), lambda i: (0, 0)),
                  pl.BlockSpec((D_MODEL, 2 * N_EXPERTS), lambda i: (0, 0))],
        out_specs=[pl.BlockSpec((tm, D_MODEL), lambda i: (i, 0)),
                   pl.BlockSpec((1, N_EXPERTS, tm), lambda i: (i // tps, 0, i % tps)),
                   pl.BlockSpec((tm, N_EXPERTS), lambda i: (i, 0))],
        out_shape=[jax.ShapeDtypeStruct((R, D_MODEL), bf16),
                   jax.ShapeDtypeStruct((S, N_EXPERTS, n_per_set), f32),
                   jax.ShapeDtypeStruct((R, N_EXPERTS), f32)],
        compiler_params=_cparams(("parallel",)),
        name="router",
    )(x2d, g.reshape(1, D_MODEL), shift.reshape(G, 1, D_MODEL), scale.reshape(G, 1, D_MODEL), wr.T, wr)


SEL_BLK = 256
SUB = LANES
SUBW = SUB + 8
UNSEL = -float(2 ** 30)


def _prefix_incl(mask_f, tri, T):
    outs = []
    off = jnp.zeros((mask_f.shape[0], 1), f32)
    for b in range(T // SEL_BLK):
        blk = mask_f[:, b * SEL_BLK:(b + 1) * SEL_BLK].astype(bf16)
        pre = jnp.dot(blk, tri, preferred_element_type=f32) + off
        outs.append(pre)
        off = pre[:, SEL_BLK - 1:SEL_BLK]
    return jnp.concatenate(outs, axis=1)


def _select_kernel(a_ref, tri_ref, cm_ref, posm_ref, cnt_ref, *, T, cap):
    aff = a_ref[0]
    bits = pltpu.bitcast(aff, i32)
    tri = tri_ref[...]

    def bit_step(i, thr):
        cand = thr | (1 << (30 - i))
        cnt = jnp.sum((bits >= cand).astype(f32), axis=1, keepdims=True)
        return jnp.where(cnt >= cap, cand, thr)

    thr = lax.fori_loop(0, 31, bit_step, jnp.zeros((N_EXPERTS, 1), i32))
    gt = bits > thr
    eq = bits == thr
    need = cap - jnp.sum(gt.astype(f32), axis=1, keepdims=True)
    eqf = eq.astype(f32)
    rank_eq = _prefix_incl(eqf, tri, T) - eqf
    sel = gt | (eq & (rank_eq < need))
    self_ = sel.astype(f32)
    pos = _prefix_incl(self_, tri, T) - self_
    posm_ref[0] = jnp.where(sel, pos, UNSEL)
    cnt_ref[0] = jnp.dot(self_.astype(bf16), cm_ref[...], preferred_element_type=f32).astype(i32)


def select_topk(aff, cap):
    S, E, T = aff.shape
    tri = jnp.asarray(np.triu(np.ones((SEL_BLK, SEL_BLK), np.float32)), bf16)
    cm = jnp.asarray((np.arange(T)[:, None] < np.arange(LANES)[None, :] * SUB).astype(np.float32), bf16)
    return pl.pallas_call(
        functools.partial(_select_kernel, T=T, cap=cap),
        grid=(S,),
        in_specs=[pl.BlockSpec((1, E, T), lambda s: (s, 0, 0)),
                  pl.BlockSpec((SEL_BLK, SEL_BLK), lambda s: (0, 0)),
                  pl.BlockSpec((T, LANES), lambda s: (0, 0))],
        out_specs=[pl.BlockSpec((1, E, T), lambda s: (s, 0, 0)),
                   pl.BlockSpec((1, E, LANES), lambda s: (s, 0, 0))],
        out_shape=[jax.ShapeDtypeStruct((S, E, T), f32), jax.ShapeDtypeStruct((S, E, LANES), i32)],
        compiler_params=_cparams(("parallel",)),
        name="select_topk",
    )(aff, tri, cm)


def _align8(v):
    return lax.shift_left(lax.shift_right_logical(v, 3), 3)


def _align16(v):
    return lax.shift_left(lax.shift_right_logical(v, 4), 4)


CMB_ROWS = SUB + 16


def _gather_kernel(cnt_ref, h_ref, pos_ref, o_ref, acc, *, TT, cap, tps, R, srows, EP):
    tl = pl.program_id(1)

    @pl.when(tl == 0)
    def _():
        acc[...] = jnp.zeros_like(acc)

    st = tl // tps
    nsub = TT // SUB
    rid = lax.broadcasted_iota(i32, (srows, SUB), 0).astype(f32)
    for ep in range(EP):
        e = pl.program_id(0) * EP + ep
        cbase = (st * N_EXPERTS + e) * LANES + (tl % tps) * nsub
        for s in range(nsub):
            off8 = _align8(cnt_ref[cbase + s])
            pos = pos_ref[0, ep, :, s * SUB:(s + 1) * SUB]
            onehot = jnp.where(pos == rid + off8.astype(f32), 1.0, 0.0).astype(bf16)
            rows = jnp.dot(onehot, h_ref[s * SUB:(s + 1) * SUB, :], preferred_element_type=f32)
            r0 = pl.multiple_of(st * cap + off8, 8)
            acc[ep, pl.ds(r0, srows), :] += rows

    @pl.when(tl == pl.num_programs(1) - 1)
    def _():
        o_ref[...] = acc[:, 0:R, :].astype(o_ref.dtype)


def _gather_call(cnt, h, posm, cap, TT, srows, EP):
    S, E, T = posm.shape
    tps = T // TT
    R = S * cap
    gs = pltpu.PrefetchScalarGridSpec(
        num_scalar_prefetch=1,
        grid=(E // EP, S * tps),
        in_specs=[pl.BlockSpec((TT, D_MODEL), lambda e, t, c: (t, 0)),
                  pl.BlockSpec((1, EP, 1, TT), lambda e, t, c: (t // tps, e, 0, t % tps))],
        out_specs=pl.BlockSpec((EP, R, D_MODEL), lambda e, t, c: (e, 0, 0)),
        scratch_shapes=[pltpu.VMEM((EP, R + srows, D_MODEL), f32)])
    return pl.pallas_call(
        functools.partial(_gather_kernel, TT=TT, cap=cap, tps=tps, R=R, srows=srows, EP=EP),
        grid_spec=gs,
        out_shape=jax.ShapeDtypeStruct((E, R, D_MODEL), bf16),
        compiler_params=_cparams(("parallel", "arbitrary")),
        name="moe_gather",
    )(cnt.reshape(-1), h, posm.reshape(S, E, 1, T))


FAST_SUB_MAX = 48
FAST_TILE_MAX = 112
FAST_ROWS = 64
CMB_TILE = 512


def _fits_fast(cnt, T):
    nsub = T // SUB
    per_sub = cnt[..., 1:nsub + 1] - cnt[..., :nsub]
    k = CMB_TILE // SUB
    per_tile = cnt[..., k:nsub + 1:k] - cnt[..., 0:nsub:k]
    return (jnp.max(per_sub) <= FAST_SUB_MAX) & (jnp.max(per_tile) <= FAST_TILE_MAX)


def gather_rows(cnt, h, posm, cap, TT, fast_ok=None):
    safe = lambda: _gather_call(cnt, h, posm, cap, TT, SUBW, 1)
    if fast_ok is None:
        return safe()
    return lax.cond(fast_ok, lambda: _gather_call(cnt, h, posm, cap, TT, FAST_ROWS, 2), safe)


EXPERT_TF = 256


def _ffn_kernel(*refs, n):
    xs = refs[:n]
    wg_ref, wu_ref, wd_ref = refs[n:n + 3]
    his = refs[n + 3:2 * n + 3]
    los = refs[2 * n + 3:3 * n + 3]
    accs = refs[3 * n + 3:]
    j = pl.program_id(1)
    wg = wg_ref[0, 0].astype(bf16)
    wu = wu_ref[0, 0].astype(bf16)
    wd = wd_ref[0, 0].astype(bf16)
    @pl.when(j == 0)
    def _():
        for acc in accs:
            acc[...] = jnp.zeros_like(acc)

    for x_ref, hi_ref, lo_ref, acc in zip(xs, his, los, accs):
        x = x_ref[0]
        a = jnp.dot(x, wg, preferred_element_type=f32)
        u = jnp.dot(x, wu, preferred_element_type=f32)
        acc[...] += jnp.dot((_silu(a) * u).astype(bf16), wd, preferred_element_type=f32)

        @pl.when(j == pl.num_programs(1) - 1)
        def _(acc=acc, hi_ref=hi_ref, lo_ref=lo_ref):
            hi, lo = _split(acc[...])
            hi_ref[0] = hi
            lo_ref[0] = lo


def expert_ffn(xgs, layer, w_gate, w_up, w_down):
    E = xgs[0].shape[0]
    nf = D_FF_EXPERT // EXPERT_TF
    n = len(xgs)
    rowspec = lambda a: pl.BlockSpec((1, a.shape[1], D_MODEL), lambda e, j: (e, 0, 0))
    res = pl.pallas_call(
        functools.partial(_ffn_kernel, n=n),
        grid=(E, nf),
        in_specs=[rowspec(a) for a in xgs] + [
            pl.BlockSpec((1, 1, D_MODEL, EXPERT_TF), lambda e, j: (layer, e, 0, j)),
            pl.BlockSpec((1, 1, D_MODEL, EXPERT_TF), lambda e, j: (layer, e, 0, j)),
            pl.BlockSpec((1, 1, EXPERT_TF, D_MODEL), lambda e, j: (layer, e, j, 0))],
        out_specs=[rowspec(a) for a in xgs] * 2,
        out_shape=[jax.ShapeDtypeStruct(a.shape, bf16) for a in xgs] * 2,
        scratch_shapes=[pltpu.VMEM(a.shape[1:], f32) for a in xgs],
        compiler_params=_cparams(("parallel", "arbitrary")),
        name="expert_ffn",
    )(*xgs, w_gate, w_up, w_down)
    return [(res[i], res[n + i]) for i in range(n)]


def _combine_kernel(cnt_ref, x_ref, pos_ref, am_ref, m_ref, *rest, TT, cap, R, W, ytot, crows, EP):
    y_refs, o_ref = rest[:2 * EP], rest[2 * EP]
    st = pl.program_id(0)
    tl = pl.program_id(1)
    eg = pl.program_id(2)

    @pl.when(eg == 0)
    def _():
        o_ref[...] = x_ref[...]

    nsub = TT // SUB
    lane = lax.broadcasted_iota(i32, (TT, N_EXPERTS), 1)
    m5 = m_ref[0]
    rid = lax.broadcasted_iota(i32, (crows, SUB), 0).astype(f32)
    am = am_ref[...]
    gcols, wss = [], []
    for ep in range(EP):
        e = eg * EP + ep
        cbase = (st * N_EXPERTS + e) * LANES + tl * nsub
        wss.append(jnp.minimum(e * R + st * cap + _align16(cnt_ref[cbase]), ytot - W))
        gcols.append(jnp.sum(jnp.where(lane == e, am, 0.0), axis=1, keepdims=True))
    for s in range(nsub):
        sl = slice(s * SUB, (s + 1) * SUB)
        tot = None
        for ep in range(EP):
            e = eg * EP + ep
            rowbase = e * R + st * cap
            off = _align16(cnt_ref[(st * N_EXPERTS + e) * LANES + tl * nsub + s])
            rel = pl.multiple_of(jnp.minimum(rowbase + off - wss[ep], W - crows), 16)
            first = (wss[ep] + rel - rowbase).astype(f32)
            pos = pos_ref[0, ep, :, sl]
            onehot = jnp.where(pos == rid + first, 1.0, 0.0).astype(bf16)
            ywin = jnp.concatenate([y_refs[2 * ep][pl.ds(rel, crows), :], y_refs[2 * ep + 1][pl.ds(rel, crows), :]],
                                   axis=0)
            picked = _tn(jnp.concatenate([onehot, onehot], axis=0), ywin)
            term = gcols[ep][sl] * picked
            tot = term if tot is None else tot + term
        o_ref[sl, :] += m5 * tot


def _combine_call(cnt, x2d, posm, aff_tm, mvec, y_hl, cap, TT, crows, W, EP):
    S, E, T = posm.shape
    tps = T // TT
    R = S * cap
    ytot = E * R
    nsub = TT // SUB
    G = mvec.shape[0]

    def yspec(ep):
        def ymap(st, tl, eg, c):
            e = eg * EP + ep
            off = _align16(c[(st * E + e) * LANES + tl * nsub])
            return (pl.multiple_of(jnp.minimum(e * R + st * cap + off, ytot - W), 16), 0)
        return pl.BlockSpec((pl.Element(W), pl.Element(D_MODEL)), ymap)

    tok = lambda w: pl.BlockSpec((TT, w), lambda st, tl, e, c: (st * tps + tl, 0))
    mmap = (lambda st, tl, e, c: (st, 0, 0)) if G > 1 else (lambda st, tl, e, c: (0, 0, 0))
    gs = pltpu.PrefetchScalarGridSpec(
        num_scalar_prefetch=1,
        grid=(S, tps, E // EP),
        in_specs=[tok(D_MODEL),
                  pl.BlockSpec((1, EP, 1, TT), lambda st, tl, e, c: (st, e, 0, tl)),
                  tok(N_EXPERTS),
                  pl.BlockSpec((1, 1, D_MODEL), mmap)] + [yspec(ep) for ep in range(EP) for _ in range(2)],
        out_specs=tok(D_MODEL))
    yh, yl = y_hl[0].reshape(ytot, D_MODEL), y_hl[1].reshape(ytot, D_MODEL)
    return pl.pallas_call(
        functools.partial(_combine_kernel, TT=TT, cap=cap, R=R, W=W, ytot=ytot, crows=crows, EP=EP),
        grid_spec=gs,
        out_shape=jax.ShapeDtypeStruct(x2d.shape, f32),
        compiler_params=_cparams(("parallel", "parallel", "arbitrary")),
        name="moe_combine",
    )(cnt.reshape(-1), x2d, posm.reshape(S, E, 1, T), aff_tm, mvec.reshape(G, 1, D_MODEL), *([yh, yl] * EP))


def combine(cnt, x2d, posm, aff_tm, mvec, y_hl, cap, TT, fast_ok=None):
    safe = lambda: _combine_call(cnt, x2d, posm, aff_tm, mvec, y_hl, cap, TT, CMB_ROWS, TT + 32, 1)
    if fast_ok is None:
        return safe()
    wfast = FAST_TILE_MAX + 16 + FAST_ROWS
    return lax.cond(fast_ok, lambda: _combine_call(cnt, x2d, posm, aff_tm, mvec, y_hl, cap, TT, FAST_ROWS, wfast, 4),
                    safe)


def _rope_tables(n):
    half = NA_HD // 2
    q = half // 2
    inv = ROPE_THETA ** (-np.arange(q, dtype=np.float64) / q)
    pos = np.arange(n)
    ang_r = (pos // GRID_W)[:, None] * inv
    ang_c = (pos % GRID_W)[:, None] * inv
    zero = np.zeros_like(ang_r)
    c = np.concatenate([np.cos(ang_r)] * 2 + [np.cos(ang_c)] * 2, axis=1)
    s1 = np.concatenate([-np.sin(ang_r), zero, -np.sin(ang_c), zero], axis=1)
    s2 = np.concatenate([zero, np.sin(ang_r), zero, np.sin(ang_c)], axis=1)
    two = lambda a: jnp.asarray(np.concatenate([a, a], axis=1).astype(np.float32))
    return two(c), two(s1), two(s2)


def _mixing(hx, hc, need_ctx, B, N, NC, la, lc, w_in, q_gain, k_gain, table, rope, bd, conv_w, conv_b,
            spec, skip, filt_c, wa, wb, wc, wo, x2d, c2d, mx2, mc2):
    tile8 = lambda v: jnp.tile(v.reshape(1, NA_HD), (1, NA_HEADS))
    qg, kg = tile8(q_gain), tile8(k_gain)
    norm_aux = [("col", kg), ("const", bd)]
    rope_aux = [("row", rope[0]), ("row", rope[1]), ("row", rope[2])]
    lf_aux = lambda d: [("col", la[d:d + 1]), ("col", lc[d:d + 1])]
    tc = hc.shape[0]

    lff_c = project(hc, w_in, OFF_FF, 512, _epi_logforget, lf_aux(0), (f32,), tm=tc).reshape(B, NC, 512)
    lfb_c = project(hc, w_in, OFF_FB, 512, _epi_logforget, lf_aux(1), (f32,), tm=tc).reshape(B, NC, 512)
    i_c = project(hc, w_in, OFF_I, 512, _epi_raw, tm=tc).reshape(B, NC, 512)
    k_c = project(hc, w_in, OFF_NK, 512, _epi_norm, norm_aux, tm=tc).reshape(B, NC, 512)
    v_c = project(hc, w_in, OFF_NV, 512, _epi_raw, tm=tc).reshape(B, NC, 512)
    if need_ctx:
        q_c = project(hc, w_in, OFF_HQ, 512, _epi_silu, tm=tc).reshape(B, NC, 512)
    else:
        q_c = jnp.zeros((B, NC, 512), bf16)
    s0 = jnp.zeros((B, HG_HEADS, LANES, LANES), f32)
    oc_f, oc_b, s_f, s_b = hgrn_bidir(lff_c, lfb_c, i_c, q_c, s0, s0)

    lff_x = project(hx, w_in, OFF_FF, 512, _epi_logforget, lf_aux(0), (f32,)).reshape(B, N, 512)
    lfb_x = project(hx, w_in, OFF_FB, 512, _epi_logforget, lf_aux(1), (f32,)).reshape(B, N, 512)
    i_x = project(hx, w_in, OFF_I, 512, _epi_raw).reshape(B, N, 512)
    q_x = project(hx, w_in, OFF_HQ, 512, _epi_silu).reshape(B, N, 512)
    g_x = project(hx, w_in, OFF_HG, 512, _epi_silu).reshape(B, N, 512)
    k_x = project(hx, w_in, OFF_NK, 512, _epi_norm_rope, norm_aux + rope_aux, rows_per_seq=N).reshape(B, N, 512)
    v_x = project(hx, w_in, OFF_NV, 512, _epi_raw).reshape(B, N, 512)
    qn_x, qr_x = project(hx, w_in, OFF_NQ, 512, _epi_norm_both, [("col", qg), ("const", bd)] + rope_aux,
                         (bf16, bf16), rows_per_seq=N)
    p_x = project(hx, w_in, OFF_HY, 3 * HY_W, _epi_raw, out_dtypes=(f32,)).reshape(B, N, 3 * HY_W)
    gates_x = project(hx, w_in, OFF_GATE, 3 * D_MODEL, _epi_sigmoid)

    ox_f, ox_b, _, _ = hgrn_bidir(lff_x, lfb_x, i_x, q_x, s_f, s_b)

    nb_x = neighbourhood_attention(qr_x.reshape(B, N, 512), qn_x.reshape(B, N, 512), k_x, v_x, k_c, v_c, table)

    z_x, x0_x = hyena_pre(p_x, conv_w, conv_b)
    hy_x = hyena_conv(z_x, x0_x, spec[0], spec[1], skip, spec[2])

    flat = lambda a: a.reshape(-1, a.shape[-1])
    x_new = merge(flat(ox_f), flat(ox_b), flat(g_x), flat(nb_x), flat(hy_x), gates_x, wa, wb, wc, wo, x2d, mx2, N)
    if not need_ctx:
        return x_new, None

    qn_c = project(hc, w_in, OFF_NQ, 512, _epi_norm, [("col", qg), ("const", bd)], tm=tc).reshape(B, NC, 512)
    nb_c = context_attention(qn_c, k_c, v_c)
    p_c = project(hc, w_in, OFF_HY, 3 * HY_W, _epi_raw, out_dtypes=(f32,), tm=tc).reshape(B, NC, 3 * HY_W)
    hy_c = hyena_ctx(p_c, conv_w, conv_b, filt_c[0], filt_c[1], skip)
    gates_c = project(hc, w_in, OFF_GATE, 3 * D_MODEL, _epi_sigmoid, tm=tc)
    g_c = project(hc, w_in, OFF_HG, 512, _epi_silu, tm=tc)
    c_new = merge(flat(oc_f), flat(oc_b), g_c, flat(nb_c), flat(hy_c), gates_c, wa, wb, wc, wo, c2d, mc2, B * NC)
    return x_new, c_new


def kernel(x, c, ctx, c_ctx, w_mod, b_mod, norm_mix, norm_ffn, w_in, hg_lb, na_q_gain, na_k_gain, na_rpb,
           hy_conv_w, hy_conv_b, hy_pe_w1, hy_pe_b1, hy_pe_freq1, hy_pe_w2, hy_pe_b2, hy_pe_freq2, hy_pe_w3,
           hy_skip, w_branch_a, w_branch_b, w_branch_c, w_out, w_router, w_e_gate, w_e_up, w_e_down):
    B, N, D = x.shape
    NC = ctx.shape[1]
    E = N_EXPERTS
    cap_x = EC_CAP_FACTOR * N // E
    cap_c = EC_CAP_FACTOR * NC // E

    lb = jnp.cumsum(jax.nn.softmax(hg_lb.astype(f32), axis=0), axis=0)
    lb = lb - lb[:1]
    la_all, lc_all = jnp.log(lb), jnp.log1p(-lb)

    s8 = jnp.zeros((8, D), f32).at[:B].set(c).at[B].set(c_ctx)
    rope = _rope_tables(N)
    bd = jnp.asarray(np.kron(np.eye(NA_HEADS), np.full((NA_HD, NA_HD), 1.0 / NA_HD)).astype(np.float32), bf16)
    dcx = _dft_consts(N)

    x2d = x.reshape(B * N, D)
    c2d = ctx.reshape(B * NC, D)
    for l in range(DEPTH):
        need_ctx = l < DEPTH - 1
        mv = modvec(s8, w_mod[l], b_mod[l])
        mx = [mv[:B, k * D:(k + 1) * D] for k in range(6)]
        mc = [mv[B:B + 1, k * D:(k + 1) * D] for k in range(6)]
        w_in_l = w_in[l].astype(bf16)
        hx = modulate(x2d, norm_mix[l], mx[0], mx[1], N, bf16)
        hc = modulate(c2d, norm_mix[l], mc[0], mc[1], B * NC, bf16)

        filt = (hy_pe_w1[l], hy_pe_b1[l], hy_pe_freq1[l], hy_pe_w2[l], hy_pe_b2[l], hy_pe_freq2[l], hy_pe_w3[l])
        k_x, nrm_x = hyena_filter(N, *filt)
        sre, sim = hyena_spectrum(k_x, 1.0 / nrm_x, dcx)
        skip = hy_skip[l].reshape(1, HY_W)
        filt_c = None
        if need_ctx:
            h_c, nrm_c = hyena_filter(NC, *filt)
            filt_c = (h_c, 1.0 / nrm_c)

        x2d, c_new = _mixing(
            hx, hc, need_ctx, B, N, NC, la_all[l], lc_all[l], w_in_l, na_q_gain[l], na_k_gain[l],
            _na_bias_table(na_rpb[l]), rope, bd, hy_conv_w[l], hy_conv_b[l], (sre, sim, dcx), skip,
            filt_c, w_branch_a[l].astype(bf16), w_branch_b[l].astype(bf16), w_branch_c[l].astype(bf16),
            w_out[l].astype(bf16), x2d, c2d, mx[2], mc[2])

        h2, aff_t, aff_m = router(x2d, norm_ffn[l], mx[3], mx[4], w_router[l], N)
        posm, cnt = select_topk(aff_t, cap_x)
        fast_ok = _fits_fast(cnt, N)
        xgs = [gather_rows(cnt, h2, posm, cap_x, 1024, fast_ok)]
        if need_ctx:
            c2d = c_new
            hc2, aff_tc, aff_mc = router(c2d, norm_ffn[l], mc[3], mc[4], w_router[l], NC)
            posm_c, cnt_c = select_topk(aff_tc, cap_c)
            xgs.append(gather_rows(cnt_c, hc2, posm_c, cap_c, NC))
        ys = expert_ffn(xgs, l, w_e_gate, w_e_up, w_e_down)
        x2d = combine(cnt, x2d, posm, aff_m, mx[5], ys[0], cap_x, CMB_TILE, fast_ok)
        if need_ctx:
            c2d = combine(cnt_c, c2d, posm_c, aff_mc, mc[5], ys[1], cap_c, NC)
    return x2d.reshape(B, N, D)
```

```python
import functools
import math

import numpy as np
import jax
import jax.numpy as jnp
from jax import lax
from jax.experimental import pallas as pl
from jax.experimental.pallas import tpu as pltpu

f32 = jnp.float32
bf16 = jnp.bfloat16
i32 = jnp.int32
HI = lax.Precision.HIGHEST

D_MODEL = 1024
DEPTH = 2
GRID_W = 64
EPS = 1e-6
HG_HEADS = 4
HG_W = 512
HG_CHUNK = 64
NA_HEADS = 8
NA_HD = 64
NA_W = 512
NA_WIN_R = 8
NA_WIN_C = 16
ROPE_THETA = 10000.0
HY_W = 512
HY_BANDS = 16
HY_PE_DIM = 1 + 2 * HY_BANDS
HY_FILT_HID = 64
HY_FAST_DECAY = 0.3
HY_SLOW_DECAY = 1.5
HY_TARGET = 1e-2
OFF_FF = 0
OFF_FB = 512
OFF_I = 1024
OFF_NK = 1536
OFF_NV = 2048
OFF_HQ = 2560
OFF_NQ = 3072
OFF_HG = 3584
OFF_HY = 4096
OFF_GATE = 5632
IN_COLS = 8704
N_EXPERTS = 16
EC_CAP_FACTOR = 2
D_FF_EXPERT = 2816

LANES = 128
NEG_BIG = -1e30
VMEM_LIMIT = 56 * 1024 * 1024


def _cparams(sem, vmem=VMEM_LIMIT):
    return pltpu.CompilerParams(dimension_semantics=sem, vmem_limit_bytes=vmem)


def _nt(a, b, precision=None):
    return lax.dot_general(a, b, (((1,), (1,)), ((), ())), precision=precision, preferred_element_type=f32)


def _tn(a, b, precision=None):
    return lax.dot_general(a, b, (((0,), (0,)), ((), ())), precision=precision, preferred_element_type=f32)


def _silu(x):
    return x * jax.nn.sigmoid(x)


def _split(x):
    hi = x.astype(bf16)
    return hi, (x - hi.astype(f32)).astype(bf16)


def _modvec_kernel(s_ref, w_ref, b_ref, o_ref):
    s = _silu(s_ref[...])
    o_ref[...] = jnp.dot(s, w_ref[...], precision=HI, preferred_element_type=f32) + b_ref[...]


def modvec(s8, w, b):
    n = w.shape[1]
    tn = 1024
    return pl.pallas_call(
        _modvec_kernel,
        grid=(n // tn,),
        in_specs=[pl.BlockSpec((8, D_MODEL), lambda j: (0, 0)),
                  pl.BlockSpec((D_MODEL, tn), lambda j: (0, j)),
                  pl.BlockSpec((1, tn), lambda j: (0, j))],
        out_specs=pl.BlockSpec((8, tn), lambda j: (0, j)),
        out_shape=jax.ShapeDtypeStruct((8, n), f32),
        compiler_params=_cparams(("parallel",)),
        name="modvec",
    )(s8, w, b.reshape(1, n))


def _modulate_kernel(x_ref, g_ref, sh_ref, sc_ref, o_ref):
    x = x_ref[...]
    ms = jnp.mean(x * x, axis=-1, keepdims=True)
    y = x * lax.rsqrt(ms + EPS)
    o_ref[...] = (y * g_ref[...] * (1.0 + sc_ref[0]) + sh_ref[0]).astype(o_ref.dtype)


def modulate(x2d, g, shift, scale, rows_per_group, out_dtype):
    R = x2d.shape[0]
    tm = 512
    tpg = rows_per_group // tm
    G = shift.shape[0]
    return pl.pallas_call(
        _modulate_kernel,
        grid=(R // tm,),
        in_specs=[pl.BlockSpec((tm, D_MODEL), lambda i: (i, 0)),
                  pl.BlockSpec((1, D_MODEL), lambda i: (0, 0)),
                  pl.BlockSpec((1, 1, D_MODEL), lambda i: (i // tpg, 0, 0)),
                  pl.BlockSpec((1, 1, D_MODEL), lambda i: (i // tpg, 0, 0))],
        out_specs=pl.BlockSpec((tm, D_MODEL), lambda i: (i, 0)),
        out_shape=jax.ShapeDtypeStruct((R, D_MODEL), out_dtype),
        compiler_params=_cparams(("parallel",)),
        name="modulate",
    )(x2d, g.reshape(1, D_MODEL), shift.reshape(G, 1, D_MODEL), scale.reshape(G, 1, D_MODEL))


def _log_sigmoid(z):
    return jnp.minimum(z, 0.0) - jnp.log1p(jnp.exp(-jnp.abs(z)))


def _epi_raw(acc, o_ref):
    o_ref[...] = acc.astype(o_ref.dtype)


def _epi_silu(acc, o_ref):
    o_ref[...] = _silu(acc).astype(o_ref.dtype)


def _epi_sigmoid(acc, o_ref):
    o_ref[...] = jax.nn.sigmoid(acc).astype(o_ref.dtype)


def _epi_logforget(acc, la_ref, lc_ref, o_ref):
    la = la_ref[...]
    c = lc_ref[...] + _log_sigmoid(acc)
    o_ref[...] = jnp.maximum(la, c) + jnp.log1p(jnp.exp(-jnp.abs(la - c)))


def _head_rms(acc, gain_ref, bd_ref):
    hi, lo = _split(acc * acc)
    ms = jnp.dot(hi, bd_ref[...], preferred_element_type=f32) + jnp.dot(lo, bd_ref[...], preferred_element_type=f32)
    return acc * lax.rsqrt(ms + EPS) * gain_ref[...]


def _rope(y, c_ref, s1_ref, s2_ref):
    reps = y.shape[1] // LANES
    c = jnp.concatenate([c_ref[...]] * reps, axis=1)
    s1 = jnp.concatenate([s1_ref[...]] * reps, axis=1)
    s2 = jnp.concatenate([s2_ref[...]] * reps, axis=1)
    w = y.shape[1]
    return y * c + pltpu.roll(y, w - 16, axis=1) * s1 + pltpu.roll(y, 16, axis=1) * s2


def _epi_norm(acc, gain_ref, bd_ref, o_ref):
    o_ref[...] = _head_rms(acc, gain_ref, bd_ref).astype(o_ref.dtype)


def _epi_norm_rope(acc, gain_ref, bd_ref, c_ref, s1_ref, s2_ref, o_ref):
    y = _head_rms(acc, gain_ref, bd_ref)
    o_ref[...] = _rope(y, c_ref, s1_ref, s2_ref).astype(o_ref.dtype)


def _epi_norm_both(acc, gain_ref, bd_ref, c_ref, s1_ref, s2_ref, on_ref, or_ref):
    y = _head_rms(acc, gain_ref, bd_ref)
    on_ref[...] = y.astype(on_ref.dtype)
    or_ref[...] = _rope(y, c_ref, s1_ref, s2_ref).astype(or_ref.dtype)


def _proj_kernel(h_ref, w_ref, *rest, epi):
    acc = jnp.dot(h_ref[...], w_ref[...], preferred_element_type=f32)
    epi(acc, *rest)


def project(h, w, c0, width, epi, aux=(), out_dtypes=(bf16,), tm=2048, rows_per_seq=None):
    R = h.shape[0]
    tn = 512
    nj = width // tn
    cb = c0 // tn
    in_specs = [pl.BlockSpec((tm, D_MODEL), lambda i, j: (i, 0)),
                pl.BlockSpec((D_MODEL, tn), lambda i, j: (0, cb + j))]
    args = [h, w]
    for kind, arr in aux:
        if kind == "col":
            in_specs.append(pl.BlockSpec((1, tn), lambda i, j: (0, j)))
        elif kind == "const":
            in_specs.append(pl.BlockSpec(arr.shape, lambda i, j: (0, 0)))
        else:
            tps = rows_per_seq // tm
            in_specs.append(pl.BlockSpec((tm, LANES), lambda i, j: (i % tps, 0)))
        args.append(arr)
    out_specs = [pl.BlockSpec((tm, tn), lambda i, j: (i, j)) for _ in out_dtypes]
    out_shape = [jax.ShapeDtypeStruct((R, width), dt) for dt in out_dtypes]
    res = pl.pallas_call(
        functools.partial(_proj_kernel, epi=epi),
        grid=(R // tm, nj),
        in_specs=in_specs,
        out_specs=out_specs,
        out_shape=out_shape,
        compiler_params=_cparams(("parallel", "parallel")),
        name="proj_" + epi.__name__[5:],
    )(*args)
    return res[0] if len(res) == 1 else res


def _hgrn_tmatrix(C, reverse):
    L = int(round(math.log2(C)))
    t = np.arange(C)
    tau = (C - 1 - t) if reverse else t
    tt, uu = tau[:, None], tau[None, :]
    T = np.zeros((2 + L, C, C), np.float32)
    T[0] = uu <= tt
    T[1] = uu > tt
    for l in range(L):
        same = (tt >> (l + 1)) == (uu >> (l + 1))
        tr = ((tt >> l) & 1) == 1
        ur = ((uu >> l) & 1) == 1
        T[2 + l] = same & ((tr & ur & (uu <= tt)) | (~tr & ~ur & (uu > tt)))
    return T.reshape((2 + L) * C, C)


def _hgrn_step(chains, C):
    L = int(round(math.log2(C)))
    row = lax.broadcasted_iota(i32, (C, LANES), 0)
    ti0 = lax.broadcasted_iota(i32, (C, C), 0)
    si0 = lax.broadcasted_iota(i32, (C, C), 1)

    es, kcs = [], []
    for lf, _, _, tm, _, _ in chains:
        W = lf.shape[1]
        hi, lo = _split(lf)
        r = jnp.dot(tm, jnp.concatenate([hi, lo], axis=1), preferred_element_type=f32)
        es.append(r[:, :W] + r[:, W:])
        kcs.append(1.0 - jnp.exp(lf))

    items = []
    for (lf, v_all, q_all, _, s_view, reverse), E, kc_all in zip(chains, es, kcs):
        tau = (C - 1 - row) if reverse else row
        ti, si = (C - 1 - ti0, C - 1 - si0) if reverse else (ti0, si0)
        end_row = 0 if reverse else C - 1
        for h in range(HG_HEADS):
            sl = slice(LANES * h, LANES * (h + 1))
            qh, kch, vh = q_all[:, sl], kc_all[:, sl], v_all[:, sl]
            b = E[0:C, sl]
            st = s_view[h]
            pairs = [(qh.astype(bf16), kch.astype(bf16), ti == si)]
            for l in range(L):
                x = jnp.exp(E[(2 + l) * C:(3 + l) * C, sl])
                later = ((tau >> l) & 1) == 1
                pairs.append((jnp.where(later, qh * x, 0.0).astype(bf16), jnp.where(later, 0.0, kch * x).astype(bf16),
                              (ti >> (l + 1)) == (si >> (l + 1))))
            items.append(dict(qb=(qh * jnp.exp(b)).astype(bf16), st=st, pairs=pairs, vh=vh,
                              kd=(kch * jnp.exp(E[C:2 * C, sl])).astype(bf16),
                              decay=jnp.exp(b[end_row:end_row + 1, :]), view=s_view, h=h))

    for it in items:
        it["o"] = _nt(it["qb"], it["st"].astype(bf16))
        it["att"] = [(_nt(ql, kl), m) for ql, kl, m in it["pairs"]]

    outs = []
    for it in items:
        att = None
        for a, m in it["att"]:
            t = jnp.where(m, a, 0.0)
            att = t if att is None else att + t
        it["o"] = it["o"] + jnp.dot(att.astype(bf16), it["vh"], preferred_element_type=f32)
        it["view"][it["h"]] = it["st"] * it["decay"] + _tn(it["vh"], it["kd"])
    nh = HG_HEADS
    for c in range(len(chains)):
        outs.append(jnp.concatenate([items[c * nh + h]["o"] for h in range(nh)], axis=1))
    return outs


def _hgrn_kernel(lff_ref, lfb_ref, vf_ref, vb_ref, qf_ref, qb_ref, s0f_ref, s0b_ref, tf_ref, tb_ref,
                 of_ref, ob_ref, sff_ref, sfb_ref, s_scr, *, C, B):
    c = pl.program_id(0)

    @pl.when(c == 0)
    def _():
        s_scr[0] = s0f_ref[...]
        s_scr[1] = s0b_ref[...]

    chains = []
    for b in range(B):
        chains.append((lff_ref[b], vf_ref[b], qf_ref[b].astype(f32), tf_ref[...], s_scr.at[0, b], False))
        chains.append((lfb_ref[b], vb_ref[b], qb_ref[b].astype(f32), tb_ref[...], s_scr.at[1, b], True))
    outs = _hgrn_step(chains, C)
    for b in range(B):
        of_ref[b] = outs[2 * b]
        ob_ref[b] = outs[2 * b + 1]

    @pl.when(c == pl.num_programs(0) - 1)
    def _():
        sff_ref[...] = s_scr[0]
        sfb_ref[...] = s_scr[1]


def hgrn_bidir(lf_f, lf_b, v, q, s0_f, s0_b):
    B, N, W = lf_f.shape
    C = HG_CHUNK
    nch = N // C
    tf = jnp.asarray(_hgrn_tmatrix(C, False), bf16)
    tb = jnp.asarray(_hgrn_tmatrix(C, True), bf16)
    fw = pl.BlockSpec((B, C, W), lambda c: (0, c, 0))
    bw = pl.BlockSpec((B, C, W), lambda c: (0, nch - 1 - c, 0))
    st = pl.BlockSpec((B, HG_HEADS, LANES, LANES), lambda c: (0, 0, 0, 0))
    tsp = pl.BlockSpec(tf.shape, lambda c: (0, 0))
    seq = jax.ShapeDtypeStruct((B, N, W), f32)
    sts = jax.ShapeDtypeStruct((B, HG_HEADS, LANES, LANES), f32)
    return pl.pallas_call(
        functools.partial(_hgrn_kernel, C=C, B=B),
        grid=(nch,),
        in_specs=[fw, bw, fw, bw, fw, bw, st, st, tsp, tsp],
        out_specs=[fw, bw, st, st],
        out_shape=[seq, seq, sts, sts],
        scratch_shapes=[pltpu.VMEM((2, B, HG_HEADS, LANES, LANES), f32)],
        compiler_params=_cparams(("arbitrary",)),
        name="hgrn",
    )(lf_f, lf_b, v, v, q, q, s0_f, s0_b, tf, tb)


def _na_kernel(qr_ref, qn_ref, k_ref, v_ref, kc_ref, vc_ref, tab_ref, o_ref, s_scr, p_scr, *, rows_per_step, n_rows):
    g = pl.program_id(2)
    scale = NA_HD ** -0.5
    lane = lax.broadcasted_iota(i32, (GRID_W, LANES), 1)
    kcx = kc_ref[0]
    vcx = vc_ref[0]
    win = NA_WIN_R * GRID_W
    ctx_len = kcx.shape[0]

    starts = []
    for i in range(rows_per_step):
        r = g * rows_per_step + i
        rs = jnp.clip(r - NA_WIN_R // 2, 0, n_rows - NA_WIN_R)
        off = rs - r + (NA_WIN_R - 1)
        start = pl.multiple_of(rs * GRID_W, GRID_W)
        starts.append(start)
        kw = k_ref[0, pl.ds(start, win), :]
        qr = qr_ref[0, i * GRID_W:(i + 1) * GRID_W, :]
        qn = qn_ref[0, i * GRID_W:(i + 1) * GRID_W, :]
        zq = jnp.zeros_like(qr)
        qrs = jnp.concatenate([jnp.where(lane < NA_HD, qr, zq), jnp.where(lane < NA_HD, zq, qr)], axis=0)
        qns = jnp.concatenate([jnp.where(lane < NA_HD, qn, zq), jnp.where(lane < NA_HD, zq, qn)], axis=0)
        bias = jnp.concatenate([tab_ref[0, off], tab_ref[1, off]], axis=0)
        row0 = 2 * i * GRID_W
        s_scr[row0:row0 + 2 * GRID_W, 0:win] = _nt(qrs, kw) * scale + bias
        s_scr[row0:row0 + 2 * GRID_W, win:win + ctx_len] = _nt(qns, kcx) * scale

    def softmax_rows(c, carry):
        r0 = pl.multiple_of(c * LANES, LANES)
        s = s_scr[pl.ds(r0, LANES), :]
        p = jnp.exp(s - jnp.max(s, axis=-1, keepdims=True))
        inv = 1.0 / jnp.sum(p, axis=-1, keepdims=True)
        p_scr[pl.ds(r0, LANES), :] = (p * inv).astype(bf16)
        return carry

    lax.fori_loop(0, 2 * rows_per_step * GRID_W // LANES, softmax_rows, 0, unroll=2)

    for i in range(rows_per_step):
        vw = v_ref[0, pl.ds(starts[i], win), :]
        row0 = 2 * i * GRID_W
        p = p_scr[row0:row0 + 2 * GRID_W, :]
        res = (jnp.dot(p[:, :win], vw, preferred_element_type=f32)
               + jnp.dot(p[:, win:], vcx, preferred_element_type=f32))
        o_ref[0, i * GRID_W:(i + 1) * GRID_W, :] = jnp.where(lane < NA_HD, res[:GRID_W], res[GRID_W:]).astype(o_ref.dtype)


def _na_bias_table(rpb):
    col = jnp.arange(GRID_W)
    cs = jnp.clip(col - NA_WIN_C // 2, 0, GRID_W - NA_WIN_C)
    kc = jnp.arange(GRID_W)
    valid = (kc[None, :] >= cs[:, None]) & (kc[None, :] < cs[:, None] + NA_WIN_C)
    dc = jnp.clip(kc[None, :] - col[:, None] + (NA_WIN_C - 1), 0, 2 * NA_WIN_C - 2)
    bc = jnp.where(valid[None, None], rpb[:, :, dc], NEG_BIG)
    t2 = jnp.stack([bc[:, o:o + NA_WIN_R] for o in range(NA_WIN_R)], axis=1)
    t2 = t2.transpose(0, 1, 3, 2, 4)
    return t2.reshape(NA_HEADS, NA_WIN_R, GRID_W, NA_WIN_R * GRID_W).astype(f32)


def neighbourhood_attention(q_rot, qn, k_rot, v, kc, vc, table):
    B, N, W = q_rot.shape
    n_rows = N // GRID_W
    rps = 8
    ctx_len = kc.shape[1]
    pairs = W // LANES
    keys = NA_WIN_R * GRID_W + ctx_len
    return pl.pallas_call(
        functools.partial(_na_kernel, rows_per_step=rps, n_rows=n_rows),
        grid=(B, pairs, n_rows // rps),
        in_specs=[pl.BlockSpec((1, rps * GRID_W, LANES), lambda b, p, g: (b, g, p)),
                  pl.BlockSpec((1, rps * GRID_W, LANES), lambda b, p, g: (b, g, p)),
                  pl.BlockSpec((1, N, LANES), lambda b, p, g: (b, 0, p)),
                  pl.BlockSpec((1, N, LANES), lambda b, p, g: (b, 0, p)),
                  pl.BlockSpec((1, ctx_len, LANES), lambda b, p, g: (b, 0, p)),
                  pl.BlockSpec((1, ctx_len, LANES), lambda b, p, g: (b, 0, p)),
                  pl.BlockSpec((2, NA_WIN_R, GRID_W, NA_WIN_R * GRID_W), lambda b, p, g: (p, 0, 0, 0))],
        out_specs=pl.BlockSpec((1, rps * GRID_W, LANES), lambda b, p, g: (b, g, p)),
        out_shape=jax.ShapeDtypeStruct((B, N, W), bf16),
        scratch_shapes=[pltpu.VMEM((2 * rps * GRID_W, keys), f32), pltpu.VMEM((2 * rps * GRID_W, keys), bf16)],
        compiler_params=_cparams(("parallel", "parallel", "arbitrary")),
        name="natten",
    )(q_rot, qn, k_rot, v, kc, vc, table)


def _ctx_attn_kernel(q_ref, k_ref, v_ref, o_ref):
    scale = NA_HD ** -0.5
    q = q_ref[0]
    k = k_ref[0]
    v = v_ref[0]
    lane = lax.broadcasted_iota(i32, q.shape, 1)
    res = []
    for hh in range(2):
        m = (lane >= NA_HD * hh) & (lane < NA_HD * (hh + 1))
        s = _nt(jnp.where(m, q, jnp.zeros_like(q)), k) * scale
        p = jnp.exp(s - jnp.max(s, axis=-1, keepdims=True))
        p = p / jnp.sum(p, axis=-1, keepdims=True)
        res.append(jnp.dot(p.astype(bf16), v, preferred_element_type=f32))
    o_ref[0] = jnp.where(lane < NA_HD, res[0], res[1]).astype(o_ref.dtype)


def context_attention(q, k, v):
    B, N, W = q.shape
    spec = pl.BlockSpec((1, N, LANES), lambda b, p: (b, 0, p))
    return pl.pallas_call(
        _ctx_attn_kernel,
        grid=(B, W // LANES),
        in_specs=[spec, spec, spec],
        out_specs=spec,
        out_shape=jax.ShapeDtypeStruct((B, N, W), bf16),
        compiler_params=_cparams(("parallel", "parallel")),
        name="ctx_attn",
    )(q, k, v)


def _filter_kernel(z_ref, w1_ref, b1_ref, f1_ref, w2_ref, b2_ref, f2_ref, w3_ref, dl_ref, k_ref, nrm_ref, *, tm, n):
    i = pl.program_id(0)
    z = z_ref[...]
    a = jnp.sin(f1_ref[...] * (jnp.dot(z, w1_ref[...], precision=HI, preferred_element_type=f32) + b1_ref[...]))
    a = jnp.sin(f2_ref[...] * (jnp.dot(a, w2_ref[...], precision=HI, preferred_element_type=f32) + b2_ref[...]))
    h = jnp.dot(a, w3_ref[...], precision=HI, preferred_element_type=f32)
    h = h * jnp.exp(-z[:, 0:1] * dl_ref[...])
    row = lax.broadcasted_iota(i32, (tm, HY_W), 0) + i * tm
    k = jnp.where(row == n, 0.0, h)
    k_ref[...] = k
    part = jnp.sum(jnp.abs(k), axis=0, keepdims=True)

    @pl.when(i == 0)
    def _():
        nrm_ref[...] = part

    @pl.when(i > 0)
    def _():
        nrm_ref[...] = nrm_ref[...] + part


def hyena_filter(n, w1, b1, fr1, w2, b2, fr2, w3):
    t = np.linspace(0.0, 1.0, n)[:, None]
    w = 2 * math.pi * np.arange(n)[:, None] / n
    fb = np.linspace(1e-4, HY_BANDS - 1, HY_BANDS)[None]
    z = np.concatenate([t, np.cos(fb * w), -np.sin(fb * w)], axis=-1)
    z = np.concatenate([z, np.zeros((1, HY_PE_DIM)), z[:0:-1]], axis=0)
    z = jnp.asarray(np.pad(z, ((0, 0), (0, LANES - HY_PE_DIM))).astype(np.float32))
    w1p = jnp.pad(w1.astype(f32), ((0, LANES - HY_PE_DIM), (0, 0)))
    deltas = jnp.asarray(np.abs(np.linspace(math.log(HY_TARGET) / HY_SLOW_DECAY, math.log(HY_TARGET) / HY_FAST_DECAY,
                                            2 * HY_W))[None].astype(np.float32))
    tm = min(n, 512)
    hid = HY_FILT_HID
    full = lambda shape: pl.BlockSpec(shape, lambda i: (0, 0))
    tph = n // tm
    return pl.pallas_call(
        functools.partial(_filter_kernel, tm=tm, n=n),
        grid=(2 * n // tm,),
        in_specs=[pl.BlockSpec((tm, LANES), lambda i: (i, 0)),
                  full((LANES, hid)), full((1, hid)), full((1, hid)),
                  full((hid, hid)), full((1, hid)), full((1, hid)),
                  pl.BlockSpec((hid, HY_W), lambda i: (0, i // tph)),
                  pl.BlockSpec((1, HY_W), lambda i: (0, i // tph))],
        out_specs=[pl.BlockSpec((tm, HY_W), lambda i: (i, 0)), pl.BlockSpec((1, HY_W), lambda i: (0, 0))],
        out_shape=[jax.ShapeDtypeStruct((2 * n, HY_W), f32), jax.ShapeDtypeStruct((1, HY_W), f32)],
        compiler_params=_cparams(("arbitrary",)),
        name="hyena_filter",
    )(z, w1p, b1.reshape(1, hid), fr1.reshape(1, hid), w2.astype(f32), b2.reshape(1, hid), fr2.reshape(1, hid),
      w3.astype(f32), deltas)


def _conv3(u, w_ref, b_ref):
    n = u.shape[0]
    row = lax.broadcasted_iota(i32, u.shape, 0)
    prev = jnp.where(row == 0, 0.0, pltpu.roll(u, 1, axis=0))
    nxt = jnp.where(row == n - 1, 0.0, pltpu.roll(u, n - 1, axis=0))
    return prev * w_ref[0:1, :] + u * w_ref[1:2, :] + nxt * w_ref[2:3, :] + b_ref[...]


def _hyena_pre_kernel(p0_ref, p1_ref, p2_ref, w0_ref, w1_ref, w2_ref, b0_ref, b1_ref, b2_ref, z_ref, x0_ref):
    x0_ref[0] = _conv3(p0_ref[0], w0_ref, b0_ref).astype(x0_ref.dtype)
    z_ref[0] = _conv3(p1_ref[0], w1_ref, b1_ref) * _conv3(p2_ref[0], w2_ref, b2_ref)


def hyena_pre(p, conv_w, conv_b):
    B, N, _ = p.shape
    nb = HY_W // LANES
    conv_b = conv_b.reshape(1, 3 * HY_W)
    pspec = lambda g: pl.BlockSpec((1, N, LANES), lambda b, c: (b, 0, g * nb + c))
    wspec = lambda g: pl.BlockSpec((3, LANES), lambda b, c: (0, g * nb + c))
    bspec = lambda g: pl.BlockSpec((1, LANES), lambda b, c: (0, g * nb + c))
    ospec = pl.BlockSpec((1, N, LANES), lambda b, c: (b, 0, c))
    return pl.pallas_call(
        _hyena_pre_kernel,
        grid=(B, nb),
        in_specs=[pspec(0), pspec(1), pspec(2), wspec(0), wspec(1), wspec(2), bspec(0), bspec(1), bspec(2)],
        out_specs=[ospec, ospec],
        out_shape=[jax.ShapeDtypeStruct((B, N, HY_W), f32), jax.ShapeDtypeStruct((B, N, HY_W), bf16)],
        compiler_params=_cparams(("parallel", "parallel")),
        name="hyena_pre",
    )(p, p, p, conv_w, conv_w, conv_w, conv_b, conv_b, conv_b)


DFT_SLABS = 4


def _hl(a):
    a32 = jnp.asarray(a.astype(np.float32))
    hi = a32.astype(bf16)
    lo = (a32 - hi.astype(f32)).astype(bf16)
    return jnp.concatenate([hi, lo], axis=-2)


def _dot3(a_hl, m, x):
    xh, xl = _split(x)
    r = jnp.dot(a_hl, xh, preferred_element_type=f32)
    return r[:m] + r[m:] + jnp.dot(a_hl[:m], xl, preferred_element_type=f32)


def _dft_consts(n):
    N = 2 * n
    na = N // LANES
    t1n = na // 2
    k1n = na // 2 + 1
    k1p = -(-k1n // 8) * 8
    k1 = np.arange(k1n)
    t1 = np.arange(t1n)
    th = 2 * np.pi * ((t1[None, :] * k1[:, None]) % na) / na
    f1c = np.zeros((2 * k1p, t1n))
    f1c[:k1n] = np.cos(th)
    f1c[k1p:k1p + k1n] = -np.sin(th)
    thf = 2 * np.pi * ((np.arange(na)[None, :] * k1[:, None]) % na) / na
    f1f = np.zeros((2 * k1p, na))
    f1f[:k1n] = np.cos(thf)
    f1f[k1p:k1p + k1n] = -np.sin(thf)
    k2 = np.arange(LANES)
    t2 = np.arange(LANES)
    m = (t2[None, None, :] * (k1[:, None, None] + na * k2[None, :, None])) % N
    ph = 2 * np.pi * m / N
    g = np.concatenate([np.cos(ph), -np.sin(ph)], axis=1)
    pht = ph.transpose(0, 2, 1)
    gi = np.concatenate([np.cos(pht), np.sin(pht)], axis=1)
    wk = np.where((k1 == 0) | (k1 == na // 2), 1.0, 2.0) / N
    f1i = np.zeros((t1n, 2 * k1p))
    f1i[:, :k1n] = np.cos(th.T) * wk[None, :]
    f1i[:, k1p:k1p + k1n] = -np.sin(th.T) * wk[None, :]
    k1e = -(-k1n // DFT_SLABS) * DFT_SLABS
    g = np.concatenate([g, np.zeros((k1e - k1n,) + g.shape[1:])], axis=0)
    gi = np.concatenate([gi, np.zeros((k1e - k1n,) + gi.shape[1:])], axis=0)
    return dict(na=na, t1n=t1n, k1n=k1n, k1e=k1e, k1p=k1p, f1c=_hl(f1c), f1f=_hl(f1f), g=_hl(g), gi=_hl(gi),
                f1i=_hl(f1i))


def _dft_stage1(src_ref, f1c_ref, are_ref, aim_ref, t1n, k1p):
    f1c = f1c_ref[...]

    def body(t2, carry):
        zs = src_ref[pl.ds(t2, t1n, stride=LANES), :]
        r = _dot3(f1c, 2 * k1p, zs)
        are_ref[pl.ds(t2, k1p, stride=LANES), :] = r[:k1p]
        aim_ref[pl.ds(t2, k1p, stride=LANES), :] = r[k1p:]
        return carry

    lax.fori_loop(0, LANES, body, 0, unroll=4)


def _cplx_left(gc_hl, xre, xim):
    cw = xre.shape[1]
    r = _dot3(gc_hl, 2 * LANES, jnp.concatenate([xre, xim], axis=1))
    p, q = r[:, :cw], r[:, cw:]
    return p[:LANES] - q[LANES:], p[LANES:] + q[:LANES]


def _spectrum_kernel(k_ref, f1f_ref, g_ref, inv_ref, xre_ref, xim_ref, are, aim, *, na, k1p):
    j = pl.program_id(1)

    @pl.when(j == 0)
    def _():
        _dft_stage1(k_ref, f1f_ref, are, aim, na, k1p)

    for half in range(DFT_SLABS):
        r0 = pl.multiple_of((DFT_SLABS * j + half) * LANES, LANES)
        xre, xim = _cplx_left(g_ref[half], are[pl.ds(r0, LANES), :], aim[pl.ds(r0, LANES), :])
        rows = slice(half * LANES, (half + 1) * LANES)
        xre_ref[rows, :] = xre * inv_ref[...]
        xim_ref[rows, :] = xim * inv_ref[...]


def hyena_spectrum(k, inv_norm, dc):
    n2, C = k.shape
    k1e, k1p, na = dc["k1e"], dc["k1p"], dc["na"]
    cw = LANES
    out = jax.ShapeDtypeStruct((k1e * LANES, C), f32)
    ospec = pl.BlockSpec((DFT_SLABS * LANES, cw), lambda c, k: (k, c))
    return pl.pallas_call(
        functools.partial(_spectrum_kernel, na=na, k1p=k1p),
        grid=(C // cw, k1e // DFT_SLABS),
        in_specs=[pl.BlockSpec((n2, cw), lambda c, k: (0, c)),
                  pl.BlockSpec(dc["f1f"].shape, lambda c, k: (0, 0)),
                  pl.BlockSpec((DFT_SLABS, 4 * LANES, LANES), lambda c, k: (k, 0, 0)),
                  pl.BlockSpec((1, cw), lambda c, k: (0, c))],
        out_specs=[ospec, ospec],
        out_shape=[out, out],
        scratch_shapes=[pltpu.VMEM((k1p * LANES, cw), f32), pltpu.VMEM((k1p * LANES, cw), f32)],
        compiler_params=_cparams(("parallel", "arbitrary")),
        name="hyena_spectrum",
    )(k, dc["f1f"], dc["g"], inv_norm)


def _hyena_conv_kernel(z_ref, x0_ref, f1c_ref, g_ref, gi_ref, f1i_ref, kre_ref, kim_ref,
                       skip_ref, o_ref, are, aim, y_scr, *, t1n, k1p):
    j = pl.program_id(2)

    @pl.when(j == 0)
    def _():
        _dft_stage1(z_ref.at[0], f1c_ref, are, aim, t1n, k1p)

    for half in range(DFT_SLABS):
        r0 = pl.multiple_of((DFT_SLABS * j + half) * LANES, LANES)
        xre, xim = _cplx_left(g_ref[half], are[pl.ds(r0, LANES), :], aim[pl.ds(r0, LANES), :])
        rows = slice(half * LANES, (half + 1) * LANES)
        kre = kre_ref[rows, :]
        kim = kim_ref[rows, :]
        yre = xre * kre - xim * kim
        yim = xre * kim + xim * kre
        bre, bim = _cplx_left(gi_ref[half], yre, yim)
        are[pl.ds(r0, LANES), :] = bre
        aim[pl.ds(r0, LANES), :] = bim

    @pl.when(j == pl.num_programs(2) - 1)
    def _():
        f1i = f1i_ref[...]

        def body(t2, carry):
            bb = jnp.concatenate([are[pl.ds(t2, k1p, stride=LANES), :], aim[pl.ds(t2, k1p, stride=LANES), :]], axis=0)
            y_scr[pl.ds(t2, t1n, stride=LANES), :] = _dot3(f1i, t1n, bb)
            return carry

        lax.fori_loop(0, LANES, body, 0, unroll=4)
        z = z_ref[0]
        o_ref[0] = (x0_ref[0].astype(f32) * (y_scr[...] + z * skip_ref[...])).astype(o_ref.dtype)


def hyena_conv(z, x0, spec_re, spec_im, skip, dc):
    B, n, W = z.shape
    nb = W // LANES
    k1e, k1p, t1n = dc["k1e"], dc["k1p"], dc["t1n"]
    seq = pl.BlockSpec((1, n, LANES), lambda b, c, k: (b, 0, c))
    fspec = pl.BlockSpec((DFT_SLABS * LANES, LANES), lambda b, c, k: (k, c))
    cspec = pl.BlockSpec((DFT_SLABS, 4 * LANES, LANES), lambda b, c, k: (k, 0, 0))
    vspec = pl.BlockSpec((1, LANES), lambda b, c, k: (0, c))
    return pl.pallas_call(
        functools.partial(_hyena_conv_kernel, t1n=t1n, k1p=k1p),
        grid=(B, nb, k1e // DFT_SLABS),
        in_specs=[seq, seq,
                  pl.BlockSpec(dc["f1c"].shape, lambda b, c, k: (0, 0)), cspec, cspec,
                  pl.BlockSpec(dc["f1i"].shape, lambda b, c, k: (0, 0)),
                  fspec, fspec, vspec],
        out_specs=seq,
        out_shape=jax.ShapeDtypeStruct((B, n, W), bf16),
        scratch_shapes=[pltpu.VMEM((k1p * LANES, LANES), f32), pltpu.VMEM((k1p * LANES, LANES), f32),
                        pltpu.VMEM((n, LANES), f32)],
        compiler_params=_cparams(("parallel", "parallel", "arbitrary")),
        name="hyena_conv",
    )(z, x0, dc["f1c"], dc["g"], dc["gi"], dc["f1i"], spec_re, spec_im, skip)


def _hyena_ctx_kernel(p0_ref, p1_ref, p2_ref, w0_ref, w1_ref, w2_ref, b0_ref, b1_ref, b2_ref,
                      k_ref, inv_ref, skip_ref, fd_ref, fi_ref, o_ref, *, n):
    x0 = _conv3(p0_ref[0], w0_ref, b0_ref)
    z = _conv3(p1_ref[0], w1_ref, b1_ref) * _conv3(p2_ref[0], w2_ref, b2_ref)
    fd = fd_ref[...]
    N = 2 * n
    zf = jnp.dot(fd[:, :n], z, precision=HI, preferred_element_type=f32)
    kf = jnp.dot(fd, k_ref[...], precision=HI, preferred_element_type=f32) * inv_ref[...]
    yre = zf[:N] * kf[:N] - zf[N:] * kf[N:]
    yim = zf[:N] * kf[N:] + zf[N:] * kf[:N]
    y = jnp.dot(fi_ref[...], jnp.concatenate([yre, yim], axis=0), precision=HI, preferred_element_type=f32)
    o_ref[0] = (x0 * (y + z * skip_ref[...])).astype(o_ref.dtype)


def hyena_ctx(p, conv_w, conv_b, k, inv_norm, skip):
    B, n, _ = p.shape
    N = 2 * n
    nb = HY_W // LANES
    kk = np.arange(N)
    ph = 2 * np.pi * ((kk[:, None] * kk[None, :]) % N) / N
    fd = jnp.asarray(np.concatenate([np.cos(ph), -np.sin(ph)], axis=0).astype(np.float32))
    fi = jnp.asarray((np.concatenate([np.cos(ph[:n]), -np.sin(ph[:n])], axis=1) / N).astype(np.float32))
    conv_b = conv_b.reshape(1, 3 * HY_W)
    pspec = lambda g: pl.BlockSpec((1, n, LANES), lambda b, c: (b, 0, g * nb + c))
    wspec = lambda g: pl.BlockSpec((3, LANES), lambda b, c: (0, g * nb + c))
    bspec = lambda g: pl.BlockSpec((1, LANES), lambda b, c: (0, g * nb + c))
    vspec = pl.BlockSpec((1, LANES), lambda b, c: (0, c))
    return pl.pallas_call(
        functools.partial(_hyena_ctx_kernel, n=n),
        grid=(B, nb),
        in_specs=[pspec(0), pspec(1), pspec(2), wspec(0), wspec(1), wspec(2), bspec(0), bspec(1), bspec(2),
                  pl.BlockSpec((N, LANES), lambda b, c: (0, c)),
                  vspec, vspec,
                  pl.BlockSpec(fd.shape, lambda b, c: (0, 0)), pl.BlockSpec(fi.shape, lambda b, c: (0, 0))],
        out_specs=pl.BlockSpec((1, n, LANES), lambda b, c: (b, 0, c)),
        out_shape=jax.ShapeDtypeStruct((B, n, HY_W), bf16),
        compiler_params=_cparams(("parallel", "parallel")),
        name="hyena_ctx",
    )(p, p, p, conv_w, conv_w, conv_w, conv_b, conv_b, conv_b, k, inv_norm, skip, fd, fi)


def _merge_kernel(of_ref, ob_ref, gs_ref, nb_ref, hc_ref, g_ref, wa_ref, wb_ref, wc_ref, wo_ref, x_ref, m_ref, o_ref):
    d = D_MODEL
    tot = of_ref[...] + ob_ref[...]
    gs = gs_ref[...].astype(f32)
    ra = []
    for h in range(HG_HEADS):
        sl = slice(LANES * h, LANES * (h + 1))
        th = tot[:, sl]
        ms = jnp.mean(th * th, axis=-1, keepdims=True)
        ra.append(th * lax.rsqrt(ms + EPS) * gs[:, sl])
    ya = jnp.dot(jnp.concatenate(ra, axis=1).astype(bf16), wa_ref[...], preferred_element_type=f32)
    yb = jnp.dot(nb_ref[...], wb_ref[...], preferred_element_type=f32)
    yc = jnp.dot(hc_ref[...], wc_ref[...], preferred_element_type=f32)
    g = g_ref[...].astype(f32)
    mix = g[:, :d] * ya + g[:, d:2 * d] * yb + g[:, 2 * d:] * yc
    y = jnp.dot(mix.astype(bf16), wo_ref[...], preferred_element_type=f32)
    o_ref[...] = x_ref[...] + m_ref[0] * y


def merge(o_f, o_b, gs, nb, hc, gates, wa, wb, wc, wo, x2d, m, rows_per_group):
    R = x2d.shape[0]
    tm = 512
    tpg = rows_per_group // tm
    G = m.shape[0]
    row = lambda w: pl.BlockSpec((tm, w), lambda i: (i, 0))
    full = lambda a: pl.BlockSpec(a.shape, lambda i: (0, 0))
    return pl.pallas_call(
        _merge_kernel,
        grid=(R // tm,),
        in_specs=[row(HG_W), row(HG_W), row(HG_W), row(NA_W), row(HY_W), row(3 * D_MODEL),
                  full(wa), full(wb), full(wc), full(wo),
                  row(D_MODEL), pl.BlockSpec((1, 1, D_MODEL), lambda i: (i // tpg, 0, 0))],
        out_specs=row(D_MODEL),
        out_shape=jax.ShapeDtypeStruct((R, D_MODEL), f32),
        compiler_params=_cparams(("parallel",)),
        name="merge",
    )(o_f, o_b, gs, nb, hc, gates, wa, wb, wc, wo, x2d, m.reshape(G, 1, D_MODEL))


def _router_kernel(x_ref, g_ref, sh_ref, sc_ref, wrt_ref, wr_ref, h_ref, at_ref, am_ref):
    x = x_ref[...]
    ms = jnp.mean(x * x, axis=-1, keepdims=True)
    h = x * lax.rsqrt(ms + EPS) * g_ref[...] * (1.0 + sc_ref[0]) + sh_ref[0]
    h_ref[...] = h.astype(h_ref.dtype)
    hh, hl = _split(h)
    ne = N_EXPERTS
    wt = wrt_ref[...]
    rt = _nt(wt, hh)
    lt = rt[:ne] + rt[ne:] + _nt(wt[:ne], hl)
    et = jnp.exp(lt - jnp.max(lt, axis=0, keepdims=True))
    at_ref[0] = et / jnp.sum(et, axis=0, keepdims=True)
    wm = wr_ref[...]
    rm = jnp.dot(hh, wm, preferred_element_type=f32)
    lm = rm[:, :ne] + rm[:, ne:] + jnp.dot(hl, wm[:, :ne], preferred_element_type=f32)
    em = jnp.exp(lm - jnp.max(lm, axis=1, keepdims=True))
    am_ref[...] = em / jnp.sum(em, axis=1, keepdims=True)


def router(x2d, g, shift, scale, w_router, n_per_set):
    R = x2d.shape[0]
    tm = min(512, n_per_set)
    tps = n_per_set // tm
    S = R // n_per_set
    G = shift.shape[0]
    gmap = (lambda i: (i // tps, 0, 0)) if G > 1 else (lambda i: (0, 0, 0))
    whi, wlo = _split(w_router.astype(f32))
    wr = jnp.concatenate([whi, wlo], axis=1)
    return pl.pallas_call(
        _router_kernel,
        grid=(R // tm,),
        in_specs=[pl.BlockSpec((tm, D_MODEL), lambda i: (i, 0)),
                  pl.BlockSpec((1, D_MODEL), lambda i: (0, 0)),
                  pl.BlockSpec((1, 1, D_MODEL), gmap),
                  pl.BlockSpec((1, 1, D_MODEL), gmap),
                  pl.BlockSpec((2 * N_EXPERTS, D_MODEL), lambda i: (0, 0)),
                  pl.BlockSpec((D_MODEL, 2 * N_EXPERTS), lambda i: (0, 0))],
        out_specs=[pl.BlockSpec((tm, D_MODEL), lambda i: (i, 0)),
                   pl.BlockSpec((1, N_EXPERTS, tm), lambda i: (i // tps, 0, i % tps)),
                   pl.BlockSpec((tm, N_EXPERTS), lambda i: (i, 0))],
        out_shape=[jax.ShapeDtypeStruct((R, D_MODEL), bf16),
                   jax.ShapeDtypeStruct((S, N_EXPERTS, n_per_set), f32),
                   jax.ShapeDtypeStruct((R, N_EXPERTS), f32)],
        compiler_params=_cparams(("parallel",)),
        name="router",
    )(x2d, g.reshape(1, D_MODEL), shift.reshape(G, 1, D_MODEL), scale.reshape(G, 1, D_MODEL), wr.T, wr)


SEL_BLK = 256
SUB = LANES
SUBW = SUB + 8
UNSEL = -float(2 ** 30)


def _prefix_incl(mask_f, tri, T):
    outs = []
    off = jnp.zeros((mask_f.shape[0], 1), f32)
    for b in range(T // SEL_BLK):
        blk = mask_f[:, b * SEL_BLK:(b + 1) * SEL_BLK].astype(bf16)
        pre = jnp.dot(blk, tri, preferred_element_type=f32) + off
        outs.append(pre)
        off = pre[:, SEL_BLK - 1:SEL_BLK]
    return jnp.concatenate(outs, axis=1)


def _select_kernel(a_ref, tri_ref, cm_ref, posm_ref, cnt_ref, *, T, cap):
    aff = a_ref[0]
    bits = pltpu.bitcast(aff, i32)
    tri = tri_ref[...]

    def bit_step(i, thr):
        cand = thr | (1 << (30 - i))
        cnt = jnp.sum((bits >= cand).astype(f32), axis=1, keepdims=True)
        return jnp.where(cnt >= cap, cand, thr)

    thr = lax.fori_loop(0, 31, bit_step, jnp.zeros((N_EXPERTS, 1), i32))
    gt = bits > thr
    eq = bits == thr
    need = cap - jnp.sum(gt.astype(f32), axis=1, keepdims=True)
    eqf = eq.astype(f32)
    rank_eq = _prefix_incl(eqf, tri, T) - eqf
    sel = gt | (eq & (rank_eq < need))
    self_ = sel.astype(f32)
    pos = _prefix_incl(self_, tri, T) - self_
    posm_ref[0] = jnp.where(sel, pos, UNSEL)
    cnt_ref[0] = jnp.dot(self_.astype(bf16), cm_ref[...], preferred_element_type=f32).astype(i32)


def select_topk(aff, cap):
    S, E, T = aff.shape
    tri = jnp.asarray(np.triu(np.ones((SEL_BLK, SEL_BLK), np.float32)), bf16)
    cm = jnp.asarray((np.arange(T)[:, None] < np.arange(LANES)[None, :] * SUB).astype(np.float32), bf16)
    return pl.pallas_call(
        functools.partial(_select_kernel, T=T, cap=cap),
        grid=(S,),
        in_specs=[pl.BlockSpec((1, E, T), lambda s: (s, 0, 0)),
                  pl.BlockSpec((SEL_BLK, SEL_BLK), lambda s: (0, 0)),
                  pl.BlockSpec((T, LANES), lambda s: (0, 0))],
        out_specs=[pl.BlockSpec((1, E, T), lambda s: (s, 0, 0)),
                   pl.BlockSpec((1, E, LANES), lambda s: (s, 0, 0))],
        out_shape=[jax.ShapeDtypeStruct((S, E, T), f32), jax.ShapeDtypeStruct((S, E, LANES), i32)],
        compiler_params=_cparams(("parallel",)),
        name="select_topk",
    )(aff, tri, cm)


def _align8(v):
    return lax.shift_left(lax.shift_right_logical(v, 3), 3)


def _align16(v):
    return lax.shift_left(lax.shift_right_logical(v, 4), 4)


CMB_ROWS = SUB + 16


def _gather_kernel(cnt_ref, h_ref, pos_ref, o_ref, acc, *, TT, cap, tps, R, srows, EP):
    tl = pl.program_id(1)

    @pl.when(tl == 0)
    def _():
        acc[...] = jnp.zeros_like(acc)

    st = tl // tps
    nsub = TT // SUB
    rid = lax.broadcasted_iota(i32, (srows, SUB), 0).astype(f32)
    for ep in range(EP):
        e = pl.program_id(0) * EP + ep
        cbase = (st * N_EXPERTS + e) * LANES + (tl % tps) * nsub
        for s in range(nsub):
            off8 = _align8(cnt_ref[cbase + s])
            pos = pos_ref[0, ep, :, s * SUB:(s + 1) * SUB]
            onehot = jnp.where(pos == rid + off8.astype(f32), 1.0, 0.0).astype(bf16)
            rows = jnp.dot(onehot, h_ref[s * SUB:(s + 1) * SUB, :], preferred_element_type=f32)
            r0 = pl.multiple_of(st * cap + off8, 8)
            acc[ep, pl.ds(r0, srows), :] += rows

    @pl.when(tl == pl.num_programs(1) - 1)
    def _():
        o_ref[...] = acc[:, 0:R, :].astype(o_ref.dtype)


def _gather_call(cnt, h, posm, cap, TT, srows, EP):
    S, E, T = posm.shape
    tps = T // TT
    R = S * cap
    gs = pltpu.PrefetchScalarGridSpec(
        num_scalar_prefetch=1,
        grid=(E // EP, S * tps),
        in_specs=[pl.BlockSpec((TT, D_MODEL), lambda e, t, c: (t, 0)),
                  pl.BlockSpec((1, EP, 1, TT), lambda e, t, c: (t // tps, e, 0, t % tps))],
        out_specs=pl.BlockSpec((EP, R, D_MODEL), lambda e, t, c: (e, 0, 0)),
        scratch_shapes=[pltpu.VMEM((EP, R + srows, D_MODEL), f32)])
    return pl.pallas_call(
        functools.partial(_gather_kernel, TT=TT, cap=cap, tps=tps, R=R, srows=srows, EP=EP),
        grid_spec=gs,
        out_shape=jax.ShapeDtypeStruct((E, R, D_MODEL), bf16),
        compiler_params=_cparams(("parallel", "arbitrary")),
        name="moe_gather",
    )(cnt.reshape(-1), h, posm.reshape(S, E, 1, T))


FAST_SUB_MAX = 48
FAST_TILE_MAX = 112
FAST_ROWS = 64
CMB_TILE = 512


def _fits_fast(cnt, T):
    nsub = T // SUB
    per_sub = cnt[..., 1:nsub + 1] - cnt[..., :nsub]
    k = CMB_TILE // SUB
    per_tile = cnt[..., k:nsub + 1:k] - cnt[..., 0:nsub:k]
    return (jnp.max(per_sub) <= FAST_SUB_MAX) & (jnp.max(per_tile) <= FAST_TILE_MAX)


def gather_rows(cnt, h, posm, cap, TT, fast_ok=None):
    safe = lambda: _gather_call(cnt, h, posm, cap, TT, SUBW, 1)
    if fast_ok is None:
        return safe()
    return lax.cond(fast_ok, lambda: _gather_call(cnt, h, posm, cap, TT, FAST_ROWS, 2), safe)


EXPERT_TF = 256


def _ffn_kernel(*refs, n):
    xs = refs[:n]
    wg_ref, wu_ref, wd_ref = refs[n:n + 3]
    his = refs[n + 3:2 * n + 3]
    los = refs[2 * n + 3:3 * n + 3]
    accs = refs[3 * n + 3:]
    j = pl.program_id(1)
    wg = wg_ref[0, 0].astype(bf16)
    wu = wu_ref[0, 0].astype(bf16)
    wd = wd_ref[0, 0].astype(bf16)
    @pl.when(j == 0)
    def _():
        for acc in accs:
            acc[...] = jnp.zeros_like(acc)

    for x_ref, hi_ref, lo_ref, acc in zip(xs, his, los, accs):
        x = x_ref[0]
        a = jnp.dot(x, wg, preferred_element_type=f32)
        u = jnp.dot(x, wu, preferred_element_type=f32)
        acc[...] += jnp.dot((_silu(a) * u).astype(bf16), wd, preferred_element_type=f32)

        @pl.when(j == pl.num_programs(1) - 1)
        def _(acc=acc, hi_ref=hi_ref, lo_ref=lo_ref):
            hi, lo = _split(acc[...])
            hi_ref[0] = hi
            lo_ref[0] = lo


def expert_ffn(xgs, layer, w_gate, w_up, w_down):
    E = xgs[0].shape[0]
    nf = D_FF_EXPERT // EXPERT_TF
    n = len(xgs)
    rowspec = lambda a: pl.BlockSpec((1, a.shape[1], D_MODEL), lambda e, j: (e, 0, 0))
    res = pl.pallas_call(
        functools.partial(_ffn_kernel, n=n),
        grid=(E, nf),
        in_specs=[rowspec(a) for a in xgs] + [
            pl.BlockSpec((1, 1, D_MODEL, EXPERT_TF), lambda e, j: (layer, e, 0, j)),
            pl.BlockSpec((1, 1, D_MODEL, EXPERT_TF), lambda e, j: (layer, e, 0, j)),
            pl.BlockSpec((1, 1, EXPERT_TF, D_MODEL), lambda e, j: (layer, e, j, 0))],
        out_specs=[rowspec(a) for a in xgs] * 2,
        out_shape=[jax.ShapeDtypeStruct(a.shape, bf16) for a in xgs] * 2,
        scratch_shapes=[pltpu.VMEM(a.shape[1:], f32) for a in xgs],
        compiler_params=_cparams(("parallel", "arbitrary")),
        name="expert_ffn",
    )(*xgs, w_gate, w_up, w_down)
    return [(res[i], res[n + i]) for i in range(n)]


def _combine_kernel(cnt_ref, x_ref, pos_ref, am_ref, m_ref, *rest, TT, cap, R, W, ytot, crows, EP):
    y_refs, o_ref = rest[:2 * EP], rest[2 * EP]
    st = pl.program_id(0)
    tl = pl.program_id(1)
    eg = pl.program_id(2)

    @pl.when(eg == 0)
    def _():
        o_ref[...] = x_ref[...]

    nsub = TT // SUB
    lane = lax.broadcasted_iota(i32, (TT, N_EXPERTS), 1)
    m5 = m_ref[0]
    rid = lax.broadcasted_iota(i32, (crows, SUB), 0).astype(f32)
    am = am_ref[...]
    gcols, wss = [], []
    for ep in range(EP):
        e = eg * EP + ep
        cbase = (st * N_EXPERTS + e) * LANES + tl * nsub
        wss.append(jnp.minimum(e * R + st * cap + _align16(cnt_ref[cbase]), ytot - W))
        gcols.append(jnp.sum(jnp.where(lane == e, am, 0.0), axis=1, keepdims=True))
    for s in range(nsub):
        sl = slice(s * SUB, (s + 1) * SUB)
        tot = None
        for ep in range(EP):
            e = eg * EP + ep
            rowbase = e * R + st * cap
            off = _align16(cnt_ref[(st * N_EXPERTS + e) * LANES + tl * nsub + s])
            rel = pl.multiple_of(jnp.minimum(rowbase + off - wss[ep], W - crows), 16)
            first = (wss[ep] + rel - rowbase).astype(f32)
            pos = pos_ref[0, ep, :, sl]
            onehot = jnp.where(pos == rid + first, 1.0, 0.0).astype(bf16)
            ywin = jnp.concatenate([y_refs[2 * ep][pl.ds(rel, crows), :], y_refs[2 * ep + 1][pl.ds(rel, crows), :]],
                                   axis=0)
            picked = _tn(jnp.concatenate([onehot, onehot], axis=0), ywin)
            term = gcols[ep][sl] * picked
            tot = term if tot is None else tot + term
        o_ref[sl, :] += m5 * tot


def _combine_call(cnt, x2d, posm, aff_tm, mvec, y_hl, cap, TT, crows, W, EP):
    S, E, T = posm.shape
    tps = T // TT
    R = S * cap
    ytot = E * R
    nsub = TT // SUB
    G = mvec.shape[0]

    def yspec(ep):
        def ymap(st, tl, eg, c):
            e = eg * EP + ep
            off = _align16(c[(st * E + e) * LANES + tl * nsub])
            return (pl.multiple_of(jnp.minimum(e * R + st * cap + off, ytot - W), 16), 0)
        return pl.BlockSpec((pl.Element(W), pl.Element(D_MODEL)), ymap)

    tok = lambda w: pl.BlockSpec((TT, w), lambda st, tl, e, c: (st * tps + tl, 0))
    mmap = (lambda st, tl, e, c: (st, 0, 0)) if G > 1 else (lambda st, tl, e, c: (0, 0, 0))
    gs = pltpu.PrefetchScalarGridSpec(
        num_scalar_prefetch=1,
        grid=(S, tps, E // EP),
        in_specs=[tok(D_MODEL),
                  pl.BlockSpec((1, EP, 1, TT), lambda st, tl, e, c: (st, e, 0, tl)),
                  tok(N_EXPERTS),
                  pl.BlockSpec((1, 1, D_MODEL), mmap)] + [yspec(ep) for ep in range(EP) for _ in range(2)],
        out_specs=tok(D_MODEL))
    yh, yl = y_hl[0].reshape(ytot, D_MODEL), y_hl[1].reshape(ytot, D_MODEL)
    return pl.pallas_call(
        functools.partial(_combine_kernel, TT=TT, cap=cap, R=R, W=W, ytot=ytot, crows=crows, EP=EP),
        grid_spec=gs,
        out_shape=jax.ShapeDtypeStruct(x2d.shape, f32),
        compiler_params=_cparams(("parallel", "parallel", "arbitrary")),
        name="moe_combine",
    )(cnt.reshape(-1), x2d, posm.reshape(S, E, 1, T), aff_tm, mvec.reshape(G, 1, D_MODEL), *([yh, yl] * EP))


def combine(cnt, x2d, posm, aff_tm, mvec, y_hl, cap, TT, fast_ok=None):
    safe = lambda: _combine_call(cnt, x2d, posm, aff_tm, mvec, y_hl, cap, TT, CMB_ROWS, TT + 32, 1)
    if fast_ok is None:
        return safe()
    wfast = FAST_TILE_MAX + 16 + FAST_ROWS
    return lax.cond(fast_ok, lambda: _combine_call(cnt, x2d, posm, aff_tm, mvec, y_hl, cap, TT, FAST_ROWS, wfast, 4),
                    safe)


def _rope_tables(n):
    half = NA_HD // 2
    q = half // 2
    inv = ROPE_THETA ** (-np.arange(q, dtype=np.float64) / q)
    pos = np.arange(n)
    ang_r = (pos // GRID_W)[:, None] * inv
    ang_c = (pos % GRID_W)[:, None] * inv
    zero = np.zeros_like(ang_r)
    c = np.concatenate([np.cos(ang_r)] * 2 + [np.cos(ang_c)] * 2, axis=1)
    s1 = np.concatenate([-np.sin(ang_r), zero, -np.sin(ang_c), zero], axis=1)
    s2 = np.concatenate([zero, np.sin(ang_r), zero, np.sin(ang_c)], axis=1)
    two = lambda a: jnp.asarray(np.concatenate([a, a], axis=1).astype(np.float32))
    return two(c), two(s1), two(s2)


def _mixing(hx, hc, need_ctx, B, N, NC, la, lc, w_in, q_gain, k_gain, table, rope, bd, conv_w, conv_b,
            spec, skip, filt_c, wa, wb, wc, wo, x2d, c2d, mx2, mc2):
    tile8 = lambda v: jnp.tile(v.reshape(1, NA_HD), (1, NA_HEADS))
    qg, kg = tile8(q_gain), tile8(k_gain)
    norm_aux = [("col", kg), ("const", bd)]
    rope_aux = [("row", rope[0]), ("row", rope[1]), ("row", rope[2])]
    lf_aux = lambda d: [("col", la[d:d + 1]), ("col", lc[d:d + 1])]
    tc = hc.shape[0]

    lff_c = project(hc, w_in, OFF_FF, 512, _epi_logforget, lf_aux(0), (f32,), tm=tc).reshape(B, NC, 512)
    lfb_c = project(hc, w_in, OFF_FB, 512, _epi_logforget, lf_aux(1), (f32,), tm=tc).reshape(B, NC, 512)
    i_c = project(hc, w_in, OFF_I, 512, _epi_raw, tm=tc).reshape(B, NC, 512)
    k_c = project(hc, w_in, OFF_NK, 512, _epi_norm, norm_aux, tm=tc).reshape(B, NC, 512)
    v_c = project(hc, w_in, OFF_NV, 512, _epi_raw, tm=tc).reshape(B, NC, 512)
    if need_ctx:
        q_c = project(hc, w_in, OFF_HQ, 512, _epi_silu, tm=tc).reshape(B, NC, 512)
    else:
        q_c = jnp.zeros((B, NC, 512), bf16)
    s0 = jnp.zeros((B, HG_HEADS, LANES, LANES), f32)
    oc_f, oc_b, s_f, s_b = hgrn_bidir(lff_c, lfb_c, i_c, q_c, s0, s0)

    lff_x = project(hx, w_in, OFF_FF, 512, _epi_logforget, lf_aux(0), (f32,)).reshape(B, N, 512)
    lfb_x = project(hx, w_in, OFF_FB, 512, _epi_logforget, lf_aux(1), (f32,)).reshape(B, N, 512)
    i_x = project(hx, w_in, OFF_I, 512, _epi_raw).reshape(B, N, 512)
    q_x = project(hx, w_in, OFF_HQ, 512, _epi_silu).reshape(B, N, 512)
    g_x = project(hx, w_in, OFF_HG, 512, _epi_silu).reshape(B, N, 512)
    k_x = project(hx, w_in, OFF_NK, 512, _epi_norm_rope, norm_aux + rope_aux, rows_per_seq=N).reshape(B, N, 512)
    v_x = project(hx, w_in, OFF_NV, 512, _epi_raw).reshape(B, N, 512)
    qn_x, qr_x = project(hx, w_in, OFF_NQ, 512, _epi_norm_both, [("col", qg), ("const", bd)] + rope_aux,
                         (bf16, bf16), rows_per_seq=N)
    p_x = project(hx, w_in, OFF_HY, 3 * HY_W, _epi_raw, out_dtypes=(f32,)).reshape(B, N, 3 * HY_W)
    gates_x = project(hx, w_in, OFF_GATE, 3 * D_MODEL, _epi_sigmoid)

    ox_f, ox_b, _, _ = hgrn_bidir(lff_x, lfb_x, i_x, q_x, s_f, s_b)

    nb_x = neighbourhood_attention(qr_x.reshape(B, N, 512), qn_x.reshape(B, N, 512), k_x, v_x, k_c, v_c, table)

    z_x, x0_x = hyena_pre(p_x, conv_w, conv_b)
    hy_x = hyena_conv(z_x, x0_x, spec[0], spec[1], skip, spec[2])

    flat = lambda a: a.reshape(-1, a.shape[-1])
    x_new = merge(flat(ox_f), flat(ox_b), flat(g_x), flat(nb_x), flat(hy_x), gates_x, wa, wb, wc, wo, x2d, mx2, N)
    if not need_ctx:
        return x_new, None

    qn_c = project(hc, w_in, OFF_NQ, 512, _epi_norm, [("col", qg), ("const", bd)], tm=tc).reshape(B, NC, 512)
    nb_c = context_attention(qn_c, k_c, v_c)
    p_c = project(hc, w_in, OFF_HY, 3 * HY_W, _epi_raw, out_dtypes=(f32,), tm=tc).reshape(B, NC, 3 * HY_W)
    hy_c = hyena_ctx(p_c, conv_w, conv_b, filt_c[0], filt_c[1], skip)
    gates_c = project(hc, w_in, OFF_GATE, 3 * D_MODEL, _epi_sigmoid, tm=tc)
    g_c = project(hc, w_in, OFF_HG, 512, _epi_silu, tm=tc)
    c_new = merge(flat(oc_f), flat(oc_b), g_c, flat(nb_c), flat(hy_c), gates_c, wa, wb, wc, wo, c2d, mc2, B * NC)
    return x_new, c_new


def kernel(x, c, ctx, c_ctx, w_mod, b_mod, norm_mix, norm_ffn, w_in, hg_lb, na_q_gain, na_k_gain, na_rpb,
           hy_conv_w, hy_conv_b, hy_pe_w1, hy_pe_b1, hy_pe_freq1, hy_pe_w2, hy_pe_b2, hy_pe_freq2, hy_pe_w3,
           hy_skip, w_branch_a, w_branch_b, w_branch_c, w_out, w_router, w_e_gate, w_e_up, w_e_down):
    B, N, D = x.shape
    NC = ctx.shape[1]
    E = N_EXPERTS
    cap_x = EC_CAP_FACTOR * N // E
    cap_c = EC_CAP_FACTOR * NC // E

    lb = jnp.cumsum(jax.nn.softmax(hg_lb.astype(f32), axis=0), axis=0)
    lb = lb - lb[:1]
    la_all, lc_all = jnp.log(lb), jnp.log1p(-lb)

    s8 = jnp.zeros((8, D), f32).at[:B].set(c).at[B].set(c_ctx)
    rope = _rope_tables(N)
    bd = jnp.asarray(np.kron(np.eye(NA_HEADS), np.full((NA_HD, NA_HD), 1.0 / NA_HD)).astype(np.float32), bf16)
    dcx = _dft_consts(N)

    x2d = x.reshape(B * N, D)
    c2d = ctx.reshape(B * NC, D)
    for l in range(DEPTH):
        need_ctx = l < DEPTH - 1
        mv = modvec(s8, w_mod[l], b_mod[l])
        mx = [mv[:B, k * D:(k + 1) * D] for k in range(6)]
        mc = [mv[B:B + 1, k * D:(k + 1) * D] for k in range(6)]
        w_in_l = w_in[l].astype(bf16)
        hx = modulate(x2d, norm_mix[l], mx[0], mx[1], N, bf16)
        hc = modulate(c2d, norm_mix[l], mc[0], mc[1], B * NC, bf16)

        filt = (hy_pe_w1[l], hy_pe_b1[l], hy_pe_freq1[l], hy_pe_w2[l], hy_pe_b2[l], hy_pe_freq2[l], hy_pe_w3[l])
        k_x, nrm_x = hyena_filter(N, *filt)
        sre, sim = hyena_spectrum(k_x, 1.0 / nrm_x, dcx)
        skip = hy_skip[l].reshape(1, HY_W)
        filt_c = None
        if need_ctx:
            h_c, nrm_c = hyena_filter(NC, *filt)
            filt_c = (h_c, 1.0 / nrm_c)

        x2d, c_new = _mixing(
            hx, hc, need_ctx, B, N, NC, la_all[l], lc_all[l], w_in_l, na_q_gain[l], na_k_gain[l],
            _na_bias_table(na_rpb[l]), rope, bd, hy_conv_w[l], hy_conv_b[l], (sre, sim, dcx), skip,
            filt_c, w_branch_a[l].astype(bf16), w_branch_b[l].astype(bf16), w_branch_c[l].astype(bf16),
            w_out[l].astype(bf16), x2d, c2d, mx[2], mc[2])

        h2, aff_t, aff_m = router(x2d, norm_ffn[l], mx[3], mx[4], w_router[l], N)
        posm, cnt = select_topk(aff_t, cap_x)
        fast_ok = _fits_fast(cnt, N)
        xgs = [gather_rows(cnt, h2, posm, cap_x, 1024, fast_ok)]
        if need_ctx:
            c2d = c_new
            hc2, aff_tc, aff_mc = router(c2d, norm_ffn[l], mc[3], mc[4], w_router[l], NC)
            posm_c, cnt_c = select_topk(aff_tc, cap_c)
            xgs.append(gather_rows(cnt_c, hc2, posm_c, cap_c, NC))
        ys = expert_ffn(xgs, l, w_e_gate, w_e_up, w_e_down)
        x2d = combine(cnt, x2d, posm, aff_m, mx[5], ys[0], cap_x, CMB_TILE, fast_ok)
        if need_ctx:
            c2d = combine(cnt_c, c2d, posm_c, aff_mc, mc[5], ys[1], cap_c, NC)
    return x2d.reshape(B, N, D)
```

```python
import functools
import math

import numpy as np
import jax
import jax.numpy as jnp
from jax import lax
from jax.experimental import pallas as pl
from jax.experimental.pallas import tpu as pltpu

f32 = jnp.float32
bf16 = jnp.bfloat16
i32 = jnp.int32
HI = lax.Precision.HIGHEST

D_MODEL = 1024
DEPTH = 2
GRID_W = 64
EPS = 1e-6
HG_HEADS = 4
HG_W = 512
HG_CHUNK = 64
NA_HEADS = 8
NA_HD = 64
NA_W = 512
NA_WIN_R = 8
NA_WIN_C = 16
ROPE_THETA = 10000.0
HY_W = 512
HY_BANDS = 16
HY_PE_DIM = 1 + 2 * HY_BANDS
HY_FILT_HID = 64
HY_FAST_DECAY = 0.3
HY_SLOW_DECAY = 1.5
HY_TARGET = 1e-2
OFF_FF = 0
OFF_FB = 512
OFF_I = 1024
OFF_NK = 1536
OFF_NV = 2048
OFF_HQ = 2560
OFF_NQ = 3072
OFF_HG = 3584
OFF_HY = 4096
OFF_GATE = 5632
IN_COLS = 8704
N_EXPERTS = 16
EC_CAP_FACTOR = 2
D_FF_EXPERT = 2816

LANES = 128
NEG_BIG = -1e30
VMEM_LIMIT = 56 * 1024 * 1024


def _cparams(sem, vmem=VMEM_LIMIT):
    return pltpu.CompilerParams(dimension_semantics=sem, vmem_limit_bytes=vmem)


def _nt(a, b, precision=None):
    return lax.dot_general(a, b, (((1,), (1,)), ((), ())), precision=precision, preferred_element_type=f32)


def _tn(a, b, precision=None):
    return lax.dot_general(a, b, (((0,), (0,)), ((), ())), precision=precision, preferred_element_type=f32)


def _silu(x):
    return x * jax.nn.sigmoid(x)


def _split(x):
    hi = x.astype(bf16)
    return hi, (x - hi.astype(f32)).astype(bf16)


def _modvec_kernel(s_ref, w_ref, b_ref, o_ref):
    s = _silu(s_ref[...])
    o_ref[...] = jnp.dot(s, w_ref[...], precision=HI, preferred_element_type=f32) + b_ref[...]


def modvec(s8, w, b):
    n = w.shape[1]
    tn = 1024
    return pl.pallas_call(
        _modvec_kernel,
        grid=(n // tn,),
        in_specs=[pl.BlockSpec((8, D_MODEL), lambda j: (0, 0)),
                  pl.BlockSpec((D_MODEL, tn), lambda j: (0, j)),
                  pl.BlockSpec((1, tn), lambda j: (0, j))],
        out_specs=pl.BlockSpec((8, tn), lambda j: (0, j)),
        out_shape=jax.ShapeDtypeStruct((8, n), f32),
        compiler_params=_cparams(("parallel",)),
        name="modvec",
    )(s8, w, b.reshape(1, n))


def _modulate_kernel(x_ref, g_ref, sh_ref, sc_ref, o_ref):
    x = x_ref[...]
    ms = jnp.mean(x * x, axis=-1, keepdims=True)
    y = x * lax.rsqrt(ms + EPS)
    o_ref[...] = (y * g_ref[...] * (1.0 + sc_ref[0]) + sh_ref[0]).astype(o_ref.dtype)


def modulate(x2d, g, shift, scale, rows_per_group, out_dtype):
    R = x2d.shape[0]
    tm = 512
    tpg = rows_per_group // tm
    G = shift.shape[0]
    return pl.pallas_call(
        _modulate_kernel,
        grid=(R // tm,),
        in_specs=[pl.BlockSpec((tm, D_MODEL), lambda i: (i, 0)),
                  pl.BlockSpec((1, D_MODEL), lambda i: (0, 0)),
                  pl.BlockSpec((1, 1, D_MODEL), lambda i: (i // tpg, 0, 0)),
                  pl.BlockSpec((1, 1, D_MODEL), lambda i: (i // tpg, 0, 0))],
        out_specs=pl.BlockSpec((tm, D_MODEL), lambda i: (i, 0)),
        out_shape=jax.ShapeDtypeStruct((R, D_MODEL), out_dtype),
        compiler_params=_cparams(("parallel",)),
        name="modulate",
    )(x2d, g.reshape(1, D_MODEL), shift.reshape(G, 1, D_MODEL), scale.reshape(G, 1, D_MODEL))


def _log1p_exp_neg(a):
    return jnp.log(1.0 + jnp.exp(-a))


def _log_sigmoid(z):
    return jnp.minimum(z, 0.0) - _log1p_exp_neg(jnp.abs(z))


def _epi_raw(acc, o_ref):
    o_ref[...] = acc.astype(o_ref.dtype)


def _epi_silu(acc, o_ref):
    o_ref[...] = _silu(acc).astype(o_ref.dtype)


def _epi_sigmoid(acc, o_ref):
    o_ref[...] = jax.nn.sigmoid(acc).astype(o_ref.dtype)


def _epi_logforget(acc, la_ref, lc_ref, o_ref):
    la = la_ref[...]
    c = lc_ref[...] + _log_sigmoid(acc)
    o_ref[...] = jnp.maximum(la, c) + _log1p_exp_neg(jnp.abs(la - c))


def _head_rms(acc, gain_ref, bd_ref):
    hi, lo = _split(acc * acc)
    ms = jnp.dot(hi, bd_ref[...], preferred_element_type=f32) + jnp.dot(lo, bd_ref[...], preferred_element_type=f32)
    return acc * lax.rsqrt(ms + EPS) * gain_ref[...]


def _rope(y, c_ref, s1_ref, s2_ref):
    reps = y.shape[1] // LANES
    c = jnp.concatenate([c_ref[...]] * reps, axis=1)
    s1 = jnp.concatenate([s1_ref[...]] * reps, axis=1)
    s2 = jnp.concatenate([s2_ref[...]] * reps, axis=1)
    w = y.shape[1]
    return y * c + pltpu.roll(y, w - 16, axis=1) * s1 + pltpu.roll(y, 16, axis=1) * s2


def _epi_norm(acc, gain_ref, bd_ref, o_ref):
    o_ref[...] = _head_rms(acc, gain_ref, bd_ref).astype(o_ref.dtype)


def _epi_norm_rope(acc, gain_ref, bd_ref, c_ref, s1_ref, s2_ref, o_ref):
    y = _head_rms(acc, gain_ref, bd_ref)
    o_ref[...] = _rope(y, c_ref, s1_ref, s2_ref).astype(o_ref.dtype)


def _epi_norm_both(acc, gain_ref, bd_ref, c_ref, s1_ref, s2_ref, on_ref, or_ref):
    y = _head_rms(acc, gain_ref, bd_ref)
    on_ref[...] = y.astype(on_ref.dtype)
    or_ref[...] = _rope(y, c_ref, s1_ref, s2_ref).astype(or_ref.dtype)


def _proj_kernel(h_ref, w_ref, *rest, epi):
    acc = jnp.dot(h_ref[...], w_ref[...], preferred_element_type=f32)
    epi(acc, *rest)


def project(h, w, c0, width, epi, aux=(), out_dtypes=(bf16,), tm=2048, rows_per_seq=None):
    R = h.shape[0]
    tn = 512
    nj = width // tn
    cb = c0 // tn
    in_specs = [pl.BlockSpec((tm, D_MODEL), lambda i, j: (i, 0)),
                pl.BlockSpec((D_MODEL, tn), lambda i, j: (0, cb + j))]
    args = [h, w]
    for kind, arr in aux:
        if kind == "col":
            in_specs.append(pl.BlockSpec((1, tn), lambda i, j: (0, j)))
        elif kind == "const":
            in_specs.append(pl.BlockSpec(arr.shape, lambda i, j: (0, 0)))
        else:
            tps = rows_per_seq // tm
            in_specs.append(pl.BlockSpec((tm, LANES), lambda i, j: (i % tps, 0)))
        args.append(arr)
    out_specs = [pl.BlockSpec((tm, tn), lambda i, j: (i, j)) for _ in out_dtypes]
    out_shape = [jax.ShapeDtypeStruct((R, width), dt) for dt in out_dtypes]
    res = pl.pallas_call(
        functools.partial(_proj_kernel, epi=epi),
        grid=(R // tm, nj),
        in_specs=in_specs,
        out_specs=out_specs,
        out_shape=out_shape,
        compiler_params=_cparams(("parallel", "parallel")),
        name="proj_" + epi.__name__[5:],
    )(*args)
    return res[0] if len(res) == 1 else res


def _hgrn_tmatrix(C, reverse):
    L = int(round(math.log2(C)))
    t = np.arange(C)
    tau = (C - 1 - t) if reverse else t
    tt, uu = tau[:, None], tau[None, :]
    T = np.zeros((2 + L, C, C), np.float32)
    T[0] = uu <= tt
    T[1] = uu > tt
    for l in range(L):
        same = (tt >> (l + 1)) == (uu >> (l + 1))
        tr = ((tt >> l) & 1) == 1
        ur = ((uu >> l) & 1) == 1
        T[2 + l] = same & ((tr & ur & (uu <= tt)) | (~tr & ~ur & (uu > tt)))
    return T.reshape((2 + L) * C, C)


def _hgrn_step(chains, C):
    L = int(round(math.log2(C)))
    row = lax.broadcasted_iota(i32, (C, LANES), 0)
    ti0 = lax.broadcasted_iota(i32, (C, C), 0)
    si0 = lax.broadcasted_iota(i32, (C, C), 1)

    es, kcs = [], []
    for lf, _, _, tm, _, _ in chains:
        W = lf.shape[1]
        hi, lo = _split(lf)
        r = jnp.dot(tm, jnp.concatenate([hi, lo], axis=1), preferred_element_type=f32)
        es.append(r[:, :W] + r[:, W:])
        kcs.append(1.0 - jnp.exp(lf))

    items = []
    for (lf, v_all, q_all, _, s_view, reverse), E, kc_all in zip(chains, es, kcs):
        tau = (C - 1 - row) if reverse else row
        ti, si = (C - 1 - ti0, C - 1 - si0) if reverse else (ti0, si0)
        end_row = 0 if reverse else C - 1
        for h in range(HG_HEADS):
            sl = slice(LANES * h, LANES * (h + 1))
            qh, kch, vh = q_all[:, sl], kc_all[:, sl], v_all[:, sl]
            b = E[0:C, sl]
            st = s_view[h]
            pairs = [(qh.astype(bf16), kch.astype(bf16), ti == si)]
            for l in range(L):
                x = jnp.exp(E[(2 + l) * C:(3 + l) * C, sl])
                later = ((tau >> l) & 1) == 1
                pairs.append((jnp.where(later, qh * x, 0.0).astype(bf16), jnp.where(later, 0.0, kch * x).astype(bf16),
                              (ti >> (l + 1)) == (si >> (l + 1))))
            items.append(dict(qb=(qh * jnp.exp(b)).astype(bf16), st=st, pairs=pairs, vh=vh,
                              kd=(kch * jnp.exp(E[C:2 * C, sl])).astype(bf16),
                              decay=jnp.exp(b[end_row:end_row + 1, :]), view=s_view, h=h))

    for it in items:
        it["o"] = _nt(it["qb"], it["st"].astype(bf16))
        it["att"] = [(_nt(ql, kl), m) for ql, kl, m in it["pairs"]]

    outs = []
    for it in items:
        att = None
        for a, m in it["att"]:
            t = jnp.where(m, a, 0.0)
            att = t if att is None else att + t
        it["o"] = it["o"] + jnp.dot(att.astype(bf16), it["vh"], preferred_element_type=f32)
        it["view"][it["h"]] = it["st"] * it["decay"] + _tn(it["vh"], it["kd"])
    nh = HG_HEADS
    for c in range(len(chains)):
        outs.append(jnp.concatenate([items[c * nh + h]["o"] for h in range(nh)], axis=1))
    return outs


def _hgrn_kernel(lff_ref, lfb_ref, vf_ref, vb_ref, qf_ref, qb_ref, s0f_ref, s0b_ref, tf_ref, tb_ref,
                 of_ref, ob_ref, sff_ref, sfb_ref, s_scr, *, C, B):
    c = pl.program_id(0)

    @pl.when(c == 0)
    def _():
        s_scr[0] = s0f_ref[...]
        s_scr[1] = s0b_ref[...]

    chains = []
    for b in range(B):
        chains.append((lff_ref[b], vf_ref[b], qf_ref[b].astype(f32), tf_ref[...], s_scr.at[0, b], False))
        chains.append((lfb_ref[b], vb_ref[b], qb_ref[b].astype(f32), tb_ref[...], s_scr.at[1, b], True))
    outs = _hgrn_step(chains, C)
    for b in range(B):
        of_ref[b] = outs[2 * b]
        ob_ref[b] = outs[2 * b + 1]

    @pl.when(c == pl.num_programs(0) - 1)
    def _():
        sff_ref[...] = s_scr[0]
        sfb_ref[...] = s_scr[1]


def hgrn_bidir(lf_f, lf_b, v, q, s0_f, s0_b):
    B, N, W = lf_f.shape
    C = HG_CHUNK
    nch = N // C
    tf = jnp.asarray(_hgrn_tmatrix(C, False), bf16)
    tb = jnp.asarray(_hgrn_tmatrix(C, True), bf16)
    fw = pl.BlockSpec((B, C, W), lambda c: (0, c, 0))
    bw = pl.BlockSpec((B, C, W), lambda c: (0, nch - 1 - c, 0))
    st = pl.BlockSpec((B, HG_HEADS, LANES, LANES), lambda c: (0, 0, 0, 0))
    tsp = pl.BlockSpec(tf.shape, lambda c: (0, 0))
    seq = jax.ShapeDtypeStruct((B, N, W), f32)
    sts = jax.ShapeDtypeStruct((B, HG_HEADS, LANES, LANES), f32)
    return pl.pallas_call(
        functools.partial(_hgrn_kernel, C=C, B=B),
        grid=(nch,),
        in_specs=[fw, bw, fw, bw, fw, bw, st, st, tsp, tsp],
        out_specs=[fw, bw, st, st],
        out_shape=[seq, seq, sts, sts],
        scratch_shapes=[pltpu.VMEM((2, B, HG_HEADS, LANES, LANES), f32)],
        compiler_params=_cparams(("arbitrary",)),
        name="hgrn",
    )(lf_f, lf_b, v, v, q, q, s0_f, s0_b, tf, tb)


def _na_kernel(qr_ref, qn_ref, k_ref, v_ref, kc_ref, vc_ref, tab_ref, o_ref, s_scr, p_scr, *, rows_per_step, n_rows):
    g = pl.program_id(2)
    scale = NA_HD ** -0.5
    lane = lax.broadcasted_iota(i32, (GRID_W, LANES), 1)
    kcx = kc_ref[0]
    vcx = vc_ref[0]
    win = NA_WIN_R * GRID_W
    ctx_len = kcx.shape[0]

    starts = []
    for i in range(rows_per_step):
        r = g * rows_per_step + i
        rs = jnp.clip(r - NA_WIN_R // 2, 0, n_rows - NA_WIN_R)
        off = rs - r + (NA_WIN_R - 1)
        start = pl.multiple_of(rs * GRID_W, GRID_W)
        starts.append(start)
        kw = k_ref[0, pl.ds(start, win), :]
        qr = qr_ref[0, i * GRID_W:(i + 1) * GRID_W, :]
        qn = qn_ref[0, i * GRID_W:(i + 1) * GRID_W, :]
        zq = jnp.zeros_like(qr)
        qrs = jnp.concatenate([jnp.where(lane < NA_HD, qr, zq), jnp.where(lane < NA_HD, zq, qr)], axis=0)
        qns = jnp.concatenate([jnp.where(lane < NA_HD, qn, zq), jnp.where(lane < NA_HD, zq, qn)], axis=0)
        bias = jnp.concatenate([tab_ref[0, off], tab_ref[1, off]], axis=0)
        row0 = 2 * i * GRID_W
        s_scr[row0:row0 + 2 * GRID_W, 0:win] = _nt(qrs, kw) * scale + bias
        s_scr[row0:row0 + 2 * GRID_W, win:win + ctx_len] = _nt(qns, kcx) * scale

    def softmax_rows(c, carry):
        r0 = pl.multiple_of(c * LANES, LANES)
        s = s_scr[pl.ds(r0, LANES), :]
        p = jnp.exp(s - jnp.max(s, axis=-1, keepdims=True))
        inv = 1.0 / jnp.sum(p, axis=-1, keepdims=True)
        p_scr[pl.ds(r0, LANES), :] = (p * inv).astype(bf16)
        return carry

    lax.fori_loop(0, 2 * rows_per_step * GRID_W // LANES, softmax_rows, 0, unroll=2)

    for i in range(rows_per_step):
        vw = v_ref[0, pl.ds(starts[i], win), :]
        row0 = 2 * i * GRID_W
        p = p_scr[row0:row0 + 2 * GRID_W, :]
        res = (jnp.dot(p[:, :win], vw, preferred_element_type=f32)
               + jnp.dot(p[:, win:], vcx, preferred_element_type=f32))
        o_ref[0, i * GRID_W:(i + 1) * GRID_W, :] = jnp.where(lane < NA_HD, res[:GRID_W], res[GRID_W:]).astype(o_ref.dtype)


def _na_bias_table(rpb):
    col = jnp.arange(GRID_W)
    cs = jnp.clip(col - NA_WIN_C // 2, 0, GRID_W - NA_WIN_C)
    kc = jnp.arange(GRID_W)
    valid = (kc[None, :] >= cs[:, None]) & (kc[None, :] < cs[:, None] + NA_WIN_C)
    dc = jnp.clip(kc[None, :] - col[:, None] + (NA_WIN_C - 1), 0, 2 * NA_WIN_C - 2)
    bc = jnp.where(valid[None, None], rpb[:, :, dc], NEG_BIG)
    t2 = jnp.stack([bc[:, o:o + NA_WIN_R] for o in range(NA_WIN_R)], axis=1)
    t2 = t2.transpose(0, 1, 3, 2, 4)
    return t2.reshape(NA_HEADS, NA_WIN_R, GRID_W, NA_WIN_R * GRID_W).astype(f32)


def neighbourhood_attention(q_rot, qn, k_rot, v, kc, vc, table):
    B, N, W = q_rot.shape
    n_rows = N // GRID_W
    rps = 8
    ctx_len = kc.shape[1]
    pairs = W // LANES
    keys = NA_WIN_R * GRID_W + ctx_len
    return pl.pallas_call(
        functools.partial(_na_kernel, rows_per_step=rps, n_rows=n_rows),
        grid=(B, pairs, n_rows // rps),
        in_specs=[pl.BlockSpec((1, rps * GRID_W, LANES), lambda b, p, g: (b, g, p)),
                  pl.BlockSpec((1, rps * GRID_W, LANES), lambda b, p, g: (b, g, p)),
                  pl.BlockSpec((1, N, LANES), lambda b, p, g: (b, 0, p)),
                  pl.BlockSpec((1, N, LANES), lambda b, p, g: (b, 0, p)),
                  pl.BlockSpec((1, ctx_len, LANES), lambda b, p, g: (b, 0, p)),
                  pl.BlockSpec((1, ctx_len, LANES), lambda b, p, g: (b, 0, p)),
                  pl.BlockSpec((2, NA_WIN_R, GRID_W, NA_WIN_R * GRID_W), lambda b, p, g: (p, 0, 0, 0))],
        out_specs=pl.BlockSpec((1, rps * GRID_W, LANES), lambda b, p, g: (b, g, p)),
        out_shape=jax.ShapeDtypeStruct((B, N, W), bf16),
        scratch_shapes=[pltpu.VMEM((2 * rps * GRID_W, keys), f32), pltpu.VMEM((2 * rps * GRID_W, keys), bf16)],
        compiler_params=_cparams(("parallel", "parallel", "arbitrary")),
        name="natten",
    )(q_rot, qn, k_rot, v, kc, vc, table)


def _ctx_attn_kernel(q_ref, k_ref, v_ref, o_ref):
    scale = NA_HD ** -0.5
    q = q_ref[0]
    k = k_ref[0]
    v = v_ref[0]
    lane = lax.broadcasted_iota(i32, q.shape, 1)
    res = []
    for hh in range(2):
        m = (lane >= NA_HD * hh) & (lane < NA_HD * (hh + 1))
        s = _nt(jnp.where(m, q, jnp.zeros_like(q)), k) * scale
        p = jnp.exp(s - jnp.max(s, axis=-1, keepdims=True))
        p = p / jnp.sum(p, axis=-1, keepdims=True)
        res.append(jnp.dot(p.astype(bf16), v, preferred_element_type=f32))
    o_ref[0] = jnp.where(lane < NA_HD, res[0], res[1]).astype(o_ref.dtype)


def context_attention(q, k, v):
    B, N, W = q.shape
    spec = pl.BlockSpec((1, N, LANES), lambda b, p: (b, 0, p))
    return pl.pallas_call(
        _ctx_attn_kernel,
        grid=(B, W // LANES),
        in_specs=[spec, spec, spec],
        out_specs=spec,
        out_shape=jax.ShapeDtypeStruct((B, N, W), bf16),
        compiler_params=_cparams(("parallel", "parallel")),
        name="ctx_attn",
    )(q, k, v)


def _filter_kernel(z_ref, w1_ref, b1_ref, f1_ref, w2_ref, b2_ref, f2_ref, w3_ref, dl_ref, k_ref, nrm_ref, *, tm, n):
    i = pl.program_id(0)
    z = z_ref[...]
    hid = HY_FILT_HID
    dot = lambda a, b: jnp.dot(a, b, precision=HI, preferred_element_type=f32)
    w1 = w1_ref[...]
    a = jnp.sin(f1_ref[...] * (dot(z[:tm // 2], w1[:, :LANES]) + dot(z[tm // 2:], w1[:, LANES:]) + b1_ref[...]))
    a = jnp.sin(f2_ref[...] * (dot(a, w2_ref[...]) + b2_ref[...]))
    w3 = w3_ref[...]
    zero = jnp.zeros_like(w3)
    h = jnp.concatenate([dot(a, jnp.concatenate([w3, zero], axis=0)), dot(a, jnp.concatenate([zero, w3], axis=0))],
                        axis=0)
    h = h * jnp.exp(-z[:, 0:1] * dl_ref[...])
    row = lax.broadcasted_iota(i32, (tm, HY_W), 0) + i * tm
    k = jnp.where(row == n, 0.0, h)
    k_ref[...] = k
    part = jnp.sum(jnp.abs(k), axis=0, keepdims=True)

    @pl.when(i == 0)
    def _():
        nrm_ref[...] = part

    @pl.when(i > 0)
    def _():
        nrm_ref[...] = nrm_ref[...] + part


def hyena_filter(n, w1, b1, fr1, w2, b2, fr2, w3):
    t = np.linspace(0.0, 1.0, n)[:, None]
    w = 2 * math.pi * np.arange(n)[:, None] / n
    fb = np.linspace(1e-4, HY_BANDS - 1, HY_BANDS)[None]
    z = np.concatenate([t, np.cos(fb * w), -np.sin(fb * w)], axis=-1)
    z = np.concatenate([z, np.zeros((1, HY_PE_DIM)), z[:0:-1]], axis=0)
    z = jnp.asarray(np.pad(z, ((0, 0), (0, LANES - HY_PE_DIM))).astype(np.float32))
    w1p = jnp.pad(w1.astype(f32), ((0, LANES - HY_PE_DIM), (0, 0)))
    deltas = jnp.asarray(np.abs(np.linspace(math.log(HY_TARGET) / HY_SLOW_DECAY, math.log(HY_TARGET) / HY_FAST_DECAY,
                                            2 * HY_W))[None].astype(np.float32))
    tm = min(n, 512)
    hid = HY_FILT_HID
    full = lambda shape: pl.BlockSpec(shape, lambda i: (0, 0))
    tph = n // tm
    zpad = jnp.zeros((LANES, hid), f32)
    w1pk = jnp.concatenate([w1p, zpad, zpad, w1p], axis=1)
    w2f = w2.astype(f32)
    z2 = jnp.zeros((hid, hid), f32)
    w2pk = jnp.concatenate([jnp.concatenate([w2f, z2], axis=1), jnp.concatenate([z2, w2f], axis=1)], axis=0)
    twice = lambda v: jnp.tile(v.reshape(1, hid).astype(f32), (1, 2))
    return pl.pallas_call(
        functools.partial(_filter_kernel, tm=tm, n=n),
        grid=(2 * n // tm,),
        in_specs=[pl.BlockSpec((tm, LANES), lambda i: (i, 0)),
                  full((LANES, 2 * LANES)), full((1, LANES)), full((1, LANES)),
                  full((LANES, LANES)), full((1, LANES)), full((1, LANES)),
                  pl.BlockSpec((hid, HY_W), lambda i: (0, i // tph)),
                  pl.BlockSpec((1, HY_W), lambda i: (0, i // tph))],
        out_specs=[pl.BlockSpec((tm, HY_W), lambda i: (i, 0)), pl.BlockSpec((1, HY_W), lambda i: (0, 0))],
        out_shape=[jax.ShapeDtypeStruct((2 * n, HY_W), f32), jax.ShapeDtypeStruct((1, HY_W), f32)],
        compiler_params=_cparams(("arbitrary",)),
        name="hyena_filter",
    )(z, w1pk, twice(b1), twice(fr1), w2pk, twice(b2), twice(fr2), w3.astype(f32), deltas)


def _conv3(u, w_ref, b_ref):
    n = u.shape[0]
    row = lax.broadcasted_iota(i32, u.shape, 0)
    prev = jnp.where(row == 0, 0.0, pltpu.roll(u, 1, axis=0))
    nxt = jnp.where(row == n - 1, 0.0, pltpu.roll(u, n - 1, axis=0))
    return prev * w_ref[0:1, :] + u * w_ref[1:2, :] + nxt * w_ref[2:3, :] + b_ref[...]


def _hyena_pre_kernel(p0_ref, p1_ref, p2_ref, w0_ref, w1_ref, w2_ref, b0_ref, b1_ref, b2_ref, z_ref, x0_ref):
    x0_ref[0] = _conv3(p0_ref[0], w0_ref, b0_ref).astype(x0_ref.dtype)
    z_ref[0] = _conv3(p1_ref[0], w1_ref, b1_ref) * _conv3(p2_ref[0], w2_ref, b2_ref)


def hyena_pre(p, conv_w, conv_b):
    B, N, _ = p.shape
    nb = HY_W // LANES
    conv_b = conv_b.reshape(1, 3 * HY_W)
    pspec = lambda g: pl.BlockSpec((1, N, LANES), lambda b, c: (b, 0, g * nb + c))
    wspec = lambda g: pl.BlockSpec((3, LANES), lambda b, c: (0, g * nb + c))
    bspec = lambda g: pl.BlockSpec((1, LANES), lambda b, c: (0, g * nb + c))
    ospec = pl.BlockSpec((1, N, LANES), lambda b, c: (b, 0, c))
    return pl.pallas_call(
        _hyena_pre_kernel,
        grid=(B, nb),
        in_specs=[pspec(0), pspec(1), pspec(2), wspec(0), wspec(1), wspec(2), bspec(0), bspec(1), bspec(2)],
        out_specs=[ospec, ospec],
        out_shape=[jax.ShapeDtypeStruct((B, N, HY_W), f32), jax.ShapeDtypeStruct((B, N, HY_W), bf16)],
        compiler_params=_cparams(("parallel", "parallel")),
        name="hyena_pre",
    )(p, p, p, conv_w, conv_w, conv_w, conv_b, conv_b, conv_b)


DFT_SLABS = 8


def _hl(a):
    a32 = jnp.asarray(a.astype(np.float32))
    hi = a32.astype(bf16)
    lo = (a32 - hi.astype(f32)).astype(bf16)
    return jnp.concatenate([hi, lo], axis=-2)


def _dot3(a_hl, m, x):
    xh, xl = _split(x)
    r = jnp.dot(a_hl, xh, preferred_element_type=f32)
    return r[:m] + r[m:] + jnp.dot(a_hl[:m], xl, preferred_element_type=f32)


def _dft_consts(n):
    N = 2 * n
    na = N // LANES
    t1n = na // 2
    k1n = na // 2 + 1
    k1p = -(-k1n // 8) * 8
    k1 = np.arange(k1n)
    t1 = np.arange(t1n)
    th = 2 * np.pi * ((t1[None, :] * k1[:, None]) % na) / na
    f1c = np.zeros((2 * k1p, t1n))
    f1c[:k1n] = np.cos(th)
    f1c[k1p:k1p + k1n] = -np.sin(th)
    thf = 2 * np.pi * ((np.arange(na)[None, :] * k1[:, None]) % na) / na
    f1f = np.zeros((2 * k1p, na))
    f1f[:k1n] = np.cos(thf)
    f1f[k1p:k1p + k1n] = -np.sin(thf)
    k2 = np.arange(LANES)
    t2 = np.arange(LANES)
    m = (t2[None, None, :] * (k1[:, None, None] + na * k2[None, :, None])) % N
    ph = 2 * np.pi * m / N
    g = np.concatenate([np.cos(ph), -np.sin(ph)], axis=1)
    pht = ph.transpose(0, 2, 1)
    gi = np.concatenate([np.cos(pht), np.sin(pht)], axis=1)
    wk = np.where((k1 == 0) | (k1 == na // 2), 1.0, 2.0) / N
    f1i = np.zeros((t1n, 2 * k1p))
    f1i[:, :k1n] = np.cos(th.T) * wk[None, :]
    f1i[:, k1p:k1p + k1n] = -np.sin(th.T) * wk[None, :]
    k1e = -(-k1n // DFT_SLABS) * DFT_SLABS
    g = np.concatenate([g, np.zeros((k1e - k1n,) + g.shape[1:])], axis=0)
    gi = np.concatenate([gi, np.zeros((k1e - k1n,) + gi.shape[1:])], axis=0)
    return dict(na=na, t1n=t1n, k1n=k1n, k1e=k1e, k1p=k1p, f1c=_hl(f1c), f1f=_hl(f1f), g=_hl(g), gi=_hl(gi),
                f1i=_hl(f1i))


def _dft_stage1(src_ref, f1c_ref, are_ref, aim_ref, t1n, k1p):
    f1c = f1c_ref[...]

    def body(t2, carry):
        zs = src_ref[pl.ds(t2, t1n, stride=LANES), :]
        r = _dot3(f1c, 2 * k1p, zs)
        are_ref[pl.ds(t2, k1p, stride=LANES), :] = r[:k1p]
        aim_ref[pl.ds(t2, k1p, stride=LANES), :] = r[k1p:]
        return carry

    lax.fori_loop(0, LANES, body, 0, unroll=4)


def _cplx_left(gc_hl, xre, xim):
    cw = xre.shape[1]
    r = _dot3(gc_hl, 2 * LANES, jnp.concatenate([xre, xim], axis=1))
    p, q = r[:, :cw], r[:, cw:]
    return p[:LANES] - q[LANES:], p[LANES:] + q[:LANES]


def _spectrum_kernel(k_ref, f1f_ref, g_ref, inv_ref, xre_ref, xim_ref, are, aim, *, na, k1p):
    j = pl.program_id(1)

    @pl.when(j == 0)
    def _():
        _dft_stage1(k_ref, f1f_ref, are, aim, na, k1p)

    for half in range(DFT_SLABS):
        r0 = pl.multiple_of((DFT_SLABS * j + half) * LANES, LANES)
        xre, xim = _cplx_left(g_ref[half], are[pl.ds(r0, LANES), :], aim[pl.ds(r0, LANES), :])
        rows = slice(half * LANES, (half + 1) * LANES)
        xre_ref[rows, :] = xre * inv_ref[...]
        xim_ref[rows, :] = xim * inv_ref[...]


def hyena_spectrum(k, inv_norm, dc):
    n2, C = k.shape
    k1e, k1p, na = dc["k1e"], dc["k1p"], dc["na"]
    cw = LANES
    out = jax.ShapeDtypeStruct((k1e * LANES, C), f32)
    ospec = pl.BlockSpec((DFT_SLABS * LANES, cw), lambda c, k: (k, c))
    return pl.pallas_call(
        functools.partial(_spectrum_kernel, na=na, k1p=k1p),
        grid=(C // cw, k1e // DFT_SLABS),
        in_specs=[pl.BlockSpec((n2, cw), lambda c, k: (0, c)),
                  pl.BlockSpec(dc["f1f"].shape, lambda c, k: (0, 0)),
                  pl.BlockSpec((DFT_SLABS, 4 * LANES, LANES), lambda c, k: (k, 0, 0)),
                  pl.BlockSpec((1, cw), lambda c, k: (0, c))],
        out_specs=[ospec, ospec],
        out_shape=[out, out],
        scratch_shapes=[pltpu.VMEM((k1p * LANES, cw), f32), pltpu.VMEM((k1p * LANES, cw), f32)],
        compiler_params=_cparams(("parallel", "arbitrary")),
        name="hyena_spectrum",
    )(k, dc["f1f"], dc["g"], inv_norm)


def _hyena_conv_kernel(z_ref, x0_ref, f1c_ref, g_ref, gi_ref, f1i_ref, kre_ref, kim_ref,
                       skip_ref, o_ref, are, aim, y_scr, *, t1n, k1p):
    j = pl.program_id(2)

    @pl.when(j == 0)
    def _():
        _dft_stage1(z_ref.at[0], f1c_ref, are, aim, t1n, k1p)

    for half in range(DFT_SLABS):
        r0 = pl.multiple_of((DFT_SLABS * j + half) * LANES, LANES)
        xre, xim = _cplx_left(g_ref[half], are[pl.ds(r0, LANES), :], aim[pl.ds(r0, LANES), :])
        rows = slice(half * LANES, (half + 1) * LANES)
        kre = kre_ref[rows, :]
        kim = kim_ref[rows, :]
        yre = xre * kre - xim * kim
        yim = xre * kim + xim * kre
        bre, bim = _cplx_left(gi_ref[half], yre, yim)
        are[pl.ds(r0, LANES), :] = bre
        aim[pl.ds(r0, LANES), :] = bim

    @pl.when(j == pl.num_programs(2) - 1)
    def _():
        f1i = f1i_ref[...]

        def body(t2, carry):
            bb = jnp.concatenate([are[pl.ds(t2, k1p, stride=LANES), :], aim[pl.ds(t2, k1p, stride=LANES), :]], axis=0)
            y_scr[pl.ds(t2, t1n, stride=LANES), :] = _dot3(f1i, t1n, bb)
            return carry

        lax.fori_loop(0, LANES, body, 0, unroll=4)
        z = z_ref[0]
        o_ref[0] = (x0_ref[0].astype(f32) * (y_scr[...] + z * skip_ref[...])).astype(o_ref.dtype)


def hyena_conv(z, x0, spec_re, spec_im, skip, dc):
    B, n, W = z.shape
    nb = W // LANES
    k1e, k1p, t1n = dc["k1e"], dc["k1p"], dc["t1n"]
    seq = pl.BlockSpec((1, n, LANES), lambda b, c, k: (b, 0, c))
    fspec = pl.BlockSpec((DFT_SLABS * LANES, LANES), lambda b, c, k: (k, c))
    cspec = pl.BlockSpec((DFT_SLABS, 4 * LANES, LANES), lambda b, c, k: (k, 0, 0))
    vspec = pl.BlockSpec((1, LANES), lambda b, c, k: (0, c))
    return pl.pallas_call(
        functools.partial(_hyena_conv_kernel, t1n=t1n, k1p=k1p),
        grid=(B, nb, k1e // DFT_SLABS),
        in_specs=[seq, seq,
                  pl.BlockSpec(dc["f1c"].shape, lambda b, c, k: (0, 0)), cspec, cspec,
                  pl.BlockSpec(dc["f1i"].shape, lambda b, c, k: (0, 0)),
                  fspec, fspec, vspec],
        out_specs=seq,
        out_shape=jax.ShapeDtypeStruct((B, n, W), bf16),
        scratch_shapes=[pltpu.VMEM((k1p * LANES, LANES), f32), pltpu.VMEM((k1p * LANES, LANES), f32),
                        pltpu.VMEM((n, LANES), f32)],
        compiler_params=_cparams(("parallel", "parallel", "arbitrary")),
        name="hyena_conv",
    )(z, x0, dc["f1c"], dc["g"], dc["gi"], dc["f1i"], spec_re, spec_im, skip)


def _hyena_ctx_kernel(p0_ref, p1_ref, p2_ref, w0_ref, w1_ref, w2_ref, b0_ref, b1_ref, b2_ref,
                      k_ref, inv_ref, skip_ref, fd_ref, fi_ref, o_ref, *, n):
    x0 = _conv3(p0_ref[0], w0_ref, b0_ref)
    z = _conv3(p1_ref[0], w1_ref, b1_ref) * _conv3(p2_ref[0], w2_ref, b2_ref)
    fd = fd_ref[...]
    N = 2 * n
    zf = jnp.dot(fd[:, :n], z, precision=HI, preferred_element_type=f32)
    kf = jnp.dot(fd, k_ref[...], precision=HI, preferred_element_type=f32) * inv_ref[...]
    yre = zf[:N] * kf[:N] - zf[N:] * kf[N:]
    yim = zf[:N] * kf[N:] + zf[N:] * kf[:N]
    y = jnp.dot(fi_ref[...], jnp.concatenate([yre, yim], axis=0), precision=HI, preferred_element_type=f32)
    o_ref[0] = (x0 * (y + z * skip_ref[...])).astype(o_ref.dtype)


def hyena_ctx(p, conv_w, conv_b, k, inv_norm, skip):
    B, n, _ = p.shape
    N = 2 * n
    nb = HY_W // LANES
    kk = np.arange(N)
    ph = 2 * np.pi * ((kk[:, None] * kk[None, :]) % N) / N
    fd = jnp.asarray(np.concatenate([np.cos(ph), -np.sin(ph)], axis=0).astype(np.float32))
    fi = jnp.asarray((np.concatenate([np.cos(ph[:n]), -np.sin(ph[:n])], axis=1) / N).astype(np.float32))
    conv_b = conv_b.reshape(1, 3 * HY_W)
    pspec = lambda g: pl.BlockSpec((1, n, LANES), lambda b, c: (b, 0, g * nb + c))
    wspec = lambda g: pl.BlockSpec((3, LANES), lambda b, c: (0, g * nb + c))
    bspec = lambda g: pl.BlockSpec((1, LANES), lambda b, c: (0, g * nb + c))
    vspec = pl.BlockSpec((1, LANES), lambda b, c: (0, c))
    return pl.pallas_call(
        functools.partial(_hyena_ctx_kernel, n=n),
        grid=(B, nb),
        in_specs=[pspec(0), pspec(1), pspec(2), wspec(0), wspec(1), wspec(2), bspec(0), bspec(1), bspec(2),
                  pl.BlockSpec((N, LANES), lambda b, c: (0, c)),
                  vspec, vspec,
                  pl.BlockSpec(fd.shape, lambda b, c: (0, 0)), pl.BlockSpec(fi.shape, lambda b, c: (0, 0))],
        out_specs=pl.BlockSpec((1, n, LANES), lambda b, c: (b, 0, c)),
        out_shape=jax.ShapeDtypeStruct((B, n, HY_W), bf16),
        compiler_params=_cparams(("parallel", "parallel")),
        name="hyena_ctx",
    )(p, p, p, conv_w, conv_w, conv_w, conv_b, conv_b, conv_b, k, inv_norm, skip, fd, fi)


def _merge_kernel(of_ref, ob_ref, gs_ref, nb_ref, hc_ref, g_ref, wa_ref, wb_ref, wc_ref, wo_ref, x_ref, m_ref, o_ref):
    d = D_MODEL
    tot = of_ref[...] + ob_ref[...]
    gs = gs_ref[...].astype(f32)
    ra = []
    for h in range(HG_HEADS):
        sl = slice(LANES * h, LANES * (h + 1))
        th = tot[:, sl]
        ms = jnp.mean(th * th, axis=-1, keepdims=True)
        ra.append(th * lax.rsqrt(ms + EPS) * gs[:, sl])
    ya = jnp.dot(jnp.concatenate(ra, axis=1).astype(bf16), wa_ref[...], preferred_element_type=f32)
    yb = jnp.dot(nb_ref[...], wb_ref[...], preferred_element_type=f32)
    yc = jnp.dot(hc_ref[...], wc_ref[...], preferred_element_type=f32)
    g = g_ref[...].astype(f32)
    mix = g[:, :d] * ya + g[:, d:2 * d] * yb + g[:, 2 * d:] * yc
    y = jnp.dot(mix.astype(bf16), wo_ref[...], preferred_element_type=f32)
    o_ref[...] = x_ref[...] + m_ref[0] * y


def merge(o_f, o_b, gs, nb, hc, gates, wa, wb, wc, wo, x2d, m, rows_per_group):
    R = x2d.shape[0]
    tm = 512
    tpg = rows_per_group // tm
    G = m.shape[0]
    row = lambda w: pl.BlockSpec((tm, w), lambda i: (i, 0))
    full = lambda a: pl.BlockSpec(a.shape, lambda i: (0, 0))
    return pl.pallas_call(
        _merge_kernel,
        grid=(R // tm,),
        in_specs=[row(HG_W), row(HG_W), row(HG_W), row(NA_W), row(HY_W), row(3 * D_MODEL),
                  full(wa), full(wb), full(wc), full(wo),
                  row(D_MODEL), pl.BlockSpec((1, 1, D_MODEL), lambda i: (i // tpg, 0, 0))],
        out_specs=row(D_MODEL),
        out_shape=jax.ShapeDtypeStruct((R, D_MODEL), f32),
        compiler_params=_cparams(("parallel",)),
        name="merge",
    )(o_f, o_b, gs, nb, hc, gates, wa, wb, wc, wo, x2d, m.reshape(G, 1, D_MODEL))


def _router_kernel(x_ref, g_ref, sh_ref, sc_ref, wrt_ref, wr_ref, h_ref, at_ref, am_ref):
    x = x_ref[...]
    ms = jnp.mean(x * x, axis=-1, keepdims=True)
    h = x * lax.rsqrt(ms + EPS) * g_ref[...] * (1.0 + sc_ref[0]) + sh_ref[0]
    h_ref[...] = h.astype(h_ref.dtype)
    hh, hl = _split(h)
    ne = N_EXPERTS
    wt = wrt_ref[...]
    rt = _nt(wt, hh)
    lt = rt[:ne] + rt[ne:] + _nt(wt[:ne], hl)
    et = jnp.exp(lt - jnp.max(lt, axis=0, keepdims=True))
    at_ref[0] = et / jnp.sum(et, axis=0, keepdims=True)
    wm = wr_ref[...]
    rm = jnp.dot(hh, wm, preferred_element_type=f32)
    lm = rm[:, :ne] + rm[:, ne:] + jnp.dot(hl, wm[:, :ne], preferred_element_type=f32)
    em = jnp.exp(lm - jnp.max(lm, axis=1, keepdims=True))
    am_ref[...] = em / jnp.sum(em, axis=1, keepdims=True)


def router(x2d, g, shift, scale, w_router, n_per_set):
    R = x2d.shape[0]
    tm = min(512, n_per_set)
    tps = n_per_set // tm
    S = R // n_per_set
    G = shift.shape[0]
    gmap = (lambda i: (i // tps, 0, 0)) if G > 1 else (lambda i: (0, 0, 0))
    whi, wlo = _split(w_router.astype(f32))
    wr = jnp.concatenate([whi, wlo], axis=1)
    return pl.pallas_call(
        _router_kernel,
        grid=(R // tm,),
        in_specs=[pl.BlockSpec((tm, D_MODEL), lambda i: (i, 0)),
                  pl.BlockSpec((1, D_MODEL), lambda i: (0, 0)),
                  pl.BlockSpec((1, 1, D_MODEL), gmap),
                  pl.BlockSpec((1, 1, D_MODEL), gmap),
                  pl.BlockSpec((2 * N_EXPERTS, D_MODEL), lambda i: (0, 0)),
                  pl.BlockSpec((D_MODEL, 2 * N_EXPERTS), lambda i: (0, 0))],
        out_specs=[pl.BlockSpec((tm, D_MODEL), lambda i: (i, 0)),
                   pl.BlockSpec((1, N_EXPERTS, tm), lambda i: (i // tps, 0, i % tps)),
                   pl.BlockSpec((tm, N_EXPERTS), lambda i: (i, 0))],
        out_shape=[jax.ShapeDtypeStruct((R, D_MODEL), bf16),
                   jax.ShapeDtypeStruct((S, N_EXPERTS, n_per_set), f32),
                   jax.ShapeDtypeStruct((R, N_EXPERTS), f32)],
        compiler_params=_cparams(("parallel",)),
        name="router",
    )(x2d, g.reshape(1, D_MODEL), shift.reshape(G, 1, D_MODEL), scale.reshape(G, 1, D_MODEL), wr.T, wr)


SEL_BLK = 256
SUB = LANES
SUBW = SUB + 8
UNSEL = -float(2 ** 30)


def _prefix_incl(mask_f, tri, T):
    outs = []
    off = jnp.zeros((mask_f.shape[0], 1), f32)
    for b in range(T // SEL_BLK):
        blk = mask_f[:, b * SEL_BLK:(b + 1) * SEL_BLK].astype(bf16)
        pre = jnp.dot(blk, tri, preferred_element_type=f32) + off
        outs.append(pre)
        off = pre[:, SEL_BLK - 1:SEL_BLK]
    return jnp.concatenate(outs, axis=1)


def _select_kernel(a_ref, tri_ref, cm_ref, posm_ref, cnt_ref, *, T, cap):
    aff = a_ref[0]
    bits = pltpu.bitcast(aff, i32)
    tri = tri_ref[...]

    def bit_step(i, thr):
        cand = thr | (1 << (30 - i))
        cnt = jnp.sum((bits >= cand).astype(f32), axis=1, keepdims=True)
        return jnp.where(cnt >= cap, cand, thr)

    thr = lax.fori_loop(0, 31, bit_step, jnp.zeros((N_EXPERTS, 1), i32))
    gt = bits > thr
    eq = bits == thr
    need = cap - jnp.sum(gt.astype(f32), axis=1, keepdims=True)
    eqf = eq.astype(f32)
    rank_eq = _prefix_incl(eqf, tri, T) - eqf
    sel = gt | (eq & (rank_eq < need))
    self_ = sel.astype(f32)
    pos = _prefix_incl(self_, tri, T) - self_
    posm_ref[0] = jnp.where(sel, pos, UNSEL)
    cnt_ref[0] = jnp.dot(self_.astype(bf16), cm_ref[...], preferred_element_type=f32).astype(i32)


def select_topk(aff, cap):
    S, E, T = aff.shape
    tri = jnp.asarray(np.triu(np.ones((SEL_BLK, SEL_BLK), np.float32)), bf16)
    cm = jnp.asarray((np.arange(T)[:, None] < np.arange(LANES)[None, :] * SUB).astype(np.float32), bf16)
    return pl.pallas_call(
        functools.partial(_select_kernel, T=T, cap=cap),
        grid=(S,),
        in_specs=[pl.BlockSpec((1, E, T), lambda s: (s, 0, 0)),
                  pl.BlockSpec((SEL_BLK, SEL_BLK), lambda s: (0, 0)),
                  pl.BlockSpec((T, LANES), lambda s: (0, 0))],
        out_specs=[pl.BlockSpec((1, E, T), lambda s: (s, 0, 0)),
                   pl.BlockSpec((1, E, LANES), lambda s: (s, 0, 0))],
        out_shape=[jax.ShapeDtypeStruct((S, E, T), f32), jax.ShapeDtypeStruct((S, E, LANES), i32)],
        compiler_params=_cparams(("parallel",)),
        name="select_topk",
    )(aff, tri, cm)


def _align8(v):
    return lax.shift_left(lax.shift_right_logical(v, 3), 3)


def _align16(v):
    return lax.shift_left(lax.shift_right_logical(v, 4), 4)


CMB_ROWS = SUB + 16


def _gather_kernel(cnt_ref, h_ref, pos_ref, o_ref, acc, *, TT, cap, tps, R, srows, EP):
    tl = pl.program_id(1)

    @pl.when(tl == 0)
    def _():
        acc[...] = jnp.zeros_like(acc)

    st = tl // tps
    nsub = TT // SUB
    rid = lax.broadcasted_iota(i32, (srows, SUB), 0).astype(f32)
    for ep in range(EP):
        e = pl.program_id(0) * EP + ep
        cbase = (st * N_EXPERTS + e) * LANES + (tl % tps) * nsub
        for s in range(nsub):
            off8 = _align8(cnt_ref[cbase + s])
            pos = pos_ref[0, ep, :, s * SUB:(s + 1) * SUB]
            onehot = jnp.where(pos == rid + off8.astype(f32), 1.0, 0.0).astype(bf16)
            rows = jnp.dot(onehot, h_ref[s * SUB:(s + 1) * SUB, :], preferred_element_type=f32)
            r0 = pl.multiple_of(st * cap + off8, 8)
            acc[ep, pl.ds(r0, srows), :] += rows

    @pl.when(tl == pl.num_programs(1) - 1)
    def _():
        o_ref[...] = acc[:, 0:R, :].astype(o_ref.dtype)


def _gather_call(cnt, h, posm, cap, TT, srows, EP):
    S, E, T = posm.shape
    tps = T // TT
    R = S * cap
    gs = pltpu.PrefetchScalarGridSpec(
        num_scalar_prefetch=1,
        grid=(E // EP, S * tps),
        in_specs=[pl.BlockSpec((TT, D_MODEL), lambda e, t, c: (t, 0)),
                  pl.BlockSpec((1, EP, 1, TT), lambda e, t, c: (t // tps, e, 0, t % tps))],
        out_specs=pl.BlockSpec((EP, R, D_MODEL), lambda e, t, c: (e, 0, 0)),
        scratch_shapes=[pltpu.VMEM((EP, R + srows, D_MODEL), f32)])
    return pl.pallas_call(
        functools.partial(_gather_kernel, TT=TT, cap=cap, tps=tps, R=R, srows=srows, EP=EP),
        grid_spec=gs,
        out_shape=jax.ShapeDtypeStruct((E, R, D_MODEL), bf16),
        compiler_params=_cparams(("parallel", "arbitrary")),
        name="moe_gather",
    )(cnt.reshape(-1), h, posm.reshape(S, E, 1, T))


FAST_SUB_MAX = 48
FAST_TILE_MAX = 112
FAST_ROWS = 64
CMB_TILE = 512


def _fits_fast(cnt, T):
    nsub = T // SUB
    per_sub = cnt[..., 1:nsub + 1] - cnt[..., :nsub]
    k = CMB_TILE // SUB
    per_tile = cnt[..., k:nsub + 1:k] - cnt[..., 0:nsub:k]
    return (jnp.max(per_sub) <= FAST_SUB_MAX) & (jnp.max(per_tile) <= FAST_TILE_MAX)


def gather_rows(cnt, h, posm, cap, TT, fast_ok=None):
    safe = lambda: _gather_call(cnt, h, posm, cap, TT, SUBW, 1)
    if fast_ok is None:
        return safe()
    return lax.cond(fast_ok, lambda: _gather_call(cnt, h, posm, cap, TT, FAST_ROWS, 2), safe)


EXPERT_TF = 256


def _ffn_kernel(*refs, n):
    xs = refs[:n]
    wg_ref, wu_ref, wd_ref = refs[n:n + 3]
    his = refs[n + 3:2 * n + 3]
    los = refs[2 * n + 3:3 * n + 3]
    accs = refs[3 * n + 3:]
    j = pl.program_id(1)
    wg = wg_ref[0, 0].astype(bf16)
    wu = wu_ref[0, 0].astype(bf16)
    wd = wd_ref[0, 0].astype(bf16)
    @pl.when(j == 0)
    def _():
        for acc in accs:
            acc[...] = jnp.zeros_like(acc)

    for x_ref, hi_ref, lo_ref, acc in zip(xs, his, los, accs):
        x = x_ref[0]
        a = jnp.dot(x, wg, preferred_element_type=f32)
        u = jnp.dot(x, wu, preferred_element_type=f32)
        acc[...] += jnp.dot((_silu(a) * u).astype(bf16), wd, preferred_element_type=f32)

        @pl.when(j == pl.num_programs(1) - 1)
        def _(acc=acc, hi_ref=hi_ref, lo_ref=lo_ref):
            hi, lo = _split(acc[...])
            hi_ref[0] = hi
            lo_ref[0] = lo


def expert_ffn(xgs, layer, w_gate, w_up, w_down):
    E = xgs[0].shape[0]
    nf = D_FF_EXPERT // EXPERT_TF
    n = len(xgs)
    rowspec = lambda a: pl.BlockSpec((1, a.shape[1], D_MODEL), lambda e, j: (e, 0, 0))
    res = pl.pallas_call(
        functools.partial(_ffn_kernel, n=n),
        grid=(E, nf),
        in_specs=[rowspec(a) for a in xgs] + [
            pl.BlockSpec((1, 1, D_MODEL, EXPERT_TF), lambda e, j: (layer, e, 0, j)),
            pl.BlockSpec((1, 1, D_MODEL, EXPERT_TF), lambda e, j: (layer, e, 0, j)),
            pl.BlockSpec((1, 1, EXPERT_TF, D_MODEL), lambda e, j: (layer, e, j, 0))],
        out_specs=[rowspec(a) for a in xgs] * 2,
        out_shape=[jax.ShapeDtypeStruct(a.shape, bf16) for a in xgs] * 2,
        scratch_shapes=[pltpu.VMEM(a.shape[1:], f32) for a in xgs],
        compiler_params=_cparams(("parallel", "arbitrary")),
        name="expert_ffn",
    )(*xgs, w_gate, w_up, w_down)
    return [(res[i], res[n + i]) for i in range(n)]


def _combine_kernel(cnt_ref, x_ref, pos_ref, am_ref, m_ref, *rest, TT, cap, R, W, ytot, crows, EP):
    y_refs, o_ref = rest[:2 * EP], rest[2 * EP]
    st = pl.program_id(0)
    tl = pl.program_id(1)
    eg = pl.program_id(2)

    @pl.when(eg == 0)
    def _():
        o_ref[...] = x_ref[...]

    nsub = TT // SUB
    lane = lax.broadcasted_iota(i32, (TT, N_EXPERTS), 1)
    m5 = m_ref[0]
    rid = lax.broadcasted_iota(i32, (crows, SUB), 0).astype(f32)
    am = am_ref[...]
    gcols, wss = [], []
    for ep in range(EP):
        e = eg * EP + ep
        cbase = (st * N_EXPERTS + e) * LANES + tl * nsub
        wss.append(jnp.minimum(e * R + st * cap + _align16(cnt_ref[cbase]), ytot - W))
        gcols.append(jnp.sum(jnp.where(lane == e, am, 0.0), axis=1, keepdims=True))
    for s in range(nsub):
        sl = slice(s * SUB, (s + 1) * SUB)
        tot = None
        for ep in range(EP):
            e = eg * EP + ep
            rowbase = e * R + st * cap
            off = _align16(cnt_ref[(st * N_EXPERTS + e) * LANES + tl * nsub + s])
            rel = pl.multiple_of(jnp.minimum(rowbase + off - wss[ep], W - crows), 16)
            first = (wss[ep] + rel - rowbase).astype(f32)
            pos = pos_ref[0, ep, :, sl]
            onehot = jnp.where(pos == rid + first, 1.0, 0.0).astype(bf16)
            ywin = jnp.concatenate([y_refs[2 * ep][pl.ds(rel, crows), :], y_refs[2 * ep + 1][pl.ds(rel, crows), :]],
                                   axis=0)
            picked = _tn(jnp.concatenate([onehot, onehot], axis=0), ywin)
            term = gcols[ep][sl] * picked
            tot = term if tot is None else tot + term
        o_ref[sl, :] += m5 * tot


def _combine_call(cnt, x2d, posm, aff_tm, mvec, y_hl, cap, TT, crows, W, EP):
    S, E, T = posm.shape
    tps = T // TT
    R = S * cap
    ytot = E * R
    nsub = TT // SUB
    G = mvec.shape[0]

    def yspec(ep):
        def ymap(st, tl, eg, c):
            e = eg * EP + ep
            off = _align16(c[(st * E + e) * LANES + tl * nsub])
            return (pl.multiple_of(jnp.minimum(e * R + st * cap + off, ytot - W), 16), 0)
        return pl.BlockSpec((pl.Element(W), pl.Element(D_MODEL)), ymap)

    tok = lambda w: pl.BlockSpec((TT, w), lambda st, tl, e, c: (st * tps + tl, 0))
    mmap = (lambda st, tl, e, c: (st, 0, 0)) if G > 1 else (lambda st, tl, e, c: (0, 0, 0))
    gs = pltpu.PrefetchScalarGridSpec(
        num_scalar_prefetch=1,
        grid=(S, tps, E // EP),
        in_specs=[tok(D_MODEL),
                  pl.BlockSpec((1, EP, 1, TT), lambda st, tl, e, c: (st, e, 0, tl)),
                  tok(N_EXPERTS),
                  pl.BlockSpec((1, 1, D_MODEL), mmap)] + [yspec(ep) for ep in range(EP) for _ in range(2)],
        out_specs=tok(D_MODEL))
    yh, yl = y_hl[0].reshape(ytot, D_MODEL), y_hl[1].reshape(ytot, D_MODEL)
    return pl.pallas_call(
        functools.partial(_combine_kernel, TT=TT, cap=cap, R=R, W=W, ytot=ytot, crows=crows, EP=EP),
        grid_spec=gs,
        out_shape=jax.ShapeDtypeStruct(x2d.shape, f32),
        compiler_params=_cparams(("parallel", "parallel", "arbitrary")),
        name="moe_combine",
    )(cnt.reshape(-1), x2d, posm.reshape(S, E, 1, T), aff_tm, mvec.reshape(G, 1, D_MODEL), *([yh, yl] * EP))


def combine(cnt, x2d, posm, aff_tm, mvec, y_hl, cap, TT, fast_ok=None):
    safe = lambda: _combine_call(cnt, x2d, posm, aff_tm, mvec, y_hl, cap, TT, CMB_ROWS, TT + 32, 1)
    if fast_ok is None:
        return safe()
    wfast = FAST_TILE_MAX + 16 + FAST_ROWS
    return lax.cond(fast_ok, lambda: _combine_call(cnt, x2d, posm, aff_tm, mvec, y_hl, cap, TT, FAST_ROWS, wfast, 4),
                    safe)


def _rope_tables(n):
    half = NA_HD // 2
    q = half // 2
    inv = ROPE_THETA ** (-np.arange(q, dtype=np.float64) / q)
    pos = np.arange(n)
    ang_r = (pos // GRID_W)[:, None] * inv
    ang_c = (pos % GRID_W)[:, None] * inv
    zero = np.zeros_like(ang_r)
    c = np.concatenate([np.cos(ang_r)] * 2 + [np.cos(ang_c)] * 2, axis=1)
    s1 = np.concatenate([-np.sin(ang_r), zero, -np.sin(ang_c), zero], axis=1)
    s2 = np.concatenate([zero, np.sin(ang_r), zero, np.sin(ang_c)], axis=1)
    two = lambda a: jnp.asarray(np.concatenate([a, a], axis=1).astype(np.float32))
    return two(c), two(s1), two(s2)


def _mixing(hx, hc, need_ctx, B, N, NC, la, lc, w_in, q_gain, k_gain, table, rope, bd, conv_w, conv_b,
            spec, skip, filt_c, wa, wb, wc, wo, x2d, c2d, mx2, mc2):
    tile8 = lambda v: jnp.tile(v.reshape(1, NA_HD), (1, NA_HEADS))
    qg, kg = tile8(q_gain), tile8(k_gain)
    norm_aux = [("col", kg), ("const", bd)]
    rope_aux = [("row", rope[0]), ("row", rope[1]), ("row", rope[2])]
    lf_aux = lambda d: [("col", la[d:d + 1]), ("col", lc[d:d + 1])]
    tc = hc.shape[0]

    lff_c = project(hc, w_in, OFF_FF, 512, _epi_logforget, lf_aux(0), (f32,), tm=tc).reshape(B, NC, 512)
    lfb_c = project(hc, w_in, OFF_FB, 512, _epi_logforget, lf_aux(1), (f32,), tm=tc).reshape(B, NC, 512)
    i_c = project(hc, w_in, OFF_I, 512, _epi_raw, tm=tc).reshape(B, NC, 512)
    k_c = project(hc, w_in, OFF_NK, 512, _epi_norm, norm_aux, tm=tc).reshape(B, NC, 512)
    v_c = project(hc, w_in, OFF_NV, 512, _epi_raw, tm=tc).reshape(B, NC, 512)
    if need_ctx:
        q_c = project(hc, w_in, OFF_HQ, 512, _epi_silu, tm=tc).reshape(B, NC, 512)
    else:
        q_c = jnp.zeros((B, NC, 512), bf16)
    s0 = jnp.zeros((B, HG_HEADS, LANES, LANES), f32)
    oc_f, oc_b, s_f, s_b = hgrn_bidir(lff_c, lfb_c, i_c, q_c, s0, s0)

    lff_x = project(hx, w_in, OFF_FF, 512, _epi_logforget, lf_aux(0), (f32,)).reshape(B, N, 512)
    lfb_x = project(hx, w_in, OFF_FB, 512, _epi_logforget, lf_aux(1), (f32,)).reshape(B, N, 512)
    i_x = project(hx, w_in, OFF_I, 512, _epi_raw).reshape(B, N, 512)
    q_x = project(hx, w_in, OFF_HQ, 512, _epi_silu).reshape(B, N, 512)
    g_x = project(hx, w_in, OFF_HG, 512, _epi_silu).reshape(B, N, 512)
    k_x = project(hx, w_in, OFF_NK, 512, _epi_norm_rope, norm_aux + rope_aux, rows_per_seq=N).reshape(B, N, 512)
    v_x = project(hx, w_in, OFF_NV, 512, _epi_raw).reshape(B, N, 512)
    qn_x, qr_x = project(hx, w_in, OFF_NQ, 512, _epi_norm_both, [("col", qg), ("const", bd)] + rope_aux,
                         (bf16, bf16), rows_per_seq=N)
    p_x = project(hx, w_in, OFF_HY, 3 * HY_W, _epi_raw, out_dtypes=(f32,)).reshape(B, N, 3 * HY_W)
    gates_x = project(hx, w_in, OFF_GATE, 3 * D_MODEL, _epi_sigmoid)

    ox_f, ox_b, _, _ = hgrn_bidir(lff_x, lfb_x, i_x, q_x, s_f, s_b)

    nb_x = neighbourhood_attention(qr_x.reshape(B, N, 512), qn_x.reshape(B, N, 512), k_x, v_x, k_c, v_c, table)

    z_x, x0_x = hyena_pre(p_x, conv_w, conv_b)
    hy_x = hyena_conv(z_x, x0_x, spec[0], spec[1], skip, spec[2])

    flat = lambda a: a.reshape(-1, a.shape[-1])
    x_new = merge(flat(ox_f), flat(ox_b), flat(g_x), flat(nb_x), flat(hy_x), gates_x, wa, wb, wc, wo, x2d, mx2, N)
    if not need_ctx:
        return x_new, None

    qn_c = project(hc, w_in, OFF_NQ, 512, _epi_norm, [("col", qg), ("const", bd)], tm=tc).reshape(B, NC, 512)
    nb_c = context_attention(qn_c, k_c, v_c)
    p_c = project(hc, w_in, OFF_HY, 3 * HY_W, _epi_raw, out_dtypes=(f32,), tm=tc).reshape(B, NC, 3 * HY_W)
    hy_c = hyena_ctx(p_c, conv_w, conv_b, filt_c[0], filt_c[1], skip)
    gates_c = project(hc, w_in, OFF_GATE, 3 * D_MODEL, _epi_sigmoid, tm=tc)
    g_c = project(hc, w_in, OFF_HG, 512, _epi_silu, tm=tc)
    c_new = merge(flat(oc_f), flat(oc_b), g_c, flat(nb_c), flat(hy_c), gates_c, wa, wb, wc, wo, c2d, mc2, B * NC)
    return x_new, c_new


def kernel(x, c, ctx, c_ctx, w_mod, b_mod, norm_mix, norm_ffn, w_in, hg_lb, na_q_gain, na_k_gain, na_rpb,
           hy_conv_w, hy_conv_b, hy_pe_w1, hy_pe_b1, hy_pe_freq1, hy_pe_w2, hy_pe_b2, hy_pe_freq2, hy_pe_w3,
           hy_skip, w_branch_a, w_branch_b, w_branch_c, w_out, w_router, w_e_gate, w_e_up, w_e_down):
    B, N, D = x.shape
    NC = ctx.shape[1]
    E = N_EXPERTS
    cap_x = EC_CAP_FACTOR * N // E
    cap_c = EC_CAP_FACTOR * NC // E

    lb = jnp.cumsum(jax.nn.softmax(hg_lb.astype(f32), axis=0), axis=0)
    lb = lb - lb[:1]
    la_all, lc_all = jnp.log(lb), jnp.log1p(-lb)

    s8 = jnp.zeros((8, D), f32).at[:B].set(c).at[B].set(c_ctx)
    rope = _rope_tables(N)
    bd = jnp.asarray(np.kron(np.eye(NA_HEADS), np.full((NA_HD, NA_HD), 1.0 / NA_HD)).astype(np.float32), bf16)
    dcx = _dft_consts(N)

    x2d = x.reshape(B * N, D)
    c2d = ctx.reshape(B * NC, D)
    for l in range(DEPTH):
        need_ctx = l < DEPTH - 1
        mv = modvec(s8, w_mod[l], b_mod[l])
        mx = [mv[:B, k * D:(k + 1) * D] for k in range(6)]
        mc = [mv[B:B + 1, k * D:(k + 1) * D] for k in range(6)]
        w_in_l = w_in[l].astype(bf16)
        hx = modulate(x2d, norm_mix[l], mx[0], mx[1], N, bf16)
        hc = modulate(c2d, norm_mix[l], mc[0], mc[1], B * NC, bf16)

        filt = (hy_pe_w1[l], hy_pe_b1[l], hy_pe_freq1[l], hy_pe_w2[l], hy_pe_b2[l], hy_pe_freq2[l], hy_pe_w3[l])
        k_x, nrm_x = hyena_filter(N, *filt)
        sre, sim = hyena_spectrum(k_x, 1.0 / nrm_x, dcx)
        skip = hy_skip[l].reshape(1, HY_W)
        filt_c = None
        if need_ctx:
            h_c, nrm_c = hyena_filter(NC, *filt)
            filt_c = (h_c, 1.0 / nrm_c)

        x2d, c_new = _mixing(
            hx, hc, need_ctx, B, N, NC, la_all[l], lc_all[l], w_in_l, na_q_gain[l], na_k_gain[l],
            _na_bias_table(na_rpb[l]), rope, bd, hy_conv_w[l], hy_conv_b[l], (sre, sim, dcx), skip,
            filt_c, w_branch_a[l].astype(bf16), w_branch_b[l].astype(bf16), w_branch_c[l].astype(bf16),
            w_out[l].astype(bf16), x2d, c2d, mx[2], mc[2])

        h2, aff_t, aff_m = router(x2d, norm_ffn[l], mx[3], mx[4], w_router[l], N)
        posm, cnt = select_topk(aff_t, cap_x)
        fast_ok = _fits_fast(cnt, N)
        xgs = [gather_rows(cnt, h2, posm, cap_x, 1024, fast_ok)]
        if need_ctx:
            c2d = c_new
            hc2, aff_tc, aff_mc = router(c2d, norm_ffn[l], mc[3], mc[4], w_router[l], NC)
            posm_c, cnt_c = select_topk(aff_tc, cap_c)
            xgs.append(gather_rows(cnt_c, hc2, posm_c, cap_c, NC))
        ys = expert_ffn(xgs, l, w_e_gate, w_e_up, w_e_down)
        x2d = combine(cnt, x2d, posm, aff_m, mx[5], ys[0], cap_x, CMB_TILE, fast_ok)
        if need_ctx:
            c2d = combine(cnt_c, c2d, posm_c, aff_mc, mc[5], ys[1], cap_c, NC)
    return x2d.reshape(B, N, D)
```

```python
import functools
import math

import numpy as np
import jax
import jax.numpy as jnp
from jax import lax
from jax.experimental import pallas as pl
from jax.experimental.pallas import tpu as pltpu

f32 = jnp.float32
bf16 = jnp.bfloat16
i32 = jnp.int32
HI = lax.Precision.HIGHEST

D_MODEL = 1024
DEPTH = 2
GRID_W = 64
EPS = 1e-6
HG_HEADS = 4
HG_W = 512
HG_CHUNK = 64
NA_HEADS = 8
NA_HD = 64
NA_W = 512
NA_WIN_R = 8
NA_WIN_C = 16
ROPE_THETA = 10000.0
HY_W = 512
HY_BANDS = 16
HY_PE_DIM = 1 + 2 * HY_BANDS
HY_FILT_HID = 64
HY_FAST_DECAY = 0.3
HY_SLOW_DECAY = 1.5
HY_TARGET = 1e-2
OFF_FF = 0
OFF_FB = 512
OFF_I = 1024
OFF_NK = 1536
OFF_NV = 2048
OFF_HQ = 2560
OFF_NQ = 3072
OFF_HG = 3584
OFF_HY = 4096
OFF_GATE = 5632
IN_COLS = 8704
N_EXPERTS = 16
EC_CAP_FACTOR = 2
D_FF_EXPERT = 2816

LANES = 128
NEG_BIG = -1e30
VMEM_LIMIT = 56 * 1024 * 1024


def _cparams(sem, vmem=VMEM_LIMIT):
    return pltpu.CompilerParams(dimension_semantics=sem, vmem_limit_bytes=vmem)


def _nt(a, b, precision=None):
    return lax.dot_general(a, b, (((1,), (1,)), ((), ())), precision=precision, preferred_element_type=f32)


def _tn(a, b, precision=None):
    return lax.dot_general(a, b, (((0,), (0,)), ((), ())), precision=precision, preferred_element_type=f32)


def _silu(x):
    return x * jax.nn.sigmoid(x)


def _split(x):
    hi = x.astype(bf16)
    return hi, (x - hi.astype(f32)).astype(bf16)


def _modvec_kernel(s_ref, w_ref, b_ref, o_ref):
    s = _silu(s_ref[...])
    o_ref[...] = jnp.dot(s, w_ref[...], precision=HI, preferred_element_type=f32) + b_ref[...]


def modvec(s8, w, b):
    n = w.shape[1]
    tn = 1024
    return pl.pallas_call(
        _modvec_kernel,
        grid=(n // tn,),
        in_specs=[pl.BlockSpec((8, D_MODEL), lambda j: (0, 0)),
                  pl.BlockSpec((D_MODEL, tn), lambda j: (0, j)),
                  pl.BlockSpec((1, tn), lambda j: (0, j))],
        out_specs=pl.BlockSpec((8, tn), lambda j: (0, j)),
        out_shape=jax.ShapeDtypeStruct((8, n), f32),
        compiler_params=_cparams(("parallel",)),
        name="modvec",
    )(s8, w, b.reshape(1, n))


def _modulate_kernel(x_ref, g_ref, sh_ref, sc_ref, o_ref):
    x = x_ref[...]
    ms = jnp.mean(x * x, axis=-1, keepdims=True)
    y = x * lax.rsqrt(ms + EPS)
    o_ref[...] = (y * g_ref[...] * (1.0 + sc_ref[0]) + sh_ref[0]).astype(o_ref.dtype)


def modulate(x2d, g, shift, scale, rows_per_group, out_dtype):
    R = x2d.shape[0]
    tm = 512
    tpg = rows_per_group // tm
    G = shift.shape[0]
    return pl.pallas_call(
        _modulate_kernel,
        grid=(R // tm,),
        in_specs=[pl.BlockSpec((tm, D_MODEL), lambda i: (i, 0)),
                  pl.BlockSpec((1, D_MODEL), lambda i: (0, 0)),
                  pl.BlockSpec((1, 1, D_MODEL), lambda i: (i // tpg, 0, 0)),
                  pl.BlockSpec((1, 1, D_MODEL), lambda i: (i // tpg, 0, 0))],
        out_specs=pl.BlockSpec((tm, D_MODEL), lambda i: (i, 0)),
        out_shape=jax.ShapeDtypeStruct((R, D_MODEL), out_dtype),
        compiler_params=_cparams(("parallel",)),
        name="modulate",
    )(x2d, g.reshape(1, D_MODEL), shift.reshape(G, 1, D_MODEL), scale.reshape(G, 1, D_MODEL))


def _log1p_exp_neg(a):
    return jnp.log(1.0 + jnp.exp(-a))


def _log_sigmoid(z):
    return jnp.minimum(z, 0.0) - _log1p_exp_neg(jnp.abs(z))


def _epi_raw(acc, o_ref):
    o_ref[...] = acc.astype(o_ref.dtype)


def _epi_silu(acc, o_ref):
    o_ref[...] = _silu(acc).astype(o_ref.dtype)


def _epi_sigmoid(acc, o_ref):
    o_ref[...] = jax.nn.sigmoid(acc).astype(o_ref.dtype)


def _epi_logforget(acc, la_ref, lc_ref, o_ref):
    la = la_ref[...]
    c = lc_ref[...] + _log_sigmoid(acc)
    o_ref[...] = jnp.maximum(la, c) + _log1p_exp_neg(jnp.abs(la - c))


def _head_rms(acc, gain_ref, bd_ref):
    hi, lo = _split(acc * acc)
    ms = jnp.dot(hi, bd_ref[...], preferred_element_type=f32) + jnp.dot(lo, bd_ref[...], preferred_element_type=f32)
    return acc * lax.rsqrt(ms + EPS) * gain_ref[...]


def _rope(y, c_ref, s1_ref, s2_ref):
    reps = y.shape[1] // LANES
    c = jnp.concatenate([c_ref[...]] * reps, axis=1)
    s1 = jnp.concatenate([s1_ref[...]] * reps, axis=1)
    s2 = jnp.concatenate([s2_ref[...]] * reps, axis=1)
    w = y.shape[1]
    return y * c + pltpu.roll(y, w - 16, axis=1) * s1 + pltpu.roll(y, 16, axis=1) * s2


def _epi_norm(acc, gain_ref, bd_ref, o_ref):
    o_ref[...] = _head_rms(acc, gain_ref, bd_ref).astype(o_ref.dtype)


def _epi_norm_rope(acc, gain_ref, bd_ref, c_ref, s1_ref, s2_ref, o_ref):
    y = _head_rms(acc, gain_ref, bd_ref)
    o_ref[...] = _rope(y, c_ref, s1_ref, s2_ref).astype(o_ref.dtype)


def _epi_norm_both(acc, gain_ref, bd_ref, c_ref, s1_ref, s2_ref, on_ref, or_ref):
    y = _head_rms(acc, gain_ref, bd_ref)
    on_ref[...] = y.astype(on_ref.dtype)
    or_ref[...] = _rope(y, c_ref, s1_ref, s2_ref).astype(or_ref.dtype)


def _proj_kernel(h_ref, w_ref, *rest, epi):
    acc = jnp.dot(h_ref[...], w_ref[...], preferred_element_type=f32)
    epi(acc, *rest)


def project(h, w, c0, width, epi, aux=(), out_dtypes=(bf16,), tm=2048, rows_per_seq=None):
    R = h.shape[0]
    tn = 512
    nj = width // tn
    cb = c0 // tn
    in_specs = [pl.BlockSpec((tm, D_MODEL), lambda i, j: (i, 0)),
                pl.BlockSpec((D_MODEL, tn), lambda i, j: (0, cb + j))]
    args = [h, w]
    for kind, arr in aux:
        if kind == "col":
            in_specs.append(pl.BlockSpec((1, tn), lambda i, j: (0, j)))
        elif kind == "const":
            in_specs.append(pl.BlockSpec(arr.shape, lambda i, j: (0, 0)))
        else:
            tps = rows_per_seq // tm
            in_specs.append(pl.BlockSpec((tm, LANES), lambda i, j: (i % tps, 0)))
        args.append(arr)
    out_specs = [pl.BlockSpec((tm, tn), lambda i, j: (i, j)) for _ in out_dtypes]
    out_shape = [jax.ShapeDtypeStruct((R, width), dt) for dt in out_dtypes]
    res = pl.pallas_call(
        functools.partial(_proj_kernel, epi=epi),
        grid=(R // tm, nj),
        in_specs=in_specs,
        out_specs=out_specs,
        out_shape=out_shape,
        compiler_params=_cparams(("parallel", "parallel")),
        name="proj_" + epi.__name__[5:],
    )(*args)
    return res[0] if len(res) == 1 else res


HG_MM_LEVELS = 3


def _hgrn_tmatrix(C, reverse):
    t = np.arange(C)
    tau = (C - 1 - t) if reverse else t
    tt, uu = tau[:, None], tau[None, :]
    T = np.zeros((1 + HG_MM_LEVELS, C, C), np.float32)
    T[0] = uu <= tt
    for l in range(HG_MM_LEVELS):
        same = (tt >> (l + 1)) == (uu >> (l + 1))
        tr = ((tt >> l) & 1) == 1
        ur = ((uu >> l) & 1) == 1
        T[1 + l] = same & ((tr & ur & (uu <= tt)) | (~tr & ~ur & (uu > tt)))
    return T.reshape((1 + HG_MM_LEVELS) * C, C)


def _hgrn_level_sums(b, l, C, reverse):
    h = 1 << l
    parts = []
    for r0 in range(0, C, 2 * h):
        tb = r0 + h if reverse else r0 + h - 1
        d = b[r0:r0 + 2 * h] - b[tb:tb + 1]
        first, second = d[:h], d[h:]
        parts += [first, -second] if reverse else [-first, second]
    return jnp.concatenate(parts, axis=0)


def _hgrn_step(chains, C):
    L = int(round(math.log2(C)))
    row = lax.broadcasted_iota(i32, (C, LANES), 0)
    ti0 = lax.broadcasted_iota(i32, (C, C), 0)
    si0 = lax.broadcasted_iota(i32, (C, C), 1)

    es, kcs = [], []
    for lf, _, _, tm, _, reverse in chains:
        W = lf.shape[1]
        hi, lo = _split(lf)
        r = jnp.dot(tm, jnp.concatenate([hi, lo], axis=1), preferred_element_type=f32)
        r = r[:, :W] + r[:, W:]
        b = r[0:C]
        end_row = 0 if reverse else C - 1
        levels = [r[(1 + l) * C:(2 + l) * C] for l in range(HG_MM_LEVELS)]
        levels += [_hgrn_level_sums(b, l, C, reverse) for l in range(HG_MM_LEVELS, L)]
        es.append((b, b[end_row:end_row + 1] - b, levels))
        kcs.append(1.0 - jnp.exp(lf))

    items = []
    for (lf, v_all, q_all, _, s_view, reverse), (b_all, suf_all, levels), kc_all in zip(chains, es, kcs):
        tau = (C - 1 - row) if reverse else row
        ti, si = (C - 1 - ti0, C - 1 - si0) if reverse else (ti0, si0)
        end_row = 0 if reverse else C - 1
        for h in range(HG_HEADS):
            sl = slice(LANES * h, LANES * (h + 1))
            qh, kch, vh = q_all[:, sl], kc_all[:, sl], v_all[:, sl]
            b = b_all[:, sl]
            st = s_view[h]
            pairs = [(qh.astype(bf16), kch.astype(bf16), ti == si)]
            for l in range(L):
                x = jnp.exp(levels[l][:, sl])
                later = ((tau >> l) & 1) == 1
                pairs.append((jnp.where(later, qh * x, 0.0).astype(bf16), jnp.where(later, 0.0, kch * x).astype(bf16),
                              (ti >> (l + 1)) == (si >> (l + 1))))
            items.append(dict(qb=(qh * jnp.exp(b)).astype(bf16), st=st, pairs=pairs, vh=vh,
                              kd=(kch * jnp.exp(suf_all[:, sl])).astype(bf16),
                              decay=jnp.exp(b[end_row:end_row + 1, :]), view=s_view, h=h))

    for it in items:
        it["o"] = _nt(it["qb"], it["st"].astype(bf16))
        it["att"] = [(_nt(ql, kl), m) for ql, kl, m in it["pairs"]]

    outs = []
    for it in items:
        att = None
        for a, m in it["att"]:
            t = jnp.where(m, a, 0.0)
            att = t if att is None else att + t
        it["o"] = it["o"] + jnp.dot(att.astype(bf16), it["vh"], preferred_element_type=f32)
        it["view"][it["h"]] = it["st"] * it["decay"] + _tn(it["vh"], it["kd"])
    nh = HG_HEADS
    for c in range(len(chains)):
        outs.append(jnp.concatenate([items[c * nh + h]["o"] for h in range(nh)], axis=1))
    return outs


def _hgrn_kernel(lff_ref, lfb_ref, vf_ref, vb_ref, qf_ref, qb_ref, s0f_ref, s0b_ref, tf_ref, tb_ref,
                 of_ref, ob_ref, sff_ref, sfb_ref, s_scr, *, C, B):
    c = pl.program_id(0)

    @pl.when(c == 0)
    def _():
        s_scr[0] = s0f_ref[...]
        s_scr[1] = s0b_ref[...]

    chains = []
    for b in range(B):
        chains.append((lff_ref[b], vf_ref[b], qf_ref[b].astype(f32), tf_ref[...], s_scr.at[0, b], False))
        chains.append((lfb_ref[b], vb_ref[b], qb_ref[b].astype(f32), tb_ref[...], s_scr.at[1, b], True))
    outs = _hgrn_step(chains, C)
    for b in range(B):
        of_ref[b] = outs[2 * b]
        ob_ref[b] = outs[2 * b + 1]

    @pl.when(c == pl.num_programs(0) - 1)
    def _():
        sff_ref[...] = s_scr[0]
        sfb_ref[...] = s_scr[1]


def hgrn_bidir(lf_f, lf_b, v, q, s0_f, s0_b):
    B, N, W = lf_f.shape
    C = HG_CHUNK
    nch = N // C
    tf = jnp.asarray(_hgrn_tmatrix(C, False), bf16)
    tb = jnp.asarray(_hgrn_tmatrix(C, True), bf16)
    fw = pl.BlockSpec((B, C, W), lambda c: (0, c, 0))
    bw = pl.BlockSpec((B, C, W), lambda c: (0, nch - 1 - c, 0))
    st = pl.BlockSpec((B, HG_HEADS, LANES, LANES), lambda c: (0, 0, 0, 0))
    tsp = pl.BlockSpec(tf.shape, lambda c: (0, 0))
    seq = jax.ShapeDtypeStruct((B, N, W), f32)
    sts = jax.ShapeDtypeStruct((B, HG_HEADS, LANES, LANES), f32)
    return pl.pallas_call(
        functools.partial(_hgrn_kernel, C=C, B=B),
        grid=(nch,),
        in_specs=[fw, bw, fw, bw, fw, bw, st, st, tsp, tsp],
        out_specs=[fw, bw, st, st],
        out_shape=[seq, seq, sts, sts],
        scratch_shapes=[pltpu.VMEM((2, B, HG_HEADS, LANES, LANES), f32)],
        compiler_params=_cparams(("arbitrary",)),
        name="hgrn",
    )(lf_f, lf_b, v, v, q, q, s0_f, s0_b, tf, tb)


def _na_kernel(qr_ref, qn_ref, k_ref, v_ref, kc_ref, vc_ref, tab_ref, o_ref, s_scr, p_scr, *, rows_per_step, n_rows):
    g = pl.program_id(2)
    scale = NA_HD ** -0.5
    lane = lax.broadcasted_iota(i32, (GRID_W, LANES), 1)
    kcx = kc_ref[0]
    vcx = vc_ref[0]
    win = NA_WIN_R * GRID_W
    ctx_len = kcx.shape[0]

    starts = []
    for i in range(rows_per_step):
        r = g * rows_per_step + i
        rs = jnp.clip(r - NA_WIN_R // 2, 0, n_rows - NA_WIN_R)
        off = rs - r + (NA_WIN_R - 1)
        start = pl.multiple_of(rs * GRID_W, GRID_W)
        starts.append(start)
        kw = k_ref[0, pl.ds(start, win), :]
        qr = qr_ref[0, i * GRID_W:(i + 1) * GRID_W, :]
        qn = qn_ref[0, i * GRID_W:(i + 1) * GRID_W, :]
        zq = jnp.zeros_like(qr)
        qrs = jnp.concatenate([jnp.where(lane < NA_HD, qr, zq), jnp.where(lane < NA_HD, zq, qr)], axis=0)
        qns = jnp.concatenate([jnp.where(lane < NA_HD, qn, zq), jnp.where(lane < NA_HD, zq, qn)], axis=0)
        bias = jnp.concatenate([tab_ref[0, off], tab_ref[1, off]], axis=0)
        row0 = 2 * i * GRID_W
        s_scr[row0:row0 + 2 * GRID_W, 0:win] = _nt(qrs, kw) * scale + bias
        s_scr[row0:row0 + 2 * GRID_W, win:win + ctx_len] = _nt(qns, kcx) * scale

    def softmax_rows(c, carry):
        r0 = pl.multiple_of(c * LANES, LANES)
        s = s_scr[pl.ds(r0, LANES), :]
        p = jnp.exp(s - jnp.max(s, axis=-1, keepdims=True))
        inv = 1.0 / jnp.sum(p, axis=-1, keepdims=True)
        p_scr[pl.ds(r0, LANES), :] = (p * inv).astype(bf16)
        return carry

    lax.fori_loop(0, 2 * rows_per_step * GRID_W // LANES, softmax_rows, 0, unroll=2)

    for i in range(rows_per_step):
        vw = v_ref[0, pl.ds(starts[i], win), :]
        row0 = 2 * i * GRID_W
        p = p_scr[row0:row0 + 2 * GRID_W, :]
        res = (jnp.dot(p[:, :win], vw, preferred_element_type=f32)
               + jnp.dot(p[:, win:], vcx, preferred_element_type=f32))
        o_ref[0, i * GRID_W:(i + 1) * GRID_W, :] = jnp.where(lane < NA_HD, res[:GRID_W], res[GRID_W:]).astype(o_ref.dtype)


def _na_bias_table(rpb):
    col = jnp.arange(GRID_W)
    cs = jnp.clip(col - NA_WIN_C // 2, 0, GRID_W - NA_WIN_C)
    kc = jnp.arange(GRID_W)
    valid = (kc[None, :] >= cs[:, None]) & (kc[None, :] < cs[:, None] + NA_WIN_C)
    dc = jnp.clip(kc[None, :] - col[:, None] + (NA_WIN_C - 1), 0, 2 * NA_WIN_C - 2)
    bc = jnp.where(valid[None, None], rpb[:, :, dc], NEG_BIG)
    t2 = jnp.stack([bc[:, o:o + NA_WIN_R] for o in range(NA_WIN_R)], axis=1)
    t2 = t2.transpose(0, 1, 3, 2, 4)
    return t2.reshape(NA_HEADS, NA_WIN_R, GRID_W, NA_WIN_R * GRID_W).astype(f32)


def neighbourhood_attention(q_rot, qn, k_rot, v, kc, vc, table):
    B, N, W = q_rot.shape
    n_rows = N // GRID_W
    rps = 8
    ctx_len = kc.shape[1]
    pairs = W // LANES
    keys = NA_WIN_R * GRID_W + ctx_len
    return pl.pallas_call(
        functools.partial(_na_kernel, rows_per_step=rps, n_rows=n_rows),
        grid=(B, pairs, n_rows // rps),
        in_specs=[pl.BlockSpec((1, rps * GRID_W, LANES), lambda b, p, g: (b, g, p)),
                  pl.BlockSpec((1, rps * GRID_W, LANES), lambda b, p, g: (b, g, p)),
                  pl.BlockSpec((1, N, LANES), lambda b, p, g: (b, 0, p)),
                  pl.BlockSpec((1, N, LANES), lambda b, p, g: (b, 0, p)),
                  pl.BlockSpec((1, ctx_len, LANES), lambda b, p, g: (b, 0, p)),
                  pl.BlockSpec((1, ctx_len, LANES), lambda b, p, g: (b, 0, p)),
                  pl.BlockSpec((2, NA_WIN_R, GRID_W, NA_WIN_R * GRID_W), lambda b, p, g: (p, 0, 0, 0))],
        out_specs=pl.BlockSpec((1, rps * GRID_W, LANES), lambda b, p, g: (b, g, p)),
        out_shape=jax.ShapeDtypeStruct((B, N, W), bf16),
        scratch_shapes=[pltpu.VMEM((2 * rps * GRID_W, keys), f32), pltpu.VMEM((2 * rps * GRID_W, keys), bf16)],
        compiler_params=_cparams(("parallel", "parallel", "arbitrary")),
        name="natten",
    )(q_rot, qn, k_rot, v, kc, vc, table)


def _ctx_attn_kernel(q_ref, k_ref, v_ref, o_ref):
    scale = NA_HD ** -0.5
    q = q_ref[0]
    k = k_ref[0]
    v = v_ref[0]
    lane = lax.broadcasted_iota(i32, q.shape, 1)
    res = []
    for hh in range(2):
        m = (lane >= NA_HD * hh) & (lane < NA_HD * (hh + 1))
        s = _nt(jnp.where(m, q, jnp.zeros_like(q)), k) * scale
        p = jnp.exp(s - jnp.max(s, axis=-1, keepdims=True))
        p = p / jnp.sum(p, axis=-1, keepdims=True)
        res.append(jnp.dot(p.astype(bf16), v, preferred_element_type=f32))
    o_ref[0] = jnp.where(lane < NA_HD, res[0], res[1]).astype(o_ref.dtype)


def context_attention(q, k, v):
    B, N, W = q.shape
    spec = pl.BlockSpec((1, N, LANES), lambda b, p: (b, 0, p))
    return pl.pallas_call(
        _ctx_attn_kernel,
        grid=(B, W // LANES),
        in_specs=[spec, spec, spec],
        out_specs=spec,
        out_shape=jax.ShapeDtypeStruct((B, N, W), bf16),
        compiler_params=_cparams(("parallel", "parallel")),
        name="ctx_attn",
    )(q, k, v)


def _filter_kernel(z_ref, w1_ref, b1_ref, f1_ref, w2_ref, b2_ref, f2_ref, w3_ref, dl_ref, k_ref, nrm_ref, *, tm, n):
    i = pl.program_id(0)
    z = z_ref[...]
    hid = HY_FILT_HID
    dot = lambda a, b: jnp.dot(a, b, precision=HI, preferred_element_type=f32)
    w1 = w1_ref[...]
    a = jnp.sin(f1_ref[...] * (dot(z[:tm // 2], w1[:, :LANES]) + dot(z[tm // 2:], w1[:, LANES:]) + b1_ref[...]))
    a = jnp.sin(f2_ref[...] * (dot(a, w2_ref[...]) + b2_ref[...]))
    w3 = w3_ref[...]
    zero = jnp.zeros_like(w3)
    h = jnp.concatenate([dot(a, jnp.concatenate([w3, zero], axis=0)), dot(a, jnp.concatenate([zero, w3], axis=0))],
                        axis=0)
    h = h * jnp.exp(-z[:, 0:1] * dl_ref[...])
    row = lax.broadcasted_iota(i32, (tm, HY_W), 0) + i * tm
    k = jnp.where(row == n, 0.0, h)
    k_ref[...] = k
    part = jnp.sum(jnp.abs(k), axis=0, keepdims=True)

    @pl.when(i == 0)
    def _():
        nrm_ref[...] = part

    @pl.when(i > 0)
    def _():
        nrm_ref[...] = nrm_ref[...] + part


def hyena_filter(n, w1, b1, fr1, w2, b2, fr2, w3):
    t = np.linspace(0.0, 1.0, n)[:, None]
    w = 2 * math.pi * np.arange(n)[:, None] / n
    fb = np.linspace(1e-4, HY_BANDS - 1, HY_BANDS)[None]
    z = np.concatenate([t, np.cos(fb * w), -np.sin(fb * w)], axis=-1)
    z = np.concatenate([z, np.zeros((1, HY_PE_DIM)), z[:0:-1]], axis=0)
    z = jnp.asarray(np.pad(z, ((0, 0), (0, LANES - HY_PE_DIM))).astype(np.float32))
    w1p = jnp.pad(w1.astype(f32), ((0, LANES - HY_PE_DIM), (0, 0)))
    deltas = jnp.asarray(np.abs(np.linspace(math.log(HY_TARGET) / HY_SLOW_DECAY, math.log(HY_TARGET) / HY_FAST_DECAY,
                                            2 * HY_W))[None].astype(np.float32))
    tm = min(n, 512)
    hid = HY_FILT_HID
    full = lambda shape: pl.BlockSpec(shape, lambda i: (0, 0))
    tph = n // tm
    zpad = jnp.zeros((LANES, hid), f32)
    w1pk = jnp.concatenate([w1p, zpad, zpad, w1p], axis=1)
    w2f = w2.astype(f32)
    z2 = jnp.zeros((hid, hid), f32)
    w2pk = jnp.concatenate([jnp.concatenate([w2f, z2], axis=1), jnp.concatenate([z2, w2f], axis=1)], axis=0)
    twice = lambda v: jnp.tile(v.reshape(1, hid).astype(f32), (1, 2))
    return pl.pallas_call(
        functools.partial(_filter_kernel, tm=tm, n=n),
        grid=(2 * n // tm,),
        in_specs=[pl.BlockSpec((tm, LANES), lambda i: (i, 0)),
                  full((LANES, 2 * LANES)), full((1, LANES)), full((1, LANES)),
                  full((LANES, LANES)), full((1, LANES)), full((1, LANES)),
                  pl.BlockSpec((hid, HY_W), lambda i: (0, i // tph)),
                  pl.BlockSpec((1, HY_W), lambda i: (0, i // tph))],
        out_specs=[pl.BlockSpec((tm, HY_W), lambda i: (i, 0)), pl.BlockSpec((1, HY_W), lambda i: (0, 0))],
        out_shape=[jax.ShapeDtypeStruct((2 * n, HY_W), f32), jax.ShapeDtypeStruct((1, HY_W), f32)],
        compiler_params=_cparams(("arbitrary",)),
        name="hyena_filter",
    )(z, w1pk, twice(b1), twice(fr1), w2pk, twice(b2), twice(fr2), w3.astype(f32), deltas)


def _conv3(u, w_ref, b_ref):
    n = u.shape[0]
    row = lax.broadcasted_iota(i32, u.shape, 0)
    prev = jnp.where(row == 0, 0.0, pltpu.roll(u, 1, axis=0))
    nxt = jnp.where(row == n - 1, 0.0, pltpu.roll(u, n - 1, axis=0))
    return prev * w_ref[0:1, :] + u * w_ref[1:2, :] + nxt * w_ref[2:3, :] + b_ref[...]


def _hyena_pre_kernel(p0_ref, p1_ref, p2_ref, w0_ref, w1_ref, w2_ref, b0_ref, b1_ref, b2_ref, z_ref, x0_ref):
    x0_ref[0] = _conv3(p0_ref[0], w0_ref, b0_ref).astype(x0_ref.dtype)
    z_ref[0] = _conv3(p1_ref[0], w1_ref, b1_ref) * _conv3(p2_ref[0], w2_ref, b2_ref)


def hyena_pre(p, conv_w, conv_b):
    B, N, _ = p.shape
    nb = HY_W // LANES
    conv_b = conv_b.reshape(1, 3 * HY_W)
    pspec = lambda g: pl.BlockSpec((1, N, LANES), lambda b, c: (b, 0, g * nb + c))
    wspec = lambda g: pl.BlockSpec((3, LANES), lambda b, c: (0, g * nb + c))
    bspec = lambda g: pl.BlockSpec((1, LANES), lambda b, c: (0, g * nb + c))
    ospec = pl.BlockSpec((1, N, LANES), lambda b, c: (b, 0, c))
    return pl.pallas_call(
        _hyena_pre_kernel,
        grid=(B, nb),
        in_specs=[pspec(0), pspec(1), pspec(2), wspec(0), wspec(1), wspec(2), bspec(0), bspec(1), bspec(2)],
        out_specs=[ospec, ospec],
        out_shape=[jax.ShapeDtypeStruct((B, N, HY_W), f32), jax.ShapeDtypeStruct((B, N, HY_W), bf16)],
        compiler_params=_cparams(("parallel", "parallel")),
        name="hyena_pre",
    )(p, p, p, conv_w, conv_w, conv_w, conv_b, conv_b, conv_b)


DFT_SLABS = 8


def _hl(a):
    a32 = jnp.asarray(a.astype(np.float32))
    hi = a32.astype(bf16)
    lo = (a32 - hi.astype(f32)).astype(bf16)
    return jnp.concatenate([hi, lo], axis=-2)


def _dot3(a_hl, m, x):
    xh, xl = _split(x)
    r = jnp.dot(a_hl, xh, preferred_element_type=f32)
    return r[:m] + r[m:] + jnp.dot(a_hl[:m], xl, preferred_element_type=f32)


def _dft_consts(n):
    N = 2 * n
    na = N // LANES
    t1n = na // 2
    k1n = na // 2 + 1
    k1p = -(-k1n // 8) * 8
    k1 = np.arange(k1n)
    t1 = np.arange(t1n)
    th = 2 * np.pi * ((t1[None, :] * k1[:, None]) % na) / na
    f1c = np.zeros((2 * k1p, t1n))
    f1c[:k1n] = np.cos(th)
    f1c[k1p:k1p + k1n] = -np.sin(th)
    thf = 2 * np.pi * ((np.arange(na)[None, :] * k1[:, None]) % na) / na
    f1f = np.zeros((2 * k1p, na))
    f1f[:k1n] = np.cos(thf)
    f1f[k1p:k1p + k1n] = -np.sin(thf)
    k2 = np.arange(LANES)
    t2 = np.arange(LANES)
    m = (t2[None, None, :] * (k1[:, None, None] + na * k2[None, :, None])) % N
    ph = 2 * np.pi * m / N
    g = np.concatenate([np.cos(ph), -np.sin(ph)], axis=1)
    pht = ph.transpose(0, 2, 1)
    gi = np.concatenate([np.cos(pht), np.sin(pht)], axis=1)
    wk = np.where((k1 == 0) | (k1 == na // 2), 1.0, 2.0) / N
    f1i = np.zeros((t1n, 2 * k1p))
    f1i[:, :k1n] = np.cos(th.T) * wk[None, :]
    f1i[:, k1p:k1p + k1n] = -np.sin(th.T) * wk[None, :]
    k1e = -(-k1n // DFT_SLABS) * DFT_SLABS
    g = np.concatenate([g, np.zeros((k1e - k1n,) + g.shape[1:])], axis=0)
    gi = np.concatenate([gi, np.zeros((k1e - k1n,) + gi.shape[1:])], axis=0)
    return dict(na=na, t1n=t1n, k1n=k1n, k1e=k1e, k1p=k1p, f1c=_hl(f1c), f1f=_hl(f1f), g=_hl(g), gi=_hl(gi),
                f1i=_hl(f1i))


def _dft_stage1(src_ref, f1c_ref, tre_ref, tim_ref, t1n, k1p):
    f1c = f1c_ref[...]

    def body(t2, carry):
        zs = src_ref[pl.ds(t2, t1n, stride=LANES), :]
        r = _dot3(f1c, 2 * k1p, zs)
        r0 = pl.multiple_of(t2 * k1p, 8)
        tre_ref[pl.ds(r0, k1p), :] = r[:k1p]
        tim_ref[pl.ds(r0, k1p), :] = r[k1p:]
        return carry

    lax.fori_loop(0, LANES, body, 0, unroll=4)


def _slab(tre_ref, tim_ref, k1, k1p):
    return tre_ref[pl.ds(k1, LANES, stride=k1p), :], tim_ref[pl.ds(k1, LANES, stride=k1p), :]


def _cplx_left(gc_hl, xre, xim):
    cw = xre.shape[1]
    r = _dot3(gc_hl, 2 * LANES, jnp.concatenate([xre, xim], axis=1))
    p, q = r[:, :cw], r[:, cw:]
    return p[:LANES] - q[LANES:], p[LANES:] + q[:LANES]


def _spectrum_kernel(k_ref, f1f_ref, g_ref, inv_ref, xre_ref, xim_ref, are, aim, *, na, k1p):
    j = pl.program_id(1)

    @pl.when(j == 0)
    def _():
        _dft_stage1(k_ref, f1f_ref, are, aim, na, k1p)

    for half in range(DFT_SLABS):
        xre, xim = _cplx_left(g_ref[half], *_slab(are, aim, DFT_SLABS * j + half, k1p))
        rows = slice(half * LANES, (half + 1) * LANES)
        xre_ref[rows, :] = xre * inv_ref[...]
        xim_ref[rows, :] = xim * inv_ref[...]


def hyena_spectrum(k, inv_norm, dc):
    n2, C = k.shape
    k1e, k1p, na = dc["k1e"], dc["k1p"], dc["na"]
    cw = LANES
    out = jax.ShapeDtypeStruct((k1e * LANES, C), f32)
    ospec = pl.BlockSpec((DFT_SLABS * LANES, cw), lambda c, k: (k, c))
    return pl.pallas_call(
        functools.partial(_spectrum_kernel, na=na, k1p=k1p),
        grid=(C // cw, k1e // DFT_SLABS),
        in_specs=[pl.BlockSpec((n2, cw), lambda c, k: (0, c)),
                  pl.BlockSpec(dc["f1f"].shape, lambda c, k: (0, 0)),
                  pl.BlockSpec((DFT_SLABS, 4 * LANES, LANES), lambda c, k: (k, 0, 0)),
                  pl.BlockSpec((1, cw), lambda c, k: (0, c))],
        out_specs=[ospec, ospec],
        out_shape=[out, out],
        scratch_shapes=[pltpu.VMEM((k1p * LANES, cw), f32), pltpu.VMEM((k1p * LANES, cw), f32)],
        compiler_params=_cparams(("parallel", "arbitrary")),
        name="hyena_spectrum",
    )(k, dc["f1f"], dc["g"], inv_norm)


def _hyena_conv_kernel(z_ref, x0_ref, f1c_ref, g_ref, gi_ref, f1i_ref, kre_ref, kim_ref,
                       skip_ref, o_ref, tre, tim, are, aim, y_scr, *, t1n, k1p):
    j = pl.program_id(2)

    @pl.when(j == 0)
    def _():
        _dft_stage1(z_ref.at[0], f1c_ref, tre, tim, t1n, k1p)

    for half in range(DFT_SLABS):
        r0 = pl.multiple_of((DFT_SLABS * j + half) * LANES, LANES)
        xre, xim = _cplx_left(g_ref[half], *_slab(tre, tim, DFT_SLABS * j + half, k1p))
        rows = slice(half * LANES, (half + 1) * LANES)
        kre = kre_ref[rows, :]
        kim = kim_ref[rows, :]
        yre = xre * kre - xim * kim
        yim = xre * kim + xim * kre
        bre, bim = _cplx_left(gi_ref[half], yre, yim)
        are[pl.ds(r0, LANES), :] = bre
        aim[pl.ds(r0, LANES), :] = bim

    @pl.when(j == pl.num_programs(2) - 1)
    def _():
        f1i = f1i_ref[...]

        def body(t2, carry):
            bb = jnp.concatenate([are[pl.ds(t2, k1p, stride=LANES), :], aim[pl.ds(t2, k1p, stride=LANES), :]], axis=0)
            y_scr[pl.ds(t2, t1n, stride=LANES), :] = _dot3(f1i, t1n, bb)
            return carry

        lax.fori_loop(0, LANES, body, 0, unroll=4)
        z = z_ref[0]
        o_ref[0] = (x0_ref[0].astype(f32) * (y_scr[...] + z * skip_ref[...])).astype(o_ref.dtype)


def hyena_conv(z, x0, spec_re, spec_im, skip, dc):
    B, n, W = z.shape
    nb = W // LANES
    k1e, k1p, t1n = dc["k1e"], dc["k1p"], dc["t1n"]
    seq = pl.BlockSpec((1, n, LANES), lambda b, c, k: (b, 0, c))
    fspec = pl.BlockSpec((DFT_SLABS * LANES, LANES), lambda b, c, k: (k, c))
    cspec = pl.BlockSpec((DFT_SLABS, 4 * LANES, LANES), lambda b, c, k: (k, 0, 0))
    vspec = pl.BlockSpec((1, LANES), lambda b, c, k: (0, c))
    return pl.pallas_call(
        functools.partial(_hyena_conv_kernel, t1n=t1n, k1p=k1p),
        grid=(B, nb, k1e // DFT_SLABS),
        in_specs=[seq, seq,
                  pl.BlockSpec(dc["f1c"].shape, lambda b, c, k: (0, 0)), cspec, cspec,
                  pl.BlockSpec(dc["f1i"].shape, lambda b, c, k: (0, 0)),
                  fspec, fspec, vspec],
        out_specs=seq,
        out_shape=jax.ShapeDtypeStruct((B, n, W), bf16),
        scratch_shapes=[pltpu.VMEM((k1p * LANES, LANES), f32)] * 4 + [pltpu.VMEM((n, LANES), f32)],
        compiler_params=_cparams(("parallel", "parallel", "arbitrary")),
        name="hyena_conv",
    )(z, x0, dc["f1c"], dc["g"], dc["gi"], dc["f1i"], spec_re, spec_im, skip)


def _hyena_ctx_kernel(p0_ref, p1_ref, p2_ref, w0_ref, w1_ref, w2_ref, b0_ref, b1_ref, b2_ref,
                      k_ref, inv_ref, skip_ref, fd_ref, fi_ref, o_ref, *, n):
    x0 = _conv3(p0_ref[0], w0_ref, b0_ref)
    z = _conv3(p1_ref[0], w1_ref, b1_ref) * _conv3(p2_ref[0], w2_ref, b2_ref)
    fd = fd_ref[...]
    N = 2 * n
    zf = jnp.dot(fd[:, :n], z, precision=HI, preferred_element_type=f32)
    kf = jnp.dot(fd, k_ref[...], precision=HI, preferred_element_type=f32) * inv_ref[...]
    yre = zf[:N] * kf[:N] - zf[N:] * kf[N:]
    yim = zf[:N] * kf[N:] + zf[N:] * kf[:N]
    y = jnp.dot(fi_ref[...], jnp.concatenate([yre, yim], axis=0), precision=HI, preferred_element_type=f32)
    o_ref[0] = (x0 * (y + z * skip_ref[...])).astype(o_ref.dtype)


def hyena_ctx(p, conv_w, conv_b, k, inv_norm, skip):
    B, n, _ = p.shape
    N = 2 * n
    nb = HY_W // LANES
    kk = np.arange(N)
    ph = 2 * np.pi * ((kk[:, None] * kk[None, :]) % N) / N
    fd = jnp.asarray(np.concatenate([np.cos(ph), -np.sin(ph)], axis=0).astype(np.float32))
    fi = jnp.asarray((np.concatenate([np.cos(ph[:n]), -np.sin(ph[:n])], axis=1) / N).astype(np.float32))
    conv_b = conv_b.reshape(1, 3 * HY_W)
    pspec = lambda g: pl.BlockSpec((1, n, LANES), lambda b, c: (b, 0, g * nb + c))
    wspec = lambda g: pl.BlockSpec((3, LANES), lambda b, c: (0, g * nb + c))
    bspec = lambda g: pl.BlockSpec((1, LANES), lambda b, c: (0, g * nb + c))
    vspec = pl.BlockSpec((1, LANES), lambda b, c: (0, c))
    return pl.pallas_call(
        functools.partial(_hyena_ctx_kernel, n=n),
        grid=(B, nb),
        in_specs=[pspec(0), pspec(1), pspec(2), wspec(0), wspec(1), wspec(2), bspec(0), bspec(1), bspec(2),
                  pl.BlockSpec((N, LANES), lambda b, c: (0, c)),
                  vspec, vspec,
                  pl.BlockSpec(fd.shape, lambda b, c: (0, 0)), pl.BlockSpec(fi.shape, lambda b, c: (0, 0))],
        out_specs=pl.BlockSpec((1, n, LANES), lambda b, c: (b, 0, c)),
        out_shape=jax.ShapeDtypeStruct((B, n, HY_W), bf16),
        compiler_params=_cparams(("parallel", "parallel")),
        name="hyena_ctx",
    )(p, p, p, conv_w, conv_w, conv_w, conv_b, conv_b, conv_b, k, inv_norm, skip, fd, fi)


def _merge_kernel(of_ref, ob_ref, gs_ref, nb_ref, hc_ref, g_ref, wa_ref, wb_ref, wc_ref, wo_ref, x_ref, m_ref, o_ref):
    d = D_MODEL
    tot = of_ref[...] + ob_ref[...]
    gs = gs_ref[...].astype(f32)
    ra = []
    for h in range(HG_HEADS):
        sl = slice(LANES * h, LANES * (h + 1))
        th = tot[:, sl]
        ms = jnp.mean(th * th, axis=-1, keepdims=True)
        ra.append(th * lax.rsqrt(ms + EPS) * gs[:, sl])
    ya = jnp.dot(jnp.concatenate(ra, axis=1).astype(bf16), wa_ref[...], preferred_element_type=f32)
    yb = jnp.dot(nb_ref[...], wb_ref[...], preferred_element_type=f32)
    yc = jnp.dot(hc_ref[...], wc_ref[...], preferred_element_type=f32)
    g = g_ref[...].astype(f32)
    mix = g[:, :d] * ya + g[:, d:2 * d] * yb + g[:, 2 * d:] * yc
    y = jnp.dot(mix.astype(bf16), wo_ref[...], preferred_element_type=f32)
    o_ref[...] = x_ref[...] + m_ref[0] * y


def merge(o_f, o_b, gs, nb, hc, gates, wa, wb, wc, wo, x2d, m, rows_per_group):
    R = x2d.shape[0]
    tm = 512
    tpg = rows_per_group // tm
    G = m.shape[0]
    row = lambda w: pl.BlockSpec((tm, w), lambda i: (i, 0))
    full = lambda a: pl.BlockSpec(a.shape, lambda i: (0, 0))
    return pl.pallas_call(
        _merge_kernel,
        grid=(R // tm,),
        in_specs=[row(HG_W), row(HG_W), row(HG_W), row(NA_W), row(HY_W), row(3 * D_MODEL),
                  full(wa), full(wb), full(wc), full(wo),
                  row(D_MODEL), pl.BlockSpec((1, 1, D_MODEL), lambda i: (i // tpg, 0, 0))],
        out_specs=row(D_MODEL),
        out_shape=jax.ShapeDtypeStruct((R, D_MODEL), f32),
        compiler_params=_cparams(("parallel",)),
        name="merge",
    )(o_f, o_b, gs, nb, hc, gates, wa, wb, wc, wo, x2d, m.reshape(G, 1, D_MODEL))


def _router_kernel(x_ref, g_ref, sh_ref, sc_ref, wrt_ref, wr_ref, h_ref, at_ref, am_ref):
    x = x_ref[...]
    ms = jnp.mean(x * x, axis=-1, keepdims=True)
    h = x * lax.rsqrt(ms + EPS) * g_ref[...] * (1.0 + sc_ref[0]) + sh_ref[0]
    h_ref[...] = h.astype(h_ref.dtype)
    hh, hl = _split(h)
    ne = N_EXPERTS
    wt = wrt_ref[...]
    rt = _nt(wt, hh)
    lt = rt[:ne] + rt[ne:] + _nt(wt[:ne], hl)
    et = jnp.exp(lt - jnp.max(lt, axis=0, keepdims=True))
    at_ref[0] = et / jnp.sum(et, axis=0, keepdims=True)
    wm = wr_ref[...]
    rm = jnp.dot(hh, wm, preferred_element_type=f32)
    lm = rm[:, :ne] + rm[:, ne:] + jnp.dot(hl, wm[:, :ne], preferred_element_type=f32)
    em = jnp.exp(lm - jnp.max(lm, axis=1, keepdims=True))
    am_ref[...] = em / jnp.sum(em, axis=1, keepdims=True)


def router(x2d, g, shift, scale, w_router, n_per_set):
    R = x2d.shape[0]
    tm = min(512, n_per_set)
    tps = n_per_set // tm
    S = R // n_per_set
    G = shift.shape[0]
    gmap = (lambda i: (i // tps, 0, 0)) if G > 1 else (lambda i: (0, 0, 0))
    whi, wlo = _split(w_router.astype(f32))
    wr = jnp.concatenate([whi, wlo], axis=1)
    return pl.pallas_call(
        _router_kernel,
        grid=(R // tm,),
        in_specs=[pl.BlockSpec((tm, D_MODEL), lambda i: (i, 0)),
                  pl.BlockSpec((1, D_MODEL), lambda i: (0, 0)),
                  pl.BlockSpec((1, 1, D_MODEL), gmap),
                  pl.BlockSpec((1, 1, D_MODEL), gmap),
                  pl.BlockSpec((2 * N_EXPERTS, D_MODEL), lambda i: (0, 0)),
                  pl.BlockSpec((D_MODEL, 2 * N_EXPERTS), lambda i: (0, 0))],
        out_specs=[pl.BlockSpec((tm, D_MODEL), lambda i: (i, 0)),
                   pl.BlockSpec((1, N_EXPERTS, tm), lambda i: (i // tps, 0, i % tps)),
                   pl.BlockSpec((tm, N_EXPERTS), lambda i: (i, 0))],
        out_shape=[jax.ShapeDtypeStruct((R, D_MODEL), bf16),
                   jax.ShapeDtypeStruct((S, N_EXPERTS, n_per_set), f32),
                   jax.ShapeDtypeStruct((R, N_EXPERTS), f32)],
        compiler_params=_cparams(("parallel",)),
        name="router",
    )(x2d, g.reshape(1, D_MODEL), shift.reshape(G, 1, D_MODEL), scale.reshape(G, 1, D_MODEL), wr.T, wr)


SEL_BLK = 256
SUB = LANES
SUBW = SUB + 8
UNSEL = -float(2 ** 30)


def _prefix_incl(mask_f, tri, T):
    outs = []
    off = jnp.zeros((mask_f.shape[0], 1), f32)
    for b in range(T // SEL_BLK):
        blk = mask_f[:, b * SEL_BLK:(b + 1) * SEL_BLK].astype(bf16)
        pre = jnp.dot(blk, tri, preferred_element_type=f32) + off
        outs.append(pre)
        off = pre[:, SEL_BLK - 1:SEL_BLK]
    return jnp.concatenate(outs, axis=1)


def _select_kernel(a_ref, tri_ref, cm_ref, posm_ref, cnt_ref, *, T, cap):
    aff = a_ref[0]
    bits = pltpu.bitcast(aff, i32)
    tri = tri_ref[...]

    def bit_step(i, thr):
        cand = thr | (1 << (30 - i))
        cnt = jnp.sum((bits >= cand).astype(f32), axis=1, keepdims=True)
        return jnp.where(cnt >= cap, cand, thr)

    thr = lax.fori_loop(0, 31, bit_step, jnp.zeros((N_EXPERTS, 1), i32))
    gt = bits > thr
    eq = bits == thr
    need = cap - jnp.sum(gt.astype(f32), axis=1, keepdims=True)
    eqf = eq.astype(f32)
    rank_eq = _prefix_incl(eqf, tri, T) - eqf
    sel = gt | (eq & (rank_eq < need))
    self_ = sel.astype(f32)
    pos = _prefix_incl(self_, tri, T) - self_
    posm_ref[0] = jnp.where(sel, pos, UNSEL)
    cnt_ref[0] = jnp.dot(self_.astype(bf16), cm_ref[...], preferred_element_type=f32).astype(i32)


def select_topk(aff, cap):
    S, E, T = aff.shape
    tri = jnp.asarray(np.triu(np.ones((SEL_BLK, SEL_BLK), np.float32)), bf16)
    cm = jnp.asarray((np.arange(T)[:, None] < np.arange(LANES)[None, :] * SUB).astype(np.float32), bf16)
    return pl.pallas_call(
        functools.partial(_select_kernel, T=T, cap=cap),
        grid=(S,),
        in_specs=[pl.BlockSpec((1, E, T), lambda s: (s, 0, 0)),
                  pl.BlockSpec((SEL_BLK, SEL_BLK), lambda s: (0, 0)),
                  pl.BlockSpec((T, LANES), lambda s: (0, 0))],
        out_specs=[pl.BlockSpec((1, E, T), lambda s: (s, 0, 0)),
                   pl.BlockSpec((1, E, LANES), lambda s: (s, 0, 0))],
        out_shape=[jax.ShapeDtypeStruct((S, E, T), f32), jax.ShapeDtypeStruct((S, E, LANES), i32)],
        compiler_params=_cparams(("parallel",)),
        name="select_topk",
    )(aff, tri, cm)


def _align8(v):
    return lax.shift_left(lax.shift_right_logical(v, 3), 3)


def _align16(v):
    return lax.shift_left(lax.shift_right_logical(v, 4), 4)


CMB_ROWS = SUB + 16


def _gather_kernel(cnt_ref, h_ref, pos_ref, o_ref, acc, *, TT, cap, tps, R, srows, EP):
    tl = pl.program_id(1)

    @pl.when(tl == 0)
    def _():
        acc[...] = jnp.zeros_like(acc)

    st = tl // tps
    nsub = TT // SUB
    rid = lax.broadcasted_iota(i32, (srows, SUB), 0).astype(f32)
    for ep in range(EP):
        e = pl.program_id(0) * EP + ep
        cbase = (st * N_EXPERTS + e) * LANES + (tl % tps) * nsub
        for s in range(nsub):
            off8 = _align8(cnt_ref[cbase + s])
            pos = pos_ref[0, ep, :, s * SUB:(s + 1) * SUB]
            onehot = jnp.where(pos == rid + off8.astype(f32), 1.0, 0.0).astype(bf16)
            rows = jnp.dot(onehot, h_ref[s * SUB:(s + 1) * SUB, :], preferred_element_type=f32)
            r0 = pl.multiple_of(st * cap + off8, 8)
            acc[ep, pl.ds(r0, srows), :] += rows

    @pl.when(tl == pl.num_programs(1) - 1)
    def _():
        o_ref[...] = acc[:, 0:R, :].astype(o_ref.dtype)


def _gather_call(cnt, h, posm, cap, TT, srows, EP):
    S, E, T = posm.shape
    tps = T // TT
    R = S * cap
    gs = pltpu.PrefetchScalarGridSpec(
        num_scalar_prefetch=1,
        grid=(E // EP, S * tps),
        in_specs=[pl.BlockSpec((TT, D_MODEL), lambda e, t, c: (t, 0)),
                  pl.BlockSpec((1, EP, 1, TT), lambda e, t, c: (t // tps, e, 0, t % tps))],
        out_specs=pl.BlockSpec((EP, R, D_MODEL), lambda e, t, c: (e, 0, 0)),
        scratch_shapes=[pltpu.VMEM((EP, R + srows, D_MODEL), f32)])
    return pl.pallas_call(
        functools.partial(_gather_kernel, TT=TT, cap=cap, tps=tps, R=R, srows=srows, EP=EP),
        grid_spec=gs,
        out_shape=jax.ShapeDtypeStruct((E, R, D_MODEL), bf16),
        compiler_params=_cparams(("parallel", "arbitrary")),
        name="moe_gather",
    )(cnt.reshape(-1), h, posm.reshape(S, E, 1, T))


FAST_SUB_MAX = 48
FAST_TILE_MAX = 112
FAST_ROWS = 64
CMB_TILE = 512


def _fits_fast(cnt, T):
    nsub = T // SUB
    per_sub = cnt[..., 1:nsub + 1] - cnt[..., :nsub]
    k = CMB_TILE // SUB
    per_tile = cnt[..., k:nsub + 1:k] - cnt[..., 0:nsub:k]
    return (jnp.max(per_sub) <= FAST_SUB_MAX) & (jnp.max(per_tile) <= FAST_TILE_MAX)


def gather_rows(cnt, h, posm, cap, TT, fast_ok=None):
    safe = lambda: _gather_call(cnt, h, posm, cap, TT, SUBW, 1)
    if fast_ok is None:
        return safe()
    return lax.cond(fast_ok, lambda: _gather_call(cnt, h, posm, cap, TT, FAST_ROWS, 2), safe)


EXPERT_TF = 256


def _ffn_kernel(*refs, n):
    xs = refs[:n]
    wg_ref, wu_ref, wd_ref = refs[n:n + 3]
    his = refs[n + 3:2 * n + 3]
    los = refs[2 * n + 3:3 * n + 3]
    accs = refs[3 * n + 3:]
    j = pl.program_id(1)
    wg = wg_ref[0, 0].astype(bf16)
    wu = wu_ref[0, 0].astype(bf16)
    wd = wd_ref[0, 0].astype(bf16)
    @pl.when(j == 0)
    def _():
        for acc in accs:
            acc[...] = jnp.zeros_like(acc)

    for x_ref, hi_ref, lo_ref, acc in zip(xs, his, los, accs):
        x = x_ref[0]
        a = jnp.dot(x, wg, preferred_element_type=f32)
        u = jnp.dot(x, wu, preferred_element_type=f32)
        acc[...] += jnp.dot((_silu(a) * u).astype(bf16), wd, preferred_element_type=f32)

        @pl.when(j == pl.num_programs(1) - 1)
        def _(acc=acc, hi_ref=hi_ref, lo_ref=lo_ref):
            hi, lo = _split(acc[...])
            hi_ref[0] = hi
            lo_ref[0] = lo


def expert_ffn(xgs, layer, w_gate, w_up, w_down):
    E = xgs[0].shape[0]
    nf = D_FF_EXPERT // EXPERT_TF
    n = len(xgs)
    rowspec = lambda a: pl.BlockSpec((1, a.shape[1], D_MODEL), lambda e, j: (e, 0, 0))
    res = pl.pallas_call(
        functools.partial(_ffn_kernel, n=n),
        grid=(E, nf),
        in_specs=[rowspec(a) for a in xgs] + [
            pl.BlockSpec((1, 1, D_MODEL, EXPERT_TF), lambda e, j: (layer, e, 0, j)),
            pl.BlockSpec((1, 1, D_MODEL, EXPERT_TF), lambda e, j: (layer, e, 0, j)),
            pl.BlockSpec((1, 1, EXPERT_TF, D_MODEL), lambda e, j: (layer, e, j, 0))],
        out_specs=[rowspec(a) for a in xgs] * 2,
        out_shape=[jax.ShapeDtypeStruct(a.shape, bf16) for a in xgs] * 2,
        scratch_shapes=[pltpu.VMEM(a.shape[1:], f32) for a in xgs],
        compiler_params=_cparams(("parallel", "arbitrary")),
        name="expert_ffn",
    )(*xgs, w_gate, w_up, w_down)
    return [(res[i], res[n + i]) for i in range(n)]


def _combine_kernel(cnt_ref, x_ref, pos_ref, am_ref, m_ref, *rest, TT, cap, R, W, ytot, crows, EP):
    y_refs, o_ref = rest[:2 * EP], rest[2 * EP]
    st = pl.program_id(0)
    tl = pl.program_id(1)
    eg = pl.program_id(2)

    @pl.when(eg == 0)
    def _():
        o_ref[...] = x_ref[...]

    nsub = TT // SUB
    lane = lax.broadcasted_iota(i32, (TT, N_EXPERTS), 1)
    m5 = m_ref[0]
    rid = lax.broadcasted_iota(i32, (crows, SUB), 0).astype(f32)
    am = am_ref[...]
    gcols, wss = [], []
    for ep in range(EP):
        e = eg * EP + ep
        cbase = (st * N_EXPERTS + e) * LANES + tl * nsub
        wss.append(jnp.minimum(e * R + st * cap + _align16(cnt_ref[cbase]), ytot - W))
        gcols.append(jnp.sum(jnp.where(lane == e, am, 0.0), axis=1, keepdims=True))
    for s in range(nsub):
        sl = slice(s * SUB, (s + 1) * SUB)
        tot = None
        for ep in range(EP):
            e = eg * EP + ep
            rowbase = e * R + st * cap
            off = _align16(cnt_ref[(st * N_EXPERTS + e) * LANES + tl * nsub + s])
            rel = pl.multiple_of(jnp.minimum(rowbase + off - wss[ep], W - crows), 16)
            first = (wss[ep] + rel - rowbase).astype(f32)
            pos = pos_ref[0, ep, :, sl]
            onehot = jnp.where(pos == rid + first, 1.0, 0.0).astype(bf16)
            ywin = jnp.concatenate([y_refs[2 * ep][pl.ds(rel, crows), :], y_refs[2 * ep + 1][pl.ds(rel, crows), :]],
                                   axis=0)
            picked = _tn(jnp.concatenate([onehot, onehot], axis=0), ywin)
            term = gcols[ep][sl] * picked
            tot = term if tot is None else tot + term
        o_ref[sl, :] += m5 * tot


def _combine_call(cnt, x2d, posm, aff_tm, mvec, y_hl, cap, TT, crows, W, EP):
    S, E, T = posm.shape
    tps = T // TT
    R = S * cap
    ytot = E * R
    nsub = TT // SUB
    G = mvec.shape[0]

    def yspec(ep):
        def ymap(st, tl, eg, c):
            e = eg * EP + ep
            off = _align16(c[(st * E + e) * LANES + tl * nsub])
            return (pl.multiple_of(jnp.minimum(e * R + st * cap + off, ytot - W), 16), 0)
        return pl.BlockSpec((pl.Element(W), pl.Element(D_MODEL)), ymap)

    tok = lambda w: pl.BlockSpec((TT, w), lambda st, tl, e, c: (st * tps + tl, 0))
    mmap = (lambda st, tl, e, c: (st, 0, 0)) if G > 1 else (lambda st, tl, e, c: (0, 0, 0))
    gs = pltpu.PrefetchScalarGridSpec(
        num_scalar_prefetch=1,
        grid=(S, tps, E // EP),
        in_specs=[tok(D_MODEL),
                  pl.BlockSpec((1, EP, 1, TT), lambda st, tl, e, c: (st, e, 0, tl)),
                  tok(N_EXPERTS),
                  pl.BlockSpec((1, 1, D_MODEL), mmap)] + [yspec(ep) for ep in range(EP) for _ in range(2)],
        out_specs=tok(D_MODEL))
    yh, yl = y_hl[0].reshape(ytot, D_MODEL), y_hl[1].reshape(ytot, D_MODEL)
    return pl.pallas_call(
        functools.partial(_combine_kernel, TT=TT, cap=cap, R=R, W=W, ytot=ytot, crows=crows, EP=EP),
        grid_spec=gs,
        out_shape=jax.ShapeDtypeStruct(x2d.shape, f32),
        compiler_params=_cparams(("parallel", "parallel", "arbitrary")),
        name="moe_combine",
    )(cnt.reshape(-1), x2d, posm.reshape(S, E, 1, T), aff_tm, mvec.reshape(G, 1, D_MODEL), *([yh, yl] * EP))


def combine(cnt, x2d, posm, aff_tm, mvec, y_hl, cap, TT, fast_ok=None):
    safe = lambda: _combine_call(cnt, x2d, posm, aff_tm, mvec, y_hl, cap, TT, CMB_ROWS, TT + 32, 1)
    if fast_ok is None:
        return safe()
    wfast = FAST_TILE_MAX + 16 + FAST_ROWS
    return lax.cond(fast_ok, lambda: _combine_call(cnt, x2d, posm, aff_tm, mvec, y_hl, cap, TT, FAST_ROWS, wfast, 4),
                    safe)


def _rope_tables(n):
    half = NA_HD // 2
    q = half // 2
    inv = ROPE_THETA ** (-np.arange(q, dtype=np.float64) / q)
    pos = np.arange(n)
    ang_r = (pos // GRID_W)[:, None] * inv
    ang_c = (pos % GRID_W)[:, None] * inv
    zero = np.zeros_like(ang_r)
    c = np.concatenate([np.cos(ang_r)] * 2 + [np.cos(ang_c)] * 2, axis=1)
    s1 = np.concatenate([-np.sin(ang_r), zero, -np.sin(ang_c), zero], axis=1)
    s2 = np.concatenate([zero, np.sin(ang_r), zero, np.sin(ang_c)], axis=1)
    two = lambda a: jnp.asarray(np.concatenate([a, a], axis=1).astype(np.float32))
    return two(c), two(s1), two(s2)


def _mixing(hx, hc, need_ctx, B, N, NC, la, lc, w_in, q_gain, k_gain, table, rope, bd, conv_w, conv_b,
            spec, skip, filt_c, wa, wb, wc, wo, x2d, c2d, mx2, mc2):
    tile8 = lambda v: jnp.tile(v.reshape(1, NA_HD), (1, NA_HEADS))
    qg, kg = tile8(q_gain), tile8(k_gain)
    norm_aux = [("col", kg), ("const", bd)]
    rope_aux = [("row", rope[0]), ("row", rope[1]), ("row", rope[2])]
    lf_aux = lambda d: [("col", la[d:d + 1]), ("col", lc[d:d + 1])]
    tc = hc.shape[0]

    lff_c = project(hc, w_in, OFF_FF, 512, _epi_logforget, lf_aux(0), (f32,), tm=tc).reshape(B, NC, 512)
    lfb_c = project(hc, w_in, OFF_FB, 512, _epi_logforget, lf_aux(1), (f32,), tm=tc).reshape(B, NC, 512)
    i_c = project(hc, w_in, OFF_I, 512, _epi_raw, tm=tc).reshape(B, NC, 512)
    k_c = project(hc, w_in, OFF_NK, 512, _epi_norm, norm_aux, tm=tc).reshape(B, NC, 512)
    v_c = project(hc, w_in, OFF_NV, 512, _epi_raw, tm=tc).reshape(B, NC, 512)
    if need_ctx:
        q_c = project(hc, w_in, OFF_HQ, 512, _epi_silu, tm=tc).reshape(B, NC, 512)
    else:
        q_c = jnp.zeros((B, NC, 512), bf16)
    s0 = jnp.zeros((B, HG_HEADS, LANES, LANES), f32)
    oc_f, oc_b, s_f, s_b = hgrn_bidir(lff_c, lfb_c, i_c, q_c, s0, s0)

    lff_x = project(hx, w_in, OFF_FF, 512, _epi_logforget, lf_aux(0), (f32,)).reshape(B, N, 512)
    lfb_x = project(hx, w_in, OFF_FB, 512, _epi_logforget, lf_aux(1), (f32,)).reshape(B, N, 512)
    i_x = project(hx, w_in, OFF_I, 512, _epi_raw).reshape(B, N, 512)
    q_x = project(hx, w_in, OFF_HQ, 512, _epi_silu).reshape(B, N, 512)
    g_x = project(hx, w_in, OFF_HG, 512, _epi_silu).reshape(B, N, 512)
    k_x = project(hx, w_in, OFF_NK, 512, _epi_norm_rope, norm_aux + rope_aux, rows_per_seq=N).reshape(B, N, 512)
    v_x = project(hx, w_in, OFF_NV, 512, _epi_raw).reshape(B, N, 512)
    qn_x, qr_x = project(hx, w_in, OFF_NQ, 512, _epi_norm_both, [("col", qg), ("const", bd)] + rope_aux,
                         (bf16, bf16), rows_per_seq=N)
    p_x = project(hx, w_in, OFF_HY, 3 * HY_W, _epi_raw, out_dtypes=(f32,)).reshape(B, N, 3 * HY_W)
    gates_x = project(hx, w_in, OFF_GATE, 3 * D_MODEL, _epi_sigmoid)

    ox_f, ox_b, _, _ = hgrn_bidir(lff_x, lfb_x, i_x, q_x, s_f, s_b)

    nb_x = neighbourhood_attention(qr_x.reshape(B, N, 512), qn_x.reshape(B, N, 512), k_x, v_x, k_c, v_c, table)

    z_x, x0_x = hyena_pre(p_x, conv_w, conv_b)
    hy_x = hyena_conv(z_x, x0_x, spec[0], spec[1], skip, spec[2])

    flat = lambda a: a.reshape(-1, a.shape[-1])
    x_new = merge(flat(ox_f), flat(ox_b), flat(g_x), flat(nb_x), flat(hy_x), gates_x, wa, wb, wc, wo, x2d, mx2, N)
    if not need_ctx:
        return x_new, None

    qn_c = project(hc, w_in, OFF_NQ, 512, _epi_norm, [("col", qg), ("const", bd)], tm=tc).reshape(B, NC, 512)
    nb_c = context_attention(qn_c, k_c, v_c)
    p_c = project(hc, w_in, OFF_HY, 3 * HY_W, _epi_raw, out_dtypes=(f32,), tm=tc).reshape(B, NC, 3 * HY_W)
    hy_c = hyena_ctx(p_c, conv_w, conv_b, filt_c[0], filt_c[1], skip)
    gates_c = project(hc, w_in, OFF_GATE, 3 * D_MODEL, _epi_sigmoid, tm=tc)
    g_c = project(hc, w_in, OFF_HG, 512, _epi_silu, tm=tc)
    c_new = merge(flat(oc_f), flat(oc_b), g_c, flat(nb_c), flat(hy_c), gates_c, wa, wb, wc, wo, c2d, mc2, B * NC)
    return x_new, c_new


def kernel(x, c, ctx, c_ctx, w_mod, b_mod, norm_mix, norm_ffn, w_in, hg_lb, na_q_gain, na_k_gain, na_rpb,
           hy_conv_w, hy_conv_b, hy_pe_w1, hy_pe_b1, hy_pe_freq1, hy_pe_w2, hy_pe_b2, hy_pe_freq2, hy_pe_w3,
           hy_skip, w_branch_a, w_branch_b, w_branch_c, w_out, w_router, w_e_gate, w_e_up, w_e_down):
    B, N, D = x.shape
    NC = ctx.shape[1]
    E = N_EXPERTS
    cap_x = EC_CAP_FACTOR * N // E
    cap_c = EC_CAP_FACTOR * NC // E

    lb = jnp.cumsum(jax.nn.softmax(hg_lb.astype(f32), axis=0), axis=0)
    lb = lb - lb[:1]
    la_all, lc_all = jnp.log(lb), jnp.log1p(-lb)

    s8 = jnp.zeros((8, D), f32).at[:B].set(c).at[B].set(c_ctx)
    rope = _rope_tables(N)
    bd = jnp.asarray(np.kron(np.eye(NA_HEADS), np.full((NA_HD, NA_HD), 1.0 / NA_HD)).astype(np.float32), bf16)
    dcx = _dft_consts(N)

    x2d = x.reshape(B * N, D)
    c2d = ctx.reshape(B * NC, D)
    for l in range(DEPTH):
        need_ctx = l < DEPTH - 1
        mv = modvec(s8, w_mod[l], b_mod[l])
        mx = [mv[:B, k * D:(k + 1) * D] for k in range(6)]
        mc = [mv[B:B + 1, k * D:(k + 1) * D] for k in range(6)]
        w_in_l = w_in[l].astype(bf16)
        hx = modulate(x2d, norm_mix[l], mx[0], mx[1], N, bf16)
        hc = modulate(c2d, norm_mix[l], mc[0], mc[1], B * NC, bf16)

        filt = (hy_pe_w1[l], hy_pe_b1[l], hy_pe_freq1[l], hy_pe_w2[l], hy_pe_b2[l], hy_pe_freq2[l], hy_pe_w3[l])
        k_x, nrm_x = hyena_filter(N, *filt)
        sre, sim = hyena_spectrum(k_x, 1.0 / nrm_x, dcx)
        skip = hy_skip[l].reshape(1, HY_W)
        filt_c = None
        if need_ctx:
            h_c, nrm_c = hyena_filter(NC, *filt)
            filt_c = (h_c, 1.0 / nrm_c)

        x2d, c_new = _mixing(
            hx, hc, need_ctx, B, N, NC, la_all[l], lc_all[l], w_in_l, na_q_gain[l], na_k_gain[l],
            _na_bias_table(na_rpb[l]), rope, bd, hy_conv_w[l], hy_conv_b[l], (sre, sim, dcx), skip,
            filt_c, w_branch_a[l].astype(bf16), w_branch_b[l].astype(bf16), w_branch_c[l].astype(bf16),
            w_out[l].astype(bf16), x2d, c2d, mx[2], mc[2])

        h2, aff_t, aff_m = router(x2d, norm_ffn[l], mx[3], mx[4], w_router[l], N)
        posm, cnt = select_topk(aff_t, cap_x)
        fast_ok = _fits_fast(cnt, N)
        xgs = [gather_rows(cnt, h2, posm, cap_x, 1024, fast_ok)]
        if need_ctx:
            c2d = c_new
            hc2, aff_tc, aff_mc = router(c2d, norm_ffn[l], mc[3], mc[4], w_router[l], NC)
            posm_c, cnt_c = select_topk(aff_tc, cap_c)
            xgs.append(gather_rows(cnt_c, hc2, posm_c, cap_c, NC))
        ys = expert_ffn(xgs, l, w_e_gate, w_e_up, w_e_down)
        x2d = combine(cnt, x2d, posm, aff_m, mx[5], ys[0], cap_x, CMB_TILE, fast_ok)
        if need_ctx:
            c2d = combine(cnt_c, c2d, posm_c, aff_mc, mc[5], ys[1], cap_c, NC)
    return x2d.reshape(B, N, D)
```

```python
import functools
import math

import numpy as np
import jax
import jax.numpy as jnp
from jax import lax
from jax.experimental import pallas as pl
from jax.experimental.pallas import tpu as pltpu

f32 = jnp.float32
bf16 = jnp.bfloat16
i32 = jnp.int32
HI = lax.Precision.HIGHEST

D_MODEL = 1024
DEPTH = 2
GRID_W = 64
EPS = 1e-6
HG_HEADS = 4
HG_W = 512
HG_CHUNK = 64
NA_HEADS = 8
NA_HD = 64
NA_W = 512
NA_WIN_R = 8
NA_WIN_C = 16
ROPE_THETA = 10000.0
HY_W = 512
HY_BANDS = 16
HY_PE_DIM = 1 + 2 * HY_BANDS
HY_FILT_HID = 64
HY_FAST_DECAY = 0.3
HY_SLOW_DECAY = 1.5
HY_TARGET = 1e-2
OFF_FF = 0
OFF_FB = 512
OFF_I = 1024
OFF_NK = 1536
OFF_NV = 2048
OFF_HQ = 2560
OFF_NQ = 3072
OFF_HG = 3584
OFF_HY = 4096
OFF_GATE = 5632
IN_COLS = 8704
N_EXPERTS = 16
EC_CAP_FACTOR = 2
D_FF_EXPERT = 2816

LANES = 128
NEG_BIG = -1e30
VMEM_LIMIT = 56 * 1024 * 1024


def _cparams(sem, vmem=VMEM_LIMIT):
    return pltpu.CompilerParams(dimension_semantics=sem, vmem_limit_bytes=vmem)


def _nt(a, b, precision=None):
    return lax.dot_general(a, b, (((1,), (1,)), ((), ())), precision=precision, preferred_element_type=f32)


def _tn(a, b, precision=None):
    return lax.dot_general(a, b, (((0,), (0,)), ((), ())), precision=precision, preferred_element_type=f32)


def _silu(x):
    return x * jax.nn.sigmoid(x)


def _split(x):
    hi = x.astype(bf16)
    return hi, (x - hi.astype(f32)).astype(bf16)


def _modvec_kernel(s_ref, w_ref, b_ref, o_ref):
    s = _silu(s_ref[...])
    o_ref[...] = jnp.dot(s, w_ref[...], precision=HI, preferred_element_type=f32) + b_ref[...]


def modvec(s8, w, b):
    n = w.shape[1]
    tn = 1024
    return pl.pallas_call(
        _modvec_kernel,
        grid=(n // tn,),
        in_specs=[pl.BlockSpec((8, D_MODEL), lambda j: (0, 0)),
                  pl.BlockSpec((D_MODEL, tn), lambda j: (0, j)),
                  pl.BlockSpec((1, tn), lambda j: (0, j))],
        out_specs=pl.BlockSpec((8, tn), lambda j: (0, j)),
        out_shape=jax.ShapeDtypeStruct((8, n), f32),
        compiler_params=_cparams(("parallel",)),
        name="modvec",
    )(s8, w, b.reshape(1, n))


def _modulate_kernel(x_ref, g_ref, sh_ref, sc_ref, o_ref):
    x = x_ref[...]
    ms = jnp.mean(x * x, axis=-1, keepdims=True)
    y = x * lax.rsqrt(ms + EPS)
    o_ref[...] = (y * g_ref[...] * (1.0 + sc_ref[0]) + sh_ref[0]).astype(o_ref.dtype)


def modulate(x2d, g, shift, scale, rows_per_group, out_dtype):
    R = x2d.shape[0]
    tm = 512
    tpg = rows_per_group // tm
    G = shift.shape[0]
    return pl.pallas_call(
        _modulate_kernel,
        grid=(R // tm,),
        in_specs=[pl.BlockSpec((tm, D_MODEL), lambda i: (i, 0)),
                  pl.BlockSpec((1, D_MODEL), lambda i: (0, 0)),
                  pl.BlockSpec((1, 1, D_MODEL), lambda i: (i // tpg, 0, 0)),
                  pl.BlockSpec((1, 1, D_MODEL), lambda i: (i // tpg, 0, 0))],
        out_specs=pl.BlockSpec((tm, D_MODEL), lambda i: (i, 0)),
        out_shape=jax.ShapeDtypeStruct((R, D_MODEL), out_dtype),
        compiler_params=_cparams(("parallel",)),
        name="modulate",
    )(x2d, g.reshape(1, D_MODEL), shift.reshape(G, 1, D_MODEL), scale.reshape(G, 1, D_MODEL))


def _log1p_exp_neg(a):
    return jnp.log(1.0 + jnp.exp(-a))


def _log_sigmoid(z):
    return jnp.minimum(z, 0.0) - _log1p_exp_neg(jnp.abs(z))


def _epi_raw(acc, o_ref):
    o_ref[...] = acc.astype(o_ref.dtype)


def _epi_silu(acc, o_ref):
    o_ref[...] = _silu(acc).astype(o_ref.dtype)


def _epi_sigmoid(acc, o_ref):
    o_ref[...] = jax.nn.sigmoid(acc).astype(o_ref.dtype)


def _epi_logforget(acc, la_ref, lc_ref, o_ref):
    la = la_ref[...]
    c = lc_ref[...] + _log_sigmoid(acc)
    o_ref[...] = jnp.maximum(la, c) + _log1p_exp_neg(jnp.abs(la - c))


def _head_rms(acc, gain_ref, bd_ref):
    hi, lo = _split(acc * acc)
    ms = jnp.dot(hi, bd_ref[...], preferred_element_type=f32) + jnp.dot(lo, bd_ref[...], preferred_element_type=f32)
    return acc * lax.rsqrt(ms + EPS) * gain_ref[...]


def _rope(y, c_ref, s1_ref, s2_ref):
    reps = y.shape[1] // LANES
    c = jnp.concatenate([c_ref[...]] * reps, axis=1)
    s1 = jnp.concatenate([s1_ref[...]] * reps, axis=1)
    s2 = jnp.concatenate([s2_ref[...]] * reps, axis=1)
    w = y.shape[1]
    return y * c + pltpu.roll(y, w - 16, axis=1) * s1 + pltpu.roll(y, 16, axis=1) * s2


def _epi_norm(acc, gain_ref, bd_ref, o_ref):
    o_ref[...] = _head_rms(acc, gain_ref, bd_ref).astype(o_ref.dtype)


def _epi_norm_rope(acc, gain_ref, bd_ref, c_ref, s1_ref, s2_ref, o_ref):
    y = _head_rms(acc, gain_ref, bd_ref)
    o_ref[...] = _rope(y, c_ref, s1_ref, s2_ref).astype(o_ref.dtype)


def _epi_norm_both(acc, gain_ref, bd_ref, c_ref, s1_ref, s2_ref, on_ref, or_ref):
    y = _head_rms(acc, gain_ref, bd_ref)
    on_ref[...] = y.astype(on_ref.dtype)
    or_ref[...] = _rope(y, c_ref, s1_ref, s2_ref).astype(or_ref.dtype)


def _proj_kernel(h_ref, w_ref, *rest, epi):
    acc = jnp.dot(h_ref[...], w_ref[...], preferred_element_type=f32)
    epi(acc, *rest)


def project(h, w, c0, width, epi, aux=(), out_dtypes=(bf16,), tm=2048, rows_per_seq=None):
    R = h.shape[0]
    tn = 512
    nj = width // tn
    cb = c0 // tn
    in_specs = [pl.BlockSpec((tm, D_MODEL), lambda i, j: (i, 0)),
                pl.BlockSpec((D_MODEL, tn), lambda i, j: (0, cb + j))]
    args = [h, w]
    for kind, arr in aux:
        if kind == "col":
            in_specs.append(pl.BlockSpec((1, tn), lambda i, j: (0, j)))
        elif kind == "const":
            in_specs.append(pl.BlockSpec(arr.shape, lambda i, j: (0, 0)))
        else:
            tps = rows_per_seq // tm
            in_specs.append(pl.BlockSpec((tm, LANES), lambda i, j: (i % tps, 0)))
        args.append(arr)
    out_specs = [pl.BlockSpec((tm, tn), lambda i, j: (i, j)) for _ in out_dtypes]
    out_shape = [jax.ShapeDtypeStruct((R, width), dt) for dt in out_dtypes]
    res = pl.pallas_call(
        functools.partial(_proj_kernel, epi=epi),
        grid=(R // tm, nj),
        in_specs=in_specs,
        out_specs=out_specs,
        out_shape=out_shape,
        compiler_params=_cparams(("parallel", "parallel")),
        name="proj_" + epi.__name__[5:],
    )(*args)
    return res[0] if len(res) == 1 else res


HG_MM_LEVELS = 3


def _hgrn_tmatrix(C, reverse):
    t = np.arange(C)
    tau = (C - 1 - t) if reverse else t
    tt, uu = tau[:, None], tau[None, :]
    T = np.zeros((1 + HG_MM_LEVELS, C, C), np.float32)
    T[0] = uu <= tt
    for l in range(HG_MM_LEVELS):
        same = (tt >> (l + 1)) == (uu >> (l + 1))
        tr = ((tt >> l) & 1) == 1
        ur = ((uu >> l) & 1) == 1
        T[1 + l] = same & ((tr & ur & (uu <= tt)) | (~tr & ~ur & (uu > tt)))
    return T.reshape((1 + HG_MM_LEVELS) * C, C)


def _hgrn_level_sums(b, l, C, reverse):
    h = 1 << l
    parts = []
    for r0 in range(0, C, 2 * h):
        tb = r0 + h if reverse else r0 + h - 1
        d = b[r0:r0 + 2 * h] - b[tb:tb + 1]
        first, second = d[:h], d[h:]
        parts += [first, -second] if reverse else [-first, second]
    return jnp.concatenate(parts, axis=0)


def _hgrn_step(chains, C):
    L = int(round(math.log2(C)))
    ti0 = lax.broadcasted_iota(i32, (C, C), 0)
    si0 = lax.broadcasted_iota(i32, (C, C), 1)

    es, kcs = [], []
    for lf, _, _, tm, _, reverse in chains:
        W = lf.shape[1]
        hi, lo = _split(lf)
        r = jnp.dot(tm, jnp.concatenate([hi, lo], axis=1), preferred_element_type=f32)
        r = r[:, :W] + r[:, W:]
        b = r[0:C]
        end_row = 0 if reverse else C - 1
        levels = [r[(1 + l) * C:(2 + l) * C] for l in range(HG_MM_LEVELS)]
        levels += [_hgrn_level_sums(b, l, C, reverse) for l in range(HG_MM_LEVELS, L)]
        es.append((b, b[end_row:end_row + 1] - b, levels))
        kcs.append(1.0 - jnp.exp(lf))

    items = []
    for (lf, v_all, q_all, _, s_view, reverse), (b_all, suf_all, levels), kc_all in zip(chains, es, kcs):
        ti, si = (C - 1 - ti0, C - 1 - si0) if reverse else (ti0, si0)
        end_row = 0 if reverse else C - 1
        lmasks = [((ti >> (l + 1)) == (si >> (l + 1))) & (((ti >> l) & 1) == 1) & (((si >> l) & 1) == 0)
                  for l in range(L)]
        for h in range(HG_HEADS):
            sl = slice(LANES * h, LANES * (h + 1))
            qh, kch, vh = q_all[:, sl], kc_all[:, sl], v_all[:, sl]
            b = b_all[:, sl]
            st = s_view[h]
            pairs = [(qh.astype(bf16), kch.astype(bf16), ti == si)]
            for l in range(L):
                x = jnp.exp(levels[l][:, sl])
                pairs.append(((qh * x).astype(bf16), (kch * x).astype(bf16), lmasks[l]))
            items.append(dict(qb=(qh * jnp.exp(b)).astype(bf16), st=st, pairs=pairs, vh=vh,
                              kd=(kch * jnp.exp(suf_all[:, sl])).astype(bf16),
                              decay=jnp.exp(b[end_row:end_row + 1, :]), view=s_view, h=h))

    for it in items:
        it["o"] = _nt(it["qb"], it["st"].astype(bf16))
        it["att"] = [(_nt(ql, kl), m) for ql, kl, m in it["pairs"]]

    outs = []
    for it in items:
        att = None
        for a, m in it["att"]:
            t = jnp.where(m, a, 0.0)
            att = t if att is None else att + t
        it["o"] = it["o"] + jnp.dot(att.astype(bf16), it["vh"], preferred_element_type=f32)
        it["view"][it["h"]] = it["st"] * it["decay"] + _tn(it["vh"], it["kd"])
    nh = HG_HEADS
    for c in range(len(chains)):
        outs.append(jnp.concatenate([items[c * nh + h]["o"] for h in range(nh)], axis=1))
    return outs


def _hgrn_kernel(lff_ref, lfb_ref, vf_ref, vb_ref, qf_ref, qb_ref, s0f_ref, s0b_ref, tf_ref, tb_ref,
                 of_ref, ob_ref, sff_ref, sfb_ref, s_scr, *, C, B):
    c = pl.program_id(0)

    @pl.when(c == 0)
    def _():
        s_scr[0] = s0f_ref[...]
        s_scr[1] = s0b_ref[...]

    chains = []
    for b in range(B):
        chains.append((lff_ref[b], vf_ref[b], qf_ref[b].astype(f32), tf_ref[...], s_scr.at[0, b], False))
        chains.append((lfb_ref[b], vb_ref[b], qb_ref[b].astype(f32), tb_ref[...], s_scr.at[1, b], True))
    outs = _hgrn_step(chains, C)
    for b in range(B):
        of_ref[b] = outs[2 * b]
        ob_ref[b] = outs[2 * b + 1]

    @pl.when(c == pl.num_programs(0) - 1)
    def _():
        sff_ref[...] = s_scr[0]
        sfb_ref[...] = s_scr[1]


def hgrn_bidir(lf_f, lf_b, v, q, s0_f, s0_b):
    B, N, W = lf_f.shape
    C = HG_CHUNK
    nch = N // C
    tf = jnp.asarray(_hgrn_tmatrix(C, False), bf16)
    tb = jnp.asarray(_hgrn_tmatrix(C, True), bf16)
    fw = pl.BlockSpec((B, C, W), lambda c: (0, c, 0))
    bw = pl.BlockSpec((B, C, W), lambda c: (0, nch - 1 - c, 0))
    st = pl.BlockSpec((B, HG_HEADS, LANES, LANES), lambda c: (0, 0, 0, 0))
    tsp = pl.BlockSpec(tf.shape, lambda c: (0, 0))
    seq = jax.ShapeDtypeStruct((B, N, W), f32)
    sts = jax.ShapeDtypeStruct((B, HG_HEADS, LANES, LANES), f32)
    return pl.pallas_call(
        functools.partial(_hgrn_kernel, C=C, B=B),
        grid=(nch,),
        in_specs=[fw, bw, fw, bw, fw, bw, st, st, tsp, tsp],
        out_specs=[fw, bw, st, st],
        out_shape=[seq, seq, sts, sts],
        scratch_shapes=[pltpu.VMEM((2, B, HG_HEADS, LANES, LANES), f32)],
        compiler_params=_cparams(("arbitrary",)),
        name="hgrn",
    )(lf_f, lf_b, v, v, q, q, s0_f, s0_b, tf, tb)


LOG2E = math.log2(math.e)


def _na_kernel(qr_ref, qn_ref, k_ref, v_ref, kc_ref, vc_ref, tab_ref, o_ref, s_scr, p_scr, inv_scr, *,
               rows_per_step, n_rows):
    g = pl.program_id(2)
    scale = NA_HD ** -0.5 * LOG2E
    lane = lax.broadcasted_iota(i32, (GRID_W, LANES), 1)
    kcx = kc_ref[0]
    vcx = vc_ref[0]
    win = NA_WIN_R * GRID_W
    ctx_len = kcx.shape[0]

    starts = []
    for i in range(rows_per_step):
        r = g * rows_per_step + i
        rs = jnp.clip(r - NA_WIN_R // 2, 0, n_rows - NA_WIN_R)
        off = rs - r + (NA_WIN_R - 1)
        start = pl.multiple_of(rs * GRID_W, GRID_W)
        starts.append(start)
        kw = k_ref[0, pl.ds(start, win), :]
        qr = qr_ref[0, i * GRID_W:(i + 1) * GRID_W, :]
        qn = qn_ref[0, i * GRID_W:(i + 1) * GRID_W, :]
        zq = jnp.zeros_like(qr)
        qrs = jnp.concatenate([jnp.where(lane < NA_HD, qr, zq), jnp.where(lane < NA_HD, zq, qr)], axis=0)
        qns = jnp.concatenate([jnp.where(lane < NA_HD, qn, zq), jnp.where(lane < NA_HD, zq, qn)], axis=0)
        bias = jnp.concatenate([tab_ref[0, off], tab_ref[1, off]], axis=0)
        row0 = 2 * i * GRID_W
        s_scr[row0:row0 + 2 * GRID_W, 0:win] = _nt(qrs, kw) * scale + bias
        s_scr[row0:row0 + 2 * GRID_W, win:win + ctx_len] = _nt(qns, kcx) * scale

    def softmax_rows(c, carry):
        r0 = pl.multiple_of(c * LANES, LANES)
        s = s_scr[pl.ds(r0, LANES), :]
        p = jnp.exp2(s - jnp.max(s, axis=-1, keepdims=True))
        inv_scr[pl.ds(r0, LANES), :] = 1.0 / jnp.sum(p, axis=-1, keepdims=True)
        p_scr[pl.ds(r0, LANES), :] = p.astype(bf16)
        return carry

    lax.fori_loop(0, 2 * rows_per_step * GRID_W // LANES, softmax_rows, 0, unroll=2)

    for i in range(rows_per_step):
        vw = v_ref[0, pl.ds(starts[i], win), :]
        row0 = 2 * i * GRID_W
        p = p_scr[row0:row0 + 2 * GRID_W, :]
        res = (jnp.dot(p[:, :win], vw, preferred_element_type=f32)
               + jnp.dot(p[:, win:], vcx, preferred_element_type=f32))
        res = res * inv_scr[row0:row0 + 2 * GRID_W, :]
        o_ref[0, i * GRID_W:(i + 1) * GRID_W, :] = jnp.where(lane < NA_HD, res[:GRID_W], res[GRID_W:]).astype(o_ref.dtype)


def _na_bias_table(rpb):
    col = jnp.arange(GRID_W)
    cs = jnp.clip(col - NA_WIN_C // 2, 0, GRID_W - NA_WIN_C)
    kc = jnp.arange(GRID_W)
    valid = (kc[None, :] >= cs[:, None]) & (kc[None, :] < cs[:, None] + NA_WIN_C)
    dc = jnp.clip(kc[None, :] - col[:, None] + (NA_WIN_C - 1), 0, 2 * NA_WIN_C - 2)
    bc = jnp.where(valid[None, None], rpb[:, :, dc], NEG_BIG)
    t2 = jnp.stack([bc[:, o:o + NA_WIN_R] for o in range(NA_WIN_R)], axis=1)
    t2 = t2.transpose(0, 1, 3, 2, 4)
    return t2.reshape(NA_HEADS, NA_WIN_R, GRID_W, NA_WIN_R * GRID_W).astype(f32) * LOG2E


def neighbourhood_attention(q_rot, qn, k_rot, v, kc, vc, table):
    B, N, W = q_rot.shape
    n_rows = N // GRID_W
    rps = 8
    ctx_len = kc.shape[1]
    pairs = W // LANES
    keys = NA_WIN_R * GRID_W + ctx_len
    return pl.pallas_call(
        functools.partial(_na_kernel, rows_per_step=rps, n_rows=n_rows),
        grid=(B, pairs, n_rows // rps),
        in_specs=[pl.BlockSpec((1, rps * GRID_W, LANES), lambda b, p, g: (b, g, p)),
                  pl.BlockSpec((1, rps * GRID_W, LANES), lambda b, p, g: (b, g, p)),
                  pl.BlockSpec((1, N, LANES), lambda b, p, g: (b, 0, p)),
                  pl.BlockSpec((1, N, LANES), lambda b, p, g: (b, 0, p)),
                  pl.BlockSpec((1, ctx_len, LANES), lambda b, p, g: (b, 0, p)),
                  pl.BlockSpec((1, ctx_len, LANES), lambda b, p, g: (b, 0, p)),
                  pl.BlockSpec((2, NA_WIN_R, GRID_W, NA_WIN_R * GRID_W), lambda b, p, g: (p, 0, 0, 0))],
        out_specs=pl.BlockSpec((1, rps * GRID_W, LANES), lambda b, p, g: (b, g, p)),
        out_shape=jax.ShapeDtypeStruct((B, N, W), bf16),
        scratch_shapes=[pltpu.VMEM((2 * rps * GRID_W, keys), f32), pltpu.VMEM((2 * rps * GRID_W, keys), bf16),
                        pltpu.VMEM((2 * rps * GRID_W, 1), f32)],
        compiler_params=_cparams(("parallel", "parallel", "arbitrary")),
        name="natten",
    )(q_rot, qn, k_rot, v, kc, vc, table)


def _ctx_attn_kernel(q_ref, k_ref, v_ref, o_ref):
    scale = NA_HD ** -0.5
    q = q_ref[0]
    k = k_ref[0]
    v = v_ref[0]
    lane = lax.broadcasted_iota(i32, q.shape, 1)
    res = []
    for hh in range(2):
        m = (lane >= NA_HD * hh) & (lane < NA_HD * (hh + 1))
        s = _nt(jnp.where(m, q, jnp.zeros_like(q)), k) * scale
        p = jnp.exp(s - jnp.max(s, axis=-1, keepdims=True))
        p = p / jnp.sum(p, axis=-1, keepdims=True)
        res.append(jnp.dot(p.astype(bf16), v, preferred_element_type=f32))
    o_ref[0] = jnp.where(lane < NA_HD, res[0], res[1]).astype(o_ref.dtype)


def context_attention(q, k, v):
    B, N, W = q.shape
    spec = pl.BlockSpec((1, N, LANES), lambda b, p: (b, 0, p))
    return pl.pallas_call(
        _ctx_attn_kernel,
        grid=(B, W // LANES),
        in_specs=[spec, spec, spec],
        out_specs=spec,
        out_shape=jax.ShapeDtypeStruct((B, N, W), bf16),
        compiler_params=_cparams(("parallel", "parallel")),
        name="ctx_attn",
    )(q, k, v)


def _filter_kernel(z_ref, w1_ref, b1_ref, f1_ref, w2_ref, b2_ref, f2_ref, w3_ref, dl_ref, k_ref, nrm_ref, *, tm, n):
    i = pl.program_id(0)
    z = z_ref[...]
    hid = HY_FILT_HID
    dot = lambda a, b: jnp.dot(a, b, precision=HI, preferred_element_type=f32)
    w1 = w1_ref[...]
    a = jnp.sin(f1_ref[...] * (dot(z[:tm // 2], w1[:, :LANES]) + dot(z[tm // 2:], w1[:, LANES:]) + b1_ref[...]))
    a = jnp.sin(f2_ref[...] * (dot(a, w2_ref[...]) + b2_ref[...]))
    w3 = w3_ref[...]
    zero = jnp.zeros_like(w3)
    h = jnp.concatenate([dot(a, jnp.concatenate([w3, zero], axis=0)), dot(a, jnp.concatenate([zero, w3], axis=0))],
                        axis=0)
    h = h * jnp.exp(-z[:, 0:1] * dl_ref[...])
    row = lax.broadcasted_iota(i32, (tm, HY_W), 0) + i * tm
    k = jnp.where(row == n, 0.0, h)
    k_ref[...] = k
    part = jnp.sum(jnp.abs(k), axis=0, keepdims=True)

    @pl.when(i == 0)
    def _():
        nrm_ref[...] = part

    @pl.when(i > 0)
    def _():
        nrm_ref[...] = nrm_ref[...] + part


def hyena_filter(n, w1, b1, fr1, w2, b2, fr2, w3):
    t = np.linspace(0.0, 1.0, n)[:, None]
    w = 2 * math.pi * np.arange(n)[:, None] / n
    fb = np.linspace(1e-4, HY_BANDS - 1, HY_BANDS)[None]
    z = np.concatenate([t, np.cos(fb * w), -np.sin(fb * w)], axis=-1)
    z = np.concatenate([z, np.zeros((1, HY_PE_DIM)), z[:0:-1]], axis=0)
    z = jnp.asarray(np.pad(z, ((0, 0), (0, LANES - HY_PE_DIM))).astype(np.float32))
    w1p = jnp.pad(w1.astype(f32), ((0, LANES - HY_PE_DIM), (0, 0)))
    deltas = jnp.asarray(np.abs(np.linspace(math.log(HY_TARGET) / HY_SLOW_DECAY, math.log(HY_TARGET) / HY_FAST_DECAY,
                                            2 * HY_W))[None].astype(np.float32))
    tm = min(n, 512)
    hid = HY_FILT_HID
    full = lambda shape: pl.BlockSpec(shape, lambda i: (0, 0))
    tph = n // tm
    zpad = jnp.zeros((LANES, hid), f32)
    w1pk = jnp.concatenate([w1p, zpad, zpad, w1p], axis=1)
    w2f = w2.astype(f32)
    z2 = jnp.zeros((hid, hid), f32)
    w2pk = jnp.concatenate([jnp.concatenate([w2f, z2], axis=1), jnp.concatenate([z2, w2f], axis=1)], axis=0)
    twice = lambda v: jnp.tile(v.reshape(1, hid).astype(f32), (1, 2))
    return pl.pallas_call(
        functools.partial(_filter_kernel, tm=tm, n=n),
        grid=(2 * n // tm,),
        in_specs=[pl.BlockSpec((tm, LANES), lambda i: (i, 0)),
                  full((LANES, 2 * LANES)), full((1, LANES)), full((1, LANES)),
                  full((LANES, LANES)), full((1, LANES)), full((1, LANES)),
                  pl.BlockSpec((hid, HY_W), lambda i: (0, i // tph)),
                  pl.BlockSpec((1, HY_W), lambda i: (0, i // tph))],
        out_specs=[pl.BlockSpec((tm, HY_W), lambda i: (i, 0)), pl.BlockSpec((1, HY_W), lambda i: (0, 0))],
        out_shape=[jax.ShapeDtypeStruct((2 * n, HY_W), f32), jax.ShapeDtypeStruct((1, HY_W), f32)],
        compiler_params=_cparams(("arbitrary",)),
        name="hyena_filter",
    )(z, w1pk, twice(b1), twice(fr1), w2pk, twice(b2), twice(fr2), w3.astype(f32), deltas)


def _conv3(u, w_ref, b_ref):
    n = u.shape[0]
    row = lax.broadcasted_iota(i32, u.shape, 0)
    prev = jnp.where(row == 0, 0.0, pltpu.roll(u, 1, axis=0))
    nxt = jnp.where(row == n - 1, 0.0, pltpu.roll(u, n - 1, axis=0))
    return prev * w_ref[0:1, :] + u * w_ref[1:2, :] + nxt * w_ref[2:3, :] + b_ref[...]


def _hyena_pre_kernel(p0_ref, p1_ref, p2_ref, w0_ref, w1_ref, w2_ref, b0_ref, b1_ref, b2_ref, z_ref, x0_ref):
    x0_ref[0] = _conv3(p0_ref[0], w0_ref, b0_ref).astype(x0_ref.dtype)
    z_ref[0] = _conv3(p1_ref[0], w1_ref, b1_ref) * _conv3(p2_ref[0], w2_ref, b2_ref)


def hyena_pre(p, conv_w, conv_b):
    B, N, _ = p.shape
    nb = HY_W // LANES
    conv_b = conv_b.reshape(1, 3 * HY_W)
    pspec = lambda g: pl.BlockSpec((1, N, LANES), lambda b, c: (b, 0, g * nb + c))
    wspec = lambda g: pl.BlockSpec((3, LANES), lambda b, c: (0, g * nb + c))
    bspec = lambda g: pl.BlockSpec((1, LANES), lambda b, c: (0, g * nb + c))
    ospec = pl.BlockSpec((1, N, LANES), lambda b, c: (b, 0, c))
    return pl.pallas_call(
        _hyena_pre_kernel,
        grid=(B, nb),
        in_specs=[pspec(0), pspec(1), pspec(2), wspec(0), wspec(1), wspec(2), bspec(0), bspec(1), bspec(2)],
        out_specs=[ospec, ospec],
        out_shape=[jax.ShapeDtypeStruct((B, N, HY_W), f32), jax.ShapeDtypeStruct((B, N, HY_W), bf16)],
        compiler_params=_cparams(("parallel", "parallel")),
        name="hyena_pre",
    )(p, p, p, conv_w, conv_w, conv_w, conv_b, conv_b, conv_b)


DFT_SLABS = 8


def _hl(a):
    a32 = jnp.asarray(a.astype(np.float32))
    hi = a32.astype(bf16)
    lo = (a32 - hi.astype(f32)).astype(bf16)
    return jnp.concatenate([hi, lo], axis=-2)


def _dot3(a_hl, m, x):
    xh, xl = _split(x)
    r = jnp.dot(a_hl, xh, preferred_element_type=f32)
    return r[:m] + r[m:] + jnp.dot(a_hl[:m], xl, preferred_element_type=f32)


def _dft_consts(n):
    N = 2 * n
    na = N // LANES
    t1n = na // 2
    k1n = na // 2 + 1
    k1p = -(-k1n // 8) * 8
    k1 = np.arange(k1n)
    t1 = np.arange(t1n)
    th = 2 * np.pi * ((t1[None, :] * k1[:, None]) % na) / na
    f1c = np.zeros((2 * k1p, t1n))
    f1c[:k1n] = np.cos(th)
    f1c[k1p:k1p + k1n] = -np.sin(th)
    thf = 2 * np.pi * ((np.arange(na)[None, :] * k1[:, None]) % na) / na
    f1f = np.zeros((2 * k1p, na))
    f1f[:k1n] = np.cos(thf)
    f1f[k1p:k1p + k1n] = -np.sin(thf)
    k2 = np.arange(LANES)
    t2 = np.arange(LANES)
    m = (t2[None, None, :] * (k1[:, None, None] + na * k2[None, :, None])) % N
    ph = 2 * np.pi * m / N
    g = np.concatenate([np.cos(ph), -np.sin(ph)], axis=1)
    pht = ph.transpose(0, 2, 1)
    gi = np.concatenate([np.cos(pht), np.sin(pht)], axis=1)
    wk = np.where((k1 == 0) | (k1 == na // 2), 1.0, 2.0) / N
    f1i = np.zeros((t1n, 2 * k1p))
    f1i[:, :k1n] = np.cos(th.T) * wk[None, :]
    f1i[:, k1p:k1p + k1n] = -np.sin(th.T) * wk[None, :]
    k1e = -(-k1n // DFT_SLABS) * DFT_SLABS
    g = np.concatenate([g, np.zeros((k1e - k1n,) + g.shape[1:])], axis=0)
    gi = np.concatenate([gi, np.zeros((k1e - k1n,) + gi.shape[1:])], axis=0)
    return dict(na=na, t1n=t1n, k1n=k1n, k1e=k1e, k1p=k1p, f1c=_hl(f1c), f1f=_hl(f1f), g=_hl(g), gi=_hl(gi),
                f1i=_hl(f1i))


def _dft_stage1(src_ref, f1c_ref, tre_ref, tim_ref, t1n, k1p):
    f1c = f1c_ref[...]

    def body(t2, carry):
        zs = src_ref[pl.ds(t2, t1n, stride=LANES), :]
        r = _dot3(f1c, 2 * k1p, zs)
        r0 = pl.multiple_of(t2 * k1p, 8)
        tre_ref[pl.ds(r0, k1p), :] = r[:k1p]
        tim_ref[pl.ds(r0, k1p), :] = r[k1p:]
        return carry

    lax.fori_loop(0, LANES, body, 0, unroll=4)


def _slab(tre_ref, tim_ref, k1, k1p):
    return tre_ref[pl.ds(k1, LANES, stride=k1p), :], tim_ref[pl.ds(k1, LANES, stride=k1p), :]


def _cplx_left(gc_hl, xre, xim):
    cw = xre.shape[1]
    r = _dot3(gc_hl, 2 * LANES, jnp.concatenate([xre, xim], axis=1))
    p, q = r[:, :cw], r[:, cw:]
    return p[:LANES] - q[LANES:], p[LANES:] + q[:LANES]


def _spectrum_kernel(k_ref, f1f_ref, g_ref, inv_ref, xre_ref, xim_ref, are, aim, *, na, k1p):
    j = pl.program_id(1)

    @pl.when(j == 0)
    def _():
        _dft_stage1(k_ref, f1f_ref, are, aim, na, k1p)

    for half in range(DFT_SLABS):
        xre, xim = _cplx_left(g_ref[half], *_slab(are, aim, DFT_SLABS * j + half, k1p))
        rows = slice(half * LANES, (half + 1) * LANES)
        xre_ref[rows, :] = xre * inv_ref[...]
        xim_ref[rows, :] = xim * inv_ref[...]


def hyena_spectrum(k, inv_norm, dc):
    n2, C = k.shape
    k1e, k1p, na = dc["k1e"], dc["k1p"], dc["na"]
    cw = LANES
    out = jax.ShapeDtypeStruct((k1e * LANES, C), f32)
    ospec = pl.BlockSpec((DFT_SLABS * LANES, cw), lambda c, k: (k, c))
    return pl.pallas_call(
        functools.partial(_spectrum_kernel, na=na, k1p=k1p),
        grid=(C // cw, k1e // DFT_SLABS),
        in_specs=[pl.BlockSpec((n2, cw), lambda c, k: (0, c)),
                  pl.BlockSpec(dc["f1f"].shape, lambda c, k: (0, 0)),
                  pl.BlockSpec((DFT_SLABS, 4 * LANES, LANES), lambda c, k: (k, 0, 0)),
                  pl.BlockSpec((1, cw), lambda c, k: (0, c))],
        out_specs=[ospec, ospec],
        out_shape=[out, out],
        scratch_shapes=[pltpu.VMEM((k1p * LANES, cw), f32), pltpu.VMEM((k1p * LANES, cw), f32)],
        compiler_params=_cparams(("parallel", "arbitrary")),
        name="hyena_spectrum",
    )(k, dc["f1f"], dc["g"], inv_norm)


def _hyena_conv_kernel(z_ref, x0_ref, f1c_ref, g_ref, gi_ref, f1i_ref, kre_ref, kim_ref,
                       skip_ref, o_ref, tre, tim, are, aim, y_scr, *, t1n, k1p):
    j = pl.program_id(2)

    @pl.when(j == 0)
    def _():
        _dft_stage1(z_ref.at[0], f1c_ref, tre, tim, t1n, k1p)

    for half in range(DFT_SLABS):
        r0 = pl.multiple_of((DFT_SLABS * j + half) * LANES, LANES)
        xre, xim = _cplx_left(g_ref[half], *_slab(tre, tim, DFT_SLABS * j + half, k1p))
        rows = slice(half * LANES, (half + 1) * LANES)
        kre = kre_ref[rows, :]
        kim = kim_ref[rows, :]
        yre = xre * kre - xim * kim
        yim = xre * kim + xim * kre
        bre, bim = _cplx_left(gi_ref[half], yre, yim)
        are[pl.ds(r0, LANES), :] = bre
        aim[pl.ds(r0, LANES), :] = bim

    @pl.when(j == pl.num_programs(2) - 1)
    def _():
        f1i = f1i_ref[...]

        def body(t2, carry):
            bb = jnp.concatenate([are[pl.ds(t2, k1p, stride=LANES), :], aim[pl.ds(t2, k1p, stride=LANES), :]], axis=0)
            y_scr[pl.ds(t2, t1n, stride=LANES), :] = _dot3(f1i, t1n, bb)
            return carry

        lax.fori_loop(0, LANES, body, 0, unroll=4)
        z = z_ref[0]
        o_ref[0] = (x0_ref[0].astype(f32) * (y_scr[...] + z * skip_ref[...])).astype(o_ref.dtype)


def hyena_conv(z, x0, spec_re, spec_im, skip, dc):
    B, n, W = z.shape
    nb = W // LANES
    k1e, k1p, t1n = dc["k1e"], dc["k1p"], dc["t1n"]
    seq = pl.BlockSpec((1, n, LANES), lambda b, c, k: (b, 0, c))
    fspec = pl.BlockSpec((DFT_SLABS * LANES, LANES), lambda b, c, k: (k, c))
    cspec = pl.BlockSpec((DFT_SLABS, 4 * LANES, LANES), lambda b, c, k: (k, 0, 0))
    vspec = pl.BlockSpec((1, LANES), lambda b, c, k: (0, c))
    return pl.pallas_call(
        functools.partial(_hyena_conv_kernel, t1n=t1n, k1p=k1p),
        grid=(B, nb, k1e // DFT_SLABS),
        in_specs=[seq, seq,
                  pl.BlockSpec(dc["f1c"].shape, lambda b, c, k: (0, 0)), cspec, cspec,
                  pl.BlockSpec(dc["f1i"].shape, lambda b, c, k: (0, 0)),
                  fspec, fspec, vspec],
        out_specs=seq,
        out_shape=jax.ShapeDtypeStruct((B, n, W), bf16),
        scratch_shapes=[pltpu.VMEM((k1p * LANES, LANES), f32)] * 4 + [pltpu.VMEM((n, LANES), f32)],
        compiler_params=_cparams(("parallel", "parallel", "arbitrary")),
        name="hyena_conv",
    )(z, x0, dc["f1c"], dc["g"], dc["gi"], dc["f1i"], spec_re, spec_im, skip)


def _hyena_ctx_kernel(p0_ref, p1_ref, p2_ref, w0_ref, w1_ref, w2_ref, b0_ref, b1_ref, b2_ref,
                      k_ref, inv_ref, skip_ref, fd_ref, fi_ref, o_ref, *, n):
    x0 = _conv3(p0_ref[0], w0_ref, b0_ref)
    z = _conv3(p1_ref[0], w1_ref, b1_ref) * _conv3(p2_ref[0], w2_ref, b2_ref)
    fd = fd_ref[...]
    N = 2 * n
    zf = jnp.dot(fd[:, :n], z, precision=HI, preferred_element_type=f32)
    kf = jnp.dot(fd, k_ref[...], precision=HI, preferred_element_type=f32) * inv_ref[...]
    yre = zf[:N] * kf[:N] - zf[N:] * kf[N:]
    yim = zf[:N] * kf[N:] + zf[N:] * kf[:N]
    y = jnp.dot(fi_ref[...], jnp.concatenate([yre, yim], axis=0), precision=HI, preferred_element_type=f32)
    o_ref[0] = (x0 * (y + z * skip_ref[...])).astype(o_ref.dtype)


def hyena_ctx(p, conv_w, conv_b, k, inv_norm, skip):
    B, n, _ = p.shape
    N = 2 * n
    nb = HY_W // LANES
    kk = np.arange(N)
    ph = 2 * np.pi * ((kk[:, None] * kk[None, :]) % N) / N
    fd = jnp.asarray(np.concatenate([np.cos(ph), -np.sin(ph)], axis=0).astype(np.float32))
    fi = jnp.asarray((np.concatenate([np.cos(ph[:n]), -np.sin(ph[:n])], axis=1) / N).astype(np.float32))
    conv_b = conv_b.reshape(1, 3 * HY_W)
    pspec = lambda g: pl.BlockSpec((1, n, LANES), lambda b, c: (b, 0, g * nb + c))
    wspec = lambda g: pl.BlockSpec((3, LANES), lambda b, c: (0, g * nb + c))
    bspec = lambda g: pl.BlockSpec((1, LANES), lambda b, c: (0, g * nb + c))
    vspec = pl.BlockSpec((1, LANES), lambda b, c: (0, c))
    return pl.pallas_call(
        functools.partial(_hyena_ctx_kernel, n=n),
        grid=(B, nb),
        in_specs=[pspec(0), pspec(1), pspec(2), wspec(0), wspec(1), wspec(2), bspec(0), bspec(1), bspec(2),
                  pl.BlockSpec((N, LANES), lambda b, c: (0, c)),
                  vspec, vspec,
                  pl.BlockSpec(fd.shape, lambda b, c: (0, 0)), pl.BlockSpec(fi.shape, lambda b, c: (0, 0))],
        out_specs=pl.BlockSpec((1, n, LANES), lambda b, c: (b, 0, c)),
        out_shape=jax.ShapeDtypeStruct((B, n, HY_W), bf16),
        compiler_params=_cparams(("parallel", "parallel")),
        name="hyena_ctx",
    )(p, p, p, conv_w, conv_w, conv_w, conv_b, conv_b, conv_b, k, inv_norm, skip, fd, fi)


def _merge_kernel(of_ref, ob_ref, gs_ref, nb_ref, hc_ref, g_ref, wa_ref, wb_ref, wc_ref, wo_ref, x_ref, m_ref, o_ref):
    d = D_MODEL
    tot = of_ref[...] + ob_ref[...]
    gs = gs_ref[...].astype(f32)
    ra = []
    for h in range(HG_HEADS):
        sl = slice(LANES * h, LANES * (h + 1))
        th = tot[:, sl]
        ms = jnp.mean(th * th, axis=-1, keepdims=True)
        ra.append(th * lax.rsqrt(ms + EPS) * gs[:, sl])
    ya = jnp.dot(jnp.concatenate(ra, axis=1).astype(bf16), wa_ref[...], preferred_element_type=f32)
    yb = jnp.dot(nb_ref[...], wb_ref[...], preferred_element_type=f32)
    yc = jnp.dot(hc_ref[...], wc_ref[...], preferred_element_type=f32)
    g = g_ref[...].astype(f32)
    mix = g[:, :d] * ya + g[:, d:2 * d] * yb + g[:, 2 * d:] * yc
    y = jnp.dot(mix.astype(bf16), wo_ref[...], preferred_element_type=f32)
    o_ref[...] = x_ref[...] + m_ref[0] * y


def merge(o_f, o_b, gs, nb, hc, gates, wa, wb, wc, wo, x2d, m, rows_per_group):
    R = x2d.shape[0]
    tm = 512
    tpg = rows_per_group // tm
    G = m.shape[0]
    row = lambda w: pl.BlockSpec((tm, w), lambda i: (i, 0))
    full = lambda a: pl.BlockSpec(a.shape, lambda i: (0, 0))
    return pl.pallas_call(
        _merge_kernel,
        grid=(R // tm,),
        in_specs=[row(HG_W), row(HG_W), row(HG_W), row(NA_W), row(HY_W), row(3 * D_MODEL),
                  full(wa), full(wb), full(wc), full(wo),
                  row(D_MODEL), pl.BlockSpec((1, 1, D_MODEL), lambda i: (i // tpg, 0, 0))],
        out_specs=row(D_MODEL),
        out_shape=jax.ShapeDtypeStruct((R, D_MODEL), f32),
        compiler_params=_cparams(("parallel",)),
        name="merge",
    )(o_f, o_b, gs, nb, hc, gates, wa, wb, wc, wo, x2d, m.reshape(G, 1, D_MODEL))


def _router_kernel(x_ref, g_ref, sh_ref, sc_ref, wrt_ref, wr_ref, h_ref, at_ref, am_ref):
    x = x_ref[...]
    ms = jnp.mean(x * x, axis=-1, keepdims=True)
    h = x * lax.rsqrt(ms + EPS) * g_ref[...] * (1.0 + sc_ref[0]) + sh_ref[0]
    h_ref[...] = h.astype(h_ref.dtype)
    hh, hl = _split(h)
    ne = N_EXPERTS
    wt = wrt_ref[...]
    rt = _nt(wt, hh)
    lt = rt[:ne] + rt[ne:] + _nt(wt[:ne], hl)
    et = jnp.exp(lt - jnp.max(lt, axis=0, keepdims=True))
    at_ref[0] = et / jnp.sum(et, axis=0, keepdims=True)
    wm = wr_ref[...]
    rm = jnp.dot(hh, wm, preferred_element_type=f32)
    lm = rm[:, :ne] + rm[:, ne:] + jnp.dot(hl, wm[:, :ne], preferred_element_type=f32)
    em = jnp.exp(lm - jnp.max(lm, axis=1, keepdims=True))
    am_ref[...] = em / jnp.sum(em, axis=1, keepdims=True)


def router(x2d, g, shift, scale, w_router, n_per_set):
    R = x2d.shape[0]
    tm = min(512, n_per_set)
    tps = n_per_set // tm
    S = R // n_per_set
    G = shift.shape[0]
    gmap = (lambda i: (i // tps, 0, 0)) if G > 1 else (lambda i: (0, 0, 0))
    whi, wlo = _split(w_router.astype(f32))
    wr = jnp.concatenate([whi, wlo], axis=1)
    return pl.pallas_call(
        _router_kernel,
        grid=(R // tm,),
        in_specs=[pl.BlockSpec((tm, D_MODEL), lambda i: (i, 0)),
                  pl.BlockSpec((1, D_MODEL), lambda i: (0, 0)),
                  pl.BlockSpec((1, 1, D_MODEL), gmap),
                  pl.BlockSpec((1, 1, D_MODEL), gmap),
                  pl.BlockSpec((2 * N_EXPERTS, D_MODEL), lambda i: (0, 0)),
                  pl.BlockSpec((D_MODEL, 2 * N_EXPERTS), lambda i: (0, 0))],
        out_specs=[pl.BlockSpec((tm, D_MODEL), lambda i: (i, 0)),
                   pl.BlockSpec((1, N_EXPERTS, tm), lambda i: (i // tps, 0, i % tps)),
                   pl.BlockSpec((tm, N_EXPERTS), lambda i: (i, 0))],
        out_shape=[jax.ShapeDtypeStruct((R, D_MODEL), bf16),
                   jax.ShapeDtypeStruct((S, N_EXPERTS, n_per_set), f32),
                   jax.ShapeDtypeStruct((R, N_EXPERTS), f32)],
        compiler_params=_cparams(("parallel",)),
        name="router",
    )(x2d, g.reshape(1, D_MODEL), shift.reshape(G, 1, D_MODEL), scale.reshape(G, 1, D_MODEL), wr.T, wr)


SEL_BLK = 256
SUB = LANES
SUBW = SUB + 8
UNSEL = -float(2 ** 30)


def _prefix_incl(mask_f, tri, T):
    outs = []
    off = jnp.zeros((mask_f.shape[0], 1), f32)
    for b in range(T // SEL_BLK):
        blk = mask_f[:, b * SEL_BLK:(b + 1) * SEL_BLK].astype(bf16)
        pre = jnp.dot(blk, tri, preferred_element_type=f32) + off
        outs.append(pre)
        off = pre[:, SEL_BLK - 1:SEL_BLK]
    return jnp.concatenate(outs, axis=1)


def _select_kernel(a_ref, tri_ref, cm_ref, posm_ref, cnt_ref, *, T, cap):
    aff = a_ref[0]
    bits = pltpu.bitcast(aff, i32)
    tri = tri_ref[...]

    def bit_step(i, thr):
        cand = thr | (1 << (30 - i))
        cnt = jnp.sum((bits >= cand).astype(f32), axis=1, keepdims=True)
        return jnp.where(cnt >= cap, cand, thr)

    thr = lax.fori_loop(0, 31, bit_step, jnp.zeros((N_EXPERTS, 1), i32))
    gt = bits > thr
    eq = bits == thr
    need = cap - jnp.sum(gt.astype(f32), axis=1, keepdims=True)
    eqf = eq.astype(f32)
    rank_eq = _prefix_incl(eqf, tri, T) - eqf
    sel = gt | (eq & (rank_eq < need))
    self_ = sel.astype(f32)
    pos = _prefix_incl(self_, tri, T) - self_
    posm_ref[0] = jnp.where(sel, pos, UNSEL)
    cnt_ref[0] = jnp.dot(self_.astype(bf16), cm_ref[...], preferred_element_type=f32).astype(i32)


def select_topk(aff, cap):
    S, E, T = aff.shape
    tri = jnp.asarray(np.triu(np.ones((SEL_BLK, SEL_BLK), np.float32)), bf16)
    cm = jnp.asarray((np.arange(T)[:, None] < np.arange(LANES)[None, :] * SUB).astype(np.float32), bf16)
    return pl.pallas_call(
        functools.partial(_select_kernel, T=T, cap=cap),
        grid=(S,),
        in_specs=[pl.BlockSpec((1, E, T), lambda s: (s, 0, 0)),
                  pl.BlockSpec((SEL_BLK, SEL_BLK), lambda s: (0, 0)),
                  pl.BlockSpec((T, LANES), lambda s: (0, 0))],
        out_specs=[pl.BlockSpec((1, E, T), lambda s: (s, 0, 0)),
                   pl.BlockSpec((1, E, LANES), lambda s: (s, 0, 0))],
        out_shape=[jax.ShapeDtypeStruct((S, E, T), f32), jax.ShapeDtypeStruct((S, E, LANES), i32)],
        compiler_params=_cparams(("parallel",)),
        name="select_topk",
    )(aff, tri, cm)


def _align8(v):
    return lax.shift_left(lax.shift_right_logical(v, 3), 3)


def _align16(v):
    return lax.shift_left(lax.shift_right_logical(v, 4), 4)


CMB_ROWS = SUB + 16


def _gather_kernel(cnt_ref, h_ref, pos_ref, o_ref, acc, *, TT, cap, tps, R, srows, EP):
    tl = pl.program_id(1)

    @pl.when(tl == 0)
    def _():
        acc[...] = jnp.zeros_like(acc)

    st = tl // tps
    nsub = TT // SUB
    rid = lax.broadcasted_iota(i32, (srows, SUB), 0).astype(f32)
    for ep in range(EP):
        e = pl.program_id(0) * EP + ep
        cbase = (st * N_EXPERTS + e) * LANES + (tl % tps) * nsub
        for s in range(nsub):
            off8 = _align8(cnt_ref[cbase + s])
            pos = pos_ref[0, ep, :, s * SUB:(s + 1) * SUB]
            onehot = jnp.where(pos == rid + off8.astype(f32), 1.0, 0.0).astype(bf16)
            rows = jnp.dot(onehot, h_ref[s * SUB:(s + 1) * SUB, :], preferred_element_type=f32)
            r0 = pl.multiple_of(st * cap + off8, 8)
            acc[ep, pl.ds(r0, srows), :] += rows

    @pl.when(tl == pl.num_programs(1) - 1)
    def _():
        o_ref[...] = acc[:, 0:R, :].astype(o_ref.dtype)


def _gather_call(cnt, h, posm, cap, TT, srows, EP):
    S, E, T = posm.shape
    tps = T // TT
    R = S * cap
    gs = pltpu.PrefetchScalarGridSpec(
        num_scalar_prefetch=1,
        grid=(E // EP, S * tps),
        in_specs=[pl.BlockSpec((TT, D_MODEL), lambda e, t, c: (t, 0)),
                  pl.BlockSpec((1, EP, 1, TT), lambda e, t, c: (t // tps, e, 0, t % tps))],
        out_specs=pl.BlockSpec((EP, R, D_MODEL), lambda e, t, c: (e, 0, 0)),
        scratch_shapes=[pltpu.VMEM((EP, R + srows, D_MODEL), f32)])
    return pl.pallas_call(
        functools.partial(_gather_kernel, TT=TT, cap=cap, tps=tps, R=R, srows=srows, EP=EP),
        grid_spec=gs,
        out_shape=jax.ShapeDtypeStruct((E, R, D_MODEL), bf16),
        compiler_params=_cparams(("parallel", "arbitrary")),
        name="moe_gather",
    )(cnt.reshape(-1), h, posm.reshape(S, E, 1, T))


FAST_SUB_MAX = 48
FAST_TILE_MAX = 112
FAST_ROWS = 64
CMB_TILE = 512


def _fits_fast(cnt, T):
    nsub = T // SUB
    per_sub = cnt[..., 1:nsub + 1] - cnt[..., :nsub]
    k = CMB_TILE // SUB
    per_tile = cnt[..., k:nsub + 1:k] - cnt[..., 0:nsub:k]
    return (jnp.max(per_sub) <= FAST_SUB_MAX) & (jnp.max(per_tile) <= FAST_TILE_MAX)


def gather_rows(cnt, h, posm, cap, TT, fast_ok=None):
    safe = lambda: _gather_call(cnt, h, posm, cap, TT, SUBW, 1)
    if fast_ok is None:
        return safe()
    return lax.cond(fast_ok, lambda: _gather_call(cnt, h, posm, cap, TT, FAST_ROWS, 2), safe)


EXPERT_TF = 256


def _ffn_kernel(*refs, n):
    xs = refs[:n]
    wg_ref, wu_ref, wd_ref = refs[n:n + 3]
    his = refs[n + 3:2 * n + 3]
    los = refs[2 * n + 3:3 * n + 3]
    accs = refs[3 * n + 3:]
    j = pl.program_id(1)
    wg = wg_ref[0, 0].astype(bf16)
    wu = wu_ref[0, 0].astype(bf16)
    wd = wd_ref[0, 0].astype(bf16)
    @pl.when(j == 0)
    def _():
        for acc in accs:
            acc[...] = jnp.zeros_like(acc)

    for x_ref, hi_ref, lo_ref, acc in zip(xs, his, los, accs):
        x = x_ref[0]
        a = jnp.dot(x, wg, preferred_element_type=f32)
        u = jnp.dot(x, wu, preferred_element_type=f32)
        acc[...] += jnp.dot((_silu(a) * u).astype(bf16), wd, preferred_element_type=f32)

        @pl.when(j == pl.num_programs(1) - 1)
        def _(acc=acc, hi_ref=hi_ref, lo_ref=lo_ref):
            hi, lo = _split(acc[...])
            hi_ref[0] = hi
            lo_ref[0] = lo


def expert_ffn(xgs, layer, w_gate, w_up, w_down):
    E = xgs[0].shape[0]
    nf = D_FF_EXPERT // EXPERT_TF
    n = len(xgs)
    rowspec = lambda a: pl.BlockSpec((1, a.shape[1], D_MODEL), lambda e, j: (e, 0, 0))
    res = pl.pallas_call(
        functools.partial(_ffn_kernel, n=n),
        grid=(E, nf),
        in_specs=[rowspec(a) for a in xgs] + [
            pl.BlockSpec((1, 1, D_MODEL, EXPERT_TF), lambda e, j: (layer, e, 0, j)),
            pl.BlockSpec((1, 1, D_MODEL, EXPERT_TF), lambda e, j: (layer, e, 0, j)),
            pl.BlockSpec((1, 1, EXPERT_TF, D_MODEL), lambda e, j: (layer, e, j, 0))],
        out_specs=[rowspec(a) for a in xgs] * 2,
        out_shape=[jax.ShapeDtypeStruct(a.shape, bf16) for a in xgs] * 2,
        scratch_shapes=[pltpu.VMEM(a.shape[1:], f32) for a in xgs],
        compiler_params=_cparams(("parallel", "arbitrary")),
        name="expert_ffn",
    )(*xgs, w_gate, w_up, w_down)
    return [(res[i], res[n + i]) for i in range(n)]


def _combine_kernel(cnt_ref, x_ref, pos_ref, am_ref, m_ref, *rest, TT, cap, R, W, ytot, crows, EP):
    y_refs, o_ref = rest[:2 * EP], rest[2 * EP]
    st = pl.program_id(0)
    tl = pl.program_id(1)
    eg = pl.program_id(2)

    @pl.when(eg == 0)
    def _():
        o_ref[...] = x_ref[...]

    nsub = TT // SUB
    lane = lax.broadcasted_iota(i32, (TT, N_EXPERTS), 1)
    m5 = m_ref[0]
    rid = lax.broadcasted_iota(i32, (crows, SUB), 0).astype(f32)
    am = am_ref[...]
    gcols, wss = [], []
    for ep in range(EP):
        e = eg * EP + ep
        cbase = (st * N_EXPERTS + e) * LANES + tl * nsub
        wss.append(jnp.minimum(e * R + st * cap + _align16(cnt_ref[cbase]), ytot - W))
        gcols.append(jnp.sum(jnp.where(lane == e, am, 0.0), axis=1, keepdims=True))
    for s in range(nsub):
        sl = slice(s * SUB, (s + 1) * SUB)
        tot = None
        for ep in range(EP):
            e = eg * EP + ep
            rowbase = e * R + st * cap
            off = _align16(cnt_ref[(st * N_EXPERTS + e) * LANES + tl * nsub + s])
            rel = pl.multiple_of(jnp.minimum(rowbase + off - wss[ep], W - crows), 16)
            first = (wss[ep] + rel - rowbase).astype(f32)
            pos = pos_ref[0, ep, :, sl]
            onehot = jnp.where(pos == rid + first, 1.0, 0.0).astype(bf16)
            ywin = jnp.concatenate([y_refs[2 * ep][pl.ds(rel, crows), :], y_refs[2 * ep + 1][pl.ds(rel, crows), :]],
                                   axis=0)
            picked = _tn(jnp.concatenate([onehot, onehot], axis=0), ywin)
            term = gcols[ep][sl] * picked
            tot = term if tot is None else tot + term
        o_ref[sl, :] += m5 * tot


def _combine_call(cnt, x2d, posm, aff_tm, mvec, y_hl, cap, TT, crows, W, EP):
    S, E, T = posm.shape
    tps = T // TT
    R = S * cap
    ytot = E * R
    nsub = TT // SUB
    G = mvec.shape[0]

    def yspec(ep):
        def ymap(st, tl, eg, c):
            e = eg * EP + ep
            off = _align16(c[(st * E + e) * LANES + tl * nsub])
            return (pl.multiple_of(jnp.minimum(e * R + st * cap + off, ytot - W), 16), 0)
        return pl.BlockSpec((pl.Element(W), pl.Element(D_MODEL)), ymap)

    tok = lambda w: pl.BlockSpec((TT, w), lambda st, tl, e, c: (st * tps + tl, 0))
    mmap = (lambda st, tl, e, c: (st, 0, 0)) if G > 1 else (lambda st, tl, e, c: (0, 0, 0))
    gs = pltpu.PrefetchScalarGridSpec(
        num_scalar_prefetch=1,
        grid=(S, tps, E // EP),
        in_specs=[tok(D_MODEL),
                  pl.BlockSpec((1, EP, 1, TT), lambda st, tl, e, c: (st, e, 0, tl)),
                  tok(N_EXPERTS),
                  pl.BlockSpec((1, 1, D_MODEL), mmap)] + [yspec(ep) for ep in range(EP) for _ in range(2)],
        out_specs=tok(D_MODEL))
    yh, yl = y_hl[0].reshape(ytot, D_MODEL), y_hl[1].reshape(ytot, D_MODEL)
    return pl.pallas_call(
        functools.partial(_combine_kernel, TT=TT, cap=cap, R=R, W=W, ytot=ytot, crows=crows, EP=EP),
        grid_spec=gs,
        out_shape=jax.ShapeDtypeStruct(x2d.shape, f32),
        compiler_params=_cparams(("parallel", "parallel", "arbitrary")),
        name="moe_combine",
    )(cnt.reshape(-1), x2d, posm.reshape(S, E, 1, T), aff_tm, mvec.reshape(G, 1, D_MODEL), *([yh, yl] * EP))


def combine(cnt, x2d, posm, aff_tm, mvec, y_hl, cap, TT, fast_ok=None):
    safe = lambda: _combine_call(cnt, x2d, posm, aff_tm, mvec, y_hl, cap, TT, CMB_ROWS, TT + 32, 1)
    if fast_ok is None:
        return safe()
    wfast = FAST_TILE_MAX + 16 + FAST_ROWS
    return lax.cond(fast_ok, lambda: _combine_call(cnt, x2d, posm, aff_tm, mvec, y_hl, cap, TT, FAST_ROWS, wfast, 4),
                    safe)


def _rope_tables(n):
    half = NA_HD // 2
    q = half // 2
    inv = ROPE_THETA ** (-np.arange(q, dtype=np.float64) / q)
    pos = np.arange(n)
    ang_r = (pos // GRID_W)[:, None] * inv
    ang_c = (pos % GRID_W)[:, None] * inv
    zero = np.zeros_like(ang_r)
    c = np.concatenate([np.cos(ang_r)] * 2 + [np.cos(ang_c)] * 2, axis=1)
    s1 = np.concatenate([-np.sin(ang_r), zero, -np.sin(ang_c), zero], axis=1)
    s2 = np.concatenate([zero, np.sin(ang_r), zero, np.sin(ang_c)], axis=1)
    two = lambda a: jnp.asarray(np.concatenate([a, a], axis=1).astype(np.float32))
    return two(c), two(s1), two(s2)


def _mixing(hx, hc, need_ctx, B, N, NC, la, lc, w_in, q_gain, k_gain, table, rope, bd, conv_w, conv_b,
            spec, skip, filt_c, wa, wb, wc, wo, x2d, c2d, mx2, mc2):
    tile8 = lambda v: jnp.tile(v.reshape(1, NA_HD), (1, NA_HEADS))
    qg, kg = tile8(q_gain), tile8(k_gain)
    norm_aux = [("col", kg), ("const", bd)]
    rope_aux = [("row", rope[0]), ("row", rope[1]), ("row", rope[2])]
    lf_aux = lambda d: [("col", la[d:d + 1]), ("col", lc[d:d + 1])]
    tc = hc.shape[0]

    lff_c = project(hc, w_in, OFF_FF, 512, _epi_logforget, lf_aux(0), (f32,), tm=tc).reshape(B, NC, 512)
    lfb_c = project(hc, w_in, OFF_FB, 512, _epi_logforget, lf_aux(1), (f32,), tm=tc).reshape(B, NC, 512)
    i_c = project(hc, w_in, OFF_I, 512, _epi_raw, tm=tc).reshape(B, NC, 512)
    k_c = project(hc, w_in, OFF_NK, 512, _epi_norm, norm_aux, tm=tc).reshape(B, NC, 512)
    v_c = project(hc, w_in, OFF_NV, 512, _epi_raw, tm=tc).reshape(B, NC, 512)
    if need_ctx:
        q_c = project(hc, w_in, OFF_HQ, 512, _epi_silu, tm=tc).reshape(B, NC, 512)
    else:
        q_c = jnp.zeros((B, NC, 512), bf16)
    s0 = jnp.zeros((B, HG_HEADS, LANES, LANES), f32)
    oc_f, oc_b, s_f, s_b = hgrn_bidir(lff_c, lfb_c, i_c, q_c, s0, s0)

    lff_x = project(hx, w_in, OFF_FF, 512, _epi_logforget, lf_aux(0), (f32,)).reshape(B, N, 512)
    lfb_x = project(hx, w_in, OFF_FB, 512, _epi_logforget, lf_aux(1), (f32,)).reshape(B, N, 512)
    i_x = project(hx, w_in, OFF_I, 512, _epi_raw).reshape(B, N, 512)
    q_x = project(hx, w_in, OFF_HQ, 512, _epi_silu).reshape(B, N, 512)
    g_x = project(hx, w_in, OFF_HG, 512, _epi_silu).reshape(B, N, 512)
    k_x = project(hx, w_in, OFF_NK, 512, _epi_norm_rope, norm_aux + rope_aux, rows_per_seq=N).reshape(B, N, 512)
    v_x = project(hx, w_in, OFF_NV, 512, _epi_raw).reshape(B, N, 512)
    qn_x, qr_x = project(hx, w_in, OFF_NQ, 512, _epi_norm_both, [("col", qg), ("const", bd)] + rope_aux,
                         (bf16, bf16), rows_per_seq=N)
    p_x = project(hx, w_in, OFF_HY, 3 * HY_W, _epi_raw, out_dtypes=(f32,)).reshape(B, N, 3 * HY_W)
    gates_x = project(hx, w_in, OFF_GATE, 3 * D_MODEL, _epi_sigmoid)

    ox_f, ox_b, _, _ = hgrn_bidir(lff_x, lfb_x, i_x, q_x, s_f, s_b)

    nb_x = neighbourhood_attention(qr_x.reshape(B, N, 512), qn_x.reshape(B, N, 512), k_x, v_x, k_c, v_c, table)

    z_x, x0_x = hyena_pre(p_x, conv_w, conv_b)
    hy_x = hyena_conv(z_x, x0_x, spec[0], spec[1], skip, spec[2])

    flat = lambda a: a.reshape(-1, a.shape[-1])
    x_new = merge(flat(ox_f), flat(ox_b), flat(g_x), flat(nb_x), flat(hy_x), gates_x, wa, wb, wc, wo, x2d, mx2, N)
    if not need_ctx:
        return x_new, None

    qn_c = project(hc, w_in, OFF_NQ, 512, _epi_norm, [("col", qg), ("const", bd)], tm=tc).reshape(B, NC, 512)
    nb_c = context_attention(qn_c, k_c, v_c)
    p_c = project(hc, w_in, OFF_HY, 3 * HY_W, _epi_raw, out_dtypes=(f32,), tm=tc).reshape(B, NC, 3 * HY_W)
    hy_c = hyena_ctx(p_c, conv_w, conv_b, filt_c[0], filt_c[1], skip)
    gates_c = project(hc, w_in, OFF_GATE, 3 * D_MODEL, _epi_sigmoid, tm=tc)
    g_c = project(hc, w_in, OFF_HG, 512, _epi_silu, tm=tc)
    c_new = merge(flat(oc_f), flat(oc_b), g_c, flat(nb_c), flat(hy_c), gates_c, wa, wb, wc, wo, c2d, mc2, B * NC)
    return x_new, c_new


def kernel(x, c, ctx, c_ctx, w_mod, b_mod, norm_mix, norm_ffn, w_in, hg_lb, na_q_gain, na_k_gain, na_rpb,
           hy_conv_w, hy_conv_b, hy_pe_w1, hy_pe_b1, hy_pe_freq1, hy_pe_w2, hy_pe_b2, hy_pe_freq2, hy_pe_w3,
           hy_skip, w_branch_a, w_branch_b, w_branch_c, w_out, w_router, w_e_gate, w_e_up, w_e_down):
    B, N, D = x.shape
    NC = ctx.shape[1]
    E = N_EXPERTS
    cap_x = EC_CAP_FACTOR * N // E
    cap_c = EC_CAP_FACTOR * NC // E

    lb = jnp.cumsum(jax.nn.softmax(hg_lb.astype(f32), axis=0), axis=0)
    lb = lb - lb[:1]
    la_all, lc_all = jnp.log(lb), jnp.log1p(-lb)

    s8 = jnp.zeros((8, D), f32).at[:B].set(c).at[B].set(c_ctx)
    rope = _rope_tables(N)
    bd = jnp.asarray(np.kron(np.eye(NA_HEADS), np.full((NA_HD, NA_HD), 1.0 / NA_HD)).astype(np.float32), bf16)
    dcx = _dft_consts(N)

    x2d = x.reshape(B * N, D)
    c2d = ctx.reshape(B * NC, D)
    for l in range(DEPTH):
        need_ctx = l < DEPTH - 1
        mv = modvec(s8, w_mod[l], b_mod[l])
        mx = [mv[:B, k * D:(k + 1) * D] for k in range(6)]
        mc = [mv[B:B + 1, k * D:(k + 1) * D] for k in range(6)]
        w_in_l = w_in[l].astype(bf16)
        hx = modulate(x2d, norm_mix[l], mx[0], mx[1], N, bf16)
        hc = modulate(c2d, norm_mix[l], mc[0], mc[1], B * NC, bf16)

        filt = (hy_pe_w1[l], hy_pe_b1[l], hy_pe_freq1[l], hy_pe_w2[l], hy_pe_b2[l], hy_pe_freq2[l], hy_pe_w3[l])
        k_x, nrm_x = hyena_filter(N, *filt)
        sre, sim = hyena_spectrum(k_x, 1.0 / nrm_x, dcx)
        skip = hy_skip[l].reshape(1, HY_W)
        filt_c = None
        if need_ctx:
            h_c, nrm_c = hyena_filter(NC, *filt)
            filt_c = (h_c, 1.0 / nrm_c)

        x2d, c_new = _mixing(
            hx, hc, need_ctx, B, N, NC, la_all[l], lc_all[l], w_in_l, na_q_gain[l], na_k_gain[l],
            _na_bias_table(na_rpb[l]), rope, bd, hy_conv_w[l], hy_conv_b[l], (sre, sim, dcx), skip,
            filt_c, w_branch_a[l].astype(bf16), w_branch_b[l].astype(bf16), w_branch_c[l].astype(bf16),
            w_out[l].astype(bf16), x2d, c2d, mx[2], mc[2])

        h2, aff_t, aff_m = router(x2d, norm_ffn[l], mx[3], mx[4], w_router[l], N)
        posm, cnt = select_topk(aff_t, cap_x)
        fast_ok = _fits_fast(cnt, N)
        xgs = [gather_rows(cnt, h2, posm, cap_x, 1024, fast_ok)]
        if need_ctx:
            c2d = c_new
            hc2, aff_tc, aff_mc = router(c2d, norm_ffn[l], mc[3], mc[4], w_router[l], NC)
            posm_c, cnt_c = select_topk(aff_tc, cap_c)
            xgs.append(gather_rows(cnt_c, hc2, posm_c, cap_c, NC))
        ys = expert_ffn(xgs, l, w_e_gate, w_e_up, w_e_down)
        x2d = combine(cnt, x2d, posm, aff_m, mx[5], ys[0], cap_x, CMB_TILE, fast_ok)
        if need_ctx:
            c2d = combine(cnt_c, c2d, posm_c, aff_mc, mc[5], ys[1], cap_c, NC)
    return x2d.reshape(B, N, D)
```

```python
import functools
import math

import numpy as np
import jax
import jax.numpy as jnp
from jax import lax
from jax.experimental import pallas as pl
from jax.experimental.pallas import tpu as pltpu

f32 = jnp.float32
bf16 = jnp.bfloat16
i32 = jnp.int32
HI = lax.Precision.HIGHEST

D_MODEL = 1024
DEPTH = 2
GRID_W = 64
EPS = 1e-6
HG_HEADS = 4
HG_W = 512
HG_CHUNK = 64
NA_HEADS = 8
NA_HD = 64
NA_W = 512
NA_WIN_R = 8
NA_WIN_C = 16
ROPE_THETA = 10000.0
HY_W = 512
HY_BANDS = 16
HY_PE_DIM = 1 + 2 * HY_BANDS
HY_FILT_HID = 64
HY_FAST_DECAY = 0.3
HY_SLOW_DECAY = 1.5
HY_TARGET = 1e-2
OFF_FF = 0
OFF_FB = 512
OFF_I = 1024
OFF_NK = 1536
OFF_NV = 2048
OFF_HQ = 2560
OFF_NQ = 3072
OFF_HG = 3584
OFF_HY = 4096
OFF_GATE = 5632
IN_COLS = 8704
N_EXPERTS = 16
EC_CAP_FACTOR = 2
D_FF_EXPERT = 2816

LANES = 128
NEG_BIG = -1e30
VMEM_LIMIT = 56 * 1024 * 1024


def _cparams(sem, vmem=VMEM_LIMIT):
    return pltpu.CompilerParams(dimension_semantics=sem, vmem_limit_bytes=vmem)


def _nt(a, b, precision=None):
    return lax.dot_general(a, b, (((1,), (1,)), ((), ())), precision=precision, preferred_element_type=f32)


def _tn(a, b, precision=None):
    return lax.dot_general(a, b, (((0,), (0,)), ((), ())), precision=precision, preferred_element_type=f32)


def _silu(x):
    return x * jax.nn.sigmoid(x)


def _split(x):
    hi = x.astype(bf16)
    return hi, (x - hi.astype(f32)).astype(bf16)


def _modvec_kernel(s_ref, w_ref, b_ref, o_ref):
    s = _silu(s_ref[...])
    o_ref[...] = jnp.dot(s, w_ref[...], precision=HI, preferred_element_type=f32) + b_ref[...]


def modvec(s8, w, b):
    n = w.shape[1]
    tn = 1024
    return pl.pallas_call(
        _modvec_kernel,
        grid=(n // tn,),
        in_specs=[pl.BlockSpec((8, D_MODEL), lambda j: (0, 0)),
                  pl.BlockSpec((D_MODEL, tn), lambda j: (0, j)),
                  pl.BlockSpec((1, tn), lambda j: (0, j))],
        out_specs=pl.BlockSpec((8, tn), lambda j: (0, j)),
        out_shape=jax.ShapeDtypeStruct((8, n), f32),
        compiler_params=_cparams(("parallel",)),
        name="modvec",
    )(s8, w, b.reshape(1, n))


def _modulate_kernel(x_ref, g_ref, sh_ref, sc_ref, o_ref):
    x = x_ref[...]
    ms = jnp.mean(x * x, axis=-1, keepdims=True)
    y = x * lax.rsqrt(ms + EPS)
    o_ref[...] = (y * g_ref[...] * (1.0 + sc_ref[0]) + sh_ref[0]).astype(o_ref.dtype)


def modulate(x2d, g, shift, scale, rows_per_group, out_dtype):
    R = x2d.shape[0]
    tm = 512
    tpg = rows_per_group // tm
    G = shift.shape[0]
    return pl.pallas_call(
        _modulate_kernel,
        grid=(R // tm,),
        in_specs=[pl.BlockSpec((tm, D_MODEL), lambda i: (i, 0)),
                  pl.BlockSpec((1, D_MODEL), lambda i: (0, 0)),
                  pl.BlockSpec((1, 1, D_MODEL), lambda i: (i // tpg, 0, 0)),
                  pl.BlockSpec((1, 1, D_MODEL), lambda i: (i // tpg, 0, 0))],
        out_specs=pl.BlockSpec((tm, D_MODEL), lambda i: (i, 0)),
        out_shape=jax.ShapeDtypeStruct((R, D_MODEL), out_dtype),
        compiler_params=_cparams(("parallel",)),
        name="modulate",
    )(x2d, g.reshape(1, D_MODEL), shift.reshape(G, 1, D_MODEL), scale.reshape(G, 1, D_MODEL))


def _log1p_exp_neg(a):
    return jnp.log(1.0 + jnp.exp(-a))


def _log_sigmoid(z):
    return jnp.minimum(z, 0.0) - _log1p_exp_neg(jnp.abs(z))


def _epi_raw(acc, o_ref):
    o_ref[...] = acc.astype(o_ref.dtype)


def _epi_silu(acc, o_ref):
    o_ref[...] = _silu(acc).astype(o_ref.dtype)


def _epi_sigmoid(acc, o_ref):
    o_ref[...] = jax.nn.sigmoid(acc).astype(o_ref.dtype)


def _epi_logforget(acc, la_ref, lc_ref, o_ref):
    la = la_ref[...]
    c = lc_ref[...] + _log_sigmoid(acc)
    o_ref[...] = jnp.maximum(la, c) + _log1p_exp_neg(jnp.abs(la - c))


def _head_rms(acc, gain_ref, bd_ref):
    hi, lo = _split(acc * acc)
    ms = jnp.dot(hi, bd_ref[...], preferred_element_type=f32) + jnp.dot(lo, bd_ref[...], preferred_element_type=f32)
    return acc * lax.rsqrt(ms + EPS) * gain_ref[...]


def _rope(y, c_ref, s1_ref, s2_ref):
    reps = y.shape[1] // LANES
    c = jnp.concatenate([c_ref[...]] * reps, axis=1)
    s1 = jnp.concatenate([s1_ref[...]] * reps, axis=1)
    s2 = jnp.concatenate([s2_ref[...]] * reps, axis=1)
    w = y.shape[1]
    return y * c + pltpu.roll(y, w - 16, axis=1) * s1 + pltpu.roll(y, 16, axis=1) * s2


def _epi_norm(acc, gain_ref, bd_ref, o_ref):
    o_ref[...] = _head_rms(acc, gain_ref, bd_ref).astype(o_ref.dtype)


def _epi_norm_rope(acc, gain_ref, bd_ref, c_ref, s1_ref, s2_ref, o_ref):
    y = _head_rms(acc, gain_ref, bd_ref)
    o_ref[...] = _rope(y, c_ref, s1_ref, s2_ref).astype(o_ref.dtype)


def _epi_norm_both(acc, gain_ref, bd_ref, c_ref, s1_ref, s2_ref, on_ref, or_ref):
    y = _head_rms(acc, gain_ref, bd_ref)
    on_ref[...] = y.astype(on_ref.dtype)
    or_ref[...] = _rope(y, c_ref, s1_ref, s2_ref).astype(or_ref.dtype)


def _proj_kernel(h_ref, w_ref, *rest, epi):
    acc = jnp.dot(h_ref[...], w_ref[...], preferred_element_type=f32)
    epi(acc, *rest)


def project(h, w, c0, width, epi, aux=(), out_dtypes=(bf16,), tm=2048, rows_per_seq=None):
    R = h.shape[0]
    tn = 512
    nj = width // tn
    cb = c0 // tn
    in_specs = [pl.BlockSpec((tm, D_MODEL), lambda i, j: (i, 0)),
                pl.BlockSpec((D_MODEL, tn), lambda i, j: (0, cb + j))]
    args = [h, w]
    for kind, arr in aux:
        if kind == "col":
            in_specs.append(pl.BlockSpec((1, tn), lambda i, j: (0, j)))
        elif kind == "const":
            in_specs.append(pl.BlockSpec(arr.shape, lambda i, j: (0, 0)))
        else:
            tps = rows_per_seq // tm
            in_specs.append(pl.BlockSpec((tm, LANES), lambda i, j: (i % tps, 0)))
        args.append(arr)
    out_specs = [pl.BlockSpec((tm, tn), lambda i, j: (i, j)) for _ in out_dtypes]
    out_shape = [jax.ShapeDtypeStruct((R, width), dt) for dt in out_dtypes]
    res = pl.pallas_call(
        functools.partial(_proj_kernel, epi=epi),
        grid=(R // tm, nj),
        in_specs=in_specs,
        out_specs=out_specs,
        out_shape=out_shape,
        compiler_params=_cparams(("parallel", "parallel")),
        name="proj_" + epi.__name__[5:],
    )(*args)
    return res[0] if len(res) == 1 else res


HG_MM_LEVELS = 3


def _hgrn_tmatrix(C, reverse):
    t = np.arange(C)
    tau = (C - 1 - t) if reverse else t
    tt, uu = tau[:, None], tau[None, :]
    T = np.zeros((1 + HG_MM_LEVELS, C, C), np.float32)
    T[0] = uu <= tt
    for l in range(HG_MM_LEVELS):
        same = (tt >> (l + 1)) == (uu >> (l + 1))
        tr = ((tt >> l) & 1) == 1
        ur = ((uu >> l) & 1) == 1
        T[1 + l] = same & ((tr & ur & (uu <= tt)) | (~tr & ~ur & (uu > tt)))
    return T.reshape((1 + HG_MM_LEVELS) * C, C)


def _hgrn_level_sums(b, l, C, reverse):
    h = 1 << l
    parts = []
    for r0 in range(0, C, 2 * h):
        tb = r0 + h if reverse else r0 + h - 1
        d = b[r0:r0 + 2 * h] - b[tb:tb + 1]
        first, second = d[:h], d[h:]
        parts += [first, -second] if reverse else [-first, second]
    return jnp.concatenate(parts, axis=0)


def _hgrn_step(chains, C):
    L = int(round(math.log2(C)))
    ti0 = lax.broadcasted_iota(i32, (C, C), 0)
    si0 = lax.broadcasted_iota(i32, (C, C), 1)

    es, kcs = [], []
    for lf, _, _, tm, _, reverse in chains:
        W = lf.shape[1]
        hi, lo = _split(lf)
        r = jnp.dot(tm, jnp.concatenate([hi, lo], axis=1), preferred_element_type=f32)
        r = r[:, :W] + r[:, W:]
        b = r[0:C]
        end_row = 0 if reverse else C - 1
        levels = [r[(1 + l) * C:(2 + l) * C] for l in range(HG_MM_LEVELS)]
        levels += [_hgrn_level_sums(b, l, C, reverse) for l in range(HG_MM_LEVELS, L)]
        es.append((b, b[end_row:end_row + 1] - b, levels))
        kcs.append(1.0 - jnp.exp(lf))

    items = []
    for (lf, v_all, q_all, _, s_view, reverse), (b_all, suf_all, levels), kc_all in zip(chains, es, kcs):
        ti, si = (C - 1 - ti0, C - 1 - si0) if reverse else (ti0, si0)
        end_row = 0 if reverse else C - 1
        lmasks = [((ti >> (l + 1)) == (si >> (l + 1))) & (((ti >> l) & 1) == 1) & (((si >> l) & 1) == 0)
                  for l in range(L)]
        for h in range(HG_HEADS):
            sl = slice(LANES * h, LANES * (h + 1))
            qh, kch, vh = q_all[:, sl], kc_all[:, sl], v_all[:, sl]
            b = b_all[:, sl]
            st = s_view[h]
            pairs = [(qh.astype(bf16), kch.astype(bf16), ti == si)]
            for l in range(L):
                x = jnp.exp(levels[l][:, sl])
                pairs.append(((qh * x).astype(bf16), (kch * x).astype(bf16), lmasks[l]))
            items.append(dict(qb=(qh * jnp.exp(b)).astype(bf16), st=st, pairs=pairs, vh=vh,
                              kd=(kch * jnp.exp(suf_all[:, sl])).astype(bf16),
                              decay=jnp.exp(b[end_row:end_row + 1, :]), view=s_view, h=h))

    for it in items:
        it["o"] = _nt(it["qb"], it["st"].astype(bf16))
        it["att"] = [(_nt(ql, kl), m) for ql, kl, m in it["pairs"]]

    outs = []
    for it in items:
        att = None
        for a, m in it["att"]:
            t = jnp.where(m, a, 0.0)
            att = t if att is None else att + t
        it["o"] = it["o"] + jnp.dot(att.astype(bf16), it["vh"], preferred_element_type=f32)
        it["view"][it["h"]] = it["st"] * it["decay"] + _tn(it["vh"], it["kd"])
    nh = HG_HEADS
    for c in range(len(chains)):
        outs.append(jnp.concatenate([items[c * nh + h]["o"] for h in range(nh)], axis=1))
    return outs


def _hgrn_kernel(lff_ref, lfb_ref, vf_ref, vb_ref, qf_ref, qb_ref, s0f_ref, s0b_ref, tf_ref, tb_ref,
                 of_ref, ob_ref, sff_ref, sfb_ref, s_scr, *, C, B):
    c = pl.program_id(0)

    @pl.when(c == 0)
    def _():
        s_scr[0] = s0f_ref[...]
        s_scr[1] = s0b_ref[...]

    chains = []
    for b in range(B):
        chains.append((lff_ref[b], vf_ref[b], qf_ref[b].astype(f32), tf_ref[...], s_scr.at[0, b], False))
        chains.append((lfb_ref[b], vb_ref[b], qb_ref[b].astype(f32), tb_ref[...], s_scr.at[1, b], True))
    outs = _hgrn_step(chains, C)
    for b in range(B):
        of_ref[b] = outs[2 * b]
        ob_ref[b] = outs[2 * b + 1]

    @pl.when(c == pl.num_programs(0) - 1)
    def _():
        sff_ref[...] = s_scr[0]
        sfb_ref[...] = s_scr[1]


def hgrn_bidir(lf_f, lf_b, v, q, s0_f, s0_b):
    B, N, W = lf_f.shape
    C = HG_CHUNK
    nch = N // C
    tf = jnp.asarray(_hgrn_tmatrix(C, False), bf16)
    tb = jnp.asarray(_hgrn_tmatrix(C, True), bf16)
    fw = pl.BlockSpec((B, C, W), lambda c: (0, c, 0))
    bw = pl.BlockSpec((B, C, W), lambda c: (0, nch - 1 - c, 0))
    st = pl.BlockSpec((B, HG_HEADS, LANES, LANES), lambda c: (0, 0, 0, 0))
    tsp = pl.BlockSpec(tf.shape, lambda c: (0, 0))
    seq = jax.ShapeDtypeStruct((B, N, W), f32)
    sts = jax.ShapeDtypeStruct((B, HG_HEADS, LANES, LANES), f32)
    return pl.pallas_call(
        functools.partial(_hgrn_kernel, C=C, B=B),
        grid=(nch,),
        in_specs=[fw, bw, fw, bw, fw, bw, st, st, tsp, tsp],
        out_specs=[fw, bw, st, st],
        out_shape=[seq, seq, sts, sts],
        scratch_shapes=[pltpu.VMEM((2, B, HG_HEADS, LANES, LANES), f32)],
        compiler_params=_cparams(("arbitrary",)),
        name="hgrn",
    )(lf_f, lf_b, v, v, q, q, s0_f, s0_b, tf, tb)


LOG2E = math.log2(math.e)


def _na_kernel(qr_ref, qn_ref, k_ref, v_ref, kc_ref, vc_ref, tab_ref, o_ref, s_scr, p_scr, inv_scr, *,
               rows_per_step, n_rows):
    g = pl.program_id(2)
    scale = NA_HD ** -0.5 * LOG2E
    lane = lax.broadcasted_iota(i32, (GRID_W, LANES), 1)
    kcx = kc_ref[0]
    vcx = vc_ref[0]
    win = NA_WIN_R * GRID_W
    ctx_len = kcx.shape[0]

    starts = []
    for i in range(rows_per_step):
        r = g * rows_per_step + i
        rs = jnp.clip(r - NA_WIN_R // 2, 0, n_rows - NA_WIN_R)
        off = rs - r + (NA_WIN_R - 1)
        start = pl.multiple_of(rs * GRID_W, GRID_W)
        starts.append(start)
        kw = k_ref[0, pl.ds(start, win), :]
        qr = qr_ref[0, i * GRID_W:(i + 1) * GRID_W, :]
        qn = qn_ref[0, i * GRID_W:(i + 1) * GRID_W, :]
        zq = jnp.zeros_like(qr)
        qrs = jnp.concatenate([jnp.where(lane < NA_HD, qr, zq), jnp.where(lane < NA_HD, zq, qr)], axis=0)
        qns = jnp.concatenate([jnp.where(lane < NA_HD, qn, zq), jnp.where(lane < NA_HD, zq, qn)], axis=0)
        bias = jnp.concatenate([tab_ref[0, off], tab_ref[1, off]], axis=0)
        row0 = 2 * i * GRID_W
        s_scr[row0:row0 + 2 * GRID_W, 0:win] = _nt(qrs, kw) * scale + bias
        s_scr[row0:row0 + 2 * GRID_W, win:win + ctx_len] = _nt(qns, kcx) * scale

    def softmax_rows(c, carry):
        r0 = pl.multiple_of(c * LANES, LANES)
        s = s_scr[pl.ds(r0, LANES), :]
        p = jnp.exp2(s - jnp.max(s, axis=-1, keepdims=True))
        inv_scr[pl.ds(r0, LANES), :] = 1.0 / jnp.sum(p, axis=-1, keepdims=True)
        p_scr[pl.ds(r0, LANES), :] = p.astype(bf16)
        return carry

    lax.fori_loop(0, 2 * rows_per_step * GRID_W // LANES, softmax_rows, 0, unroll=2)

    for i in range(rows_per_step):
        vw = v_ref[0, pl.ds(starts[i], win), :]
        row0 = 2 * i * GRID_W
        p = p_scr[row0:row0 + 2 * GRID_W, :]
        res = (jnp.dot(p[:, :win], vw, preferred_element_type=f32)
               + jnp.dot(p[:, win:], vcx, preferred_element_type=f32))
        res = res * inv_scr[row0:row0 + 2 * GRID_W, :]
        o_ref[0, i * GRID_W:(i + 1) * GRID_W, :] = jnp.where(lane < NA_HD, res[:GRID_W], res[GRID_W:]).astype(o_ref.dtype)


def _na_bias_table(rpb):
    col = jnp.arange(GRID_W)
    cs = jnp.clip(col - NA_WIN_C // 2, 0, GRID_W - NA_WIN_C)
    kc = jnp.arange(GRID_W)
    valid = (kc[None, :] >= cs[:, None]) & (kc[None, :] < cs[:, None] + NA_WIN_C)
    dc = jnp.clip(kc[None, :] - col[:, None] + (NA_WIN_C - 1), 0, 2 * NA_WIN_C - 2)
    bc = jnp.where(valid[None, None], rpb[:, :, dc], NEG_BIG)
    t2 = jnp.stack([bc[:, o:o + NA_WIN_R] for o in range(NA_WIN_R)], axis=1)
    t2 = t2.transpose(0, 1, 3, 2, 4)
    return t2.reshape(NA_HEADS, NA_WIN_R, GRID_W, NA_WIN_R * GRID_W).astype(f32) * LOG2E


def neighbourhood_attention(q_rot, qn, k_rot, v, kc, vc, table):
    B, N, W = q_rot.shape
    n_rows = N // GRID_W
    rps = 8
    ctx_len = kc.shape[1]
    pairs = W // LANES
    keys = NA_WIN_R * GRID_W + ctx_len
    return pl.pallas_call(
        functools.partial(_na_kernel, rows_per_step=rps, n_rows=n_rows),
        grid=(B, pairs, n_rows // rps),
        in_specs=[pl.BlockSpec((1, rps * GRID_W, LANES), lambda b, p, g: (b, g, p)),
                  pl.BlockSpec((1, rps * GRID_W, LANES), lambda b, p, g: (b, g, p)),
                  pl.BlockSpec((1, N, LANES), lambda b, p, g: (b, 0, p)),
                  pl.BlockSpec((1, N, LANES), lambda b, p, g: (b, 0, p)),
                  pl.BlockSpec((1, ctx_len, LANES), lambda b, p, g: (b, 0, p)),
                  pl.BlockSpec((1, ctx_len, LANES), lambda b, p, g: (b, 0, p)),
                  pl.BlockSpec((2, NA_WIN_R, GRID_W, NA_WIN_R * GRID_W), lambda b, p, g: (p, 0, 0, 0))],
        out_specs=pl.BlockSpec((1, rps * GRID_W, LANES), lambda b, p, g: (b, g, p)),
        out_shape=jax.ShapeDtypeStruct((B, N, W), bf16),
        scratch_shapes=[pltpu.VMEM((2 * rps * GRID_W, keys), f32), pltpu.VMEM((2 * rps * GRID_W, keys), bf16),
                        pltpu.VMEM((2 * rps * GRID_W, 1), f32)],
        compiler_params=_cparams(("parallel", "parallel", "arbitrary")),
        name="natten",
    )(q_rot, qn, k_rot, v, kc, vc, table)


def _ctx_attn_kernel(q_ref, k_ref, v_ref, o_ref):
    scale = NA_HD ** -0.5
    q = q_ref[0]
    k = k_ref[0]
    v = v_ref[0]
    lane = lax.broadcasted_iota(i32, q.shape, 1)
    res = []
    for hh in range(2):
        m = (lane >= NA_HD * hh) & (lane < NA_HD * (hh + 1))
        s = _nt(jnp.where(m, q, jnp.zeros_like(q)), k) * scale
        p = jnp.exp(s - jnp.max(s, axis=-1, keepdims=True))
        p = p / jnp.sum(p, axis=-1, keepdims=True)
        res.append(jnp.dot(p.astype(bf16), v, preferred_element_type=f32))
    o_ref[0] = jnp.where(lane < NA_HD, res[0], res[1]).astype(o_ref.dtype)


def context_attention(q, k, v):
    B, N, W = q.shape
    spec = pl.BlockSpec((1, N, LANES), lambda b, p: (b, 0, p))
    return pl.pallas_call(
        _ctx_attn_kernel,
        grid=(B, W // LANES),
        in_specs=[spec, spec, spec],
        out_specs=spec,
        out_shape=jax.ShapeDtypeStruct((B, N, W), bf16),
        compiler_params=_cparams(("parallel", "parallel")),
        name="ctx_attn",
    )(q, k, v)


def _filter_kernel(z_ref, w1_ref, b1_ref, f1_ref, w2_ref, b2_ref, f2_ref, w3_ref, dl_ref, k_ref, nrm_ref, *, tm, n):
    i = pl.program_id(0)
    z = z_ref[...]
    hid = HY_FILT_HID
    dot = lambda a, b: jnp.dot(a, b, precision=HI, preferred_element_type=f32)
    w1 = w1_ref[...]
    a = jnp.sin(f1_ref[...] * (dot(z[:tm // 2], w1[:, :LANES]) + dot(z[tm // 2:], w1[:, LANES:]) + b1_ref[...]))
    a = jnp.sin(f2_ref[...] * (dot(a, w2_ref[...]) + b2_ref[...]))
    w3 = w3_ref[...]
    zero = jnp.zeros_like(w3)
    h = jnp.concatenate([dot(a, jnp.concatenate([w3, zero], axis=0)), dot(a, jnp.concatenate([zero, w3], axis=0))],
                        axis=0)
    h = h * jnp.exp(-z[:, 0:1] * dl_ref[...])
    row = lax.broadcasted_iota(i32, (tm, HY_W), 0) + i * tm
    k = jnp.where(row == n, 0.0, h)
    k_ref[...] = k
    part = jnp.sum(jnp.abs(k), axis=0, keepdims=True)

    @pl.when(i == 0)
    def _():
        nrm_ref[...] = part

    @pl.when(i > 0)
    def _():
        nrm_ref[...] = nrm_ref[...] + part


def hyena_filter(n, w1, b1, fr1, w2, b2, fr2, w3):
    t = np.linspace(0.0, 1.0, n)[:, None]
    w = 2 * math.pi * np.arange(n)[:, None] / n
    fb = np.linspace(1e-4, HY_BANDS - 1, HY_BANDS)[None]
    z = np.concatenate([t, np.cos(fb * w), -np.sin(fb * w)], axis=-1)
    z = np.concatenate([z, np.zeros((1, HY_PE_DIM)), z[:0:-1]], axis=0)
    z = jnp.asarray(np.pad(z, ((0, 0), (0, LANES - HY_PE_DIM))).astype(np.float32))
    w1p = jnp.pad(w1.astype(f32), ((0, LANES - HY_PE_DIM), (0, 0)))
    deltas = jnp.asarray(np.abs(np.linspace(math.log(HY_TARGET) / HY_SLOW_DECAY, math.log(HY_TARGET) / HY_FAST_DECAY,
                                            2 * HY_W))[None].astype(np.float32))
    tm = min(n, 512)
    hid = HY_FILT_HID
    full = lambda shape: pl.BlockSpec(shape, lambda i: (0, 0))
    tph = n // tm
    zpad = jnp.zeros((LANES, hid), f32)
    w1pk = jnp.concatenate([w1p, zpad, zpad, w1p], axis=1)
    w2f = w2.astype(f32)
    z2 = jnp.zeros((hid, hid), f32)
    w2pk = jnp.concatenate([jnp.concatenate([w2f, z2], axis=1), jnp.concatenate([z2, w2f], axis=1)], axis=0)
    twice = lambda v: jnp.tile(v.reshape(1, hid).astype(f32), (1, 2))
    return pl.pallas_call(
        functools.partial(_filter_kernel, tm=tm, n=n),
        grid=(2 * n // tm,),
        in_specs=[pl.BlockSpec((tm, LANES), lambda i: (i, 0)),
                  full((LANES, 2 * LANES)), full((1, LANES)), full((1, LANES)),
                  full((LANES, LANES)), full((1, LANES)), full((1, LANES)),
                  pl.BlockSpec((hid, HY_W), lambda i: (0, i // tph)),
                  pl.BlockSpec((1, HY_W), lambda i: (0, i // tph))],
        out_specs=[pl.BlockSpec((tm, HY_W), lambda i: (i, 0)), pl.BlockSpec((1, HY_W), lambda i: (0, 0))],
        out_shape=[jax.ShapeDtypeStruct((2 * n, HY_W), f32), jax.ShapeDtypeStruct((1, HY_W), f32)],
        compiler_params=_cparams(("arbitrary",)),
        name="hyena_filter",
    )(z, w1pk, twice(b1), twice(fr1), w2pk, twice(b2), twice(fr2), w3.astype(f32), deltas)


def _conv3(u, w_ref, b_ref):
    n = u.shape[0]
    row = lax.broadcasted_iota(i32, u.shape, 0)
    prev = jnp.where(row == 0, 0.0, pltpu.roll(u, 1, axis=0))
    nxt = jnp.where(row == n - 1, 0.0, pltpu.roll(u, n - 1, axis=0))
    return prev * w_ref[0:1, :] + u * w_ref[1:2, :] + nxt * w_ref[2:3, :] + b_ref[...]


def _hyena_pre_kernel(p0_ref, p1_ref, p2_ref, w0_ref, w1_ref, w2_ref, b0_ref, b1_ref, b2_ref, z_ref, x0_ref):
    x0_ref[0] = _conv3(p0_ref[0], w0_ref, b0_ref).astype(x0_ref.dtype)
    z_ref[0] = _conv3(p1_ref[0], w1_ref, b1_ref) * _conv3(p2_ref[0], w2_ref, b2_ref)


def hyena_pre(p, conv_w, conv_b):
    B, N, _ = p.shape
    nb = HY_W // LANES
    conv_b = conv_b.reshape(1, 3 * HY_W)
    pspec = lambda g: pl.BlockSpec((1, N, LANES), lambda b, c: (b, 0, g * nb + c))
    wspec = lambda g: pl.BlockSpec((3, LANES), lambda b, c: (0, g * nb + c))
    bspec = lambda g: pl.BlockSpec((1, LANES), lambda b, c: (0, g * nb + c))
    ospec = pl.BlockSpec((1, N, LANES), lambda b, c: (b, 0, c))
    return pl.pallas_call(
        _hyena_pre_kernel,
        grid=(B, nb),
        in_specs=[pspec(0), pspec(1), pspec(2), wspec(0), wspec(1), wspec(2), bspec(0), bspec(1), bspec(2)],
        out_specs=[ospec, ospec],
        out_shape=[jax.ShapeDtypeStruct((B, N, HY_W), f32), jax.ShapeDtypeStruct((B, N, HY_W), bf16)],
        compiler_params=_cparams(("parallel", "parallel")),
        name="hyena_pre",
    )(p, p, p, conv_w, conv_w, conv_w, conv_b, conv_b, conv_b)


DFT_SLABS = 8


def _hl(a):
    a32 = jnp.asarray(a.astype(np.float32))
    hi = a32.astype(bf16)
    lo = (a32 - hi.astype(f32)).astype(bf16)
    return jnp.concatenate([hi, lo], axis=-2)


def _dot3(a_hl, m, x):
    xh, xl = _split(x)
    r = jnp.dot(a_hl, xh, preferred_element_type=f32)
    return r[:m] + r[m:] + jnp.dot(a_hl[:m], xl, preferred_element_type=f32)


def _dft_consts(n):
    N = 2 * n
    na = N // LANES
    t1n = na // 2
    k1n = na // 2 + 1
    k1p = -(-k1n // 8) * 8
    k1 = np.arange(k1n)
    t1 = np.arange(t1n)
    th = 2 * np.pi * ((t1[None, :] * k1[:, None]) % na) / na
    f1c = np.zeros((2 * k1p, t1n))
    f1c[:k1n] = np.cos(th)
    f1c[k1p:k1p + k1n] = -np.sin(th)
    thf = 2 * np.pi * ((np.arange(na)[None, :] * k1[:, None]) % na) / na
    f1f = np.zeros((2 * k1p, na))
    f1f[:k1n] = np.cos(thf)
    f1f[k1p:k1p + k1n] = -np.sin(thf)
    k2 = np.arange(LANES)
    t2 = np.arange(LANES)
    m = (t2[None, None, :] * (k1[:, None, None] + na * k2[None, :, None])) % N
    ph = 2 * np.pi * m / N
    g = np.concatenate([np.cos(ph), -np.sin(ph)], axis=1)
    pht = ph.transpose(0, 2, 1)
    gi = np.concatenate([np.cos(pht), np.sin(pht)], axis=1)
    wk = np.where((k1 == 0) | (k1 == na // 2), 1.0, 2.0) / N
    k1h = na // 2
    f1i = np.concatenate([np.cos(th.T)[:, :k1h] * wk[None, :k1h], -np.sin(th.T)[:, :k1h] * wk[None, :k1h]], axis=1)
    k1e = -(-k1n // DFT_SLABS) * DFT_SLABS
    g = np.concatenate([g, np.zeros((k1e - k1n,) + g.shape[1:])], axis=0)
    gi = np.concatenate([gi, np.zeros((k1e - k1n,) + gi.shape[1:])], axis=0)
    return dict(na=na, t1n=t1n, k1n=k1n, k1e=k1e, k1p=k1p, f1c=_hl(f1c), f1f=_hl(f1f), g=_hl(g), gi=_hl(gi),
                f1i=_hl(f1i))


def _dft_stage1(src_ref, f1c_ref, tre_ref, tim_ref, t1n, k1p):
    f1c = f1c_ref[...]

    def body(t2, carry):
        zs = src_ref[pl.ds(t2, t1n, stride=LANES), :]
        r = _dot3(f1c, 2 * k1p, zs)
        r0 = pl.multiple_of(t2 * k1p, 8)
        tre_ref[pl.ds(r0, k1p), :] = r[:k1p]
        tim_ref[pl.ds(r0, k1p), :] = r[k1p:]
        return carry

    lax.fori_loop(0, LANES, body, 0, unroll=4)


def _slab(tre_ref, tim_ref, k1, k1p):
    return tre_ref[pl.ds(k1, LANES, stride=k1p), :], tim_ref[pl.ds(k1, LANES, stride=k1p), :]


def _cplx_left(gc_hl, xre, xim):
    cw = xre.shape[1]
    r = _dot3(gc_hl, 2 * LANES, jnp.concatenate([xre, xim], axis=1))
    p, q = r[:, :cw], r[:, cw:]
    return p[:LANES] - q[LANES:], p[LANES:] + q[:LANES]


def _spectrum_kernel(k_ref, f1f_ref, g_ref, inv_ref, xre_ref, xim_ref, are, aim, *, na, k1p):
    j = pl.program_id(1)

    @pl.when(j == 0)
    def _():
        _dft_stage1(k_ref, f1f_ref, are, aim, na, k1p)

    for half in range(DFT_SLABS):
        xre, xim = _cplx_left(g_ref[half], *_slab(are, aim, DFT_SLABS * j + half, k1p))
        rows = slice(half * LANES, (half + 1) * LANES)
        xre_ref[rows, :] = xre * inv_ref[...]
        xim_ref[rows, :] = xim * inv_ref[...]


def hyena_spectrum(k, inv_norm, dc):
    n2, C = k.shape
    k1e, k1p, na = dc["k1e"], dc["k1p"], dc["na"]
    cw = LANES
    out = jax.ShapeDtypeStruct((k1e * LANES, C), f32)
    ospec = pl.BlockSpec((DFT_SLABS * LANES, cw), lambda c, k: (k, c))
    return pl.pallas_call(
        functools.partial(_spectrum_kernel, na=na, k1p=k1p),
        grid=(C // cw, k1e // DFT_SLABS),
        in_specs=[pl.BlockSpec((n2, cw), lambda c, k: (0, c)),
                  pl.BlockSpec(dc["f1f"].shape, lambda c, k: (0, 0)),
                  pl.BlockSpec((DFT_SLABS, 4 * LANES, LANES), lambda c, k: (k, 0, 0)),
                  pl.BlockSpec((1, cw), lambda c, k: (0, c))],
        out_specs=[ospec, ospec],
        out_shape=[out, out],
        scratch_shapes=[pltpu.VMEM((k1p * LANES, cw), f32), pltpu.VMEM((k1p * LANES, cw), f32)],
        compiler_params=_cparams(("parallel", "arbitrary")),
        name="hyena_spectrum",
    )(k, dc["f1f"], dc["g"], inv_norm)


def _hyena_conv_kernel(z_ref, x0_ref, f1c_ref, g_ref, gi_ref, f1i_ref, kre_ref, kim_ref,
                       skip_ref, o_ref, tre, tim, are, aim, y_scr, *, t1n, k1p):
    j = pl.program_id(2)

    @pl.when(j == 0)
    def _():
        _dft_stage1(z_ref.at[0], f1c_ref, tre, tim, t1n, k1p)

    for half in range(DFT_SLABS):
        r0 = pl.multiple_of((DFT_SLABS * j + half) * LANES, LANES)
        xre, xim = _cplx_left(g_ref[half], *_slab(tre, tim, DFT_SLABS * j + half, k1p))
        rows = slice(half * LANES, (half + 1) * LANES)
        kre = kre_ref[rows, :]
        kim = kim_ref[rows, :]
        yre = xre * kre - xim * kim
        yim = xre * kim + xim * kre
        bre, bim = _cplx_left(gi_ref[half], yre, yim)
        are[pl.ds(r0, LANES), :] = bre
        aim[pl.ds(r0, LANES), :] = bim

    @pl.when(j == pl.num_programs(2) - 1)
    def _():
        f1i = f1i_ref[...]
        k1h = t1n

        def body(t2, carry):
            bb = jnp.concatenate([are[pl.ds(t2, k1h, stride=LANES), :], aim[pl.ds(t2, k1h, stride=LANES), :]], axis=0)
            y_scr[pl.ds(t2, t1n, stride=LANES), :] = _dot3(f1i, t1n, bb)
            return carry

        lax.fori_loop(0, LANES, body, 0, unroll=4)
        r0 = k1h * LANES
        nyq = are[r0:r0 + LANES, :] * (1.0 / (2 * t1n * LANES))
        nyq = jnp.concatenate([nyq, -nyq] * (t1n // 2), axis=0)
        z = z_ref[0]
        o_ref[0] = (x0_ref[0].astype(f32) * (y_scr[...] + nyq + z * skip_ref[...])).astype(o_ref.dtype)


def hyena_conv(z, x0, spec_re, spec_im, skip, dc):
    B, n, W = z.shape
    nb = W // LANES
    k1e, k1p, t1n = dc["k1e"], dc["k1p"], dc["t1n"]
    seq = pl.BlockSpec((1, n, LANES), lambda b, c, k: (b, 0, c))
    fspec = pl.BlockSpec((DFT_SLABS * LANES, LANES), lambda b, c, k: (k, c))
    cspec = pl.BlockSpec((DFT_SLABS, 4 * LANES, LANES), lambda b, c, k: (k, 0, 0))
    vspec = pl.BlockSpec((1, LANES), lambda b, c, k: (0, c))
    return pl.pallas_call(
        functools.partial(_hyena_conv_kernel, t1n=t1n, k1p=k1p),
        grid=(B, nb, k1e // DFT_SLABS),
        in_specs=[seq, seq,
                  pl.BlockSpec(dc["f1c"].shape, lambda b, c, k: (0, 0)), cspec, cspec,
                  pl.BlockSpec(dc["f1i"].shape, lambda b, c, k: (0, 0)),
                  fspec, fspec, vspec],
        out_specs=seq,
        out_shape=jax.ShapeDtypeStruct((B, n, W), bf16),
        scratch_shapes=[pltpu.VMEM((k1p * LANES, LANES), f32)] * 4 + [pltpu.VMEM((n, LANES), f32)],
        compiler_params=_cparams(("parallel", "parallel", "arbitrary")),
        name="hyena_conv",
    )(z, x0, dc["f1c"], dc["g"], dc["gi"], dc["f1i"], spec_re, spec_im, skip)


def _hyena_ctx_kernel(p0_ref, p1_ref, p2_ref, w0_ref, w1_ref, w2_ref, b0_ref, b1_ref, b2_ref,
                      k_ref, inv_ref, skip_ref, fd_ref, fi_ref, o_ref, *, n):
    x0 = _conv3(p0_ref[0], w0_ref, b0_ref)
    z = _conv3(p1_ref[0], w1_ref, b1_ref) * _conv3(p2_ref[0], w2_ref, b2_ref)
    fd = fd_ref[...]
    N = 2 * n
    zf = jnp.dot(fd[:, :n], z, precision=HI, preferred_element_type=f32)
    kf = jnp.dot(fd, k_ref[...], precision=HI, preferred_element_type=f32) * inv_ref[...]
    yre = zf[:N] * kf[:N] - zf[N:] * kf[N:]
    yim = zf[:N] * kf[N:] + zf[N:] * kf[:N]
    y = jnp.dot(fi_ref[...], jnp.concatenate([yre, yim], axis=0), precision=HI, preferred_element_type=f32)
    o_ref[0] = (x0 * (y + z * skip_ref[...])).astype(o_ref.dtype)


def hyena_ctx(p, conv_w, conv_b, k, inv_norm, skip):
    B, n, _ = p.shape
    N = 2 * n
    nb = HY_W // LANES
    kk = np.arange(N)
    ph = 2 * np.pi * ((kk[:, None] * kk[None, :]) % N) / N
    fd = jnp.asarray(np.concatenate([np.cos(ph), -np.sin(ph)], axis=0).astype(np.float32))
    fi = jnp.asarray((np.concatenate([np.cos(ph[:n]), -np.sin(ph[:n])], axis=1) / N).astype(np.float32))
    conv_b = conv_b.reshape(1, 3 * HY_W)
    pspec = lambda g: pl.BlockSpec((1, n, LANES), lambda b, c: (b, 0, g * nb + c))
    wspec = lambda g: pl.BlockSpec((3, LANES), lambda b, c: (0, g * nb + c))
    bspec = lambda g: pl.BlockSpec((1, LANES), lambda b, c: (0, g * nb + c))
    vspec = pl.BlockSpec((1, LANES), lambda b, c: (0, c))
    return pl.pallas_call(
        functools.partial(_hyena_ctx_kernel, n=n),
        grid=(B, nb),
        in_specs=[pspec(0), pspec(1), pspec(2), wspec(0), wspec(1), wspec(2), bspec(0), bspec(1), bspec(2),
                  pl.BlockSpec((N, LANES), lambda b, c: (0, c)),
                  vspec, vspec,
                  pl.BlockSpec(fd.shape, lambda b, c: (0, 0)), pl.BlockSpec(fi.shape, lambda b, c: (0, 0))],
        out_specs=pl.BlockSpec((1, n, LANES), lambda b, c: (b, 0, c)),
        out_shape=jax.ShapeDtypeStruct((B, n, HY_W), bf16),
        compiler_params=_cparams(("parallel", "parallel")),
        name="hyena_ctx",
    )(p, p, p, conv_w, conv_w, conv_w, conv_b, conv_b, conv_b, k, inv_norm, skip, fd, fi)


def _merge_kernel(of_ref, ob_ref, gs_ref, nb_ref, hc_ref, g_ref, wa_ref, wb_ref, wc_ref, wo_ref, x_ref, m_ref, o_ref):
    d = D_MODEL
    tot = of_ref[...] + ob_ref[...]
    gs = gs_ref[...].astype(f32)
    ra = []
    for h in range(HG_HEADS):
        sl = slice(LANES * h, LANES * (h + 1))
        th = tot[:, sl]
        ms = jnp.mean(th * th, axis=-1, keepdims=True)
        ra.append(th * lax.rsqrt(ms + EPS) * gs[:, sl])
    ya = jnp.dot(jnp.concatenate(ra, axis=1).astype(bf16), wa_ref[...], preferred_element_type=f32)
    yb = jnp.dot(nb_ref[...], wb_ref[...], preferred_element_type=f32)
    yc = jnp.dot(hc_ref[...], wc_ref[...], preferred_element_type=f32)
    g = g_ref[...].astype(f32)
    mix = g[:, :d] * ya + g[:, d:2 * d] * yb + g[:, 2 * d:] * yc
    y = jnp.dot(mix.astype(bf16), wo_ref[...], preferred_element_type=f32)
    o_ref[...] = x_ref[...] + m_ref[0] * y


def merge(o_f, o_b, gs, nb, hc, gates, wa, wb, wc, wo, x2d, m, rows_per_group):
    R = x2d.shape[0]
    tm = 512
    tpg = rows_per_group // tm
    G = m.shape[0]
    row = lambda w: pl.BlockSpec((tm, w), lambda i: (i, 0))
    full = lambda a: pl.BlockSpec(a.shape, lambda i: (0, 0))
    return pl.pallas_call(
        _merge_kernel,
        grid=(R // tm,),
        in_specs=[row(HG_W), row(HG_W), row(HG_W), row(NA_W), row(HY_W), row(3 * D_MODEL),
                  full(wa), full(wb), full(wc), full(wo),
                  row(D_MODEL), pl.BlockSpec((1, 1, D_MODEL), lambda i: (i // tpg, 0, 0))],
        out_specs=row(D_MODEL),
        out_shape=jax.ShapeDtypeStruct((R, D_MODEL), f32),
        compiler_params=_cparams(("parallel",)),
        name="merge",
    )(o_f, o_b, gs, nb, hc, gates, wa, wb, wc, wo, x2d, m.reshape(G, 1, D_MODEL))


def _router_kernel(x_ref, g_ref, sh_ref, sc_ref, wrt_ref, wr_ref, h_ref, at_ref, am_ref):
    x = x_ref[...]
    ms = jnp.mean(x * x, axis=-1, keepdims=True)
    h = x * lax.rsqrt(ms + EPS) * g_ref[...] * (1.0 + sc_ref[0]) + sh_ref[0]
    h_ref[...] = h.astype(h_ref.dtype)
    hh, hl = _split(h)
    ne = N_EXPERTS
    wt = wrt_ref[...]
    rt = _nt(wt, hh)
    lt = rt[:ne] + rt[ne:] + _nt(wt[:ne], hl)
    et = jnp.exp(lt - jnp.max(lt, axis=0, keepdims=True))
    at_ref[0] = et / jnp.sum(et, axis=0, keepdims=True)
    wm = wr_ref[...]
    rm = jnp.dot(hh, wm, preferred_element_type=f32)
    lm = rm[:, :ne] + rm[:, ne:] + jnp.dot(hl, wm[:, :ne], preferred_element_type=f32)
    em = jnp.exp(lm - jnp.max(lm, axis=1, keepdims=True))
    am_ref[...] = em / jnp.sum(em, axis=1, keepdims=True)


def router(x2d, g, shift, scale, w_router, n_per_set):
    R = x2d.shape[0]
    tm = min(512, n_per_set)
    tps = n_per_set // tm
    S = R // n_per_set
    G = shift.shape[0]
    gmap = (lambda i: (i // tps, 0, 0)) if G > 1 else (lambda i: (0, 0, 0))
    whi, wlo = _split(w_router.astype(f32))
    wr = jnp.concatenate([whi, wlo], axis=1)
    return pl.pallas_call(
        _router_kernel,
        grid=(R // tm,),
        in_specs=[pl.BlockSpec((tm, D_MODEL), lambda i: (i, 0)),
                  pl.BlockSpec((1, D_MODEL), lambda i: (0, 0)),
                  pl.BlockSpec((1, 1, D_MODEL), gmap),
                  pl.BlockSpec((1, 1, D_MODEL), gmap),
                  pl.BlockSpec((2 * N_EXPERTS, D_MODEL), lambda i: (0, 0)),
                  pl.BlockSpec((D_MODEL, 2 * N_EXPERTS), lambda i: (0, 0))],
        out_specs=[pl.BlockSpec((tm, D_MODEL), lambda i: (i, 0)),
                   pl.BlockSpec((1, N_EXPERTS, tm), lambda i: (i // tps, 0, i % tps)),
                   pl.BlockSpec((tm, N_EXPERTS), lambda i: (i, 0))],
        out_shape=[jax.ShapeDtypeStruct((R, D_MODEL), bf16),
                   jax.ShapeDtypeStruct((S, N_EXPERTS, n_per_set), f32),
                   jax.ShapeDtypeStruct((R, N_EXPERTS), f32)],
        compiler_params=_cparams(("parallel",)),
        name="router",
    )(x2d, g.reshape(1, D_MODEL), shift.reshape(G, 1, D_MODEL), scale.reshape(G, 1, D_MODEL), wr.T, wr)


SEL_BLK = 256
SUB = LANES
SUBW = SUB + 8
UNSEL = -float(2 ** 30)


def _prefix_incl(mask_f, tri, T):
    outs = []
    off = jnp.zeros((mask_f.shape[0], 1), f32)
    for b in range(T // SEL_BLK):
        blk = mask_f[:, b * SEL_BLK:(b + 1) * SEL_BLK].astype(bf16)
        pre = jnp.dot(blk, tri, preferred_element_type=f32) + off
        outs.append(pre)
        off = pre[:, SEL_BLK - 1:SEL_BLK]
    return jnp.concatenate(outs, axis=1)


def _select_kernel(a_ref, tri_ref, cm_ref, posm_ref, cnt_ref, *, T, cap):
    aff = a_ref[0]
    bits = pltpu.bitcast(aff, i32)
    tri = tri_ref[...]

    def bit_step(i, thr):
        cand = thr | (1 << (30 - i))
        cnt = jnp.sum((bits >= cand).astype(f32), axis=1, keepdims=True)
        return jnp.where(cnt >= cap, cand, thr)

    thr = lax.fori_loop(0, 31, bit_step, jnp.zeros((N_EXPERTS, 1), i32))
    gt = bits > thr
    eq = bits == thr
    need = cap - jnp.sum(gt.astype(f32), axis=1, keepdims=True)
    eqf = eq.astype(f32)
    rank_eq = _prefix_incl(eqf, tri, T) - eqf
    sel = gt | (eq & (rank_eq < need))
    self_ = sel.astype(f32)
    pos = _prefix_incl(self_, tri, T) - self_
    posm_ref[0] = jnp.where(sel, pos, UNSEL)
    cnt_ref[0] = jnp.dot(self_.astype(bf16), cm_ref[...], preferred_element_type=f32).astype(i32)


def select_topk(aff, cap):
    S, E, T = aff.shape
    tri = jnp.asarray(np.triu(np.ones((SEL_BLK, SEL_BLK), np.float32)), bf16)
    cm = jnp.asarray((np.arange(T)[:, None] < np.arange(LANES)[None, :] * SUB).astype(np.float32), bf16)
    return pl.pallas_call(
        functools.partial(_select_kernel, T=T, cap=cap),
        grid=(S,),
        in_specs=[pl.BlockSpec((1, E, T), lambda s: (s, 0, 0)),
                  pl.BlockSpec((SEL_BLK, SEL_BLK), lambda s: (0, 0)),
                  pl.BlockSpec((T, LANES), lambda s: (0, 0))],
        out_specs=[pl.BlockSpec((1, E, T), lambda s: (s, 0, 0)),
                   pl.BlockSpec((1, E, LANES), lambda s: (s, 0, 0))],
        out_shape=[jax.ShapeDtypeStruct((S, E, T), f32), jax.ShapeDtypeStruct((S, E, LANES), i32)],
        compiler_params=_cparams(("parallel",)),
        name="select_topk",
    )(aff, tri, cm)


def _align8(v):
    return lax.shift_left(lax.shift_right_logical(v, 3), 3)


def _align16(v):
    return lax.shift_left(lax.shift_right_logical(v, 4), 4)


CMB_ROWS = SUB + 16


def _gather_kernel(cnt_ref, h_ref, pos_ref, o_ref, acc, *, TT, cap, tps, R, srows, EP):
    tl = pl.program_id(1)

    @pl.when(tl == 0)
    def _():
        acc[...] = jnp.zeros_like(acc)

    st = tl // tps
    nsub = TT // SUB
    rid = lax.broadcasted_iota(i32, (srows, SUB), 0).astype(f32)
    for ep in range(EP):
        e = pl.program_id(0) * EP + ep
        cbase = (st * N_EXPERTS + e) * LANES + (tl % tps) * nsub
        for s in range(nsub):
            off8 = _align8(cnt_ref[cbase + s])
            pos = pos_ref[0, ep, :, s * SUB:(s + 1) * SUB]
            onehot = jnp.where(pos == rid + off8.astype(f32), 1.0, 0.0).astype(bf16)
            rows = jnp.dot(onehot, h_ref[s * SUB:(s + 1) * SUB, :], preferred_element_type=f32)
            r0 = pl.multiple_of(st * cap + off8, 8)
            acc[ep, pl.ds(r0, srows), :] += rows

    @pl.when(tl == pl.num_programs(1) - 1)
    def _():
        o_ref[...] = acc[:, 0:R, :].astype(o_ref.dtype)


def _gather_call(cnt, h, posm, cap, TT, srows, EP):
    S, E, T = posm.shape
    tps = T // TT
    R = S * cap
    gs = pltpu.PrefetchScalarGridSpec(
        num_scalar_prefetch=1,
        grid=(E // EP, S * tps),
        in_specs=[pl.BlockSpec((TT, D_MODEL), lambda e, t, c: (t, 0)),
                  pl.BlockSpec((1, EP, 1, TT), lambda e, t, c: (t // tps, e, 0, t % tps))],
        out_specs=pl.BlockSpec((EP, R, D_MODEL), lambda e, t, c: (e, 0, 0)),
        scratch_shapes=[pltpu.VMEM((EP, R + srows, D_MODEL), f32)])
    return pl.pallas_call(
        functools.partial(_gather_kernel, TT=TT, cap=cap, tps=tps, R=R, srows=srows, EP=EP),
        grid_spec=gs,
        out_shape=jax.ShapeDtypeStruct((E, R, D_MODEL), bf16),
        compiler_params=_cparams(("parallel", "arbitrary")),
        name="moe_gather",
    )(cnt.reshape(-1), h, posm.reshape(S, E, 1, T))


FAST_SUB_MAX = 48
FAST_TILE_MAX = 192
FAST_ROWS = 64
CMB_TILE = 1024


def _fits_fast(cnt, T):
    nsub = T // SUB
    per_sub = cnt[..., 1:nsub + 1] - cnt[..., :nsub]
    k = CMB_TILE // SUB
    per_tile = cnt[..., k:nsub + 1:k] - cnt[..., 0:nsub:k]
    return (jnp.max(per_sub) <= FAST_SUB_MAX) & (jnp.max(per_tile) <= FAST_TILE_MAX)


def gather_rows(cnt, h, posm, cap, TT, fast_ok=None):
    safe = lambda: _gather_call(cnt, h, posm, cap, TT, SUBW, 1)
    if fast_ok is None:
        return safe()
    return lax.cond(fast_ok, lambda: _gather_call(cnt, h, posm, cap, TT, FAST_ROWS, 2), safe)


EXPERT_TF = 256


def _ffn_kernel(*refs, n):
    xs = refs[:n]
    wg_ref, wu_ref, wd_ref = refs[n:n + 3]
    his = refs[n + 3:2 * n + 3]
    los = refs[2 * n + 3:3 * n + 3]
    accs = refs[3 * n + 3:]
    j = pl.program_id(1)
    wg = wg_ref[0, 0].astype(bf16)
    wu = wu_ref[0, 0].astype(bf16)
    wd = wd_ref[0, 0].astype(bf16)
    @pl.when(j == 0)
    def _():
        for acc in accs:
            acc[...] = jnp.zeros_like(acc)

    for x_ref, hi_ref, lo_ref, acc in zip(xs, his, los, accs):
        x = x_ref[0]
        a = jnp.dot(x, wg, preferred_element_type=f32)
        u = jnp.dot(x, wu, preferred_element_type=f32)
        acc[...] += jnp.dot((_silu(a) * u).astype(bf16), wd, preferred_element_type=f32)

        @pl.when(j == pl.num_programs(1) - 1)
        def _(acc=acc, hi_ref=hi_ref, lo_ref=lo_ref):
            hi, lo = _split(acc[...])
            hi_ref[0] = hi
            lo_ref[0] = lo


def expert_ffn(xgs, layer, w_gate, w_up, w_down):
    E = xgs[0].shape[0]
    nf = D_FF_EXPERT // EXPERT_TF
    n = len(xgs)
    rowspec = lambda a: pl.BlockSpec((1, a.shape[1], D_MODEL), lambda e, j: (e, 0, 0))
    res = pl.pallas_call(
        functools.partial(_ffn_kernel, n=n),
        grid=(E, nf),
        in_specs=[rowspec(a) for a in xgs] + [
            pl.BlockSpec((1, 1, D_MODEL, EXPERT_TF), lambda e, j: (layer, e, 0, j)),
            pl.BlockSpec((1, 1, D_MODEL, EXPERT_TF), lambda e, j: (layer, e, 0, j)),
            pl.BlockSpec((1, 1, EXPERT_TF, D_MODEL), lambda e, j: (layer, e, j, 0))],
        out_specs=[rowspec(a) for a in xgs] * 2,
        out_shape=[jax.ShapeDtypeStruct(a.shape, bf16) for a in xgs] * 2,
        scratch_shapes=[pltpu.VMEM(a.shape[1:], f32) for a in xgs],
        compiler_params=_cparams(("parallel", "arbitrary")),
        name="expert_ffn",
    )(*xgs, w_gate, w_up, w_down)
    return [(res[i], res[n + i]) for i in range(n)]


def _combine_kernel(cnt_ref, x_ref, pos_ref, am_ref, m_ref, *rest, TT, cap, R, W, ytot, crows, EP):
    y_refs, o_ref = rest[:2 * EP], rest[2 * EP]
    st = pl.program_id(0)
    tl = pl.program_id(1)
    eg = pl.program_id(2)

    @pl.when(eg == 0)
    def _():
        o_ref[...] = x_ref[...]

    nsub = TT // SUB
    lane = lax.broadcasted_iota(i32, (TT, N_EXPERTS), 1)
    m5 = m_ref[0]
    rid = lax.broadcasted_iota(i32, (crows, SUB), 0).astype(f32)
    am = am_ref[...]
    gcols, wss = [], []
    for ep in range(EP):
        e = eg * EP + ep
        cbase = (st * N_EXPERTS + e) * LANES + tl * nsub
        wss.append(jnp.minimum(e * R + st * cap + _align16(cnt_ref[cbase]), ytot - W))
        gcols.append(jnp.sum(jnp.where(lane == e, am, 0.0), axis=1, keepdims=True))
    for s in range(nsub):
        sl = slice(s * SUB, (s + 1) * SUB)
        tot = None
        for ep in range(EP):
            e = eg * EP + ep
            rowbase = e * R + st * cap
            off = _align16(cnt_ref[(st * N_EXPERTS + e) * LANES + tl * nsub + s])
            rel = pl.multiple_of(jnp.minimum(rowbase + off - wss[ep], W - crows), 16)
            first = (wss[ep] + rel - rowbase).astype(f32)
            pos = pos_ref[0, ep, :, sl]
            onehot = jnp.where(pos == rid + first, 1.0, 0.0).astype(bf16)
            ywin = jnp.concatenate([y_refs[2 * ep][pl.ds(rel, crows), :], y_refs[2 * ep + 1][pl.ds(rel, crows), :]],
                                   axis=0)
            picked = _tn(jnp.concatenate([onehot, onehot], axis=0), ywin)
            term = gcols[ep][sl] * picked
            tot = term if tot is None else tot + term
        o_ref[sl, :] += m5 * tot


def _combine_call(cnt, x2d, posm, aff_tm, mvec, y_hl, cap, TT, crows, W, EP):
    S, E, T = posm.shape
    tps = T // TT
    R = S * cap
    ytot = E * R
    nsub = TT // SUB
    G = mvec.shape[0]

    def yspec(ep):
        def ymap(st, tl, eg, c):
            e = eg * EP + ep
            off = _align16(c[(st * E + e) * LANES + tl * nsub])
            return (pl.multiple_of(jnp.minimum(e * R + st * cap + off, ytot - W), 16), 0)
        return pl.BlockSpec((pl.Element(W), pl.Element(D_MODEL)), ymap)

    tok = lambda w: pl.BlockSpec((TT, w), lambda st, tl, e, c: (st * tps + tl, 0))
    mmap = (lambda st, tl, e, c: (st, 0, 0)) if G > 1 else (lambda st, tl, e, c: (0, 0, 0))
    gs = pltpu.PrefetchScalarGridSpec(
        num_scalar_prefetch=1,
        grid=(S, tps, E // EP),
        in_specs=[tok(D_MODEL),
                  pl.BlockSpec((1, EP, 1, TT), lambda st, tl, e, c: (st, e, 0, tl)),
                  tok(N_EXPERTS),
                  pl.BlockSpec((1, 1, D_MODEL), mmap)] + [yspec(ep) for ep in range(EP) for _ in range(2)],
        out_specs=tok(D_MODEL))
    yh, yl = y_hl[0].reshape(ytot, D_MODEL), y_hl[1].reshape(ytot, D_MODEL)
    return pl.pallas_call(
        functools.partial(_combine_kernel, TT=TT, cap=cap, R=R, W=W, ytot=ytot, crows=crows, EP=EP),
        grid_spec=gs,
        out_shape=jax.ShapeDtypeStruct(x2d.shape, f32),
        compiler_params=_cparams(("parallel", "parallel", "arbitrary")),
        name="moe_combine",
    )(cnt.reshape(-1), x2d, posm.reshape(S, E, 1, T), aff_tm, mvec.reshape(G, 1, D_MODEL), *([yh, yl] * EP))


def combine(cnt, x2d, posm, aff_tm, mvec, y_hl, cap, TT, fast_ok=None):
    safe = lambda: _combine_call(cnt, x2d, posm, aff_tm, mvec, y_hl, cap, TT, CMB_ROWS, TT + 32, 1)
    if fast_ok is None:
        return safe()
    wfast = FAST_TILE_MAX + 16 + FAST_ROWS
    return lax.cond(fast_ok, lambda: _combine_call(cnt, x2d, posm, aff_tm, mvec, y_hl, cap, TT, FAST_ROWS, wfast, 4),
                    safe)


def _rope_tables(n):
    half = NA_HD // 2
    q = half // 2
    inv = ROPE_THETA ** (-np.arange(q, dtype=np.float64) / q)
    pos = np.arange(n)
    ang_r = (pos // GRID_W)[:, None] * inv
    ang_c = (pos % GRID_W)[:, None] * inv
    zero = np.zeros_like(ang_r)
    c = np.concatenate([np.cos(ang_r)] * 2 + [np.cos(ang_c)] * 2, axis=1)
    s1 = np.concatenate([-np.sin(ang_r), zero, -np.sin(ang_c), zero], axis=1)
    s2 = np.concatenate([zero, np.sin(ang_r), zero, np.sin(ang_c)], axis=1)
    two = lambda a: jnp.asarray(np.concatenate([a, a], axis=1).astype(np.float32))
    return two(c), two(s1), two(s2)


def _mixing(hx, hc, need_ctx, B, N, NC, la, lc, w_in, q_gain, k_gain, table, rope, bd, conv_w, conv_b,
            spec, skip, filt_c, wa, wb, wc, wo, x2d, c2d, mx2, mc2):
    tile8 = lambda v: jnp.tile(v.reshape(1, NA_HD), (1, NA_HEADS))
    qg, kg = tile8(q_gain), tile8(k_gain)
    norm_aux = [("col", kg), ("const", bd)]
    rope_aux = [("row", rope[0]), ("row", rope[1]), ("row", rope[2])]
    lf_aux = lambda d: [("col", la[d:d + 1]), ("col", lc[d:d + 1])]
    tc = hc.shape[0]

    lff_c = project(hc, w_in, OFF_FF, 512, _epi_logforget, lf_aux(0), (f32,), tm=tc).reshape(B, NC, 512)
    lfb_c = project(hc, w_in, OFF_FB, 512, _epi_logforget, lf_aux(1), (f32,), tm=tc).reshape(B, NC, 512)
    i_c = project(hc, w_in, OFF_I, 512, _epi_raw, tm=tc).reshape(B, NC, 512)
    k_c = project(hc, w_in, OFF_NK, 512, _epi_norm, norm_aux, tm=tc).reshape(B, NC, 512)
    v_c = project(hc, w_in, OFF_NV, 512, _epi_raw, tm=tc).reshape(B, NC, 512)
    if need_ctx:
        q_c = project(hc, w_in, OFF_HQ, 512, _epi_silu, tm=tc).reshape(B, NC, 512)
    else:
        q_c = jnp.zeros((B, NC, 512), bf16)
    s0 = jnp.zeros((B, HG_HEADS, LANES, LANES), f32)
    oc_f, oc_b, s_f, s_b = hgrn_bidir(lff_c, lfb_c, i_c, q_c, s0, s0)

    lff_x = project(hx, w_in, OFF_FF, 512, _epi_logforget, lf_aux(0), (f32,)).reshape(B, N, 512)
    lfb_x = project(hx, w_in, OFF_FB, 512, _epi_logforget, lf_aux(1), (f32,)).reshape(B, N, 512)
    i_x = project(hx, w_in, OFF_I, 512, _epi_raw).reshape(B, N, 512)
    q_x = project(hx, w_in, OFF_HQ, 512, _epi_silu).reshape(B, N, 512)
    g_x = project(hx, w_in, OFF_HG, 512, _epi_silu).reshape(B, N, 512)
    k_x = project(hx, w_in, OFF_NK, 512, _epi_norm_rope, norm_aux + rope_aux, rows_per_seq=N).reshape(B, N, 512)
    v_x = project(hx, w_in, OFF_NV, 512, _epi_raw).reshape(B, N, 512)
    qn_x, qr_x = project(hx, w_in, OFF_NQ, 512, _epi_norm_both, [("col", qg), ("const", bd)] + rope_aux,
                         (bf16, bf16), rows_per_seq=N)
    p_x = project(hx, w_in, OFF_HY, 3 * HY_W, _epi_raw, out_dtypes=(f32,)).reshape(B, N, 3 * HY_W)
    gates_x = project(hx, w_in, OFF_GATE, 3 * D_MODEL, _epi_sigmoid)

    ox_f, ox_b, _, _ = hgrn_bidir(lff_x, lfb_x, i_x, q_x, s_f, s_b)

    nb_x = neighbourhood_attention(qr_x.reshape(B, N, 512), qn_x.reshape(B, N, 512), k_x, v_x, k_c, v_c, table)

    z_x, x0_x = hyena_pre(p_x, conv_w, conv_b)
    hy_x = hyena_conv(z_x, x0_x, spec[0], spec[1], skip, spec[2])

    flat = lambda a: a.reshape(-1, a.shape[-1])
    x_new = merge(flat(ox_f), flat(ox_b), flat(g_x), flat(nb_x), flat(hy_x), gates_x, wa, wb, wc, wo, x2d, mx2, N)
    if not need_ctx:
        return x_new, None

    qn_c = project(hc, w_in, OFF_NQ, 512, _epi_norm, [("col", qg), ("const", bd)], tm=tc).reshape(B, NC, 512)
    nb_c = context_attention(qn_c, k_c, v_c)
    p_c = project(hc, w_in, OFF_HY, 3 * HY_W, _epi_raw, out_dtypes=(f32,), tm=tc).reshape(B, NC, 3 * HY_W)
    hy_c = hyena_ctx(p_c, conv_w, conv_b, filt_c[0], filt_c[1], skip)
    gates_c = project(hc, w_in, OFF_GATE, 3 * D_MODEL, _epi_sigmoid, tm=tc)
    g_c = project(hc, w_in, OFF_HG, 512, _epi_silu, tm=tc)
    c_new = merge(flat(oc_f), flat(oc_b), g_c, flat(nb_c), flat(hy_c), gates_c, wa, wb, wc, wo, c2d, mc2, B * NC)
    return x_new, c_new


def kernel(x, c, ctx, c_ctx, w_mod, b_mod, norm_mix, norm_ffn, w_in, hg_lb, na_q_gain, na_k_gain, na_rpb,
           hy_conv_w, hy_conv_b, hy_pe_w1, hy_pe_b1, hy_pe_freq1, hy_pe_w2, hy_pe_b2, hy_pe_freq2, hy_pe_w3,
           hy_skip, w_branch_a, w_branch_b, w_branch_c, w_out, w_router, w_e_gate, w_e_up, w_e_down):
    B, N, D = x.shape
    NC = ctx.shape[1]
    E = N_EXPERTS
    cap_x = EC_CAP_FACTOR * N // E
    cap_c = EC_CAP_FACTOR * NC // E

    lb = jnp.cumsum(jax.nn.softmax(hg_lb.astype(f32), axis=0), axis=0)
    lb = lb - lb[:1]
    la_all, lc_all = jnp.log(lb), jnp.log1p(-lb)

    s8 = jnp.zeros((8, D), f32).at[:B].set(c).at[B].set(c_ctx)
    rope = _rope_tables(N)
    bd = jnp.asarray(np.kron(np.eye(NA_HEADS), np.full((NA_HD, NA_HD), 1.0 / NA_HD)).astype(np.float32), bf16)
    dcx = _dft_consts(N)

    x2d = x.reshape(B * N, D)
    c2d = ctx.reshape(B * NC, D)
    for l in range(DEPTH):
        need_ctx = l < DEPTH - 1
        mv = modvec(s8, w_mod[l], b_mod[l])
        mx = [mv[:B, k * D:(k + 1) * D] for k in range(6)]
        mc = [mv[B:B + 1, k * D:(k + 1) * D] for k in range(6)]
        w_in_l = w_in[l].astype(bf16)
        hx = modulate(x2d, norm_mix[l], mx[0], mx[1], N, bf16)
        hc = modulate(c2d, norm_mix[l], mc[0], mc[1], B * NC, bf16)

        filt = (hy_pe_w1[l], hy_pe_b1[l], hy_pe_freq1[l], hy_pe_w2[l], hy_pe_b2[l], hy_pe_freq2[l], hy_pe_w3[l])
        k_x, nrm_x = hyena_filter(N, *filt)
        sre, sim = hyena_spectrum(k_x, 1.0 / nrm_x, dcx)
        skip = hy_skip[l].reshape(1, HY_W)
        filt_c = None
        if need_ctx:
            h_c, nrm_c = hyena_filter(NC, *filt)
            filt_c = (h_c, 1.0 / nrm_c)

        x2d, c_new = _mixing(
            hx, hc, need_ctx, B, N, NC, la_all[l], lc_all[l], w_in_l, na_q_gain[l], na_k_gain[l],
            _na_bias_table(na_rpb[l]), rope, bd, hy_conv_w[l], hy_conv_b[l], (sre, sim, dcx), skip,
            filt_c, w_branch_a[l].astype(bf16), w_branch_b[l].astype(bf16), w_branch_c[l].astype(bf16),
            w_out[l].astype(bf16), x2d, c2d, mx[2], mc[2])

        h2, aff_t, aff_m = router(x2d, norm_ffn[l], mx[3], mx[4], w_router[l], N)
        posm, cnt = select_topk(aff_t, cap_x)
        fast_ok = _fits_fast(cnt, N)
        xgs = [gather_rows(cnt, h2, posm, cap_x, 1024, fast_ok)]
        if need_ctx:
            c2d = c_new
            hc2, aff_tc, aff_mc = router(c2d, norm_ffn[l], mc[3], mc[4], w_router[l], NC)
            posm_c, cnt_c = select_topk(aff_tc, cap_c)
            xgs.append(gather_rows(cnt_c, hc2, posm_c, cap_c, NC))
        ys = expert_ffn(xgs, l, w_e_gate, w_e_up, w_e_down)
        x2d = combine(cnt, x2d, posm, aff_m, mx[5], ys[0], cap_x, CMB_TILE, fast_ok)
        if need_ctx:
            c2d = combine(cnt_c, c2d, posm_c, aff_mc, mc[5], ys[1], cap_c, NC)
    return x2d.reshape(B, N, D)
```

```python
import functools
import math

import numpy as np
import jax
import jax.numpy as jnp
from jax import lax
from jax.experimental import pallas as pl
from jax.experimental.pallas import tpu as pltpu

f32 = jnp.float32
bf16 = jnp.bfloat16
i32 = jnp.int32
HI = lax.Precision.HIGHEST

D_MODEL = 1024
DEPTH = 2
GRID_W = 64
EPS = 1e-6
HG_HEADS = 4
HG_W = 512
HG_CHUNK = 64
NA_HEADS = 8
NA_HD = 64
NA_W = 512
NA_WIN_R = 8
NA_WIN_C = 16
ROPE_THETA = 10000.0
HY_W = 512
HY_BANDS = 16
HY_PE_DIM = 1 + 2 * HY_BANDS
HY_FILT_HID = 64
HY_FAST_DECAY = 0.3
HY_SLOW_DECAY = 1.5
HY_TARGET = 1e-2
OFF_FF = 0
OFF_FB = 512
OFF_I = 1024
OFF_NK = 1536
OFF_NV = 2048
OFF_HQ = 2560
OFF_NQ = 3072
OFF_HG = 3584
OFF_HY = 4096
OFF_GATE = 5632
IN_COLS = 8704
N_EXPERTS = 16
EC_CAP_FACTOR = 2
D_FF_EXPERT = 2816

LANES = 128
NEG_BIG = -1e30
VMEM_LIMIT = 56 * 1024 * 1024


def _cparams(sem, vmem=VMEM_LIMIT):
    return pltpu.CompilerParams(dimension_semantics=sem, vmem_limit_bytes=vmem)


def _nt(a, b, precision=None):
    return lax.dot_general(a, b, (((1,), (1,)), ((), ())), precision=precision, preferred_element_type=f32)


def _tn(a, b, precision=None):
    return lax.dot_general(a, b, (((0,), (0,)), ((), ())), precision=precision, preferred_element_type=f32)


def _silu(x):
    return x * jax.nn.sigmoid(x)


def _split(x):
    hi = x.astype(bf16)
    return hi, (x - hi.astype(f32)).astype(bf16)


def _modvec_kernel(s_ref, w_ref, b_ref, o_ref):
    s = _silu(s_ref[...])
    o_ref[...] = jnp.dot(s, w_ref[...], precision=HI, preferred_element_type=f32) + b_ref[...]


def modvec(s8, w, b):
    n = w.shape[1]
    tn = 1024
    return pl.pallas_call(
        _modvec_kernel,
        grid=(n // tn,),
        in_specs=[pl.BlockSpec((8, D_MODEL), lambda j: (0, 0)),
                  pl.BlockSpec((D_MODEL, tn), lambda j: (0, j)),
                  pl.BlockSpec((1, tn), lambda j: (0, j))],
        out_specs=pl.BlockSpec((8, tn), lambda j: (0, j)),
        out_shape=jax.ShapeDtypeStruct((8, n), f32),
        compiler_params=_cparams(("parallel",)),
        name="modvec",
    )(s8, w, b.reshape(1, n))


def _modulate_kernel(x_ref, g_ref, sh_ref, sc_ref, o_ref):
    x = x_ref[...]
    ms = jnp.mean(x * x, axis=-1, keepdims=True)
    y = x * lax.rsqrt(ms + EPS)
    o_ref[...] = (y * g_ref[...] * (1.0 + sc_ref[0]) + sh_ref[0]).astype(o_ref.dtype)


def modulate(x2d, g, shift, scale, rows_per_group, out_dtype):
    R = x2d.shape[0]
    tm = 512
    tpg = rows_per_group // tm
    G = shift.shape[0]
    return pl.pallas_call(
        _modulate_kernel,
        grid=(R // tm,),
        in_specs=[pl.BlockSpec((tm, D_MODEL), lambda i: (i, 0)),
                  pl.BlockSpec((1, D_MODEL), lambda i: (0, 0)),
                  pl.BlockSpec((1, 1, D_MODEL), lambda i: (i // tpg, 0, 0)),
                  pl.BlockSpec((1, 1, D_MODEL), lambda i: (i // tpg, 0, 0))],
        out_specs=pl.BlockSpec((tm, D_MODEL), lambda i: (i, 0)),
        out_shape=jax.ShapeDtypeStruct((R, D_MODEL), out_dtype),
        compiler_params=_cparams(("parallel",)),
        name="modulate",
    )(x2d, g.reshape(1, D_MODEL), shift.reshape(G, 1, D_MODEL), scale.reshape(G, 1, D_MODEL))


def _log1p_exp_neg(a):
    return jnp.log(1.0 + jnp.exp(-a))


def _log_sigmoid(z):
    return jnp.minimum(z, 0.0) - _log1p_exp_neg(jnp.abs(z))


def _epi_raw(acc, o_ref):
    o_ref[...] = acc.astype(o_ref.dtype)


def _epi_silu(acc, o_ref):
    o_ref[...] = _silu(acc).astype(o_ref.dtype)


def _epi_sigmoid(acc, o_ref):
    o_ref[...] = jax.nn.sigmoid(acc).astype(o_ref.dtype)


def _epi_logforget(acc, la_ref, lc_ref, o_ref):
    la = la_ref[...]
    c = lc_ref[...] + _log_sigmoid(acc)
    o_ref[...] = jnp.maximum(la, c) + _log1p_exp_neg(jnp.abs(la - c))


def _head_rms(acc, gain_ref, bd_ref):
    hi, lo = _split(acc * acc)
    ms = jnp.dot(hi, bd_ref[...], preferred_element_type=f32) + jnp.dot(lo, bd_ref[...], preferred_element_type=f32)
    return acc * lax.rsqrt(ms + EPS) * gain_ref[...]


def _rope(y, c_ref, s1_ref, s2_ref):
    reps = y.shape[1] // LANES
    c = jnp.concatenate([c_ref[...]] * reps, axis=1)
    s1 = jnp.concatenate([s1_ref[...]] * reps, axis=1)
    s2 = jnp.concatenate([s2_ref[...]] * reps, axis=1)
    w = y.shape[1]
    return y * c + pltpu.roll(y, w - 16, axis=1) * s1 + pltpu.roll(y, 16, axis=1) * s2


def _epi_norm(acc, gain_ref, bd_ref, o_ref):
    o_ref[...] = _head_rms(acc, gain_ref, bd_ref).astype(o_ref.dtype)


def _epi_norm_rope(acc, gain_ref, bd_ref, c_ref, s1_ref, s2_ref, o_ref):
    y = _head_rms(acc, gain_ref, bd_ref)
    o_ref[...] = _rope(y, c_ref, s1_ref, s2_ref).astype(o_ref.dtype)


def _epi_norm_both(acc, gain_ref, bd_ref, c_ref, s1_ref, s2_ref, on_ref, or_ref):
    y = _head_rms(acc, gain_ref, bd_ref)
    on_ref[...] = y.astype(on_ref.dtype)
    or_ref[...] = _rope(y, c_ref, s1_ref, s2_ref).astype(or_ref.dtype)


def _proj_kernel(h_ref, w_ref, *rest, epi):
    acc = jnp.dot(h_ref[...], w_ref[...], preferred_element_type=f32)
    epi(acc, *rest)


def project(h, w, c0, width, epi, aux=(), out_dtypes=(bf16,), tm=2048, rows_per_seq=None):
    R = h.shape[0]
    tn = 512
    nj = width // tn
    cb = c0 // tn
    in_specs = [pl.BlockSpec((tm, D_MODEL), lambda i, j: (i, 0)),
                pl.BlockSpec((D_MODEL, tn), lambda i, j: (0, cb + j))]
    args = [h, w]
    for kind, arr in aux:
        if kind == "col":
            in_specs.append(pl.BlockSpec((1, tn), lambda i, j: (0, j)))
        elif kind == "const":
            in_specs.append(pl.BlockSpec(arr.shape, lambda i, j: (0, 0)))
        else:
            tps = rows_per_seq // tm
            in_specs.append(pl.BlockSpec((tm, LANES), lambda i, j: (i % tps, 0)))
        args.append(arr)
    out_specs = [pl.BlockSpec((tm, tn), lambda i, j: (i, j)) for _ in out_dtypes]
    out_shape = [jax.ShapeDtypeStruct((R, width), dt) for dt in out_dtypes]
    res = pl.pallas_call(
        functools.partial(_proj_kernel, epi=epi),
        grid=(R // tm, nj),
        in_specs=in_specs,
        out_specs=out_specs,
        out_shape=out_shape,
        compiler_params=_cparams(("parallel", "parallel")),
        name="proj_" + epi.__name__[5:],
    )(*args)
    return res[0] if len(res) == 1 else res


HG_MM_LEVELS = 3


def _hgrn_tmatrix(C, reverse):
    t = np.arange(C)
    tau = (C - 1 - t) if reverse else t
    tt, uu = tau[:, None], tau[None, :]
    T = np.zeros((1 + HG_MM_LEVELS, C, C), np.float32)
    T[0] = uu <= tt
    for l in range(HG_MM_LEVELS):
        same = (tt >> (l + 1)) == (uu >> (l + 1))
        tr = ((tt >> l) & 1) == 1
        ur = ((uu >> l) & 1) == 1
        T[1 + l] = same & ((tr & ur & (uu <= tt)) | (~tr & ~ur & (uu > tt)))
    return T.reshape((1 + HG_MM_LEVELS) * C, C)


def _hgrn_level_sums(b, l, C, reverse):
    h = 1 << l
    parts = []
    for r0 in range(0, C, 2 * h):
        tb = r0 + h if reverse else r0 + h - 1
        d = b[r0:r0 + 2 * h] - b[tb:tb + 1]
        first, second = d[:h], d[h:]
        parts += [first, -second] if reverse else [-first, second]
    return jnp.concatenate(parts, axis=0)


def _hgrn_step(chains, C):
    L = int(round(math.log2(C)))
    ti0 = lax.broadcasted_iota(i32, (C, C), 0)
    si0 = lax.broadcasted_iota(i32, (C, C), 1)

    es, kcs = [], []
    for lf, _, _, tm, _, reverse in chains:
        W = lf.shape[1]
        hi, lo = _split(lf)
        r = jnp.dot(tm, jnp.concatenate([hi, lo], axis=1), preferred_element_type=f32)
        r = r[:, :W] + r[:, W:]
        b = r[0:C]
        end_row = 0 if reverse else C - 1
        levels = [r[(1 + l) * C:(2 + l) * C] for l in range(HG_MM_LEVELS)]
        levels += [_hgrn_level_sums(b, l, C, reverse) for l in range(HG_MM_LEVELS, L)]
        es.append((b, b[end_row:end_row + 1] - b, levels))
        kcs.append(1.0 - jnp.exp(lf))

    items = []
    for (lf, v_all, q_all, _, s_view, reverse), (b_all, suf_all, levels), kc_all in zip(chains, es, kcs):
        ti, si = (C - 1 - ti0, C - 1 - si0) if reverse else (ti0, si0)
        end_row = 0 if reverse else C - 1
        lmasks = [((ti >> (l + 1)) == (si >> (l + 1))) & (((ti >> l) & 1) == 1) & (((si >> l) & 1) == 0)
                  for l in range(L)]
        for h in range(HG_HEADS):
            sl = slice(LANES * h, LANES * (h + 1))
            qh, kch, vh = q_all[:, sl], kc_all[:, sl], v_all[:, sl]
            b = b_all[:, sl]
            st = s_view[h]
            pairs = [(qh.astype(bf16), kch.astype(bf16), ti == si)]
            for l in range(L):
                x = jnp.exp(levels[l][:, sl])
                pairs.append(((qh * x).astype(bf16), (kch * x).astype(bf16), lmasks[l]))
            items.append(dict(qb=(qh * jnp.exp(b)).astype(bf16), st=st, pairs=pairs, vh=vh,
                              kd=(kch * jnp.exp(suf_all[:, sl])).astype(bf16),
                              decay=jnp.exp(b[end_row:end_row + 1, :]), view=s_view, h=h))

    for it in items:
        it["o"] = _nt(it["qb"], it["st"].astype(bf16))
        it["att"] = [(_nt(ql, kl), m) for ql, kl, m in it["pairs"]]

    outs = []
    for it in items:
        att = None
        for a, m in it["att"]:
            t = jnp.where(m, a, 0.0)
            att = t if att is None else att + t
        it["o"] = it["o"] + jnp.dot(att.astype(bf16), it["vh"], preferred_element_type=f32)
        it["view"][it["h"]] = it["st"] * it["decay"] + _tn(it["vh"], it["kd"])
    nh = HG_HEADS
    for c in range(len(chains)):
        outs.append(jnp.concatenate([items[c * nh + h]["o"] for h in range(nh)], axis=1))
    return outs


def _hgrn_kernel(lff_ref, lfb_ref, vf_ref, vb_ref, qf_ref, qb_ref, s0f_ref, s0b_ref, tf_ref, tb_ref,
                 of_ref, ob_ref, sff_ref, sfb_ref, s_scr, *, C, B):
    c = pl.program_id(0)

    @pl.when(c == 0)
    def _():
        s_scr[0] = s0f_ref[...]
        s_scr[1] = s0b_ref[...]

    chains = []
    for b in range(B):
        chains.append((lff_ref[b], vf_ref[b], qf_ref[b].astype(f32), tf_ref[...], s_scr.at[0, b], False))
        chains.append((lfb_ref[b], vb_ref[b], qb_ref[b].astype(f32), tb_ref[...], s_scr.at[1, b], True))
    outs = _hgrn_step(chains, C)
    for b in range(B):
        of_ref[b] = outs[2 * b]
        ob_ref[b] = outs[2 * b + 1]

    @pl.when(c == pl.num_programs(0) - 1)
    def _():
        sff_ref[...] = s_scr[0]
        sfb_ref[...] = s_scr[1]


def hgrn_bidir(lf_f, lf_b, v, q, s0_f, s0_b):
    B, N, W = lf_f.shape
    C = HG_CHUNK
    nch = N // C
    tf = jnp.asarray(_hgrn_tmatrix(C, False), bf16)
    tb = jnp.asarray(_hgrn_tmatrix(C, True), bf16)
    fw = pl.BlockSpec((B, C, W), lambda c: (0, c, 0))
    bw = pl.BlockSpec((B, C, W), lambda c: (0, nch - 1 - c, 0))
    st = pl.BlockSpec((B, HG_HEADS, LANES, LANES), lambda c: (0, 0, 0, 0))
    tsp = pl.BlockSpec(tf.shape, lambda c: (0, 0))
    seq = jax.ShapeDtypeStruct((B, N, W), f32)
    sts = jax.ShapeDtypeStruct((B, HG_HEADS, LANES, LANES), f32)
    return pl.pallas_call(
        functools.partial(_hgrn_kernel, C=C, B=B),
        grid=(nch,),
        in_specs=[fw, bw, fw, bw, fw, bw, st, st, tsp, tsp],
        out_specs=[fw, bw, st, st],
        out_shape=[seq, seq, sts, sts],
        scratch_shapes=[pltpu.VMEM((2, B, HG_HEADS, LANES, LANES), f32)],
        compiler_params=_cparams(("arbitrary",)),
        name="hgrn",
    )(lf_f, lf_b, v, v, q, q, s0_f, s0_b, tf, tb)


LOG2E = math.log2(math.e)


def _na_kernel(qr_ref, qn_ref, k_ref, v_ref, kc_ref, vc_ref, tab_ref, o_ref, s_scr, p_scr, inv_scr, *,
               rows_per_step, n_rows):
    g = pl.program_id(2)
    scale = NA_HD ** -0.5 * LOG2E
    lane = lax.broadcasted_iota(i32, (GRID_W, LANES), 1)
    kcx = kc_ref[0]
    vcx = vc_ref[0]
    win = NA_WIN_R * GRID_W
    ctx_len = kcx.shape[0]

    starts = []
    for i in range(rows_per_step):
        r = g * rows_per_step + i
        rs = jnp.clip(r - NA_WIN_R // 2, 0, n_rows - NA_WIN_R)
        off = rs - r + (NA_WIN_R - 1)
        start = pl.multiple_of(rs * GRID_W, GRID_W)
        starts.append(start)
        kw = k_ref[0, pl.ds(start, win), :]
        qr = qr_ref[0, i * GRID_W:(i + 1) * GRID_W, :]
        qn = qn_ref[0, i * GRID_W:(i + 1) * GRID_W, :]
        zq = jnp.zeros_like(qr)
        qrs = jnp.concatenate([jnp.where(lane < NA_HD, qr, zq), jnp.where(lane < NA_HD, zq, qr)], axis=0)
        qns = jnp.concatenate([jnp.where(lane < NA_HD, qn, zq), jnp.where(lane < NA_HD, zq, qn)], axis=0)
        bias = jnp.concatenate([tab_ref[0, off], tab_ref[1, off]], axis=0)
        row0 = 2 * i * GRID_W
        s_scr[row0:row0 + 2 * GRID_W, 0:win] = _nt(qrs, kw) * scale + bias
        s_scr[row0:row0 + 2 * GRID_W, win:win + ctx_len] = _nt(qns, kcx) * scale

    def softmax_rows(c, carry):
        r0 = pl.multiple_of(c * LANES, LANES)
        s = s_scr[pl.ds(r0, LANES), :]
        p = jnp.exp2(s - jnp.max(s, axis=-1, keepdims=True))
        inv_scr[pl.ds(r0, LANES), :] = 1.0 / jnp.sum(p, axis=-1, keepdims=True)
        p_scr[pl.ds(r0, LANES), :] = p.astype(bf16)
        return carry

    lax.fori_loop(0, 2 * rows_per_step * GRID_W // LANES, softmax_rows, 0, unroll=2)

    for i in range(rows_per_step):
        vw = v_ref[0, pl.ds(starts[i], win), :]
        row0 = 2 * i * GRID_W
        p = p_scr[row0:row0 + 2 * GRID_W, :]
        res = (jnp.dot(p[:, :win], vw, preferred_element_type=f32)
               + jnp.dot(p[:, win:], vcx, preferred_element_type=f32))
        res = res * inv_scr[row0:row0 + 2 * GRID_W, :]
        o_ref[0, i * GRID_W:(i + 1) * GRID_W, :] = jnp.where(lane < NA_HD, res[:GRID_W], res[GRID_W:]).astype(o_ref.dtype)


def _na_bias_table(rpb):
    col = jnp.arange(GRID_W)
    cs = jnp.clip(col - NA_WIN_C // 2, 0, GRID_W - NA_WIN_C)
    kc = jnp.arange(GRID_W)
    valid = (kc[None, :] >= cs[:, None]) & (kc[None, :] < cs[:, None] + NA_WIN_C)
    dc = jnp.clip(kc[None, :] - col[:, None] + (NA_WIN_C - 1), 0, 2 * NA_WIN_C - 2)
    bc = jnp.where(valid[None, None], rpb[:, :, dc], NEG_BIG)
    t2 = jnp.stack([bc[:, o:o + NA_WIN_R] for o in range(NA_WIN_R)], axis=1)
    t2 = t2.transpose(0, 1, 3, 2, 4)
    return t2.reshape(NA_HEADS, NA_WIN_R, GRID_W, NA_WIN_R * GRID_W).astype(f32) * LOG2E


def neighbourhood_attention(q_rot, qn, k_rot, v, kc, vc, table):
    B, N, W = q_rot.shape
    n_rows = N // GRID_W
    rps = 16 if n_rows % 16 == 0 else 8
    ctx_len = kc.shape[1]
    pairs = W // LANES
    keys = NA_WIN_R * GRID_W + ctx_len
    return pl.pallas_call(
        functools.partial(_na_kernel, rows_per_step=rps, n_rows=n_rows),
        grid=(B, pairs, n_rows // rps),
        in_specs=[pl.BlockSpec((1, rps * GRID_W, LANES), lambda b, p, g: (b, g, p)),
                  pl.BlockSpec((1, rps * GRID_W, LANES), lambda b, p, g: (b, g, p)),
                  pl.BlockSpec((1, N, LANES), lambda b, p, g: (b, 0, p)),
                  pl.BlockSpec((1, N, LANES), lambda b, p, g: (b, 0, p)),
                  pl.BlockSpec((1, ctx_len, LANES), lambda b, p, g: (b, 0, p)),
                  pl.BlockSpec((1, ctx_len, LANES), lambda b, p, g: (b, 0, p)),
                  pl.BlockSpec((2, NA_WIN_R, GRID_W, NA_WIN_R * GRID_W), lambda b, p, g: (p, 0, 0, 0))],
        out_specs=pl.BlockSpec((1, rps * GRID_W, LANES), lambda b, p, g: (b, g, p)),
        out_shape=jax.ShapeDtypeStruct((B, N, W), bf16),
        scratch_shapes=[pltpu.VMEM((2 * rps * GRID_W, keys), f32), pltpu.VMEM((2 * rps * GRID_W, keys), bf16),
                        pltpu.VMEM((2 * rps * GRID_W, 1), f32)],
        compiler_params=_cparams(("parallel", "parallel", "arbitrary")),
        name="natten",
    )(q_rot, qn, k_rot, v, kc, vc, table)


def _ctx_attn_kernel(q_ref, k_ref, v_ref, o_ref):
    scale = NA_HD ** -0.5
    q = q_ref[0]
    k = k_ref[0]
    v = v_ref[0]
    lane = lax.broadcasted_iota(i32, q.shape, 1)
    res = []
    for hh in range(2):
        m = (lane >= NA_HD * hh) & (lane < NA_HD * (hh + 1))
        s = _nt(jnp.where(m, q, jnp.zeros_like(q)), k) * scale
        p = jnp.exp(s - jnp.max(s, axis=-1, keepdims=True))
        p = p / jnp.sum(p, axis=-1, keepdims=True)
        res.append(jnp.dot(p.astype(bf16), v, preferred_element_type=f32))
    o_ref[0] = jnp.where(lane < NA_HD, res[0], res[1]).astype(o_ref.dtype)


def context_attention(q, k, v):
    B, N, W = q.shape
    spec = pl.BlockSpec((1, N, LANES), lambda b, p: (b, 0, p))
    return pl.pallas_call(
        _ctx_attn_kernel,
        grid=(B, W // LANES),
        in_specs=[spec, spec, spec],
        out_specs=spec,
        out_shape=jax.ShapeDtypeStruct((B, N, W), bf16),
        compiler_params=_cparams(("parallel", "parallel")),
        name="ctx_attn",
    )(q, k, v)


def _filter_kernel(z_ref, w1_ref, b1_ref, f1_ref, w2_ref, b2_ref, f2_ref, w3_ref, dl_ref, k_ref, nrm_ref, *, tm, n):
    i = pl.program_id(0)
    z = z_ref[...]
    dot = lambda a, b: jnp.dot(a, b, precision=HI, preferred_element_type=f32)
    w1 = w1_ref[...]
    a = jnp.sin(f1_ref[...] * (dot(z[:tm // 2], w1[:, :LANES]) + dot(z[tm // 2:], w1[:, LANES:]) + b1_ref[...]))
    a = jnp.sin(f2_ref[...] * (dot(a, w2_ref[...]) + b2_ref[...]))
    w3 = w3_ref[...]
    zero = jnp.zeros_like(w3)
    h = jnp.concatenate([dot(a, jnp.concatenate([w3, zero], axis=0)), dot(a, jnp.concatenate([zero, w3], axis=0))],
                        axis=0)
    h = h * jnp.exp(-z[:, 0:1] * dl_ref[...])
    row = lax.broadcasted_iota(i32, (tm, HY_W), 0) + i * tm
    k = jnp.where(row == n, 0.0, h)
    k_ref[...] = k
    part = jnp.sum(jnp.abs(k), axis=0, keepdims=True)

    @pl.when(i == 0)
    def _():
        nrm_ref[...] = part

    @pl.when(i > 0)
    def _():
        nrm_ref[...] = nrm_ref[...] + part


def hyena_filter(n, w1, b1, fr1, w2, b2, fr2, w3):
    t = np.linspace(0.0, 1.0, n)[:, None]
    w = 2 * math.pi * np.arange(n)[:, None] / n
    fb = np.linspace(1e-4, HY_BANDS - 1, HY_BANDS)[None]
    z = np.concatenate([t, np.cos(fb * w), -np.sin(fb * w)], axis=-1)
    z = np.concatenate([z, np.zeros((1, HY_PE_DIM)), z[:0:-1]], axis=0)
    z = jnp.asarray(np.pad(z, ((0, 0), (0, LANES - HY_PE_DIM))).astype(np.float32))
    w1p = jnp.pad(w1.astype(f32), ((0, LANES - HY_PE_DIM), (0, 0)))
    deltas = jnp.asarray(np.abs(np.linspace(math.log(HY_TARGET) / HY_SLOW_DECAY, math.log(HY_TARGET) / HY_FAST_DECAY,
                                            2 * HY_W))[None].astype(np.float32))
    tm = min(n, 512)
    hid = HY_FILT_HID
    full = lambda shape: pl.BlockSpec(shape, lambda i: (0, 0))
    tph = n // tm
    zpad = jnp.zeros((LANES, hid), f32)
    w1pk = jnp.concatenate([w1p, zpad, zpad, w1p], axis=1)
    w2f = w2.astype(f32)
    z2 = jnp.zeros((hid, hid), f32)
    w2pk = jnp.concatenate([jnp.concatenate([w2f, z2], axis=1), jnp.concatenate([z2, w2f], axis=1)], axis=0)
    twice = lambda v: jnp.tile(v.reshape(1, hid).astype(f32), (1, 2))
    return pl.pallas_call(
        functools.partial(_filter_kernel, tm=tm, n=n),
        grid=(2 * n // tm,),
        in_specs=[pl.BlockSpec((tm, LANES), lambda i: (i, 0)),
                  full((LANES, 2 * LANES)), full((1, LANES)), full((1, LANES)),
                  full((LANES, LANES)), full((1, LANES)), full((1, LANES)),
                  pl.BlockSpec((hid, HY_W), lambda i: (0, i // tph)),
                  pl.BlockSpec((1, HY_W), lambda i: (0, i // tph))],
        out_specs=[pl.BlockSpec((tm, HY_W), lambda i: (i, 0)), pl.BlockSpec((1, HY_W), lambda i: (0, 0))],
        out_shape=[jax.ShapeDtypeStruct((2 * n, HY_W), f32), jax.ShapeDtypeStruct((1, HY_W), f32)],
        compiler_params=_cparams(("arbitrary",)),
        name="hyena_filter",
    )(z, w1pk, twice(b1), twice(fr1), w2pk, twice(b2), twice(fr2), w3.astype(f32), deltas)


def _conv3(u, w_ref, b_ref):
    n = u.shape[0]
    row = lax.broadcasted_iota(i32, u.shape, 0)
    prev = jnp.where(row == 0, 0.0, pltpu.roll(u, 1, axis=0))
    nxt = jnp.where(row == n - 1, 0.0, pltpu.roll(u, n - 1, axis=0))
    return prev * w_ref[0:1, :] + u * w_ref[1:2, :] + nxt * w_ref[2:3, :] + b_ref[...]


def _hyena_pre_kernel(p0_ref, p1_ref, p2_ref, w0_ref, w1_ref, w2_ref, b0_ref, b1_ref, b2_ref, z_ref, x0_ref):
    x0_ref[0] = _conv3(p0_ref[0], w0_ref, b0_ref).astype(x0_ref.dtype)
    z_ref[0] = _conv3(p1_ref[0], w1_ref, b1_ref) * _conv3(p2_ref[0], w2_ref, b2_ref)


def hyena_pre(p, conv_w, conv_b):
    B, N, _ = p.shape
    nb = HY_W // LANES
    conv_b = conv_b.reshape(1, 3 * HY_W)
    pspec = lambda g: pl.BlockSpec((1, N, LANES), lambda b, c: (b, 0, g * nb + c))
    wspec = lambda g: pl.BlockSpec((3, LANES), lambda b, c: (0, g * nb + c))
    bspec = lambda g: pl.BlockSpec((1, LANES), lambda b, c: (0, g * nb + c))
    ospec = pl.BlockSpec((1, N, LANES), lambda b, c: (b, 0, c))
    return pl.pallas_call(
        _hyena_pre_kernel,
        grid=(B, nb),
        in_specs=[pspec(0), pspec(1), pspec(2), wspec(0), wspec(1), wspec(2), bspec(0), bspec(1), bspec(2)],
        out_specs=[ospec, ospec],
        out_shape=[jax.ShapeDtypeStruct((B, N, HY_W), f32), jax.ShapeDtypeStruct((B, N, HY_W), bf16)],
        compiler_params=_cparams(("parallel", "parallel")),
        name="hyena_pre",
    )(p, p, p, conv_w, conv_w, conv_w, conv_b, conv_b, conv_b)


DFT_SLABS = 8


def _hl(a):
    a32 = jnp.asarray(a.astype(np.float32))
    hi = a32.astype(bf16)
    lo = (a32 - hi.astype(f32)).astype(bf16)
    return jnp.concatenate([hi, lo], axis=-2)


def _dot3(a_hl, m, x):
    xh, xl = _split(x)
    r = jnp.dot(a_hl, xh, preferred_element_type=f32)
    return r[:m] + r[m:] + jnp.dot(a_hl[:m], xl, preferred_element_type=f32)


def _dft_consts(n):
    N = 2 * n
    na = N // LANES
    t1n = na // 2
    k1n = na // 2 + 1
    k1p = -(-k1n // 8) * 8
    k1 = np.arange(k1n)
    t1 = np.arange(t1n)
    th = 2 * np.pi * ((t1[None, :] * k1[:, None]) % na) / na
    f1c = np.zeros((2 * k1p, t1n))
    f1c[:k1n] = np.cos(th)
    f1c[k1p:k1p + k1n] = -np.sin(th)
    thf = 2 * np.pi * ((np.arange(na)[None, :] * k1[:, None]) % na) / na
    f1f = np.zeros((2 * k1p, na))
    f1f[:k1n] = np.cos(thf)
    f1f[k1p:k1p + k1n] = -np.sin(thf)
    k2 = np.arange(LANES)
    t2 = np.arange(LANES)
    m = (t2[None, None, :] * (k1[:, None, None] + na * k2[None, :, None])) % N
    ph = 2 * np.pi * m / N
    g = np.concatenate([np.cos(ph), -np.sin(ph)], axis=1)
    pht = ph.transpose(0, 2, 1)
    gi = np.concatenate([np.cos(pht), np.sin(pht)], axis=1)
    wk = np.where((k1 == 0) | (k1 == na // 2), 1.0, 2.0) / N
    k1h = na // 2
    f1i = np.concatenate([np.cos(th.T)[:, :k1h] * wk[None, :k1h], -np.sin(th.T)[:, :k1h] * wk[None, :k1h]], axis=1)
    k1e = -(-k1n // DFT_SLABS) * DFT_SLABS
    g = np.concatenate([g, np.zeros((k1e - k1n,) + g.shape[1:])], axis=0)
    gi = np.concatenate([gi, np.zeros((k1e - k1n,) + gi.shape[1:])], axis=0)
    return dict(na=na, t1n=t1n, k1n=k1n, k1e=k1e, k1p=k1p, f1c=_hl(f1c), f1f=_hl(f1f), g=_hl(g), gi=_hl(gi),
                f1i=_hl(f1i))


def _dft_stage1(src_ref, f1c_ref, tre_ref, tim_ref, t1n, k1p):
    f1c = f1c_ref[...]

    def body(t2, carry):
        zs = src_ref[pl.ds(t2, t1n, stride=LANES), :]
        r = _dot3(f1c, 2 * k1p, zs)
        r0 = pl.multiple_of(t2 * k1p, 8)
        tre_ref[pl.ds(r0, k1p), :] = r[:k1p]
        tim_ref[pl.ds(r0, k1p), :] = r[k1p:]
        return carry

    lax.fori_loop(0, LANES, body, 0, unroll=4)


def _slab(tre_ref, tim_ref, k1, k1p):
    return tre_ref[pl.ds(k1, LANES, stride=k1p), :], tim_ref[pl.ds(k1, LANES, stride=k1p), :]


def _cplx_left(gc_hl, xre, xim):
    cw = xre.shape[1]
    r = _dot3(gc_hl, 2 * LANES, jnp.concatenate([xre, xim], axis=1))
    p, q = r[:, :cw], r[:, cw:]
    return p[:LANES] - q[LANES:], p[LANES:] + q[:LANES]


def _spectrum_kernel(k_ref, f1f_ref, g_ref, inv_ref, xre_ref, xim_ref, are, aim, *, na, k1p):
    j = pl.program_id(1)

    @pl.when(j == 0)
    def _():
        _dft_stage1(k_ref, f1f_ref, are, aim, na, k1p)

    for half in range(DFT_SLABS):
        xre, xim = _cplx_left(g_ref[half], *_slab(are, aim, DFT_SLABS * j + half, k1p))
        rows = slice(half * LANES, (half + 1) * LANES)
        xre_ref[rows, :] = xre * inv_ref[...]
        xim_ref[rows, :] = xim * inv_ref[...]


def hyena_spectrum(k, inv_norm, dc):
    n2, C = k.shape
    k1e, k1p, na = dc["k1e"], dc["k1p"], dc["na"]
    cw = LANES
    out = jax.ShapeDtypeStruct((k1e * LANES, C), f32)
    ospec = pl.BlockSpec((DFT_SLABS * LANES, cw), lambda c, k: (k, c))
    return pl.pallas_call(
        functools.partial(_spectrum_kernel, na=na, k1p=k1p),
        grid=(C // cw, k1e // DFT_SLABS),
        in_specs=[pl.BlockSpec((n2, cw), lambda c, k: (0, c)),
                  pl.BlockSpec(dc["f1f"].shape, lambda c, k: (0, 0)),
                  pl.BlockSpec((DFT_SLABS, 4 * LANES, LANES), lambda c, k: (k, 0, 0)),
                  pl.BlockSpec((1, cw), lambda c, k: (0, c))],
        out_specs=[ospec, ospec],
        out_shape=[out, out],
        scratch_shapes=[pltpu.VMEM((k1p * LANES, cw), f32), pltpu.VMEM((k1p * LANES, cw), f32)],
        compiler_params=_cparams(("parallel", "arbitrary")),
        name="hyena_spectrum",
    )(k, dc["f1f"], dc["g"], inv_norm)


def _hyena_conv_kernel(z_ref, x0_ref, f1c_ref, g_ref, gi_ref, f1i_ref, kre_ref, kim_ref,
                       skip_ref, o_ref, tre, tim, are, aim, y_scr, *, t1n, k1p):
    j = pl.program_id(2)

    @pl.when(j == 0)
    def _():
        _dft_stage1(z_ref.at[0], f1c_ref, tre, tim, t1n, k1p)

    for half in range(DFT_SLABS):
        r0 = pl.multiple_of((DFT_SLABS * j + half) * LANES, LANES)
        xre, xim = _cplx_left(g_ref[half], *_slab(tre, tim, DFT_SLABS * j + half, k1p))
        rows = slice(half * LANES, (half + 1) * LANES)
        kre = kre_ref[rows, :]
        kim = kim_ref[rows, :]
        yre = xre * kre - xim * kim
        yim = xre * kim + xim * kre
        bre, bim = _cplx_left(gi_ref[half], yre, yim)
        are[pl.ds(r0, LANES), :] = bre
        aim[pl.ds(r0, LANES), :] = bim

    @pl.when(j == pl.num_programs(2) - 1)
    def _():
        f1i = f1i_ref[...]
        k1h = t1n

        def body(t2, carry):
            bb = jnp.concatenate([are[pl.ds(t2, k1h, stride=LANES), :], aim[pl.ds(t2, k1h, stride=LANES), :]], axis=0)
            y_scr[pl.ds(t2, t1n, stride=LANES), :] = _dot3(f1i, t1n, bb)
            return carry

        lax.fori_loop(0, LANES, body, 0, unroll=4)
        r0 = k1h * LANES
        nyq = are[r0:r0 + LANES, :] * (1.0 / (2 * t1n * LANES))
        nyq = jnp.concatenate([nyq, -nyq] * (t1n // 2), axis=0)
        z = z_ref[0]
        o_ref[0] = (x0_ref[0].astype(f32) * (y_scr[...] + nyq + z * skip_ref[...])).astype(o_ref.dtype)


def hyena_conv(z, x0, spec_re, spec_im, skip, dc):
    B, n, W = z.shape
    nb = W // LANES
    k1e, k1p, t1n = dc["k1e"], dc["k1p"], dc["t1n"]
    seq = pl.BlockSpec((1, n, LANES), lambda b, c, k: (b, 0, c))
    fspec = pl.BlockSpec((DFT_SLABS * LANES, LANES), lambda b, c, k: (k, c))
    cspec = pl.BlockSpec((DFT_SLABS, 4 * LANES, LANES), lambda b, c, k: (k, 0, 0))
    vspec = pl.BlockSpec((1, LANES), lambda b, c, k: (0, c))
    return pl.pallas_call(
        functools.partial(_hyena_conv_kernel, t1n=t1n, k1p=k1p),
        grid=(B, nb, k1e // DFT_SLABS),
        in_specs=[seq, seq,
                  pl.BlockSpec(dc["f1c"].shape, lambda b, c, k: (0, 0)), cspec, cspec,
                  pl.BlockSpec(dc["f1i"].shape, lambda b, c, k: (0, 0)),
                  fspec, fspec, vspec],
        out_specs=seq,
        out_shape=jax.ShapeDtypeStruct((B, n, W), bf16),
        scratch_shapes=[pltpu.VMEM((k1p * LANES, LANES), f32)] * 4 + [pltpu.VMEM((n, LANES), f32)],
        compiler_params=_cparams(("parallel", "parallel", "arbitrary")),
        name="hyena_conv",
    )(z, x0, dc["f1c"], dc["g"], dc["gi"], dc["f1i"], spec_re, spec_im, skip)


def _hyena_ctx_kernel(p0_ref, p1_ref, p2_ref, w0_ref, w1_ref, w2_ref, b0_ref, b1_ref, b2_ref,
                      k_ref, inv_ref, skip_ref, fd_ref, fi_ref, o_ref, *, n):
    x0 = _conv3(p0_ref[0], w0_ref, b0_ref)
    z = _conv3(p1_ref[0], w1_ref, b1_ref) * _conv3(p2_ref[0], w2_ref, b2_ref)
    fd = fd_ref[...]
    N = 2 * n
    zf = jnp.dot(fd[:, :n], z, precision=HI, preferred_element_type=f32)
    kf = jnp.dot(fd, k_ref[...], precision=HI, preferred_element_type=f32) * inv_ref[...]
    yre = zf[:N] * kf[:N] - zf[N:] * kf[N:]
    yim = zf[:N] * kf[N:] + zf[N:] * kf[:N]
    y = jnp.dot(fi_ref[...], jnp.concatenate([yre, yim], axis=0), precision=HI, preferred_element_type=f32)
    o_ref[0] = (x0 * (y + z * skip_ref[...])).astype(o_ref.dtype)


def hyena_ctx(p, conv_w, conv_b, k, inv_norm, skip):
    B, n, _ = p.shape
    N = 2 * n
    nb = HY_W // LANES
    kk = np.arange(N)
    ph = 2 * np.pi * ((kk[:, None] * kk[None, :]) % N) / N
    fd = jnp.asarray(np.concatenate([np.cos(ph), -np.sin(ph)], axis=0).astype(np.float32))
    fi = jnp.asarray((np.concatenate([np.cos(ph[:n]), -np.sin(ph[:n])], axis=1) / N).astype(np.float32))
    conv_b = conv_b.reshape(1, 3 * HY_W)
    pspec = lambda g: pl.BlockSpec((1, n, LANES), lambda b, c: (b, 0, g * nb + c))
    wspec = lambda g: pl.BlockSpec((3, LANES), lambda b, c: (0, g * nb + c))
    bspec = lambda g: pl.BlockSpec((1, LANES), lambda b, c: (0, g * nb + c))
    vspec = pl.BlockSpec((1, LANES), lambda b, c: (0, c))
    return pl.pallas_call(
        functools.partial(_hyena_ctx_kernel, n=n),
        grid=(B, nb),
        in_specs=[pspec(0), pspec(1), pspec(2), wspec(0), wspec(1), wspec(2), bspec(0), bspec(1), bspec(2),
                  pl.BlockSpec((N, LANES), lambda b, c: (0, c)),
                  vspec, vspec,
                  pl.BlockSpec(fd.shape, lambda b, c: (0, 0)), pl.BlockSpec(fi.shape, lambda b, c: (0, 0))],
        out_specs=pl.BlockSpec((1, n, LANES), lambda b, c: (b, 0, c)),
        out_shape=jax.ShapeDtypeStruct((B, n, HY_W), bf16),
        compiler_params=_cparams(("parallel", "parallel")),
        name="hyena_ctx",
    )(p, p, p, conv_w, conv_w, conv_w, conv_b, conv_b, conv_b, k, inv_norm, skip, fd, fi)


def _merge_kernel(of_ref, ob_ref, gs_ref, nb_ref, hc_ref, g_ref, wa_ref, wb_ref, wc_ref, wo_ref, x_ref, m_ref, o_ref):
    d = D_MODEL
    tot = of_ref[...] + ob_ref[...]
    gs = gs_ref[...].astype(f32)
    ra = []
    for h in range(HG_HEADS):
        sl = slice(LANES * h, LANES * (h + 1))
        th = tot[:, sl]
        ms = jnp.mean(th * th, axis=-1, keepdims=True)
        ra.append(th * lax.rsqrt(ms + EPS) * gs[:, sl])
    ya = jnp.dot(jnp.concatenate(ra, axis=1).astype(bf16), wa_ref[...], preferred_element_type=f32)
    yb = jnp.dot(nb_ref[...], wb_ref[...], preferred_element_type=f32)
    yc = jnp.dot(hc_ref[...], wc_ref[...], preferred_element_type=f32)
    g = g_ref[...].astype(f32)
    mix = g[:, :d] * ya + g[:, d:2 * d] * yb + g[:, 2 * d:] * yc
    y = jnp.dot(mix.astype(bf16), wo_ref[...], preferred_element_type=f32)
    o_ref[...] = x_ref[...] + m_ref[0] * y


def merge(o_f, o_b, gs, nb, hc, gates, wa, wb, wc, wo, x2d, m, rows_per_group):
    R = x2d.shape[0]
    tm = 512
    tpg = rows_per_group // tm
    G = m.shape[0]
    row = lambda w: pl.BlockSpec((tm, w), lambda i: (i, 0))
    full = lambda a: pl.BlockSpec(a.shape, lambda i: (0, 0))
    return pl.pallas_call(
        _merge_kernel,
        grid=(R // tm,),
        in_specs=[row(HG_W), row(HG_W), row(HG_W), row(NA_W), row(HY_W), row(3 * D_MODEL),
                  full(wa), full(wb), full(wc), full(wo),
                  row(D_MODEL), pl.BlockSpec((1, 1, D_MODEL), lambda i: (i // tpg, 0, 0))],
        out_specs=row(D_MODEL),
        out_shape=jax.ShapeDtypeStruct((R, D_MODEL), f32),
        compiler_params=_cparams(("parallel",)),
        name="merge",
    )(o_f, o_b, gs, nb, hc, gates, wa, wb, wc, wo, x2d, m.reshape(G, 1, D_MODEL))


def _router_kernel(x_ref, g_ref, sh_ref, sc_ref, wrt_ref, wr_ref, h_ref, at_ref, am_ref):
    x = x_ref[...]
    ms = jnp.mean(x * x, axis=-1, keepdims=True)
    h = x * lax.rsqrt(ms + EPS) * g_ref[...] * (1.0 + sc_ref[0]) + sh_ref[0]
    h_ref[...] = h.astype(h_ref.dtype)
    hh, hl = _split(h)
    ne = N_EXPERTS
    wt = wrt_ref[...]
    rt = _nt(wt, hh)
    lt = rt[:ne] + rt[ne:] + _nt(wt[:ne], hl)
    et = jnp.exp(lt - jnp.max(lt, axis=0, keepdims=True))
    at_ref[0] = et / jnp.sum(et, axis=0, keepdims=True)
    wm = wr_ref[...]
    rm = jnp.dot(hh, wm, preferred_element_type=f32)
    lm = rm[:, :ne] + rm[:, ne:] + jnp.dot(hl, wm[:, :ne], preferred_element_type=f32)
    em = jnp.exp(lm - jnp.max(lm, axis=1, keepdims=True))
    am_ref[...] = em / jnp.sum(em, axis=1, keepdims=True)


def router(x2d, g, shift, scale, w_router, n_per_set):
    R = x2d.shape[0]
    tm = min(512, n_per_set)
    tps = n_per_set // tm
    S = R // n_per_set
    G = shift.shape[0]
    gmap = (lambda i: (i // tps, 0, 0)) if G > 1 else (lambda i: (0, 0, 0))
    whi, wlo = _split(w_router.astype(f32))
    wr = jnp.concatenate([whi, wlo], axis=1)
    return pl.pallas_call(
        _router_kernel,
        grid=(R // tm,),
        in_specs=[pl.BlockSpec((tm, D_MODEL), lambda i: (i, 0)),
                  pl.BlockSpec((1, D_MODEL), lambda i: (0, 0)),
                  pl.BlockSpec((1, 1, D_MODEL), gmap),
                  pl.BlockSpec((1, 1, D_MODEL), gmap),
                  pl.BlockSpec((2 * N_EXPERTS, D_MODEL), lambda i: (0, 0)),
                  pl.BlockSpec((D_MODEL, 2 * N_EXPERTS), lambda i: (0, 0))],
        out_specs=[pl.BlockSpec((tm, D_MODEL), lambda i: (i, 0)),
                   pl.BlockSpec((1, N_EXPERTS, tm), lambda i: (i // tps, 0, i % tps)),
                   pl.BlockSpec((tm, N_EXPERTS), lambda i: (i, 0))],
        out_shape=[jax.ShapeDtypeStruct((R, D_MODEL), bf16),
                   jax.ShapeDtypeStruct((S, N_EXPERTS, n_per_set), f32),
                   jax.ShapeDtypeStruct((R, N_EXPERTS), f32)],
        compiler_params=_cparams(("parallel",)),
        name="router",
    )(x2d, g.reshape(1, D_MODEL), shift.reshape(G, 1, D_MODEL), scale.reshape(G, 1, D_MODEL), wr.T, wr)


SEL_BLK = 256
SUB = LANES
SUBW = SUB + 8
UNSEL = -float(2 ** 30)


def _prefix_incl(mask_f, tri, T):
    outs = []
    off = jnp.zeros((mask_f.shape[0], 1), f32)
    for b in range(T // SEL_BLK):
        blk = mask_f[:, b * SEL_BLK:(b + 1) * SEL_BLK].astype(bf16)
        pre = jnp.dot(blk, tri, preferred_element_type=f32) + off
        outs.append(pre)
        off = pre[:, SEL_BLK - 1:SEL_BLK]
    return jnp.concatenate(outs, axis=1)


def _select_kernel(a_ref, tri_ref, cm_ref, posm_ref, cnt_ref, *, T, cap):
    aff = a_ref[0]
    bits = pltpu.bitcast(aff, i32)
    tri = tri_ref[...]

    def bit_step(i, thr):
        cand = thr | (1 << (30 - i))
        cnt = jnp.sum((bits >= cand).astype(f32), axis=1, keepdims=True)
        return jnp.where(cnt >= cap, cand, thr)

    thr = lax.fori_loop(0, 31, bit_step, jnp.zeros((N_EXPERTS, 1), i32))
    gt = bits > thr
    eq = bits == thr
    need = cap - jnp.sum(gt.astype(f32), axis=1, keepdims=True)
    eqf = eq.astype(f32)
    rank_eq = _prefix_incl(eqf, tri, T) - eqf
    sel = gt | (eq & (rank_eq < need))
    self_ = sel.astype(f32)
    pos = _prefix_incl(self_, tri, T) - self_
    posm_ref[0] = jnp.where(sel, pos, UNSEL)
    cnt_ref[0] = jnp.dot(self_.astype(bf16), cm_ref[...], preferred_element_type=f32).astype(i32)


def select_topk(aff, cap):
    S, E, T = aff.shape
    tri = jnp.asarray(np.triu(np.ones((SEL_BLK, SEL_BLK), np.float32)), bf16)
    cm = jnp.asarray((np.arange(T)[:, None] < np.arange(LANES)[None, :] * SUB).astype(np.float32), bf16)
    return pl.pallas_call(
        functools.partial(_select_kernel, T=T, cap=cap),
        grid=(S,),
        in_specs=[pl.BlockSpec((1, E, T), lambda s: (s, 0, 0)),
                  pl.BlockSpec((SEL_BLK, SEL_BLK), lambda s: (0, 0)),
                  pl.BlockSpec((T, LANES), lambda s: (0, 0))],
        out_specs=[pl.BlockSpec((1, E, T), lambda s: (s, 0, 0)),
                   pl.BlockSpec((1, E, LANES), lambda s: (s, 0, 0))],
        out_shape=[jax.ShapeDtypeStruct((S, E, T), f32), jax.ShapeDtypeStruct((S, E, LANES), i32)],
        compiler_params=_cparams(("parallel",)),
        name="select_topk",
    )(aff, tri, cm)


def _align8(v):
    return lax.shift_left(lax.shift_right_logical(v, 3), 3)


def _align16(v):
    return lax.shift_left(lax.shift_right_logical(v, 4), 4)


CMB_ROWS = SUB + 16


def _gather_kernel(cnt_ref, h_ref, pos_ref, o_ref, acc, *, TT, cap, tps, R, srows, EP):
    tl = pl.program_id(1)

    @pl.when(tl == 0)
    def _():
        acc[...] = jnp.zeros_like(acc)

    st = tl // tps
    nsub = TT // SUB
    rid = lax.broadcasted_iota(i32, (srows, SUB), 0).astype(f32)
    for ep in range(EP):
        e = pl.program_id(0) * EP + ep
        cbase = (st * N_EXPERTS + e) * LANES + (tl % tps) * nsub
        for s in range(nsub):
            off8 = _align8(cnt_ref[cbase + s])
            pos = pos_ref[0, ep, :, s * SUB:(s + 1) * SUB]
            onehot = jnp.where(pos == rid + off8.astype(f32), 1.0, 0.0).astype(bf16)
            rows = jnp.dot(onehot, h_ref[s * SUB:(s + 1) * SUB, :], preferred_element_type=f32)
            r0 = pl.multiple_of(st * cap + off8, 8)
            acc[ep, pl.ds(r0, srows), :] += rows

    @pl.when(tl == pl.num_programs(1) - 1)
    def _():
        o_ref[...] = acc[:, 0:R, :].astype(o_ref.dtype)


def _gather_call(cnt, h, posm, cap, TT, srows, EP):
    S, E, T = posm.shape
    tps = T // TT
    R = S * cap
    gs = pltpu.PrefetchScalarGridSpec(
        num_scalar_prefetch=1,
        grid=(E // EP, S * tps),
        in_specs=[pl.BlockSpec((TT, D_MODEL), lambda e, t, c: (t, 0)),
                  pl.BlockSpec((1, EP, 1, TT), lambda e, t, c: (t // tps, e, 0, t % tps))],
        out_specs=pl.BlockSpec((EP, R, D_MODEL), lambda e, t, c: (e, 0, 0)),
        scratch_shapes=[pltpu.VMEM((EP, R + srows, D_MODEL), f32)])
    return pl.pallas_call(
        functools.partial(_gather_kernel, TT=TT, cap=cap, tps=tps, R=R, srows=srows, EP=EP),
        grid_spec=gs,
        out_shape=jax.ShapeDtypeStruct((E, R, D_MODEL), bf16),
        compiler_params=_cparams(("parallel", "arbitrary")),
        name="moe_gather",
    )(cnt.reshape(-1), h, posm.reshape(S, E, 1, T))


FAST_SUB_MAX = 48
FAST_TILE_MAX = 192
FAST_ROWS = 64
CMB_TILE = 1024


def _fits_fast(cnt, T):
    nsub = T // SUB
    per_sub = cnt[..., 1:nsub + 1] - cnt[..., :nsub]
    k = CMB_TILE // SUB
    per_tile = cnt[..., k:nsub + 1:k] - cnt[..., 0:nsub:k]
    return (jnp.max(per_sub) <= FAST_SUB_MAX) & (jnp.max(per_tile) <= FAST_TILE_MAX)


def gather_rows(cnt, h, posm, cap, TT, fast_ok=None):
    safe = lambda: _gather_call(cnt, h, posm, cap, TT, SUBW, 1)
    if fast_ok is None:
        return safe()
    return lax.cond(fast_ok, lambda: _gather_call(cnt, h, posm, cap, TT, FAST_ROWS, 2), safe)


EXPERT_TF = 256


def _ffn_kernel(*refs, n):
    xs = refs[:n]
    wg_ref, wu_ref, wd_ref = refs[n:n + 3]
    his = refs[n + 3:2 * n + 3]
    los = refs[2 * n + 3:3 * n + 3]
    accs = refs[3 * n + 3:]
    j = pl.program_id(1)
    wg = wg_ref[0, 0].astype(bf16)
    wu = wu_ref[0, 0].astype(bf16)
    wd = wd_ref[0, 0].astype(bf16)
    @pl.when(j == 0)
    def _():
        for acc in accs:
            acc[...] = jnp.zeros_like(acc)

    for x_ref, hi_ref, lo_ref, acc in zip(xs, his, los, accs):
        x = x_ref[0]
        a = jnp.dot(x, wg, preferred_element_type=f32)
        u = jnp.dot(x, wu, preferred_element_type=f32)
        acc[...] += jnp.dot((_silu(a) * u).astype(bf16), wd, preferred_element_type=f32)

        @pl.when(j == pl.num_programs(1) - 1)
        def _(acc=acc, hi_ref=hi_ref, lo_ref=lo_ref):
            hi, lo = _split(acc[...])
            hi_ref[0] = hi
            lo_ref[0] = lo


def expert_ffn(xgs, layer, w_gate, w_up, w_down):
    E = xgs[0].shape[0]
    nf = D_FF_EXPERT // EXPERT_TF
    n = len(xgs)
    rowspec = lambda a: pl.BlockSpec((1, a.shape[1], D_MODEL), lambda e, j: (e, 0, 0))
    res = pl.pallas_call(
        functools.partial(_ffn_kernel, n=n),
        grid=(E, nf),
        in_specs=[rowspec(a) for a in xgs] + [
            pl.BlockSpec((1, 1, D_MODEL, EXPERT_TF), lambda e, j: (layer, e, 0, j)),
            pl.BlockSpec((1, 1, D_MODEL, EXPERT_TF), lambda e, j: (layer, e, 0, j)),
            pl.BlockSpec((1, 1, EXPERT_TF, D_MODEL), lambda e, j: (layer, e, j, 0))],
        out_specs=[rowspec(a) for a in xgs] * 2,
        out_shape=[jax.ShapeDtypeStruct(a.shape, bf16) for a in xgs] * 2,
        scratch_shapes=[pltpu.VMEM(a.shape[1:], f32) for a in xgs],
        compiler_params=_cparams(("parallel", "arbitrary")),
        name="expert_ffn",
    )(*xgs, w_gate, w_up, w_down)
    return [(res[i], res[n + i]) for i in range(n)]


def _combine_kernel(cnt_ref, x_ref, pos_ref, am_ref, m_ref, *rest, TT, cap, R, W, ytot, crows, EP):
    y_refs, o_ref = rest[:2 * EP], rest[2 * EP]
    st = pl.program_id(0)
    tl = pl.program_id(1)
    eg = pl.program_id(2)

    @pl.when(eg == 0)
    def _():
        o_ref[...] = x_ref[...]

    nsub = TT // SUB
    lane = lax.broadcasted_iota(i32, (TT, N_EXPERTS), 1)
    m5 = m_ref[0]
    rid = lax.broadcasted_iota(i32, (crows, SUB), 0).astype(f32)
    am = am_ref[...]
    gcols, wss = [], []
    for ep in range(EP):
        e = eg * EP + ep
        cbase = (st * N_EXPERTS + e) * LANES + tl * nsub
        wss.append(jnp.minimum(e * R + st * cap + _align16(cnt_ref[cbase]), ytot - W))
        gcols.append(jnp.sum(jnp.where(lane == e, am, 0.0), axis=1, keepdims=True))
    for s in range(nsub):
        sl = slice(s * SUB, (s + 1) * SUB)
        tot = None
        for ep in range(EP):
            e = eg * EP + ep
            rowbase = e * R + st * cap
            off = _align16(cnt_ref[(st * N_EXPERTS + e) * LANES + tl * nsub + s])
            rel = pl.multiple_of(jnp.minimum(rowbase + off - wss[ep], W - crows), 16)
            first = (wss[ep] + rel - rowbase).astype(f32)
            pos = pos_ref[0, ep, :, sl]
            onehot = jnp.where(pos == rid + first, 1.0, 0.0).astype(bf16)
            ywin = jnp.concatenate([y_refs[2 * ep][pl.ds(rel, crows), :], y_refs[2 * ep + 1][pl.ds(rel, crows), :]],
                                   axis=0)
            picked = _tn(jnp.concatenate([onehot, onehot], axis=0), ywin)
            term = gcols[ep][sl] * picked
            tot = term if tot is None else tot + term
        o_ref[sl, :] += m5 * tot


def _combine_call(cnt, x2d, posm, aff_tm, mvec, y_hl, cap, TT, crows, W, EP):
    S, E, T = posm.shape
    tps = T // TT
    R = S * cap
    ytot = E * R
    nsub = TT // SUB
    G = mvec.shape[0]

    def yspec(ep):
        def ymap(st, tl, eg, c):
            e = eg * EP + ep
            off = _align16(c[(st * E + e) * LANES + tl * nsub])
            return (pl.multiple_of(jnp.minimum(e * R + st * cap + off, ytot - W), 16), 0)
        return pl.BlockSpec((pl.Element(W), pl.Element(D_MODEL)), ymap)

    tok = lambda w: pl.BlockSpec((TT, w), lambda st, tl, e, c: (st * tps + tl, 0))
    mmap = (lambda st, tl, e, c: (st, 0, 0)) if G > 1 else (lambda st, tl, e, c: (0, 0, 0))
    gs = pltpu.PrefetchScalarGridSpec(
        num_scalar_prefetch=1,
        grid=(S, tps, E // EP),
        in_specs=[tok(D_MODEL),
                  pl.BlockSpec((1, EP, 1, TT), lambda st, tl, e, c: (st, e, 0, tl)),
                  tok(N_EXPERTS),
                  pl.BlockSpec((1, 1, D_MODEL), mmap)] + [yspec(ep) for ep in range(EP) for _ in range(2)],
        out_specs=tok(D_MODEL))
    yh, yl = y_hl[0].reshape(ytot, D_MODEL), y_hl[1].reshape(ytot, D_MODEL)
    return pl.pallas_call(
        functools.partial(_combine_kernel, TT=TT, cap=cap, R=R, W=W, ytot=ytot, crows=crows, EP=EP),
        grid_spec=gs,
        out_shape=jax.ShapeDtypeStruct(x2d.shape, f32),
        compiler_params=_cparams(("parallel", "parallel", "arbitrary")),
        name="moe_combine",
    )(cnt.reshape(-1), x2d, posm.reshape(S, E, 1, T), aff_tm, mvec.reshape(G, 1, D_MODEL), *([yh, yl] * EP))


def combine(cnt, x2d, posm, aff_tm, mvec, y_hl, cap, TT, fast_ok=None):
    safe = lambda: _combine_call(cnt, x2d, posm, aff_tm, mvec, y_hl, cap, TT, CMB_ROWS, TT + 32, 1)
    if fast_ok is None:
        return safe()
    wfast = FAST_TILE_MAX + 16 + FAST_ROWS
    return lax.cond(fast_ok, lambda: _combine_call(cnt, x2d, posm, aff_tm, mvec, y_hl, cap, TT, FAST_ROWS, wfast, 4),
                    safe)


def _rope_tables(n):
    half = NA_HD // 2
    q = half // 2
    inv = ROPE_THETA ** (-np.arange(q, dtype=np.float64) / q)
    pos = np.arange(n)
    ang_r = (pos // GRID_W)[:, None] * inv
    ang_c = (pos % GRID_W)[:, None] * inv
    zero = np.zeros_like(ang_r)
    c = np.concatenate([np.cos(ang_r)] * 2 + [np.cos(ang_c)] * 2, axis=1)
    s1 = np.concatenate([-np.sin(ang_r), zero, -np.sin(ang_c), zero], axis=1)
    s2 = np.concatenate([zero, np.sin(ang_r), zero, np.sin(ang_c)], axis=1)
    two = lambda a: jnp.asarray(np.concatenate([a, a], axis=1).astype(np.float32))
    return two(c), two(s1), two(s2)


def _mixing(hx, hc, need_ctx, B, N, NC, la, lc, w_in, q_gain, k_gain, table, rope, bd, conv_w, conv_b,
            spec, skip, filt_c, wa, wb, wc, wo, x2d, c2d, mx2, mc2):
    tile8 = lambda v: jnp.tile(v.reshape(1, NA_HD), (1, NA_HEADS))
    qg, kg = tile8(q_gain), tile8(k_gain)
    norm_aux = [("col", kg), ("const", bd)]
    rope_aux = [("row", rope[0]), ("row", rope[1]), ("row", rope[2])]
    lf_aux = lambda d: [("col", la[d:d + 1]), ("col", lc[d:d + 1])]
    tc = hc.shape[0]

    lff_c = project(hc, w_in, OFF_FF, HG_W, _epi_logforget, lf_aux(0), (f32,), tm=tc).reshape(B, NC, -1)
    lfb_c = project(hc, w_in, OFF_FB, HG_W, _epi_logforget, lf_aux(1), (f32,), tm=tc).reshape(B, NC, -1)
    i_c = project(hc, w_in, OFF_I, HG_W, _epi_raw, tm=tc).reshape(B, NC, -1)
    k_c = project(hc, w_in, OFF_NK, NA_W, _epi_norm, norm_aux, tm=tc).reshape(B, NC, -1)
    v_c = project(hc, w_in, OFF_NV, NA_W, _epi_raw, tm=tc).reshape(B, NC, -1)
    if need_ctx:
        q_c = project(hc, w_in, OFF_HQ, HG_W, _epi_silu, tm=tc).reshape(B, NC, -1)
    else:
        q_c = jnp.zeros((B, NC, HG_W), bf16)
    s0 = jnp.zeros((B, HG_HEADS, LANES, LANES), f32)
    oc_f, oc_b, s_f, s_b = hgrn_bidir(lff_c, lfb_c, i_c, q_c, s0, s0)

    lff_x = project(hx, w_in, OFF_FF, HG_W, _epi_logforget, lf_aux(0), (f32,)).reshape(B, N, -1)
    lfb_x = project(hx, w_in, OFF_FB, HG_W, _epi_logforget, lf_aux(1), (f32,)).reshape(B, N, -1)
    i_x = project(hx, w_in, OFF_I, HG_W, _epi_raw).reshape(B, N, -1)
    q_x = project(hx, w_in, OFF_HQ, HG_W, _epi_silu).reshape(B, N, -1)
    g_x = project(hx, w_in, OFF_HG, HG_W, _epi_silu).reshape(B, N, -1)
    k_x = project(hx, w_in, OFF_NK, NA_W, _epi_norm_rope, norm_aux + rope_aux, rows_per_seq=N).reshape(B, N, -1)
    v_x = project(hx, w_in, OFF_NV, NA_W, _epi_raw).reshape(B, N, -1)
    qn_x, qr_x = project(hx, w_in, OFF_NQ, NA_W, _epi_norm_both, [("col", qg), ("const", bd)] + rope_aux,
                         (bf16, bf16), rows_per_seq=N)
    p_x = project(hx, w_in, OFF_HY, 3 * HY_W, _epi_raw, out_dtypes=(f32,)).reshape(B, N, 3 * HY_W)
    gates_x = project(hx, w_in, OFF_GATE, 3 * D_MODEL, _epi_sigmoid)

    ox_f, ox_b, _, _ = hgrn_bidir(lff_x, lfb_x, i_x, q_x, s_f, s_b)

    nb_x = neighbourhood_attention(qr_x.reshape(B, N, -1), qn_x.reshape(B, N, -1), k_x, v_x, k_c, v_c, table)

    z_x, x0_x = hyena_pre(p_x, conv_w, conv_b)
    hy_x = hyena_conv(z_x, x0_x, spec[0], spec[1], skip, spec[2])

    flat = lambda a: a.reshape(-1, a.shape[-1])
    x_new = merge(flat(ox_f), flat(ox_b), flat(g_x), flat(nb_x), flat(hy_x), gates_x, wa, wb, wc, wo, x2d, mx2, N)
    if not need_ctx:
        return x_new, None

    qn_c = project(hc, w_in, OFF_NQ, NA_W, _epi_norm, [("col", qg), ("const", bd)], tm=tc).reshape(B, NC, -1)
    nb_c = context_attention(qn_c, k_c, v_c)
    p_c = project(hc, w_in, OFF_HY, 3 * HY_W, _epi_raw, out_dtypes=(f32,), tm=tc).reshape(B, NC, 3 * HY_W)
    hy_c = hyena_ctx(p_c, conv_w, conv_b, filt_c[0], filt_c[1], skip)
    gates_c = project(hc, w_in, OFF_GATE, 3 * D_MODEL, _epi_sigmoid, tm=tc)
    g_c = project(hc, w_in, OFF_HG, HG_W, _epi_silu, tm=tc)
    c_new = merge(flat(oc_f), flat(oc_b), g_c, flat(nb_c), flat(hy_c), gates_c, wa, wb, wc, wo, c2d, mc2, B * NC)
    return x_new, c_new


def kernel(x, c, ctx, c_ctx, w_mod, b_mod, norm_mix, norm_ffn, w_in, hg_lb, na_q_gain, na_k_gain, na_rpb,
           hy_conv_w, hy_conv_b, hy_pe_w1, hy_pe_b1, hy_pe_freq1, hy_pe_w2, hy_pe_b2, hy_pe_freq2, hy_pe_w3,
           hy_skip, w_branch_a, w_branch_b, w_branch_c, w_out, w_router, w_e_gate, w_e_up, w_e_down):
    B, N, D = x.shape
    NC = ctx.shape[1]
    E = N_EXPERTS
    cap_x = EC_CAP_FACTOR * N // E
    cap_c = EC_CAP_FACTOR * NC // E

    lb = jnp.cumsum(jax.nn.softmax(hg_lb.astype(f32), axis=0), axis=0)
    lb = lb - lb[:1]
    la_all, lc_all = jnp.log(lb), jnp.log1p(-lb)

    s8 = jnp.zeros((8, D), f32).at[:B].set(c).at[B].set(c_ctx)
    rope = _rope_tables(N)
    bd = jnp.asarray(np.kron(np.eye(NA_HEADS), np.full((NA_HD, NA_HD), 1.0 / NA_HD)).astype(np.float32), bf16)
    dcx = _dft_consts(N)

    x2d = x.reshape(B * N, D)
    c2d = ctx.reshape(B * NC, D)
    for l in range(DEPTH):
        need_ctx = l < DEPTH - 1
        mv = modvec(s8, w_mod[l], b_mod[l])
        mx = [mv[:B, k * D:(k + 1) * D] for k in range(6)]
        mc = [mv[B:B + 1, k * D:(k + 1) * D] for k in range(6)]
        w_in_l = w_in[l].astype(bf16)
        hx = modulate(x2d, norm_mix[l], mx[0], mx[1], N, bf16)
        hc = modulate(c2d, norm_mix[l], mc[0], mc[1], B * NC, bf16)

        filt = (hy_pe_w1[l], hy_pe_b1[l], hy_pe_freq1[l], hy_pe_w2[l], hy_pe_b2[l], hy_pe_freq2[l], hy_pe_w3[l])
        k_x, nrm_x = hyena_filter(N, *filt)
        sre, sim = hyena_spectrum(k_x, 1.0 / nrm_x, dcx)
        skip = hy_skip[l].reshape(1, HY_W)
        filt_c = None
        if need_ctx:
            h_c, nrm_c = hyena_filter(NC, *filt)
            filt_c = (h_c, 1.0 / nrm_c)

        x2d, c_new = _mixing(
            hx, hc, need_ctx, B, N, NC, la_all[l], lc_all[l], w_in_l, na_q_gain[l], na_k_gain[l],
            _na_bias_table(na_rpb[l]), rope, bd, hy_conv_w[l], hy_conv_b[l], (sre, sim, dcx), skip,
            filt_c, w_branch_a[l].astype(bf16), w_branch_b[l].astype(bf16), w_branch_c[l].astype(bf16),
            w_out[l].astype(bf16), x2d, c2d, mx[2], mc[2])

        h2, aff_t, aff_m = router(x2d, norm_ffn[l], mx[3], mx[4], w_router[l], N)
        posm, cnt = select_topk(aff_t, cap_x)
        fast_ok = _fits_fast(cnt, N)
        xgs = [gather_rows(cnt, h2, posm, cap_x, 2048, fast_ok)]
        if need_ctx:
            c2d = c_new
            hc2, aff_tc, aff_mc = router(c2d, norm_ffn[l], mc[3], mc[4], w_router[l], NC)
            posm_c, cnt_c = select_topk(aff_tc, cap_c)
            xgs.append(gather_rows(cnt_c, hc2, posm_c, cap_c, NC))
        ys = expert_ffn(xgs, l, w_e_gate, w_e_up, w_e_down)
        x2d = combine(cnt, x2d, posm, aff_m, mx[5], ys[0], cap_x, CMB_TILE, fast_ok)
        if need_ctx:
            c2d = combine(cnt_c, c2d, posm_c, aff_mc, mc[5], ys[1], cap_c, NC)
    return x2d.reshape(B, N, D)
```

```python
import functools
import math

import numpy as np
import jax
import jax.numpy as jnp
from jax import lax
from jax.experimental import pallas as pl
from jax.experimental.pallas import tpu as pltpu

f32 = jnp.float32
bf16 = jnp.bfloat16
i32 = jnp.int32
HI = lax.Precision.HIGHEST

D_MODEL = 1024
DEPTH = 2
GRID_W = 64
EPS = 1e-6
HG_HEADS = 4
HG_W = 512
HG_CHUNK = 64
NA_HEADS = 8
NA_HD = 64
NA_W = 512
NA_WIN_R = 8
NA_WIN_C = 16
ROPE_THETA = 10000.0
HY_W = 512
HY_BANDS = 16
HY_PE_DIM = 1 + 2 * HY_BANDS
HY_FILT_HID = 64
HY_FAST_DECAY = 0.3
HY_SLOW_DECAY = 1.5
HY_TARGET = 1e-2
OFF_FF = 0
OFF_FB = 512
OFF_I = 1024
OFF_NK = 1536
OFF_NV = 2048
OFF_HQ = 2560
OFF_NQ = 3072
OFF_HG = 3584
OFF_HY = 4096
OFF_GATE = 5632
IN_COLS = 8704
N_EXPERTS = 16
EC_CAP_FACTOR = 2
D_FF_EXPERT = 2816

LANES = 128
NEG_BIG = -1e30
VMEM_LIMIT = 56 * 1024 * 1024


def _cparams(sem, vmem=VMEM_LIMIT):
    return pltpu.CompilerParams(dimension_semantics=sem, vmem_limit_bytes=vmem)


def _nt(a, b, precision=None):
    return lax.dot_general(a, b, (((1,), (1,)), ((), ())), precision=precision, preferred_element_type=f32)


def _tn(a, b, precision=None):
    return lax.dot_general(a, b, (((0,), (0,)), ((), ())), precision=precision, preferred_element_type=f32)


def _silu(x):
    return x * jax.nn.sigmoid(x)


def _split(x):
    hi = x.astype(bf16)
    return hi, (x - hi.astype(f32)).astype(bf16)


def _modvec_kernel(s_ref, w_ref, b_ref, o_ref):
    s = _silu(s_ref[...])
    o_ref[...] = jnp.dot(s, w_ref[...], precision=HI, preferred_element_type=f32) + b_ref[...]


def modvec(s8, w, b):
    n = w.shape[1]
    tn = 1024
    return pl.pallas_call(
        _modvec_kernel,
        grid=(n // tn,),
        in_specs=[pl.BlockSpec((8, D_MODEL), lambda j: (0, 0)),
                  pl.BlockSpec((D_MODEL, tn), lambda j: (0, j)),
                  pl.BlockSpec((1, tn), lambda j: (0, j))],
        out_specs=pl.BlockSpec((8, tn), lambda j: (0, j)),
        out_shape=jax.ShapeDtypeStruct((8, n), f32),
        compiler_params=_cparams(("parallel",)),
        name="modvec",
    )(s8, w, b.reshape(1, n))


def _modulate_kernel(x_ref, g_ref, sh_ref, sc_ref, o_ref):
    x = x_ref[...]
    ms = jnp.mean(x * x, axis=-1, keepdims=True)
    y = x * lax.rsqrt(ms + EPS)
    o_ref[...] = (y * g_ref[...] * (1.0 + sc_ref[0]) + sh_ref[0]).astype(o_ref.dtype)


def modulate(x2d, g, shift, scale, rows_per_group, out_dtype):
    R = x2d.shape[0]
    tm = 512
    tpg = rows_per_group // tm
    G = shift.shape[0]
    return pl.pallas_call(
        _modulate_kernel,
        grid=(R // tm,),
        in_specs=[pl.BlockSpec((tm, D_MODEL), lambda i: (i, 0)),
                  pl.BlockSpec((1, D_MODEL), lambda i: (0, 0)),
                  pl.BlockSpec((1, 1, D_MODEL), lambda i: (i // tpg, 0, 0)),
                  pl.BlockSpec((1, 1, D_MODEL), lambda i: (i // tpg, 0, 0))],
        out_specs=pl.BlockSpec((tm, D_MODEL), lambda i: (i, 0)),
        out_shape=jax.ShapeDtypeStruct((R, D_MODEL), out_dtype),
        compiler_params=_cparams(("parallel",)),
        name="modulate",
    )(x2d, g.reshape(1, D_MODEL), shift.reshape(G, 1, D_MODEL), scale.reshape(G, 1, D_MODEL))


def _log1p_exp_neg(a):
    return jnp.log(1.0 + jnp.exp(-a))


def _log_sigmoid(z):
    return jnp.minimum(z, 0.0) - _log1p_exp_neg(jnp.abs(z))


def _epi_raw(acc, o_ref):
    o_ref[...] = acc.astype(o_ref.dtype)


def _epi_silu(acc, o_ref):
    o_ref[...] = _silu(acc).astype(o_ref.dtype)


def _epi_sigmoid(acc, o_ref):
    o_ref[...] = jax.nn.sigmoid(acc).astype(o_ref.dtype)


def _epi_logforget(acc, la_ref, lc_ref, o_ref):
    la = la_ref[...]
    c = lc_ref[...] + _log_sigmoid(acc)
    o_ref[...] = jnp.maximum(la, c) + _log1p_exp_neg(jnp.abs(la - c))


def _head_rms(acc, gain_ref, bd_ref):
    hi, lo = _split(acc * acc)
    ms = jnp.dot(hi, bd_ref[...], preferred_element_type=f32) + jnp.dot(lo, bd_ref[...], preferred_element_type=f32)
    return acc * lax.rsqrt(ms + EPS) * gain_ref[...]


def _rope(y, c_ref, s1_ref, s2_ref):
    reps = y.shape[1] // LANES
    c = jnp.concatenate([c_ref[...]] * reps, axis=1)
    s1 = jnp.concatenate([s1_ref[...]] * reps, axis=1)
    s2 = jnp.concatenate([s2_ref[...]] * reps, axis=1)
    w = y.shape[1]
    return y * c + pltpu.roll(y, w - 16, axis=1) * s1 + pltpu.roll(y, 16, axis=1) * s2


def _epi_norm(acc, gain_ref, bd_ref, o_ref):
    o_ref[...] = _head_rms(acc, gain_ref, bd_ref).astype(o_ref.dtype)


def _epi_norm_rope(acc, gain_ref, bd_ref, c_ref, s1_ref, s2_ref, o_ref):
    y = _head_rms(acc, gain_ref, bd_ref)
    o_ref[...] = _rope(y, c_ref, s1_ref, s2_ref).astype(o_ref.dtype)


def _epi_norm_both(acc, gain_ref, bd_ref, c_ref, s1_ref, s2_ref, on_ref, or_ref):
    y = _head_rms(acc, gain_ref, bd_ref)
    on_ref[...] = y.astype(on_ref.dtype)
    or_ref[...] = _rope(y, c_ref, s1_ref, s2_ref).astype(or_ref.dtype)


def _proj_kernel(h_ref, w_ref, *rest, epi):
    acc = jnp.dot(h_ref[...], w_ref[...], preferred_element_type=f32)
    epi(acc, *rest)


def project(h, w, c0, width, epi, aux=(), out_dtypes=(bf16,), tm=2048, rows_per_seq=None):
    R = h.shape[0]
    tn = 512
    nj = width // tn
    cb = c0 // tn
    in_specs = [pl.BlockSpec((tm, D_MODEL), lambda i, j: (i, 0)),
                pl.BlockSpec((D_MODEL, tn), lambda i, j: (0, cb + j))]
    args = [h, w]
    for kind, arr in aux:
        if kind == "col":
            in_specs.append(pl.BlockSpec((1, tn), lambda i, j: (0, j)))
        elif kind == "const":
            in_specs.append(pl.BlockSpec(arr.shape, lambda i, j: (0, 0)))
        else:
            tps = rows_per_seq // tm
            in_specs.append(pl.BlockSpec((tm, LANES), lambda i, j: (i % tps, 0)))
        args.append(arr)
    out_specs = [pl.BlockSpec((tm, tn), lambda i, j: (i, j)) for _ in out_dtypes]
    out_shape = [jax.ShapeDtypeStruct((R, width), dt) for dt in out_dtypes]
    res = pl.pallas_call(
        functools.partial(_proj_kernel, epi=epi),
        grid=(R // tm, nj),
        in_specs=in_specs,
        out_specs=out_specs,
        out_shape=out_shape,
        compiler_params=_cparams(("parallel", "parallel")),
        name="proj_" + epi.__name__[5:],
    )(*args)
    return res[0] if len(res) == 1 else res


HG_MM_LEVELS = 3


def _hgrn_tmatrix(C, reverse):
    t = np.arange(C)
    tau = (C - 1 - t) if reverse else t
    tt, uu = tau[:, None], tau[None, :]
    T = np.zeros((1 + HG_MM_LEVELS, C, C), np.float32)
    T[0] = uu <= tt
    for l in range(HG_MM_LEVELS):
        same = (tt >> (l + 1)) == (uu >> (l + 1))
        tr = ((tt >> l) & 1) == 1
        ur = ((uu >> l) & 1) == 1
        T[1 + l] = same & ((tr & ur & (uu <= tt)) | (~tr & ~ur & (uu > tt)))
    return T.reshape((1 + HG_MM_LEVELS) * C, C)


def _hgrn_level_sums(b, l, C, reverse):
    h = 1 << l
    parts = []
    for r0 in range(0, C, 2 * h):
        tb = r0 + h if reverse else r0 + h - 1
        d = b[r0:r0 + 2 * h] - b[tb:tb + 1]
        first, second = d[:h], d[h:]
        parts += [first, -second] if reverse else [-first, second]
    return jnp.concatenate(parts, axis=0)


def _hgrn_step(chains, C):
    L = int(round(math.log2(C)))
    ti0 = lax.broadcasted_iota(i32, (C, C), 0)
    si0 = lax.broadcasted_iota(i32, (C, C), 1)

    es, kcs = [], []
    for lf, _, _, tm, _, reverse in chains:
        W = lf.shape[1]
        hi, lo = _split(lf)
        r = jnp.dot(tm, jnp.concatenate([hi, lo], axis=1), preferred_element_type=f32)
        r = r[:, :W] + r[:, W:]
        b = r[0:C]
        end_row = 0 if reverse else C - 1
        levels = [r[(1 + l) * C:(2 + l) * C] for l in range(HG_MM_LEVELS)]
        levels += [_hgrn_level_sums(b, l, C, reverse) for l in range(HG_MM_LEVELS, L)]
        es.append((b, b[end_row:end_row + 1] - b, levels))
        kcs.append(1.0 - jnp.exp(lf))

    items = []
    for (lf, v_all, q_all, _, s_view, reverse), (b_all, suf_all, levels), kc_all in zip(chains, es, kcs):
        ti, si = (C - 1 - ti0, C - 1 - si0) if reverse else (ti0, si0)
        end_row = 0 if reverse else C - 1
        lmasks = [((ti >> (l + 1)) == (si >> (l + 1))) & (((ti >> l) & 1) == 1) & (((si >> l) & 1) == 0)
                  for l in range(L)]
        for h in range(HG_HEADS):
            sl = slice(LANES * h, LANES * (h + 1))
            qh, kch, vh = q_all[:, sl], kc_all[:, sl], v_all[:, sl]
            b = b_all[:, sl]
            st = s_view[h]
            pairs = [(qh.astype(bf16), kch.astype(bf16), ti == si)]
            for l in range(L):
                x = jnp.exp(levels[l][:, sl])
                pairs.append(((qh * x).astype(bf16), (kch * x).astype(bf16), lmasks[l]))
            items.append(dict(qb=(qh * jnp.exp(b)).astype(bf16), st=st, pairs=pairs, vh=vh,
                              kd=(kch * jnp.exp(suf_all[:, sl])).astype(bf16),
                              decay=jnp.exp(b[end_row:end_row + 1, :]), view=s_view, h=h))

    for it in items:
        it["o"] = _nt(it["qb"], it["st"].astype(bf16))
        it["att"] = [(_nt(ql, kl), m) for ql, kl, m in it["pairs"]]

    outs = []
    for it in items:
        att = None
        for a, m in it["att"]:
            t = jnp.where(m, a, 0.0)
            att = t if att is None else att + t
        it["o"] = it["o"] + jnp.dot(att.astype(bf16), it["vh"], preferred_element_type=f32)
        it["view"][it["h"]] = it["st"] * it["decay"] + _tn(it["vh"], it["kd"])
    nh = HG_HEADS
    for c in range(len(chains)):
        outs.append(jnp.concatenate([items[c * nh + h]["o"] for h in range(nh)], axis=1))
    return outs


def _hgrn_kernel(lff_ref, lfb_ref, vf_ref, vb_ref, qf_ref, qb_ref, s0f_ref, s0b_ref, tf_ref, tb_ref,
                 of_ref, ob_ref, sff_ref, sfb_ref, s_scr, *, C, B, CPS):
    c = pl.program_id(0)

    @pl.when(c == 0)
    def _():
        s_scr[0] = s0f_ref[...]
        s_scr[1] = s0b_ref[...]

    for k in range(CPS):
        rf = slice(k * C, (k + 1) * C)
        rb = slice((CPS - 1 - k) * C, (CPS - k) * C)
        chains = []
        for b in range(B):
            chains.append((lff_ref[b, rf, :], vf_ref[b, rf, :], qf_ref[b, rf, :].astype(f32), tf_ref[...],
                           s_scr.at[0, b], False))
            chains.append((lfb_ref[b, rb, :], vb_ref[b, rb, :], qb_ref[b, rb, :].astype(f32), tb_ref[...],
                           s_scr.at[1, b], True))
        outs = _hgrn_step(chains, C)
        for b in range(B):
            of_ref[b, rf, :] = outs[2 * b]
            ob_ref[b, rb, :] = outs[2 * b + 1]

    @pl.when(c == pl.num_programs(0) - 1)
    def _():
        sff_ref[...] = s_scr[0]
        sfb_ref[...] = s_scr[1]


def hgrn_bidir(lf_f, lf_b, v, q, s0_f, s0_b):
    B, N, W = lf_f.shape
    C = HG_CHUNK
    cps = 2 if (N // C) % 2 == 0 else 1
    nch = N // (C * cps)
    tf = jnp.asarray(_hgrn_tmatrix(C, False), bf16)
    tb = jnp.asarray(_hgrn_tmatrix(C, True), bf16)
    fw = pl.BlockSpec((B, C * cps, W), lambda c: (0, c, 0))
    bw = pl.BlockSpec((B, C * cps, W), lambda c: (0, nch - 1 - c, 0))
    st = pl.BlockSpec((B, HG_HEADS, LANES, LANES), lambda c: (0, 0, 0, 0))
    tsp = pl.BlockSpec(tf.shape, lambda c: (0, 0))
    seq = jax.ShapeDtypeStruct((B, N, W), f32)
    sts = jax.ShapeDtypeStruct((B, HG_HEADS, LANES, LANES), f32)
    return pl.pallas_call(
        functools.partial(_hgrn_kernel, C=C, B=B, CPS=cps),
        grid=(nch,),
        in_specs=[fw, bw, fw, bw, fw, bw, st, st, tsp, tsp],
        out_specs=[fw, bw, st, st],
        out_shape=[seq, seq, sts, sts],
        scratch_shapes=[pltpu.VMEM((2, B, HG_HEADS, LANES, LANES), f32)],
        compiler_params=_cparams(("arbitrary",)),
        name="hgrn",
    )(lf_f, lf_b, v, v, q, q, s0_f, s0_b, tf, tb)


LOG2E = math.log2(math.e)


def _na_kernel(qr_ref, qn_ref, k_ref, v_ref, kc_ref, vc_ref, tab_ref, o_ref, s_scr, p_scr, inv_scr, *,
               rows_per_step, n_rows):
    g = pl.program_id(2)
    scale = NA_HD ** -0.5 * LOG2E
    lane = lax.broadcasted_iota(i32, (GRID_W, LANES), 1)
    kcx = kc_ref[0]
    vcx = vc_ref[0]
    win = NA_WIN_R * GRID_W
    ctx_len = kcx.shape[0]

    starts = []
    for i in range(rows_per_step):
        r = g * rows_per_step + i
        rs = jnp.clip(r - NA_WIN_R // 2, 0, n_rows - NA_WIN_R)
        off = rs - r + (NA_WIN_R - 1)
        start = pl.multiple_of(rs * GRID_W, GRID_W)
        starts.append(start)
        kw = k_ref[0, pl.ds(start, win), :]
        qr = qr_ref[0, i * GRID_W:(i + 1) * GRID_W, :]
        qn = qn_ref[0, i * GRID_W:(i + 1) * GRID_W, :]
        zq = jnp.zeros_like(qr)
        qrs = jnp.concatenate([jnp.where(lane < NA_HD, qr, zq), jnp.where(lane < NA_HD, zq, qr)], axis=0)
        qns = jnp.concatenate([jnp.where(lane < NA_HD, qn, zq), jnp.where(lane < NA_HD, zq, qn)], axis=0)
        bias = jnp.concatenate([tab_ref[0, off], tab_ref[1, off]], axis=0)
        row0 = 2 * i * GRID_W
        s_scr[row0:row0 + 2 * GRID_W, 0:win] = _nt(qrs, kw) * scale + bias
        s_scr[row0:row0 + 2 * GRID_W, win:win + ctx_len] = _nt(qns, kcx) * scale

    def softmax_rows(c, carry):
        r0 = pl.multiple_of(c * LANES, LANES)
        s = s_scr[pl.ds(r0, LANES), :]
        p = jnp.exp2(s - jnp.max(s, axis=-1, keepdims=True))
        inv_scr[pl.ds(r0, LANES), :] = 1.0 / jnp.sum(p, axis=-1, keepdims=True)
        p_scr[pl.ds(r0, LANES), :] = p.astype(bf16)
        return carry

    lax.fori_loop(0, 2 * rows_per_step * GRID_W // LANES, softmax_rows, 0, unroll=2)

    for i in range(rows_per_step):
        vw = v_ref[0, pl.ds(starts[i], win), :]
        row0 = 2 * i * GRID_W
        p = p_scr[row0:row0 + 2 * GRID_W, :]
        res = (jnp.dot(p[:, :win], vw, preferred_element_type=f32)
               + jnp.dot(p[:, win:], vcx, preferred_element_type=f32))
        res = res * inv_scr[row0:row0 + 2 * GRID_W, :]
        o_ref[0, i * GRID_W:(i + 1) * GRID_W, :] = jnp.where(lane < NA_HD, res[:GRID_W], res[GRID_W:]).astype(o_ref.dtype)


def _na_bias_table(rpb):
    col = jnp.arange(GRID_W)
    cs = jnp.clip(col - NA_WIN_C // 2, 0, GRID_W - NA_WIN_C)
    kc = jnp.arange(GRID_W)
    valid = (kc[None, :] >= cs[:, None]) & (kc[None, :] < cs[:, None] + NA_WIN_C)
    dc = jnp.clip(kc[None, :] - col[:, None] + (NA_WIN_C - 1), 0, 2 * NA_WIN_C - 2)
    bc = jnp.where(valid[None, None], rpb[:, :, dc], NEG_BIG)
    t2 = jnp.stack([bc[:, o:o + NA_WIN_R] for o in range(NA_WIN_R)], axis=1)
    t2 = t2.transpose(0, 1, 3, 2, 4)
    return t2.reshape(NA_HEADS, NA_WIN_R, GRID_W, NA_WIN_R * GRID_W).astype(f32) * LOG2E


def neighbourhood_attention(q_rot, qn, k_rot, v, kc, vc, table):
    B, N, W = q_rot.shape
    n_rows = N // GRID_W
    rps = 16 if n_rows % 16 == 0 else 8
    ctx_len = kc.shape[1]
    pairs = W // LANES
    keys = NA_WIN_R * GRID_W + ctx_len
    return pl.pallas_call(
        functools.partial(_na_kernel, rows_per_step=rps, n_rows=n_rows),
        grid=(B, pairs, n_rows // rps),
        in_specs=[pl.BlockSpec((1, rps * GRID_W, LANES), lambda b, p, g: (b, g, p)),
                  pl.BlockSpec((1, rps * GRID_W, LANES), lambda b, p, g: (b, g, p)),
                  pl.BlockSpec((1, N, LANES), lambda b, p, g: (b, 0, p)),
                  pl.BlockSpec((1, N, LANES), lambda b, p, g: (b, 0, p)),
                  pl.BlockSpec((1, ctx_len, LANES), lambda b, p, g: (b, 0, p)),
                  pl.BlockSpec((1, ctx_len, LANES), lambda b, p, g: (b, 0, p)),
                  pl.BlockSpec((2, NA_WIN_R, GRID_W, NA_WIN_R * GRID_W), lambda b, p, g: (p, 0, 0, 0))],
        out_specs=pl.BlockSpec((1, rps * GRID_W, LANES), lambda b, p, g: (b, g, p)),
        out_shape=jax.ShapeDtypeStruct((B, N, W), bf16),
        scratch_shapes=[pltpu.VMEM((2 * rps * GRID_W, keys), f32), pltpu.VMEM((2 * rps * GRID_W, keys), bf16),
                        pltpu.VMEM((2 * rps * GRID_W, 1), f32)],
        compiler_params=_cparams(("parallel", "parallel", "arbitrary")),
        name="natten",
    )(q_rot, qn, k_rot, v, kc, vc, table)


def _ctx_attn_kernel(q_ref, k_ref, v_ref, o_ref):
    scale = NA_HD ** -0.5
    q = q_ref[0]
    k = k_ref[0]
    v = v_ref[0]
    lane = lax.broadcasted_iota(i32, q.shape, 1)
    res = []
    for hh in range(2):
        m = (lane >= NA_HD * hh) & (lane < NA_HD * (hh + 1))
        s = _nt(jnp.where(m, q, jnp.zeros_like(q)), k) * scale
        p = jnp.exp(s - jnp.max(s, axis=-1, keepdims=True))
        p = p / jnp.sum(p, axis=-1, keepdims=True)
        res.append(jnp.dot(p.astype(bf16), v, preferred_element_type=f32))
    o_ref[0] = jnp.where(lane < NA_HD, res[0], res[1]).astype(o_ref.dtype)


def context_attention(q, k, v):
    B, N, W = q.shape
    spec = pl.BlockSpec((1, N, LANES), lambda b, p: (b, 0, p))
    return pl.pallas_call(
        _ctx_attn_kernel,
        grid=(B, W // LANES),
        in_specs=[spec, spec, spec],
        out_specs=spec,
        out_shape=jax.ShapeDtypeStruct((B, N, W), bf16),
        compiler_params=_cparams(("parallel", "parallel")),
        name="ctx_attn",
    )(q, k, v)


def _filter_kernel(z_ref, w1_ref, b1_ref, f1_ref, w2_ref, b2_ref, f2_ref, w3_ref, dl_ref, k_ref, nrm_ref, *, tm, n):
    i = pl.program_id(0)
    z = z_ref[...]
    dot = lambda a, b: jnp.dot(a, b, precision=HI, preferred_element_type=f32)
    w1 = w1_ref[...]
    a = jnp.sin(f1_ref[...] * (dot(z[:tm // 2], w1[:, :LANES]) + dot(z[tm // 2:], w1[:, LANES:]) + b1_ref[...]))
    a = jnp.sin(f2_ref[...] * (dot(a, w2_ref[...]) + b2_ref[...]))
    w3 = w3_ref[...]
    zero = jnp.zeros_like(w3)
    h = jnp.concatenate([dot(a, jnp.concatenate([w3, zero], axis=0)), dot(a, jnp.concatenate([zero, w3], axis=0))],
                        axis=0)
    h = h * jnp.exp(-z[:, 0:1] * dl_ref[...])
    row = lax.broadcasted_iota(i32, (tm, HY_W), 0) + i * tm
    k = jnp.where(row == n, 0.0, h)
    k_ref[...] = k
    part = jnp.sum(jnp.abs(k), axis=0, keepdims=True)

    @pl.when(i == 0)
    def _():
        nrm_ref[...] = part

    @pl.when(i > 0)
    def _():
        nrm_ref[...] = nrm_ref[...] + part


def hyena_filter(n, w1, b1, fr1, w2, b2, fr2, w3):
    t = np.linspace(0.0, 1.0, n)[:, None]
    w = 2 * math.pi * np.arange(n)[:, None] / n
    fb = np.linspace(1e-4, HY_BANDS - 1, HY_BANDS)[None]
    z = np.concatenate([t, np.cos(fb * w), -np.sin(fb * w)], axis=-1)
    z = np.concatenate([z, np.zeros((1, HY_PE_DIM)), z[:0:-1]], axis=0)
    z = jnp.asarray(np.pad(z, ((0, 0), (0, LANES - HY_PE_DIM))).astype(np.float32))
    w1p = jnp.pad(w1.astype(f32), ((0, LANES - HY_PE_DIM), (0, 0)))
    deltas = jnp.asarray(np.abs(np.linspace(math.log(HY_TARGET) / HY_SLOW_DECAY, math.log(HY_TARGET) / HY_FAST_DECAY,
                                            2 * HY_W))[None].astype(np.float32))
    tm = min(n, 512)
    hid = HY_FILT_HID
    full = lambda shape: pl.BlockSpec(shape, lambda i: (0, 0))
    tph = n // tm
    zpad = jnp.zeros((LANES, hid), f32)
    w1pk = jnp.concatenate([w1p, zpad, zpad, w1p], axis=1)
    w2f = w2.astype(f32)
    z2 = jnp.zeros((hid, hid), f32)
    w2pk = jnp.concatenate([jnp.concatenate([w2f, z2], axis=1), jnp.concatenate([z2, w2f], axis=1)], axis=0)
    twice = lambda v: jnp.tile(v.reshape(1, hid).astype(f32), (1, 2))
    return pl.pallas_call(
        functools.partial(_filter_kernel, tm=tm, n=n),
        grid=(2 * n // tm,),
        in_specs=[pl.BlockSpec((tm, LANES), lambda i: (i, 0)),
                  full((LANES, 2 * LANES)), full((1, LANES)), full((1, LANES)),
                  full((LANES, LANES)), full((1, LANES)), full((1, LANES)),
                  pl.BlockSpec((hid, HY_W), lambda i: (0, i // tph)),
                  pl.BlockSpec((1, HY_W), lambda i: (0, i // tph))],
        out_specs=[pl.BlockSpec((tm, HY_W), lambda i: (i, 0)), pl.BlockSpec((1, HY_W), lambda i: (0, 0))],
        out_shape=[jax.ShapeDtypeStruct((2 * n, HY_W), f32), jax.ShapeDtypeStruct((1, HY_W), f32)],
        compiler_params=_cparams(("arbitrary",)),
        name="hyena_filter",
    )(z, w1pk, twice(b1), twice(fr1), w2pk, twice(b2), twice(fr2), w3.astype(f32), deltas)


def _conv3(u, w_ref, b_ref):
    n = u.shape[0]
    row = lax.broadcasted_iota(i32, u.shape, 0)
    prev = jnp.where(row == 0, 0.0, pltpu.roll(u, 1, axis=0))
    nxt = jnp.where(row == n - 1, 0.0, pltpu.roll(u, n - 1, axis=0))
    return prev * w_ref[0:1, :] + u * w_ref[1:2, :] + nxt * w_ref[2:3, :] + b_ref[...]


def _hyena_pre_kernel(p0_ref, p1_ref, p2_ref, w0_ref, w1_ref, w2_ref, b0_ref, b1_ref, b2_ref, z_ref, x0_ref):
    x0_ref[0] = _conv3(p0_ref[0], w0_ref, b0_ref).astype(x0_ref.dtype)
    z_ref[0] = _conv3(p1_ref[0], w1_ref, b1_ref) * _conv3(p2_ref[0], w2_ref, b2_ref)


def hyena_pre(p, conv_w, conv_b):
    B, N, _ = p.shape
    nb = HY_W // LANES
    conv_b = conv_b.reshape(1, 3 * HY_W)
    pspec = lambda g: pl.BlockSpec((1, N, LANES), lambda b, c: (b, 0, g * nb + c))
    wspec = lambda g: pl.BlockSpec((3, LANES), lambda b, c: (0, g * nb + c))
    bspec = lambda g: pl.BlockSpec((1, LANES), lambda b, c: (0, g * nb + c))
    ospec = pl.BlockSpec((1, N, LANES), lambda b, c: (b, 0, c))
    return pl.pallas_call(
        _hyena_pre_kernel,
        grid=(B, nb),
        in_specs=[pspec(0), pspec(1), pspec(2), wspec(0), wspec(1), wspec(2), bspec(0), bspec(1), bspec(2)],
        out_specs=[ospec, ospec],
        out_shape=[jax.ShapeDtypeStruct((B, N, HY_W), f32), jax.ShapeDtypeStruct((B, N, HY_W), bf16)],
        compiler_params=_cparams(("parallel", "parallel")),
        name="hyena_pre",
    )(p, p, p, conv_w, conv_w, conv_w, conv_b, conv_b, conv_b)


DFT_SLABS = 8


def _hl(a):
    a32 = jnp.asarray(a.astype(np.float32))
    hi = a32.astype(bf16)
    lo = (a32 - hi.astype(f32)).astype(bf16)
    return jnp.concatenate([hi, lo], axis=-2)


def _dot3(a_hl, m, x):
    xh, xl = _split(x)
    r = jnp.dot(a_hl, xh, preferred_element_type=f32)
    return r[:m] + r[m:] + jnp.dot(a_hl[:m], xl, preferred_element_type=f32)


def _dft_consts(n):
    N = 2 * n
    na = N // LANES
    t1n = na // 2
    k1n = na // 2 + 1
    k1p = -(-k1n // 8) * 8
    k1 = np.arange(k1n)
    t1 = np.arange(t1n)
    th = 2 * np.pi * ((t1[None, :] * k1[:, None]) % na) / na
    f1c = np.zeros((2 * k1p, t1n))
    f1c[:k1n] = np.cos(th)
    f1c[k1p:k1p + k1n] = -np.sin(th)
    thf = 2 * np.pi * ((np.arange(na)[None, :] * k1[:, None]) % na) / na
    f1f = np.zeros((2 * k1p, na))
    f1f[:k1n] = np.cos(thf)
    f1f[k1p:k1p + k1n] = -np.sin(thf)
    k2 = np.arange(LANES)
    t2 = np.arange(LANES)
    m = (t2[None, None, :] * (k1[:, None, None] + na * k2[None, :, None])) % N
    ph = 2 * np.pi * m / N
    g = np.concatenate([np.cos(ph), -np.sin(ph)], axis=1)
    pht = ph.transpose(0, 2, 1)
    gi = np.concatenate([np.cos(pht), np.sin(pht)], axis=1)
    wk = np.where((k1 == 0) | (k1 == na // 2), 1.0, 2.0) / N
    k1h = na // 2
    f1i = np.concatenate([np.cos(th.T)[:, :k1h] * wk[None, :k1h], -np.sin(th.T)[:, :k1h] * wk[None, :k1h]], axis=1)
    k1e = -(-k1n // DFT_SLABS) * DFT_SLABS
    g = np.concatenate([g, np.zeros((k1e - k1n,) + g.shape[1:])], axis=0)
    gi = np.concatenate([gi, np.zeros((k1e - k1n,) + gi.shape[1:])], axis=0)
    return dict(na=na, t1n=t1n, k1n=k1n, k1e=k1e, k1p=k1p, f1c=_hl(f1c), f1f=_hl(f1f), g=_hl(g), gi=_hl(gi),
                f1i=_hl(f1i))


def _dft_stage1(src_ref, f1c_ref, tre_ref, tim_ref, t1n, k1p):
    f1c = f1c_ref[...]

    def body(t2, carry):
        zs = src_ref[pl.ds(t2, t1n, stride=LANES), :]
        r = _dot3(f1c, 2 * k1p, zs)
        r0 = pl.multiple_of(t2 * k1p, 8)
        tre_ref[pl.ds(r0, k1p), :] = r[:k1p]
        tim_ref[pl.ds(r0, k1p), :] = r[k1p:]
        return carry

    lax.fori_loop(0, LANES, body, 0, unroll=4)


def _slab(tre_ref, tim_ref, k1, k1p):
    return tre_ref[pl.ds(k1, LANES, stride=k1p), :], tim_ref[pl.ds(k1, LANES, stride=k1p), :]


def _cplx_left(gc_hl, xre, xim):
    cw = xre.shape[1]
    r = _dot3(gc_hl, 2 * LANES, jnp.concatenate([xre, xim], axis=1))
    p, q = r[:, :cw], r[:, cw:]
    return p[:LANES] - q[LANES:], p[LANES:] + q[:LANES]


def _spectrum_kernel(k_ref, f1f_ref, g_ref, inv_ref, xre_ref, xim_ref, are, aim, *, na, k1p):
    j = pl.program_id(1)

    @pl.when(j == 0)
    def _():
        _dft_stage1(k_ref, f1f_ref, are, aim, na, k1p)

    for half in range(DFT_SLABS):
        xre, xim = _cplx_left(g_ref[half], *_slab(are, aim, DFT_SLABS * j + half, k1p))
        rows = slice(half * LANES, (half + 1) * LANES)
        xre_ref[rows, :] = xre * inv_ref[...]
        xim_ref[rows, :] = xim * inv_ref[...]


def hyena_spectrum(k, inv_norm, dc):
    n2, C = k.shape
    k1e, k1p, na = dc["k1e"], dc["k1p"], dc["na"]
    cw = LANES
    out = jax.ShapeDtypeStruct((k1e * LANES, C), f32)
    ospec = pl.BlockSpec((DFT_SLABS * LANES, cw), lambda c, k: (k, c))
    return pl.pallas_call(
        functools.partial(_spectrum_kernel, na=na, k1p=k1p),
        grid=(C // cw, k1e // DFT_SLABS),
        in_specs=[pl.BlockSpec((n2, cw), lambda c, k: (0, c)),
                  pl.BlockSpec(dc["f1f"].shape, lambda c, k: (0, 0)),
                  pl.BlockSpec((DFT_SLABS, 4 * LANES, LANES), lambda c, k: (k, 0, 0)),
                  pl.BlockSpec((1, cw), lambda c, k: (0, c))],
        out_specs=[ospec, ospec],
        out_shape=[out, out],
        scratch_shapes=[pltpu.VMEM((k1p * LANES, cw), f32), pltpu.VMEM((k1p * LANES, cw), f32)],
        compiler_params=_cparams(("parallel", "arbitrary")),
        name="hyena_spectrum",
    )(k, dc["f1f"], dc["g"], inv_norm)


def _hyena_conv_kernel(z_ref, x0_ref, f1c_ref, g_ref, gi_ref, f1i_ref, kre_ref, kim_ref,
                       skip_ref, o_ref, tre, tim, are, aim, y_scr, *, t1n, k1p):
    j = pl.program_id(2)

    @pl.when(j == 0)
    def _():
        _dft_stage1(z_ref.at[0], f1c_ref, tre, tim, t1n, k1p)

    for half in range(DFT_SLABS):
        r0 = pl.multiple_of((DFT_SLABS * j + half) * LANES, LANES)
        xre, xim = _cplx_left(g_ref[half], *_slab(tre, tim, DFT_SLABS * j + half, k1p))
        rows = slice(half * LANES, (half + 1) * LANES)
        kre = kre_ref[rows, :]
        kim = kim_ref[rows, :]
        yre = xre * kre - xim * kim
        yim = xre * kim + xim * kre
        bre, bim = _cplx_left(gi_ref[half], yre, yim)
        are[pl.ds(r0, LANES), :] = bre
        aim[pl.ds(r0, LANES), :] = bim

    @pl.when(j == pl.num_programs(2) - 1)
    def _():
        f1i = f1i_ref[...]
        k1h = t1n

        def body(t2, carry):
            bb = jnp.concatenate([are[pl.ds(t2, k1h, stride=LANES), :], aim[pl.ds(t2, k1h, stride=LANES), :]], axis=0)
            y_scr[pl.ds(t2, t1n, stride=LANES), :] = _dot3(f1i, t1n, bb)
            return carry

        lax.fori_loop(0, LANES, body, 0, unroll=4)
        r0 = k1h * LANES
        nyq = are[r0:r0 + LANES, :] * (1.0 / (2 * t1n * LANES))
        nyq = jnp.concatenate([nyq, -nyq] * (t1n // 2), axis=0)
        z = z_ref[0]
        o_ref[0] = (x0_ref[0].astype(f32) * (y_scr[...] + nyq + z * skip_ref[...])).astype(o_ref.dtype)


def hyena_conv(z, x0, spec_re, spec_im, skip, dc):
    B, n, W = z.shape
    nb = W // LANES
    k1e, k1p, t1n = dc["k1e"], dc["k1p"], dc["t1n"]
    seq = pl.BlockSpec((1, n, LANES), lambda b, c, k: (b, 0, c))
    fspec = pl.BlockSpec((DFT_SLABS * LANES, LANES), lambda b, c, k: (k, c))
    cspec = pl.BlockSpec((DFT_SLABS, 4 * LANES, LANES), lambda b, c, k: (k, 0, 0))
    vspec = pl.BlockSpec((1, LANES), lambda b, c, k: (0, c))
    return pl.pallas_call(
        functools.partial(_hyena_conv_kernel, t1n=t1n, k1p=k1p),
        grid=(B, nb, k1e // DFT_SLABS),
        in_specs=[seq, seq,
                  pl.BlockSpec(dc["f1c"].shape, lambda b, c, k: (0, 0)), cspec, cspec,
                  pl.BlockSpec(dc["f1i"].shape, lambda b, c, k: (0, 0)),
                  fspec, fspec, vspec],
        out_specs=seq,
        out_shape=jax.ShapeDtypeStruct((B, n, W), bf16),
        scratch_shapes=[pltpu.VMEM((k1p * LANES, LANES), f32)] * 4 + [pltpu.VMEM((n, LANES), f32)],
        compiler_params=_cparams(("parallel", "parallel", "arbitrary")),
        name="hyena_conv",
    )(z, x0, dc["f1c"], dc["g"], dc["gi"], dc["f1i"], spec_re, spec_im, skip)


def _hyena_ctx_kernel(p0_ref, p1_ref, p2_ref, w0_ref, w1_ref, w2_ref, b0_ref, b1_ref, b2_ref,
                      k_ref, inv_ref, skip_ref, fd_ref, fi_ref, o_ref, *, n):
    x0 = _conv3(p0_ref[0], w0_ref, b0_ref)
    z = _conv3(p1_ref[0], w1_ref, b1_ref) * _conv3(p2_ref[0], w2_ref, b2_ref)
    fd = fd_ref[...]
    N = 2 * n
    zf = jnp.dot(fd[:, :n], z, precision=HI, preferred_element_type=f32)
    kf = jnp.dot(fd, k_ref[...], precision=HI, preferred_element_type=f32) * inv_ref[...]
    yre = zf[:N] * kf[:N] - zf[N:] * kf[N:]
    yim = zf[:N] * kf[N:] + zf[N:] * kf[:N]
    y = jnp.dot(fi_ref[...], jnp.concatenate([yre, yim], axis=0), precision=HI, preferred_element_type=f32)
    o_ref[0] = (x0 * (y + z * skip_ref[...])).astype(o_ref.dtype)


def hyena_ctx(p, conv_w, conv_b, k, inv_norm, skip):
    B, n, _ = p.shape
    N = 2 * n
    nb = HY_W // LANES
    kk = np.arange(N)
    ph = 2 * np.pi * ((kk[:, None] * kk[None, :]) % N) / N
    fd = jnp.asarray(np.concatenate([np.cos(ph), -np.sin(ph)], axis=0).astype(np.float32))
    fi = jnp.asarray((np.concatenate([np.cos(ph[:n]), -np.sin(ph[:n])], axis=1) / N).astype(np.float32))
    conv_b = conv_b.reshape(1, 3 * HY_W)
    pspec = lambda g: pl.BlockSpec((1, n, LANES), lambda b, c: (b, 0, g * nb + c))
    wspec = lambda g: pl.BlockSpec((3, LANES), lambda b, c: (0, g * nb + c))
    bspec = lambda g: pl.BlockSpec((1, LANES), lambda b, c: (0, g * nb + c))
    vspec = pl.BlockSpec((1, LANES), lambda b, c: (0, c))
    return pl.pallas_call(
        functools.partial(_hyena_ctx_kernel, n=n),
        grid=(B, nb),
        in_specs=[pspec(0), pspec(1), pspec(2), wspec(0), wspec(1), wspec(2), bspec(0), bspec(1), bspec(2),
                  pl.BlockSpec((N, LANES), lambda b, c: (0, c)),
                  vspec, vspec,
                  pl.BlockSpec(fd.shape, lambda b, c: (0, 0)), pl.BlockSpec(fi.shape, lambda b, c: (0, 0))],
        out_specs=pl.BlockSpec((1, n, LANES), lambda b, c: (b, 0, c)),
        out_shape=jax.ShapeDtypeStruct((B, n, HY_W), bf16),
        compiler_params=_cparams(("parallel", "parallel")),
        name="hyena_ctx",
    )(p, p, p, conv_w, conv_w, conv_w, conv_b, conv_b, conv_b, k, inv_norm, skip, fd, fi)


def _merge_kernel(of_ref, ob_ref, gs_ref, nb_ref, hc_ref, g_ref, wa_ref, wb_ref, wc_ref, wo_ref, x_ref, m_ref, o_ref):
    d = D_MODEL
    tot = of_ref[...] + ob_ref[...]
    gs = gs_ref[...].astype(f32)
    ra = []
    for h in range(HG_HEADS):
        sl = slice(LANES * h, LANES * (h + 1))
        th = tot[:, sl]
        ms = jnp.mean(th * th, axis=-1, keepdims=True)
        ra.append(th * lax.rsqrt(ms + EPS) * gs[:, sl])
    ya = jnp.dot(jnp.concatenate(ra, axis=1).astype(bf16), wa_ref[...], preferred_element_type=f32)
    yb = jnp.dot(nb_ref[...], wb_ref[...], preferred_element_type=f32)
    yc = jnp.dot(hc_ref[...], wc_ref[...], preferred_element_type=f32)
    g = g_ref[...].astype(f32)
    mix = g[:, :d] * ya + g[:, d:2 * d] * yb + g[:, 2 * d:] * yc
    y = jnp.dot(mix.astype(bf16), wo_ref[...], preferred_element_type=f32)
    o_ref[...] = x_ref[...] + m_ref[0] * y


def merge(o_f, o_b, gs, nb, hc, gates, wa, wb, wc, wo, x2d, m, rows_per_group):
    R = x2d.shape[0]
    tm = 512
    tpg = rows_per_group // tm
    G = m.shape[0]
    row = lambda w: pl.BlockSpec((tm, w), lambda i: (i, 0))
    full = lambda a: pl.BlockSpec(a.shape, lambda i: (0, 0))
    return pl.pallas_call(
        _merge_kernel,
        grid=(R // tm,),
        in_specs=[row(HG_W), row(HG_W), row(HG_W), row(NA_W), row(HY_W), row(3 * D_MODEL),
                  full(wa), full(wb), full(wc), full(wo),
                  row(D_MODEL), pl.BlockSpec((1, 1, D_MODEL), lambda i: (i // tpg, 0, 0))],
        out_specs=row(D_MODEL),
        out_shape=jax.ShapeDtypeStruct((R, D_MODEL), f32),
        compiler_params=_cparams(("parallel",)),
        name="merge",
    )(o_f, o_b, gs, nb, hc, gates, wa, wb, wc, wo, x2d, m.reshape(G, 1, D_MODEL))


def _router_kernel(x_ref, g_ref, sh_ref, sc_ref, wrt_ref, wr_ref, h_ref, at_ref, am_ref):
    x = x_ref[...]
    ms = jnp.mean(x * x, axis=-1, keepdims=True)
    h = x * lax.rsqrt(ms + EPS) * g_ref[...] * (1.0 + sc_ref[0]) + sh_ref[0]
    h_ref[...] = h.astype(h_ref.dtype)
    hh, hl = _split(h)
    ne = N_EXPERTS
    wt = wrt_ref[...]
    rt = _nt(wt, hh)
    lt = rt[:ne] + rt[ne:] + _nt(wt[:ne], hl)
    et = jnp.exp(lt - jnp.max(lt, axis=0, keepdims=True))
    at_ref[0] = et / jnp.sum(et, axis=0, keepdims=True)
    wm = wr_ref[...]
    rm = jnp.dot(hh, wm, preferred_element_type=f32)
    lm = rm[:, :ne] + rm[:, ne:] + jnp.dot(hl, wm[:, :ne], preferred_element_type=f32)
    em = jnp.exp(lm - jnp.max(lm, axis=1, keepdims=True))
    am_ref[...] = em / jnp.sum(em, axis=1, keepdims=True)


def router(x2d, g, shift, scale, w_router, n_per_set):
    R = x2d.shape[0]
    tm = min(512, n_per_set)
    tps = n_per_set // tm
    S = R // n_per_set
    G = shift.shape[0]
    gmap = (lambda i: (i // tps, 0, 0)) if G > 1 else (lambda i: (0, 0, 0))
    whi, wlo = _split(w_router.astype(f32))
    wr = jnp.concatenate([whi, wlo], axis=1)
    return pl.pallas_call(
        _router_kernel,
        grid=(R // tm,),
        in_specs=[pl.BlockSpec((tm, D_MODEL), lambda i: (i, 0)),
                  pl.BlockSpec((1, D_MODEL), lambda i: (0, 0)),
                  pl.BlockSpec((1, 1, D_MODEL), gmap),
                  pl.BlockSpec((1, 1, D_MODEL), gmap),
                  pl.BlockSpec((2 * N_EXPERTS, D_MODEL), lambda i: (0, 0)),
                  pl.BlockSpec((D_MODEL, 2 * N_EXPERTS), lambda i: (0, 0))],
        out_specs=[pl.BlockSpec((tm, D_MODEL), lambda i: (i, 0)),
                   pl.BlockSpec((1, N_EXPERTS, tm), lambda i: (i // tps, 0, i % tps)),
                   pl.BlockSpec((tm, N_EXPERTS), lambda i: (i, 0))],
        out_shape=[jax.ShapeDtypeStruct((R, D_MODEL), bf16),
                   jax.ShapeDtypeStruct((S, N_EXPERTS, n_per_set), f32),
                   jax.ShapeDtypeStruct((R, N_EXPERTS), f32)],
        compiler_params=_cparams(("parallel",)),
        name="router",
    )(x2d, g.reshape(1, D_MODEL), shift.reshape(G, 1, D_MODEL), scale.reshape(G, 1, D_MODEL), wr.T, wr)


SEL_BLK = 256
SUB = LANES
SUBW = SUB + 8
UNSEL = -float(2 ** 30)


def _prefix_incl(mask_f, tri, T):
    outs = []
    off = jnp.zeros((mask_f.shape[0], 1), f32)
    for b in range(T // SEL_BLK):
        blk = mask_f[:, b * SEL_BLK:(b + 1) * SEL_BLK].astype(bf16)
        pre = jnp.dot(blk, tri, preferred_element_type=f32) + off
        outs.append(pre)
        off = pre[:, SEL_BLK - 1:SEL_BLK]
    return jnp.concatenate(outs, axis=1)


def _select_kernel(a_ref, tri_ref, cm_ref, posm_ref, cnt_ref, *, T, cap):
    aff = a_ref[0]
    bits = pltpu.bitcast(aff, i32)
    tri = tri_ref[...]

    def bit_step(i, thr):
        cand = thr | (1 << (30 - i))
        cnt = jnp.sum((bits >= cand).astype(f32), axis=1, keepdims=True)
        return jnp.where(cnt >= cap, cand, thr)

    thr = lax.fori_loop(0, 31, bit_step, jnp.zeros((N_EXPERTS, 1), i32))
    gt = bits > thr
    eq = bits == thr
    need = cap - jnp.sum(gt.astype(f32), axis=1, keepdims=True)
    eqf = eq.astype(f32)
    rank_eq = _prefix_incl(eqf, tri, T) - eqf
    sel = gt | (eq & (rank_eq < need))
    self_ = sel.astype(f32)
    pos = _prefix_incl(self_, tri, T) - self_
    posm_ref[0] = jnp.where(sel, pos, UNSEL)
    cnt_ref[0] = jnp.dot(self_.astype(bf16), cm_ref[...], preferred_element_type=f32).astype(i32)


def select_topk(aff, cap):
    S, E, T = aff.shape
    tri = jnp.asarray(np.triu(np.ones((SEL_BLK, SEL_BLK), np.float32)), bf16)
    cm = jnp.asarray((np.arange(T)[:, None] < np.arange(LANES)[None, :] * SUB).astype(np.float32), bf16)
    return pl.pallas_call(
        functools.partial(_select_kernel, T=T, cap=cap),
        grid=(S,),
        in_specs=[pl.BlockSpec((1, E, T), lambda s: (s, 0, 0)),
                  pl.BlockSpec((SEL_BLK, SEL_BLK), lambda s: (0, 0)),
                  pl.BlockSpec((T, LANES), lambda s: (0, 0))],
        out_specs=[pl.BlockSpec((1, E, T), lambda s: (s, 0, 0)),
                   pl.BlockSpec((1, E, LANES), lambda s: (s, 0, 0))],
        out_shape=[jax.ShapeDtypeStruct((S, E, T), f32), jax.ShapeDtypeStruct((S, E, LANES), i32)],
        compiler_params=_cparams(("parallel",)),
        name="select_topk",
    )(aff, tri, cm)


def _align8(v):
    return lax.shift_left(lax.shift_right_logical(v, 3), 3)


def _align16(v):
    return lax.shift_left(lax.shift_right_logical(v, 4), 4)


CMB_ROWS = SUB + 16


def _gather_kernel(cnt_ref, h_ref, pos_ref, o_ref, acc, *, TT, cap, tps, R, srows, EP):
    tl = pl.program_id(1)

    @pl.when(tl == 0)
    def _():
        acc[...] = jnp.zeros_like(acc)

    st = tl // tps
    nsub = TT // SUB
    rid = lax.broadcasted_iota(i32, (srows, SUB), 0).astype(f32)
    for ep in range(EP):
        e = pl.program_id(0) * EP + ep
        cbase = (st * N_EXPERTS + e) * LANES + (tl % tps) * nsub
        for s in range(nsub):
            off8 = _align8(cnt_ref[cbase + s])
            pos = pos_ref[0, ep, :, s * SUB:(s + 1) * SUB]
            onehot = jnp.where(pos == rid + off8.astype(f32), 1.0, 0.0).astype(bf16)
            rows = jnp.dot(onehot, h_ref[s * SUB:(s + 1) * SUB, :], preferred_element_type=f32)
            r0 = pl.multiple_of(st * cap + off8, 8)
            acc[ep, pl.ds(r0, srows), :] += rows

    @pl.when(tl == pl.num_programs(1) - 1)
    def _():
        o_ref[...] = acc[:, 0:R, :].astype(o_ref.dtype)


def _gather_call(cnt, h, posm, cap, TT, srows, EP):
    S, E, T = posm.shape
    tps = T // TT
    R = S * cap
    gs = pltpu.PrefetchScalarGridSpec(
        num_scalar_prefetch=1,
        grid=(E // EP, S * tps),
        in_specs=[pl.BlockSpec((TT, D_MODEL), lambda e, t, c: (t, 0)),
                  pl.BlockSpec((1, EP, 1, TT), lambda e, t, c: (t // tps, e, 0, t % tps))],
        out_specs=pl.BlockSpec((EP, R, D_MODEL), lambda e, t, c: (e, 0, 0)),
        scratch_shapes=[pltpu.VMEM((EP, R + srows, D_MODEL), f32)])
    return pl.pallas_call(
        functools.partial(_gather_kernel, TT=TT, cap=cap, tps=tps, R=R, srows=srows, EP=EP),
        grid_spec=gs,
        out_shape=jax.ShapeDtypeStruct((E, R, D_MODEL), bf16),
        compiler_params=_cparams(("parallel", "arbitrary")),
        name="moe_gather",
    )(cnt.reshape(-1), h, posm.reshape(S, E, 1, T))


FAST_SUB_MAX = 48
FAST_TILE_MAX = 192
FAST_ROWS = 64
CMB_TILE = 1024


def _fits_fast(cnt, T):
    nsub = T // SUB
    per_sub = cnt[..., 1:nsub + 1] - cnt[..., :nsub]
    k = CMB_TILE // SUB
    per_tile = cnt[..., k:nsub + 1:k] - cnt[..., 0:nsub:k]
    return (jnp.max(per_sub) <= FAST_SUB_MAX) & (jnp.max(per_tile) <= FAST_TILE_MAX)


def gather_rows(cnt, h, posm, cap, TT, fast_ok=None):
    safe = lambda: _gather_call(cnt, h, posm, cap, TT, SUBW, 1)
    if fast_ok is None:
        return safe()
    return lax.cond(fast_ok, lambda: _gather_call(cnt, h, posm, cap, TT, FAST_ROWS, 2), safe)


EXPERT_TF = 256


def _ffn_kernel(*refs, n):
    xs = refs[:n]
    wg_ref, wu_ref, wd_ref = refs[n:n + 3]
    his = refs[n + 3:2 * n + 3]
    los = refs[2 * n + 3:3 * n + 3]
    accs = refs[3 * n + 3:]
    j = pl.program_id(1)
    wg = wg_ref[0, 0].astype(bf16)
    wu = wu_ref[0, 0].astype(bf16)
    wd = wd_ref[0, 0].astype(bf16)
    @pl.when(j == 0)
    def _():
        for acc in accs:
            acc[...] = jnp.zeros_like(acc)

    for x_ref, hi_ref, lo_ref, acc in zip(xs, his, los, accs):
        x = x_ref[0]
        a = jnp.dot(x, wg, preferred_element_type=f32)
        u = jnp.dot(x, wu, preferred_element_type=f32)
        acc[...] += jnp.dot((_silu(a) * u).astype(bf16), wd, preferred_element_type=f32)

        @pl.when(j == pl.num_programs(1) - 1)
        def _(acc=acc, hi_ref=hi_ref, lo_ref=lo_ref):
            hi, lo = _split(acc[...])
            hi_ref[0] = hi
            lo_ref[0] = lo


def expert_ffn(xgs, layer, w_gate, w_up, w_down):
    E = xgs[0].shape[0]
    nf = D_FF_EXPERT // EXPERT_TF
    n = len(xgs)
    rowspec = lambda a: pl.BlockSpec((1, a.shape[1], D_MODEL), lambda e, j: (e, 0, 0))
    res = pl.pallas_call(
        functools.partial(_ffn_kernel, n=n),
        grid=(E, nf),
        in_specs=[rowspec(a) for a in xgs] + [
            pl.BlockSpec((1, 1, D_MODEL, EXPERT_TF), lambda e, j: (layer, e, 0, j)),
            pl.BlockSpec((1, 1, D_MODEL, EXPERT_TF), lambda e, j: (layer, e, 0, j)),
            pl.BlockSpec((1, 1, EXPERT_TF, D_MODEL), lambda e, j: (layer, e, j, 0))],
        out_specs=[rowspec(a) for a in xgs] * 2,
        out_shape=[jax.ShapeDtypeStruct(a.shape, bf16) for a in xgs] * 2,
        scratch_shapes=[pltpu.VMEM(a.shape[1:], f32) for a in xgs],
        compiler_params=_cparams(("parallel", "arbitrary")),
        name="expert_ffn",
    )(*xgs, w_gate, w_up, w_down)
    return [(res[i], res[n + i]) for i in range(n)]


def _combine_kernel(cnt_ref, x_ref, pos_ref, am_ref, m_ref, *rest, TT, cap, R, W, ytot, crows, EP):
    y_refs, o_ref = rest[:2 * EP], rest[2 * EP]
    st = pl.program_id(0)
    tl = pl.program_id(1)
    eg = pl.program_id(2)

    @pl.when(eg == 0)
    def _():
        o_ref[...] = x_ref[...]

    nsub = TT // SUB
    lane = lax.broadcasted_iota(i32, (TT, N_EXPERTS), 1)
    m5 = m_ref[0]
    rid = lax.broadcasted_iota(i32, (crows, SUB), 0).astype(f32)
    am = am_ref[...]
    gcols, wss = [], []
    for ep in range(EP):
        e = eg * EP + ep
        cbase = (st * N_EXPERTS + e) * LANES + tl * nsub
        wss.append(jnp.minimum(e * R + st * cap + _align16(cnt_ref[cbase]), ytot - W))
        gcols.append(jnp.sum(jnp.where(lane == e, am, 0.0), axis=1, keepdims=True))
    for s in range(nsub):
        sl = slice(s * SUB, (s + 1) * SUB)
        tot = None
        for ep in range(EP):
            e = eg * EP + ep
            rowbase = e * R + st * cap
            off = _align16(cnt_ref[(st * N_EXPERTS + e) * LANES + tl * nsub + s])
            rel = pl.multiple_of(jnp.minimum(rowbase + off - wss[ep], W - crows), 16)
            first = (wss[ep] + rel - rowbase).astype(f32)
            pos = pos_ref[0, ep, :, sl]
            onehot = jnp.where(pos == rid + first, 1.0, 0.0).astype(bf16)
            ywin = jnp.concatenate([y_refs[2 * ep][pl.ds(rel, crows), :], y_refs[2 * ep + 1][pl.ds(rel, crows), :]],
                                   axis=0)
            picked = _tn(jnp.concatenate([onehot, onehot], axis=0), ywin)
            term = gcols[ep][sl] * picked
            tot = term if tot is None else tot + term
        o_ref[sl, :] += m5 * tot


def _combine_call(cnt, x2d, posm, aff_tm, mvec, y_hl, cap, TT, crows, W, EP):
    S, E, T = posm.shape
    tps = T // TT
    R = S * cap
    ytot = E * R
    nsub = TT // SUB
    G = mvec.shape[0]

    def yspec(ep):
        def ymap(st, tl, eg, c):
            e = eg * EP + ep
            off = _align16(c[(st * E + e) * LANES + tl * nsub])
            return (pl.multiple_of(jnp.minimum(e * R + st * cap + off, ytot - W), 16), 0)
        return pl.BlockSpec((pl.Element(W), pl.Element(D_MODEL)), ymap)

    tok = lambda w: pl.BlockSpec((TT, w), lambda st, tl, e, c: (st * tps + tl, 0))
    mmap = (lambda st, tl, e, c: (st, 0, 0)) if G > 1 else (lambda st, tl, e, c: (0, 0, 0))
    gs = pltpu.PrefetchScalarGridSpec(
        num_scalar_prefetch=1,
        grid=(S, tps, E // EP),
        in_specs=[tok(D_MODEL),
                  pl.BlockSpec((1, EP, 1, TT), lambda st, tl, e, c: (st, e, 0, tl)),
                  tok(N_EXPERTS),
                  pl.BlockSpec((1, 1, D_MODEL), mmap)] + [yspec(ep) for ep in range(EP) for _ in range(2)],
        out_specs=tok(D_MODEL))
    yh, yl = y_hl[0].reshape(ytot, D_MODEL), y_hl[1].reshape(ytot, D_MODEL)
    return pl.pallas_call(
        functools.partial(_combine_kernel, TT=TT, cap=cap, R=R, W=W, ytot=ytot, crows=crows, EP=EP),
        grid_spec=gs,
        out_shape=jax.ShapeDtypeStruct(x2d.shape, f32),
        compiler_params=_cparams(("parallel", "parallel", "arbitrary")),
        name="moe_combine",
    )(cnt.reshape(-1), x2d, posm.reshape(S, E, 1, T), aff_tm, mvec.reshape(G, 1, D_MODEL), *([yh, yl] * EP))


def combine(cnt, x2d, posm, aff_tm, mvec, y_hl, cap, TT, fast_ok=None):
    safe = lambda: _combine_call(cnt, x2d, posm, aff_tm, mvec, y_hl, cap, TT, CMB_ROWS, TT + 32, 1)
    if fast_ok is None:
        return safe()
    wfast = FAST_TILE_MAX + 16 + FAST_ROWS
    return lax.cond(fast_ok, lambda: _combine_call(cnt, x2d, posm, aff_tm, mvec, y_hl, cap, TT, FAST_ROWS, wfast, 8),
                    safe)


def _rope_tables(n):
    half = NA_HD // 2
    q = half // 2
    inv = ROPE_THETA ** (-np.arange(q, dtype=np.float64) / q)
    pos = np.arange(n)
    ang_r = (pos // GRID_W)[:, None] * inv
    ang_c = (pos % GRID_W)[:, None] * inv
    zero = np.zeros_like(ang_r)
    c = np.concatenate([np.cos(ang_r)] * 2 + [np.cos(ang_c)] * 2, axis=1)
    s1 = np.concatenate([-np.sin(ang_r), zero, -np.sin(ang_c), zero], axis=1)
    s2 = np.concatenate([zero, np.sin(ang_r), zero, np.sin(ang_c)], axis=1)
    two = lambda a: jnp.asarray(np.concatenate([a, a], axis=1).astype(np.float32))
    return two(c), two(s1), two(s2)


def _mixing(hx, hc, need_ctx, B, N, NC, la, lc, w_in, q_gain, k_gain, table, rope, bd, conv_w, conv_b,
            spec, skip, filt_c, wa, wb, wc, wo, x2d, c2d, mx2, mc2):
    tile8 = lambda v: jnp.tile(v.reshape(1, NA_HD), (1, NA_HEADS))
    qg, kg = tile8(q_gain), tile8(k_gain)
    norm_aux = [("col", kg), ("const", bd)]
    rope_aux = [("row", rope[0]), ("row", rope[1]), ("row", rope[2])]
    lf_aux = lambda d: [("col", la[d:d + 1]), ("col", lc[d:d + 1])]
    tc = hc.shape[0]

    lff_c = project(hc, w_in, OFF_FF, HG_W, _epi_logforget, lf_aux(0), (f32,), tm=tc).reshape(B, NC, -1)
    lfb_c = project(hc, w_in, OFF_FB, HG_W, _epi_logforget, lf_aux(1), (f32,), tm=tc).reshape(B, NC, -1)
    i_c = project(hc, w_in, OFF_I, HG_W, _epi_raw, tm=tc).reshape(B, NC, -1)
    k_c = project(hc, w_in, OFF_NK, NA_W, _epi_norm, norm_aux, tm=tc).reshape(B, NC, -1)
    v_c = project(hc, w_in, OFF_NV, NA_W, _epi_raw, tm=tc).reshape(B, NC, -1)
    if need_ctx:
        q_c = project(hc, w_in, OFF_HQ, HG_W, _epi_silu, tm=tc).reshape(B, NC, -1)
    else:
        q_c = jnp.zeros((B, NC, HG_W), bf16)
    s0 = jnp.zeros((B, HG_HEADS, LANES, LANES), f32)
    oc_f, oc_b, s_f, s_b = hgrn_bidir(lff_c, lfb_c, i_c, q_c, s0, s0)

    lff_x = project(hx, w_in, OFF_FF, HG_W, _epi_logforget, lf_aux(0), (f32,)).reshape(B, N, -1)
    lfb_x = project(hx, w_in, OFF_FB, HG_W, _epi_logforget, lf_aux(1), (f32,)).reshape(B, N, -1)
    i_x = project(hx, w_in, OFF_I, HG_W, _epi_raw).reshape(B, N, -1)
    q_x = project(hx, w_in, OFF_HQ, HG_W, _epi_silu).reshape(B, N, -1)
    g_x = project(hx, w_in, OFF_HG, HG_W, _epi_silu).reshape(B, N, -1)
    k_x = project(hx, w_in, OFF_NK, NA_W, _epi_norm_rope, norm_aux + rope_aux, rows_per_seq=N).reshape(B, N, -1)
    v_x = project(hx, w_in, OFF_NV, NA_W, _epi_raw).reshape(B, N, -1)
    qn_x, qr_x = project(hx, w_in, OFF_NQ, NA_W, _epi_norm_both, [("col", qg), ("const", bd)] + rope_aux,
                         (bf16, bf16), rows_per_seq=N)
    p_x = project(hx, w_in, OFF_HY, 3 * HY_W, _epi_raw, out_dtypes=(f32,)).reshape(B, N, 3 * HY_W)
    gates_x = project(hx, w_in, OFF_GATE, 3 * D_MODEL, _epi_sigmoid)

    ox_f, ox_b, _, _ = hgrn_bidir(lff_x, lfb_x, i_x, q_x, s_f, s_b)

    nb_x = neighbourhood_attention(qr_x.reshape(B, N, -1), qn_x.reshape(B, N, -1), k_x, v_x, k_c, v_c, table)

    z_x, x0_x = hyena_pre(p_x, conv_w, conv_b)
    hy_x = hyena_conv(z_x, x0_x, spec[0], spec[1], skip, spec[2])

    flat = lambda a: a.reshape(-1, a.shape[-1])
    x_new = merge(flat(ox_f), flat(ox_b), flat(g_x), flat(nb_x), flat(hy_x), gates_x, wa, wb, wc, wo, x2d, mx2, N)
    if not need_ctx:
        return x_new, None

    qn_c = project(hc, w_in, OFF_NQ, NA_W, _epi_norm, [("col", qg), ("const", bd)], tm=tc).reshape(B, NC, -1)
    nb_c = context_attention(qn_c, k_c, v_c)
    p_c = project(hc, w_in, OFF_HY, 3 * HY_W, _epi_raw, out_dtypes=(f32,), tm=tc).reshape(B, NC, 3 * HY_W)
    hy_c = hyena_ctx(p_c, conv_w, conv_b, filt_c[0], filt_c[1], skip)
    gates_c = project(hc, w_in, OFF_GATE, 3 * D_MODEL, _epi_sigmoid, tm=tc)
    g_c = project(hc, w_in, OFF_HG, HG_W, _epi_silu, tm=tc)
    c_new = merge(flat(oc_f), flat(oc_b), g_c, flat(nb_c), flat(hy_c), gates_c, wa, wb, wc, wo, c2d, mc2, B * NC)
    return x_new, c_new


def kernel(x, c, ctx, c_ctx, w_mod, b_mod, norm_mix, norm_ffn, w_in, hg_lb, na_q_gain, na_k_gain, na_rpb,
           hy_conv_w, hy_conv_b, hy_pe_w1, hy_pe_b1, hy_pe_freq1, hy_pe_w2, hy_pe_b2, hy_pe_freq2, hy_pe_w3,
           hy_skip, w_branch_a, w_branch_b, w_branch_c, w_out, w_router, w_e_gate, w_e_up, w_e_down):
    B, N, D = x.shape
    NC = ctx.shape[1]
    E = N_EXPERTS
    cap_x = EC_CAP_FACTOR * N // E
    cap_c = EC_CAP_FACTOR * NC // E

    lb = jnp.cumsum(jax.nn.softmax(hg_lb.astype(f32), axis=0), axis=0)
    lb = lb - lb[:1]
    la_all, lc_all = jnp.log(lb), jnp.log1p(-lb)

    s8 = jnp.zeros((8, D), f32).at[:B].set(c).at[B].set(c_ctx)
    rope = _rope_tables(N)
    bd = jnp.asarray(np.kron(np.eye(NA_HEADS), np.full((NA_HD, NA_HD), 1.0 / NA_HD)).astype(np.float32), bf16)
    dcx = _dft_consts(N)

    x2d = x.reshape(B * N, D)
    c2d = ctx.reshape(B * NC, D)
    for l in range(DEPTH):
        need_ctx = l < DEPTH - 1
        mv = modvec(s8, w_mod[l], b_mod[l])
        mx = [mv[:B, k * D:(k + 1) * D] for k in range(6)]
        mc = [mv[B:B + 1, k * D:(k + 1) * D] for k in range(6)]
        w_in_l = w_in[l].astype(bf16)
        hx = modulate(x2d, norm_mix[l], mx[0], mx[1], N, bf16)
        hc = modulate(c2d, norm_mix[l], mc[0], mc[1], B * NC, bf16)

        filt = (hy_pe_w1[l], hy_pe_b1[l], hy_pe_freq1[l], hy_pe_w2[l], hy_pe_b2[l], hy_pe_freq2[l], hy_pe_w3[l])
        k_x, nrm_x = hyena_filter(N, *filt)
        sre, sim = hyena_spectrum(k_x, 1.0 / nrm_x, dcx)
        skip = hy_skip[l].reshape(1, HY_W)
        filt_c = None
        if need_ctx:
            h_c, nrm_c = hyena_filter(NC, *filt)
            filt_c = (h_c, 1.0 / nrm_c)

        x2d, c_new = _mixing(
            hx, hc, need_ctx, B, N, NC, la_all[l], lc_all[l], w_in_l, na_q_gain[l], na_k_gain[l],
            _na_bias_table(na_rpb[l]), rope, bd, hy_conv_w[l], hy_conv_b[l], (sre, sim, dcx), skip,
            filt_c, w_branch_a[l].astype(bf16), w_branch_b[l].astype(bf16), w_branch_c[l].astype(bf16),
            w_out[l].astype(bf16), x2d, c2d, mx[2], mc[2])

        h2, aff_t, aff_m = router(x2d, norm_ffn[l], mx[3], mx[4], w_router[l], N)
        posm, cnt = select_topk(aff_t, cap_x)
        fast_ok = _fits_fast(cnt, N)
        xgs = [gather_rows(cnt, h2, posm, cap_x, 2048, fast_ok)]
        if need_ctx:
            c2d = c_new
            hc2, aff_tc, aff_mc = router(c2d, norm_ffn[l], mc[3], mc[4], w_router[l], NC)
            posm_c, cnt_c = select_topk(aff_tc, cap_c)
            xgs.append(gather_rows(cnt_c, hc2, posm_c, cap_c, NC))
        ys = expert_ffn(xgs, l, w_e_gate, w_e_up, w_e_down)
        x2d = combine(cnt, x2d, posm, aff_m, mx[5], ys[0], cap_x, CMB_TILE, fast_ok)
        if need_ctx:
            c2d = combine(cnt_c, c2d, posm_c, aff_mc, mc[5], ys[1], cap_c, NC)
    return x2d.reshape(B, N, D)
```

```python
import functools
import math

import numpy as np
import jax
import jax.numpy as jnp
from jax import lax
from jax.experimental import pallas as pl
from jax.experimental.pallas import tpu as pltpu

f32 = jnp.float32
bf16 = jnp.bfloat16
i32 = jnp.int32
HI = lax.Precision.HIGHEST

D_MODEL = 1024
DEPTH = 2
GRID_W = 64
EPS = 1e-6
HG_HEADS = 4
HG_W = 512
HG_CHUNK = 64
NA_HEADS = 8
NA_HD = 64
NA_W = 512
NA_WIN_R = 8
NA_WIN_C = 16
ROPE_THETA = 10000.0
HY_W = 512
HY_BANDS = 16
HY_PE_DIM = 1 + 2 * HY_BANDS
HY_FILT_HID = 64
HY_FAST_DECAY = 0.3
HY_SLOW_DECAY = 1.5
HY_TARGET = 1e-2
OFF_FF = 0
OFF_FB = 512
OFF_I = 1024
OFF_NK = 1536
OFF_NV = 2048
OFF_HQ = 2560
OFF_NQ = 3072
OFF_HG = 3584
OFF_HY = 4096
OFF_GATE = 5632
IN_COLS = 8704
N_EXPERTS = 16
EC_CAP_FACTOR = 2
D_FF_EXPERT = 2816

LANES = 128
NEG_BIG = -1e30
VMEM_LIMIT = 56 * 1024 * 1024


def _cparams(sem, vmem=VMEM_LIMIT):
    return pltpu.CompilerParams(dimension_semantics=sem, vmem_limit_bytes=vmem)


def _nt(a, b, precision=None):
    return lax.dot_general(a, b, (((1,), (1,)), ((), ())), precision=precision, preferred_element_type=f32)


def _tn(a, b, precision=None):
    return lax.dot_general(a, b, (((0,), (0,)), ((), ())), precision=precision, preferred_element_type=f32)


def _silu(x):
    return x * jax.nn.sigmoid(x)


def _split(x):
    hi = x.astype(bf16)
    return hi, (x - hi.astype(f32)).astype(bf16)


def _modvec_kernel(s_ref, w_ref, b_ref, o_ref):
    s = _silu(s_ref[...])
    o_ref[...] = jnp.dot(s, w_ref[...], precision=HI, preferred_element_type=f32) + b_ref[...]


def modvec(s8, w, b):
    n = w.shape[1]
    tn = 1024
    return pl.pallas_call(
        _modvec_kernel,
        grid=(n // tn,),
        in_specs=[pl.BlockSpec((8, D_MODEL), lambda j: (0, 0)),
                  pl.BlockSpec((D_MODEL, tn), lambda j: (0, j)),
                  pl.BlockSpec((1, tn), lambda j: (0, j))],
        out_specs=pl.BlockSpec((8, tn), lambda j: (0, j)),
        out_shape=jax.ShapeDtypeStruct((8, n), f32),
        compiler_params=_cparams(("parallel",)),
        name="modvec",
    )(s8, w, b.reshape(1, n))


def _modulate_kernel(x_ref, g_ref, sh_ref, sc_ref, o_ref):
    x = x_ref[...]
    ms = jnp.mean(x * x, axis=-1, keepdims=True)
    y = x * lax.rsqrt(ms + EPS)
    o_ref[...] = (y * g_ref[...] * (1.0 + sc_ref[0]) + sh_ref[0]).astype(o_ref.dtype)


def modulate(x2d, g, shift, scale, rows_per_group, out_dtype):
    R = x2d.shape[0]
    tm = 512
    tpg = rows_per_group // tm
    G = shift.shape[0]
    return pl.pallas_call(
        _modulate_kernel,
        grid=(R // tm,),
        in_specs=[pl.BlockSpec((tm, D_MODEL), lambda i: (i, 0)),
                  pl.BlockSpec((1, D_MODEL), lambda i: (0, 0)),
                  pl.BlockSpec((1, 1, D_MODEL), lambda i: (i // tpg, 0, 0)),
                  pl.BlockSpec((1, 1, D_MODEL), lambda i: (i // tpg, 0, 0))],
        out_specs=pl.BlockSpec((tm, D_MODEL), lambda i: (i, 0)),
        out_shape=jax.ShapeDtypeStruct((R, D_MODEL), out_dtype),
        compiler_params=_cparams(("parallel",)),
        name="modulate",
    )(x2d, g.reshape(1, D_MODEL), shift.reshape(G, 1, D_MODEL), scale.reshape(G, 1, D_MODEL))


def _log1p_exp_neg(a):
    return jnp.log(1.0 + jnp.exp(-a))


def _log_sigmoid(z):
    return jnp.minimum(z, 0.0) - _log1p_exp_neg(jnp.abs(z))


def _epi_raw(acc, o_ref):
    o_ref[...] = acc.astype(o_ref.dtype)


def _epi_silu(acc, o_ref):
    o_ref[...] = _silu(acc).astype(o_ref.dtype)


def _epi_sigmoid(acc, o_ref):
    o_ref[...] = jax.nn.sigmoid(acc).astype(o_ref.dtype)


def _epi_logforget(acc, la_ref, lc_ref, o_ref):
    la = la_ref[...]
    c = lc_ref[...] + _log_sigmoid(acc)
    o_ref[...] = jnp.maximum(la, c) + _log1p_exp_neg(jnp.abs(la - c))


def _head_rms(acc, gain_ref, bd_ref):
    hi, lo = _split(acc * acc)
    ms = jnp.dot(hi, bd_ref[...], preferred_element_type=f32) + jnp.dot(lo, bd_ref[...], preferred_element_type=f32)
    return acc * lax.rsqrt(ms + EPS) * gain_ref[...]


def _rope(y, c_ref, s1_ref, s2_ref):
    reps = y.shape[1] // LANES
    c = jnp.concatenate([c_ref[...]] * reps, axis=1)
    s1 = jnp.concatenate([s1_ref[...]] * reps, axis=1)
    s2 = jnp.concatenate([s2_ref[...]] * reps, axis=1)
    w = y.shape[1]
    return y * c + pltpu.roll(y, w - 16, axis=1) * s1 + pltpu.roll(y, 16, axis=1) * s2


def _epi_norm(acc, gain_ref, bd_ref, o_ref):
    o_ref[...] = _head_rms(acc, gain_ref, bd_ref).astype(o_ref.dtype)


def _epi_norm_rope(acc, gain_ref, bd_ref, c_ref, s1_ref, s2_ref, o_ref):
    y = _head_rms(acc, gain_ref, bd_ref)
    o_ref[...] = _rope(y, c_ref, s1_ref, s2_ref).astype(o_ref.dtype)


def _epi_norm_both(acc, gain_ref, bd_ref, c_ref, s1_ref, s2_ref, on_ref, or_ref):
    y = _head_rms(acc, gain_ref, bd_ref)
    on_ref[...] = y.astype(on_ref.dtype)
    or_ref[...] = _rope(y, c_ref, s1_ref, s2_ref).astype(or_ref.dtype)


def _proj_kernel(h_ref, w_ref, *rest, epi):
    acc = jnp.dot(h_ref[...], w_ref[...], preferred_element_type=f32)
    epi(acc, *rest)


def project(h, w, c0, width, epi, aux=(), out_dtypes=(bf16,), tm=2048, rows_per_seq=None):
    R = h.shape[0]
    tn = 512
    nj = width // tn
    cb = c0 // tn
    in_specs = [pl.BlockSpec((tm, D_MODEL), lambda i, j: (i, 0)),
                pl.BlockSpec((D_MODEL, tn), lambda i, j: (0, cb + j))]
    args = [h, w]
    for kind, arr in aux:
        if kind == "col":
            in_specs.append(pl.BlockSpec((1, tn), lambda i, j: (0, j)))
        elif kind == "const":
            in_specs.append(pl.BlockSpec(arr.shape, lambda i, j: (0, 0)))
        else:
            tps = rows_per_seq // tm
            in_specs.append(pl.BlockSpec((tm, LANES), lambda i, j: (i % tps, 0)))
        args.append(arr)
    out_specs = [pl.BlockSpec((tm, tn), lambda i, j: (i, j)) for _ in out_dtypes]
    out_shape = [jax.ShapeDtypeStruct((R, width), dt) for dt in out_dtypes]
    res = pl.pallas_call(
        functools.partial(_proj_kernel, epi=epi),
        grid=(R // tm, nj),
        in_specs=in_specs,
        out_specs=out_specs,
        out_shape=out_shape,
        compiler_params=_cparams(("parallel", "parallel")),
        name="proj_" + epi.__name__[5:],
    )(*args)
    return res[0] if len(res) == 1 else res


HG_MM_LEVELS = 3


def _hgrn_tmatrix(C, reverse):
    t = np.arange(C)
    tau = (C - 1 - t) if reverse else t
    tt, uu = tau[:, None], tau[None, :]
    T = np.zeros((1 + HG_MM_LEVELS, C, C), np.float32)
    T[0] = uu <= tt
    for l in range(HG_MM_LEVELS):
        same = (tt >> (l + 1)) == (uu >> (l + 1))
        tr = ((tt >> l) & 1) == 1
        ur = ((uu >> l) & 1) == 1
        T[1 + l] = same & ((tr & ur & (uu <= tt)) | (~tr & ~ur & (uu > tt)))
    return T.reshape((1 + HG_MM_LEVELS) * C, C)


def _hgrn_level_sums(b, l, C, reverse):
    h = 1 << l
    parts = []
    for r0 in range(0, C, 2 * h):
        tb = r0 + h if reverse else r0 + h - 1
        d = b[r0:r0 + 2 * h] - b[tb:tb + 1]
        first, second = d[:h], d[h:]
        parts += [first, -second] if reverse else [-first, second]
    return jnp.concatenate(parts, axis=0)


def _hgrn_step(chains, C):
    L = int(round(math.log2(C)))
    ti0 = lax.broadcasted_iota(i32, (C, C), 0)
    si0 = lax.broadcasted_iota(i32, (C, C), 1)

    es, kcs = [], []
    for lf, _, _, tm, _, reverse in chains:
        W = lf.shape[1]
        hi, lo = _split(lf)
        r = jnp.dot(tm, jnp.concatenate([hi, lo], axis=1), preferred_element_type=f32)
        r = r[:, :W] + r[:, W:]
        b = r[0:C]
        end_row = 0 if reverse else C - 1
        levels = [r[(1 + l) * C:(2 + l) * C] for l in range(HG_MM_LEVELS)]
        levels += [_hgrn_level_sums(b, l, C, reverse) for l in range(HG_MM_LEVELS, L)]
        es.append((b, b[end_row:end_row + 1] - b, levels))
        kcs.append(1.0 - jnp.exp(lf))

    items = []
    for (lf, v_all, q_all, _, s_view, reverse), (b_all, suf_all, levels), kc_all in zip(chains, es, kcs):
        ti, si = (C - 1 - ti0, C - 1 - si0) if reverse else (ti0, si0)
        end_row = 0 if reverse else C - 1
        lmasks = [((ti >> (l + 1)) == (si >> (l + 1))) & (((ti >> l) & 1) == 1) & (((si >> l) & 1) == 0)
                  for l in range(L)]
        for h in range(HG_HEADS):
            sl = slice(LANES * h, LANES * (h + 1))
            qh, kch, vh = q_all[:, sl], kc_all[:, sl], v_all[:, sl]
            b = b_all[:, sl]
            st = s_view[h]
            pairs = [(qh.astype(bf16), kch.astype(bf16), ti == si)]
            for l in range(L):
                x = jnp.exp(levels[l][:, sl])
                pairs.append(((qh * x).astype(bf16), (kch * x).astype(bf16), lmasks[l]))
            items.append(dict(qb=(qh * jnp.exp(b)).astype(bf16), st=st, pairs=pairs, vh=vh,
                              kd=(kch * jnp.exp(suf_all[:, sl])).astype(bf16),
                              decay=jnp.exp(b[end_row:end_row + 1, :]), view=s_view, h=h))

    for it in items:
        it["o"] = _nt(it["qb"], it["st"].astype(bf16))
        it["att"] = [(_nt(ql, kl), m) for ql, kl, m in it["pairs"]]

    outs = []
    for it in items:
        att = None
        for a, m in it["att"]:
            t = jnp.where(m, a, 0.0)
            att = t if att is None else att + t
        it["o"] = it["o"] + jnp.dot(att.astype(bf16), it["vh"], preferred_element_type=f32)
        it["view"][it["h"]] = it["st"] * it["decay"] + _tn(it["vh"], it["kd"])
    nh = HG_HEADS
    for c in range(len(chains)):
        outs.append(jnp.concatenate([items[c * nh + h]["o"] for h in range(nh)], axis=1))
    return outs


def _hgrn_kernel(lff_ref, lfb_ref, vf_ref, vb_ref, qf_ref, qb_ref, s0f_ref, s0b_ref, tf_ref, tb_ref,
                 of_ref, ob_ref, sff_ref, sfb_ref, s_scr, *, C, B, CPS):
    c = pl.program_id(0)

    @pl.when(c == 0)
    def _():
        s_scr[0] = s0f_ref[...]
        s_scr[1] = s0b_ref[...]

    for k in range(CPS):
        rf = slice(k * C, (k + 1) * C)
        rb = slice((CPS - 1 - k) * C, (CPS - k) * C)
        chains = []
        for b in range(B):
            chains.append((lff_ref[b, rf, :], vf_ref[b, rf, :], qf_ref[b, rf, :].astype(f32), tf_ref[...],
                           s_scr.at[0, b], False))
            chains.append((lfb_ref[b, rb, :], vb_ref[b, rb, :], qb_ref[b, rb, :].astype(f32), tb_ref[...],
                           s_scr.at[1, b], True))
        outs = _hgrn_step(chains, C)
        for b in range(B):
            of_ref[b, rf, :] = outs[2 * b]
            ob_ref[b, rb, :] = outs[2 * b + 1]

    @pl.when(c == pl.num_programs(0) - 1)
    def _():
        sff_ref[...] = s_scr[0]
        sfb_ref[...] = s_scr[1]


def hgrn_bidir(lf_f, lf_b, v, q, s0_f, s0_b):
    B, N, W = lf_f.shape
    C = HG_CHUNK
    cps = 2 if (N // C) % 2 == 0 else 1
    nch = N // (C * cps)
    tf = jnp.asarray(_hgrn_tmatrix(C, False), bf16)
    tb = jnp.asarray(_hgrn_tmatrix(C, True), bf16)
    fw = pl.BlockSpec((B, C * cps, W), lambda c: (0, c, 0))
    bw = pl.BlockSpec((B, C * cps, W), lambda c: (0, nch - 1 - c, 0))
    st = pl.BlockSpec((B, HG_HEADS, LANES, LANES), lambda c: (0, 0, 0, 0))
    tsp = pl.BlockSpec(tf.shape, lambda c: (0, 0))
    seq = jax.ShapeDtypeStruct((B, N, W), f32)
    sts = jax.ShapeDtypeStruct((B, HG_HEADS, LANES, LANES), f32)
    return pl.pallas_call(
        functools.partial(_hgrn_kernel, C=C, B=B, CPS=cps),
        grid=(nch,),
        in_specs=[fw, bw, fw, bw, fw, bw, st, st, tsp, tsp],
        out_specs=[fw, bw, st, st],
        out_shape=[seq, seq, sts, sts],
        scratch_shapes=[pltpu.VMEM((2, B, HG_HEADS, LANES, LANES), f32)],
        compiler_params=_cparams(("arbitrary",)),
        name="hgrn",
    )(lf_f, lf_b, v, v, q, q, s0_f, s0_b, tf, tb)


LOG2E = math.log2(math.e)


def _na_kernel(qr_ref, qn_ref, k_ref, v_ref, kc_ref, vc_ref, tab_ref, o_ref, s_scr, p_scr, inv_scr, *,
               rows_per_step, n_rows):
    g = pl.program_id(2)
    scale = NA_HD ** -0.5 * LOG2E
    lane = lax.broadcasted_iota(i32, (GRID_W, LANES), 1)
    kcx = kc_ref[0]
    vcx = vc_ref[0]
    win = NA_WIN_R * GRID_W
    ctx_len = kcx.shape[0]

    starts = []
    for i in range(rows_per_step):
        r = g * rows_per_step + i
        rs = jnp.clip(r - NA_WIN_R // 2, 0, n_rows - NA_WIN_R)
        off = rs - r + (NA_WIN_R - 1)
        start = pl.multiple_of(rs * GRID_W, GRID_W)
        starts.append(start)
        kw = k_ref[0, pl.ds(start, win), :]
        qr = qr_ref[0, i * GRID_W:(i + 1) * GRID_W, :]
        qn = qn_ref[0, i * GRID_W:(i + 1) * GRID_W, :]
        zq = jnp.zeros_like(qr)
        qrs = jnp.concatenate([jnp.where(lane < NA_HD, qr, zq), jnp.where(lane < NA_HD, zq, qr)], axis=0)
        qns = jnp.concatenate([jnp.where(lane < NA_HD, qn, zq), jnp.where(lane < NA_HD, zq, qn)], axis=0)
        bias = jnp.concatenate([tab_ref[0, off], tab_ref[1, off]], axis=0)
        row0 = 2 * i * GRID_W
        s_scr[row0:row0 + 2 * GRID_W, 0:win] = _nt(qrs, kw) * scale + bias
        s_scr[row0:row0 + 2 * GRID_W, win:win + ctx_len] = _nt(qns, kcx) * scale

    def softmax_rows(c, carry):
        r0 = pl.multiple_of(c * LANES, LANES)
        s = s_scr[pl.ds(r0, LANES), :]
        p = jnp.exp2(s - jnp.max(s, axis=-1, keepdims=True))
        inv_scr[pl.ds(r0, LANES), :] = 1.0 / jnp.sum(p, axis=-1, keepdims=True)
        p_scr[pl.ds(r0, LANES), :] = p.astype(bf16)
        return carry

    lax.fori_loop(0, 2 * rows_per_step * GRID_W // LANES, softmax_rows, 0, unroll=2)

    for i in range(rows_per_step):
        vw = v_ref[0, pl.ds(starts[i], win), :]
        row0 = 2 * i * GRID_W
        p = p_scr[row0:row0 + 2 * GRID_W, :]
        res = (jnp.dot(p[:, :win], vw, preferred_element_type=f32)
               + jnp.dot(p[:, win:], vcx, preferred_element_type=f32))
        res = res * inv_scr[row0:row0 + 2 * GRID_W, :]
        o_ref[0, i * GRID_W:(i + 1) * GRID_W, :] = jnp.where(lane < NA_HD, res[:GRID_W], res[GRID_W:]).astype(o_ref.dtype)


def _na_bias_table(rpb):
    col = jnp.arange(GRID_W)
    cs = jnp.clip(col - NA_WIN_C // 2, 0, GRID_W - NA_WIN_C)
    kc = jnp.arange(GRID_W)
    valid = (kc[None, :] >= cs[:, None]) & (kc[None, :] < cs[:, None] + NA_WIN_C)
    dc = jnp.clip(kc[None, :] - col[:, None] + (NA_WIN_C - 1), 0, 2 * NA_WIN_C - 2)
    bc = jnp.where(valid[None, None], rpb[:, :, dc], NEG_BIG)
    t2 = jnp.stack([bc[:, o:o + NA_WIN_R] for o in range(NA_WIN_R)], axis=1)
    t2 = t2.transpose(0, 1, 3, 2, 4)
    return t2.reshape(NA_HEADS, NA_WIN_R, GRID_W, NA_WIN_R * GRID_W).astype(f32) * LOG2E


def neighbourhood_attention(q_rot, qn, k_rot, v, kc, vc, table):
    B, N, W = q_rot.shape
    n_rows = N // GRID_W
    rps = 16 if n_rows % 16 == 0 else 8
    ctx_len = kc.shape[1]
    pairs = W // LANES
    keys = NA_WIN_R * GRID_W + ctx_len
    return pl.pallas_call(
        functools.partial(_na_kernel, rows_per_step=rps, n_rows=n_rows),
        grid=(B, pairs, n_rows // rps),
        in_specs=[pl.BlockSpec((1, rps * GRID_W, LANES), lambda b, p, g: (b, g, p)),
                  pl.BlockSpec((1, rps * GRID_W, LANES), lambda b, p, g: (b, g, p)),
                  pl.BlockSpec((1, N, LANES), lambda b, p, g: (b, 0, p)),
                  pl.BlockSpec((1, N, LANES), lambda b, p, g: (b, 0, p)),
                  pl.BlockSpec((1, ctx_len, LANES), lambda b, p, g: (b, 0, p)),
                  pl.BlockSpec((1, ctx_len, LANES), lambda b, p, g: (b, 0, p)),
                  pl.BlockSpec((2, NA_WIN_R, GRID_W, NA_WIN_R * GRID_W), lambda b, p, g: (p, 0, 0, 0))],
        out_specs=pl.BlockSpec((1, rps * GRID_W, LANES), lambda b, p, g: (b, g, p)),
        out_shape=jax.ShapeDtypeStruct((B, N, W), bf16),
        scratch_shapes=[pltpu.VMEM((2 * rps * GRID_W, keys), f32), pltpu.VMEM((2 * rps * GRID_W, keys), bf16),
                        pltpu.VMEM((2 * rps * GRID_W, 1), f32)],
        compiler_params=_cparams(("parallel", "parallel", "arbitrary")),
        name="natten",
    )(q_rot, qn, k_rot, v, kc, vc, table)


def _ctx_attn_kernel(q_ref, k_ref, v_ref, o_ref):
    scale = NA_HD ** -0.5
    q = q_ref[0]
    k = k_ref[0]
    v = v_ref[0]
    lane = lax.broadcasted_iota(i32, q.shape, 1)
    res = []
    for hh in range(2):
        m = (lane >= NA_HD * hh) & (lane < NA_HD * (hh + 1))
        s = _nt(jnp.where(m, q, jnp.zeros_like(q)), k) * scale
        p = jnp.exp(s - jnp.max(s, axis=-1, keepdims=True))
        p = p / jnp.sum(p, axis=-1, keepdims=True)
        res.append(jnp.dot(p.astype(bf16), v, preferred_element_type=f32))
    o_ref[0] = jnp.where(lane < NA_HD, res[0], res[1]).astype(o_ref.dtype)


def context_attention(q, k, v):
    B, N, W = q.shape
    spec = pl.BlockSpec((1, N, LANES), lambda b, p: (b, 0, p))
    return pl.pallas_call(
        _ctx_attn_kernel,
        grid=(B, W // LANES),
        in_specs=[spec, spec, spec],
        out_specs=spec,
        out_shape=jax.ShapeDtypeStruct((B, N, W), bf16),
        compiler_params=_cparams(("parallel", "parallel")),
        name="ctx_attn",
    )(q, k, v)


def _filter_kernel(z_ref, w1_ref, b1_ref, f1_ref, w2_ref, b2_ref, f2_ref, w3_ref, dl_ref, k_ref, nrm_ref, *, tm, n):
    i = pl.program_id(0)
    z = z_ref[...]
    dot = lambda a, b: jnp.dot(a, b, precision=HI, preferred_element_type=f32)
    w1 = w1_ref[...]
    a = jnp.sin(f1_ref[...] * (dot(z[:tm // 2], w1[:, :LANES]) + dot(z[tm // 2:], w1[:, LANES:]) + b1_ref[...]))
    a = jnp.sin(f2_ref[...] * (dot(a, w2_ref[...]) + b2_ref[...]))
    w3 = w3_ref[...]
    zero = jnp.zeros_like(w3)
    h = jnp.concatenate([dot(a, jnp.concatenate([w3, zero], axis=0)), dot(a, jnp.concatenate([zero, w3], axis=0))],
                        axis=0)
    h = h * jnp.exp(-z[:, 0:1] * dl_ref[...])
    row = lax.broadcasted_iota(i32, (tm, HY_W), 0) + i * tm
    k = jnp.where(row == n, 0.0, h)
    k_ref[...] = k
    part = jnp.sum(jnp.abs(k), axis=0, keepdims=True)

    @pl.when(i == 0)
    def _():
        nrm_ref[...] = part

    @pl.when(i > 0)
    def _():
        nrm_ref[...] = nrm_ref[...] + part


def hyena_filter(n, w1, b1, fr1, w2, b2, fr2, w3):
    t = np.linspace(0.0, 1.0, n)[:, None]
    w = 2 * math.pi * np.arange(n)[:, None] / n
    fb = np.linspace(1e-4, HY_BANDS - 1, HY_BANDS)[None]
    z = np.concatenate([t, np.cos(fb * w), -np.sin(fb * w)], axis=-1)
    z = np.concatenate([z, np.zeros((1, HY_PE_DIM)), z[:0:-1]], axis=0)
    z = jnp.asarray(np.pad(z, ((0, 0), (0, LANES - HY_PE_DIM))).astype(np.float32))
    w1p = jnp.pad(w1.astype(f32), ((0, LANES - HY_PE_DIM), (0, 0)))
    deltas = jnp.asarray(np.abs(np.linspace(math.log(HY_TARGET) / HY_SLOW_DECAY, math.log(HY_TARGET) / HY_FAST_DECAY,
                                            2 * HY_W))[None].astype(np.float32))
    tm = min(n, 512)
    hid = HY_FILT_HID
    full = lambda shape: pl.BlockSpec(shape, lambda i: (0, 0))
    tph = n // tm
    zpad = jnp.zeros((LANES, hid), f32)
    w1pk = jnp.concatenate([w1p, zpad, zpad, w1p], axis=1)
    w2f = w2.astype(f32)
    z2 = jnp.zeros((hid, hid), f32)
    w2pk = jnp.concatenate([jnp.concatenate([w2f, z2], axis=1), jnp.concatenate([z2, w2f], axis=1)], axis=0)
    twice = lambda v: jnp.tile(v.reshape(1, hid).astype(f32), (1, 2))
    return pl.pallas_call(
        functools.partial(_filter_kernel, tm=tm, n=n),
        grid=(2 * n // tm,),
        in_specs=[pl.BlockSpec((tm, LANES), lambda i: (i, 0)),
                  full((LANES, 2 * LANES)), full((1, LANES)), full((1, LANES)),
                  full((LANES, LANES)), full((1, LANES)), full((1, LANES)),
                  pl.BlockSpec((hid, HY_W), lambda i: (0, i // tph)),
                  pl.BlockSpec((1, HY_W), lambda i: (0, i // tph))],
        out_specs=[pl.BlockSpec((tm, HY_W), lambda i: (i, 0)), pl.BlockSpec((1, HY_W), lambda i: (0, 0))],
        out_shape=[jax.ShapeDtypeStruct((2 * n, HY_W), f32), jax.ShapeDtypeStruct((1, HY_W), f32)],
        compiler_params=_cparams(("arbitrary",)),
        name="hyena_filter",
    )(z, w1pk, twice(b1), twice(fr1), w2pk, twice(b2), twice(fr2), w3.astype(f32), deltas)


def _conv3(u, w_ref, b_ref):
    n = u.shape[0]
    row = lax.broadcasted_iota(i32, u.shape, 0)
    prev = jnp.where(row == 0, 0.0, pltpu.roll(u, 1, axis=0))
    nxt = jnp.where(row == n - 1, 0.0, pltpu.roll(u, n - 1, axis=0))
    return prev * w_ref[0:1, :] + u * w_ref[1:2, :] + nxt * w_ref[2:3, :] + b_ref[...]


def _hyena_pre_kernel(p0_ref, p1_ref, p2_ref, w0_ref, w1_ref, w2_ref, b0_ref, b1_ref, b2_ref, z_ref, x0_ref):
    x0_ref[0] = _conv3(p0_ref[0], w0_ref, b0_ref).astype(x0_ref.dtype)
    z_ref[0] = _conv3(p1_ref[0], w1_ref, b1_ref) * _conv3(p2_ref[0], w2_ref, b2_ref)


def hyena_pre(p, conv_w, conv_b):
    B, N, _ = p.shape
    nb = HY_W // LANES
    conv_b = conv_b.reshape(1, 3 * HY_W)
    pspec = lambda g: pl.BlockSpec((1, N, LANES), lambda b, c: (b, 0, g * nb + c))
    wspec = lambda g: pl.BlockSpec((3, LANES), lambda b, c: (0, g * nb + c))
    bspec = lambda g: pl.BlockSpec((1, LANES), lambda b, c: (0, g * nb + c))
    ospec = pl.BlockSpec((1, N, LANES), lambda b, c: (b, 0, c))
    return pl.pallas_call(
        _hyena_pre_kernel,
        grid=(B, nb),
        in_specs=[pspec(0), pspec(1), pspec(2), wspec(0), wspec(1), wspec(2), bspec(0), bspec(1), bspec(2)],
        out_specs=[ospec, ospec],
        out_shape=[jax.ShapeDtypeStruct((B, N, HY_W), f32), jax.ShapeDtypeStruct((B, N, HY_W), bf16)],
        compiler_params=_cparams(("parallel", "parallel")),
        name="hyena_pre",
    )(p, p, p, conv_w, conv_w, conv_w, conv_b, conv_b, conv_b)


DFT_SLABS = 8


def _hl(a):
    a32 = jnp.asarray(a.astype(np.float32))
    hi = a32.astype(bf16)
    lo = (a32 - hi.astype(f32)).astype(bf16)
    return jnp.concatenate([hi, lo], axis=-2)


def _dot3(a_hl, m, x):
    xh, xl = _split(x)
    r = jnp.dot(a_hl, xh, preferred_element_type=f32)
    return r[:m] + r[m:] + jnp.dot(a_hl[:m], xl, preferred_element_type=f32)


def _dft_consts(n):
    N = 2 * n
    na = N // LANES
    t1n = na // 2
    k1n = na // 2 + 1
    k1p = -(-k1n // 8) * 8
    k1 = np.arange(k1n)
    t1 = np.arange(t1n)
    th = 2 * np.pi * ((t1[None, :] * k1[:, None]) % na) / na
    f1c = np.zeros((2 * k1p, t1n))
    f1c[:k1n] = np.cos(th)
    f1c[k1p:k1p + k1n] = -np.sin(th)
    thf = 2 * np.pi * ((np.arange(na)[None, :] * k1[:, None]) % na) / na
    f1f = np.zeros((2 * k1p, na))
    f1f[:k1n] = np.cos(thf)
    f1f[k1p:k1p + k1n] = -np.sin(thf)
    k2 = np.arange(LANES)
    t2 = np.arange(LANES)
    m = (t2[None, None, :] * (k1[:, None, None] + na * k2[None, :, None])) % N
    ph = 2 * np.pi * m / N
    g = np.concatenate([np.cos(ph), -np.sin(ph)], axis=1)
    pht = ph.transpose(0, 2, 1)
    gi = np.concatenate([np.cos(pht), np.sin(pht)], axis=1)
    wk = np.where((k1 == 0) | (k1 == na // 2), 1.0, 2.0) / N
    k1h = na // 2
    f1i = np.concatenate([np.cos(th.T)[:, :k1h] * wk[None, :k1h], -np.sin(th.T)[:, :k1h] * wk[None, :k1h]], axis=1)
    k1e = -(-k1n // DFT_SLABS) * DFT_SLABS
    g = np.concatenate([g, np.zeros((k1e - k1n,) + g.shape[1:])], axis=0)
    gi = np.concatenate([gi, np.zeros((k1e - k1n,) + gi.shape[1:])], axis=0)
    return dict(na=na, t1n=t1n, k1n=k1n, k1e=k1e, k1p=k1p, f1c=_hl(f1c), f1f=_hl(f1f), g=_hl(g), gi=_hl(gi),
                f1i=_hl(f1i))


def _dft_stage1(src_ref, f1c_ref, tre_ref, tim_ref, t1n, k1p):
    f1c = f1c_ref[...]

    def body(t2, carry):
        zs = src_ref[pl.ds(t2, t1n, stride=LANES), :]
        r = _dot3(f1c, 2 * k1p, zs)
        r0 = pl.multiple_of(t2 * k1p, 8)
        tre_ref[pl.ds(r0, k1p), :] = r[:k1p]
        tim_ref[pl.ds(r0, k1p), :] = r[k1p:]
        return carry

    lax.fori_loop(0, LANES, body, 0, unroll=4)


def _slab(tre_ref, tim_ref, k1, k1p):
    return tre_ref[pl.ds(k1, LANES, stride=k1p), :], tim_ref[pl.ds(k1, LANES, stride=k1p), :]


def _cplx_left(gc_hl, xre, xim):
    cw = xre.shape[1]
    r = _dot3(gc_hl, 2 * LANES, jnp.concatenate([xre, xim], axis=1))
    p, q = r[:, :cw], r[:, cw:]
    return p[:LANES] - q[LANES:], p[LANES:] + q[:LANES]


def _spectrum_kernel(k_ref, f1f_ref, g_ref, inv_ref, xre_ref, xim_ref, are, aim, *, na, k1p):
    j = pl.program_id(1)

    @pl.when(j == 0)
    def _():
        _dft_stage1(k_ref, f1f_ref, are, aim, na, k1p)

    for half in range(DFT_SLABS):
        xre, xim = _cplx_left(g_ref[half], *_slab(are, aim, DFT_SLABS * j + half, k1p))
        rows = slice(half * LANES, (half + 1) * LANES)
        xre_ref[rows, :] = xre * inv_ref[...]
        xim_ref[rows, :] = xim * inv_ref[...]


def hyena_spectrum(k, inv_norm, dc):
    n2, C = k.shape
    k1e, k1p, na = dc["k1e"], dc["k1p"], dc["na"]
    cw = LANES
    out = jax.ShapeDtypeStruct((k1e * LANES, C), f32)
    ospec = pl.BlockSpec((DFT_SLABS * LANES, cw), lambda c, k: (k, c))
    return pl.pallas_call(
        functools.partial(_spectrum_kernel, na=na, k1p=k1p),
        grid=(C // cw, k1e // DFT_SLABS),
        in_specs=[pl.BlockSpec((n2, cw), lambda c, k: (0, c)),
                  pl.BlockSpec(dc["f1f"].shape, lambda c, k: (0, 0)),
                  pl.BlockSpec((DFT_SLABS, 4 * LANES, LANES), lambda c, k: (k, 0, 0)),
                  pl.BlockSpec((1, cw), lambda c, k: (0, c))],
        out_specs=[ospec, ospec],
        out_shape=[out, out],
        scratch_shapes=[pltpu.VMEM((k1p * LANES, cw), f32), pltpu.VMEM((k1p * LANES, cw), f32)],
        compiler_params=_cparams(("parallel", "arbitrary")),
        name="hyena_spectrum",
    )(k, dc["f1f"], dc["g"], inv_norm)


def _hyena_conv_kernel(z_ref, x0_ref, f1c_ref, g_ref, gi_ref, f1i_ref, kre_ref, kim_ref,
                       skip_ref, o_ref, tre, tim, are, aim, y_scr, *, t1n, k1p):
    j = pl.program_id(2)

    @pl.when(j == 0)
    def _():
        _dft_stage1(z_ref.at[0], f1c_ref, tre, tim, t1n, k1p)

    fwd = [_cplx_left(g_ref[half], *_slab(tre, tim, DFT_SLABS * j + half, k1p)) for half in range(DFT_SLABS)]
    prod = []
    for half, (xre, xim) in enumerate(fwd):
        rows = slice(half * LANES, (half + 1) * LANES)
        kre = kre_ref[rows, :]
        kim = kim_ref[rows, :]
        prod.append((xre * kre - xim * kim, xre * kim + xim * kre))
    inv = [_cplx_left(gi_ref[half], yre, yim) for half, (yre, yim) in enumerate(prod)]
    for half, (bre, bim) in enumerate(inv):
        r0 = pl.multiple_of((DFT_SLABS * j + half) * LANES, LANES)
        are[pl.ds(r0, LANES), :] = bre
        aim[pl.ds(r0, LANES), :] = bim

    @pl.when(j == pl.num_programs(2) - 1)
    def _():
        f1i = f1i_ref[...]
        k1h = t1n

        def body(t2, carry):
            bb = jnp.concatenate([are[pl.ds(t2, k1h, stride=LANES), :], aim[pl.ds(t2, k1h, stride=LANES), :]], axis=0)
            y_scr[pl.ds(t2, t1n, stride=LANES), :] = _dot3(f1i, t1n, bb)
            return carry

        lax.fori_loop(0, LANES, body, 0, unroll=4)
        r0 = k1h * LANES
        nyq = are[r0:r0 + LANES, :] * (1.0 / (2 * t1n * LANES))
        nyq = jnp.concatenate([nyq, -nyq] * (t1n // 2), axis=0)
        z = z_ref[0]
        o_ref[0] = (x0_ref[0].astype(f32) * (y_scr[...] + nyq + z * skip_ref[...])).astype(o_ref.dtype)


def hyena_conv(z, x0, spec_re, spec_im, skip, dc):
    B, n, W = z.shape
    nb = W // LANES
    k1e, k1p, t1n = dc["k1e"], dc["k1p"], dc["t1n"]
    seq = pl.BlockSpec((1, n, LANES), lambda b, c, k: (b, 0, c))
    fspec = pl.BlockSpec((DFT_SLABS * LANES, LANES), lambda b, c, k: (k, c))
    cspec = pl.BlockSpec((DFT_SLABS, 4 * LANES, LANES), lambda b, c, k: (k, 0, 0))
    vspec = pl.BlockSpec((1, LANES), lambda b, c, k: (0, c))
    return pl.pallas_call(
        functools.partial(_hyena_conv_kernel, t1n=t1n, k1p=k1p),
        grid=(B, nb, k1e // DFT_SLABS),
        in_specs=[seq, seq,
                  pl.BlockSpec(dc["f1c"].shape, lambda b, c, k: (0, 0)), cspec, cspec,
                  pl.BlockSpec(dc["f1i"].shape, lambda b, c, k: (0, 0)),
                  fspec, fspec, vspec],
        out_specs=seq,
        out_shape=jax.ShapeDtypeStruct((B, n, W), bf16),
        scratch_shapes=[pltpu.VMEM((k1p * LANES, LANES), f32)] * 4 + [pltpu.VMEM((n, LANES), f32)],
        compiler_params=_cparams(("parallel", "parallel", "arbitrary")),
        name="hyena_conv",
    )(z, x0, dc["f1c"], dc["g"], dc["gi"], dc["f1i"], spec_re, spec_im, skip)


def _hyena_ctx_kernel(p0_ref, p1_ref, p2_ref, w0_ref, w1_ref, w2_ref, b0_ref, b1_ref, b2_ref,
                      k_ref, inv_ref, skip_ref, fd_ref, fi_ref, o_ref, *, n):
    x0 = _conv3(p0_ref[0], w0_ref, b0_ref)
    z = _conv3(p1_ref[0], w1_ref, b1_ref) * _conv3(p2_ref[0], w2_ref, b2_ref)
    fd = fd_ref[...]
    N = 2 * n
    zf = jnp.dot(fd[:, :n], z, precision=HI, preferred_element_type=f32)
    kf = jnp.dot(fd, k_ref[...], precision=HI, preferred_element_type=f32) * inv_ref[...]
    yre = zf[:N] * kf[:N] - zf[N:] * kf[N:]
    yim = zf[:N] * kf[N:] + zf[N:] * kf[:N]
    y = jnp.dot(fi_ref[...], jnp.concatenate([yre, yim], axis=0), precision=HI, preferred_element_type=f32)
    o_ref[0] = (x0 * (y + z * skip_ref[...])).astype(o_ref.dtype)


def hyena_ctx(p, conv_w, conv_b, k, inv_norm, skip):
    B, n, _ = p.shape
    N = 2 * n
    nb = HY_W // LANES
    kk = np.arange(N)
    ph = 2 * np.pi * ((kk[:, None] * kk[None, :]) % N) / N
    fd = jnp.asarray(np.concatenate([np.cos(ph), -np.sin(ph)], axis=0).astype(np.float32))
    fi = jnp.asarray((np.concatenate([np.cos(ph[:n]), -np.sin(ph[:n])], axis=1) / N).astype(np.float32))
    conv_b = conv_b.reshape(1, 3 * HY_W)
    pspec = lambda g: pl.BlockSpec((1, n, LANES), lambda b, c: (b, 0, g * nb + c))
    wspec = lambda g: pl.BlockSpec((3, LANES), lambda b, c: (0, g * nb + c))
    bspec = lambda g: pl.BlockSpec((1, LANES), lambda b, c: (0, g * nb + c))
    vspec = pl.BlockSpec((1, LANES), lambda b, c: (0, c))
    return pl.pallas_call(
        functools.partial(_hyena_ctx_kernel, n=n),
        grid=(B, nb),
        in_specs=[pspec(0), pspec(1), pspec(2), wspec(0), wspec(1), wspec(2), bspec(0), bspec(1), bspec(2),
                  pl.BlockSpec((N, LANES), lambda b, c: (0, c)),
                  vspec, vspec,
                  pl.BlockSpec(fd.shape, lambda b, c: (0, 0)), pl.BlockSpec(fi.shape, lambda b, c: (0, 0))],
        out_specs=pl.BlockSpec((1, n, LANES), lambda b, c: (b, 0, c)),
        out_shape=jax.ShapeDtypeStruct((B, n, HY_W), bf16),
        compiler_params=_cparams(("parallel", "parallel")),
        name="hyena_ctx",
    )(p, p, p, conv_w, conv_w, conv_w, conv_b, conv_b, conv_b, k, inv_norm, skip, fd, fi)


def _merge_kernel(of_ref, ob_ref, gs_ref, nb_ref, hc_ref, g_ref, wa_ref, wb_ref, wc_ref, wo_ref, x_ref, m_ref, o_ref):
    d = D_MODEL
    tot = of_ref[...] + ob_ref[...]
    gs = gs_ref[...].astype(f32)
    ra = []
    for h in range(HG_HEADS):
        sl = slice(LANES * h, LANES * (h + 1))
        th = tot[:, sl]
        ms = jnp.mean(th * th, axis=-1, keepdims=True)
        ra.append(th * lax.rsqrt(ms + EPS) * gs[:, sl])
    ya = jnp.dot(jnp.concatenate(ra, axis=1).astype(bf16), wa_ref[...], preferred_element_type=f32)
    yb = jnp.dot(nb_ref[...], wb_ref[...], preferred_element_type=f32)
    yc = jnp.dot(hc_ref[...], wc_ref[...], preferred_element_type=f32)
    g = g_ref[...].astype(f32)
    mix = g[:, :d] * ya + g[:, d:2 * d] * yb + g[:, 2 * d:] * yc
    y = jnp.dot(mix.astype(bf16), wo_ref[...], preferred_element_type=f32)
    o_ref[...] = x_ref[...] + m_ref[0] * y


def merge(o_f, o_b, gs, nb, hc, gates, wa, wb, wc, wo, x2d, m, rows_per_group):
    R = x2d.shape[0]
    tm = 512
    tpg = rows_per_group // tm
    G = m.shape[0]
    row = lambda w: pl.BlockSpec((tm, w), lambda i: (i, 0))
    full = lambda a: pl.BlockSpec(a.shape, lambda i: (0, 0))
    return pl.pallas_call(
        _merge_kernel,
        grid=(R // tm,),
        in_specs=[row(HG_W), row(HG_W), row(HG_W), row(NA_W), row(HY_W), row(3 * D_MODEL),
                  full(wa), full(wb), full(wc), full(wo),
                  row(D_MODEL), pl.BlockSpec((1, 1, D_MODEL), lambda i: (i // tpg, 0, 0))],
        out_specs=row(D_MODEL),
        out_shape=jax.ShapeDtypeStruct((R, D_MODEL), f32),
        compiler_params=_cparams(("parallel",)),
        name="merge",
    )(o_f, o_b, gs, nb, hc, gates, wa, wb, wc, wo, x2d, m.reshape(G, 1, D_MODEL))


def _router_kernel(x_ref, g_ref, sh_ref, sc_ref, wrt_ref, wr_ref, h_ref, at_ref, am_ref):
    x = x_ref[...]
    ms = jnp.mean(x * x, axis=-1, keepdims=True)
    h = x * lax.rsqrt(ms + EPS) * g_ref[...] * (1.0 + sc_ref[0]) + sh_ref[0]
    h_ref[...] = h.astype(h_ref.dtype)
    hh, hl = _split(h)
    ne = N_EXPERTS
    wt = wrt_ref[...]
    rt = _nt(wt, hh)
    lt = rt[:ne] + rt[ne:] + _nt(wt[:ne], hl)
    et = jnp.exp(lt - jnp.max(lt, axis=0, keepdims=True))
    at_ref[0] = et / jnp.sum(et, axis=0, keepdims=True)
    wm = wr_ref[...]
    rm = jnp.dot(hh, wm, preferred_element_type=f32)
    lm = rm[:, :ne] + rm[:, ne:] + jnp.dot(hl, wm[:, :ne], preferred_element_type=f32)
    em = jnp.exp(lm - jnp.max(lm, axis=1, keepdims=True))
    am_ref[...] = em / jnp.sum(em, axis=1, keepdims=True)


def router(x2d, g, shift, scale, w_router, n_per_set):
    R = x2d.shape[0]
    tm = min(512, n_per_set)
    tps = n_per_set // tm
    S = R // n_per_set
    G = shift.shape[0]
    gmap = (lambda i: (i // tps, 0, 0)) if G > 1 else (lambda i: (0, 0, 0))
    whi, wlo = _split(w_router.astype(f32))
    wr = jnp.concatenate([whi, wlo], axis=1)
    return pl.pallas_call(
        _router_kernel,
        grid=(R // tm,),
        in_specs=[pl.BlockSpec((tm, D_MODEL), lambda i: (i, 0)),
                  pl.BlockSpec((1, D_MODEL), lambda i: (0, 0)),
                  pl.BlockSpec((1, 1, D_MODEL), gmap),
                  pl.BlockSpec((1, 1, D_MODEL), gmap),
                  pl.BlockSpec((2 * N_EXPERTS, D_MODEL), lambda i: (0, 0)),
                  pl.BlockSpec((D_MODEL, 2 * N_EXPERTS), lambda i: (0, 0))],
        out_specs=[pl.BlockSpec((tm, D_MODEL), lambda i: (i, 0)),
                   pl.BlockSpec((1, N_EXPERTS, tm), lambda i: (i // tps, 0, i % tps)),
                   pl.BlockSpec((tm, N_EXPERTS), lambda i: (i, 0))],
        out_shape=[jax.ShapeDtypeStruct((R, D_MODEL), bf16),
                   jax.ShapeDtypeStruct((S, N_EXPERTS, n_per_set), f32),
                   jax.ShapeDtypeStruct((R, N_EXPERTS), f32)],
        compiler_params=_cparams(("parallel",)),
        name="router",
    )(x2d, g.reshape(1, D_MODEL), shift.reshape(G, 1, D_MODEL), scale.reshape(G, 1, D_MODEL), wr.T, wr)


SEL_BLK = 256
SUB = LANES
SUBW = SUB + 8
UNSEL = -float(2 ** 30)


def _prefix_incl(mask_f, tri, T):
    outs = []
    off = jnp.zeros((mask_f.shape[0], 1), f32)
    for b in range(T // SEL_BLK):
        blk = mask_f[:, b * SEL_BLK:(b + 1) * SEL_BLK].astype(bf16)
        pre = jnp.dot(blk, tri, preferred_element_type=f32) + off
        outs.append(pre)
        off = pre[:, SEL_BLK - 1:SEL_BLK]
    return jnp.concatenate(outs, axis=1)


def _select_kernel(a_ref, tri_ref, cm_ref, posm_ref, cnt_ref, *, T, cap):
    aff = a_ref[0]
    bits = pltpu.bitcast(aff, i32)
    tri = tri_ref[...]

    def bit_step(i, thr):
        cand = thr | (1 << (30 - i))
        cnt = jnp.sum((bits >= cand).astype(f32), axis=1, keepdims=True)
        return jnp.where(cnt >= cap, cand, thr)

    thr = lax.fori_loop(0, 31, bit_step, jnp.zeros((N_EXPERTS, 1), i32))
    gt = bits > thr
    eq = bits == thr
    need = cap - jnp.sum(gt.astype(f32), axis=1, keepdims=True)
    eqf = eq.astype(f32)
    rank_eq = _prefix_incl(eqf, tri, T) - eqf
    sel = gt | (eq & (rank_eq < need))
    self_ = sel.astype(f32)
    pos = _prefix_incl(self_, tri, T) - self_
    posm_ref[0] = jnp.where(sel, pos, UNSEL)
    cnt_ref[0] = jnp.dot(self_.astype(bf16), cm_ref[...], preferred_element_type=f32).astype(i32)


def select_topk(aff, cap):
    S, E, T = aff.shape
    tri = jnp.asarray(np.triu(np.ones((SEL_BLK, SEL_BLK), np.float32)), bf16)
    cm = jnp.asarray((np.arange(T)[:, None] < np.arange(LANES)[None, :] * SUB).astype(np.float32), bf16)
    return pl.pallas_call(
        functools.partial(_select_kernel, T=T, cap=cap),
        grid=(S,),
        in_specs=[pl.BlockSpec((1, E, T), lambda s: (s, 0, 0)),
                  pl.BlockSpec((SEL_BLK, SEL_BLK), lambda s: (0, 0)),
                  pl.BlockSpec((T, LANES), lambda s: (0, 0))],
        out_specs=[pl.BlockSpec((1, E, T), lambda s: (s, 0, 0)),
                   pl.BlockSpec((1, E, LANES), lambda s: (s, 0, 0))],
        out_shape=[jax.ShapeDtypeStruct((S, E, T), f32), jax.ShapeDtypeStruct((S, E, LANES), i32)],
        compiler_params=_cparams(("parallel",)),
        name="select_topk",
    )(aff, tri, cm)


def _align8(v):
    return lax.shift_left(lax.shift_right_logical(v, 3), 3)


def _align16(v):
    return lax.shift_left(lax.shift_right_logical(v, 4), 4)


CMB_ROWS = SUB + 16


def _gather_kernel(cnt_ref, h_ref, pos_ref, o_ref, acc, *, TT, cap, tps, R, srows, EP):
    tl = pl.program_id(1)

    @pl.when(tl == 0)
    def _():
        acc[...] = jnp.zeros_like(acc)

    st = tl // tps
    nsub = TT // SUB
    rid = lax.broadcasted_iota(i32, (srows, SUB), 0).astype(f32)
    for ep in range(EP):
        e = pl.program_id(0) * EP + ep
        cbase = (st * N_EXPERTS + e) * LANES + (tl % tps) * nsub
        for s in range(nsub):
            off8 = _align8(cnt_ref[cbase + s])
            pos = pos_ref[0, ep, :, s * SUB:(s + 1) * SUB]
            onehot = jnp.where(pos == rid + off8.astype(f32), 1.0, 0.0).astype(bf16)
            rows = jnp.dot(onehot, h_ref[s * SUB:(s + 1) * SUB, :], preferred_element_type=f32)
            r0 = pl.multiple_of(st * cap + off8, 8)
            acc[ep, pl.ds(r0, srows), :] += rows

    @pl.when(tl == pl.num_programs(1) - 1)
    def _():
        o_ref[...] = acc[:, 0:R, :].astype(o_ref.dtype)


def _gather_call(cnt, h, posm, cap, TT, srows, EP):
    S, E, T = posm.shape
    tps = T // TT
    R = S * cap
    gs = pltpu.PrefetchScalarGridSpec(
        num_scalar_prefetch=1,
        grid=(E // EP, S * tps),
        in_specs=[pl.BlockSpec((TT, D_MODEL), lambda e, t, c: (t, 0)),
                  pl.BlockSpec((1, EP, 1, TT), lambda e, t, c: (t // tps, e, 0, t % tps))],
        out_specs=pl.BlockSpec((EP, R, D_MODEL), lambda e, t, c: (e, 0, 0)),
        scratch_shapes=[pltpu.VMEM((EP, R + srows, D_MODEL), f32)])
    return pl.pallas_call(
        functools.partial(_gather_kernel, TT=TT, cap=cap, tps=tps, R=R, srows=srows, EP=EP),
        grid_spec=gs,
        out_shape=jax.ShapeDtypeStruct((E, R, D_MODEL), bf16),
        compiler_params=_cparams(("parallel", "arbitrary")),
        name="moe_gather",
    )(cnt.reshape(-1), h, posm.reshape(S, E, 1, T))


FAST_SUB_MAX = 48
FAST_TILE_MAX = 192
FAST_ROWS = 64
CMB_TILE = 1024


def _fits_fast(cnt, T):
    nsub = T // SUB
    per_sub = cnt[..., 1:nsub + 1] - cnt[..., :nsub]
    k = CMB_TILE // SUB
    per_tile = cnt[..., k:nsub + 1:k] - cnt[..., 0:nsub:k]
    return (jnp.max(per_sub) <= FAST_SUB_MAX) & (jnp.max(per_tile) <= FAST_TILE_MAX)


def gather_rows(cnt, h, posm, cap, TT, fast_ok=None):
    safe = lambda: _gather_call(cnt, h, posm, cap, TT, SUBW, 1)
    if fast_ok is None:
        return safe()
    return lax.cond(fast_ok, lambda: _gather_call(cnt, h, posm, cap, TT, FAST_ROWS, 2), safe)


EXPERT_TF = 256


def _ffn_kernel(*refs, n):
    xs = refs[:n]
    wg_ref, wu_ref, wd_ref = refs[n:n + 3]
    his = refs[n + 3:2 * n + 3]
    los = refs[2 * n + 3:3 * n + 3]
    accs = refs[3 * n + 3:]
    j = pl.program_id(1)
    wg = wg_ref[0, 0].astype(bf16)
    wu = wu_ref[0, 0].astype(bf16)
    wd = wd_ref[0, 0].astype(bf16)
    @pl.when(j == 0)
    def _():
        for acc in accs:
            acc[...] = jnp.zeros_like(acc)

    for x_ref, hi_ref, lo_ref, acc in zip(xs, his, los, accs):
        x = x_ref[0]
        a = jnp.dot(x, wg, preferred_element_type=f32)
        u = jnp.dot(x, wu, preferred_element_type=f32)
        acc[...] += jnp.dot((_silu(a) * u).astype(bf16), wd, preferred_element_type=f32)

        @pl.when(j == pl.num_programs(1) - 1)
        def _(acc=acc, hi_ref=hi_ref, lo_ref=lo_ref):
            hi, lo = _split(acc[...])
            hi_ref[0] = hi
            lo_ref[0] = lo


def expert_ffn(xgs, layer, w_gate, w_up, w_down):
    E = xgs[0].shape[0]
    nf = D_FF_EXPERT // EXPERT_TF
    n = len(xgs)
    rowspec = lambda a: pl.BlockSpec((1, a.shape[1], D_MODEL), lambda e, j: (e, 0, 0))
    res = pl.pallas_call(
        functools.partial(_ffn_kernel, n=n),
        grid=(E, nf),
        in_specs=[rowspec(a) for a in xgs] + [
            pl.BlockSpec((1, 1, D_MODEL, EXPERT_TF), lambda e, j: (layer, e, 0, j)),
            pl.BlockSpec((1, 1, D_MODEL, EXPERT_TF), lambda e, j: (layer, e, 0, j)),
            pl.BlockSpec((1, 1, EXPERT_TF, D_MODEL), lambda e, j: (layer, e, j, 0))],
        out_specs=[rowspec(a) for a in xgs] * 2,
        out_shape=[jax.ShapeDtypeStruct(a.shape, bf16) for a in xgs] * 2,
        scratch_shapes=[pltpu.VMEM(a.shape[1:], f32) for a in xgs],
        compiler_params=_cparams(("parallel", "arbitrary")),
        name="expert_ffn",
    )(*xgs, w_gate, w_up, w_down)
    return [(res[i], res[n + i]) for i in range(n)]


def _combine_kernel(cnt_ref, x_ref, pos_ref, am_ref, m_ref, *rest, TT, cap, R, W, ytot, crows, EP):
    y_refs, o_ref = rest[:2 * EP], rest[2 * EP]
    st = pl.program_id(0)
    tl = pl.program_id(1)
    eg = pl.program_id(2)

    @pl.when(eg == 0)
    def _():
        o_ref[...] = x_ref[...]

    nsub = TT // SUB
    lane = lax.broadcasted_iota(i32, (TT, N_EXPERTS), 1)
    m5 = m_ref[0]
    rid = lax.broadcasted_iota(i32, (crows, SUB), 0).astype(f32)
    am = am_ref[...]
    gcols, wss = [], []
    for ep in range(EP):
        e = eg * EP + ep
        cbase = (st * N_EXPERTS + e) * LANES + tl * nsub
        wss.append(jnp.minimum(e * R + st * cap + _align16(cnt_ref[cbase]), ytot - W))
        gcols.append(jnp.sum(jnp.where(lane == e, am, 0.0), axis=1, keepdims=True))
    for s in range(nsub):
        sl = slice(s * SUB, (s + 1) * SUB)
        tot = None
        for ep in range(EP):
            e = eg * EP + ep
            rowbase = e * R + st * cap
            off = _align16(cnt_ref[(st * N_EXPERTS + e) * LANES + tl * nsub + s])
            rel = pl.multiple_of(jnp.minimum(rowbase + off - wss[ep], W - crows), 16)
            first = (wss[ep] + rel - rowbase).astype(f32)
            pos = pos_ref[0, ep, :, sl]
            onehot = jnp.where(pos == rid + first, 1.0, 0.0).astype(bf16)
            ywin = jnp.concatenate([y_refs[2 * ep][pl.ds(rel, crows), :], y_refs[2 * ep + 1][pl.ds(rel, crows), :]],
                                   axis=0)
            picked = _tn(jnp.concatenate([onehot, onehot], axis=0), ywin)
            term = gcols[ep][sl] * picked
            tot = term if tot is None else tot + term
        o_ref[sl, :] += m5 * tot


def _combine_call(cnt, x2d, posm, aff_tm, mvec, y_hl, cap, TT, crows, W, EP):
    S, E, T = posm.shape
    tps = T // TT
    R = S * cap
    ytot = E * R
    nsub = TT // SUB
    G = mvec.shape[0]

    def yspec(ep):
        def ymap(st, tl, eg, c):
            e = eg * EP + ep
            off = _align16(c[(st * E + e) * LANES + tl * nsub])
            return (pl.multiple_of(jnp.minimum(e * R + st * cap + off, ytot - W), 16), 0)
        return pl.BlockSpec((pl.Element(W), pl.Element(D_MODEL)), ymap)

    tok = lambda w: pl.BlockSpec((TT, w), lambda st, tl, e, c: (st * tps + tl, 0))
    mmap = (lambda st, tl, e, c: (st, 0, 0)) if G > 1 else (lambda st, tl, e, c: (0, 0, 0))
    gs = pltpu.PrefetchScalarGridSpec(
        num_scalar_prefetch=1,
        grid=(S, tps, E // EP),
        in_specs=[tok(D_MODEL),
                  pl.BlockSpec((1, EP, 1, TT), lambda st, tl, e, c: (st, e, 0, tl)),
                  tok(N_EXPERTS),
                  pl.BlockSpec((1, 1, D_MODEL), mmap)] + [yspec(ep) for ep in range(EP) for _ in range(2)],
        out_specs=tok(D_MODEL))
    yh, yl = y_hl[0].reshape(ytot, D_MODEL), y_hl[1].reshape(ytot, D_MODEL)
    return pl.pallas_call(
        functools.partial(_combine_kernel, TT=TT, cap=cap, R=R, W=W, ytot=ytot, crows=crows, EP=EP),
        grid_spec=gs,
        out_shape=jax.ShapeDtypeStruct(x2d.shape, f32),
        compiler_params=_cparams(("parallel", "parallel", "arbitrary")),
        name="moe_combine",
    )(cnt.reshape(-1), x2d, posm.reshape(S, E, 1, T), aff_tm, mvec.reshape(G, 1, D_MODEL), *([yh, yl] * EP))


def combine(cnt, x2d, posm, aff_tm, mvec, y_hl, cap, TT, fast_ok=None):
    safe = lambda: _combine_call(cnt, x2d, posm, aff_tm, mvec, y_hl, cap, TT, CMB_ROWS, TT + 32, 1)
    if fast_ok is None:
        return safe()
    wfast = FAST_TILE_MAX + 16 + FAST_ROWS
    return lax.cond(fast_ok, lambda: _combine_call(cnt, x2d, posm, aff_tm, mvec, y_hl, cap, TT, FAST_ROWS, wfast, 8),
                    safe)


def _rope_tables(n):
    half = NA_HD // 2
    q = half // 2
    inv = ROPE_THETA ** (-np.arange(q, dtype=np.float64) / q)
    pos = np.arange(n)
    ang_r = (pos // GRID_W)[:, None] * inv
    ang_c = (pos % GRID_W)[:, None] * inv
    zero = np.zeros_like(ang_r)
    c = np.concatenate([np.cos(ang_r)] * 2 + [np.cos(ang_c)] * 2, axis=1)
    s1 = np.concatenate([-np.sin(ang_r), zero, -np.sin(ang_c), zero], axis=1)
    s2 = np.concatenate([zero, np.sin(ang_r), zero, np.sin(ang_c)], axis=1)
    two = lambda a: jnp.asarray(np.concatenate([a, a], axis=1).astype(np.float32))
    return two(c), two(s1), two(s2)


def _mixing(hx, hc, need_ctx, B, N, NC, la, lc, w_in, q_gain, k_gain, table, rope, bd, conv_w, conv_b,
            spec, skip, filt_c, wa, wb, wc, wo, x2d, c2d, mx2, mc2):
    tile8 = lambda v: jnp.tile(v.reshape(1, NA_HD), (1, NA_HEADS))
    qg, kg = tile8(q_gain), tile8(k_gain)
    norm_aux = [("col", kg), ("const", bd)]
    rope_aux = [("row", rope[0]), ("row", rope[1]), ("row", rope[2])]
    lf_aux = lambda d: [("col", la[d:d + 1]), ("col", lc[d:d + 1])]
    tc = hc.shape[0]

    lff_c = project(hc, w_in, OFF_FF, HG_W, _epi_logforget, lf_aux(0), (f32,), tm=tc).reshape(B, NC, -1)
    lfb_c = project(hc, w_in, OFF_FB, HG_W, _epi_logforget, lf_aux(1), (f32,), tm=tc).reshape(B, NC, -1)
    i_c = project(hc, w_in, OFF_I, HG_W, _epi_raw, tm=tc).reshape(B, NC, -1)
    k_c = project(hc, w_in, OFF_NK, NA_W, _epi_norm, norm_aux, tm=tc).reshape(B, NC, -1)
    v_c = project(hc, w_in, OFF_NV, NA_W, _epi_raw, tm=tc).reshape(B, NC, -1)
    if need_ctx:
        q_c = project(hc, w_in, OFF_HQ, HG_W, _epi_silu, tm=tc).reshape(B, NC, -1)
    else:
        q_c = jnp.zeros((B, NC, HG_W), bf16)
    s0 = jnp.zeros((B, HG_HEADS, LANES, LANES), f32)
    oc_f, oc_b, s_f, s_b = hgrn_bidir(lff_c, lfb_c, i_c, q_c, s0, s0)

    lff_x = project(hx, w_in, OFF_FF, HG_W, _epi_logforget, lf_aux(0), (f32,)).reshape(B, N, -1)
    lfb_x = project(hx, w_in, OFF_FB, HG_W, _epi_logforget, lf_aux(1), (f32,)).reshape(B, N, -1)
    i_x = project(hx, w_in, OFF_I, HG_W, _epi_raw).reshape(B, N, -1)
    q_x = project(hx, w_in, OFF_HQ, HG_W, _epi_silu).reshape(B, N, -1)
    g_x = project(hx, w_in, OFF_HG, HG_W, _epi_silu).reshape(B, N, -1)
    k_x = project(hx, w_in, OFF_NK, NA_W, _epi_norm_rope, norm_aux + rope_aux, rows_per_seq=N).reshape(B, N, -1)
    v_x = project(hx, w_in, OFF_NV, NA_W, _epi_raw).reshape(B, N, -1)
    qn_x, qr_x = project(hx, w_in, OFF_NQ, NA_W, _epi_norm_both, [("col", qg), ("const", bd)] + rope_aux,
                         (bf16, bf16), rows_per_seq=N)
    p_x = project(hx, w_in, OFF_HY, 3 * HY_W, _epi_raw, out_dtypes=(f32,)).reshape(B, N, 3 * HY_W)
    gates_x = project(hx, w_in, OFF_GATE, 3 * D_MODEL, _epi_sigmoid)

    ox_f, ox_b, _, _ = hgrn_bidir(lff_x, lfb_x, i_x, q_x, s_f, s_b)

    nb_x = neighbourhood_attention(qr_x.reshape(B, N, -1), qn_x.reshape(B, N, -1), k_x, v_x, k_c, v_c, table)

    z_x, x0_x = hyena_pre(p_x, conv_w, conv_b)
    hy_x = hyena_conv(z_x, x0_x, spec[0], spec[1], skip, spec[2])

    flat = lambda a: a.reshape(-1, a.shape[-1])
    x_new = merge(flat(ox_f), flat(ox_b), flat(g_x), flat(nb_x), flat(hy_x), gates_x, wa, wb, wc, wo, x2d, mx2, N)
    if not need_ctx:
        return x_new, None

    qn_c = project(hc, w_in, OFF_NQ, NA_W, _epi_norm, [("col", qg), ("const", bd)], tm=tc).reshape(B, NC, -1)
    nb_c = context_attention(qn_c, k_c, v_c)
    p_c = project(hc, w_in, OFF_HY, 3 * HY_W, _epi_raw, out_dtypes=(f32,), tm=tc).reshape(B, NC, 3 * HY_W)
    hy_c = hyena_ctx(p_c, conv_w, conv_b, filt_c[0], filt_c[1], skip)
    gates_c = project(hc, w_in, OFF_GATE, 3 * D_MODEL, _epi_sigmoid, tm=tc)
    g_c = project(hc, w_in, OFF_HG, HG_W, _epi_silu, tm=tc)
    c_new = merge(flat(oc_f), flat(oc_b), g_c, flat(nb_c), flat(hy_c), gates_c, wa, wb, wc, wo, c2d, mc2, B * NC)
    return x_new, c_new


def kernel(x, c, ctx, c_ctx, w_mod, b_mod, norm_mix, norm_ffn, w_in, hg_lb, na_q_gain, na_k_gain, na_rpb,
           hy_conv_w, hy_conv_b, hy_pe_w1, hy_pe_b1, hy_pe_freq1, hy_pe_w2, hy_pe_b2, hy_pe_freq2, hy_pe_w3,
           hy_skip, w_branch_a, w_branch_b, w_branch_c, w_out, w_router, w_e_gate, w_e_up, w_e_down):
    B, N, D = x.shape
    NC = ctx.shape[1]
    E = N_EXPERTS
    cap_x = EC_CAP_FACTOR * N // E
    cap_c = EC_CAP_FACTOR * NC // E

    lb = jnp.cumsum(jax.nn.softmax(hg_lb.astype(f32), axis=0), axis=0)
    lb = lb - lb[:1]
    la_all, lc_all = jnp.log(lb), jnp.log1p(-lb)

    s8 = jnp.zeros((8, D), f32).at[:B].set(c).at[B].set(c_ctx)
    rope = _rope_tables(N)
    bd = jnp.asarray(np.kron(np.eye(NA_HEADS), np.full((NA_HD, NA_HD), 1.0 / NA_HD)).astype(np.float32), bf16)
    dcx = _dft_consts(N)

    x2d = x.reshape(B * N, D)
    c2d = ctx.reshape(B * NC, D)
    for l in range(DEPTH):
        need_ctx = l < DEPTH - 1
        mv = modvec(s8, w_mod[l], b_mod[l])
        mx = [mv[:B, k * D:(k + 1) * D] for k in range(6)]
        mc = [mv[B:B + 1, k * D:(k + 1) * D] for k in range(6)]
        w_in_l = w_in[l].astype(bf16)
        hx = modulate(x2d, norm_mix[l], mx[0], mx[1], N, bf16)
        hc = modulate(c2d, norm_mix[l], mc[0], mc[1], B * NC, bf16)

        filt = (hy_pe_w1[l], hy_pe_b1[l], hy_pe_freq1[l], hy_pe_w2[l], hy_pe_b2[l], hy_pe_freq2[l], hy_pe_w3[l])
        k_x, nrm_x = hyena_filter(N, *filt)
        sre, sim = hyena_spectrum(k_x, 1.0 / nrm_x, dcx)
        skip = hy_skip[l].reshape(1, HY_W)
        filt_c = None
        if need_ctx:
            h_c, nrm_c = hyena_filter(NC, *filt)
            filt_c = (h_c, 1.0 / nrm_c)

        x2d, c_new = _mixing(
            hx, hc, need_ctx, B, N, NC, la_all[l], lc_all[l], w_in_l, na_q_gain[l], na_k_gain[l],
            _na_bias_table(na_rpb[l]), rope, bd, hy_conv_w[l], hy_conv_b[l], (sre, sim, dcx), skip,
            filt_c, w_branch_a[l].astype(bf16), w_branch_b[l].astype(bf16), w_branch_c[l].astype(bf16),
            w_out[l].astype(bf16), x2d, c2d, mx[2], mc[2])

        h2, aff_t, aff_m = router(x2d, norm_ffn[l], mx[3], mx[4], w_router[l], N)
        posm, cnt = select_topk(aff_t, cap_x)
        fast_ok = _fits_fast(cnt, N)
        xgs = [gather_rows(cnt, h2, posm, cap_x, 2048, fast_ok)]
        if need_ctx:
            c2d = c_new
            hc2, aff_tc, aff_mc = router(c2d, norm_ffn[l], mc[3], mc[4], w_router[l], NC)
            posm_c, cnt_c = select_topk(aff_tc, cap_c)
            xgs.append(gather_rows(cnt_c, hc2, posm_c, cap_c, NC))
        ys = expert_ffn(xgs, l, w_e_gate, w_e_up, w_e_down)
        x2d = combine(cnt, x2d, posm, aff_m, mx[5], ys[0], cap_x, CMB_TILE, fast_ok)
        if need_ctx:
            c2d = combine(cnt_c, c2d, posm_c, aff_mc, mc[5], ys[1], cap_c, NC)
    return x2d.reshape(B, N, D)
```
